```python
import math
import jax, jax.numpy as jnp
from jax import lax
import numpy as np

D_MODEL = 2048
BATCH = 8
SEQ = 8192
DEPTH = 1

MIX_WIDTH = D_MODEL
ATTN_WIDTH = D_MODEL // 2
HEAD_DIM = 128
N_HEADS = ATTN_WIDTH // HEAD_DIM
DILATION_PATTERNS = ((128, 1), (512, 4), (2048, 16))
SSM_WIDTH = MIX_WIDTH - ATTN_WIDTH
SSM_GROUP = 16
N_SSM_GROUPS = SSM_WIDTH // SSM_GROUP
STATE_DIM = 64
SSM_CHUNK = 128
IN_WIDTH = 3 * ATTN_WIDTH + SSM_WIDTH
D_FF = 4 * D_MODEL
N_MOD = 6
EPS = 1e-6
DT_MIN = 1e-3
DT_MAX = 1e-1

kernel_name = "hymba_dilated_attn_s5_sqrelu_adaln"


def rms_norm(x, g):
    xf = x.astype(jnp.float32)
    y = xf * lax.rsqrt(jnp.mean(xf * xf, axis=-1, keepdims=True) + EPS) * g.astype(jnp.float32)
    return y.astype(x.dtype)


def alibi_slopes(n_heads):
    return 2.0 ** (-8.0 * (jnp.arange(n_heads, dtype=jnp.float32) + 1.0) / n_heads)


def dilated_pattern(q, k, v, slopes, window, dilation):
    b, s, h, e = q.shape
    n = window // dilation
    L = s // dilation
    nb = -(-L // n)
    Lp = nb * n

    def to_blocks(t):
        t = t.reshape(b, L, dilation, h, e)
        t = jnp.pad(t, ((0, 0), (0, Lp - L), (0, 0), (0, 0), (0, 0)))
        return t.reshape(b, nb, n, dilation, h, e)

    def with_prev(t):
        prev = jnp.pad(t, ((0, 0), (1, 0), (0, 0), (0, 0), (0, 0), (0, 0)))[:, :-1]
        return jnp.concatenate([prev, t], axis=2)

    qb = to_blocks(q)
    kw = with_prev(to_blocks(k))
    vw = with_prev(to_blocks(v))
    scores = jnp.einsum('bnqrhe,bnkrhe->bnrhqk', qb, kw) * (HEAD_DIM ** -0.5)

    qi = jnp.arange(n)[:, None]
    ki = jnp.arange(2 * n)[None, :]
    steps = qi - ki + n
    key_idx = jnp.arange(nb)[:, None, None] * n - n + ki
    valid = (steps >= 0) & (steps <= n) & (key_idx >= 0)
    bias = -slopes[:, None, None] * (steps * dilation).astype(jnp.float32)
    scores = scores + bias[None, None, None]
    scores = jnp.where(valid[None, :, None, None], scores, -jnp.inf)

    m = jnp.max(scores, axis=-1, keepdims=True)
    p = jnp.exp(scores - m)
    denom = jnp.sum(p, axis=-1, keepdims=True)
    out = jnp.einsum('bnrhqk,bnkrhe->bnqrhe', p, vw)
    denom_q = jnp.moveaxis(denom[..., 0], -1, 2)
    lse_q = jnp.moveaxis(m[..., 0], -1, 2) + jnp.log(denom_q)
    out = out / denom_q[..., None]
    out = out.reshape(b, Lp, dilation, h, e)[:, :L].reshape(b, s, h, e)
    lse = lse_q.reshape(b, Lp, dilation, h)[:, :L].reshape(b, s, h)
    return out, lse


def dilated_attention(q, k, v):
    slopes = alibi_slopes(N_HEADS)
    outs, lses = [], []
    for window, dilation in DILATION_PATTERNS:
        o, l = dilated_pattern(q, k, v, slopes, window, dilation)
        outs.append(o)
        lses.append(l)
    w = jax.nn.softmax(jnp.stack(lses, axis=0), axis=0)
    return jnp.sum(w[..., None] * jnp.stack(outs, axis=0), axis=0)


def _ssm_combine(e_i, e_j):
    a_i, b_i = e_i
    a_j, b_j = e_j
    return a_j * a_i, a_j * b_i + b_j


def s5_mixer(u, lam_re, lam_im, log_step, b_re, b_im, c_re, c_im, d_skip):
    bsz, s, _ = u.shape
    uf = u.astype(jnp.float32).reshape(bsz, s, N_SSM_GROUPS, SSM_GROUP)
    lam = lax.complex(lam_re.astype(jnp.float32), lam_im.astype(jnp.float32))
    step = jnp.exp(log_step.astype(jnp.float32))[:, None]
    a_bar = jnp.exp(lam * step)
    b_mat = lax.complex(b_re.astype(jnp.float32), b_im.astype(jnp.float32))
    b_bar = ((a_bar - 1.0) / lam)[..., None] * b_mat
    c_mat = lax.complex(c_re.astype(jnp.float32), c_im.astype(jnp.float32))
    n_chunks = s // SSM_CHUNK
    u_chunks = uf.reshape(bsz, n_chunks, SSM_CHUNK, N_SSM_GROUPS, SSM_GROUP).transpose(1, 0, 2, 3, 4)
    a_full = jnp.broadcast_to(a_bar, (bsz, SSM_CHUNK, N_SSM_GROUPS, STATE_DIM))

    def segment(h, u_c):
        bu = jnp.einsum('blgi,gpi->blgp', u_c.astype(jnp.complex64), b_bar)
        bu = bu.at[:, 0].add(a_bar * h)
        _, hs = lax.associative_scan(_ssm_combine, (a_full, bu), axis=1)
        y = jnp.real(jnp.einsum('blgp,gip->blgi', hs, c_mat))
        return hs[:, -1], y

    h0 = jnp.zeros((bsz, N_SSM_GROUPS, STATE_DIM), jnp.complex64)
    _, ys = lax.scan(segment, h0, u_chunks)
    y = ys.transpose(1, 0, 2, 3, 4).reshape(bsz, s, N_SSM_GROUPS, SSM_GROUP)
    y = y + d_skip.astype(jnp.float32).reshape(N_SSM_GROUPS, SSM_GROUP) * uf
    return y.reshape(bsz, s, SSM_WIDTH)


def hybrid_layer(x, c, w_ada, b_ada, norm1_g, w_in, q_norm_g, k_norm_g, lam_re, lam_im,
                 log_step, b_re, b_im, c_re, c_im, d_skip, w_glu, b_glu, attn_out_g,
                 ssm_out_g, w_out, norm2_g, w_ff1, w_ff2):
    bsz, s, _ = x.shape
    mod = (jax.nn.silu(c) @ w_ada + b_ada)[:, None, :]
    sh1, sc1, g1, sh2, sc2, g2 = jnp.split(mod, N_MOD, axis=-1)

    h = rms_norm(x, norm1_g) * (1.0 + sc1) + sh1
    proj = h @ w_in
    q, k, v, u = jnp.split(proj, [ATTN_WIDTH, 2 * ATTN_WIDTH, 3 * ATTN_WIDTH], axis=-1)
    q = rms_norm(q.reshape(bsz, s, N_HEADS, HEAD_DIM), q_norm_g).astype(jnp.float32)
    k = rms_norm(k.reshape(bsz, s, N_HEADS, HEAD_DIM), k_norm_g).astype(jnp.float32)
    v = v.reshape(bsz, s, N_HEADS, HEAD_DIM).astype(jnp.float32)
    attn = dilated_attention(q, k, v).reshape(bsz, s, ATTN_WIDTH).astype(x.dtype)

    y = jax.nn.gelu(s5_mixer(u, lam_re, lam_im, log_step, b_re, b_im, c_re, c_im, d_skip)).astype(x.dtype)
    ssm = y * jax.nn.sigmoid(y @ w_glu + b_glu)

    mixed = jnp.concatenate([rms_norm(attn, attn_out_g), rms_norm(ssm, ssm_out_g)], axis=-1) @ w_out
    x = x + g1 * mixed

    h2 = rms_norm(x, norm2_g) * (1.0 + sc2) + sh2
    ff = jnp.square(jax.nn.relu(h2 @ w_ff1)) @ w_ff2
    return x + g2 * ff


def _fwd_setup_inputs(seed: int = 0) -> dict:
    key = jax.random.key(seed)
    ks = jax.random.split(key, 26)
    f32 = jnp.float32
    nrm = lambda k, shape, scale: jax.random.normal(k, shape, f32) * scale
    G, P = N_SSM_GROUPS, STATE_DIM
    lam_im_base = jnp.pi * jnp.arange(P, dtype=f32)
    return {
        "x": nrm(ks[0], (BATCH, SEQ, D_MODEL), 1.0),
        "c": nrm(ks[1], (BATCH, D_MODEL), 1.0),
        "w_ada": nrm(ks[2], (DEPTH, D_MODEL, N_MOD * D_MODEL), 0.5 * D_MODEL ** -0.5),
        "b_ada": nrm(ks[3], (DEPTH, N_MOD * D_MODEL), 0.01),
        "norm1_g": 1.0 + nrm(ks[4], (DEPTH, D_MODEL), 0.01),
        "w_in": nrm(ks[5], (DEPTH, D_MODEL, IN_WIDTH), D_MODEL ** -0.5),
        "q_norm_g": 1.0 + nrm(ks[6], (DEPTH, HEAD_DIM), 0.01),
        "k_norm_g": 1.0 + nrm(ks[7], (DEPTH, HEAD_DIM), 0.01),
        "lam_re": -0.5 + nrm(ks[8], (DEPTH, G, P), 0.01),
        "lam_im": lam_im_base + nrm(ks[9], (DEPTH, G, P), 0.01),
        "log_step": jax.random.uniform(ks[10], (DEPTH, G), f32, math.log(DT_MIN), math.log(DT_MAX)),
        "b_re": nrm(ks[11], (DEPTH, G, P, SSM_GROUP), (2.0 * SSM_GROUP) ** -0.5),
        "b_im": nrm(ks[12], (DEPTH, G, P, SSM_GROUP), (2.0 * SSM_GROUP) ** -0.5),
        "c_re": nrm(ks[13], (DEPTH, G, SSM_GROUP, P), (2.0 * P) ** -0.5),
        "c_im": nrm(ks[14], (DEPTH, G, SSM_GROUP, P), (2.0 * P) ** -0.5),
        "d_skip": nrm(ks[15], (DEPTH, SSM_WIDTH), 1.0),
        "w_glu": nrm(ks[16], (DEPTH, SSM_WIDTH, SSM_WIDTH), SSM_WIDTH ** -0.5),
        "b_glu": nrm(ks[17], (DEPTH, SSM_WIDTH), 0.01),
        "attn_out_g": 1.0 + nrm(ks[18], (DEPTH, ATTN_WIDTH), 0.01),
        "ssm_out_g": 1.0 + nrm(ks[19], (DEPTH, SSM_WIDTH), 0.01),
        "w_out": nrm(ks[20], (DEPTH, MIX_WIDTH, D_MODEL), MIX_WIDTH ** -0.5),
        "norm2_g": 1.0 + nrm(ks[21], (DEPTH, D_MODEL), 0.01),
        "w_ff1": nrm(ks[22], (DEPTH, D_MODEL, D_FF), D_MODEL ** -0.5),
        "w_ff2": nrm(ks[23], (DEPTH, D_FF, D_MODEL), D_FF ** -0.5),
    }


def _fwd_reference(x, c, w_ada, b_ada, norm1_g, w_in, q_norm_g, k_norm_g, lam_re, lam_im, log_step,
              b_re, b_im, c_re, c_im, d_skip, w_glu, b_glu, attn_out_g, ssm_out_g, w_out,
              norm2_g, w_ff1, w_ff2):
    for l in range(DEPTH):
        x = hybrid_layer(x, c, w_ada[l], b_ada[l], norm1_g[l], w_in[l], q_norm_g[l], k_norm_g[l],
                         lam_re[l], lam_im[l], log_step[l], b_re[l], b_im[l], c_re[l], c_im[l],
                         d_skip[l], w_glu[l], b_glu[l], attn_out_g[l], ssm_out_g[l], w_out[l],
                         norm2_g[l], w_ff1[l], w_ff2[l])
    return x


import jax as _jax
import jax.numpy as _jnp

TWIN_FORMAT = 'train_step'
FWD_PARAMS = ['x', 'c', 'w_ada', 'b_ada', 'norm1_g', 'w_in', 'q_norm_g', 'k_norm_g', 'lam_re', 'lam_im', 'log_step', 'b_re', 'b_im', 'c_re', 'c_im', 'd_skip', 'w_glu', 'b_glu', 'attn_out_g', 'ssm_out_g', 'w_out', 'norm2_g', 'w_ff1', 'w_ff2']
TWIN_WEIGHTS = ['w_ada', 'b_ada', 'norm1_g', 'w_in', 'q_norm_g', 'k_norm_g', 'lam_re', 'lam_im', 'log_step', 'b_re', 'b_im', 'c_re', 'c_im', 'd_skip', 'w_glu', 'b_glu', 'attn_out_g', 'ssm_out_g', 'w_out', 'norm2_g', 'w_ff1', 'w_ff2']
TWIN_DIFF_INPUT = 'x'
TWIN_INPUTS = ['x', 'c', 'w_ada', 'b_ada', 'norm1_g', 'w_in', 'q_norm_g', 'k_norm_g', 'lam_re', 'lam_im', 'log_step', 'b_re', 'b_im', 'c_re', 'c_im', 'd_skip', 'w_glu', 'b_glu', 'attn_out_g', 'ssm_out_g', 'w_out', 'norm2_g', 'w_ff1', 'w_ff2', 'loss_target', 'm_w_ada', 'm_b_ada', 'm_norm1_g', 'm_w_in', 'm_q_norm_g', 'm_k_norm_g', 'm_lam_re', 'm_lam_im', 'm_log_step', 'm_b_re', 'm_b_im', 'm_c_re', 'm_c_im', 'm_d_skip', 'm_w_glu', 'm_b_glu', 'm_attn_out_g', 'm_ssm_out_g', 'm_w_out', 'm_norm2_g', 'm_w_ff1', 'm_w_ff2', 'v_w_ada', 'v_b_ada', 'v_norm1_g', 'v_w_in', 'v_q_norm_g', 'v_k_norm_g', 'v_lam_re', 'v_lam_im', 'v_log_step', 'v_b_re', 'v_b_im', 'v_c_re', 'v_c_im', 'v_d_skip', 'v_w_glu', 'v_b_glu', 'v_attn_out_g', 'v_ssm_out_g', 'v_w_out', 'v_norm2_g', 'v_w_ff1', 'v_w_ff2']
TWIN_OUTPUTS = ['loss', 'grad_x', 'grad_w_ada', 'grad_b_ada', 'grad_norm1_g', 'grad_w_in', 'grad_q_norm_g', 'grad_k_norm_g', 'grad_lam_re', 'grad_lam_im', 'grad_log_step', 'grad_b_re', 'grad_b_im', 'grad_c_re', 'grad_c_im', 'grad_d_skip', 'grad_w_glu', 'grad_b_glu', 'grad_attn_out_g', 'grad_ssm_out_g', 'grad_w_out', 'grad_norm2_g', 'grad_w_ff1', 'grad_w_ff2', 'delta_w_ada', 'delta_b_ada', 'delta_norm1_g', 'delta_w_in', 'delta_q_norm_g', 'delta_k_norm_g', 'delta_lam_re', 'delta_lam_im', 'delta_log_step', 'delta_b_re', 'delta_b_im', 'delta_c_re', 'delta_c_im', 'delta_d_skip', 'delta_w_glu', 'delta_b_glu', 'delta_attn_out_g', 'delta_ssm_out_g', 'delta_w_out', 'delta_norm2_g', 'delta_w_ff1', 'delta_w_ff2', 'new_m_w_ada', 'new_m_b_ada', 'new_m_norm1_g', 'new_m_w_in', 'new_m_q_norm_g', 'new_m_k_norm_g', 'new_m_lam_re', 'new_m_lam_im', 'new_m_log_step', 'new_m_b_re', 'new_m_b_im', 'new_m_c_re', 'new_m_c_im', 'new_m_d_skip', 'new_m_w_glu', 'new_m_b_glu', 'new_m_attn_out_g', 'new_m_ssm_out_g', 'new_m_w_out', 'new_m_norm2_g', 'new_m_w_ff1', 'new_m_w_ff2', 'new_v_w_ada', 'new_v_b_ada', 'new_v_norm1_g', 'new_v_w_in', 'new_v_q_norm_g', 'new_v_k_norm_g', 'new_v_lam_re', 'new_v_lam_im', 'new_v_log_step', 'new_v_b_re', 'new_v_b_im', 'new_v_c_re', 'new_v_c_im', 'new_v_d_skip', 'new_v_w_glu', 'new_v_b_glu', 'new_v_attn_out_g', 'new_v_ssm_out_g', 'new_v_w_out', 'new_v_norm2_g', 'new_v_w_ff1', 'new_v_w_ff2']
TWIN_LEAF_KINDS = {'loss': 'loss', 'grad_x': 'grad_x', 'grad_w_ada': 'grad_w', 'grad_b_ada': 'grad_w', 'grad_norm1_g': 'grad_w', 'grad_w_in': 'grad_w', 'grad_q_norm_g': 'grad_w', 'grad_k_norm_g': 'grad_w', 'grad_lam_re': 'grad_w', 'grad_lam_im': 'grad_w', 'grad_log_step': 'grad_w', 'grad_b_re': 'grad_w', 'grad_b_im': 'grad_w', 'grad_c_re': 'grad_w', 'grad_c_im': 'grad_w', 'grad_d_skip': 'grad_w', 'grad_w_glu': 'grad_w', 'grad_b_glu': 'grad_w', 'grad_attn_out_g': 'grad_w', 'grad_ssm_out_g': 'grad_w', 'grad_w_out': 'grad_w', 'grad_norm2_g': 'grad_w', 'grad_w_ff1': 'grad_w', 'grad_w_ff2': 'grad_w', 'delta_w_ada': 'delta_w', 'delta_b_ada': 'delta_w', 'delta_norm1_g': 'delta_w', 'delta_w_in': 'delta_w', 'delta_q_norm_g': 'delta_w', 'delta_k_norm_g': 'delta_w', 'delta_lam_re': 'delta_w', 'delta_lam_im': 'delta_w', 'delta_log_step': 'delta_w', 'delta_b_re': 'delta_w', 'delta_b_im': 'delta_w', 'delta_c_re': 'delta_w', 'delta_c_im': 'delta_w', 'delta_d_skip': 'delta_w', 'delta_w_glu': 'delta_w', 'delta_b_glu': 'delta_w', 'delta_attn_out_g': 'delta_w', 'delta_ssm_out_g': 'delta_w', 'delta_w_out': 'delta_w', 'delta_norm2_g': 'delta_w', 'delta_w_ff1': 'delta_w', 'delta_w_ff2': 'delta_w', 'new_m_w_ada': 'new_m', 'new_m_b_ada': 'new_m', 'new_m_norm1_g': 'new_m', 'new_m_w_in': 'new_m', 'new_m_q_norm_g': 'new_m', 'new_m_k_norm_g': 'new_m', 'new_m_lam_re': 'new_m', 'new_m_lam_im': 'new_m', 'new_m_log_step': 'new_m', 'new_m_b_re': 'new_m', 'new_m_b_im': 'new_m', 'new_m_c_re': 'new_m', 'new_m_c_im': 'new_m', 'new_m_d_skip': 'new_m', 'new_m_w_glu': 'new_m', 'new_m_b_glu': 'new_m', 'new_m_attn_out_g': 'new_m', 'new_m_ssm_out_g': 'new_m', 'new_m_w_out': 'new_m', 'new_m_norm2_g': 'new_m', 'new_m_w_ff1': 'new_m', 'new_m_w_ff2': 'new_m', 'new_v_w_ada': 'new_v', 'new_v_b_ada': 'new_v', 'new_v_norm1_g': 'new_v', 'new_v_w_in': 'new_v', 'new_v_q_norm_g': 'new_v', 'new_v_k_norm_g': 'new_v', 'new_v_lam_re': 'new_v', 'new_v_lam_im': 'new_v', 'new_v_log_step': 'new_v', 'new_v_b_re': 'new_v', 'new_v_b_im': 'new_v', 'new_v_c_re': 'new_v', 'new_v_c_im': 'new_v', 'new_v_d_skip': 'new_v', 'new_v_w_glu': 'new_v', 'new_v_b_glu': 'new_v', 'new_v_attn_out_g': 'new_v', 'new_v_ssm_out_g': 'new_v', 'new_v_w_out': 'new_v', 'new_v_norm2_g': 'new_v', 'new_v_w_ff1': 'new_v', 'new_v_w_ff2': 'new_v'}


def _forward(args):
    return _fwd_reference(*[args[k] for k in FWD_PARAMS])


def _output_shape():
    def fwd():
        inp = _fwd_setup_inputs(0)
        return _fwd_reference(*[inp[k] for k in FWD_PARAMS])
    out = _jax.eval_shape(fwd)
    return out.shape, out.dtype

N_MICROBATCH = 1
ADAM_LR = 0.001
ADAM_B1 = 0.9
ADAM_B2 = 0.999
ADAM_EPS = 1e-08
ADAM_WD = 0.01
ADAM_STEP = 10
PER_EXAMPLE_BATCH_AXIS = {'x': 0, 'c': 0, 'loss_target': 0}
SHARED_INPUTS = []
_WEIGHT_DTYPES = {'w_ada': _jnp.float32, 'b_ada': _jnp.float32, 'norm1_g': _jnp.float32, 'w_in': _jnp.float32, 'q_norm_g': _jnp.float32, 'k_norm_g': _jnp.float32, 'lam_re': _jnp.float32, 'lam_im': _jnp.float32, 'log_step': _jnp.float32, 'b_re': _jnp.float32, 'b_im': _jnp.float32, 'c_re': _jnp.float32, 'c_im': _jnp.float32, 'd_skip': _jnp.float32, 'w_glu': _jnp.float32, 'b_glu': _jnp.float32, 'attn_out_g': _jnp.float32, 'ssm_out_g': _jnp.float32, 'w_out': _jnp.float32, 'norm2_g': _jnp.float32, 'w_ff1': _jnp.float32, 'w_ff2': _jnp.float32}
MOMENT_SCALE = {'w_ada': 2.654420e+00, 'b_ada': 6.718745e+00, 'norm1_g': 8.450748e-02, 'w_in': 4.261902e-01, 'q_norm_g': 1.099751e-01, 'k_norm_g': 1.101253e-01, 'lam_re': 2.962450e-02, 'lam_im': 2.738275e-02, 'log_step': 2.186708e+00, 'b_re': 2.025846e-02, 'b_im': 1.950542e-02, 'c_re': 4.406405e-02, 'c_im': 3.855573e-02, 'd_skip': 8.734522e-01, 'w_glu': 1.320716e-01, 'b_glu': 3.781547e-01, 'attn_out_g': 3.514786e+00, 'ssm_out_g': 5.530736e+00, 'w_out': 8.646240e-01, 'norm2_g': 1.174439e+01, 'w_ff1': 3.092429e-01, 'w_ff2': 1.282067e+00}


def _to_microbatches(a, axis):
    t = _jnp.moveaxis(a, axis, 0)
    t = t.reshape((N_MICROBATCH, t.shape[0] // N_MICROBATCH) + t.shape[1:])
    return _jnp.moveaxis(t, 1, axis + 1)


def setup_inputs(seed: int = 0) -> dict:
    inp = _fwd_setup_inputs(seed)
    key = _jax.random.fold_in(_jax.random.key(seed), 7919)
    shape, _ = _output_shape()
    out = dict(inp)
    out["loss_target"] = _jax.random.normal(_jax.random.fold_in(key, 0), shape, _jnp.float32)
    for i, name in enumerate(TWIN_WEIGHTS):
        w = inp[name].astype(_jnp.float32)
        if MOMENT_SCALE is None:
            s = _jnp.sqrt(_jnp.mean(_jnp.square(w)) + 1e-30)
        else:
            s = MOMENT_SCALE[name]
        km, kv = _jax.random.split(_jax.random.fold_in(key, i + 1))
        out[name] = w
        out["m_" + name] = s * _jax.random.normal(km, w.shape, _jnp.float32)
        out["v_" + name] = (s * s) * _jax.random.uniform(kv, w.shape, _jnp.float32, 0.5, 1.5)
    if N_MICROBATCH > 1:
        for name, axis in PER_EXAMPLE_BATCH_AXIS.items():
            out[name] = _to_microbatches(out[name], axis)
    return {'x': out['x'], 'c': out['c'], 'w_ada': out['w_ada'], 'b_ada': out['b_ada'], 'norm1_g': out['norm1_g'], 'w_in': out['w_in'], 'q_norm_g': out['q_norm_g'], 'k_norm_g': out['k_norm_g'], 'lam_re': out['lam_re'], 'lam_im': out['lam_im'], 'log_step': out['log_step'], 'b_re': out['b_re'], 'b_im': out['b_im'], 'c_re': out['c_re'], 'c_im': out['c_im'], 'd_skip': out['d_skip'], 'w_glu': out['w_glu'], 'b_glu': out['b_glu'], 'attn_out_g': out['attn_out_g'], 'ssm_out_g': out['ssm_out_g'], 'w_out': out['w_out'], 'norm2_g': out['norm2_g'], 'w_ff1': out['w_ff1'], 'w_ff2': out['w_ff2'], 'loss_target': out['loss_target'], 'm_w_ada': out['m_w_ada'], 'm_b_ada': out['m_b_ada'], 'm_norm1_g': out['m_norm1_g'], 'm_w_in': out['m_w_in'], 'm_q_norm_g': out['m_q_norm_g'], 'm_k_norm_g': out['m_k_norm_g'], 'm_lam_re': out['m_lam_re'], 'm_lam_im': out['m_lam_im'], 'm_log_step': out['m_log_step'], 'm_b_re': out['m_b_re'], 'm_b_im': out['m_b_im'], 'm_c_re': out['m_c_re'], 'm_c_im': out['m_c_im'], 'm_d_skip': out['m_d_skip'], 'm_w_glu': out['m_w_glu'], 'm_b_glu': out['m_b_glu'], 'm_attn_out_g': out['m_attn_out_g'], 'm_ssm_out_g': out['m_ssm_out_g'], 'm_w_out': out['m_w_out'], 'm_norm2_g': out['m_norm2_g'], 'm_w_ff1': out['m_w_ff1'], 'm_w_ff2': out['m_w_ff2'], 'v_w_ada': out['v_w_ada'], 'v_b_ada': out['v_b_ada'], 'v_norm1_g': out['v_norm1_g'], 'v_w_in': out['v_w_in'], 'v_q_norm_g': out['v_q_norm_g'], 'v_k_norm_g': out['v_k_norm_g'], 'v_lam_re': out['v_lam_re'], 'v_lam_im': out['v_lam_im'], 'v_log_step': out['v_log_step'], 'v_b_re': out['v_b_re'], 'v_b_im': out['v_b_im'], 'v_c_re': out['v_c_re'], 'v_c_im': out['v_c_im'], 'v_d_skip': out['v_d_skip'], 'v_w_glu': out['v_w_glu'], 'v_b_glu': out['v_b_glu'], 'v_attn_out_g': out['v_attn_out_g'], 'v_ssm_out_g': out['v_ssm_out_g'], 'v_w_out': out['v_w_out'], 'v_norm2_g': out['v_norm2_g'], 'v_w_ff1': out['v_w_ff1'], 'v_w_ff2': out['v_w_ff2']}


def _loss(weights, diff, rest, loss_target):
    with _jax.named_scope("forward"):
        args = {**rest, TWIN_DIFF_INPUT: diff, **{k: w.astype(_WEIGHT_DTYPES[k]) for k, w in weights.items()}}
        y = _forward(args)
    with _jax.named_scope("loss_head"):
        err = _jnp.square(y.astype(_jnp.float32) - loss_target)
        return 0.5 * _jnp.sum(_jnp.mean(err, axis=-1)) if err.ndim else 0.5 * err


def _adamw(w, g, m, v):
    m = ADAM_B1 * m + (1.0 - ADAM_B1) * g
    v = ADAM_B2 * v + (1.0 - ADAM_B2) * _jnp.square(g)
    m_hat = m / (1.0 - ADAM_B1 ** ADAM_STEP)
    v_hat = v / (1.0 - ADAM_B2 ** ADAM_STEP)
    delta = -ADAM_LR * (m_hat / (_jnp.sqrt(v_hat) + ADAM_EPS) + ADAM_WD * w)
    return delta, m, v


def reference(x, c, w_ada, b_ada, norm1_g, w_in, q_norm_g, k_norm_g, lam_re, lam_im, log_step, b_re, b_im, c_re, c_im, d_skip, w_glu, b_glu, attn_out_g, ssm_out_g, w_out, norm2_g, w_ff1, w_ff2, loss_target, m_w_ada, m_b_ada, m_norm1_g, m_w_in, m_q_norm_g, m_k_norm_g, m_lam_re, m_lam_im, m_log_step, m_b_re, m_b_im, m_c_re, m_c_im, m_d_skip, m_w_glu, m_b_glu, m_attn_out_g, m_ssm_out_g, m_w_out, m_norm2_g, m_w_ff1, m_w_ff2, v_w_ada, v_b_ada, v_norm1_g, v_w_in, v_q_norm_g, v_k_norm_g, v_lam_re, v_lam_im, v_log_step, v_b_re, v_b_im, v_c_re, v_c_im, v_d_skip, v_w_glu, v_b_glu, v_attn_out_g, v_ssm_out_g, v_w_out, v_norm2_g, v_w_ff1, v_w_ff2):
    given = dict(x=x, c=c, w_ada=w_ada, b_ada=b_ada, norm1_g=norm1_g, w_in=w_in, q_norm_g=q_norm_g, k_norm_g=k_norm_g, lam_re=lam_re, lam_im=lam_im, log_step=log_step, b_re=b_re, b_im=b_im, c_re=c_re, c_im=c_im, d_skip=d_skip, w_glu=w_glu, b_glu=b_glu, attn_out_g=attn_out_g, ssm_out_g=ssm_out_g, w_out=w_out, norm2_g=norm2_g, w_ff1=w_ff1, w_ff2=w_ff2, loss_target=loss_target, m_w_ada=m_w_ada, m_b_ada=m_b_ada, m_norm1_g=m_norm1_g, m_w_in=m_w_in, m_q_norm_g=m_q_norm_g, m_k_norm_g=m_k_norm_g, m_lam_re=m_lam_re, m_lam_im=m_lam_im, m_log_step=m_log_step, m_b_re=m_b_re, m_b_im=m_b_im, m_c_re=m_c_re, m_c_im=m_c_im, m_d_skip=m_d_skip, m_w_glu=m_w_glu, m_b_glu=m_b_glu, m_attn_out_g=m_attn_out_g, m_ssm_out_g=m_ssm_out_g, m_w_out=m_w_out, m_norm2_g=m_norm2_g, m_w_ff1=m_w_ff1, m_w_ff2=m_w_ff2, v_w_ada=v_w_ada, v_b_ada=v_b_ada, v_norm1_g=v_norm1_g, v_w_in=v_w_in, v_q_norm_g=v_q_norm_g, v_k_norm_g=v_k_norm_g, v_lam_re=v_lam_re, v_lam_im=v_lam_im, v_log_step=v_log_step, v_b_re=v_b_re, v_b_im=v_b_im, v_c_re=v_c_re, v_c_im=v_c_im, v_d_skip=v_d_skip, v_w_glu=v_w_glu, v_b_glu=v_b_glu, v_attn_out_g=v_attn_out_g, v_ssm_out_g=v_ssm_out_g, v_w_out=v_w_out, v_norm2_g=v_norm2_g, v_w_ff1=v_w_ff1, v_w_ff2=v_w_ff2)
    weights = {n: given[n] for n in TWIN_WEIGHTS}
    shared = {n: given[n] for n in SHARED_INPUTS}
    per_example = {n: given[n] for n in ['x', 'c']}
    grad_fn = _jax.value_and_grad(_loss, argnums=(0, 1))

    def one_microbatch(ex, loss_target):
        ex = dict(ex)
        diff = ex.pop(TWIN_DIFF_INPUT)
        return grad_fn(weights, diff, {**shared, **ex}, loss_target)

    if N_MICROBATCH == 1:
        loss, (grad_w, grad_x) = one_microbatch(per_example, given["loss_target"])
    else:
        def body(carry, xs):
            loss_sum, grad_sum = carry
            l_k, (gw_k, gx_k) = one_microbatch(xs[0], xs[1])
            with _jax.named_scope("update"):
                return (loss_sum + l_k, _jax.tree.map(_jnp.add, grad_sum, gw_k)), gx_k

        init = (_jnp.zeros((), _jnp.float32), _jax.tree.map(_jnp.zeros_like, weights))
        (loss, grad_w), grad_x = _jax.lax.scan(body, init, (per_example, given["loss_target"]))
    with _jax.named_scope("update"):
        delta_w, new_m, new_v = {}, {}, {}
        for n in TWIN_WEIGHTS:
            delta_w[n], new_m[n], new_v[n] = _adamw(weights[n], grad_w[n], given["m_" + n], given["v_" + n])
    return (loss, grad_x, *[grad_w[n] for n in TWIN_WEIGHTS], *[delta_w[n] for n in TWIN_WEIGHTS],
            *[new_m[n] for n in TWIN_WEIGHTS], *[new_v[n] for n in TWIN_WEIGHTS])
```

```python
import functools
import math

import jax
import jax.numpy as jnp
import numpy as np
from jax import lax
from jax.experimental import pallas as pl
from jax.experimental.pallas import tpu as pltpu

F32 = jnp.float32
BF16 = jnp.bfloat16
MESH = pl.DeviceIdType.MESH

EPS = 1e-6
HEAD_DIM = 128
SSM_GROUP = 16
STATE_DIM = 64
GROUPS_PER_BLOCK = 8
DILATION_PATTERNS = ((128, 1), (512, 4), (2048, 16))
BAND = 128
NEG = -1e30

ADAM_LR, ADAM_B1, ADAM_B2, ADAM_EPS, ADAM_WD, ADAM_STEP = 0.001, 0.9, 0.999, 1e-08, 0.01, 10

V7X_VMEM_BYTES = 64 * 2**20
VMEM_LIMIT = 48 * 2**20
SUBLANES = 8
PACK_LANES = 1024


def _cparams(sem=None):
    return pltpu.CompilerParams(dimension_semantics=sem, vmem_limit_bytes=VMEM_LIMIT)


def _blk(n, want):
    b = min(n, want)
    while n % b:
        b //= 2
    return b


def _colsum8(v):
    tm, w = v.shape
    return v.reshape(tm // SUBLANES, SUBLANES, w).sum(axis=0)


def _rowwise(name, fn, rows, vecs, out_rows, out_accs, tm=256, n_rows=None):
    rows = [r if isinstance(r, tuple) else (r, r.shape[1], 0) for r in rows]
    rows = [r if len(r) == 4 else r + (0,) for r in rows]
    s = rows[0][0].shape[0] if n_rows is None else n_rows
    tm = _blk(s, tm)
    assert all(r[3] % tm == 0 for r in rows)
    n_in, n_or = len(rows) + len(vecs), len(out_rows)

    def body(*refs):
        outs, accs = fn(*[r[...] for r in refs[:n_in]])
        for r, v in zip(refs[n_in : n_in + n_or], outs):
            r[...] = v.astype(r.dtype)
        if out_accs:
            acc_refs = refs[n_in + n_or :]

            @pl.when(pl.program_id(0) == 0)
            def _():
                for r in acc_refs:
                    r[...] = jnp.zeros(r.shape, r.dtype)

            for r, v in zip(acc_refs, accs):
                r[...] += v

    in_specs = [pl.BlockSpec((tm, w), lambda i, cb=cb, rb=r0 // tm: (i + rb, cb)) for (_, w, cb, r0) in rows]
    in_specs += [pl.BlockSpec(v.shape, lambda i, nd=v.ndim: (0,) * nd) for v in vecs]
    out_specs = [pl.BlockSpec((tm, w), lambda i: (i, 0)) for (w, _) in out_rows]
    out_specs += [pl.BlockSpec(sh, lambda i, nd=len(sh): (0,) * nd) for sh in out_accs]
    out_shape = [jax.ShapeDtypeStruct((s, w), dt) for (w, dt) in out_rows]
    out_shape += [jax.ShapeDtypeStruct(sh, F32) for sh in out_accs]
    res = pl.pallas_call(
        body, name=name, grid=(s // tm,), in_specs=in_specs, out_specs=out_specs, out_shape=out_shape,
        compiler_params=_cparams(("arbitrary",)),
    )(*[r[0] for r in rows], *vecs)
    return res


def _single(name, fn, ins, out_shapes):
    n_in = len(ins)

    def body(*refs):
        outs = fn(*[r[...] for r in refs[:n_in]])
        for r, v in zip(refs[n_in:], outs):
            r[...] = v.astype(r.dtype)

    vm = pl.BlockSpec(memory_space=pltpu.VMEM)
    return pl.pallas_call(
        body, name=name, in_specs=[vm] * n_in, out_specs=[vm] * len(out_shapes),
        out_shape=[jax.ShapeDtypeStruct(sh, dt) for sh, dt in out_shapes], compiler_params=_cparams(),
    )(*ins)


_NN = (((1,), (0,)), ((), ()))
_NT = (((1,), (1,)), ((), ()))
_TN = (((0,), (0,)), ((), ()))


def _mm(name, a, b, mode, out_dtype, *, tm=512, tn=1024, tk=2048, b_cs=False, o_cs=False, epi=None, extras=()):
    if mode == "tn":
        k, m = a.shape
        n = b.shape[1]
    else:
        m, k = a.shape
        if mode == "nn":
            n = b.shape[2] * 4 if b_cs else b.shape[1]
        else:
            n = b.shape[1] if b_cs else b.shape[0]
    tm = _blk(m, tm)
    tn = _blk(n // 4 if (b_cs and mode == "nn") or o_cs else n, tn)
    tk = _blk(k // 4 if (b_cs and mode == "nt") else k, tk)
    nk = k // tk
    dims = {"nn": _NN, "nt": _NT, "tn": _TN}[mode]
    n_ex = len(extras)

    def body(*refs):
        a_ref, b_ref = refs[0], refs[1]
        ex_refs, o_ref = refs[2 : 2 + n_ex], refs[2 + n_ex]

        def finish(acc):
            if epi is not None:
                acc = epi(acc, *[r[...] for r in ex_refs])
            o_ref[...] = acc.astype(o_ref.dtype)

        part = lax.dot_general(a_ref[...], b_ref[...], dims, preferred_element_type=F32)
        if nk == 1:
            finish(part)
        else:
            acc_ref = refs[3 + n_ex]
            kk = pl.program_id(2)

            @pl.when(kk == 0)
            def _():
                acc_ref[...] = part

            @pl.when(kk > 0)
            def _():
                acc_ref[...] += part

            @pl.when(kk == nk - 1)
            def _():
                finish(acc_ref[...])

    a_spec = pl.BlockSpec((tk, tm), lambda i, j, kk: (kk, i)) if mode == "tn" else pl.BlockSpec((tm, tk), lambda i, j, kk: (i, kk))
    if mode == "nn":
        if b_cs:
            per = (n // 4) // tn
            b_spec = pl.BlockSpec((None, tk, tn), lambda i, j, kk: (j // per, kk, j % per))
        else:
            b_spec = pl.BlockSpec((tk, tn), lambda i, j, kk: (kk, j))
    elif mode == "nt":
        if b_cs:
            per = (k // 4) // tk
            b_spec = pl.BlockSpec((None, tn, tk), lambda i, j, kk: (kk // per, j, kk % per))
        else:
            b_spec = pl.BlockSpec((tn, tk), lambda i, j, kk: (j, kk))
    else:
        b_spec = pl.BlockSpec((tk, tn), lambda i, j, kk: (kk, j))
    if o_cs:
        per = (n // 4) // tn
        o_spec = pl.BlockSpec((None, tm, tn), lambda i, j, kk: (j // per, i, j % per))
        o_shape = jax.ShapeDtypeStruct((4, m, n // 4), out_dtype)
    else:
        o_spec = pl.BlockSpec((tm, tn), lambda i, j, kk: (i, j))
        o_shape = jax.ShapeDtypeStruct((m, n), out_dtype)
    ex_specs = [pl.BlockSpec((tm, tn), lambda i, j, kk: (i, j)) for _ in extras]
    return pl.pallas_call(
        body, name=name, grid=(m // tm, n // tn, nk), in_specs=[a_spec, b_spec] + ex_specs, out_specs=o_spec, out_shape=o_shape,
        scratch_shapes=[pltpu.VMEM((tm, tn), F32)] if nk > 1 else [],
        compiler_params=_cparams(("parallel", "parallel", "arbitrary")),
    )(a, b, *extras)


def _place():
    x, y, c = lax.axis_index("x"), lax.axis_index("y"), lax.axis_index("c")
    return x, y, c, [(1 - x, y), (x, 1 - y), (1 - x, 1 - y)]


def _allgather8(name, v):
    m_per, n = v.shape

    def body(x_ref, out_ref, send_sems, recv_sems, local_sem):
        x, y, c, chips = _place()
        me, sibling = (x, y, c), (x, y, 1 - c)

        def rows(px, py, pc):
            return out_ref.at[pl.ds((4 * px + 2 * py + pc) * m_per, m_per), :]

        def copy(k, block, to, src=None):
            return pltpu.make_async_remote_copy(
                src_ref=rows(*block) if src is None else src, dst_ref=rows(*block), send_sem=send_sems.at[k],
                recv_sem=recv_sems.at[k], device_id=to, device_id_type=MESH)

        mine = pltpu.make_async_copy(x_ref, rows(*me), local_sem)
        mine.start()
        first = [copy(0, me, sibling, src=x_ref)]
        first += [copy(1 + j, me, (*chip, c), src=x_ref) for j, chip in enumerate(chips)]
        for cp in first:
            cp.start()
        passed = [copy(4 + j, (*chip, c), sibling) for j, chip in enumerate(chips)]
        for j, chip in enumerate(chips):
            copy(1 + j, (*chip, c), me).wait_recv()
            passed[j].start()
        copy(0, sibling, me).wait_recv()
        for j, chip in enumerate(chips):
            copy(4 + j, (*chip, 1 - c), me).wait_recv()
        for cp in first + passed:
            cp.wait_send()
        mine.wait()

    return pl.pallas_call(
        body, name=name, out_shape=jax.ShapeDtypeStruct((8 * m_per, n), v.dtype),
        in_specs=[pl.BlockSpec(memory_space=pltpu.VMEM)], out_specs=pl.BlockSpec(memory_space=pltpu.VMEM),
        scratch_shapes=[pltpu.SemaphoreType.DMA((7,)), pltpu.SemaphoreType.DMA((7,)), pltpu.SemaphoreType.DMA],
        compiler_params=_cparams(),
    )(v)


def _gather_weights(ws):
    nw = len(ws)

    def body(*refs):
        ins, outs = refs[:nw], refs[nw : 2 * nw]
        send_sems, recv_sems, local_sems = refs[2 * nw :]
        x, y, c, chips = _place()
        sibling = (x, y, 1 - c)
        pending = []
        for w in range(nw):
            half = ins[w].shape[0] // 2

            def blk(px, py, pc, w=w, half=half):
                return outs[w].at[2 * px + py, pl.ds(pc * half, half), :]

            def copy(k, block, to, src=None, w=w, blk=blk):
                return pltpu.make_async_remote_copy(
                    src_ref=blk(*block) if src is None else src, dst_ref=blk(*block), send_sem=send_sems.at[6 * w + k],
                    recv_sem=recv_sems.at[6 * w + k], device_id=to, device_id_type=MESH)

            mine = pltpu.make_async_copy(ins[w], outs[w].at[2 * x + y], local_sems.at[w])
            mine.start()
            first = [copy(j, (x, y, c), (*chip, c), src=ins[w].at[pl.ds(c * half, half), :]) for j, chip in enumerate(chips)]
            for cp in first:
                cp.start()
            pending.append((mine, first, copy))
        for w in range(nw):
            mine, first, copy = pending[w]
            passed = [copy(3 + j, (*chip, c), sibling) for j, chip in enumerate(chips)]
            for j, chip in enumerate(chips):
                copy(j, (*chip, c), (x, y, c)).wait_recv()
                passed[j].start()
            pending[w] = (mine, first + passed, copy)
        for w in range(nw):
            mine, sent, copy = pending[w]
            for j, chip in enumerate(chips):
                copy(3 + j, (*chip, 1 - c), (x, y, c)).wait_recv()
            for cp in sent:
                cp.wait_send()
            mine.wait()

    hbm = pl.BlockSpec(memory_space=pl.ANY)
    return pl.pallas_call(
        body, name="gather_weights", out_shape=[jax.ShapeDtypeStruct((4,) + w.shape, w.dtype) for w in ws],
        in_specs=[hbm] * nw, out_specs=[hbm] * nw,
        scratch_shapes=[pltpu.SemaphoreType.DMA((6 * nw,)), pltpu.SemaphoreType.DMA((6 * nw,)), pltpu.SemaphoreType.DMA((nw,))],
        compiler_params=_cparams(),
    )(*ws)


def _swap_halves(gs):
    nw = len(gs)

    def body(*refs):
        ins, owns, gots = refs[:nw], refs[nw : 2 * nw], refs[2 * nw : 3 * nw]
        send_sems, recv_sems, local_sems = refs[3 * nw :]
        x, y, c, _ = _place()
        cps = []
        for w in range(nw):
            half = ins[w].shape[1] // 2
            mine = pltpu.make_async_copy(ins[w].at[:, pl.ds(c * half, half), :], owns[w], local_sems.at[w])
            mine.start()
            cp = pltpu.make_async_remote_copy(
                src_ref=ins[w].at[:, pl.ds((1 - c) * half, half), :], dst_ref=gots[w], send_sem=send_sems.at[w],
                recv_sem=recv_sems.at[w], device_id=(x, y, 1 - c), device_id_type=MESH)
            cp.start()
            cps.append((mine, cp))
        for mine, cp in cps:
            cp.wait()
            mine.wait()

    hbm = pl.BlockSpec(memory_space=pl.ANY)
    halves = [jax.ShapeDtypeStruct((4, g.shape[1] // 2, g.shape[2]), g.dtype) for g in gs]
    res = pl.pallas_call(
        body, name="swap_halves", out_shape=halves + halves, in_specs=[hbm] * nw, out_specs=[hbm] * (2 * nw),
        scratch_shapes=[pltpu.SemaphoreType.DMA((nw,)), pltpu.SemaphoreType.DMA((nw,)), pltpu.SemaphoreType.DMA((nw,))],
        compiler_params=_cparams(),
    )(*gs)
    return res[:nw], res[nw:]


def _scatter_chips(ps):
    nw = len(ps)

    def body(*refs):
        ins, owns, gots = refs[:nw], refs[nw : 2 * nw], refs[2 * nw : 3 * nw]
        send_sems, recv_sems, local_sems = refs[3 * nw :]
        x, y, c, chips = _place()
        cps = []
        for w in range(nw):
            mine = pltpu.make_async_copy(ins[w].at[2 * x + y], owns[w], local_sems.at[w])
            mine.start()
            cps.append(mine)
            for j, (px, py) in enumerate(chips):
                cp = pltpu.make_async_remote_copy(
                    src_ref=ins[w].at[2 * px + py], dst_ref=gots[w].at[j], send_sem=send_sems.at[3 * w + j],
                    recv_sem=recv_sems.at[3 * w + j], device_id=(px, py, c), device_id_type=MESH)
                cp.start()
                cps.append(cp)
        for cp in cps:
            cp.wait()

    hbm = pl.BlockSpec(memory_space=pl.ANY)
    owns = [jax.ShapeDtypeStruct(p.shape[1:], p.dtype) for p in ps]
    gots = [jax.ShapeDtypeStruct((3,) + p.shape[1:], p.dtype) for p in ps]
    res = pl.pallas_call(
        body, name="scatter_chips", out_shape=owns + gots, in_specs=[hbm] * nw, out_specs=[hbm] * (2 * nw),
        scratch_shapes=[pltpu.SemaphoreType.DMA((3 * nw,)), pltpu.SemaphoreType.DMA((3 * nw,)), pltpu.SemaphoreType.DMA((nw,))],
        compiler_params=_cparams(),
    )(*ps)
    return res[:nw], res[nw:]


def _join_halves(fs):
    nw = len(fs)

    def body(*refs):
        ins, outs = refs[:nw], refs[nw : 2 * nw]
        send_sems, recv_sems, local_sems = refs[2 * nw :]
        x, y, c, _ = _place()
        cps = []
        for w in range(nw):
            mine = pltpu.make_async_copy(ins[w], outs[w].at[c], local_sems.at[w])
            mine.start()
            cp = pltpu.make_async_remote_copy(
                src_ref=ins[w], dst_ref=outs[w].at[c], send_sem=send_sems.at[w], recv_sem=recv_sems.at[w],
                device_id=(x, y, 1 - c), device_id_type=MESH)
            cp.start()
            cps.append((mine, cp, w))
        for mine, cp, w in cps:
            cp.wait_send()
            pltpu.make_async_remote_copy(
                src_ref=ins[w], dst_ref=outs[w].at[1 - c], send_sem=send_sems.at[w], recv_sem=recv_sems.at[w],
                device_id=(x, y, 1 - c), device_id_type=MESH).wait_recv()
            mine.wait()

    hbm = pl.BlockSpec(memory_space=pl.ANY)
    return pl.pallas_call(
        body, name="join_halves", out_shape=[jax.ShapeDtypeStruct((2,) + f.shape, f.dtype) for f in fs],
        in_specs=[hbm] * nw, out_specs=[hbm] * nw,
        scratch_shapes=[pltpu.SemaphoreType.DMA((nw,)), pltpu.SemaphoreType.DMA((nw,)), pltpu.SemaphoreType.DMA((nw,))],
        compiler_params=_cparams(),
    )(*fs)


def _rstd(v):
    return lax.rsqrt(jnp.mean(v * v, axis=-1, keepdims=True) + EPS)


def _rms_bwd(dy, xh, r):
    return r * (dy - xh * jnp.mean(dy * xh, axis=-1, keepdims=True))


def _heads(v):
    return [v[:, h * HEAD_DIM : (h + 1) * HEAD_DIM] for h in range(v.shape[1] // HEAD_DIM)]


def _gelu(v):
    k = math.sqrt(2.0 / math.pi)
    return 0.5 * v * (1.0 + jnp.tanh(k * (v + 0.044715 * v * v * v)))


def _gelu_grad(v):
    k = math.sqrt(2.0 / math.pi)
    t = jnp.tanh(k * (v + 0.044715 * v * v * v))
    return 0.5 * (1.0 + t) + 0.5 * v * (1.0 - t * t) * k * (1.0 + 3 * 0.044715 * v * v)


def _sigmoid(v):
    return 1.0 / (1.0 + jnp.exp(-v))


def _slopes(n_heads):
    return [2.0 ** (-8.0 * (h + 1.0) / n_heads) for h in range(n_heads)]


def _band_iotas():
    ri = lax.broadcasted_iota(jnp.int32, (BAND, BAND), 0)
    ci = lax.broadcasted_iota(jnp.int32, (BAND, BAND), 1)
    return ri, ci


def _attn_fwd(name, q, k, v, bps, dil):
    s, aw = q.shape
    n_heads, nb = aw // HEAD_DIM, s // BAND
    scale = HEAD_DIM**-0.5
    slopes = _slopes(n_heads)

    def body(q_ref, kp_ref, kc_ref, vp_ref, vc_ref, o_ref, l_ref):
        b = pl.program_id(0)
        first = (b % bps) == 0
        ri, ci = _band_iotas()
        dist_c = (ri - ci).astype(F32)
        dist_p = (ri - ci + BAND).astype(F32)
        valid_c = ci <= ri
        valid_p = ci >= ri + jnp.where(first, BAND, 0)
        for h in range(n_heads):
            sl = slice(h * HEAD_DIM, (h + 1) * HEAD_DIM)
            bias = slopes[h] * dil
            qh = q_ref[:, sl]
            sc = lax.dot_general(qh, kc_ref[:, sl], _NT, preferred_element_type=F32) * scale - bias * dist_c
            sp = lax.dot_general(qh, kp_ref[:, sl], _NT, preferred_element_type=F32) * scale - bias * dist_p
            sc = jnp.where(valid_c, sc, NEG)
            sp = jnp.where(valid_p, sp, NEG)
            m = jnp.maximum(jnp.max(sc, axis=1, keepdims=True), jnp.max(sp, axis=1, keepdims=True))
            pc, pp = jnp.exp(sc - m), jnp.exp(sp - m)
            den = jnp.sum(pc, axis=1, keepdims=True) + jnp.sum(pp, axis=1, keepdims=True)
            acc = jnp.dot(pc.astype(BF16), vc_ref[:, sl], preferred_element_type=F32)
            acc += jnp.dot(pp.astype(BF16), vp_ref[:, sl], preferred_element_type=F32)
            o_ref[:, sl] = acc / den
            l_ref[:, sl] = jnp.broadcast_to(m + jnp.log(den), (BAND, HEAD_DIM))

    cur = pl.BlockSpec((BAND, aw), lambda b: (b, 0))
    prev = pl.BlockSpec((BAND, aw), lambda b: (jnp.maximum(b - 1, 0), 0))
    return pl.pallas_call(
        body, name=name, grid=(nb,), in_specs=[cur, prev, cur, prev, cur], out_specs=[cur, cur],
        out_shape=[jax.ShapeDtypeStruct((s, aw), F32)] * 2, compiler_params=_cparams(("arbitrary",)),
    )(q, k, k, v, v)


def _attn_bwd(name, q, k, v, do, lse, delta, bps, dil):
    s, aw = q.shape
    n_heads, nb = aw // HEAD_DIM, s // BAND
    scale = HEAD_DIM**-0.5
    slopes = _slopes(n_heads)

    def body(qc_ref, qn_ref, doc_ref, don_ref, lc_ref, ln_ref, dc_ref, dn_ref, k_ref, v_ref, dq_ref, dk_ref, dv_ref, carry):
        j = pl.program_id(0)

        @pl.when(j == 0)
        def _():
            carry[...] = jnp.zeros(carry.shape, F32)

        has_next = jnp.logical_and((j + 1) % bps != 0, j + 1 < nb)
        ri, ci = _band_iotas()
        dist_c = (ri - ci).astype(F32)
        dist_p = (ri - ci + BAND).astype(F32)
        valid_c = ci <= ri
        valid_n = ci >= ri + jnp.where(has_next, 0, BAND)
        for h in range(n_heads):
            sl = slice(h * HEAD_DIM, (h + 1) * HEAD_DIM)
            bias = slopes[h] * dil
            kh, vh = k_ref[:, sl], v_ref[:, sl]

            def side(q_ref, do_ref, l_ref, d_ref, dist, valid):
                qh, doh = q_ref[:, sl], do_ref[:, sl]
                sc = lax.dot_general(qh, kh, _NT, preferred_element_type=F32) * scale - bias * dist
                p = jnp.where(valid, jnp.exp(sc - l_ref[:, sl]), 0.0)
                dp = lax.dot_general(doh, vh, _NT, preferred_element_type=F32)
                ds = (p * (dp - d_ref[:, sl])).astype(BF16)
                dv = lax.dot_general(p.astype(BF16), doh, _TN, preferred_element_type=F32)
                dk = lax.dot_general(ds, qh, _TN, preferred_element_type=F32) * scale
                dq = jnp.dot(ds, kh, preferred_element_type=F32) * scale
                return dq, dk, dv

            dq_c, dk_c, dv_c = side(qc_ref, doc_ref, lc_ref, dc_ref, dist_c, valid_c)
            dq_n, dk_n, dv_n = side(qn_ref, don_ref, ln_ref, dn_ref, dist_p, valid_n)
            dq_ref[:, sl] = carry[:, sl] + dq_c
            carry[:, sl] = dq_n
            dk_ref[:, sl] = dk_c + dk_n
            dv_ref[:, sl] = dv_c + dv_n

    cur = pl.BlockSpec((BAND, aw), lambda b: (b, 0))
    nxt = pl.BlockSpec((BAND, aw), lambda b: (jnp.minimum(b + 1, nb - 1), 0))
    return pl.pallas_call(
        body, name=name, grid=(nb,), in_specs=[cur, nxt, cur, nxt, cur, nxt, cur, nxt, cur, cur], out_specs=[cur] * 3,
        out_shape=[jax.ShapeDtypeStruct((s, aw), F32)] * 3, scratch_shapes=[pltpu.VMEM((BAND, aw), F32)],
        compiler_params=_cparams(("arbitrary",)),
    )(q, q, do, do, lse, lse, delta, delta, k, v)


def _to_streams(v, dil):
    if dil == 1:
        return v
    s, w = v.shape
    return v.reshape(s // dil, dil, w).transpose(1, 0, 2).reshape(s, w)


def _from_streams(v, dil):
    if dil == 1:
        return v
    s, w = v.shape
    return v.reshape(dil, s // dil, w).transpose(1, 0, 2).reshape(s, w)


SCAN_LANES = 256
SSM_ROWS = 128


def _cmul(ar, ai, br, bi):
    return ar * br - ai * bi, ar * bi + ai * br


def _cmul_conj(ar, ai, br, bi):
    return ar * br + ai * bi, ar * bi - ai * br


def _ssm_fwd(u_src, bre, bim, cre, cim, consts, d_skip):
    u_arr, sw, ucb = u_src
    s = u_arr.shape[0]
    nblk = bre.shape[0]
    nst = nblk * 512
    t = _blk(s, SSM_ROWS)
    w = _blk(nst, SCAN_LANES)

    def body(u_ref, bre_ref, bim_ref, cre_ref, cim_ref, k_ref, ds_ref, y_ref, hr_ref, hi_ref, xr, xi, car_r, car_i):
        @pl.when(pl.program_id(0) == 0)
        def _():
            car_r[...] = jnp.zeros(car_r.shape, F32)
            car_i[...] = jnp.zeros(car_i.shape, F32)

        u = u_ref[...]
        ub = u.astype(BF16)
        for b in range(nblk):
            ch, st = slice(b * 128, (b + 1) * 128), slice(b * 512, (b + 1) * 512)
            xr[:, st] = jnp.dot(ub[:, ch], bre_ref[b], preferred_element_type=F32)
            xi[:, st] = jnp.dot(ub[:, ch], bim_ref[b], preferred_element_type=F32)
        for c in range(nst // w):
            ls = slice(c * w, (c + 1) * w)
            mult = [k_ref[i, :, ls] for i in range(8)]

            def step(i, carry, ls=ls, mult=mult):
                cr, ci = carry
                r0 = pl.multiple_of(i * SUBLANES, SUBLANES)
                hr, hi = xr[pl.ds(r0, SUBLANES), ls], xi[pl.ds(r0, SUBLANES), ls]
                for n, sh in enumerate((1, 2, 4)):
                    pr, pi = _cmul(mult[2 * n], mult[2 * n + 1], pltpu.roll(hr, sh, 0), pltpu.roll(hi, sh, 0))
                    hr, hi = hr + pr, hi + pi
                pr, pi = _cmul(mult[6], mult[7], cr, ci)
                hr, hi = hr + pr, hi + pi
                xr[pl.ds(r0, SUBLANES), ls] = hr
                xi[pl.ds(r0, SUBLANES), ls] = hi
                return jnp.broadcast_to(hr[7:8, :], hr.shape), jnp.broadcast_to(hi[7:8, :], hi.shape)

            cr, ci = lax.fori_loop(0, t // SUBLANES, step, (car_r[:, ls], car_i[:, ls]))
            car_r[:, ls] = cr
            car_i[:, ls] = ci
        hrb, hib = xr[...].astype(BF16), xi[...].astype(BF16)
        hr_ref[...] = hrb
        hi_ref[...] = hib
        for b in range(nblk):
            ch, st = slice(b * 128, (b + 1) * 128), slice(b * 512, (b + 1) * 512)
            yb = jnp.dot(hrb[:, st], cre_ref[b], preferred_element_type=F32)
            yb -= jnp.dot(hib[:, st], cim_ref[b], preferred_element_type=F32)
            y_ref[:, ch] = yb + ds_ref[:, ch] * u[:, ch]

    whole = lambda a: pl.BlockSpec(a.shape, lambda i, nd=a.ndim: (0,) * nd)
    return pl.pallas_call(
        body, name="ssm_fwd", grid=(s // t,),
        in_specs=[pl.BlockSpec((t, sw), lambda i: (i, ucb))] + [whole(a) for a in (bre, bim, cre, cim, consts, d_skip)],
        out_specs=[pl.BlockSpec((t, sw), lambda i: (i, 0)), pl.BlockSpec((t, nst), lambda i: (i, 0)), pl.BlockSpec((t, nst), lambda i: (i, 0))],
        out_shape=[jax.ShapeDtypeStruct((s, sw), F32), jax.ShapeDtypeStruct((s, nst), BF16), jax.ShapeDtypeStruct((s, nst), BF16)],
        scratch_shapes=[pltpu.VMEM((t, nst), F32), pltpu.VMEM((t, nst), F32), pltpu.VMEM((SUBLANES, nst), F32), pltpu.VMEM((SUBLANES, nst), F32)],
        compiler_params=_cparams(("arbitrary",)),
    )(u_arr, bre, bim, cre, cim, consts, d_skip)


def _ssm_bwd(dy, u_src, hr, hi, ctre, ctim, btre, btim, consts, d_skip):
    u_arr, sw, ucb = u_src
    s = dy.shape[0]
    nblk = ctre.shape[0]
    nst = nblk * 512
    t = _blk(s, SSM_ROWS)
    w = _blk(nst, SCAN_LANES)
    nt = s // t

    def body(dy_ref, u_ref, hr_ref, hi_ref, ctre_ref, ctim_ref, btre_ref, btim_ref, k_ref, ds_ref,
             du_ref, db_ref, dc_ref, da_ref, dd_ref, lr, li, hfr, hfi, car_r, car_i):
        @pl.when(pl.program_id(0) == 0)
        def _():
            car_r[...] = jnp.zeros(car_r.shape, F32)
            car_i[...] = jnp.zeros(car_i.shape, F32)
            db_ref[...] = jnp.zeros(db_ref.shape, F32)
            dc_ref[...] = jnp.zeros(dc_ref.shape, F32)
            da_ref[...] = jnp.zeros(da_ref.shape, F32)
            dd_ref[...] = jnp.zeros(dd_ref.shape, F32)

        dyv, u = dy_ref[...], u_ref[...]
        dyb, ub = dyv.astype(BF16), u.astype(BF16)
        hrb, hib = hr_ref[...], hi_ref[...]
        hfr[...] = hrb.astype(F32)
        hfi[...] = hib.astype(F32)
        for b in range(nblk):
            ch, st = slice(b * 128, (b + 1) * 128), slice(b * 512, (b + 1) * 512)
            lr[:, st] = jnp.dot(dyb[:, ch], ctre_ref[b], preferred_element_type=F32)
            li[:, st] = -jnp.dot(dyb[:, ch], ctim_ref[b], preferred_element_type=F32)
        last_row = lax.broadcasted_iota(jnp.int32, (SUBLANES, w), 0) == SUBLANES - 1
        for c in range(nst // w):
            ls = slice(c * w, (c + 1) * w)
            mult = [k_ref[i, :, ls] for i in range(8)]

            def step(i, carry, ls=ls, mult=mult):
                cr, ci, ar, ai = carry
                r0 = pl.multiple_of((t // SUBLANES - 1 - i) * SUBLANES, SUBLANES)
                gr, gi = lr[pl.ds(r0, SUBLANES), ls], li[pl.ds(r0, SUBLANES), ls]
                for n, sh in enumerate((1, 2, 4)):
                    pr, pi = _cmul_conj(mult[2 * n], mult[2 * n + 1], pltpu.roll(gr, SUBLANES - sh, 0), pltpu.roll(gi, SUBLANES - sh, 0))
                    gr, gi = gr + pr, gi + pi
                pr, pi = _cmul_conj(mult[6], mult[7], cr, ci)
                gr, gi = gr + pr, gi + pi
                lr[pl.ds(r0, SUBLANES), ls] = gr
                li[pl.ds(r0, SUBLANES), ls] = gi
                nr = jnp.where(last_row, cr, pltpu.roll(gr, SUBLANES - 1, 0))
                ni = jnp.where(last_row, ci, pltpu.roll(gi, SUBLANES - 1, 0))
                fr, fi = hfr[pl.ds(r0, SUBLANES), ls], hfi[pl.ds(r0, SUBLANES), ls]
                pr, pi = _cmul_conj(fr, fi, nr, ni)
                return (jnp.broadcast_to(gr[0:1, :], gr.shape), jnp.broadcast_to(gi[0:1, :], gi.shape), ar + pr, ai + pi)

            zero = jnp.zeros((SUBLANES, w), F32)
            cr, ci, ar, ai = lax.fori_loop(0, t // SUBLANES, step, (car_r[:, ls], car_i[:, ls], zero, zero))
            car_r[:, ls] = cr
            car_i[:, ls] = ci
            da_ref[0, :, ls] += ar
            da_ref[1, :, ls] += ai
        lrb, lib = lr[...].astype(BF16), li[...].astype(BF16)
        for b in range(nblk):
            ch, st = slice(b * 128, (b + 1) * 128), slice(b * 512, (b + 1) * 512)
            dub = jnp.dot(lrb[:, st], btre_ref[b], preferred_element_type=F32)
            dub += jnp.dot(lib[:, st], btim_ref[b], preferred_element_type=F32)
            du_ref[:, ch] = dub + ds_ref[:, ch] * dyv[:, ch]
            db_ref[0, b] += lax.dot_general(ub[:, ch], lrb[:, st], _TN, preferred_element_type=F32)
            db_ref[1, b] += lax.dot_general(ub[:, ch], lib[:, st], _TN, preferred_element_type=F32)
            dc_ref[0, b] += lax.dot_general(dyb[:, ch], hrb[:, st], _TN, preferred_element_type=F32)
            dc_ref[1, b] -= lax.dot_general(dyb[:, ch], hib[:, st], _TN, preferred_element_type=F32)
        dd_ref[...] += _colsum8(dyv * u)

    whole = lambda a: pl.BlockSpec(a.shape, lambda i, nd=a.ndim: (0,) * nd)
    rev = lambda wd, cb=0: pl.BlockSpec((t, wd), lambda i, cb=cb: (nt - 1 - i, cb))
    acc = lambda sh: pl.BlockSpec(sh, lambda i, nd=len(sh): (0,) * nd)
    db_shape, da_shape, dd_shape = (2, nblk, 128, 512), (2, SUBLANES, nst), (SUBLANES, sw)
    return pl.pallas_call(
        body, name="ssm_bwd", grid=(nt,),
        in_specs=[rev(sw), rev(sw, ucb), rev(nst), rev(nst)] + [whole(a) for a in (ctre, ctim, btre, btim, consts, d_skip)],
        out_specs=[rev(sw), acc(db_shape), acc(db_shape), acc(da_shape), acc(dd_shape)],
        out_shape=[jax.ShapeDtypeStruct((s, sw), F32), jax.ShapeDtypeStruct(db_shape, F32), jax.ShapeDtypeStruct(db_shape, F32),
                   jax.ShapeDtypeStruct(da_shape, F32), jax.ShapeDtypeStruct(dd_shape, F32)],
        scratch_shapes=[pltpu.VMEM((t, nst), F32)] * 4 + [pltpu.VMEM((SUBLANES, nst), F32)] * 2,
        compiler_params=_cparams(("arbitrary",)),
    )(dy, u_arr, hr, hi, ctre, ctim, btre, btim, consts, d_skip)


def _ssm_discretise(lam_re, lam_im, log_step):
    step = jnp.exp(log_step)
    e = jnp.exp(lam_re * step)
    ar, ai = e * jnp.cos(lam_im * step), e * jnp.sin(lam_im * step)
    den = lam_re * lam_re + lam_im * lam_im
    inv_r, inv_i = lam_re / den, -lam_im / den
    fr, fi = _cmul(ar - 1.0, ai, inv_r, inv_i)
    return step, ar, ai, inv_r, inv_i, fr, fi


def _ssm_prep_fn(lam_re, lam_im, log_step):
    _, ar, ai, _, _, fr, fi = _ssm_discretise(lam_re, lam_im, log_step)
    pw = [(ar, ai)]
    for n in range(1, 8):
        pw.append(_cmul(*pw[n - 1], ar, ai))
    return [p[0] for p in pw] + [p[1] for p in pw] + [fr, fi]


def _ssm_param_grad_fn(lam_re, lam_im, log_step, da_r, da_i, df_r, df_i):
    step, ar, ai, inv_r, inv_i, fr, fi = _ssm_discretise(lam_re, lam_im, log_step)
    pr, pi = _cmul_conj(inv_r, inv_i, df_r, df_i)
    dat_r, dat_i = da_r + pr, da_i + pi
    wr, wi = _cmul(fr, fi, inv_r, inv_i)
    dl_r, dl_i = _cmul_conj(-wr, -wi, df_r, df_i)
    dz_r, dz_i = _cmul_conj(ar, ai, dat_r, dat_i)
    dl_r, dl_i = dl_r + step * dz_r, dl_i + step * dz_i
    dstep = jnp.sum(dz_r * lam_re + dz_i * lam_im, axis=-1, keepdims=True)
    return [dl_r, dl_i, step * dstep]


def _block_diag(v, rows_first):
    g, a, b = v.shape
    nb = g // GROUPS_PER_BLOCK
    eye = jnp.eye(GROUPS_PER_BLOCK, dtype=v.dtype)
    v = v.reshape(nb, GROUPS_PER_BLOCK, a, 1, b) * eye[None, :, None, :, None]
    return v.reshape(nb, GROUPS_PER_BLOCK * a, GROUPS_PER_BLOCK * b)


def _block_diag_take(v, a, b):
    nb = v.shape[0]
    v = v.reshape(nb, GROUPS_PER_BLOCK, a, GROUPS_PER_BLOCK, b)
    idx = jnp.arange(GROUPS_PER_BLOCK)
    return v[:, idx, :, idx, :].transpose(1, 0, 2, 3).reshape(nb * GROUPS_PER_BLOCK, a, b)


def _adamw(w, g, m, v):
    m = ADAM_B1 * m + (1.0 - ADAM_B1) * g
    v = ADAM_B2 * v + (1.0 - ADAM_B2) * (g * g)
    m_hat = m / (1.0 - ADAM_B1**ADAM_STEP)
    v_hat = v / (1.0 - ADAM_B2**ADAM_STEP)
    delta = -ADAM_LR * (m_hat / (jnp.sqrt(v_hat) + ADAM_EPS) + ADAM_WD * w)
    return delta, m, v


def _adamw_rows(name, w, g, m, v):
    wd = w.shape[1]
    return _rowwise(name, lambda a, b, c, d: (list(_adamw(a, b, c, d)), []), [w, g, m, v], [], [(wd, F32)] * 3, [], tm=128)


SMALL = ["b_ada", "norm1_g", "q_norm_g", "k_norm_g", "lam_re", "lam_im", "log_step", "b_re", "b_im", "c_re", "c_im",
         "d_skip", "b_glu", "attn_out_g", "ssm_out_g", "norm2_g"]
LARGE = ["w_in", "w_glu", "w_out", "w_ff1", "w_ff2"]
ORDER = ["w_ada", "b_ada", "norm1_g", "w_in", "q_norm_g", "k_norm_g", "lam_re", "lam_im", "log_step", "b_re", "b_im", "c_re",
         "c_im", "d_skip", "w_glu", "b_glu", "attn_out_g", "ssm_out_g", "w_out", "norm2_g", "w_ff1", "w_ff2"]


def _pack(arrs):
    flat = jnp.concatenate([a.reshape(-1) for a in arrs])
    rows = -(-flat.shape[0] // PACK_LANES)
    rows = -(-rows // SUBLANES) * SUBLANES
    return jnp.pad(flat, (0, rows * PACK_LANES - flat.shape[0])).reshape(rows, PACK_LANES)


def _unpack(packed, shapes):
    flat, out, o = packed.reshape(-1), [], 0
    for sh in shapes:
        n = int(np.prod(sh))
        out.append(flat[o : o + n].reshape(sh))
        o += n
    return out


def kernel(x, c, w_ada, b_ada, norm1_g, w_in, q_norm_g, k_norm_g, lam_re, lam_im, log_step, b_re, b_im, c_re, c_im, d_skip, w_glu, b_glu, attn_out_g, ssm_out_g, w_out, norm2_g, w_ff1, w_ff2, loss_target, m_w_ada, m_b_ada, m_norm1_g, m_w_in, m_q_norm_g, m_k_norm_g, m_lam_re, m_lam_im, m_log_step, m_b_re, m_b_im, m_c_re, m_c_im, m_d_skip, m_w_glu, m_b_glu, m_attn_out_g, m_ssm_out_g, m_w_out, m_norm2_g, m_w_ff1, m_w_ff2, v_w_ada, v_b_ada, v_norm1_g, v_w_in, v_q_norm_g, v_k_norm_g, v_lam_re, v_lam_im, v_log_step, v_b_re, v_b_im, v_c_re, v_c_im, v_d_skip, v_w_glu, v_b_glu, v_attn_out_g, v_ssm_out_g, v_w_out, v_norm2_g, v_w_ff1, v_w_ff2):
    args = dict(locals())
    wts = {n: args[n] for n in ORDER}
    mom = {n: args["m_" + n] for n in ORDER}
    var = {n: args["v_" + n] for n in ORDER}

    x2, tgt = x[0], loss_target[0]
    s, d = x2.shape
    aw = d // 2
    sw = d - aw
    n_groups = sw // SSM_GROUP
    nst = n_groups * STATE_DIM
    chip = 2 * lax.axis_index("x") + lax.axis_index("y")

    shards2d = {n: wts[n][0] for n in LARGE}
    cast = lambda n: _rowwise("cast_" + n, lambda a: ([a], []), [shards2d[n]], [], [(shards2d[n].shape[1], BF16)], [], tm=256)[0]
    w_in_s, w_glu_s, w_out_s, w_ff1_s, w_ff2_s = _gather_weights([cast(n) for n in LARGE])
    w_glu_f = w_glu_s.reshape(sw, sw)
    w_out_f = w_out_s.reshape(d, d)
    w_ff2_f = w_ff2_s.reshape(-1, d)
    dff = w_ff2_f.shape[0]

    c_all = _allgather8("gather_c", jnp.pad(c, ((0, SUBLANES - 1), (0, 0))))[::SUBLANES]
    n_mod = w_ada.shape[2]
    b_ada_mine = lax.dynamic_slice(b_ada, (0, chip * n_mod), (1, n_mod))
    sil_c = _single("silu_c", lambda v: [v * _sigmoid(v)], [c_all], [(c_all.shape, F32)])[0]
    mod_part = _mm("ada_fwd", sil_c, w_ada[0], "nn", F32, tn=512, epi=lambda acc, b: acc + b[0:1, :],
                   extras=[jnp.broadcast_to(b_ada_mine, (SUBLANES, n_mod))])
    mod_all = _allgather8("gather_mod", mod_part)
    me = 2 * chip + lax.axis_index("c")
    mod = jnp.concatenate([lax.dynamic_slice(mod_all, (16 * k + me, 0), (1, n_mod)) for k in range(4)], axis=1)
    sh1, sc1, g1, sh2, sc2, g2 = [mod[:, i * d : (i + 1) * d] for i in range(6)]

    def norm_mod(v, g, sc, sh):
        return [v * _rstd(v) * g * (1.0 + sc) + sh], []

    (h1b,) = _rowwise("norm1", norm_mod, [x2], [norm1_g, sc1, sh1], [(d, BF16)], [])
    proj = _mm("proj_in", h1b, w_in_s, "nn", F32, b_cs=True)
    assert aw == sw
    q_src, k_src, v_src, u_src = [(proj, aw, i) for i in range(3)] + [(proj, sw, 3)]

    def qk_prep(q, k, v, gq, gk):
        qn = jnp.concatenate([h * _rstd(h) * gq for h in _heads(q)], axis=1)
        kn = jnp.concatenate([h * _rstd(h) * gk for h in _heads(k)], axis=1)
        return [qn, kn, v], []

    qn, kn, vb = _rowwise("qk_prep", qk_prep, [q_src, k_src, v_src], [q_norm_g, k_norm_g], [(aw, BF16)] * 3, [])

    pats = []
    for window, dil in DILATION_PATTERNS:
        assert window // dil == BAND and s % (dil * BAND) == 0
        bps = s // dil // BAND
        qp, kp, vp = [_to_streams(t, dil) for t in (qn, kn, vb)]
        o_p, l_p = _attn_fwd(f"attn_fwd_d{dil}", qp, kp, vp, bps, dil)
        pats.append((dil, bps, qp, kp, vp, _from_streams(o_p, dil), _from_streams(l_p, dil)))

    def combine(o1, o2, o3, l1, l2, l3):
        m = jnp.maximum(jnp.maximum(l1, l2), l3)
        tot = m + jnp.log(jnp.exp(l1 - m) + jnp.exp(l2 - m) + jnp.exp(l3 - m))
        return [jnp.exp(l1 - tot) * o1 + jnp.exp(l2 - tot) * o2 + jnp.exp(l3 - tot) * o3, tot], []

    attn, lse = _rowwise("attn_mix", combine, [p[5] for p in pats] + [p[6] for p in pats], [], [(aw, F32)] * 2, [])

    lam_re2, lam_im2, log_step2 = lam_re[0], lam_im[0], log_step[0].reshape(n_groups, 1)
    prep = _single("ssm_prep", _ssm_prep_fn, [lam_re2, lam_im2, log_step2], [((n_groups, STATE_DIM), F32)] * 18)
    p_r = jnp.stack([p.reshape(nst) for p in prep[:8]])
    p_i = jnp.stack([p.reshape(nst) for p in prep[8:16]])
    f_ri = jnp.stack(prep[16:])
    row = jnp.arange(SUBLANES)[:, None]
    zero = jnp.zeros((SUBLANES, nst), F32)
    sel = lambda cond, v: jnp.where(cond, jnp.broadcast_to(v[None, :], (SUBLANES, nst)), zero)
    fwd_consts = jnp.stack([sel(row >= 1, p_r[0]), sel(row >= 1, p_i[0]), sel(row >= 2, p_r[1]), sel(row >= 2, p_i[1]),
                            sel(row >= 4, p_r[3]), sel(row >= 4, p_i[3]), p_r, p_i])
    bwd_consts = jnp.stack([sel(row <= 6, p_r[0]), sel(row <= 6, p_i[0]), sel(row <= 5, p_r[1]), sel(row <= 5, p_i[1]),
                            sel(row <= 3, p_r[3]), sel(row <= 3, p_i[3]), p_r[::-1], p_i[::-1]])
    nblk = n_groups // GROUPS_PER_BLOCK
    f_rows = f_ri.reshape(2, nblk, 1, 512)
    braw_r = _block_diag(b_re[0].transpose(0, 2, 1), True)
    braw_i = _block_diag(b_im[0].transpose(0, 2, 1), True)

    def bbar_fn(br, bi, f):
        r, i = _cmul(f[0], f[1], br, bi)
        return [r, i]

    bbar_r, bbar_i = _single("ssm_bbar", bbar_fn, [braw_r, braw_i, f_rows], [(braw_r.shape, BF16)] * 2)
    ct_r, ct_i = _block_diag(c_re[0], True).astype(BF16), _block_diag(c_im[0], True).astype(BF16)
    ds2 = d_skip
    y_ssm, h_r, h_i = _ssm_fwd(u_src, bbar_r, bbar_i, ct_r.transpose(0, 2, 1), ct_i.transpose(0, 2, 1), fwd_consts, ds2)

    (yb,) = _rowwise("gelu", lambda v: ([_gelu(v)], []), [y_ssm], [], [(sw, BF16)], [])
    z = _mm("glu_fwd", yb, w_glu_f, "nn", F32)

    def glu(ys, zz, b):
        return [_gelu(ys) * _sigmoid(zz + b)], []

    (ssm,) = _rowwise("glu", glu, [y_ssm, z], [b_glu], [(sw, F32)], [])

    def cat_norm(a, sm, ga, gs):
        return [jnp.concatenate([a * _rstd(a) * ga, sm * _rstd(sm) * gs], axis=1)], []

    (catb,) = _rowwise("cat_norm", cat_norm, [attn, ssm], [attn_out_g, ssm_out_g], [(d, BF16)], [])
    mixed = _mm("mix_out", catb, w_out_f, "nn", F32)

    def resid_norm(xv, mx, g1v, g, sc, sh):
        x1v = xv + g1v * mx
        return [x1v, x1v * _rstd(x1v) * g * (1.0 + sc) + sh], []

    x1, h2b = _rowwise("resid_norm2", resid_norm, [x2, mixed], [g1, norm2_g, sc2, sh2], [(d, F32), (d, BF16)], [])
    a_ff = _mm("ff1", h2b, w_ff1_s, "nn", BF16, b_cs=True)
    (rb,) = _rowwise("sq_relu", lambda v: ([jnp.square(jnp.maximum(v.astype(F32), 0.0))], []), [a_ff], [], [(dff, BF16)], [])
    ff = _mm("ff2", rb, w_ff2_f, "nn", F32)

    def loss_fn(x1v, ffv, tg, g2v):
        diff = x1v + g2v * ffv - tg
        dout = diff * (1.0 / d)
        return [dout, dout * g2v], [_colsum8(0.5 * diff * dout), _colsum8(dout * ffv)]

    dout, dffb, loss_acc, dg2_acc = _rowwise("loss", loss_fn, [x1, ff, tgt], [g2], [(d, F32), (d, BF16)], [(SUBLANES, d)] * 2)

    da_ff = _mm("ff2_bwd", dffb, w_ff2_f, "nt", BF16, epi=lambda acc, av: acc * (2.0 * jnp.maximum(av.astype(F32), 0.0)), extras=[a_ff])
    gw = {}
    gw["w_ff2"] = _mm("ff2_wgrad", rb, dffb, "tn", BF16).reshape(4, dff // 4, d)
    gw["w_ff1"] = _mm("ff1_wgrad", h2b, da_ff, "tn", BF16, o_cs=True)
    dh2 = _mm("ff1_bwd", da_ff, w_ff1_s, "nt", F32, b_cs=True)

    def norm2_bwd(dh, x1v, do, mx, g, sc, g1v):
        r = _rstd(x1v)
        xh = x1v * r
        dn = dh * (1.0 + sc)
        dx = do + _rms_bwd(dn * g, xh, r)
        return [dx, dx * g1v], [_colsum8(dh * xh * g), _colsum8(dh), _colsum8(dn * xh), _colsum8(dx * mx)]

    dx1, dmixb, dsc2_acc, dsh2_acc, dn2g_acc, dg1_acc = _rowwise(
        "norm2_bwd", norm2_bwd, [dh2, x1, dout, mixed], [norm2_g, sc2, g1], [(d, F32), (d, BF16)], [(SUBLANES, d)] * 4)

    dcat = _mm("mix_out_bwd", dmixb, w_out_f, "nt", F32)
    gw["w_out"] = _mm("mix_out_wgrad", catb, dmixb, "tn", BF16).reshape(4, d // 4, d)

    def cat_bwd(da, dsm, a, sm, ga, gs):
        ra, rs = _rstd(a), _rstd(sm)
        ah, sh = a * ra, sm * rs
        return [_rms_bwd(da * ga, ah, ra), _rms_bwd(dsm * gs, sh, rs)], [_colsum8(da * ah), _colsum8(dsm * sh)]

    dattn, dssm, dga_acc, dgs_acc = _rowwise(
        "cat_norm_bwd", cat_bwd, [(dcat, aw, 0), (dcat, sw, 1), attn, ssm], [attn_out_g, ssm_out_g], [(aw, F32), (sw, F32)],
        [(SUBLANES, aw), (SUBLANES, sw)])

    def glu_bwd(dsm, ys, zz, b):
        sg = _sigmoid(zz + b)
        dz = dsm * _gelu(ys) * sg * (1.0 - sg)
        return [dz, dsm * sg], [_colsum8(dz)]

    dzb, dy_direct, dbglu_acc = _rowwise("glu_bwd", glu_bwd, [dssm, y_ssm, z], [b_glu], [(sw, BF16), (sw, F32)], [(SUBLANES, sw)])
    gw["w_glu"] = _mm("glu_wgrad", yb, dzb, "tn", BF16).reshape(4, sw // 4, sw)
    dy_glu = _mm("glu_bwd_mm", dzb, w_glu_f, "nt", F32)
    (dys,) = _rowwise("gelu_bwd", lambda a, b, ys: ([(a + b) * _gelu_grad(ys)], []), [dy_direct, dy_glu, y_ssm], [], [(sw, F32)], [])
    du, db_bar, dc_bd, da_acc, dds_acc = _ssm_bwd(dys, u_src, h_r, h_i, ct_r, ct_i, bbar_r.transpose(0, 2, 1),
                                                  bbar_i.transpose(0, 2, 1), bwd_consts, ds2)

    def delta_fn(da, a):
        dl = jnp.concatenate([jnp.broadcast_to(jnp.sum(p * q, axis=1, keepdims=True), p.shape) for p, q in zip(_heads(da), _heads(a))], axis=1)
        return [da, dl], []

    dob, delta = _rowwise("attn_delta", delta_fn, [dattn, attn], [], [(aw, BF16), (aw, F32)], [])
    dqs, dks, dvs = [], [], []
    for dil, bps, qp, kp, vp, _, _ in pats:
        dop, lsp, dlp = [_to_streams(t, dil) for t in (dob, lse, delta)]
        dq_p, dk_p, dv_p = _attn_bwd(f"attn_bwd_d{dil}", qp, kp, vp, dop, lsp, dlp, bps, dil)
        dqs.append(_from_streams(dq_p, dil))
        dks.append(_from_streams(dk_p, dil))
        dvs.append(_from_streams(dv_p, dil))

    def qk_bwd(dq1, dq2, dq3, dk1, dk2, dk3, dv1, dv2, dv3, duv, q, k, gq, gk):
        def one(dn, raw, g):
            outs, acc = [], 0.0
            for dh, h in zip(_heads(dn), _heads(raw)):
                r = _rstd(h)
                hh = h * r
                outs.append(_rms_bwd(dh * g, hh, r))
                acc = acc + _colsum8(dh * hh)
            return jnp.concatenate(outs, axis=1), acc

        dq, gq_acc = one(dq1 + dq2 + dq3, q, gq)
        dk, gk_acc = one(dk1 + dk2 + dk3, k, gk)
        return [jnp.concatenate([dq, dk, dv1 + dv2 + dv3, duv], axis=1)], [gq_acc, gk_acc]

    dprojb, dgq_acc, dgk_acc = _rowwise("qk_norm_bwd", qk_bwd, dqs + dks + dvs + [du, q_src, k_src], [q_norm_g, k_norm_g],
                                        [(3 * aw + sw, BF16)], [(SUBLANES, HEAD_DIM)] * 2, tm=128)
    gw["w_in"] = _mm("proj_in_wgrad", h1b, dprojb, "tn", BF16, o_cs=True)
    dh1 = _mm("proj_in_bwd", dprojb, w_in_s, "nt", F32, b_cs=True)

    def norm1_bwd(dh, xv, dxr, g, sc):
        r = _rstd(xv)
        xh = xv * r
        dn = dh * (1.0 + sc)
        return [dxr + _rms_bwd(dn * g, xh, r)], [_colsum8(dh * xh * g), _colsum8(dh), _colsum8(dn * xh)]

    grad_x, dsc1_acc, dsh1_acc, dn1g_acc = _rowwise("norm1_bwd", norm1_bwd, [dh1, x2, dx1], [norm1_g, sc1], [(d, F32)], [(SUBLANES, d)] * 3)

    def b_grad_fn(dbb, br, bi, f):
        fr, fi = f[0], f[1]
        dbr, dbi = _cmul_conj(fr, fi, dbb[0], dbb[1])
        dfr = jnp.sum(dbb[0] * br + dbb[1] * bi, axis=1, keepdims=True)
        dfi = jnp.sum(dbb[1] * br - dbb[0] * bi, axis=1, keepdims=True)
        return [dbr, dbi, dfr, dfi]

    dbr_bd, dbi_bd, df_r, df_i = _single("ssm_b_grad", b_grad_fn, [db_bar, braw_r, braw_i, f_rows],
                                         [(braw_r.shape, F32)] * 2 + [((nblk, 1, 512), F32)] * 2)

    def sums_fn(*accs):
        return [jnp.sum(a, axis=-2, keepdims=True) for a in accs]

    acc_list = [dsh1_acc, dsc1_acc, dg1_acc, dsh2_acc, dsc2_acc, dg2_acc, dn1g_acc, dgq_acc, dgk_acc, dds_acc, dbglu_acc, dga_acc,
                dgs_acc, dn2g_acc, loss_acc, da_acc]
    sums = _single("small_sums", sums_fn, acc_list, [(a.shape[:-2] + (1, a.shape[-1]), F32) for a in acc_list])
    (dsh1, dsc1, dg1, dsh2, dsc2, dg2, dn1g, dgq, dgk, dds, dbglu, dga, dgs, dn2g, loss_row, da_sum) = sums
    loss_dev = _single("loss_sum", lambda v: [jnp.sum(v, axis=1, keepdims=True)], [loss_row], [((1, 1), F32)])[0]
    gp = (n_groups, STATE_DIM)
    dlam_re, dlam_im, dlog_step = _single(
        "ssm_param_grad", _ssm_param_grad_fn,
        [lam_re2, lam_im2, log_step2, da_sum[0].reshape(gp), da_sum[1].reshape(gp), df_r.reshape(gp), df_i.reshape(gp)],
        [(gp, F32), (gp, F32), ((n_groups, 1), F32)])
    small_dev = {
        "b_ada": jnp.concatenate([dsh1, dsc1, dg1, dsh2, dsc2, dg2], axis=1),
        "norm1_g": dn1g, "q_norm_g": dgq, "k_norm_g": dgk, "lam_re": dlam_re, "lam_im": dlam_im, "log_step": dlog_step,
        "b_re": _block_diag_take(dbr_bd, SSM_GROUP, STATE_DIM).transpose(0, 2, 1),
        "b_im": _block_diag_take(dbi_bd, SSM_GROUP, STATE_DIM).transpose(0, 2, 1),
        "c_re": _block_diag_take(dc_bd[0], SSM_GROUP, STATE_DIM), "c_im": _block_diag_take(dc_bd[1], SSM_GROUP, STATE_DIM),
        "d_skip": dds, "b_glu": dbglu, "attn_out_g": dga, "ssm_out_g": dgs, "norm2_g": dn2g,
    }

    packed = _pack([small_dev[n] for n in SMALL])
    rows = packed.shape[0]
    gathered = _allgather8("gather_small", packed).reshape(8, rows, PACK_LANES)

    def sum8_fn(g):
        acc = g[0]
        for i in range(1, 8):
            acc = acc + g[i]
        return [acc]

    g_small = _single("sum_small", sum8_fn, [gathered], [((rows, PACK_LANES), F32)])[0]
    w_small, m_small, v_small = [_pack([src[n] for n in SMALL]) for src in (wts, mom, var)]
    upd_small = _adamw_rows("adamw_small", w_small, g_small, m_small, v_small)
    shapes = [wts[n].shape for n in SMALL]
    out = {}
    for kind, arr in zip(("grad", "delta", "new_m", "new_v"), (g_small,) + tuple(upd_small)):
        for n, val in zip(SMALL, _unpack(arr, shapes)):
            out[kind, n] = val

    dmod_all = gathered[:, 0 : -(-6 * d // PACK_LANES), :].reshape(8, -1)[:, : 6 * d]
    dmod_mine = lax.dynamic_slice(dmod_all, (0, chip * n_mod), (8, n_mod))
    g_ada = _mm("ada_wgrad", sil_c, dmod_mine, "tn", F32, tm=512, tn=512)
    upd = _adamw_rows("adamw_w_ada", w_ada[0], g_ada, m_w_ada[0], v_w_ada[0])
    for kind, val in zip(("grad", "delta", "new_m", "new_v"), (g_ada,) + tuple(upd)):
        out[kind, "w_ada"] = val[None]

    gs = [gw[n] for n in LARGE]
    own, got = _swap_halves(gs)
    add2 = lambda i, a, b: _rowwise(f"grad_add2_{LARGE[i]}", lambda p, q: ([p.astype(F32) + q.astype(F32)], []),
                                    [a.reshape(-1, a.shape[-1]), b.reshape(-1, b.shape[-1])], [], [(a.shape[-1], BF16)], [])[0].reshape(a.shape)
    parts = [add2(i, a, b) for i, (a, b) in enumerate(zip(own, got))]
    own2, got2 = _scatter_chips(parts)

    def add4(i, a, b):
        r = a.shape[0]
        fn = lambda p, q0, q1, q2: ([((p.astype(F32) + q0.astype(F32)) + q1.astype(F32)) + q2.astype(F32)], [])
        b2 = b.reshape(3 * r, b.shape[-1])
        srcs = [a] + [(b2, b.shape[-1], 0, j * r) for j in range(3)]
        return _rowwise(f"grad_add4_{LARGE[i]}", fn, srcs, [], [(a.shape[-1], F32)], [], n_rows=r)[0]

    finals = [add4(i, a, b) for i, (a, b) in enumerate(zip(own2, got2))]
    for n, full in zip(LARGE, _join_halves(finals)):
        g = full.reshape(wts[n].shape[1:])
        upd = _adamw_rows("adamw_" + n, wts[n][0], g, mom[n][0], var[n][0])
        for kind, val in zip(("grad", "delta", "new_m", "new_v"), (g,) + tuple(upd)):
            out[kind, n] = val[None]

    loss = lax.psum(loss_dev[0, 0], ("x", "y", "c"))
    return (loss, grad_x[None], *[out[kind, n] for kind in ("grad", "delta", "new_m", "new_v") for n in ORDER])
```

```python
import functools
import math

import jax
import jax.numpy as jnp
import numpy as np
from jax import lax
from jax.experimental import pallas as pl
from jax.experimental.pallas import tpu as pltpu

F32 = jnp.float32
BF16 = jnp.bfloat16
MESH = pl.DeviceIdType.MESH

EPS = 1e-6
HEAD_DIM = 128
SSM_GROUP = 16
STATE_DIM = 64
GROUPS_PER_BLOCK = 8
DILATION_PATTERNS = ((128, 1), (512, 4), (2048, 16))
BAND = 128
NEG = -1e30

ADAM_LR, ADAM_B1, ADAM_B2, ADAM_EPS, ADAM_WD, ADAM_STEP = 0.001, 0.9, 0.999, 1e-08, 0.01, 10

V7X_VMEM_BYTES = 64 * 2**20
VMEM_LIMIT = 48 * 2**20
SUBLANES = 8
BF16_TILE_ROWS = 16
PACK_LANES = 1024
DMA_CHUNK_BYTES = 512 * 1024


def _cparams(sem=None):
    return pltpu.CompilerParams(dimension_semantics=sem, vmem_limit_bytes=VMEM_LIMIT)


def _blk(n, want):
    b = min(n, want)
    while n % b:
        b //= 2
    return b


def _colsum8(v):
    tm, w = v.shape
    return v.reshape(tm // SUBLANES, SUBLANES, w).sum(axis=0)


def _rowwise(name, fn, rows, vecs, out_rows, out_accs, tm=256, n_rows=None):
    rows = [r if isinstance(r, tuple) else (r, r.shape[1], 0) for r in rows]
    rows = [r if len(r) == 4 else r + (0,) for r in rows]
    s = rows[0][0].shape[0] if n_rows is None else n_rows
    tm = _blk(s, tm)
    assert all(r[3] % tm == 0 for r in rows)
    n_in, n_or = len(rows) + len(vecs), len(out_rows)

    def body(*refs):
        outs, accs = fn(*[r[...] for r in refs[:n_in]])
        for r, v in zip(refs[n_in : n_in + n_or], outs):
            r[...] = v.astype(r.dtype)
        if out_accs:
            acc_refs = refs[n_in + n_or :]

            @pl.when(pl.program_id(0) == 0)
            def _():
                for r in acc_refs:
                    r[...] = jnp.zeros(r.shape, r.dtype)

            for r, v in zip(acc_refs, accs):
                r[...] += v

    in_specs = [pl.BlockSpec((tm, w), lambda i, cb=cb, rb=r0 // tm: (i + rb, cb)) for (_, w, cb, r0) in rows]
    in_specs += [pl.BlockSpec(v.shape, lambda i, nd=v.ndim: (0,) * nd) for v in vecs]
    out_specs = [pl.BlockSpec((tm, w), lambda i: (i, 0)) for (w, _) in out_rows]
    out_specs += [pl.BlockSpec(sh, lambda i, nd=len(sh): (0,) * nd) for sh in out_accs]
    out_shape = [jax.ShapeDtypeStruct((s, w), dt) for (w, dt) in out_rows]
    out_shape += [jax.ShapeDtypeStruct(sh, F32) for sh in out_accs]
    res = pl.pallas_call(
        body, name=name, grid=(s // tm,), in_specs=in_specs, out_specs=out_specs, out_shape=out_shape,
        compiler_params=_cparams(("arbitrary",)),
    )(*[r[0] for r in rows], *vecs)
    return res


def _single(name, fn, ins, out_shapes):
    n_in = len(ins)

    def body(*refs):
        outs = fn(*[r[...] for r in refs[:n_in]])
        for r, v in zip(refs[n_in:], outs):
            r[...] = v.astype(r.dtype)

    vm = pl.BlockSpec(memory_space=pltpu.VMEM)
    return pl.pallas_call(
        body, name=name, in_specs=[vm] * n_in, out_specs=[vm] * len(out_shapes),
        out_shape=[jax.ShapeDtypeStruct(sh, dt) for sh, dt in out_shapes], compiler_params=_cparams(),
    )(*ins)


_NN = (((1,), (0,)), ((), ()))
_NT = (((1,), (1,)), ((), ()))
_TN = (((0,), (0,)), ((), ()))


def _mm(name, a, b, mode, out_dtype, *, tm=512, tn=1024, tk=2048, b_cs=False, o_cs=False, epi=None, extras=()):
    if mode == "tn":
        k, m = a.shape
        n = b.shape[1]
    else:
        m, k = a.shape
        if mode == "nn":
            n = b.shape[2] * 4 if b_cs else b.shape[1]
        else:
            n = b.shape[1] if b_cs else b.shape[0]
    tm = _blk(m, tm)
    tn = _blk(n // 4 if (b_cs and mode == "nn") or o_cs else n, tn)
    tk = _blk(k // 4 if (b_cs and mode == "nt") else k, tk)
    nk = k // tk
    dims = {"nn": _NN, "nt": _NT, "tn": _TN}[mode]
    n_ex = len(extras)

    def body(*refs):
        a_ref, b_ref = refs[0], refs[1]
        ex_refs, o_ref = refs[2 : 2 + n_ex], refs[2 + n_ex]

        def finish(acc):
            if epi is not None:
                acc = epi(acc, *[r[...] for r in ex_refs])
            o_ref[...] = acc.astype(o_ref.dtype)

        part = lax.dot_general(a_ref[...], b_ref[...], dims, preferred_element_type=F32)
        if nk == 1:
            finish(part)
        else:
            acc_ref = refs[3 + n_ex]
            kk = pl.program_id(2)

            @pl.when(kk == 0)
            def _():
                acc_ref[...] = part

            @pl.when(kk > 0)
            def _():
                acc_ref[...] += part

            @pl.when(kk == nk - 1)
            def _():
                finish(acc_ref[...])

    a_spec = pl.BlockSpec((tk, tm), lambda i, j, kk: (kk, i)) if mode == "tn" else pl.BlockSpec((tm, tk), lambda i, j, kk: (i, kk))
    if mode == "nn":
        if b_cs:
            per = (n // 4) // tn
            b_spec = pl.BlockSpec((None, tk, tn), lambda i, j, kk: (j // per, kk, j % per))
        else:
            b_spec = pl.BlockSpec((tk, tn), lambda i, j, kk: (kk, j))
    elif mode == "nt":
        if b_cs:
            per = (k // 4) // tk
            b_spec = pl.BlockSpec((None, tn, tk), lambda i, j, kk: (kk // per, j, kk % per))
        else:
            b_spec = pl.BlockSpec((tn, tk), lambda i, j, kk: (j, kk))
    else:
        b_spec = pl.BlockSpec((tk, tn), lambda i, j, kk: (kk, j))
    if o_cs:
        per = (n // 4) // tn
        o_spec = pl.BlockSpec((None, tm, tn), lambda i, j, kk: (j // per, i, j % per))
        o_shape = jax.ShapeDtypeStruct((4, m, n // 4), out_dtype)
    else:
        o_spec = pl.BlockSpec((tm, tn), lambda i, j, kk: (i, j))
        o_shape = jax.ShapeDtypeStruct((m, n), out_dtype)
    ex_specs = [pl.BlockSpec((tm, tn), lambda i, j, kk: (i, j)) for _ in extras]
    return pl.pallas_call(
        body, name=name, grid=(m // tm, n // tn, nk), in_specs=[a_spec, b_spec] + ex_specs, out_specs=o_spec, out_shape=o_shape,
        scratch_shapes=[pltpu.VMEM((tm, tn), F32)] if nk > 1 else [],
        compiler_params=_cparams(("parallel", "parallel", "arbitrary")),
    )(a, b, *extras)


def _place():
    x, y, c = lax.axis_index("x"), lax.axis_index("y"), lax.axis_index("c")
    return x, y, c, [(1 - x, y), (x, 1 - y), (1 - x, 1 - y)]


def _row_parts(rows, ref):
    row_bytes = ref.shape[-1] * jnp.dtype(ref.dtype).itemsize
    n = 1
    while rows % (2 * n) == 0 and (rows // (2 * n)) % BF16_TILE_ROWS == 0 and (rows // n) * row_bytes > DMA_CHUNK_BYTES:
        n *= 2
    return [(q * (rows // n), rows // n) for q in range(n)]


def _allgather8(name, v):
    m_per, n = v.shape

    def body(x_ref, out_ref, send_sems, recv_sems, local_sem):
        x, y, c, chips = _place()
        me, sibling = (x, y, c), (x, y, 1 - c)

        def rows(px, py, pc):
            return out_ref.at[pl.ds((4 * px + 2 * py + pc) * m_per, m_per), :]

        def copy(k, block, to, src=None):
            return pltpu.make_async_remote_copy(
                src_ref=rows(*block) if src is None else src, dst_ref=rows(*block), send_sem=send_sems.at[k],
                recv_sem=recv_sems.at[k], device_id=to, device_id_type=MESH)

        mine = pltpu.make_async_copy(x_ref, rows(*me), local_sem)
        mine.start()
        first = [copy(0, me, sibling, src=x_ref)]
        first += [copy(1 + j, me, (*chip, c), src=x_ref) for j, chip in enumerate(chips)]
        for cp in first:
            cp.start()
        passed = [copy(4 + j, (*chip, c), sibling) for j, chip in enumerate(chips)]
        for j, chip in enumerate(chips):
            copy(1 + j, (*chip, c), me).wait_recv()
            passed[j].start()
        copy(0, sibling, me).wait_recv()
        for j, chip in enumerate(chips):
            copy(4 + j, (*chip, 1 - c), me).wait_recv()
        for cp in first + passed:
            cp.wait_send()
        mine.wait()

    return pl.pallas_call(
        body, name=name, out_shape=jax.ShapeDtypeStruct((8 * m_per, n), v.dtype),
        in_specs=[pl.BlockSpec(memory_space=pltpu.VMEM)], out_specs=pl.BlockSpec(memory_space=pltpu.VMEM),
        scratch_shapes=[pltpu.SemaphoreType.DMA((7,)), pltpu.SemaphoreType.DMA((7,)), pltpu.SemaphoreType.DMA],
        compiler_params=_cparams(),
    )(v)


def _gather_weights(ws):
    nw = len(ws)

    def body(*refs):
        ins, outs = refs[:nw], refs[nw : 2 * nw]
        send_sems, recv_sems, local_sems = refs[2 * nw :]
        x, y, c, chips = _place()
        sibling = (x, y, 1 - c)
        pending = []
        for w in range(nw):
            half = ins[w].shape[0] // 2

            def blk(px, py, pc, r0=0, rc=half, w=w, half=half):
                return outs[w].at[2 * px + py, pl.ds(pc * half + r0, rc), :]

            def copy(k, block, to, src=None, part=(), w=w, blk=blk):
                return pltpu.make_async_remote_copy(
                    src_ref=blk(*block, *part) if src is None else src, dst_ref=blk(*block, *part), send_sem=send_sems.at[6 * w + k],
                    recv_sem=recv_sems.at[6 * w + k], device_id=to, device_id_type=MESH)

            for r0, rc in _row_parts(2 * half, ins[w]):
                pltpu.make_async_copy(ins[w].at[pl.ds(r0, rc), :], outs[w].at[2 * x + y, pl.ds(r0, rc), :], local_sems.at[w]).start()
            first = [copy(j, (x, y, c), (*chip, c), src=ins[w].at[pl.ds(c * half, half), :]) for j, chip in enumerate(chips)]
            for cp in first:
                cp.start()
            pending.append((first, copy))
        for w in range(nw):
            first, copy = pending[w]
            for j, chip in enumerate(chips):
                copy(j, (*chip, c), (x, y, c)).wait_recv()
                for part in _row_parts(ins[w].shape[0] // 2, ins[w]):
                    copy(3 + j, (*chip, c), sibling, part=part).start()
        for w in range(nw):
            first, copy = pending[w]
            for j, chip in enumerate(chips):
                copy(3 + j, (*chip, 1 - c), (x, y, c)).wait_recv()
                copy(3 + j, (*chip, c), sibling).wait_send()
            for cp in first:
                cp.wait_send()
            pltpu.make_async_copy(ins[w], outs[w].at[2 * x + y], local_sems.at[w]).wait()

    hbm = pl.BlockSpec(memory_space=pl.ANY)
    return pl.pallas_call(
        body, name="gather_weights", out_shape=[jax.ShapeDtypeStruct((4,) + w.shape, w.dtype) for w in ws],
        in_specs=[hbm] * nw, out_specs=[hbm] * nw,
        scratch_shapes=[pltpu.SemaphoreType.DMA((6 * nw,)), pltpu.SemaphoreType.DMA((6 * nw,)), pltpu.SemaphoreType.DMA((nw,))],
        compiler_params=_cparams(),
    )(*ws)


def _swap_halves(gs):
    nw = len(gs)

    def body(*refs):
        ins, owns, gots = refs[:nw], refs[nw : 2 * nw], refs[2 * nw : 3 * nw]
        send_sems, recv_sems, local_sems = refs[3 * nw :]
        x, y, c, _ = _place()
        sibling = (x, y, 1 - c)
        for w in range(nw):
            half = ins[w].shape[1] // 2
            for k in range(4):
                for r0, rc in _row_parts(half, ins[w]):
                    dst = pl.ds(r0, rc)
                    pltpu.make_async_copy(ins[w].at[k, pl.ds(c * half + r0, rc), :], owns[w].at[k, dst, :], local_sems.at[w]).start()
                    pltpu.make_async_remote_copy(
                        src_ref=ins[w].at[k, pl.ds((1 - c) * half + r0, rc), :], dst_ref=gots[w].at[k, dst, :],
                        send_sem=send_sems.at[w], recv_sem=recv_sems.at[w], device_id=sibling, device_id_type=MESH).start()
        for w in range(nw):
            pltpu.make_async_remote_copy(src_ref=gots[w], dst_ref=gots[w], send_sem=send_sems.at[w], recv_sem=recv_sems.at[w],
                                         device_id=sibling, device_id_type=MESH).wait()
            pltpu.make_async_copy(owns[w], owns[w], local_sems.at[w]).wait()

    hbm = pl.BlockSpec(memory_space=pl.ANY)
    halves = [jax.ShapeDtypeStruct((4, g.shape[1] // 2, g.shape[2]), g.dtype) for g in gs]
    res = pl.pallas_call(
        body, name="swap_halves", out_shape=halves + halves, in_specs=[hbm] * nw, out_specs=[hbm] * (2 * nw),
        scratch_shapes=[pltpu.SemaphoreType.DMA((nw,)), pltpu.SemaphoreType.DMA((nw,)), pltpu.SemaphoreType.DMA((nw,))],
        compiler_params=_cparams(),
    )(*gs)
    return res[:nw], res[nw:]


def _scatter_chips(ps):
    nw = len(ps)

    def body(*refs):
        ins, owns, gots = refs[:nw], refs[nw : 2 * nw], refs[2 * nw : 3 * nw]
        send_sems, recv_sems, local_sems = refs[3 * nw :]
        x, y, c, chips = _place()
        cps = []
        for w in range(nw):
            for r0, rc in _row_parts(owns[w].shape[0], owns[w]):
                pltpu.make_async_copy(ins[w].at[2 * x + y, pl.ds(r0, rc), :], owns[w].at[pl.ds(r0, rc), :], local_sems.at[w]).start()
            cps.append(pltpu.make_async_copy(owns[w], owns[w], local_sems.at[w]))
            for j, (px, py) in enumerate(chips):
                cp = pltpu.make_async_remote_copy(
                    src_ref=ins[w].at[2 * px + py], dst_ref=gots[w].at[j], send_sem=send_sems.at[3 * w + j],
                    recv_sem=recv_sems.at[3 * w + j], device_id=(px, py, c), device_id_type=MESH)
                cp.start()
                cps.append(cp)
        for cp in cps:
            cp.wait()

    hbm = pl.BlockSpec(memory_space=pl.ANY)
    owns = [jax.ShapeDtypeStruct(p.shape[1:], p.dtype) for p in ps]
    gots = [jax.ShapeDtypeStruct((3,) + p.shape[1:], p.dtype) for p in ps]
    res = pl.pallas_call(
        body, name="scatter_chips", out_shape=owns + gots, in_specs=[hbm] * nw, out_specs=[hbm] * (2 * nw),
        scratch_shapes=[pltpu.SemaphoreType.DMA((3 * nw,)), pltpu.SemaphoreType.DMA((3 * nw,)), pltpu.SemaphoreType.DMA((nw,))],
        compiler_params=_cparams(),
    )(*ps)
    return res[:nw], res[nw:]


def _join_halves(fs):
    nw = len(fs)

    def body(*refs):
        ins, outs = refs[:nw], refs[nw : 2 * nw]
        send_sems, recv_sems, local_sems = refs[2 * nw :]
        x, y, c, _ = _place()
        sibling = (x, y, 1 - c)

        def whole(w, slot):
            return pltpu.make_async_remote_copy(src_ref=ins[w], dst_ref=outs[w].at[slot], send_sem=send_sems.at[w],
                                                recv_sem=recv_sems.at[w], device_id=sibling, device_id_type=MESH)

        for w in range(nw):
            for r0, rc in _row_parts(ins[w].shape[0], ins[w]):
                src, dst = ins[w].at[pl.ds(r0, rc), :], outs[w].at[c, pl.ds(r0, rc), :]
                pltpu.make_async_copy(src, dst, local_sems.at[w]).start()
                pltpu.make_async_remote_copy(src_ref=src, dst_ref=dst, send_sem=send_sems.at[w], recv_sem=recv_sems.at[w],
                                             device_id=sibling, device_id_type=MESH).start()
        for w in range(nw):
            whole(w, c).wait_send()
            whole(w, 1 - c).wait_recv()
            pltpu.make_async_copy(ins[w], outs[w].at[c], local_sems.at[w]).wait()

    hbm = pl.BlockSpec(memory_space=pl.ANY)
    return pl.pallas_call(
        body, name="join_halves", out_shape=[jax.ShapeDtypeStruct((2,) + f.shape, f.dtype) for f in fs],
        in_specs=[hbm] * nw, out_specs=[hbm] * nw,
        scratch_shapes=[pltpu.SemaphoreType.DMA((nw,)), pltpu.SemaphoreType.DMA((nw,)), pltpu.SemaphoreType.DMA((nw,))],
        compiler_params=_cparams(),
    )(*fs)


def _rstd(v):
    return lax.rsqrt(jnp.mean(v * v, axis=-1, keepdims=True) + EPS)


def _rms_bwd(dy, xh, r):
    return r * (dy - xh * jnp.mean(dy * xh, axis=-1, keepdims=True))


def _heads(v):
    return [v[:, h * HEAD_DIM : (h + 1) * HEAD_DIM] for h in range(v.shape[1] // HEAD_DIM)]


def _gelu(v):
    k = math.sqrt(2.0 / math.pi)
    return 0.5 * v * (1.0 + jnp.tanh(k * (v + 0.044715 * v * v * v)))


def _gelu_grad(v):
    k = math.sqrt(2.0 / math.pi)
    t = jnp.tanh(k * (v + 0.044715 * v * v * v))
    return 0.5 * (1.0 + t) + 0.5 * v * (1.0 - t * t) * k * (1.0 + 3 * 0.044715 * v * v)


def _sigmoid(v):
    return 1.0 / (1.0 + jnp.exp(-v))


def _slopes(n_heads):
    return [2.0 ** (-8.0 * (h + 1.0) / n_heads) for h in range(n_heads)]


def _band_iotas():
    ri = lax.broadcasted_iota(jnp.int32, (BAND, BAND), 0)
    ci = lax.broadcasted_iota(jnp.int32, (BAND, BAND), 1)
    return ri, ci


def _attn_fwd(name, q, k, v, bps, dil):
    s, aw = q.shape
    n_heads, nb = aw // HEAD_DIM, s // BAND
    scale = HEAD_DIM**-0.5
    slopes = _slopes(n_heads)

    def body(q_ref, kp_ref, kc_ref, vp_ref, vc_ref, o_ref, l_ref):
        b = pl.program_id(0)
        first = (b % bps) == 0
        ri, ci = _band_iotas()
        dist_c = (ri - ci).astype(F32)
        dist_p = (ri - ci + BAND).astype(F32)
        valid_c = ci <= ri
        valid_p = ci >= ri + jnp.where(first, BAND, 0)
        hs = range(n_heads)
        sls = [slice(h * HEAD_DIM, (h + 1) * HEAD_DIM) for h in hs]
        qs = [q_ref[:, sl] for sl in sls]
        sc = [lax.dot_general(qs[h], kc_ref[:, sls[h]], _NT, preferred_element_type=F32) for h in hs]
        sp = [lax.dot_general(qs[h], kp_ref[:, sls[h]], _NT, preferred_element_type=F32) for h in hs]
        sc = [jnp.where(valid_c, sc[h] * scale - (slopes[h] * dil) * dist_c, NEG) for h in hs]
        sp = [jnp.where(valid_p, sp[h] * scale - (slopes[h] * dil) * dist_p, NEG) for h in hs]
        m = [jnp.maximum(jnp.max(sc[h], axis=1, keepdims=True), jnp.max(sp[h], axis=1, keepdims=True)) for h in hs]
        pc = [jnp.exp(sc[h] - m[h]) for h in hs]
        pp = [jnp.exp(sp[h] - m[h]) for h in hs]
        den = [jnp.sum(pc[h], axis=1, keepdims=True) + jnp.sum(pp[h], axis=1, keepdims=True) for h in hs]
        acc = [jnp.dot(pc[h].astype(BF16), vc_ref[:, sls[h]], preferred_element_type=F32) for h in hs]
        acc = [acc[h] + jnp.dot(pp[h].astype(BF16), vp_ref[:, sls[h]], preferred_element_type=F32) for h in hs]
        o_ref[...] = jnp.concatenate([acc[h] / den[h] for h in hs], axis=1)
        l_ref[...] = jnp.concatenate([jnp.broadcast_to(m[h] + jnp.log(den[h]), (BAND, HEAD_DIM)) for h in hs], axis=1)

    cur = pl.BlockSpec((BAND, aw), lambda b: (b, 0))
    prev = pl.BlockSpec((BAND, aw), lambda b: (jnp.maximum(b - 1, 0), 0))
    return pl.pallas_call(
        body, name=name, grid=(nb,), in_specs=[cur, prev, cur, prev, cur], out_specs=[cur, cur],
        out_shape=[jax.ShapeDtypeStruct((s, aw), F32)] * 2, compiler_params=_cparams(("arbitrary",)),
    )(q, k, k, v, v)


def _attn_bwd(name, q, k, v, do, lse, delta, bps, dil):
    s, aw = q.shape
    n_heads, nb = aw // HEAD_DIM, s // BAND
    scale = HEAD_DIM**-0.5
    slopes = _slopes(n_heads)

    def body(qc_ref, qn_ref, doc_ref, don_ref, lc_ref, ln_ref, dc_ref, dn_ref, k_ref, v_ref, dq_ref, dk_ref, dv_ref, carry):
        j = pl.program_id(0)

        @pl.when(j == 0)
        def _():
            carry[...] = jnp.zeros(carry.shape, F32)

        has_next = jnp.logical_and((j + 1) % bps != 0, j + 1 < nb)
        ri, ci = _band_iotas()
        dist_c = (ri - ci).astype(F32)
        dist_p = (ri - ci + BAND).astype(F32)
        valid_c = ci <= ri
        valid_n = ci >= ri + jnp.where(has_next, 0, BAND)
        carried = carry[...]
        hs = range(n_heads)
        sls = [slice(h * HEAD_DIM, (h + 1) * HEAD_DIM) for h in hs]
        ks, vs = [k_ref[:, sl] for sl in sls], [v_ref[:, sl] for sl in sls]
        sides = [(qc_ref, doc_ref, lc_ref, dc_ref, dist_c, valid_c), (qn_ref, don_ref, ln_ref, dn_ref, dist_p, valid_n)]
        res = []
        for q_ref, do_ref, l_ref, d_ref, dist, valid in sides:
            qs, dos = [q_ref[:, sl] for sl in sls], [do_ref[:, sl] for sl in sls]
            sc = [lax.dot_general(qs[h], ks[h], _NT, preferred_element_type=F32) for h in hs]
            dp = [lax.dot_general(dos[h], vs[h], _NT, preferred_element_type=F32) for h in hs]
            p = [jnp.where(valid, jnp.exp(sc[h] * scale - (slopes[h] * dil) * dist - l_ref[:, sls[h]]), 0.0) for h in hs]
            ds = [(p[h] * (dp[h] - d_ref[:, sls[h]])).astype(BF16) for h in hs]
            pb = [p[h].astype(BF16) for h in hs]
            dv = [lax.dot_general(pb[h], dos[h], _TN, preferred_element_type=F32) for h in hs]
            dk = [lax.dot_general(ds[h], qs[h], _TN, preferred_element_type=F32) for h in hs]
            dq = [jnp.dot(ds[h], ks[h], preferred_element_type=F32) for h in hs]
            res.append((dq, dk, dv))
        (dq_c, dk_c, dv_c), (dq_n, dk_n, dv_n) = res
        dq_ref[...] = carried + jnp.concatenate(dq_c, axis=1) * scale
        carry[...] = jnp.concatenate(dq_n, axis=1) * scale
        dk_ref[...] = jnp.concatenate([dk_c[h] + dk_n[h] for h in hs], axis=1) * scale
        dv_ref[...] = jnp.concatenate([dv_c[h] + dv_n[h] for h in hs], axis=1)

    cur = pl.BlockSpec((BAND, aw), lambda b: (b, 0))
    nxt = pl.BlockSpec((BAND, aw), lambda b: (jnp.minimum(b + 1, nb - 1), 0))
    return pl.pallas_call(
        body, name=name, grid=(nb,), in_specs=[cur, nxt, cur, nxt, cur, nxt, cur, nxt, cur, cur], out_specs=[cur] * 3,
        out_shape=[jax.ShapeDtypeStruct((s, aw), F32)] * 3, scratch_shapes=[pltpu.VMEM((BAND, aw), F32)],
        compiler_params=_cparams(("arbitrary",)),
    )(q, q, do, do, lse, lse, delta, delta, k, v)


def _to_streams(v, dil):
    if dil == 1:
        return v
    s, w = v.shape
    return v.reshape(s // dil, dil, w).transpose(1, 0, 2).reshape(s, w)


def _from_streams(v, dil):
    if dil == 1:
        return v
    s, w = v.shape
    return v.reshape(dil, s // dil, w).transpose(1, 0, 2).reshape(s, w)


SCAN_LANES = 1024
SSM_ROWS = 128


def _cmul(ar, ai, br, bi):
    return ar * br - ai * bi, ar * bi + ai * br


def _cmul_conj(ar, ai, br, bi):
    return ar * br + ai * bi, ar * bi - ai * br


def _ssm_fwd(u_src, bre, bim, cre, cim, consts, d_skip):
    u_arr, sw, ucb = u_src
    s = u_arr.shape[0]
    nblk = bre.shape[0]
    nst = nblk * 512
    t = _blk(s, SSM_ROWS)
    w = _blk(nst, SCAN_LANES)

    def body(u_ref, bre_ref, bim_ref, cre_ref, cim_ref, k_ref, ds_ref, y_ref, hr_ref, hi_ref, xr, xi, car_r, car_i):
        @pl.when(pl.program_id(0) == 0)
        def _():
            car_r[...] = jnp.zeros(car_r.shape, F32)
            car_i[...] = jnp.zeros(car_i.shape, F32)

        u = u_ref[...]
        ub = u.astype(BF16)
        for b in range(nblk):
            ch, st = slice(b * 128, (b + 1) * 128), slice(b * 512, (b + 1) * 512)
            xr[:, st] = jnp.dot(ub[:, ch], bre_ref[b], preferred_element_type=F32)
            xi[:, st] = jnp.dot(ub[:, ch], bim_ref[b], preferred_element_type=F32)
        for c in range(nst // w):
            ls = slice(c * w, (c + 1) * w)
            mult = [k_ref[i, :, ls] for i in range(8)]

            def step(i, carry, ls=ls, mult=mult):
                cr, ci = carry
                r0 = pl.multiple_of(i * SUBLANES, SUBLANES)
                hr, hi = xr[pl.ds(r0, SUBLANES), ls], xi[pl.ds(r0, SUBLANES), ls]
                for n, sh in enumerate((1, 2, 4)):
                    pr, pi = _cmul(mult[2 * n], mult[2 * n + 1], pltpu.roll(hr, sh, 0), pltpu.roll(hi, sh, 0))
                    hr, hi = hr + pr, hi + pi
                pr, pi = _cmul(mult[6], mult[7], cr, ci)
                hr, hi = hr + pr, hi + pi
                xr[pl.ds(r0, SUBLANES), ls] = hr
                xi[pl.ds(r0, SUBLANES), ls] = hi
                return jnp.broadcast_to(hr[7:8, :], hr.shape), jnp.broadcast_to(hi[7:8, :], hi.shape)

            cr, ci = lax.fori_loop(0, t // SUBLANES, step, (car_r[:, ls], car_i[:, ls]))
            car_r[:, ls] = cr
            car_i[:, ls] = ci
        hrb, hib = xr[...].astype(BF16), xi[...].astype(BF16)
        hr_ref[...] = hrb
        hi_ref[...] = hib
        for b in range(nblk):
            ch, st = slice(b * 128, (b + 1) * 128), slice(b * 512, (b + 1) * 512)
            yb = jnp.dot(hrb[:, st], cre_ref[b], preferred_element_type=F32)
            yb -= jnp.dot(hib[:, st], cim_ref[b], preferred_element_type=F32)
            y_ref[:, ch] = yb + ds_ref[:, ch] * u[:, ch]

    whole = lambda a: pl.BlockSpec(a.shape, lambda i, nd=a.ndim: (0,) * nd)
    return pl.pallas_call(
        body, name="ssm_fwd", grid=(s // t,),
        in_specs=[pl.BlockSpec((t, sw), lambda i: (i, ucb))] + [whole(a) for a in (bre, bim, cre, cim, consts, d_skip)],
        out_specs=[pl.BlockSpec((t, sw), lambda i: (i, 0)), pl.BlockSpec((t, nst), lambda i: (i, 0)), pl.BlockSpec((t, nst), lambda i: (i, 0))],
        out_shape=[jax.ShapeDtypeStruct((s, sw), F32), jax.ShapeDtypeStruct((s, nst), BF16), jax.ShapeDtypeStruct((s, nst), BF16)],
        scratch_shapes=[pltpu.VMEM((t, nst), F32), pltpu.VMEM((t, nst), F32), pltpu.VMEM((SUBLANES, nst), F32), pltpu.VMEM((SUBLANES, nst), F32)],
        compiler_params=_cparams(("arbitrary",)),
    )(u_arr, bre, bim, cre, cim, consts, d_skip)


def _ssm_bwd(dy, u_src, hr, hi, ctre, ctim, btre, btim, consts, d_skip):
    u_arr, sw, ucb = u_src
    s = dy.shape[0]
    nblk = ctre.shape[0]
    nst = nblk * 512
    t = _blk(s, SSM_ROWS)
    w = _blk(nst, SCAN_LANES)
    nt = s // t

    def body(dy_ref, u_ref, hr_ref, hi_ref, ctre_ref, ctim_ref, btre_ref, btim_ref, k_ref, ds_ref,
             du_ref, db_ref, dc_ref, da_ref, dd_ref, lr, li, hfr, hfi, car_r, car_i):
        @pl.when(pl.program_id(0) == 0)
        def _():
            car_r[...] = jnp.zeros(car_r.shape, F32)
            car_i[...] = jnp.zeros(car_i.shape, F32)
            db_ref[...] = jnp.zeros(db_ref.shape, F32)
            dc_ref[...] = jnp.zeros(dc_ref.shape, F32)
            da_ref[...] = jnp.zeros(da_ref.shape, F32)
            dd_ref[...] = jnp.zeros(dd_ref.shape, F32)

        dyv, u = dy_ref[...], u_ref[...]
        dyb, ub = dyv.astype(BF16), u.astype(BF16)
        hrb, hib = hr_ref[...], hi_ref[...]
        hfr[...] = hrb.astype(F32)
        hfi[...] = hib.astype(F32)
        for b in range(nblk):
            ch, st = slice(b * 128, (b + 1) * 128), slice(b * 512, (b + 1) * 512)
            lr[:, st] = jnp.dot(dyb[:, ch], ctre_ref[b], preferred_element_type=F32)
            li[:, st] = -jnp.dot(dyb[:, ch], ctim_ref[b], preferred_element_type=F32)
        last_row = lax.broadcasted_iota(jnp.int32, (SUBLANES, w), 0) == SUBLANES - 1
        for c in range(nst // w):
            ls = slice(c * w, (c + 1) * w)
            mult = [k_ref[i, :, ls] for i in range(8)]

            def step(i, carry, ls=ls, mult=mult):
                cr, ci, ar, ai = carry
                r0 = pl.multiple_of((t // SUBLANES - 1 - i) * SUBLANES, SUBLANES)
                gr, gi = lr[pl.ds(r0, SUBLANES), ls], li[pl.ds(r0, SUBLANES), ls]
                for n, sh in enumerate((1, 2, 4)):
                    pr, pi = _cmul_conj(mult[2 * n], mult[2 * n + 1], pltpu.roll(gr, SUBLANES - sh, 0), pltpu.roll(gi, SUBLANES - sh, 0))
                    gr, gi = gr + pr, gi + pi
                pr, pi = _cmul_conj(mult[6], mult[7], cr, ci)
                gr, gi = gr + pr, gi + pi
                lr[pl.ds(r0, SUBLANES), ls] = gr
                li[pl.ds(r0, SUBLANES), ls] = gi
                nr = jnp.where(last_row, cr, pltpu.roll(gr, SUBLANES - 1, 0))
                ni = jnp.where(last_row, ci, pltpu.roll(gi, SUBLANES - 1, 0))
                fr, fi = hfr[pl.ds(r0, SUBLANES), ls], hfi[pl.ds(r0, SUBLANES), ls]
                pr, pi = _cmul_conj(fr, fi, nr, ni)
                return (jnp.broadcast_to(gr[0:1, :], gr.shape), jnp.broadcast_to(gi[0:1, :], gi.shape), ar + pr, ai + pi)

            zero = jnp.zeros((SUBLANES, w), F32)
            cr, ci, ar, ai = lax.fori_loop(0, t // SUBLANES, step, (car_r[:, ls], car_i[:, ls], zero, zero))
            car_r[:, ls] = cr
            car_i[:, ls] = ci
            da_ref[0, :, ls] += ar
            da_ref[1, :, ls] += ai
        lrb, lib = lr[...].astype(BF16), li[...].astype(BF16)
        for b in range(nblk):
            ch, st = slice(b * 128, (b + 1) * 128), slice(b * 512, (b + 1) * 512)
            dub = jnp.dot(lrb[:, st], btre_ref[b], preferred_element_type=F32)
            dub += jnp.dot(lib[:, st], btim_ref[b], preferred_element_type=F32)
            du_ref[:, ch] = dub + ds_ref[:, ch] * dyv[:, ch]
            db_ref[0, b] += lax.dot_general(ub[:, ch], lrb[:, st], _TN, preferred_element_type=F32)
            db_ref[1, b] += lax.dot_general(ub[:, ch], lib[:, st], _TN, preferred_element_type=F32)
            dc_ref[0, b] += lax.dot_general(dyb[:, ch], hrb[:, st], _TN, preferred_element_type=F32)
            dc_ref[1, b] -= lax.dot_general(dyb[:, ch], hib[:, st], _TN, preferred_element_type=F32)
        dd_ref[...] += _colsum8(dyv * u)

    whole = lambda a: pl.BlockSpec(a.shape, lambda i, nd=a.ndim: (0,) * nd)
    rev = lambda wd, cb=0: pl.BlockSpec((t, wd), lambda i, cb=cb: (nt - 1 - i, cb))
    acc = lambda sh: pl.BlockSpec(sh, lambda i, nd=len(sh): (0,) * nd)
    db_shape, da_shape, dd_shape = (2, nblk, 128, 512), (2, SUBLANES, nst), (SUBLANES, sw)
    return pl.pallas_call(
        body, name="ssm_bwd", grid=(nt,),
        in_specs=[rev(sw), rev(sw, ucb), rev(nst), rev(nst)] + [whole(a) for a in (ctre, ctim, btre, btim, consts, d_skip)],
        out_specs=[rev(sw), acc(db_shape), acc(db_shape), acc(da_shape), acc(dd_shape)],
        out_shape=[jax.ShapeDtypeStruct((s, sw), F32), jax.ShapeDtypeStruct(db_shape, F32), jax.ShapeDtypeStruct(db_shape, F32),
                   jax.ShapeDtypeStruct(da_shape, F32), jax.ShapeDtypeStruct(dd_shape, F32)],
        scratch_shapes=[pltpu.VMEM((t, nst), F32)] * 4 + [pltpu.VMEM((SUBLANES, nst), F32)] * 2,
        compiler_params=_cparams(("arbitrary",)),
    )(dy, u_arr, hr, hi, ctre, ctim, btre, btim, consts, d_skip)


def _ssm_discretise(lam_re, lam_im, log_step):
    step = jnp.exp(log_step)
    e = jnp.exp(lam_re * step)
    ar, ai = e * jnp.cos(lam_im * step), e * jnp.sin(lam_im * step)
    den = lam_re * lam_re + lam_im * lam_im
    inv_r, inv_i = lam_re / den, -lam_im / den
    fr, fi = _cmul(ar - 1.0, ai, inv_r, inv_i)
    return step, ar, ai, inv_r, inv_i, fr, fi


def _ssm_prep_fn(lam_re, lam_im, log_step):
    _, ar, ai, _, _, fr, fi = _ssm_discretise(lam_re, lam_im, log_step)
    pw = [(ar, ai)]
    for n in range(1, 8):
        pw.append(_cmul(*pw[n - 1], ar, ai))
    return [p[0] for p in pw] + [p[1] for p in pw] + [fr, fi]


def _ssm_param_grad_fn(lam_re, lam_im, log_step, da_r, da_i, df_r, df_i):
    step, ar, ai, inv_r, inv_i, fr, fi = _ssm_discretise(lam_re, lam_im, log_step)
    pr, pi = _cmul_conj(inv_r, inv_i, df_r, df_i)
    dat_r, dat_i = da_r + pr, da_i + pi
    wr, wi = _cmul(fr, fi, inv_r, inv_i)
    dl_r, dl_i = _cmul_conj(-wr, -wi, df_r, df_i)
    dz_r, dz_i = _cmul_conj(ar, ai, dat_r, dat_i)
    dl_r, dl_i = dl_r + step * dz_r, dl_i + step * dz_i
    dstep = jnp.sum(dz_r * lam_re + dz_i * lam_im, axis=-1, keepdims=True)
    return [dl_r, dl_i, step * dstep]


def _block_diag(v, rows_first):
    g, a, b = v.shape
    nb = g // GROUPS_PER_BLOCK
    eye = jnp.eye(GROUPS_PER_BLOCK, dtype=v.dtype)
    v = v.reshape(nb, GROUPS_PER_BLOCK, a, 1, b) * eye[None, :, None, :, None]
    return v.reshape(nb, GROUPS_PER_BLOCK * a, GROUPS_PER_BLOCK * b)


def _block_diag_take(v, a, b):
    nb = v.shape[0]
    v = v.reshape(nb, GROUPS_PER_BLOCK, a, GROUPS_PER_BLOCK, b)
    idx = jnp.arange(GROUPS_PER_BLOCK)
    return v[:, idx, :, idx, :].transpose(1, 0, 2, 3).reshape(nb * GROUPS_PER_BLOCK, a, b)


def _adamw(w, g, m, v):
    m = ADAM_B1 * m + (1.0 - ADAM_B1) * g
    v = ADAM_B2 * v + (1.0 - ADAM_B2) * (g * g)
    m_hat = m / (1.0 - ADAM_B1**ADAM_STEP)
    v_hat = v / (1.0 - ADAM_B2**ADAM_STEP)
    delta = -ADAM_LR * (m_hat / (jnp.sqrt(v_hat) + ADAM_EPS) + ADAM_WD * w)
    return delta, m, v


def _adamw_rows(name, w, g, m, v):
    wd = w.shape[1]
    return _rowwise(name, lambda a, b, c, d: (list(_adamw(a, b, c, d)), []), [w, g, m, v], [], [(wd, F32)] * 3, [], tm=128)


SMALL = ["b_ada", "norm1_g", "q_norm_g", "k_norm_g", "lam_re", "lam_im", "log_step", "b_re", "b_im", "c_re", "c_im",
         "d_skip", "b_glu", "attn_out_g", "ssm_out_g", "norm2_g"]
LARGE = ["w_in", "w_glu", "w_out", "w_ff1", "w_ff2"]
ORDER = ["w_ada", "b_ada", "norm1_g", "w_in", "q_norm_g", "k_norm_g", "lam_re", "lam_im", "log_step", "b_re", "b_im", "c_re",
         "c_im", "d_skip", "w_glu", "b_glu", "attn_out_g", "ssm_out_g", "w_out", "norm2_g", "w_ff1", "w_ff2"]


def _pack(arrs):
    flat = jnp.concatenate([a.reshape(-1) for a in arrs])
    rows = -(-flat.shape[0] // PACK_LANES)
    rows = -(-rows // SUBLANES) * SUBLANES
    return jnp.pad(flat, (0, rows * PACK_LANES - flat.shape[0])).reshape(rows, PACK_LANES)


def _unpack(packed, shapes):
    flat, out, o = packed.reshape(-1), [], 0
    for sh in shapes:
        n = int(np.prod(sh))
        out.append(flat[o : o + n].reshape(sh))
        o += n
    return out


def kernel(x, c, w_ada, b_ada, norm1_g, w_in, q_norm_g, k_norm_g, lam_re, lam_im, log_step, b_re, b_im, c_re, c_im, d_skip, w_glu, b_glu, attn_out_g, ssm_out_g, w_out, norm2_g, w_ff1, w_ff2, loss_target, m_w_ada, m_b_ada, m_norm1_g, m_w_in, m_q_norm_g, m_k_norm_g, m_lam_re, m_lam_im, m_log_step, m_b_re, m_b_im, m_c_re, m_c_im, m_d_skip, m_w_glu, m_b_glu, m_attn_out_g, m_ssm_out_g, m_w_out, m_norm2_g, m_w_ff1, m_w_ff2, v_w_ada, v_b_ada, v_norm1_g, v_w_in, v_q_norm_g, v_k_norm_g, v_lam_re, v_lam_im, v_log_step, v_b_re, v_b_im, v_c_re, v_c_im, v_d_skip, v_w_glu, v_b_glu, v_attn_out_g, v_ssm_out_g, v_w_out, v_norm2_g, v_w_ff1, v_w_ff2):
    args = dict(locals())
    wts = {n: args[n] for n in ORDER}
    mom = {n: args["m_" + n] for n in ORDER}
    var = {n: args["v_" + n] for n in ORDER}

    x2, tgt = x[0], loss_target[0]
    s, d = x2.shape
    aw = d // 2
    sw = d - aw
    n_groups = sw // SSM_GROUP
    nst = n_groups * STATE_DIM
    chip = 2 * lax.axis_index("x") + lax.axis_index("y")

    shards2d = {n: wts[n][0] for n in LARGE}
    cast = lambda n: _rowwise("cast_" + n, lambda a: ([a], []), [shards2d[n]], [], [(shards2d[n].shape[1], BF16)], [], tm=256)[0]
    w_in_s, w_glu_s, w_out_s, w_ff1_s, w_ff2_s = _gather_weights([cast(n) for n in LARGE])
    w_glu_f = w_glu_s.reshape(sw, sw)
    w_out_f = w_out_s.reshape(d, d)
    w_ff2_f = w_ff2_s.reshape(-1, d)
    dff = w_ff2_f.shape[0]

    c_all = _allgather8("gather_c", jnp.pad(c, ((0, SUBLANES - 1), (0, 0))))[::SUBLANES]
    n_mod = w_ada.shape[2]
    b_ada_mine = lax.dynamic_slice(b_ada, (0, chip * n_mod), (1, n_mod))
    sil_c = _single("silu_c", lambda v: [v * _sigmoid(v)], [c_all], [(c_all.shape, F32)])[0]
    mod_part = _mm("ada_fwd", sil_c, w_ada[0], "nn", F32, tn=512, epi=lambda acc, b: acc + b[0:1, :],
                   extras=[jnp.broadcast_to(b_ada_mine, (SUBLANES, n_mod))])
    mod_all = _allgather8("gather_mod", mod_part)
    me = 2 * chip + lax.axis_index("c")
    mod = jnp.concatenate([lax.dynamic_slice(mod_all, (16 * k + me, 0), (1, n_mod)) for k in range(4)], axis=1)
    sh1, sc1, g1, sh2, sc2, g2 = [mod[:, i * d : (i + 1) * d] for i in range(6)]

    def norm_mod(v, g, sc, sh):
        return [v * _rstd(v) * g * (1.0 + sc) + sh], []

    (h1b,) = _rowwise("norm1", norm_mod, [x2], [norm1_g, sc1, sh1], [(d, BF16)], [])
    proj = _mm("proj_in", h1b, w_in_s, "nn", F32, b_cs=True)
    assert aw == sw
    q_src, k_src, v_src, u_src = [(proj, aw, i) for i in range(3)] + [(proj, sw, 3)]

    def qk_prep(q, k, v, gq, gk):
        qn = jnp.concatenate([h * _rstd(h) * gq for h in _heads(q)], axis=1)
        kn = jnp.concatenate([h * _rstd(h) * gk for h in _heads(k)], axis=1)
        return [qn, kn, v], []

    qn, kn, vb = _rowwise("qk_prep", qk_prep, [q_src, k_src, v_src], [q_norm_g, k_norm_g], [(aw, BF16)] * 3, [])

    pats = []
    for window, dil in DILATION_PATTERNS:
        assert window // dil == BAND and s % (dil * BAND) == 0
        bps = s // dil // BAND
        qp, kp, vp = [_to_streams(t, dil) for t in (qn, kn, vb)]
        o_p, l_p = _attn_fwd(f"attn_fwd_d{dil}", qp, kp, vp, bps, dil)
        pats.append((dil, bps, qp, kp, vp, _from_streams(o_p, dil), _from_streams(l_p, dil)))

    def combine(o1, o2, o3, l1, l2, l3):
        m = jnp.maximum(jnp.maximum(l1, l2), l3)
        tot = m + jnp.log(jnp.exp(l1 - m) + jnp.exp(l2 - m) + jnp.exp(l3 - m))
        return [jnp.exp(l1 - tot) * o1 + jnp.exp(l2 - tot) * o2 + jnp.exp(l3 - tot) * o3, tot], []

    attn, lse = _rowwise("attn_mix", combine, [p[5] for p in pats] + [p[6] for p in pats], [], [(aw, F32)] * 2, [])

    lam_re2, lam_im2, log_step2 = lam_re[0], lam_im[0], log_step[0].reshape(n_groups, 1)
    prep = _single("ssm_prep", _ssm_prep_fn, [lam_re2, lam_im2, log_step2], [((n_groups, STATE_DIM), F32)] * 18)
    p_r = jnp.stack([p.reshape(nst) for p in prep[:8]])
    p_i = jnp.stack([p.reshape(nst) for p in prep[8:16]])
    f_ri = jnp.stack(prep[16:])
    row = jnp.arange(SUBLANES)[:, None]
    zero = jnp.zeros((SUBLANES, nst), F32)
    sel = lambda cond, v: jnp.where(cond, jnp.broadcast_to(v[None, :], (SUBLANES, nst)), zero)
    fwd_consts = jnp.stack([sel(row >= 1, p_r[0]), sel(row >= 1, p_i[0]), sel(row >= 2, p_r[1]), sel(row >= 2, p_i[1]),
                            sel(row >= 4, p_r[3]), sel(row >= 4, p_i[3]), p_r, p_i])
    bwd_consts = jnp.stack([sel(row <= 6, p_r[0]), sel(row <= 6, p_i[0]), sel(row <= 5, p_r[1]), sel(row <= 5, p_i[1]),
                            sel(row <= 3, p_r[3]), sel(row <= 3, p_i[3]), p_r[::-1], p_i[::-1]])
    nblk = n_groups // GROUPS_PER_BLOCK
    f_rows = f_ri.reshape(2, nblk, 1, 512)
    braw_r = _block_diag(b_re[0].transpose(0, 2, 1), True)
    braw_i = _block_diag(b_im[0].transpose(0, 2, 1), True)

    def bbar_fn(br, bi, f):
        r, i = _cmul(f[0], f[1], br, bi)
        return [r, i]

    bbar_r, bbar_i = _single("ssm_bbar", bbar_fn, [braw_r, braw_i, f_rows], [(braw_r.shape, BF16)] * 2)
    ct_r, ct_i = _block_diag(c_re[0], True).astype(BF16), _block_diag(c_im[0], True).astype(BF16)
    ds2 = d_skip
    y_ssm, h_r, h_i = _ssm_fwd(u_src, bbar_r, bbar_i, ct_r.transpose(0, 2, 1), ct_i.transpose(0, 2, 1), fwd_consts, ds2)

    (yb,) = _rowwise("gelu", lambda v: ([_gelu(v)], []), [y_ssm], [], [(sw, BF16)], [])
    z = _mm("glu_fwd", yb, w_glu_f, "nn", F32)

    def glu(ys, zz, b):
        return [_gelu(ys) * _sigmoid(zz + b)], []

    (ssm,) = _rowwise("glu", glu, [y_ssm, z], [b_glu], [(sw, F32)], [])

    def cat_norm(a, sm, ga, gs):
        return [jnp.concatenate([a * _rstd(a) * ga, sm * _rstd(sm) * gs], axis=1)], []

    (catb,) = _rowwise("cat_norm", cat_norm, [attn, ssm], [attn_out_g, ssm_out_g], [(d, BF16)], [])
    mixed = _mm("mix_out", catb, w_out_f, "nn", F32)

    def resid_norm(xv, mx, g1v, g, sc, sh):
        x1v = xv + g1v * mx
        return [x1v, x1v * _rstd(x1v) * g * (1.0 + sc) + sh], []

    x1, h2b = _rowwise("resid_norm2", resid_norm, [x2, mixed], [g1, norm2_g, sc2, sh2], [(d, F32), (d, BF16)], [])
    a_ff = _mm("ff1", h2b, w_ff1_s, "nn", BF16, b_cs=True)
    (rb,) = _rowwise("sq_relu", lambda v: ([jnp.square(jnp.maximum(v.astype(F32), 0.0))], []), [a_ff], [], [(dff, BF16)], [])
    ff = _mm("ff2", rb, w_ff2_f, "nn", F32)

    def loss_fn(x1v, ffv, tg, g2v):
        diff = x1v + g2v * ffv - tg
        dout = diff * (1.0 / d)
        return [dout, dout * g2v], [_colsum8(0.5 * diff * dout), _colsum8(dout * ffv)]

    dout, dffb, loss_acc, dg2_acc = _rowwise("loss", loss_fn, [x1, ff, tgt], [g2], [(d, F32), (d, BF16)], [(SUBLANES, d)] * 2)

    da_ff = _mm("ff2_bwd", dffb, w_ff2_f, "nt", BF16, epi=lambda acc, av: acc * (2.0 * jnp.maximum(av.astype(F32), 0.0)), extras=[a_ff])
    gw = {}
    gw["w_ff2"] = _mm("ff2_wgrad", rb, dffb, "tn", BF16).reshape(4, dff // 4, d)
    gw["w_ff1"] = _mm("ff1_wgrad", h2b, da_ff, "tn", BF16, o_cs=True)
    dh2 = _mm("ff1_bwd", da_ff, w_ff1_s, "nt", F32, b_cs=True)

    def norm2_bwd(dh, x1v, do, mx, g, sc, g1v):
        r = _rstd(x1v)
        xh = x1v * r
        dn = dh * (1.0 + sc)
        dx = do + _rms_bwd(dn * g, xh, r)
        return [dx, dx * g1v], [_colsum8(dh * xh * g), _colsum8(dh), _colsum8(dn * xh), _colsum8(dx * mx)]

    dx1, dmixb, dsc2_acc, dsh2_acc, dn2g_acc, dg1_acc = _rowwise(
        "norm2_bwd", norm2_bwd, [dh2, x1, dout, mixed], [norm2_g, sc2, g1], [(d, F32), (d, BF16)], [(SUBLANES, d)] * 4)

    dcat = _mm("mix_out_bwd", dmixb, w_out_f, "nt", F32)
    gw["w_out"] = _mm("mix_out_wgrad", catb, dmixb, "tn", BF16).reshape(4, d // 4, d)

    def cat_bwd(da, dsm, a, sm, ga, gs):
        ra, rs = _rstd(a), _rstd(sm)
        ah, sh = a * ra, sm * rs
        return [_rms_bwd(da * ga, ah, ra), _rms_bwd(dsm * gs, sh, rs)], [_colsum8(da * ah), _colsum8(dsm * sh)]

    dattn, dssm, dga_acc, dgs_acc = _rowwise(
        "cat_norm_bwd", cat_bwd, [(dcat, aw, 0), (dcat, sw, 1), attn, ssm], [attn_out_g, ssm_out_g], [(aw, F32), (sw, F32)],
        [(SUBLANES, aw), (SUBLANES, sw)])

    def glu_bwd(dsm, ys, zz, b):
        sg = _sigmoid(zz + b)
        dz = dsm * _gelu(ys) * sg * (1.0 - sg)
        return [dz, dsm * sg], [_colsum8(dz)]

    dzb, dy_direct, dbglu_acc = _rowwise("glu_bwd", glu_bwd, [dssm, y_ssm, z], [b_glu], [(sw, BF16), (sw, F32)], [(SUBLANES, sw)])
    gw["w_glu"] = _mm("glu_wgrad", yb, dzb, "tn", BF16).reshape(4, sw // 4, sw)
    dy_glu = _mm("glu_bwd_mm", dzb, w_glu_f, "nt", F32)
    (dys,) = _rowwise("gelu_bwd", lambda a, b, ys: ([(a + b) * _gelu_grad(ys)], []), [dy_direct, dy_glu, y_ssm], [], [(sw, F32)], [])
    du, db_bar, dc_bd, da_acc, dds_acc = _ssm_bwd(dys, u_src, h_r, h_i, ct_r, ct_i, bbar_r.transpose(0, 2, 1),
                                                  bbar_i.transpose(0, 2, 1), bwd_consts, ds2)

    def delta_fn(da, a):
        dl = jnp.concatenate([jnp.broadcast_to(jnp.sum(p * q, axis=1, keepdims=True), p.shape) for p, q in zip(_heads(da), _heads(a))], axis=1)
        return [da, dl], []

    dob, delta = _rowwise("attn_delta", delta_fn, [dattn, attn], [], [(aw, BF16), (aw, F32)], [])
    dqs, dks, dvs = [], [], []
    for dil, bps, qp, kp, vp, _, _ in pats:
        dop, lsp, dlp = [_to_streams(t, dil) for t in (dob, lse, delta)]
        dq_p, dk_p, dv_p = _attn_bwd(f"attn_bwd_d{dil}", qp, kp, vp, dop, lsp, dlp, bps, dil)
        dqs.append(_from_streams(dq_p, dil))
        dks.append(_from_streams(dk_p, dil))
        dvs.append(_from_streams(dv_p, dil))

    def qk_bwd(dq1, dq2, dq3, dk1, dk2, dk3, dv1, dv2, dv3, duv, q, k, gq, gk):
        def one(dn, raw, g):
            outs, acc = [], 0.0
            for dh, h in zip(_heads(dn), _heads(raw)):
                r = _rstd(h)
                hh = h * r
                outs.append(_rms_bwd(dh * g, hh, r))
                acc = acc + _colsum8(dh * hh)
            return jnp.concatenate(outs, axis=1), acc

        dq, gq_acc = one(dq1 + dq2 + dq3, q, gq)
        dk, gk_acc = one(dk1 + dk2 + dk3, k, gk)
        return [jnp.concatenate([dq, dk, dv1 + dv2 + dv3, duv], axis=1)], [gq_acc, gk_acc]

    dprojb, dgq_acc, dgk_acc = _rowwise("qk_norm_bwd", qk_bwd, dqs + dks + dvs + [du, q_src, k_src], [q_norm_g, k_norm_g],
                                        [(3 * aw + sw, BF16)], [(SUBLANES, HEAD_DIM)] * 2, tm=128)
    gw["w_in"] = _mm("proj_in_wgrad", h1b, dprojb, "tn", BF16, o_cs=True)
    dh1 = _mm("proj_in_bwd", dprojb, w_in_s, "nt", F32, b_cs=True)

    def norm1_bwd(dh, xv, dxr, g, sc):
        r = _rstd(xv)
        xh = xv * r
        dn = dh * (1.0 + sc)
        return [dxr + _rms_bwd(dn * g, xh, r)], [_colsum8(dh * xh * g), _colsum8(dh), _colsum8(dn * xh)]

    grad_x, dsc1_acc, dsh1_acc, dn1g_acc = _rowwise("norm1_bwd", norm1_bwd, [dh1, x2, dx1], [norm1_g, sc1], [(d, F32)], [(SUBLANES, d)] * 3)

    def b_grad_fn(dbb, br, bi, f):
        fr, fi = f[0], f[1]
        dbr, dbi = _cmul_conj(fr, fi, dbb[0], dbb[1])
        dfr = jnp.sum(dbb[0] * br + dbb[1] * bi, axis=1, keepdims=True)
        dfi = jnp.sum(dbb[1] * br - dbb[0] * bi, axis=1, keepdims=True)
        return [dbr, dbi, dfr, dfi]

    dbr_bd, dbi_bd, df_r, df_i = _single("ssm_b_grad", b_grad_fn, [db_bar, braw_r, braw_i, f_rows],
                                         [(braw_r.shape, F32)] * 2 + [((nblk, 1, 512), F32)] * 2)

    def sums_fn(*accs):
        return [jnp.sum(a, axis=-2, keepdims=True) for a in accs]

    acc_list = [dsh1_acc, dsc1_acc, dg1_acc, dsh2_acc, dsc2_acc, dg2_acc, dn1g_acc, dgq_acc, dgk_acc, dds_acc, dbglu_acc, dga_acc,
                dgs_acc, dn2g_acc, loss_acc, da_acc]
    sums = _single("small_sums", sums_fn, acc_list, [(a.shape[:-2] + (1, a.shape[-1]), F32) for a in acc_list])
    (dsh1, dsc1, dg1, dsh2, dsc2, dg2, dn1g, dgq, dgk, dds, dbglu, dga, dgs, dn2g, loss_row, da_sum) = sums
    loss_dev = _single("loss_sum", lambda v: [jnp.sum(v, axis=1, keepdims=True)], [loss_row], [((1, 1), F32)])[0]
    gp = (n_groups, STATE_DIM)
    dlam_re, dlam_im, dlog_step = _single(
        "ssm_param_grad", _ssm_param_grad_fn,
        [lam_re2, lam_im2, log_step2, da_sum[0].reshape(gp), da_sum[1].reshape(gp), df_r.reshape(gp), df_i.reshape(gp)],
        [(gp, F32), (gp, F32), ((n_groups, 1), F32)])
    small_dev = {
        "b_ada": jnp.concatenate([dsh1, dsc1, dg1, dsh2, dsc2, dg2], axis=1),
        "norm1_g": dn1g, "q_norm_g": dgq, "k_norm_g": dgk, "lam_re": dlam_re, "lam_im": dlam_im, "log_step": dlog_step,
        "b_re": _block_diag_take(dbr_bd, SSM_GROUP, STATE_DIM).transpose(0, 2, 1),
        "b_im": _block_diag_take(dbi_bd, SSM_GROUP, STATE_DIM).transpose(0, 2, 1),
        "c_re": _block_diag_take(dc_bd[0], SSM_GROUP, STATE_DIM), "c_im": _block_diag_take(dc_bd[1], SSM_GROUP, STATE_DIM),
        "d_skip": dds, "b_glu": dbglu, "attn_out_g": dga, "ssm_out_g": dgs, "norm2_g": dn2g,
    }

    packed = _pack([small_dev[n] for n in SMALL])
    rows = packed.shape[0]
    gathered = _allgather8("gather_small", packed).reshape(8, rows, PACK_LANES)

    def sum8_fn(g):
        acc = g[0]
        for i in range(1, 8):
            acc = acc + g[i]
        return [acc]

    g_small = _single("sum_small", sum8_fn, [gathered], [((rows, PACK_LANES), F32)])[0]
    w_small, m_small, v_small = [_pack([src[n] for n in SMALL]) for src in (wts, mom, var)]
    upd_small = _adamw_rows("adamw_small", w_small, g_small, m_small, v_small)
    shapes = [wts[n].shape for n in SMALL]
    out = {}
    for kind, arr in zip(("grad", "delta", "new_m", "new_v"), (g_small,) + tuple(upd_small)):
        for n, val in zip(SMALL, _unpack(arr, shapes)):
            out[kind, n] = val

    dmod_all = gathered[:, 0 : -(-6 * d // PACK_LANES), :].reshape(8, -1)[:, : 6 * d]
    dmod_mine = lax.dynamic_slice(dmod_all, (0, chip * n_mod), (8, n_mod))
    g_ada = _mm("ada_wgrad", sil_c, dmod_mine, "tn", F32, tm=512, tn=512)
    upd = _adamw_rows("adamw_w_ada", w_ada[0], g_ada, m_w_ada[0], v_w_ada[0])
    for kind, val in zip(("grad", "delta", "new_m", "new_v"), (g_ada,) + tuple(upd)):
        out[kind, "w_ada"] = val[None]

    gs = [gw[n] for n in LARGE]
    own, got = _swap_halves(gs)
    add2 = lambda i, a, b: _rowwise(f"grad_add2_{LARGE[i]}", lambda p, q: ([p.astype(F32) + q.astype(F32)], []),
                                    [a.reshape(-1, a.shape[-1]), b.reshape(-1, b.shape[-1])], [], [(a.shape[-1], BF16)], [])[0].reshape(a.shape)
    parts = [add2(i, a, b) for i, (a, b) in enumerate(zip(own, got))]
    own2, got2 = _scatter_chips(parts)

    def add4(i, a, b):
        r = a.shape[0]
        fn = lambda p, q0, q1, q2: ([((p.astype(F32) + q0.astype(F32)) + q1.astype(F32)) + q2.astype(F32)], [])
        b2 = b.reshape(3 * r, b.shape[-1])
        srcs = [a] + [(b2, b.shape[-1], 0, j * r) for j in range(3)]
        return _rowwise(f"grad_add4_{LARGE[i]}", fn, srcs, [], [(a.shape[-1], F32)], [], n_rows=r)[0]

    finals = [add4(i, a, b) for i, (a, b) in enumerate(zip(own2, got2))]
    for n, full in zip(LARGE, _join_halves(finals)):
        g = full.reshape(wts[n].shape[1:])
        upd = _adamw_rows("adamw_" + n, wts[n][0], g, mom[n][0], var[n][0])
        for kind, val in zip(("grad", "delta", "new_m", "new_v"), (g,) + tuple(upd)):
            out[kind, n] = val[None]

    loss = lax.psum(loss_dev[0, 0], ("x", "y", "c"))
    return (loss, grad_x[None], *[out[kind, n] for kind in ("grad", "delta", "new_m", "new_v") for n in ORDER])
```

```python
import functools
import math

import jax
import jax.numpy as jnp
import numpy as np
from jax import lax
from jax.experimental import pallas as pl
from jax.experimental.pallas import tpu as pltpu

F32 = jnp.float32
BF16 = jnp.bfloat16
MESH = pl.DeviceIdType.MESH

EPS = 1e-6
HEAD_DIM = 128
SSM_GROUP = 16
STATE_DIM = 64
GROUPS_PER_BLOCK = 8
DILATION_PATTERNS = ((128, 1), (512, 4), (2048, 16))
BAND = 128
NEG = -1e30

ADAM_LR, ADAM_B1, ADAM_B2, ADAM_EPS, ADAM_WD, ADAM_STEP = 0.001, 0.9, 0.999, 1e-08, 0.01, 10

V7X_VMEM_BYTES = 64 * 2**20
VMEM_LIMIT = 48 * 2**20
JOIN_VMEM_LIMIT = 56 * 2**20
SUBLANES = 8
PACK_LANES = 1024


def _cparams(sem=None):
    return pltpu.CompilerParams(dimension_semantics=sem, vmem_limit_bytes=VMEM_LIMIT)


def _blk(n, want):
    b = min(n, want)
    while n % b:
        b //= 2
    return b


def _colsum8(v):
    tm, w = v.shape
    return v.reshape(tm // SUBLANES, SUBLANES, w).sum(axis=0)


def _rowwise(name, fn, rows, vecs, out_rows, out_accs, tm=256, n_rows=None):
    rows = [r if isinstance(r, tuple) else (r, r.shape[1], 0) for r in rows]
    rows = [r if len(r) == 4 else r + (0,) for r in rows]
    s = rows[0][0].shape[0] if n_rows is None else n_rows
    tm = _blk(s, tm)
    assert all(r[3] % tm == 0 for r in rows)
    n_in, n_or = len(rows) + len(vecs), len(out_rows)

    def body(*refs):
        outs, accs = fn(*[r[...] for r in refs[:n_in]])
        for r, v in zip(refs[n_in : n_in + n_or], outs):
            r[...] = v.astype(r.dtype)
        if out_accs:
            acc_refs = refs[n_in + n_or :]

            @pl.when(pl.program_id(0) == 0)
            def _():
                for r in acc_refs:
                    r[...] = jnp.zeros(r.shape, r.dtype)

            for r, v in zip(acc_refs, accs):
                r[...] += v

    in_specs = [pl.BlockSpec((tm, w), lambda i, cb=cb, rb=r0 // tm: (i + rb, cb)) for (_, w, cb, r0) in rows]
    in_specs += [pl.BlockSpec(v.shape, lambda i, nd=v.ndim: (0,) * nd) for v in vecs]
    out_specs = [pl.BlockSpec((tm, w), lambda i: (i, 0)) for (w, _) in out_rows]
    out_specs += [pl.BlockSpec(sh, lambda i, nd=len(sh): (0,) * nd) for sh in out_accs]
    out_shape = [jax.ShapeDtypeStruct((s, w), dt) for (w, dt) in out_rows]
    out_shape += [jax.ShapeDtypeStruct(sh, F32) for sh in out_accs]
    res = pl.pallas_call(
        body, name=name, grid=(s // tm,), in_specs=in_specs, out_specs=out_specs, out_shape=out_shape,
        compiler_params=_cparams(("arbitrary",)),
    )(*[r[0] for r in rows], *vecs)
    return res


def _single(name, fn, ins, out_shapes):
    n_in = len(ins)

    def body(*refs):
        outs = fn(*[r[...] for r in refs[:n_in]])
        for r, v in zip(refs[n_in:], outs):
            r[...] = v.astype(r.dtype)

    vm = pl.BlockSpec(memory_space=pltpu.VMEM)
    return pl.pallas_call(
        body, name=name, in_specs=[vm] * n_in, out_specs=[vm] * len(out_shapes),
        out_shape=[jax.ShapeDtypeStruct(sh, dt) for sh, dt in out_shapes], compiler_params=_cparams(),
    )(*ins)


_NN = (((1,), (0,)), ((), ()))
_NT = (((1,), (1,)), ((), ()))
_TN = (((0,), (0,)), ((), ()))


def _mm(name, a, b, mode, out_dtype, *, tm=512, tn=1024, tk=2048, b_cs=False, o_cs=False, epi=None, extras=()):
    if mode == "tn":
        k, m = a.shape
        n = b.shape[1]
    else:
        m, k = a.shape
        if mode == "nn":
            n = b.shape[2] * 4 if b_cs else b.shape[1]
        else:
            n = b.shape[1] if b_cs else b.shape[0]
    tm = _blk(m, tm)
    tn = _blk(n // 4 if (b_cs and mode == "nn") or o_cs else n, tn)
    tk = _blk(k // 4 if (b_cs and mode == "nt") else k, tk)
    nk = k // tk
    dims = {"nn": _NN, "nt": _NT, "tn": _TN}[mode]
    n_ex = len(extras)

    def body(*refs):
        a_ref, b_ref = refs[0], refs[1]
        ex_refs, o_ref = refs[2 : 2 + n_ex], refs[2 + n_ex]

        def finish(acc):
            if epi is not None:
                acc = epi(acc, *[r[...] for r in ex_refs])
            o_ref[...] = acc.astype(o_ref.dtype)

        part = lax.dot_general(a_ref[...], b_ref[...], dims, preferred_element_type=F32)
        if nk == 1:
            finish(part)
        else:
            acc_ref = refs[3 + n_ex]
            kk = pl.program_id(2)

            @pl.when(kk == 0)
            def _():
                acc_ref[...] = part

            @pl.when(kk > 0)
            def _():
                acc_ref[...] += part

            @pl.when(kk == nk - 1)
            def _():
                finish(acc_ref[...])

    a_spec = pl.BlockSpec((tk, tm), lambda i, j, kk: (kk, i)) if mode == "tn" else pl.BlockSpec((tm, tk), lambda i, j, kk: (i, kk))
    if mode == "nn":
        if b_cs:
            per = (n // 4) // tn
            b_spec = pl.BlockSpec((None, tk, tn), lambda i, j, kk: (j // per, kk, j % per))
        else:
            b_spec = pl.BlockSpec((tk, tn), lambda i, j, kk: (kk, j))
    elif mode == "nt":
        if b_cs:
            per = (k // 4) // tk
            b_spec = pl.BlockSpec((None, tn, tk), lambda i, j, kk: (kk // per, j, kk % per))
        else:
            b_spec = pl.BlockSpec((tn, tk), lambda i, j, kk: (j, kk))
    else:
        b_spec = pl.BlockSpec((tk, tn), lambda i, j, kk: (kk, j))
    if o_cs:
        per = (n // 4) // tn
        o_spec = pl.BlockSpec((None, tm, tn), lambda i, j, kk: (j // per, i, j % per))
        o_shape = jax.ShapeDtypeStruct((4, m, n // 4), out_dtype)
    else:
        o_spec = pl.BlockSpec((tm, tn), lambda i, j, kk: (i, j))
        o_shape = jax.ShapeDtypeStruct((m, n), out_dtype)
    ex_specs = [pl.BlockSpec((tm, tn), lambda i, j, kk: (i, j)) for _ in extras]
    return pl.pallas_call(
        body, name=name, grid=(m // tm, n // tn, nk), in_specs=[a_spec, b_spec] + ex_specs, out_specs=o_spec, out_shape=o_shape,
        scratch_shapes=[pltpu.VMEM((tm, tn), F32)] if nk > 1 else [],
        compiler_params=_cparams(("parallel", "parallel", "arbitrary")),
    )(a, b, *extras)


def _place():
    x, y, c = lax.axis_index("x"), lax.axis_index("y"), lax.axis_index("c")
    return x, y, c, [(1 - x, y), (x, 1 - y), (1 - x, 1 - y)]


def _allgather8(name, v):
    m_per, n = v.shape

    def body(x_ref, out_ref, send_sems, recv_sems, local_sem):
        x, y, c, chips = _place()
        me, sibling = (x, y, c), (x, y, 1 - c)

        def rows(px, py, pc):
            return out_ref.at[pl.ds((4 * px + 2 * py + pc) * m_per, m_per), :]

        def copy(k, block, to, src=None):
            return pltpu.make_async_remote_copy(
                src_ref=rows(*block) if src is None else src, dst_ref=rows(*block), send_sem=send_sems.at[k],
                recv_sem=recv_sems.at[k], device_id=to, device_id_type=MESH)

        mine = pltpu.make_async_copy(x_ref, rows(*me), local_sem)
        mine.start()
        first = [copy(0, me, sibling, src=x_ref)]
        first += [copy(1 + j, me, (*chip, c), src=x_ref) for j, chip in enumerate(chips)]
        for cp in first:
            cp.start()
        passed = [copy(4 + j, (*chip, c), sibling) for j, chip in enumerate(chips)]
        for j, chip in enumerate(chips):
            copy(1 + j, (*chip, c), me).wait_recv()
            passed[j].start()
        copy(0, sibling, me).wait_recv()
        for j, chip in enumerate(chips):
            copy(4 + j, (*chip, 1 - c), me).wait_recv()
        for cp in first + passed:
            cp.wait_send()
        mine.wait()

    return pl.pallas_call(
        body, name=name, out_shape=jax.ShapeDtypeStruct((8 * m_per, n), v.dtype),
        in_specs=[pl.BlockSpec(memory_space=pltpu.VMEM)], out_specs=pl.BlockSpec(memory_space=pltpu.VMEM),
        scratch_shapes=[pltpu.SemaphoreType.DMA((7,)), pltpu.SemaphoreType.DMA((7,)), pltpu.SemaphoreType.DMA],
        compiler_params=_cparams(),
    )(v)


def _gather_weights(stacks):
    nw = len(stacks)

    def body(*refs):
        outs = refs[nw : 2 * nw]
        send_sems, recv_sems = refs[2 * nw :]
        x, y, c, chips = _place()
        me, sibling = (x, y, c), (x, y, 1 - c)

        def copy(w, k, block, to):
            px, py, pc = block
            half = outs[w].shape[1] // 2
            rows = outs[w].at[2 * px + py, pl.ds(pc * half, half), :]
            return pltpu.make_async_remote_copy(src_ref=rows, dst_ref=rows, send_sem=send_sems.at[6 * w + k],
                                                recv_sem=recv_sems.at[6 * w + k], device_id=to, device_id_type=MESH)

        sent = [copy(w, j, me, (*chip, c)) for w in range(nw) for j, chip in enumerate(chips)]
        for cp in sent:
            cp.start()
        for w in range(nw):
            for j, chip in enumerate(chips):
                copy(w, j, (*chip, c), me).wait_recv()
                passed = copy(w, 3 + j, (*chip, c), sibling)
                passed.start()
                sent.append(passed)
        for w in range(nw):
            for j, chip in enumerate(chips):
                copy(w, 3 + j, (*chip, 1 - c), me).wait_recv()
        for cp in sent:
            cp.wait_send()

    hbm = pl.BlockSpec(memory_space=pl.ANY)
    return pl.pallas_call(
        body, name="gather_weights", out_shape=[jax.ShapeDtypeStruct(w.shape, w.dtype) for w in stacks],
        in_specs=[hbm] * nw, out_specs=[hbm] * nw, input_output_aliases={i: i for i in range(nw)},
        scratch_shapes=[pltpu.SemaphoreType.DMA((6 * nw,)), pltpu.SemaphoreType.DMA((6 * nw,))],
        compiler_params=_cparams(),
    )(*stacks)


def _swap_halves(gs):
    nw = len(gs)

    def body(*refs):
        ins, gots = refs[:nw], refs[nw : 2 * nw]
        send_sems, recv_sems = refs[2 * nw :]
        x, y, c, _ = _place()
        cps = []
        for w in range(nw):
            half = ins[w].shape[1] // 2
            cps.append(pltpu.make_async_remote_copy(
                src_ref=ins[w].at[:, pl.ds((1 - c) * half, half), :], dst_ref=gots[w], send_sem=send_sems.at[w],
                recv_sem=recv_sems.at[w], device_id=(x, y, 1 - c), device_id_type=MESH))
            cps[-1].start()
        for cp in cps:
            cp.wait()

    hbm = pl.BlockSpec(memory_space=pl.ANY)
    return pl.pallas_call(
        body, name="swap_halves", out_shape=[jax.ShapeDtypeStruct((4, g.shape[1] // 2, g.shape[2]), g.dtype) for g in gs],
        in_specs=[hbm] * nw, out_specs=[hbm] * nw,
        scratch_shapes=[pltpu.SemaphoreType.DMA((nw,)), pltpu.SemaphoreType.DMA((nw,))], compiler_params=_cparams(),
    )(*gs)


def _scatter_chips(ps):
    nw = len(ps)

    def body(*refs):
        ins, gots = refs[:nw], refs[nw : 2 * nw]
        send_sems, recv_sems = refs[2 * nw :]
        x, y, c, chips = _place()
        cps = []
        for w in range(nw):
            for j, (px, py) in enumerate(chips):
                cps.append(pltpu.make_async_remote_copy(
                    src_ref=ins[w].at[2 * px + py], dst_ref=gots[w].at[j], send_sem=send_sems.at[3 * w + j],
                    recv_sem=recv_sems.at[3 * w + j], device_id=(px, py, c), device_id_type=MESH))
                cps[-1].start()
        for cp in cps:
            cp.wait()

    hbm = pl.BlockSpec(memory_space=pl.ANY)
    return pl.pallas_call(
        body, name="scatter_chips", out_shape=[jax.ShapeDtypeStruct((3,) + p.shape[1:], p.dtype) for p in ps],
        in_specs=[hbm] * nw, out_specs=[hbm] * nw,
        scratch_shapes=[pltpu.SemaphoreType.DMA((3 * nw,)), pltpu.SemaphoreType.DMA((3 * nw,))], compiler_params=_cparams(),
    )(*ps)


def _indexed(name, fn, idx, ins, out, grid):
    def body(idx_ref, *refs):
        refs[-1][...] = fn(*[r[...] for r in refs[:-1]]).astype(refs[-1].dtype)

    (o_shape, o_dtype), o_block, o_map = out
    return pl.pallas_call(
        body, name=name, out_shape=jax.ShapeDtypeStruct(o_shape, o_dtype),
        grid_spec=pltpu.PrefetchScalarGridSpec(
            num_scalar_prefetch=1, grid=grid, in_specs=[pl.BlockSpec(blk, mp) for (_, blk, mp) in ins],
            out_specs=pl.BlockSpec(o_block, o_map)),
        compiler_params=_cparams(("arbitrary",) * len(grid)),
    )(idx, *[a for (a, _, _) in ins])


def _join_halves(fs):
    nw = len(fs)

    def body(*refs):
        ins, outs, lands = refs[:nw], refs[nw : 2 * nw], refs[2 * nw : 3 * nw]
        send_sems, recv_sems, local_sems = refs[3 * nw :]
        x, y, c, _ = _place()
        sibling = (x, y, 1 - c)
        cps = [pltpu.make_async_remote_copy(src_ref=ins[w], dst_ref=lands[w], send_sem=send_sems.at[w], recv_sem=recv_sems.at[w],
                                            device_id=sibling, device_id_type=MESH) for w in range(nw)]
        outs_own = [pltpu.make_async_copy(ins[w], outs[w].at[c], local_sems.at[w]) for w in range(nw)]
        outs_got = [pltpu.make_async_copy(lands[w], outs[w].at[1 - c], local_sems.at[nw + w]) for w in range(nw)]
        for cp in cps + outs_own:
            cp.start()
        for w in range(nw):
            cps[w].wait_recv()
            outs_got[w].start()
        for w in range(nw):
            cps[w].wait_send()
            outs_own[w].wait()
            outs_got[w].wait()

    hbm, vm = pl.BlockSpec(memory_space=pl.ANY), pl.BlockSpec(memory_space=pltpu.VMEM)
    return pl.pallas_call(
        body, name="join_halves", out_shape=[jax.ShapeDtypeStruct((2,) + f.shape, f.dtype) for f in fs],
        in_specs=[vm] * nw, out_specs=[hbm] * nw,
        scratch_shapes=[pltpu.VMEM(f.shape, f.dtype) for f in fs]
        + [pltpu.SemaphoreType.DMA((nw,)), pltpu.SemaphoreType.DMA((nw,)), pltpu.SemaphoreType.DMA((2 * nw,))],
        compiler_params=pltpu.CompilerParams(vmem_limit_bytes=JOIN_VMEM_LIMIT),
    )(*fs)


def _rstd(v):
    return lax.rsqrt(jnp.mean(v * v, axis=-1, keepdims=True) + EPS)


def _rms_bwd(dy, xh, r):
    return r * (dy - xh * jnp.mean(dy * xh, axis=-1, keepdims=True))


def _heads(v):
    return [v[:, h * HEAD_DIM : (h + 1) * HEAD_DIM] for h in range(v.shape[1] // HEAD_DIM)]


def _gelu(v):
    k = math.sqrt(2.0 / math.pi)
    return 0.5 * v * (1.0 + jnp.tanh(k * (v + 0.044715 * v * v * v)))


def _gelu_grad(v):
    k = math.sqrt(2.0 / math.pi)
    t = jnp.tanh(k * (v + 0.044715 * v * v * v))
    return 0.5 * (1.0 + t) + 0.5 * v * (1.0 - t * t) * k * (1.0 + 3 * 0.044715 * v * v)


def _sigmoid(v):
    return 1.0 / (1.0 + jnp.exp(-v))


def _slopes(n_heads):
    return [2.0 ** (-8.0 * (h + 1.0) / n_heads) for h in range(n_heads)]


def _band_iotas():
    ri = lax.broadcasted_iota(jnp.int32, (BAND, BAND), 0)
    ci = lax.broadcasted_iota(jnp.int32, (BAND, BAND), 1)
    return ri, ci


def _attn_fwd(name, q, k, v, bps, dil):
    s, aw = q.shape
    n_heads, nb = aw // HEAD_DIM, s // BAND
    scale = HEAD_DIM**-0.5
    slopes = _slopes(n_heads)

    def body(q_ref, kp_ref, kc_ref, vp_ref, vc_ref, o_ref, l_ref):
        b = pl.program_id(0)
        first = (b % bps) == 0
        ri, ci = _band_iotas()
        dist_c = (ri - ci).astype(F32)
        dist_p = (ri - ci + BAND).astype(F32)
        valid_c = ci <= ri
        valid_p = ci >= ri + jnp.where(first, BAND, 0)
        hs = range(n_heads)
        sls = [slice(h * HEAD_DIM, (h + 1) * HEAD_DIM) for h in hs]
        qs = [q_ref[:, sl] for sl in sls]
        sc = [lax.dot_general(qs[h], kc_ref[:, sls[h]], _NT, preferred_element_type=F32) for h in hs]
        sp = [lax.dot_general(qs[h], kp_ref[:, sls[h]], _NT, preferred_element_type=F32) for h in hs]
        sc = [jnp.where(valid_c, sc[h] * scale - (slopes[h] * dil) * dist_c, NEG) for h in hs]
        sp = [jnp.where(valid_p, sp[h] * scale - (slopes[h] * dil) * dist_p, NEG) for h in hs]
        m = [jnp.maximum(jnp.max(sc[h], axis=1, keepdims=True), jnp.max(sp[h], axis=1, keepdims=True)) for h in hs]
        pc = [jnp.exp(sc[h] - m[h]) for h in hs]
        pp = [jnp.exp(sp[h] - m[h]) for h in hs]
        den = [jnp.sum(pc[h], axis=1, keepdims=True) + jnp.sum(pp[h], axis=1, keepdims=True) for h in hs]
        acc = [jnp.dot(pc[h].astype(BF16), vc_ref[:, sls[h]], preferred_element_type=F32) for h in hs]
        acc = [acc[h] + jnp.dot(pp[h].astype(BF16), vp_ref[:, sls[h]], preferred_element_type=F32) for h in hs]
        o_ref[...] = jnp.concatenate([acc[h] / den[h] for h in hs], axis=1)
        l_ref[...] = jnp.concatenate([jnp.broadcast_to(m[h] + jnp.log(den[h]), (BAND, HEAD_DIM)) for h in hs], axis=1)

    cur = pl.BlockSpec((BAND, aw), lambda b: (b, 0))
    prev = pl.BlockSpec((BAND, aw), lambda b: (jnp.maximum(b - 1, 0), 0))
    return pl.pallas_call(
        body, name=name, grid=(nb,), in_specs=[cur, prev, cur, prev, cur], out_specs=[cur, cur],
        out_shape=[jax.ShapeDtypeStruct((s, aw), F32)] * 2, compiler_params=_cparams(("arbitrary",)),
    )(q, k, k, v, v)


def _attn_bwd(name, q, k, v, do, lse, delta, bps, dil):
    s, aw = q.shape
    n_heads, nb = aw // HEAD_DIM, s // BAND
    scale = HEAD_DIM**-0.5
    slopes = _slopes(n_heads)

    def body(qc_ref, qn_ref, doc_ref, don_ref, lc_ref, ln_ref, dc_ref, dn_ref, k_ref, v_ref, dq_ref, dk_ref, dv_ref, carry):
        j = pl.program_id(0)

        @pl.when(j == 0)
        def _():
            carry[...] = jnp.zeros(carry.shape, F32)

        has_next = jnp.logical_and((j + 1) % bps != 0, j + 1 < nb)
        ri, ci = _band_iotas()
        dist_c = (ri - ci).astype(F32)
        dist_p = (ri - ci + BAND).astype(F32)
        valid_c = ci <= ri
        valid_n = ci >= ri + jnp.where(has_next, 0, BAND)
        carried = carry[...]
        hs = range(n_heads)
        sls = [slice(h * HEAD_DIM, (h + 1) * HEAD_DIM) for h in hs]
        ks, vs = [k_ref[:, sl] for sl in sls], [v_ref[:, sl] for sl in sls]
        sides = [(qc_ref, doc_ref, lc_ref, dc_ref, dist_c, valid_c), (qn_ref, don_ref, ln_ref, dn_ref, dist_p, valid_n)]
        res = []
        for q_ref, do_ref, l_ref, d_ref, dist, valid in sides:
            qs, dos = [q_ref[:, sl] for sl in sls], [do_ref[:, sl] for sl in sls]
            sc = [lax.dot_general(qs[h], ks[h], _NT, preferred_element_type=F32) for h in hs]
            dp = [lax.dot_general(dos[h], vs[h], _NT, preferred_element_type=F32) for h in hs]
            p = [jnp.where(valid, jnp.exp(sc[h] * scale - (slopes[h] * dil) * dist - l_ref[:, sls[h]]), 0.0) for h in hs]
            ds = [(p[h] * (dp[h] - d_ref[:, sls[h]])).astype(BF16) for h in hs]
            pb = [p[h].astype(BF16) for h in hs]
            dv = [lax.dot_general(pb[h], dos[h], _TN, preferred_element_type=F32) for h in hs]
            dk = [lax.dot_general(ds[h], qs[h], _TN, preferred_element_type=F32) for h in hs]
            dq = [jnp.dot(ds[h], ks[h], preferred_element_type=F32) for h in hs]
            res.append((dq, dk, dv))
        (dq_c, dk_c, dv_c), (dq_n, dk_n, dv_n) = res
        dq_ref[...] = carried + jnp.concatenate(dq_c, axis=1) * scale
        carry[...] = jnp.concatenate(dq_n, axis=1) * scale
        dk_ref[...] = jnp.concatenate([dk_c[h] + dk_n[h] for h in hs], axis=1) * scale
        dv_ref[...] = jnp.concatenate([dv_c[h] + dv_n[h] for h in hs], axis=1)

    cur = pl.BlockSpec((BAND, aw), lambda b: (b, 0))
    nxt = pl.BlockSpec((BAND, aw), lambda b: (jnp.minimum(b + 1, nb - 1), 0))
    return pl.pallas_call(
        body, name=name, grid=(nb,), in_specs=[cur, nxt, cur, nxt, cur, nxt, cur, nxt, cur, cur], out_specs=[cur] * 3,
        out_shape=[jax.ShapeDtypeStruct((s, aw), F32)] * 3, scratch_shapes=[pltpu.VMEM((BAND, aw), F32)],
        compiler_params=_cparams(("arbitrary",)),
    )(q, q, do, do, lse, lse, delta, delta, k, v)


def _to_streams(v, dil):
    if dil == 1:
        return v
    s, w = v.shape
    return v.reshape(s // dil, dil, w).transpose(1, 0, 2).reshape(s, w)


def _from_streams(v, dil):
    if dil == 1:
        return v
    s, w = v.shape
    return v.reshape(dil, s // dil, w).transpose(1, 0, 2).reshape(s, w)


SCAN_LANES = 1024
SSM_ROWS = 128


def _cmul(ar, ai, br, bi):
    return ar * br - ai * bi, ar * bi + ai * br


def _cmul_conj(ar, ai, br, bi):
    return ar * br + ai * bi, ar * bi - ai * br


def _ssm_fwd(u_src, bre, bim, cre, cim, consts, d_skip):
    u_arr, sw, ucb = u_src
    s = u_arr.shape[0]
    nblk = bre.shape[0]
    nst = nblk * 512
    t = _blk(s, SSM_ROWS)
    w = _blk(nst, SCAN_LANES)

    def body(u_ref, bre_ref, bim_ref, cre_ref, cim_ref, k_ref, ds_ref, y_ref, hr_ref, hi_ref, xr, xi, car_r, car_i):
        @pl.when(pl.program_id(0) == 0)
        def _():
            car_r[...] = jnp.zeros(car_r.shape, F32)
            car_i[...] = jnp.zeros(car_i.shape, F32)

        u = u_ref[...]
        ub = u.astype(BF16)
        for b in range(nblk):
            ch, st = slice(b * 128, (b + 1) * 128), slice(b * 512, (b + 1) * 512)
            xr[:, st] = jnp.dot(ub[:, ch], bre_ref[b], preferred_element_type=F32)
            xi[:, st] = jnp.dot(ub[:, ch], bim_ref[b], preferred_element_type=F32)
        for c in range(nst // w):
            ls = slice(c * w, (c + 1) * w)
            mult = [k_ref[i, :, ls] for i in range(8)]

            def step(i, carry, ls=ls, mult=mult):
                cr, ci = carry
                r0 = pl.multiple_of(i * SUBLANES, SUBLANES)
                hr, hi = xr[pl.ds(r0, SUBLANES), ls], xi[pl.ds(r0, SUBLANES), ls]
                for n, sh in enumerate((1, 2, 4)):
                    pr, pi = _cmul(mult[2 * n], mult[2 * n + 1], pltpu.roll(hr, sh, 0), pltpu.roll(hi, sh, 0))
                    hr, hi = hr + pr, hi + pi
                pr, pi = _cmul(mult[6], mult[7], cr, ci)
                hr, hi = hr + pr, hi + pi
                xr[pl.ds(r0, SUBLANES), ls] = hr
                xi[pl.ds(r0, SUBLANES), ls] = hi
                return jnp.broadcast_to(hr[7:8, :], hr.shape), jnp.broadcast_to(hi[7:8, :], hi.shape)

            cr, ci = lax.fori_loop(0, t // SUBLANES, step, (car_r[:, ls], car_i[:, ls]))
            car_r[:, ls] = cr
            car_i[:, ls] = ci
        hrb, hib = xr[...].astype(BF16), xi[...].astype(BF16)
        hr_ref[...] = hrb
        hi_ref[...] = hib
        for b in range(nblk):
            ch, st = slice(b * 128, (b + 1) * 128), slice(b * 512, (b + 1) * 512)
            yb = jnp.dot(hrb[:, st], cre_ref[b], preferred_element_type=F32)
            yb -= jnp.dot(hib[:, st], cim_ref[b], preferred_element_type=F32)
            y_ref[:, ch] = yb + ds_ref[:, ch] * u[:, ch]

    whole = lambda a: pl.BlockSpec(a.shape, lambda i, nd=a.ndim: (0,) * nd)
    return pl.pallas_call(
        body, name="ssm_fwd", grid=(s // t,),
        in_specs=[pl.BlockSpec((t, sw), lambda i: (i, ucb))] + [whole(a) for a in (bre, bim, cre, cim, consts, d_skip)],
        out_specs=[pl.BlockSpec((t, sw), lambda i: (i, 0)), pl.BlockSpec((t, nst), lambda i: (i, 0)), pl.BlockSpec((t, nst), lambda i: (i, 0))],
        out_shape=[jax.ShapeDtypeStruct((s, sw), F32), jax.ShapeDtypeStruct((s, nst), BF16), jax.ShapeDtypeStruct((s, nst), BF16)],
        scratch_shapes=[pltpu.VMEM((t, nst), F32), pltpu.VMEM((t, nst), F32), pltpu.VMEM((SUBLANES, nst), F32), pltpu.VMEM((SUBLANES, nst), F32)],
        compiler_params=_cparams(("arbitrary",)),
    )(u_arr, bre, bim, cre, cim, consts, d_skip)


def _ssm_bwd(dy, u_src, hr, hi, ctre, ctim, btre, btim, consts, d_skip):
    u_arr, sw, ucb = u_src
    s = dy.shape[0]
    nblk = ctre.shape[0]
    nst = nblk * 512
    t = _blk(s, SSM_ROWS)
    w = _blk(nst, SCAN_LANES)
    nt = s // t

    def body(dy_ref, u_ref, hr_ref, hi_ref, ctre_ref, ctim_ref, btre_ref, btim_ref, k_ref, ds_ref,
             du_ref, db_ref, dc_ref, da_ref, dd_ref, lr, li, hfr, hfi, car_r, car_i):
        @pl.when(pl.program_id(0) == 0)
        def _():
            car_r[...] = jnp.zeros(car_r.shape, F32)
            car_i[...] = jnp.zeros(car_i.shape, F32)
            db_ref[...] = jnp.zeros(db_ref.shape, F32)
            dc_ref[...] = jnp.zeros(dc_ref.shape, F32)
            da_ref[...] = jnp.zeros(da_ref.shape, F32)
            dd_ref[...] = jnp.zeros(dd_ref.shape, F32)

        dyv, u = dy_ref[...], u_ref[...]
        dyb, ub = dyv.astype(BF16), u.astype(BF16)
        hrb, hib = hr_ref[...], hi_ref[...]
        hfr[...] = hrb.astype(F32)
        hfi[...] = hib.astype(F32)
        for b in range(nblk):
            ch, st = slice(b * 128, (b + 1) * 128), slice(b * 512, (b + 1) * 512)
            lr[:, st] = jnp.dot(dyb[:, ch], ctre_ref[b], preferred_element_type=F32)
            li[:, st] = -jnp.dot(dyb[:, ch], ctim_ref[b], preferred_element_type=F32)
        last_row = lax.broadcasted_iota(jnp.int32, (SUBLANES, w), 0) == SUBLANES - 1
        for c in range(nst // w):
            ls = slice(c * w, (c + 1) * w)
            mult = [k_ref[i, :, ls] for i in range(8)]

            def step(i, carry, ls=ls, mult=mult):
                cr, ci, ar, ai = carry
                r0 = pl.multiple_of((t // SUBLANES - 1 - i) * SUBLANES, SUBLANES)
                gr, gi = lr[pl.ds(r0, SUBLANES), ls], li[pl.ds(r0, SUBLANES), ls]
                for n, sh in enumerate((1, 2, 4)):
                    pr, pi = _cmul_conj(mult[2 * n], mult[2 * n + 1], pltpu.roll(gr, SUBLANES - sh, 0), pltpu.roll(gi, SUBLANES - sh, 0))
                    gr, gi = gr + pr, gi + pi
                pr, pi = _cmul_conj(mult[6], mult[7], cr, ci)
                gr, gi = gr + pr, gi + pi
                lr[pl.ds(r0, SUBLANES), ls] = gr
                li[pl.ds(r0, SUBLANES), ls] = gi
                nr = jnp.where(last_row, cr, pltpu.roll(gr, SUBLANES - 1, 0))
                ni = jnp.where(last_row, ci, pltpu.roll(gi, SUBLANES - 1, 0))
                fr, fi = hfr[pl.ds(r0, SUBLANES), ls], hfi[pl.ds(r0, SUBLANES), ls]
                pr, pi = _cmul_conj(fr, fi, nr, ni)
                return (jnp.broadcast_to(gr[0:1, :], gr.shape), jnp.broadcast_to(gi[0:1, :], gi.shape), ar + pr, ai + pi)

            zero = jnp.zeros((SUBLANES, w), F32)
            cr, ci, ar, ai = lax.fori_loop(0, t // SUBLANES, step, (car_r[:, ls], car_i[:, ls], zero, zero))
            car_r[:, ls] = cr
            car_i[:, ls] = ci
            da_ref[0, :, ls] += ar
            da_ref[1, :, ls] += ai
        lrb, lib = lr[...].astype(BF16), li[...].astype(BF16)
        for b in range(nblk):
            ch, st = slice(b * 128, (b + 1) * 128), slice(b * 512, (b + 1) * 512)
            dub = jnp.dot(lrb[:, st], btre_ref[b], preferred_element_type=F32)
            dub += jnp.dot(lib[:, st], btim_ref[b], preferred_element_type=F32)
            du_ref[:, ch] = dub + ds_ref[:, ch] * dyv[:, ch]
            db_ref[0, b] += lax.dot_general(ub[:, ch], lrb[:, st], _TN, preferred_element_type=F32)
            db_ref[1, b] += lax.dot_general(ub[:, ch], lib[:, st], _TN, preferred_element_type=F32)
            dc_ref[0, b] += lax.dot_general(dyb[:, ch], hrb[:, st], _TN, preferred_element_type=F32)
            dc_ref[1, b] -= lax.dot_general(dyb[:, ch], hib[:, st], _TN, preferred_element_type=F32)
        dd_ref[...] += _colsum8(dyv * u)

    whole = lambda a: pl.BlockSpec(a.shape, lambda i, nd=a.ndim: (0,) * nd)
    rev = lambda wd, cb=0: pl.BlockSpec((t, wd), lambda i, cb=cb: (nt - 1 - i, cb))
    acc = lambda sh: pl.BlockSpec(sh, lambda i, nd=len(sh): (0,) * nd)
    db_shape, da_shape, dd_shape = (2, nblk, 128, 512), (2, SUBLANES, nst), (SUBLANES, sw)
    return pl.pallas_call(
        body, name="ssm_bwd", grid=(nt,),
        in_specs=[rev(sw), rev(sw, ucb), rev(nst), rev(nst)] + [whole(a) for a in (ctre, ctim, btre, btim, consts, d_skip)],
        out_specs=[rev(sw), acc(db_shape), acc(db_shape), acc(da_shape), acc(dd_shape)],
        out_shape=[jax.ShapeDtypeStruct((s, sw), F32), jax.ShapeDtypeStruct(db_shape, F32), jax.ShapeDtypeStruct(db_shape, F32),
                   jax.ShapeDtypeStruct(da_shape, F32), jax.ShapeDtypeStruct(dd_shape, F32)],
        scratch_shapes=[pltpu.VMEM((t, nst), F32)] * 4 + [pltpu.VMEM((SUBLANES, nst), F32)] * 2,
        compiler_params=_cparams(("arbitrary",)),
    )(dy, u_arr, hr, hi, ctre, ctim, btre, btim, consts, d_skip)


def _ssm_discretise(lam_re, lam_im, log_step):
    step = jnp.exp(log_step)
    e = jnp.exp(lam_re * step)
    ar, ai = e * jnp.cos(lam_im * step), e * jnp.sin(lam_im * step)
    den = lam_re * lam_re + lam_im * lam_im
    inv_r, inv_i = lam_re / den, -lam_im / den
    fr, fi = _cmul(ar - 1.0, ai, inv_r, inv_i)
    return step, ar, ai, inv_r, inv_i, fr, fi


def _ssm_prep_fn(lam_re, lam_im, log_step):
    _, ar, ai, _, _, fr, fi = _ssm_discretise(lam_re, lam_im, log_step)
    pw = [(ar, ai)]
    for n in range(1, 8):
        pw.append(_cmul(*pw[n - 1], ar, ai))
    return [p[0] for p in pw] + [p[1] for p in pw] + [fr, fi]


def _ssm_param_grad_fn(lam_re, lam_im, log_step, da_r, da_i, df_r, df_i):
    step, ar, ai, inv_r, inv_i, fr, fi = _ssm_discretise(lam_re, lam_im, log_step)
    pr, pi = _cmul_conj(inv_r, inv_i, df_r, df_i)
    dat_r, dat_i = da_r + pr, da_i + pi
    wr, wi = _cmul(fr, fi, inv_r, inv_i)
    dl_r, dl_i = _cmul_conj(-wr, -wi, df_r, df_i)
    dz_r, dz_i = _cmul_conj(ar, ai, dat_r, dat_i)
    dl_r, dl_i = dl_r + step * dz_r, dl_i + step * dz_i
    dstep = jnp.sum(dz_r * lam_re + dz_i * lam_im, axis=-1, keepdims=True)
    return [dl_r, dl_i, step * dstep]


def _block_diag(v, rows_first):
    g, a, b = v.shape
    nb = g // GROUPS_PER_BLOCK
    eye = jnp.eye(GROUPS_PER_BLOCK, dtype=v.dtype)
    v = v.reshape(nb, GROUPS_PER_BLOCK, a, 1, b) * eye[None, :, None, :, None]
    return v.reshape(nb, GROUPS_PER_BLOCK * a, GROUPS_PER_BLOCK * b)


def _block_diag_take(v, a, b):
    nb = v.shape[0]
    v = v.reshape(nb, GROUPS_PER_BLOCK, a, GROUPS_PER_BLOCK, b)
    idx = jnp.arange(GROUPS_PER_BLOCK)
    return v[:, idx, :, idx, :].transpose(1, 0, 2, 3).reshape(nb * GROUPS_PER_BLOCK, a, b)


def _adamw(w, g, m, v):
    m = ADAM_B1 * m + (1.0 - ADAM_B1) * g
    v = ADAM_B2 * v + (1.0 - ADAM_B2) * (g * g)
    m_hat = m / (1.0 - ADAM_B1**ADAM_STEP)
    v_hat = v / (1.0 - ADAM_B2**ADAM_STEP)
    delta = -ADAM_LR * (m_hat / (jnp.sqrt(v_hat) + ADAM_EPS) + ADAM_WD * w)
    return delta, m, v


def _adamw_rows(name, w, g, m, v):
    wd = w.shape[1]
    return _rowwise(name, lambda a, b, c, d: (list(_adamw(a, b, c, d)), []), [w, g, m, v], [], [(wd, F32)] * 3, [], tm=128)


SMALL = ["b_ada", "norm1_g", "q_norm_g", "k_norm_g", "lam_re", "lam_im", "log_step", "b_re", "b_im", "c_re", "c_im",
         "d_skip", "b_glu", "attn_out_g", "ssm_out_g", "norm2_g"]
LARGE = ["w_in", "w_glu", "w_out", "w_ff1", "w_ff2"]
ORDER = ["w_ada", "b_ada", "norm1_g", "w_in", "q_norm_g", "k_norm_g", "lam_re", "lam_im", "log_step", "b_re", "b_im", "c_re",
         "c_im", "d_skip", "w_glu", "b_glu", "attn_out_g", "ssm_out_g", "w_out", "norm2_g", "w_ff1", "w_ff2"]


def _pack(arrs):
    flat = jnp.concatenate([a.reshape(-1) for a in arrs])
    rows = -(-flat.shape[0] // PACK_LANES)
    rows = -(-rows // SUBLANES) * SUBLANES
    return jnp.pad(flat, (0, rows * PACK_LANES - flat.shape[0])).reshape(rows, PACK_LANES)


def _unpack(packed, shapes):
    flat, out, o = packed.reshape(-1), [], 0
    for sh in shapes:
        n = int(np.prod(sh))
        out.append(flat[o : o + n].reshape(sh))
        o += n
    return out


def kernel(x, c, w_ada, b_ada, norm1_g, w_in, q_norm_g, k_norm_g, lam_re, lam_im, log_step, b_re, b_im, c_re, c_im, d_skip, w_glu, b_glu, attn_out_g, ssm_out_g, w_out, norm2_g, w_ff1, w_ff2, loss_target, m_w_ada, m_b_ada, m_norm1_g, m_w_in, m_q_norm_g, m_k_norm_g, m_lam_re, m_lam_im, m_log_step, m_b_re, m_b_im, m_c_re, m_c_im, m_d_skip, m_w_glu, m_b_glu, m_attn_out_g, m_ssm_out_g, m_w_out, m_norm2_g, m_w_ff1, m_w_ff2, v_w_ada, v_b_ada, v_norm1_g, v_w_in, v_q_norm_g, v_k_norm_g, v_lam_re, v_lam_im, v_log_step, v_b_re, v_b_im, v_c_re, v_c_im, v_d_skip, v_w_glu, v_b_glu, v_attn_out_g, v_ssm_out_g, v_w_out, v_norm2_g, v_w_ff1, v_w_ff2):
    args = dict(locals())
    wts = {n: args[n] for n in ORDER}
    mom = {n: args["m_" + n] for n in ORDER}
    var = {n: args["v_" + n] for n in ORDER}

    x2, tgt = x[0], loss_target[0]
    s, d = x2.shape
    aw = d // 2
    sw = d - aw
    n_groups = sw // SSM_GROUP
    nst = n_groups * STATE_DIM
    chip = 2 * lax.axis_index("x") + lax.axis_index("y")

    chip_idx = jnp.reshape(chip, (1,)).astype(jnp.int32)
    core_idx = jnp.reshape(lax.axis_index("c"), (1,)).astype(jnp.int32)

    def cast_into_stack(n):
        w2 = wts[n][0]
        rows, cols = w2.shape
        tr = _blk(rows, 256)
        return _indexed("cast_" + n, lambda a: a, chip_idx, [(w2, (tr, cols), lambda r, ix: (r, 0))],
                        (((4, rows, cols), BF16), (None, tr, cols), lambda r, ix: (ix[0], r, 0)), (rows // tr,))

    w_in_s, w_glu_s, w_out_s, w_ff1_s, w_ff2_s = _gather_weights([cast_into_stack(n) for n in LARGE])
    w_glu_f = w_glu_s.reshape(sw, sw)
    w_out_f = w_out_s.reshape(d, d)
    w_ff2_f = w_ff2_s.reshape(-1, d)
    dff = w_ff2_f.shape[0]

    c_all = _allgather8("gather_c", jnp.pad(c, ((0, SUBLANES - 1), (0, 0))))[::SUBLANES]
    n_mod = w_ada.shape[2]
    b_ada_mine = lax.dynamic_slice(b_ada, (0, chip * n_mod), (1, n_mod))
    sil_c = _single("silu_c", lambda v: [v * _sigmoid(v)], [c_all], [(c_all.shape, F32)])[0]
    mod_part = _mm("ada_fwd", sil_c, w_ada[0], "nn", F32, tn=512, epi=lambda acc, b: acc + b[0:1, :],
                   extras=[jnp.broadcast_to(b_ada_mine, (SUBLANES, n_mod))])
    mod_all = _allgather8("gather_mod", mod_part)
    me = 2 * chip + lax.axis_index("c")
    mod = jnp.concatenate([lax.dynamic_slice(mod_all, (16 * k + me, 0), (1, n_mod)) for k in range(4)], axis=1)
    sh1, sc1, g1, sh2, sc2, g2 = [mod[:, i * d : (i + 1) * d] for i in range(6)]

    def norm_mod(v, g, sc, sh):
        return [v * _rstd(v) * g * (1.0 + sc) + sh], []

    (h1b,) = _rowwise("norm1", norm_mod, [x2], [norm1_g, sc1, sh1], [(d, BF16)], [])
    proj = _mm("proj_in", h1b, w_in_s, "nn", F32, b_cs=True)
    assert aw == sw
    q_src, k_src, v_src, u_src = [(proj, aw, i) for i in range(3)] + [(proj, sw, 3)]

    def qk_prep(q, k, v, gq, gk):
        qn = jnp.concatenate([h * _rstd(h) * gq for h in _heads(q)], axis=1)
        kn = jnp.concatenate([h * _rstd(h) * gk for h in _heads(k)], axis=1)
        return [qn, kn, v], []

    qn, kn, vb = _rowwise("qk_prep", qk_prep, [q_src, k_src, v_src], [q_norm_g, k_norm_g], [(aw, BF16)] * 3, [])

    pats = []
    for window, dil in DILATION_PATTERNS:
        assert window // dil == BAND and s % (dil * BAND) == 0
        bps = s // dil // BAND
        qp, kp, vp = [_to_streams(t, dil) for t in (qn, kn, vb)]
        o_p, l_p = _attn_fwd(f"attn_fwd_d{dil}", qp, kp, vp, bps, dil)
        pats.append((dil, bps, qp, kp, vp, _from_streams(o_p, dil), _from_streams(l_p, dil)))

    def combine(o1, o2, o3, l1, l2, l3):
        m = jnp.maximum(jnp.maximum(l1, l2), l3)
        tot = m + jnp.log(jnp.exp(l1 - m) + jnp.exp(l2 - m) + jnp.exp(l3 - m))
        return [jnp.exp(l1 - tot) * o1 + jnp.exp(l2 - tot) * o2 + jnp.exp(l3 - tot) * o3, tot], []

    attn, lse = _rowwise("attn_mix", combine, [p[5] for p in pats] + [p[6] for p in pats], [], [(aw, F32)] * 2, [])

    lam_re2, lam_im2, log_step2 = lam_re[0], lam_im[0], log_step[0].reshape(n_groups, 1)
    prep = _single("ssm_prep", _ssm_prep_fn, [lam_re2, lam_im2, log_step2], [((n_groups, STATE_DIM), F32)] * 18)
    p_r = jnp.stack([p.reshape(nst) for p in prep[:8]])
    p_i = jnp.stack([p.reshape(nst) for p in prep[8:16]])
    f_ri = jnp.stack(prep[16:])
    row = jnp.arange(SUBLANES)[:, None]
    zero = jnp.zeros((SUBLANES, nst), F32)
    sel = lambda cond, v: jnp.where(cond, jnp.broadcast_to(v[None, :], (SUBLANES, nst)), zero)
    fwd_consts = jnp.stack([sel(row >= 1, p_r[0]), sel(row >= 1, p_i[0]), sel(row >= 2, p_r[1]), sel(row >= 2, p_i[1]),
                            sel(row >= 4, p_r[3]), sel(row >= 4, p_i[3]), p_r, p_i])
    bwd_consts = jnp.stack([sel(row <= 6, p_r[0]), sel(row <= 6, p_i[0]), sel(row <= 5, p_r[1]), sel(row <= 5, p_i[1]),
                            sel(row <= 3, p_r[3]), sel(row <= 3, p_i[3]), p_r[::-1], p_i[::-1]])
    nblk = n_groups // GROUPS_PER_BLOCK
    f_rows = f_ri.reshape(2, nblk, 1, 512)
    braw_r = _block_diag(b_re[0].transpose(0, 2, 1), True)
    braw_i = _block_diag(b_im[0].transpose(0, 2, 1), True)

    def bbar_fn(br, bi, f):
        r, i = _cmul(f[0], f[1], br, bi)
        return [r, i]

    bbar_r, bbar_i = _single("ssm_bbar", bbar_fn, [braw_r, braw_i, f_rows], [(braw_r.shape, BF16)] * 2)
    ct_r, ct_i = _block_diag(c_re[0], True).astype(BF16), _block_diag(c_im[0], True).astype(BF16)
    ds2 = d_skip
    y_ssm, h_r, h_i = _ssm_fwd(u_src, bbar_r, bbar_i, ct_r.transpose(0, 2, 1), ct_i.transpose(0, 2, 1), fwd_consts, ds2)

    (yb,) = _rowwise("gelu", lambda v: ([_gelu(v)], []), [y_ssm], [], [(sw, BF16)], [])
    z = _mm("glu_fwd", yb, w_glu_f, "nn", F32)

    def glu(ys, zz, b):
        return [_gelu(ys) * _sigmoid(zz + b)], []

    (ssm,) = _rowwise("glu", glu, [y_ssm, z], [b_glu], [(sw, F32)], [])

    def cat_norm(a, sm, ga, gs):
        return [jnp.concatenate([a * _rstd(a) * ga, sm * _rstd(sm) * gs], axis=1)], []

    (catb,) = _rowwise("cat_norm", cat_norm, [attn, ssm], [attn_out_g, ssm_out_g], [(d, BF16)], [])
    mixed = _mm("mix_out", catb, w_out_f, "nn", F32)

    def resid_norm(xv, mx, g1v, g, sc, sh):
        x1v = xv + g1v * mx
        return [x1v, x1v * _rstd(x1v) * g * (1.0 + sc) + sh], []

    x1, h2b = _rowwise("resid_norm2", resid_norm, [x2, mixed], [g1, norm2_g, sc2, sh2], [(d, F32), (d, BF16)], [])
    a_ff = _mm("ff1", h2b, w_ff1_s, "nn", BF16, b_cs=True)
    (rb,) = _rowwise("sq_relu", lambda v: ([jnp.square(jnp.maximum(v.astype(F32), 0.0))], []), [a_ff], [], [(dff, BF16)], [])
    ff = _mm("ff2", rb, w_ff2_f, "nn", F32)

    def loss_fn(x1v, ffv, tg, g2v):
        diff = x1v + g2v * ffv - tg
        dout = diff * (1.0 / d)
        return [dout, dout * g2v], [_colsum8(0.5 * diff * dout), _colsum8(dout * ffv)]

    dout, dffb, loss_acc, dg2_acc = _rowwise("loss", loss_fn, [x1, ff, tgt], [g2], [(d, F32), (d, BF16)], [(SUBLANES, d)] * 2)

    da_ff = _mm("ff2_bwd", dffb, w_ff2_f, "nt", BF16, epi=lambda acc, av: acc * (2.0 * jnp.maximum(av.astype(F32), 0.0)), extras=[a_ff])
    gw = {}
    gw["w_ff2"] = _mm("ff2_wgrad", rb, dffb, "tn", BF16).reshape(4, dff // 4, d)
    gw["w_ff1"] = _mm("ff1_wgrad", h2b, da_ff, "tn", BF16, o_cs=True)
    dh2 = _mm("ff1_bwd", da_ff, w_ff1_s, "nt", F32, b_cs=True)

    def norm2_bwd(dh, x1v, do, mx, g, sc, g1v):
        r = _rstd(x1v)
        xh = x1v * r
        dn = dh * (1.0 + sc)
        dx = do + _rms_bwd(dn * g, xh, r)
        return [dx, dx * g1v], [_colsum8(dh * xh * g), _colsum8(dh), _colsum8(dn * xh), _colsum8(dx * mx)]

    dx1, dmixb, dsc2_acc, dsh2_acc, dn2g_acc, dg1_acc = _rowwise(
        "norm2_bwd", norm2_bwd, [dh2, x1, dout, mixed], [norm2_g, sc2, g1], [(d, F32), (d, BF16)], [(SUBLANES, d)] * 4)

    dcat = _mm("mix_out_bwd", dmixb, w_out_f, "nt", F32)
    gw["w_out"] = _mm("mix_out_wgrad", catb, dmixb, "tn", BF16).reshape(4, d // 4, d)

    def cat_bwd(da, dsm, a, sm, ga, gs):
        ra, rs = _rstd(a), _rstd(sm)
        ah, sh = a * ra, sm * rs
        return [_rms_bwd(da * ga, ah, ra), _rms_bwd(dsm * gs, sh, rs)], [_colsum8(da * ah), _colsum8(dsm * sh)]

    dattn, dssm, dga_acc, dgs_acc = _rowwise(
        "cat_norm_bwd", cat_bwd, [(dcat, aw, 0), (dcat, sw, 1), attn, ssm], [attn_out_g, ssm_out_g], [(aw, F32), (sw, F32)],
        [(SUBLANES, aw), (SUBLANES, sw)])

    def glu_bwd(dsm, ys, zz, b):
        sg = _sigmoid(zz + b)
        dz = dsm * _gelu(ys) * sg * (1.0 - sg)
        return [dz, dsm * sg], [_colsum8(dz)]

    dzb, dy_direct, dbglu_acc = _rowwise("glu_bwd", glu_bwd, [dssm, y_ssm, z], [b_glu], [(sw, BF16), (sw, F32)], [(SUBLANES, sw)])
    gw["w_glu"] = _mm("glu_wgrad", yb, dzb, "tn", BF16).reshape(4, sw // 4, sw)
    dy_glu = _mm("glu_bwd_mm", dzb, w_glu_f, "nt", F32)
    (dys,) = _rowwise("gelu_bwd", lambda a, b, ys: ([(a + b) * _gelu_grad(ys)], []), [dy_direct, dy_glu, y_ssm], [], [(sw, F32)], [])
    du, db_bar, dc_bd, da_acc, dds_acc = _ssm_bwd(dys, u_src, h_r, h_i, ct_r, ct_i, bbar_r.transpose(0, 2, 1),
                                                  bbar_i.transpose(0, 2, 1), bwd_consts, ds2)

    def delta_fn(da, a):
        dl = jnp.concatenate([jnp.broadcast_to(jnp.sum(p * q, axis=1, keepdims=True), p.shape) for p, q in zip(_heads(da), _heads(a))], axis=1)
        return [da, dl], []

    dob, delta = _rowwise("attn_delta", delta_fn, [dattn, attn], [], [(aw, BF16), (aw, F32)], [])
    dqs, dks, dvs = [], [], []
    for dil, bps, qp, kp, vp, _, _ in pats:
        dop, lsp, dlp = [_to_streams(t, dil) for t in (dob, lse, delta)]
        dq_p, dk_p, dv_p = _attn_bwd(f"attn_bwd_d{dil}", qp, kp, vp, dop, lsp, dlp, bps, dil)
        dqs.append(_from_streams(dq_p, dil))
        dks.append(_from_streams(dk_p, dil))
        dvs.append(_from_streams(dv_p, dil))

    def qk_bwd(dq1, dq2, dq3, dk1, dk2, dk3, dv1, dv2, dv3, duv, q, k, gq, gk):
        def one(dn, raw, g):
            outs, acc = [], 0.0
            for dh, h in zip(_heads(dn), _heads(raw)):
                r = _rstd(h)
                hh = h * r
                outs.append(_rms_bwd(dh * g, hh, r))
                acc = acc + _colsum8(dh * hh)
            return jnp.concatenate(outs, axis=1), acc

        dq, gq_acc = one(dq1 + dq2 + dq3, q, gq)
        dk, gk_acc = one(dk1 + dk2 + dk3, k, gk)
        return [jnp.concatenate([dq, dk, dv1 + dv2 + dv3, duv], axis=1)], [gq_acc, gk_acc]

    dprojb, dgq_acc, dgk_acc = _rowwise("qk_norm_bwd", qk_bwd, dqs + dks + dvs + [du, q_src, k_src], [q_norm_g, k_norm_g],
                                        [(3 * aw + sw, BF16)], [(SUBLANES, HEAD_DIM)] * 2, tm=128)
    gw["w_in"] = _mm("proj_in_wgrad", h1b, dprojb, "tn", BF16, o_cs=True)
    dh1 = _mm("proj_in_bwd", dprojb, w_in_s, "nt", F32, b_cs=True)

    def norm1_bwd(dh, xv, dxr, g, sc):
        r = _rstd(xv)
        xh = xv * r
        dn = dh * (1.0 + sc)
        return [dxr + _rms_bwd(dn * g, xh, r)], [_colsum8(dh * xh * g), _colsum8(dh), _colsum8(dn * xh)]

    grad_x, dsc1_acc, dsh1_acc, dn1g_acc = _rowwise("norm1_bwd", norm1_bwd, [dh1, x2, dx1], [norm1_g, sc1], [(d, F32)], [(SUBLANES, d)] * 3)

    def b_grad_fn(dbb, br, bi, f):
        fr, fi = f[0], f[1]
        dbr, dbi = _cmul_conj(fr, fi, dbb[0], dbb[1])
        dfr = jnp.sum(dbb[0] * br + dbb[1] * bi, axis=1, keepdims=True)
        dfi = jnp.sum(dbb[1] * br - dbb[0] * bi, axis=1, keepdims=True)
        return [dbr, dbi, dfr, dfi]

    dbr_bd, dbi_bd, df_r, df_i = _single("ssm_b_grad", b_grad_fn, [db_bar, braw_r, braw_i, f_rows],
                                         [(braw_r.shape, F32)] * 2 + [((nblk, 1, 512), F32)] * 2)

    def sums_fn(*accs):
        return [jnp.sum(a, axis=-2, keepdims=True) for a in accs]

    acc_list = [dsh1_acc, dsc1_acc, dg1_acc, dsh2_acc, dsc2_acc, dg2_acc, dn1g_acc, dgq_acc, dgk_acc, dds_acc, dbglu_acc, dga_acc,
                dgs_acc, dn2g_acc, loss_acc, da_acc]
    sums = _single("small_sums", sums_fn, acc_list, [(a.shape[:-2] + (1, a.shape[-1]), F32) for a in acc_list])
    (dsh1, dsc1, dg1, dsh2, dsc2, dg2, dn1g, dgq, dgk, dds, dbglu, dga, dgs, dn2g, loss_row, da_sum) = sums
    loss_dev = _single("loss_sum", lambda v: [jnp.sum(v, axis=1, keepdims=True)], [loss_row], [((1, 1), F32)])[0]
    gp = (n_groups, STATE_DIM)
    dlam_re, dlam_im, dlog_step = _single(
        "ssm_param_grad", _ssm_param_grad_fn,
        [lam_re2, lam_im2, log_step2, da_sum[0].reshape(gp), da_sum[1].reshape(gp), df_r.reshape(gp), df_i.reshape(gp)],
        [(gp, F32), (gp, F32), ((n_groups, 1), F32)])
    small_dev = {
        "b_ada": jnp.concatenate([dsh1, dsc1, dg1, dsh2, dsc2, dg2], axis=1),
        "norm1_g": dn1g, "q_norm_g": dgq, "k_norm_g": dgk, "lam_re": dlam_re, "lam_im": dlam_im, "log_step": dlog_step,
        "b_re": _block_diag_take(dbr_bd, SSM_GROUP, STATE_DIM).transpose(0, 2, 1),
        "b_im": _block_diag_take(dbi_bd, SSM_GROUP, STATE_DIM).transpose(0, 2, 1),
        "c_re": _block_diag_take(dc_bd[0], SSM_GROUP, STATE_DIM), "c_im": _block_diag_take(dc_bd[1], SSM_GROUP, STATE_DIM),
        "d_skip": dds, "b_glu": dbglu, "attn_out_g": dga, "ssm_out_g": dgs, "norm2_g": dn2g,
    }

    packed = _pack([small_dev[n] for n in SMALL])
    rows = packed.shape[0]
    gathered = _allgather8("gather_small", packed).reshape(8, rows, PACK_LANES)

    def sum8_fn(g):
        acc = g[0]
        for i in range(1, 8):
            acc = acc + g[i]
        return [acc]

    g_small = _single("sum_small", sum8_fn, [gathered], [((rows, PACK_LANES), F32)])[0]
    w_small, m_small, v_small = [_pack([src[n] for n in SMALL]) for src in (wts, mom, var)]
    upd_small = _adamw_rows("adamw_small", w_small, g_small, m_small, v_small)
    shapes = [wts[n].shape for n in SMALL]
    out = {}
    for kind, arr in zip(("grad", "delta", "new_m", "new_v"), (g_small,) + tuple(upd_small)):
        for n, val in zip(SMALL, _unpack(arr, shapes)):
            out[kind, n] = val

    dmod_all = gathered[:, 0 : -(-6 * d // PACK_LANES), :].reshape(8, -1)[:, : 6 * d]
    dmod_mine = lax.dynamic_slice(dmod_all, (0, chip * n_mod), (8, n_mod))
    g_ada = _mm("ada_wgrad", sil_c, dmod_mine, "tn", F32, tm=512, tn=512)
    upd = _adamw_rows("adamw_w_ada", w_ada[0], g_ada, m_w_ada[0], v_w_ada[0])
    for kind, val in zip(("grad", "delta", "new_m", "new_v"), (g_ada,) + tuple(upd)):
        out[kind, "w_ada"] = val[None]

    gs = [gw[n] for n in LARGE]
    got = _swap_halves(gs)

    def add2(i, g, b):
        _, half, cols = b.shape
        tr = _blk(half, 256)
        nrb = half // tr
        blk = (None, tr, cols)
        return _indexed(f"grad_add2_{LARGE[i]}", lambda p, q: p.astype(F32) + q.astype(F32), core_idx,
                        [(g, blk, lambda k, r, ix: (k, ix[0] * nrb + r, 0)), (b, blk, lambda k, r, ix: (k, r, 0))],
                        ((b.shape, BF16), blk, lambda k, r, ix: (k, r, 0)), (4, nrb))

    parts = [add2(i, g, b) for i, (g, b) in enumerate(zip(gs, got))]
    got2 = _scatter_chips(parts)

    def add4(i, p, b):
        _, half, cols = p.shape
        tr = _blk(half, 256)
        blk = (None, tr, cols)
        fn = lambda own, q0, q1, q2: ((own.astype(F32) + q0.astype(F32)) + q1.astype(F32)) + q2.astype(F32)
        srcs = [(p, blk, lambda r, ix: (ix[0], r, 0))] + [(b, blk, lambda r, ix, j=j: (j, r, 0)) for j in range(3)]
        return _indexed(f"grad_add4_{LARGE[i]}", fn, chip_idx, srcs, (((half, cols), F32), (tr, cols), lambda r, ix: (r, 0)), (half // tr,))

    finals = [add4(i, p, b) for i, (p, b) in enumerate(zip(parts, got2))]
    for n, full in zip(LARGE, _join_halves(finals)):
        g = full.reshape(wts[n].shape[1:])
        upd = _adamw_rows("adamw_" + n, wts[n][0], g, mom[n][0], var[n][0])
        for kind, val in zip(("grad", "delta", "new_m", "new_v"), (g,) + tuple(upd)):
            out[kind, n] = val[None]

    loss = lax.psum(loss_dev[0, 0], ("x", "y", "c"))
    return (loss, grad_x[None], *[out[kind, n] for kind in ("grad", "delta", "new_m", "new_v") for n in ORDER])
```

```python
import functools
import math

import jax
import jax.numpy as jnp
import numpy as np
from jax import lax
from jax.experimental import pallas as pl
from jax.experimental.pallas import tpu as pltpu

F32 = jnp.float32
BF16 = jnp.bfloat16
MESH = pl.DeviceIdType.MESH

EPS = 1e-6
HEAD_DIM = 128
SSM_GROUP = 16
STATE_DIM = 64
GROUPS_PER_BLOCK = 8
DILATION_PATTERNS = ((128, 1), (512, 4), (2048, 16))
BAND = 128
NEG = -1e30

ADAM_LR, ADAM_B1, ADAM_B2, ADAM_EPS, ADAM_WD, ADAM_STEP = 0.001, 0.9, 0.999, 1e-08, 0.01, 10

V7X_VMEM_BYTES = 64 * 2**20
VMEM_LIMIT = 48 * 2**20
JOIN_VMEM_LIMIT = 56 * 2**20
SUBLANES = 8
PACK_LANES = 1024


def _cparams(sem=None):
    return pltpu.CompilerParams(dimension_semantics=sem, vmem_limit_bytes=VMEM_LIMIT)


def _blk(n, want):
    b = min(n, want)
    while n % b:
        b //= 2
    return b


def _colsum8(v):
    tm, w = v.shape
    return v.reshape(tm // SUBLANES, SUBLANES, w).sum(axis=0)


def _rowwise(name, fn, rows, vecs, out_rows, out_accs, tm=256, n_rows=None):
    rows = [r if isinstance(r, tuple) else (r, r.shape[1], 0) for r in rows]
    rows = [r if len(r) == 4 else r + (0,) for r in rows]
    s = rows[0][0].shape[0] if n_rows is None else n_rows
    tm = _blk(s, tm)
    assert all(r[3] % tm == 0 for r in rows)
    n_in, n_or = len(rows) + len(vecs), len(out_rows)

    def body(*refs):
        outs, accs = fn(*[r[...] for r in refs[:n_in]])
        for r, v in zip(refs[n_in : n_in + n_or], outs):
            r[...] = v.astype(r.dtype)
        if out_accs:
            acc_refs = refs[n_in + n_or :]

            @pl.when(pl.program_id(0) == 0)
            def _():
                for r in acc_refs:
                    r[...] = jnp.zeros(r.shape, r.dtype)

            for r, v in zip(acc_refs, accs):
                r[...] += v

    in_specs = [pl.BlockSpec((tm, w), lambda i, cb=cb, rb=r0 // tm: (i + rb, cb)) for (_, w, cb, r0) in rows]
    in_specs += [pl.BlockSpec(v.shape, lambda i, nd=v.ndim: (0,) * nd) for v in vecs]
    out_specs = [pl.BlockSpec((tm, w), lambda i: (i, 0)) for (w, _) in out_rows]
    out_specs += [pl.BlockSpec(sh, lambda i, nd=len(sh): (0,) * nd) for sh in out_accs]
    out_shape = [jax.ShapeDtypeStruct((s, w), dt) for (w, dt) in out_rows]
    out_shape += [jax.ShapeDtypeStruct(sh, F32) for sh in out_accs]
    res = pl.pallas_call(
        body, name=name, grid=(s // tm,), in_specs=in_specs, out_specs=out_specs, out_shape=out_shape,
        compiler_params=_cparams(("arbitrary",)),
    )(*[r[0] for r in rows], *vecs)
    return res


def _single(name, fn, ins, out_shapes):
    n_in = len(ins)

    def body(*refs):
        outs = fn(*[r[...] for r in refs[:n_in]])
        for r, v in zip(refs[n_in:], outs):
            r[...] = v.astype(r.dtype)

    vm = pl.BlockSpec(memory_space=pltpu.VMEM)
    return pl.pallas_call(
        body, name=name, in_specs=[vm] * n_in, out_specs=[vm] * len(out_shapes),
        out_shape=[jax.ShapeDtypeStruct(sh, dt) for sh, dt in out_shapes], compiler_params=_cparams(),
    )(*ins)


_NN = (((1,), (0,)), ((), ()))
_NT = (((1,), (1,)), ((), ()))
_TN = (((0,), (0,)), ((), ()))


def _mm(name, a, b, mode, out_dtype, *, tm=512, tn=1024, tk=2048, b_cs=False, o_cs=False, epi=None, extras=()):
    if mode == "tn":
        k, m = a.shape
        n = b.shape[1]
    else:
        m, k = a.shape
        if mode == "nn":
            n = b.shape[2] * 4 if b_cs else b.shape[1]
        else:
            n = b.shape[1] if b_cs else b.shape[0]
    tm = _blk(m, tm)
    tn = _blk(n // 4 if (b_cs and mode == "nn") or o_cs else n, tn)
    tk = _blk(k // 4 if (b_cs and mode == "nt") else k, tk)
    nk = k // tk
    dims = {"nn": _NN, "nt": _NT, "tn": _TN}[mode]
    n_ex = len(extras)

    def body(*refs):
        a_ref, b_ref = refs[0], refs[1]
        ex_refs, o_ref = refs[2 : 2 + n_ex], refs[2 + n_ex]

        def finish(acc):
            if epi is not None:
                acc = epi(acc, *[r[...] for r in ex_refs])
            o_ref[...] = acc.astype(o_ref.dtype)

        part = lax.dot_general(a_ref[...], b_ref[...], dims, preferred_element_type=F32)
        if nk == 1:
            finish(part)
        else:
            acc_ref = refs[3 + n_ex]
            kk = pl.program_id(2)

            @pl.when(kk == 0)
            def _():
                acc_ref[...] = part

            @pl.when(kk > 0)
            def _():
                acc_ref[...] += part

            @pl.when(kk == nk - 1)
            def _():
                finish(acc_ref[...])

    a_spec = pl.BlockSpec((tk, tm), lambda i, j, kk: (kk, i)) if mode == "tn" else pl.BlockSpec((tm, tk), lambda i, j, kk: (i, kk))
    if mode == "nn":
        if b_cs:
            per = (n // 4) // tn
            b_spec = pl.BlockSpec((None, tk, tn), lambda i, j, kk: (j // per, kk, j % per))
        else:
            b_spec = pl.BlockSpec((tk, tn), lambda i, j, kk: (kk, j))
    elif mode == "nt":
        if b_cs:
            per = (k // 4) // tk
            b_spec = pl.BlockSpec((None, tn, tk), lambda i, j, kk: (kk // per, j, kk % per))
        else:
            b_spec = pl.BlockSpec((tn, tk), lambda i, j, kk: (j, kk))
    else:
        b_spec = pl.BlockSpec((tk, tn), lambda i, j, kk: (kk, j))
    if o_cs:
        per = (n // 4) // tn
        o_spec = pl.BlockSpec((None, tm, tn), lambda i, j, kk: (j // per, i, j % per))
        o_shape = jax.ShapeDtypeStruct((4, m, n // 4), out_dtype)
    else:
        o_spec = pl.BlockSpec((tm, tn), lambda i, j, kk: (i, j))
        o_shape = jax.ShapeDtypeStruct((m, n), out_dtype)
    ex_specs = [pl.BlockSpec((tm, tn), lambda i, j, kk: (i, j)) for _ in extras]
    return pl.pallas_call(
        body, name=name, grid=(m // tm, n // tn, nk), in_specs=[a_spec, b_spec] + ex_specs, out_specs=o_spec, out_shape=o_shape,
        scratch_shapes=[pltpu.VMEM((tm, tn), F32)] if nk > 1 else [],
        compiler_params=_cparams(("parallel", "parallel", "arbitrary")),
    )(a, b, *extras)


def _place():
    x, y, c = lax.axis_index("x"), lax.axis_index("y"), lax.axis_index("c")
    return x, y, c, [(1 - x, y), (x, 1 - y), (1 - x, 1 - y)]


def _allgather8(name, v):
    m_per, n = v.shape

    def body(x_ref, out_ref, send_sems, recv_sems, local_sem):
        x, y, c, chips = _place()
        me, sibling = (x, y, c), (x, y, 1 - c)

        def rows(px, py, pc):
            return out_ref.at[pl.ds((4 * px + 2 * py + pc) * m_per, m_per), :]

        def copy(k, block, to, src=None):
            return pltpu.make_async_remote_copy(
                src_ref=rows(*block) if src is None else src, dst_ref=rows(*block), send_sem=send_sems.at[k],
                recv_sem=recv_sems.at[k], device_id=to, device_id_type=MESH)

        mine = pltpu.make_async_copy(x_ref, rows(*me), local_sem)
        mine.start()
        first = [copy(0, me, sibling, src=x_ref)]
        first += [copy(1 + j, me, (*chip, c), src=x_ref) for j, chip in enumerate(chips)]
        for cp in first:
            cp.start()
        passed = [copy(4 + j, (*chip, c), sibling) for j, chip in enumerate(chips)]
        for j, chip in enumerate(chips):
            copy(1 + j, (*chip, c), me).wait_recv()
            passed[j].start()
        copy(0, sibling, me).wait_recv()
        for j, chip in enumerate(chips):
            copy(4 + j, (*chip, 1 - c), me).wait_recv()
        for cp in first + passed:
            cp.wait_send()
        mine.wait()

    return pl.pallas_call(
        body, name=name, out_shape=jax.ShapeDtypeStruct((8 * m_per, n), v.dtype),
        in_specs=[pl.BlockSpec(memory_space=pltpu.VMEM)], out_specs=pl.BlockSpec(memory_space=pltpu.VMEM),
        scratch_shapes=[pltpu.SemaphoreType.DMA((7,)), pltpu.SemaphoreType.DMA((7,)), pltpu.SemaphoreType.DMA],
        compiler_params=_cparams(),
    )(v)


def _gather_weights(stacks):
    nw = len(stacks)

    def body(*refs):
        outs = refs[nw : 2 * nw]
        send_sems, recv_sems = refs[2 * nw :]
        x, y, c, chips = _place()
        me, sibling = (x, y, c), (x, y, 1 - c)

        def copy(w, k, block, to):
            px, py, pc = block
            half = outs[w].shape[1] // 2
            rows = outs[w].at[2 * px + py, pl.ds(pc * half, half), :]
            return pltpu.make_async_remote_copy(src_ref=rows, dst_ref=rows, send_sem=send_sems.at[6 * w + k],
                                                recv_sem=recv_sems.at[6 * w + k], device_id=to, device_id_type=MESH)

        sent = [copy(w, j, me, (*chip, c)) for w in range(nw) for j, chip in enumerate(chips)]
        for cp in sent:
            cp.start()
        for w in range(nw):
            for j, chip in enumerate(chips):
                copy(w, j, (*chip, c), me).wait_recv()
                passed = copy(w, 3 + j, (*chip, c), sibling)
                passed.start()
                sent.append(passed)
        for w in range(nw):
            for j, chip in enumerate(chips):
                copy(w, 3 + j, (*chip, 1 - c), me).wait_recv()
        for cp in sent:
            cp.wait_send()

    hbm = pl.BlockSpec(memory_space=pl.ANY)
    return pl.pallas_call(
        body, name="gather_weights", out_shape=[jax.ShapeDtypeStruct(w.shape, w.dtype) for w in stacks],
        in_specs=[hbm] * nw, out_specs=[hbm] * nw, input_output_aliases={i: i for i in range(nw)},
        scratch_shapes=[pltpu.SemaphoreType.DMA((6 * nw,)), pltpu.SemaphoreType.DMA((6 * nw,))],
        compiler_params=_cparams(),
    )(*stacks)


def _swap_halves(gs, name):
    nw = len(gs)

    def body(*refs):
        ins, gots = refs[:nw], refs[nw : 2 * nw]
        send_sems, recv_sems = refs[2 * nw :]
        x, y, c, _ = _place()
        cps = []
        for w in range(nw):
            half = ins[w].shape[1] // 2
            cps.append(pltpu.make_async_remote_copy(
                src_ref=ins[w].at[:, pl.ds((1 - c) * half, half), :], dst_ref=gots[w], send_sem=send_sems.at[w],
                recv_sem=recv_sems.at[w], device_id=(x, y, 1 - c), device_id_type=MESH))
            cps[-1].start()
        for cp in cps:
            cp.wait()

    hbm = pl.BlockSpec(memory_space=pl.ANY)
    return pl.pallas_call(
        body, name=name, out_shape=[jax.ShapeDtypeStruct((4, g.shape[1] // 2, g.shape[2]), g.dtype) for g in gs],
        in_specs=[hbm] * nw, out_specs=[hbm] * nw,
        scratch_shapes=[pltpu.SemaphoreType.DMA((nw,)), pltpu.SemaphoreType.DMA((nw,))], compiler_params=_cparams(),
    )(*gs)


def _scatter_chips(ps):
    nw = len(ps)

    def body(*refs):
        ins, gots = refs[:nw], refs[nw : 2 * nw]
        send_sems, recv_sems = refs[2 * nw :]
        x, y, c, chips = _place()
        cps = []
        for w in range(nw):
            for j, (px, py) in enumerate(chips):
                cps.append(pltpu.make_async_remote_copy(
                    src_ref=ins[w].at[2 * px + py], dst_ref=gots[w].at[j], send_sem=send_sems.at[3 * w + j],
                    recv_sem=recv_sems.at[3 * w + j], device_id=(px, py, c), device_id_type=MESH))
                cps[-1].start()
        for cp in cps:
            cp.wait()

    hbm = pl.BlockSpec(memory_space=pl.ANY)
    return pl.pallas_call(
        body, name="scatter_chips", out_shape=[jax.ShapeDtypeStruct((3,) + p.shape[1:], p.dtype) for p in ps],
        in_specs=[hbm] * nw, out_specs=[hbm] * nw,
        scratch_shapes=[pltpu.SemaphoreType.DMA((3 * nw,)), pltpu.SemaphoreType.DMA((3 * nw,))], compiler_params=_cparams(),
    )(*ps)


def _split_start(name, bufs, plan, n):
    nb = len(bufs)

    def body(*refs):
        send_sems, recv_sems, token = refs[nb], refs[nb + 1], refs[-1]
        for k, (src, dst, to, _) in enumerate(plan(refs[:nb])):
            pltpu.make_async_remote_copy(src_ref=src, dst_ref=dst, send_sem=send_sems.at[k], recv_sem=recv_sems.at[k],
                                         device_id=to, device_id_type=MESH).start()
        token[...] = jnp.zeros(token.shape, token.dtype)

    hbm, sem = pl.BlockSpec(memory_space=pltpu.HBM), pl.BlockSpec(memory_space=pltpu.SEMAPHORE)
    res = pl.pallas_call(
        body, name=name,
        out_shape=(pltpu.SemaphoreType.DMA((n,)), pltpu.SemaphoreType.DMA((n,)), *[pltpu.HBM(b.shape, b.dtype) for b in bufs],
                   jax.ShapeDtypeStruct((SUBLANES, 128), F32)),
        in_specs=(hbm,) * nb, out_specs=(sem, sem) + (hbm,) * nb + (pl.BlockSpec(memory_space=pltpu.VMEM),),
        input_output_aliases={i: 2 + i for i in range(nb)},
        compiler_params=pltpu.CompilerParams(has_side_effects=pltpu.SideEffectType.DATAFLOW_SIDE_EFFECTING),
    )(*[pltpu.with_memory_space_constraint(b, pltpu.HBM) for b in bufs])
    return res[0], res[1], list(res[2 : 2 + nb]), res[-1]


def _split_wait(name, bufs, send_sems, recv_sems, after, plan):
    nb = len(bufs)

    def body(*refs):
        send, recv = refs[nb], refs[nb + 1]
        for k, (src, _, to, land) in enumerate(plan(refs[:nb])):
            cp = pltpu.make_async_remote_copy(src_ref=src, dst_ref=land, send_sem=send.at[k], recv_sem=recv.at[k],
                                              device_id=to, device_id_type=MESH)
            cp.wait_send()
            cp.wait_recv()

    hbm, sem = pl.BlockSpec(memory_space=pltpu.HBM), pl.BlockSpec(memory_space=pltpu.SEMAPHORE)
    return pl.pallas_call(
        body, name=name, out_shape=tuple(pltpu.HBM(b.shape, b.dtype) for b in bufs),
        in_specs=(hbm,) * nb + (sem, sem, pl.BlockSpec(memory_space=pl.ANY)), out_specs=(hbm,) * nb,
        input_output_aliases={i: i for i in range(nb)},
        compiler_params=pltpu.CompilerParams(has_side_effects=pltpu.SideEffectType.DATAFLOW_SIDE_EFFECTING),
    )(*bufs, send_sems, recv_sems, after)


def _indexed(name, fn, idx, ins, out, grid):
    def body(idx_ref, *refs):
        refs[-1][...] = fn(*[r[...] for r in refs[:-1]]).astype(refs[-1].dtype)

    (o_shape, o_dtype), o_block, o_map = out
    return pl.pallas_call(
        body, name=name, out_shape=jax.ShapeDtypeStruct(o_shape, o_dtype),
        grid_spec=pltpu.PrefetchScalarGridSpec(
            num_scalar_prefetch=1, grid=grid, in_specs=[pl.BlockSpec(blk, mp) for (_, blk, mp) in ins],
            out_specs=pl.BlockSpec(o_block, o_map)),
        compiler_params=_cparams(("arbitrary",) * len(grid)),
    )(idx, *[a for (a, _, _) in ins])


def _join_halves(fs):
    nw = len(fs)

    def body(*refs):
        ins, outs, lands = refs[:nw], refs[nw : 2 * nw], refs[2 * nw : 3 * nw]
        send_sems, recv_sems, local_sems = refs[3 * nw :]
        x, y, c, _ = _place()
        sibling = (x, y, 1 - c)
        cps = [pltpu.make_async_remote_copy(src_ref=ins[w], dst_ref=lands[w], send_sem=send_sems.at[w], recv_sem=recv_sems.at[w],
                                            device_id=sibling, device_id_type=MESH) for w in range(nw)]
        outs_own = [pltpu.make_async_copy(ins[w], outs[w].at[c], local_sems.at[w]) for w in range(nw)]
        outs_got = [pltpu.make_async_copy(lands[w], outs[w].at[1 - c], local_sems.at[nw + w]) for w in range(nw)]
        for cp in cps + outs_own:
            cp.start()
        for w in range(nw):
            cps[w].wait_recv()
            outs_got[w].start()
        for w in range(nw):
            cps[w].wait_send()
            outs_own[w].wait()
            outs_got[w].wait()

    hbm, vm = pl.BlockSpec(memory_space=pl.ANY), pl.BlockSpec(memory_space=pltpu.VMEM)
    return pl.pallas_call(
        body, name="join_halves", out_shape=[jax.ShapeDtypeStruct((2,) + f.shape, f.dtype) for f in fs],
        in_specs=[vm] * nw, out_specs=[hbm] * nw,
        scratch_shapes=[pltpu.VMEM(f.shape, f.dtype) for f in fs]
        + [pltpu.SemaphoreType.DMA((nw,)), pltpu.SemaphoreType.DMA((nw,)), pltpu.SemaphoreType.DMA((2 * nw,))],
        compiler_params=pltpu.CompilerParams(vmem_limit_bytes=JOIN_VMEM_LIMIT),
    )(*fs)


def _rstd(v):
    return lax.rsqrt(jnp.mean(v * v, axis=-1, keepdims=True) + EPS)


def _rms_bwd(dy, xh, r):
    return r * (dy - xh * jnp.mean(dy * xh, axis=-1, keepdims=True))


def _heads(v):
    return [v[:, h * HEAD_DIM : (h + 1) * HEAD_DIM] for h in range(v.shape[1] // HEAD_DIM)]


def _gelu(v):
    k = math.sqrt(2.0 / math.pi)
    return 0.5 * v * (1.0 + jnp.tanh(k * (v + 0.044715 * v * v * v)))


def _gelu_grad(v):
    k = math.sqrt(2.0 / math.pi)
    t = jnp.tanh(k * (v + 0.044715 * v * v * v))
    return 0.5 * (1.0 + t) + 0.5 * v * (1.0 - t * t) * k * (1.0 + 3 * 0.044715 * v * v)


def _sigmoid(v):
    return 1.0 / (1.0 + jnp.exp(-v))


def _slopes(n_heads):
    return [2.0 ** (-8.0 * (h + 1.0) / n_heads) for h in range(n_heads)]


def _band_iotas():
    ri = lax.broadcasted_iota(jnp.int32, (BAND, BAND), 0)
    ci = lax.broadcasted_iota(jnp.int32, (BAND, BAND), 1)
    return ri, ci


def _attn_fwd(name, q, k, v, bps, dil):
    s, aw = q.shape
    n_heads, nb = aw // HEAD_DIM, s // BAND
    scale = HEAD_DIM**-0.5
    slopes = _slopes(n_heads)

    def body(q_ref, kp_ref, kc_ref, vp_ref, vc_ref, o_ref, l_ref):
        b = pl.program_id(0)
        first = (b % bps) == 0
        ri, ci = _band_iotas()
        dist_c = (ri - ci).astype(F32)
        dist_p = (ri - ci + BAND).astype(F32)
        valid_c = ci <= ri
        valid_p = ci >= ri + jnp.where(first, BAND, 0)
        hs = range(n_heads)
        sls = [slice(h * HEAD_DIM, (h + 1) * HEAD_DIM) for h in hs]
        qs = [q_ref[:, sl] for sl in sls]
        sc = [lax.dot_general(qs[h], kc_ref[:, sls[h]], _NT, preferred_element_type=F32) for h in hs]
        sp = [lax.dot_general(qs[h], kp_ref[:, sls[h]], _NT, preferred_element_type=F32) for h in hs]
        sc = [jnp.where(valid_c, sc[h] * scale - (slopes[h] * dil) * dist_c, NEG) for h in hs]
        sp = [jnp.where(valid_p, sp[h] * scale - (slopes[h] * dil) * dist_p, NEG) for h in hs]
        m = [jnp.maximum(jnp.max(sc[h], axis=1, keepdims=True), jnp.max(sp[h], axis=1, keepdims=True)) for h in hs]
        pc = [jnp.exp(sc[h] - m[h]) for h in hs]
        pp = [jnp.exp(sp[h] - m[h]) for h in hs]
        den = [jnp.sum(pc[h], axis=1, keepdims=True) + jnp.sum(pp[h], axis=1, keepdims=True) for h in hs]
        acc = [jnp.dot(pc[h].astype(BF16), vc_ref[:, sls[h]], preferred_element_type=F32) for h in hs]
        acc = [acc[h] + jnp.dot(pp[h].astype(BF16), vp_ref[:, sls[h]], preferred_element_type=F32) for h in hs]
        o_ref[...] = jnp.concatenate([acc[h] / den[h] for h in hs], axis=1)
        l_ref[...] = jnp.concatenate([jnp.broadcast_to(m[h] + jnp.log(den[h]), (BAND, HEAD_DIM)) for h in hs], axis=1)

    cur = pl.BlockSpec((BAND, aw), lambda b: (b, 0))
    prev = pl.BlockSpec((BAND, aw), lambda b: (jnp.maximum(b - 1, 0), 0))
    return pl.pallas_call(
        body, name=name, grid=(nb,), in_specs=[cur, prev, cur, prev, cur], out_specs=[cur, cur],
        out_shape=[jax.ShapeDtypeStruct((s, aw), F32)] * 2, compiler_params=_cparams(("arbitrary",)),
    )(q, k, k, v, v)


def _attn_bwd(name, q, k, v, do, lse, delta, bps, dil):
    s, aw = q.shape
    n_heads, nb = aw // HEAD_DIM, s // BAND
    scale = HEAD_DIM**-0.5
    slopes = _slopes(n_heads)

    def body(qc_ref, qn_ref, doc_ref, don_ref, lc_ref, ln_ref, dc_ref, dn_ref, k_ref, v_ref, dq_ref, dk_ref, dv_ref, carry):
        j = pl.program_id(0)

        @pl.when(j == 0)
        def _():
            carry[...] = jnp.zeros(carry.shape, F32)

        has_next = jnp.logical_and((j + 1) % bps != 0, j + 1 < nb)
        ri, ci = _band_iotas()
        dist_c = (ri - ci).astype(F32)
        dist_p = (ri - ci + BAND).astype(F32)
        valid_c = ci <= ri
        valid_n = ci >= ri + jnp.where(has_next, 0, BAND)
        carried = carry[...]
        hs = range(n_heads)
        sls = [slice(h * HEAD_DIM, (h + 1) * HEAD_DIM) for h in hs]
        ks, vs = [k_ref[:, sl] for sl in sls], [v_ref[:, sl] for sl in sls]
        sides = [(qc_ref, doc_ref, lc_ref, dc_ref, dist_c, valid_c), (qn_ref, don_ref, ln_ref, dn_ref, dist_p, valid_n)]
        res = []
        for q_ref, do_ref, l_ref, d_ref, dist, valid in sides:
            qs, dos = [q_ref[:, sl] for sl in sls], [do_ref[:, sl] for sl in sls]
            sc = [lax.dot_general(qs[h], ks[h], _NT, preferred_element_type=F32) for h in hs]
            dp = [lax.dot_general(dos[h], vs[h], _NT, preferred_element_type=F32) for h in hs]
            p = [jnp.where(valid, jnp.exp(sc[h] * scale - (slopes[h] * dil) * dist - l_ref[:, sls[h]]), 0.0) for h in hs]
            ds = [(p[h] * (dp[h] - d_ref[:, sls[h]])).astype(BF16) for h in hs]
            pb = [p[h].astype(BF16) for h in hs]
            dv = [lax.dot_general(pb[h], dos[h], _TN, preferred_element_type=F32) for h in hs]
            dk = [lax.dot_general(ds[h], qs[h], _TN, preferred_element_type=F32) for h in hs]
            dq = [jnp.dot(ds[h], ks[h], preferred_element_type=F32) for h in hs]
            res.append((dq, dk, dv))
        (dq_c, dk_c, dv_c), (dq_n, dk_n, dv_n) = res
        dq_ref[...] = carried + jnp.concatenate(dq_c, axis=1) * scale
        carry[...] = jnp.concatenate(dq_n, axis=1) * scale
        dk_ref[...] = jnp.concatenate([dk_c[h] + dk_n[h] for h in hs], axis=1) * scale
        dv_ref[...] = jnp.concatenate([dv_c[h] + dv_n[h] for h in hs], axis=1)

    cur = pl.BlockSpec((BAND, aw), lambda b: (b, 0))
    nxt = pl.BlockSpec((BAND, aw), lambda b: (jnp.minimum(b + 1, nb - 1), 0))
    return pl.pallas_call(
        body, name=name, grid=(nb,), in_specs=[cur, nxt, cur, nxt, cur, nxt, cur, nxt, cur, cur], out_specs=[cur] * 3,
        out_shape=[jax.ShapeDtypeStruct((s, aw), F32)] * 3, scratch_shapes=[pltpu.VMEM((BAND, aw), F32)],
        compiler_params=_cparams(("arbitrary",)),
    )(q, q, do, do, lse, lse, delta, delta, k, v)


def _to_streams(v, dil):
    if dil == 1:
        return v
    s, w = v.shape
    return v.reshape(s // dil, dil, w).transpose(1, 0, 2).reshape(s, w)


def _from_streams(v, dil):
    if dil == 1:
        return v
    s, w = v.shape
    return v.reshape(dil, s // dil, w).transpose(1, 0, 2).reshape(s, w)


SCAN_LANES = 1024
SSM_ROWS = 128


def _cmul(ar, ai, br, bi):
    return ar * br - ai * bi, ar * bi + ai * br


def _cmul_conj(ar, ai, br, bi):
    return ar * br + ai * bi, ar * bi - ai * br


def _ssm_fwd(u_src, bre, bim, cre, cim, consts, d_skip):
    u_arr, sw, ucb = u_src
    s = u_arr.shape[0]
    nblk = bre.shape[0]
    nst = nblk * 512
    t = _blk(s, SSM_ROWS)
    w = _blk(nst, SCAN_LANES)

    def body(u_ref, bre_ref, bim_ref, cre_ref, cim_ref, k_ref, ds_ref, y_ref, hr_ref, hi_ref, xr, xi, car_r, car_i):
        @pl.when(pl.program_id(0) == 0)
        def _():
            car_r[...] = jnp.zeros(car_r.shape, F32)
            car_i[...] = jnp.zeros(car_i.shape, F32)

        u = u_ref[...]
        ub = u.astype(BF16)
        for b in range(nblk):
            ch, st = slice(b * 128, (b + 1) * 128), slice(b * 512, (b + 1) * 512)
            xr[:, st] = jnp.dot(ub[:, ch], bre_ref[b], preferred_element_type=F32)
            xi[:, st] = jnp.dot(ub[:, ch], bim_ref[b], preferred_element_type=F32)
        for c in range(nst // w):
            ls = slice(c * w, (c + 1) * w)
            mult = [k_ref[i, :, ls] for i in range(8)]

            def step(i, carry, ls=ls, mult=mult):
                cr, ci = carry
                r0 = pl.multiple_of(i * SUBLANES, SUBLANES)
                hr, hi = xr[pl.ds(r0, SUBLANES), ls], xi[pl.ds(r0, SUBLANES), ls]
                for n, sh in enumerate((1, 2, 4)):
                    pr, pi = _cmul(mult[2 * n], mult[2 * n + 1], pltpu.roll(hr, sh, 0), pltpu.roll(hi, sh, 0))
                    hr, hi = hr + pr, hi + pi
                pr, pi = _cmul(mult[6], mult[7], cr, ci)
                hr, hi = hr + pr, hi + pi
                xr[pl.ds(r0, SUBLANES), ls] = hr
                xi[pl.ds(r0, SUBLANES), ls] = hi
                return jnp.broadcast_to(hr[7:8, :], hr.shape), jnp.broadcast_to(hi[7:8, :], hi.shape)

            cr, ci = lax.fori_loop(0, t // SUBLANES, step, (car_r[:, ls], car_i[:, ls]))
            car_r[:, ls] = cr
            car_i[:, ls] = ci
        hrb, hib = xr[...].astype(BF16), xi[...].astype(BF16)
        hr_ref[...] = hrb
        hi_ref[...] = hib
        for b in range(nblk):
            ch, st = slice(b * 128, (b + 1) * 128), slice(b * 512, (b + 1) * 512)
            yb = jnp.dot(hrb[:, st], cre_ref[b], preferred_element_type=F32)
            yb -= jnp.dot(hib[:, st], cim_ref[b], preferred_element_type=F32)
            y_ref[:, ch] = yb + ds_ref[:, ch] * u[:, ch]

    whole = lambda a: pl.BlockSpec(a.shape, lambda i, nd=a.ndim: (0,) * nd)
    return pl.pallas_call(
        body, name="ssm_fwd", grid=(s // t,),
        in_specs=[pl.BlockSpec((t, sw), lambda i: (i, ucb))] + [whole(a) for a in (bre, bim, cre, cim, consts, d_skip)],
        out_specs=[pl.BlockSpec((t, sw), lambda i: (i, 0)), pl.BlockSpec((t, nst), lambda i: (i, 0)), pl.BlockSpec((t, nst), lambda i: (i, 0))],
        out_shape=[jax.ShapeDtypeStruct((s, sw), F32), jax.ShapeDtypeStruct((s, nst), BF16), jax.ShapeDtypeStruct((s, nst), BF16)],
        scratch_shapes=[pltpu.VMEM((t, nst), F32), pltpu.VMEM((t, nst), F32), pltpu.VMEM((SUBLANES, nst), F32), pltpu.VMEM((SUBLANES, nst), F32)],
        compiler_params=_cparams(("arbitrary",)),
    )(u_arr, bre, bim, cre, cim, consts, d_skip)


def _ssm_bwd(dy, u_src, hr, hi, ctre, ctim, btre, btim, consts, d_skip):
    u_arr, sw, ucb = u_src
    s = dy.shape[0]
    nblk = ctre.shape[0]
    nst = nblk * 512
    t = _blk(s, SSM_ROWS)
    w = _blk(nst, SCAN_LANES)
    nt = s // t

    def body(dy_ref, u_ref, hr_ref, hi_ref, ctre_ref, ctim_ref, btre_ref, btim_ref, k_ref, ds_ref,
             du_ref, db_ref, dc_ref, da_ref, dd_ref, lr, li, hfr, hfi, car_r, car_i):
        @pl.when(pl.program_id(0) == 0)
        def _():
            car_r[...] = jnp.zeros(car_r.shape, F32)
            car_i[...] = jnp.zeros(car_i.shape, F32)
            db_ref[...] = jnp.zeros(db_ref.shape, F32)
            dc_ref[...] = jnp.zeros(dc_ref.shape, F32)
            da_ref[...] = jnp.zeros(da_ref.shape, F32)
            dd_ref[...] = jnp.zeros(dd_ref.shape, F32)

        dyv, u = dy_ref[...], u_ref[...]
        dyb, ub = dyv.astype(BF16), u.astype(BF16)
        hrb, hib = hr_ref[...], hi_ref[...]
        hfr[...] = hrb.astype(F32)
        hfi[...] = hib.astype(F32)
        for b in range(nblk):
            ch, st = slice(b * 128, (b + 1) * 128), slice(b * 512, (b + 1) * 512)
            lr[:, st] = jnp.dot(dyb[:, ch], ctre_ref[b], preferred_element_type=F32)
            li[:, st] = -jnp.dot(dyb[:, ch], ctim_ref[b], preferred_element_type=F32)
        last_row = lax.broadcasted_iota(jnp.int32, (SUBLANES, w), 0) == SUBLANES - 1
        for c in range(nst // w):
            ls = slice(c * w, (c + 1) * w)
            mult = [k_ref[i, :, ls] for i in range(8)]

            def step(i, carry, ls=ls, mult=mult):
                cr, ci, ar, ai = carry
                r0 = pl.multiple_of((t // SUBLANES - 1 - i) * SUBLANES, SUBLANES)
                gr, gi = lr[pl.ds(r0, SUBLANES), ls], li[pl.ds(r0, SUBLANES), ls]
                for n, sh in enumerate((1, 2, 4)):
                    pr, pi = _cmul_conj(mult[2 * n], mult[2 * n + 1], pltpu.roll(gr, SUBLANES - sh, 0), pltpu.roll(gi, SUBLANES - sh, 0))
                    gr, gi = gr + pr, gi + pi
                pr, pi = _cmul_conj(mult[6], mult[7], cr, ci)
                gr, gi = gr + pr, gi + pi
                lr[pl.ds(r0, SUBLANES), ls] = gr
                li[pl.ds(r0, SUBLANES), ls] = gi
                nr = jnp.where(last_row, cr, pltpu.roll(gr, SUBLANES - 1, 0))
                ni = jnp.where(last_row, ci, pltpu.roll(gi, SUBLANES - 1, 0))
                fr, fi = hfr[pl.ds(r0, SUBLANES), ls], hfi[pl.ds(r0, SUBLANES), ls]
                pr, pi = _cmul_conj(fr, fi, nr, ni)
                return (jnp.broadcast_to(gr[0:1, :], gr.shape), jnp.broadcast_to(gi[0:1, :], gi.shape), ar + pr, ai + pi)

            zero = jnp.zeros((SUBLANES, w), F32)
            cr, ci, ar, ai = lax.fori_loop(0, t // SUBLANES, step, (car_r[:, ls], car_i[:, ls], zero, zero))
            car_r[:, ls] = cr
            car_i[:, ls] = ci
            da_ref[0, :, ls] += ar
            da_ref[1, :, ls] += ai
        lrb, lib = lr[...].astype(BF16), li[...].astype(BF16)
        for b in range(nblk):
            ch, st = slice(b * 128, (b + 1) * 128), slice(b * 512, (b + 1) * 512)
            dub = jnp.dot(lrb[:, st], btre_ref[b], preferred_element_type=F32)
            dub += jnp.dot(lib[:, st], btim_ref[b], preferred_element_type=F32)
            du_ref[:, ch] = dub + ds_ref[:, ch] * dyv[:, ch]
            db_ref[0, b] += lax.dot_general(ub[:, ch], lrb[:, st], _TN, preferred_element_type=F32)
            db_ref[1, b] += lax.dot_general(ub[:, ch], lib[:, st], _TN, preferred_element_type=F32)
            dc_ref[0, b] += lax.dot_general(dyb[:, ch], hrb[:, st], _TN, preferred_element_type=F32)
            dc_ref[1, b] -= lax.dot_general(dyb[:, ch], hib[:, st], _TN, preferred_element_type=F32)
        dd_ref[...] += _colsum8(dyv * u)

    whole = lambda a: pl.BlockSpec(a.shape, lambda i, nd=a.ndim: (0,) * nd)
    rev = lambda wd, cb=0: pl.BlockSpec((t, wd), lambda i, cb=cb: (nt - 1 - i, cb))
    acc = lambda sh: pl.BlockSpec(sh, lambda i, nd=len(sh): (0,) * nd)
    db_shape, da_shape, dd_shape = (2, nblk, 128, 512), (2, SUBLANES, nst), (SUBLANES, sw)
    return pl.pallas_call(
        body, name="ssm_bwd", grid=(nt,),
        in_specs=[rev(sw), rev(sw, ucb), rev(nst), rev(nst)] + [whole(a) for a in (ctre, ctim, btre, btim, consts, d_skip)],
        out_specs=[rev(sw), acc(db_shape), acc(db_shape), acc(da_shape), acc(dd_shape)],
        out_shape=[jax.ShapeDtypeStruct((s, sw), F32), jax.ShapeDtypeStruct(db_shape, F32), jax.ShapeDtypeStruct(db_shape, F32),
                   jax.ShapeDtypeStruct(da_shape, F32), jax.ShapeDtypeStruct(dd_shape, F32)],
        scratch_shapes=[pltpu.VMEM((t, nst), F32)] * 4 + [pltpu.VMEM((SUBLANES, nst), F32)] * 2,
        compiler_params=_cparams(("arbitrary",)),
    )(dy, u_arr, hr, hi, ctre, ctim, btre, btim, consts, d_skip)


def _ssm_discretise(lam_re, lam_im, log_step):
    step = jnp.exp(log_step)
    e = jnp.exp(lam_re * step)
    ar, ai = e * jnp.cos(lam_im * step), e * jnp.sin(lam_im * step)
    den = lam_re * lam_re + lam_im * lam_im
    inv_r, inv_i = lam_re / den, -lam_im / den
    fr, fi = _cmul(ar - 1.0, ai, inv_r, inv_i)
    return step, ar, ai, inv_r, inv_i, fr, fi


def _ssm_prep_fn(lam_re, lam_im, log_step):
    _, ar, ai, _, _, fr, fi = _ssm_discretise(lam_re, lam_im, log_step)
    pw = [(ar, ai)]
    for n in range(1, 8):
        pw.append(_cmul(*pw[n - 1], ar, ai))
    return [p[0] for p in pw] + [p[1] for p in pw] + [fr, fi]


def _ssm_param_grad_fn(lam_re, lam_im, log_step, da_r, da_i, df_r, df_i):
    step, ar, ai, inv_r, inv_i, fr, fi = _ssm_discretise(lam_re, lam_im, log_step)
    pr, pi = _cmul_conj(inv_r, inv_i, df_r, df_i)
    dat_r, dat_i = da_r + pr, da_i + pi
    wr, wi = _cmul(fr, fi, inv_r, inv_i)
    dl_r, dl_i = _cmul_conj(-wr, -wi, df_r, df_i)
    dz_r, dz_i = _cmul_conj(ar, ai, dat_r, dat_i)
    dl_r, dl_i = dl_r + step * dz_r, dl_i + step * dz_i
    dstep = jnp.sum(dz_r * lam_re + dz_i * lam_im, axis=-1, keepdims=True)
    return [dl_r, dl_i, step * dstep]


def _block_diag(v, rows_first):
    g, a, b = v.shape
    nb = g // GROUPS_PER_BLOCK
    eye = jnp.eye(GROUPS_PER_BLOCK, dtype=v.dtype)
    v = v.reshape(nb, GROUPS_PER_BLOCK, a, 1, b) * eye[None, :, None, :, None]
    return v.reshape(nb, GROUPS_PER_BLOCK * a, GROUPS_PER_BLOCK * b)


def _block_diag_take(v, a, b):
    nb = v.shape[0]
    v = v.reshape(nb, GROUPS_PER_BLOCK, a, GROUPS_PER_BLOCK, b)
    idx = jnp.arange(GROUPS_PER_BLOCK)
    return v[:, idx, :, idx, :].transpose(1, 0, 2, 3).reshape(nb * GROUPS_PER_BLOCK, a, b)


def _adamw(w, g, m, v):
    m = ADAM_B1 * m + (1.0 - ADAM_B1) * g
    v = ADAM_B2 * v + (1.0 - ADAM_B2) * (g * g)
    m_hat = m / (1.0 - ADAM_B1**ADAM_STEP)
    v_hat = v / (1.0 - ADAM_B2**ADAM_STEP)
    delta = -ADAM_LR * (m_hat / (jnp.sqrt(v_hat) + ADAM_EPS) + ADAM_WD * w)
    return delta, m, v


def _adamw_rows(name, w, g, m, v):
    wd = w.shape[1]
    return _rowwise(name, lambda a, b, c, d: (list(_adamw(a, b, c, d)), []), [w, g, m, v], [], [(wd, F32)] * 3, [], tm=128)


SMALL = ["b_ada", "norm1_g", "q_norm_g", "k_norm_g", "lam_re", "lam_im", "log_step", "b_re", "b_im", "c_re", "c_im",
         "d_skip", "b_glu", "attn_out_g", "ssm_out_g", "norm2_g"]
LARGE = ["w_in", "w_glu", "w_out", "w_ff1", "w_ff2"]
ORDER = ["w_ada", "b_ada", "norm1_g", "w_in", "q_norm_g", "k_norm_g", "lam_re", "lam_im", "log_step", "b_re", "b_im", "c_re",
         "c_im", "d_skip", "w_glu", "b_glu", "attn_out_g", "ssm_out_g", "w_out", "norm2_g", "w_ff1", "w_ff2"]


def _pack(arrs):
    flat = jnp.concatenate([a.reshape(-1) for a in arrs])
    rows = -(-flat.shape[0] // PACK_LANES)
    rows = -(-rows // SUBLANES) * SUBLANES
    return jnp.pad(flat, (0, rows * PACK_LANES - flat.shape[0])).reshape(rows, PACK_LANES)


def _unpack(packed, shapes):
    flat, out, o = packed.reshape(-1), [], 0
    for sh in shapes:
        n = int(np.prod(sh))
        out.append(flat[o : o + n].reshape(sh))
        o += n
    return out


def kernel(x, c, w_ada, b_ada, norm1_g, w_in, q_norm_g, k_norm_g, lam_re, lam_im, log_step, b_re, b_im, c_re, c_im, d_skip, w_glu, b_glu, attn_out_g, ssm_out_g, w_out, norm2_g, w_ff1, w_ff2, loss_target, m_w_ada, m_b_ada, m_norm1_g, m_w_in, m_q_norm_g, m_k_norm_g, m_lam_re, m_lam_im, m_log_step, m_b_re, m_b_im, m_c_re, m_c_im, m_d_skip, m_w_glu, m_b_glu, m_attn_out_g, m_ssm_out_g, m_w_out, m_norm2_g, m_w_ff1, m_w_ff2, v_w_ada, v_b_ada, v_norm1_g, v_w_in, v_q_norm_g, v_k_norm_g, v_lam_re, v_lam_im, v_log_step, v_b_re, v_b_im, v_c_re, v_c_im, v_d_skip, v_w_glu, v_b_glu, v_attn_out_g, v_ssm_out_g, v_w_out, v_norm2_g, v_w_ff1, v_w_ff2):
    args = dict(locals())
    wts = {n: args[n] for n in ORDER}
    mom = {n: args["m_" + n] for n in ORDER}
    var = {n: args["v_" + n] for n in ORDER}

    x2, tgt = x[0], loss_target[0]
    s, d = x2.shape
    aw = d // 2
    sw = d - aw
    n_groups = sw // SSM_GROUP
    nst = n_groups * STATE_DIM
    chip = 2 * lax.axis_index("x") + lax.axis_index("y")

    chip_idx = jnp.reshape(chip, (1,)).astype(jnp.int32)
    core_idx = jnp.reshape(lax.axis_index("c"), (1,)).astype(jnp.int32)

    def cast_into_stack(n):
        w2 = wts[n][0]
        rows, cols = w2.shape
        tr = _blk(rows, 256)
        return _indexed("cast_" + n, lambda a: a, chip_idx, [(w2, (tr, cols), lambda r, ix: (r, 0))],
                        (((4, rows, cols), BF16), (None, tr, cols), lambda r, ix: (ix[0], r, 0)), (rows // tr,))

    stacks = {n: cast_into_stack(n) for n in LARGE}
    (w_in_s,) = _gather_weights([stacks["w_in"]])

    def gather_plan(refs):
        px0, py0, pc, chips = _place()
        slot = 2 * px0 + py0
        return [(r.at[slot], r.at[slot], (px, py, pc), r.at[2 * px + py]) for r in refs for (px, py) in chips]

    mix_sems_s, mix_sems_r, mix_stacks, mix_token = _split_start("gather_mix_start", [stacks["w_glu"], stacks["w_out"]], gather_plan, 6)
    ff_sems_s, ff_sems_r, ff_stacks, ff_token = _split_start("gather_ff_start", [stacks["w_ff1"], stacks["w_ff2"]], gather_plan, 6)
    dff = 4 * wts["w_ff2"].shape[1]

    c_all = _allgather8("gather_c", jnp.pad(c, ((0, SUBLANES - 1), (0, 0))))[::SUBLANES]
    n_mod = w_ada.shape[2]
    b_ada_mine = lax.dynamic_slice(b_ada, (0, chip * n_mod), (1, n_mod))
    sil_c = _single("silu_c", lambda v: [v * _sigmoid(v)], [c_all], [(c_all.shape, F32)])[0]
    mod_part = _mm("ada_fwd", sil_c, w_ada[0], "nn", F32, tn=512, epi=lambda acc, b: acc + b[0:1, :],
                   extras=[jnp.broadcast_to(b_ada_mine, (SUBLANES, n_mod))])
    mod_all = _allgather8("gather_mod", mod_part)
    me = 2 * chip + lax.axis_index("c")
    mod = jnp.concatenate([lax.dynamic_slice(mod_all, (16 * k + me, 0), (1, n_mod)) for k in range(4)], axis=1)
    sh1, sc1, g1, sh2, sc2, g2 = [mod[:, i * d : (i + 1) * d] for i in range(6)]
    sc1 = sc1 + (mix_token[0:1, 0:1] + ff_token[0:1, 0:1])

    def norm_mod(v, g, sc, sh):
        return [v * _rstd(v) * g * (1.0 + sc) + sh], []

    (h1b,) = _rowwise("norm1", norm_mod, [x2], [norm1_g, sc1, sh1], [(d, BF16)], [])
    proj = _mm("proj_in", h1b, w_in_s, "nn", F32, b_cs=True)
    assert aw == sw
    q_src, k_src, v_src, u_src = [(proj, aw, i) for i in range(3)] + [(proj, sw, 3)]

    def qk_prep(q, k, v, gq, gk):
        qn = jnp.concatenate([h * _rstd(h) * gq for h in _heads(q)], axis=1)
        kn = jnp.concatenate([h * _rstd(h) * gk for h in _heads(k)], axis=1)
        return [qn, kn, v], []

    qn, kn, vb = _rowwise("qk_prep", qk_prep, [q_src, k_src, v_src], [q_norm_g, k_norm_g], [(aw, BF16)] * 3, [])

    pats = []
    for window, dil in DILATION_PATTERNS:
        assert window // dil == BAND and s % (dil * BAND) == 0
        bps = s // dil // BAND
        qp, kp, vp = [_to_streams(t, dil) for t in (qn, kn, vb)]
        o_p, l_p = _attn_fwd(f"attn_fwd_d{dil}", qp, kp, vp, bps, dil)
        pats.append((dil, bps, qp, kp, vp, _from_streams(o_p, dil), _from_streams(l_p, dil)))

    def combine(o1, o2, o3, l1, l2, l3):
        m = jnp.maximum(jnp.maximum(l1, l2), l3)
        tot = m + jnp.log(jnp.exp(l1 - m) + jnp.exp(l2 - m) + jnp.exp(l3 - m))
        return [jnp.exp(l1 - tot) * o1 + jnp.exp(l2 - tot) * o2 + jnp.exp(l3 - tot) * o3, tot], []

    attn, lse = _rowwise("attn_mix", combine, [p[5] for p in pats] + [p[6] for p in pats], [], [(aw, F32)] * 2, [])

    lam_re2, lam_im2, log_step2 = lam_re[0], lam_im[0], log_step[0].reshape(n_groups, 1)
    prep = _single("ssm_prep", _ssm_prep_fn, [lam_re2, lam_im2, log_step2], [((n_groups, STATE_DIM), F32)] * 18)
    p_r = jnp.stack([p.reshape(nst) for p in prep[:8]])
    p_i = jnp.stack([p.reshape(nst) for p in prep[8:16]])
    f_ri = jnp.stack(prep[16:])
    row = jnp.arange(SUBLANES)[:, None]
    zero = jnp.zeros((SUBLANES, nst), F32)
    sel = lambda cond, v: jnp.where(cond, jnp.broadcast_to(v[None, :], (SUBLANES, nst)), zero)
    fwd_consts = jnp.stack([sel(row >= 1, p_r[0]), sel(row >= 1, p_i[0]), sel(row >= 2, p_r[1]), sel(row >= 2, p_i[1]),
                            sel(row >= 4, p_r[3]), sel(row >= 4, p_i[3]), p_r, p_i])
    bwd_consts = jnp.stack([sel(row <= 6, p_r[0]), sel(row <= 6, p_i[0]), sel(row <= 5, p_r[1]), sel(row <= 5, p_i[1]),
                            sel(row <= 3, p_r[3]), sel(row <= 3, p_i[3]), p_r[::-1], p_i[::-1]])
    nblk = n_groups // GROUPS_PER_BLOCK
    f_rows = f_ri.reshape(2, nblk, 1, 512)
    braw_r = _block_diag(b_re[0].transpose(0, 2, 1), True)
    braw_i = _block_diag(b_im[0].transpose(0, 2, 1), True)

    def bbar_fn(br, bi, f):
        r, i = _cmul(f[0], f[1], br, bi)
        return [r, i]

    bbar_r, bbar_i = _single("ssm_bbar", bbar_fn, [braw_r, braw_i, f_rows], [(braw_r.shape, BF16)] * 2)
    ct_r, ct_i = _block_diag(c_re[0], True).astype(BF16), _block_diag(c_im[0], True).astype(BF16)
    ds2 = d_skip
    y_ssm, h_r, h_i = _ssm_fwd(u_src, bbar_r, bbar_i, ct_r.transpose(0, 2, 1), ct_i.transpose(0, 2, 1), fwd_consts, ds2)

    (yb,) = _rowwise("gelu", lambda v: ([_gelu(v)], []), [y_ssm], [], [(sw, BF16)], [])
    w_glu_s, w_out_s = _split_wait("gather_mix_wait", mix_stacks, mix_sems_s, mix_sems_r, yb, gather_plan)
    w_glu_f = w_glu_s.reshape(sw, sw)
    w_out_f = w_out_s.reshape(d, d)
    z = _mm("glu_fwd", yb, w_glu_f, "nn", F32)

    def glu(ys, zz, b):
        return [_gelu(ys) * _sigmoid(zz + b)], []

    (ssm,) = _rowwise("glu", glu, [y_ssm, z], [b_glu], [(sw, F32)], [])

    def cat_norm(a, sm, ga, gs):
        return [jnp.concatenate([a * _rstd(a) * ga, sm * _rstd(sm) * gs], axis=1)], []

    (catb,) = _rowwise("cat_norm", cat_norm, [attn, ssm], [attn_out_g, ssm_out_g], [(d, BF16)], [])
    mixed = _mm("mix_out", catb, w_out_f, "nn", F32)

    def resid_norm(xv, mx, g1v, g, sc, sh):
        x1v = xv + g1v * mx
        return [x1v, x1v * _rstd(x1v) * g * (1.0 + sc) + sh], []

    x1, h2b = _rowwise("resid_norm2", resid_norm, [x2, mixed], [g1, norm2_g, sc2, sh2], [(d, F32), (d, BF16)], [])
    w_ff1_s, w_ff2_s = _split_wait("gather_ff_wait", ff_stacks, ff_sems_s, ff_sems_r, x1, gather_plan)
    w_ff2_f = w_ff2_s.reshape(dff, d)
    a_ff = _mm("ff1", h2b, w_ff1_s, "nn", BF16, b_cs=True)
    (rb,) = _rowwise("sq_relu", lambda v: ([jnp.square(jnp.maximum(v.astype(F32), 0.0))], []), [a_ff], [], [(dff, BF16)], [])
    ff = _mm("ff2", rb, w_ff2_f, "nn", F32)

    def loss_fn(x1v, ffv, tg, g2v):
        diff = x1v + g2v * ffv - tg
        dout = diff * (1.0 / d)
        return [dout, dout * g2v], [_colsum8(0.5 * diff * dout), _colsum8(dout * ffv)]

    dout, dffb, loss_acc, dg2_acc = _rowwise("loss", loss_fn, [x1, ff, tgt], [g2], [(d, F32), (d, BF16)], [(SUBLANES, d)] * 2)

    da_ff = _mm("ff2_bwd", dffb, w_ff2_f, "nt", BF16, epi=lambda acc, av: acc * (2.0 * jnp.maximum(av.astype(F32), 0.0)), extras=[a_ff])
    gw = {}
    gw["w_ff2"] = _mm("ff2_wgrad", rb, dffb, "tn", BF16).reshape(4, dff // 4, d)
    gw["w_ff1"] = _mm("ff1_wgrad", h2b, da_ff, "tn", BF16, o_cs=True)

    def add2(n, g, b):
        _, half, cols = b.shape
        tr = _blk(half, 256)
        nrb = half // tr
        blk = (None, tr, cols)
        return _indexed("grad_add2_" + n, lambda p, q: p.astype(F32) + q.astype(F32), core_idx,
                        [(g, blk, lambda k, r, ix: (k, ix[0] * nrb + r, 0)), (b, blk, lambda k, r, ix: (k, r, 0))],
                        ((b.shape, BF16), blk, lambda k, r, ix: (k, r, 0)), (4, nrb))

    def scatter_plan(refs):
        px0, py0, pc, chips = _place()
        nw = len(refs) // 2
        return [(refs[w].at[2 * px + py], refs[nw + w].at[j], (px, py, pc), refs[nw + w].at[j])
                for w in range(nw) for j, (px, py) in enumerate(chips)]

    early = ["w_ff2", "w_ff1"]
    parts = {n: add2(n, gw[n], b) for n, b in zip(early, _swap_halves([gw[n] for n in early], "swap_halves_ff"))}
    lands = [lax.empty((3,) + parts[n].shape[1:], BF16) for n in early]
    sc_sems_s, sc_sems_r, sc_bufs, sc_token = _split_start("scatter_ff_start", [parts[n] for n in early] + lands, scatter_plan, 6)
    sc2_late = sc2 + sc_token[0:1, 0:1]

    dh2 = _mm("ff1_bwd", da_ff, w_ff1_s, "nt", F32, b_cs=True)

    def norm2_bwd(dh, x1v, do, mx, g, sc, g1v):
        r = _rstd(x1v)
        xh = x1v * r
        dn = dh * (1.0 + sc)
        dx = do + _rms_bwd(dn * g, xh, r)
        return [dx, dx * g1v], [_colsum8(dh * xh * g), _colsum8(dh), _colsum8(dn * xh), _colsum8(dx * mx)]

    dx1, dmixb, dsc2_acc, dsh2_acc, dn2g_acc, dg1_acc = _rowwise(
        "norm2_bwd", norm2_bwd, [dh2, x1, dout, mixed], [norm2_g, sc2_late, g1], [(d, F32), (d, BF16)], [(SUBLANES, d)] * 4)

    dcat = _mm("mix_out_bwd", dmixb, w_out_f, "nt", F32)
    gw["w_out"] = _mm("mix_out_wgrad", catb, dmixb, "tn", BF16).reshape(4, d // 4, d)

    def cat_bwd(da, dsm, a, sm, ga, gs):
        ra, rs = _rstd(a), _rstd(sm)
        ah, sh = a * ra, sm * rs
        return [_rms_bwd(da * ga, ah, ra), _rms_bwd(dsm * gs, sh, rs)], [_colsum8(da * ah), _colsum8(dsm * sh)]

    dattn, dssm, dga_acc, dgs_acc = _rowwise(
        "cat_norm_bwd", cat_bwd, [(dcat, aw, 0), (dcat, sw, 1), attn, ssm], [attn_out_g, ssm_out_g], [(aw, F32), (sw, F32)],
        [(SUBLANES, aw), (SUBLANES, sw)])

    def glu_bwd(dsm, ys, zz, b):
        sg = _sigmoid(zz + b)
        dz = dsm * _gelu(ys) * sg * (1.0 - sg)
        return [dz, dsm * sg], [_colsum8(dz)]

    dzb, dy_direct, dbglu_acc = _rowwise("glu_bwd", glu_bwd, [dssm, y_ssm, z], [b_glu], [(sw, BF16), (sw, F32)], [(SUBLANES, sw)])
    gw["w_glu"] = _mm("glu_wgrad", yb, dzb, "tn", BF16).reshape(4, sw // 4, sw)
    dy_glu = _mm("glu_bwd_mm", dzb, w_glu_f, "nt", F32)
    (dys,) = _rowwise("gelu_bwd", lambda a, b, ys: ([(a + b) * _gelu_grad(ys)], []), [dy_direct, dy_glu, y_ssm], [], [(sw, F32)], [])
    du, db_bar, dc_bd, da_acc, dds_acc = _ssm_bwd(dys, u_src, h_r, h_i, ct_r, ct_i, bbar_r.transpose(0, 2, 1),
                                                  bbar_i.transpose(0, 2, 1), bwd_consts, ds2)

    def delta_fn(da, a):
        dl = jnp.concatenate([jnp.broadcast_to(jnp.sum(p * q, axis=1, keepdims=True), p.shape) for p, q in zip(_heads(da), _heads(a))], axis=1)
        return [da, dl], []

    dob, delta = _rowwise("attn_delta", delta_fn, [dattn, attn], [], [(aw, BF16), (aw, F32)], [])
    dqs, dks, dvs = [], [], []
    for dil, bps, qp, kp, vp, _, _ in pats:
        dop, lsp, dlp = [_to_streams(t, dil) for t in (dob, lse, delta)]
        dq_p, dk_p, dv_p = _attn_bwd(f"attn_bwd_d{dil}", qp, kp, vp, dop, lsp, dlp, bps, dil)
        dqs.append(_from_streams(dq_p, dil))
        dks.append(_from_streams(dk_p, dil))
        dvs.append(_from_streams(dv_p, dil))

    def qk_bwd(dq1, dq2, dq3, dk1, dk2, dk3, dv1, dv2, dv3, duv, q, k, gq, gk):
        def one(dn, raw, g):
            outs, acc = [], 0.0
            for dh, h in zip(_heads(dn), _heads(raw)):
                r = _rstd(h)
                hh = h * r
                outs.append(_rms_bwd(dh * g, hh, r))
                acc = acc + _colsum8(dh * hh)
            return jnp.concatenate(outs, axis=1), acc

        dq, gq_acc = one(dq1 + dq2 + dq3, q, gq)
        dk, gk_acc = one(dk1 + dk2 + dk3, k, gk)
        return [jnp.concatenate([dq, dk, dv1 + dv2 + dv3, duv], axis=1)], [gq_acc, gk_acc]

    dprojb, dgq_acc, dgk_acc = _rowwise("qk_norm_bwd", qk_bwd, dqs + dks + dvs + [du, q_src, k_src], [q_norm_g, k_norm_g],
                                        [(3 * aw + sw, BF16)], [(SUBLANES, HEAD_DIM)] * 2, tm=128)
    gw["w_in"] = _mm("proj_in_wgrad", h1b, dprojb, "tn", BF16, o_cs=True)
    dh1 = _mm("proj_in_bwd", dprojb, w_in_s, "nt", F32, b_cs=True)

    def norm1_bwd(dh, xv, dxr, g, sc):
        r = _rstd(xv)
        xh = xv * r
        dn = dh * (1.0 + sc)
        return [dxr + _rms_bwd(dn * g, xh, r)], [_colsum8(dh * xh * g), _colsum8(dh), _colsum8(dn * xh)]

    grad_x, dsc1_acc, dsh1_acc, dn1g_acc = _rowwise("norm1_bwd", norm1_bwd, [dh1, x2, dx1], [norm1_g, sc1], [(d, F32)], [(SUBLANES, d)] * 3)

    def b_grad_fn(dbb, br, bi, f):
        fr, fi = f[0], f[1]
        dbr, dbi = _cmul_conj(fr, fi, dbb[0], dbb[1])
        dfr = jnp.sum(dbb[0] * br + dbb[1] * bi, axis=1, keepdims=True)
        dfi = jnp.sum(dbb[1] * br - dbb[0] * bi, axis=1, keepdims=True)
        return [dbr, dbi, dfr, dfi]

    dbr_bd, dbi_bd, df_r, df_i = _single("ssm_b_grad", b_grad_fn, [db_bar, braw_r, braw_i, f_rows],
                                         [(braw_r.shape, F32)] * 2 + [((nblk, 1, 512), F32)] * 2)

    def sums_fn(*accs):
        return [jnp.sum(a, axis=-2, keepdims=True) for a in accs]

    acc_list = [dsh1_acc, dsc1_acc, dg1_acc, dsh2_acc, dsc2_acc, dg2_acc, dn1g_acc, dgq_acc, dgk_acc, dds_acc, dbglu_acc, dga_acc,
                dgs_acc, dn2g_acc, loss_acc, da_acc]
    sums = _single("small_sums", sums_fn, acc_list, [(a.shape[:-2] + (1, a.shape[-1]), F32) for a in acc_list])
    (dsh1, dsc1, dg1, dsh2, dsc2, dg2, dn1g, dgq, dgk, dds, dbglu, dga, dgs, dn2g, loss_row, da_sum) = sums
    loss_dev = _single("loss_sum", lambda v: [jnp.sum(v, axis=1, keepdims=True)], [loss_row], [((1, 1), F32)])[0]
    gp = (n_groups, STATE_DIM)
    dlam_re, dlam_im, dlog_step = _single(
        "ssm_param_grad", _ssm_param_grad_fn,
        [lam_re2, lam_im2, log_step2, da_sum[0].reshape(gp), da_sum[1].reshape(gp), df_r.reshape(gp), df_i.reshape(gp)],
        [(gp, F32), (gp, F32), ((n_groups, 1), F32)])
    small_dev = {
        "b_ada": jnp.concatenate([dsh1, dsc1, dg1, dsh2, dsc2, dg2], axis=1),
        "norm1_g": dn1g, "q_norm_g": dgq, "k_norm_g": dgk, "lam_re": dlam_re, "lam_im": dlam_im, "log_step": dlog_step,
        "b_re": _block_diag_take(dbr_bd, SSM_GROUP, STATE_DIM).transpose(0, 2, 1),
        "b_im": _block_diag_take(dbi_bd, SSM_GROUP, STATE_DIM).transpose(0, 2, 1),
        "c_re": _block_diag_take(dc_bd[0], SSM_GROUP, STATE_DIM), "c_im": _block_diag_take(dc_bd[1], SSM_GROUP, STATE_DIM),
        "d_skip": dds, "b_glu": dbglu, "attn_out_g": dga, "ssm_out_g": dgs, "norm2_g": dn2g,
    }

    packed = _pack([small_dev[n] for n in SMALL])
    rows = packed.shape[0]
    gathered = _allgather8("gather_small", packed).reshape(8, rows, PACK_LANES)

    def sum8_fn(g):
        acc = g[0]
        for i in range(1, 8):
            acc = acc + g[i]
        return [acc]

    g_small = _single("sum_small", sum8_fn, [gathered], [((rows, PACK_LANES), F32)])[0]
    w_small, m_small, v_small = [_pack([src[n] for n in SMALL]) for src in (wts, mom, var)]
    upd_small = _adamw_rows("adamw_small", w_small, g_small, m_small, v_small)
    shapes = [wts[n].shape for n in SMALL]
    out = {}
    for kind, arr in zip(("grad", "delta", "new_m", "new_v"), (g_small,) + tuple(upd_small)):
        for n, val in zip(SMALL, _unpack(arr, shapes)):
            out[kind, n] = val

    dmod_all = gathered[:, 0 : -(-6 * d // PACK_LANES), :].reshape(8, -1)[:, : 6 * d]
    dmod_mine = lax.dynamic_slice(dmod_all, (0, chip * n_mod), (8, n_mod))
    g_ada = _mm("ada_wgrad", sil_c, dmod_mine, "tn", F32, tm=512, tn=512)
    upd = _adamw_rows("adamw_w_ada", w_ada[0], g_ada, m_w_ada[0], v_w_ada[0])
    for kind, val in zip(("grad", "delta", "new_m", "new_v"), (g_ada,) + tuple(upd)):
        out[kind, "w_ada"] = val[None]

    late = [n for n in LARGE if n not in early]
    parts.update({n: add2(n, gw[n], b) for n, b in zip(late, _swap_halves([gw[n] for n in late], "swap_halves_rest"))})
    got2 = dict(zip(late, _scatter_chips([parts[n] for n in late])))
    sc_bufs = _split_wait("scatter_ff_wait", sc_bufs, sc_sems_s, sc_sems_r, grad_x, scatter_plan)
    parts.update(zip(early, sc_bufs[: len(early)]))
    got2.update(zip(early, sc_bufs[len(early) :]))

    def add4(n, p, b):
        _, half, cols = p.shape
        tr = _blk(half, 256)
        blk = (None, tr, cols)
        fn = lambda own, q0, q1, q2: ((own.astype(F32) + q0.astype(F32)) + q1.astype(F32)) + q2.astype(F32)
        srcs = [(p, blk, lambda r, ix: (ix[0], r, 0))] + [(b, blk, lambda r, ix, j=j: (j, r, 0)) for j in range(3)]
        return _indexed("grad_add4_" + n, fn, chip_idx, srcs, (((half, cols), F32), (tr, cols), lambda r, ix: (r, 0)), (half // tr,))

    finals = [add4(n, parts[n], got2[n]) for n in LARGE]
    for n, full in zip(LARGE, _join_halves(finals)):
        g = full.reshape(wts[n].shape[1:])
        upd = _adamw_rows("adamw_" + n, wts[n][0], g, mom[n][0], var[n][0])
        for kind, val in zip(("grad", "delta", "new_m", "new_v"), (g,) + tuple(upd)):
            out[kind, n] = val[None]

    loss = lax.psum(loss_dev[0, 0], ("x", "y", "c"))
    return (loss, grad_x[None], *[out[kind, n] for kind in ("grad", "delta", "new_m", "new_v") for n in ORDER])
```

```python
import functools
import math

import jax
import jax.numpy as jnp
import numpy as np
from jax import lax
from jax.experimental import pallas as pl
from jax.experimental.pallas import tpu as pltpu

F32 = jnp.float32
BF16 = jnp.bfloat16
MESH = pl.DeviceIdType.MESH

EPS = 1e-6
HEAD_DIM = 128
SSM_GROUP = 16
STATE_DIM = 64
GROUPS_PER_BLOCK = 8
DILATION_PATTERNS = ((128, 1), (512, 4), (2048, 16))
BAND = 128
NEG = -1e30

ADAM_LR, ADAM_B1, ADAM_B2, ADAM_EPS, ADAM_WD, ADAM_STEP = 0.001, 0.9, 0.999, 1e-08, 0.01, 10

V7X_VMEM_BYTES = 64 * 2**20
VMEM_LIMIT = 48 * 2**20
JOIN_VMEM_LIMIT = 56 * 2**20
SUBLANES = 8
PACK_LANES = 1024


def _cparams(sem=None):
    return pltpu.CompilerParams(dimension_semantics=sem, vmem_limit_bytes=VMEM_LIMIT)


def _blk(n, want):
    b = min(n, want)
    while n % b:
        b //= 2
    return b


def _colsum8(v):
    tm, w = v.shape
    return v.reshape(tm // SUBLANES, SUBLANES, w).sum(axis=0)


def _rowwise(name, fn, rows, vecs, out_rows, out_accs, tm=256, n_rows=None):
    rows = [r if isinstance(r, tuple) else (r, r.shape[1], 0) for r in rows]
    rows = [r if len(r) == 4 else r + (0,) for r in rows]
    s = rows[0][0].shape[0] if n_rows is None else n_rows
    tm = _blk(s, tm)
    assert all(r[3] % tm == 0 for r in rows)
    n_in, n_or = len(rows) + len(vecs), len(out_rows)

    def body(*refs):
        outs, accs = fn(*[r[...] for r in refs[:n_in]])
        for r, v in zip(refs[n_in : n_in + n_or], outs):
            r[...] = v.astype(r.dtype)
        if out_accs:
            acc_refs = refs[n_in + n_or :]

            @pl.when(pl.program_id(0) == 0)
            def _():
                for r in acc_refs:
                    r[...] = jnp.zeros(r.shape, r.dtype)

            for r, v in zip(acc_refs, accs):
                r[...] += v

    in_specs = [pl.BlockSpec((tm, w), lambda i, cb=cb, rb=r0 // tm: (i + rb, cb)) for (_, w, cb, r0) in rows]
    in_specs += [pl.BlockSpec(v.shape, lambda i, nd=v.ndim: (0,) * nd) for v in vecs]
    out_specs = [pl.BlockSpec((tm, w), lambda i: (i, 0)) for (w, _) in out_rows]
    out_specs += [pl.BlockSpec(sh, lambda i, nd=len(sh): (0,) * nd) for sh in out_accs]
    out_shape = [jax.ShapeDtypeStruct((s, w), dt) for (w, dt) in out_rows]
    out_shape += [jax.ShapeDtypeStruct(sh, F32) for sh in out_accs]
    res = pl.pallas_call(
        body, name=name, grid=(s // tm,), in_specs=in_specs, out_specs=out_specs, out_shape=out_shape,
        compiler_params=_cparams(("arbitrary",)),
    )(*[r[0] for r in rows], *vecs)
    return res


def _single(name, fn, ins, out_shapes):
    n_in = len(ins)

    def body(*refs):
        outs = fn(*[r[...] for r in refs[:n_in]])
        for r, v in zip(refs[n_in:], outs):
            r[...] = v.astype(r.dtype)

    vm = pl.BlockSpec(memory_space=pltpu.VMEM)
    return pl.pallas_call(
        body, name=name, in_specs=[vm] * n_in, out_specs=[vm] * len(out_shapes),
        out_shape=[jax.ShapeDtypeStruct(sh, dt) for sh, dt in out_shapes], compiler_params=_cparams(),
    )(*ins)


_NN = (((1,), (0,)), ((), ()))
_NT = (((1,), (1,)), ((), ()))
_TN = (((0,), (0,)), ((), ()))


def _mm(name, a, b, mode, out_dtype, *, tm=512, tn=1024, tk=2048, b_cs=False, o_cs=False, epi=None, extras=(), a_pro=None):
    if mode == "tn":
        k, m = a.shape
        n = b.shape[1]
    else:
        m, k = a.shape
        if mode == "nn":
            n = b.shape[2] * 4 if b_cs else b.shape[1]
        else:
            n = b.shape[1] if b_cs else b.shape[0]
    tm = _blk(m, tm)
    tn = _blk(n // 4 if (b_cs and mode == "nn") or o_cs else n, tn)
    tk = _blk(k // 4 if (b_cs and mode == "nt") else k, tk)
    nk = k // tk
    dims = {"nn": _NN, "nt": _NT, "tn": _TN}[mode]
    n_ex = len(extras)

    def body(*refs):
        a_ref, b_ref = refs[0], refs[1]
        ex_refs, o_ref = refs[2 : 2 + n_ex], refs[2 + n_ex]

        def finish(acc):
            if epi is not None:
                acc = epi(acc, *[r[...] for r in ex_refs])
            o_ref[...] = acc.astype(o_ref.dtype)

        a_blk = a_ref[...] if a_pro is None else a_pro(a_ref[...])
        part = lax.dot_general(a_blk, b_ref[...], dims, preferred_element_type=F32)
        if nk == 1:
            finish(part)
        else:
            acc_ref = refs[3 + n_ex]
            kk = pl.program_id(2)

            @pl.when(kk == 0)
            def _():
                acc_ref[...] = part

            @pl.when(kk > 0)
            def _():
                acc_ref[...] += part

            @pl.when(kk == nk - 1)
            def _():
                finish(acc_ref[...])

    a_spec = pl.BlockSpec((tk, tm), lambda i, j, kk: (kk, i)) if mode == "tn" else pl.BlockSpec((tm, tk), lambda i, j, kk: (i, kk))
    if mode == "nn":
        if b_cs:
            per = (n // 4) // tn
            b_spec = pl.BlockSpec((None, tk, tn), lambda i, j, kk: (j // per, kk, j % per))
        else:
            b_spec = pl.BlockSpec((tk, tn), lambda i, j, kk: (kk, j))
    elif mode == "nt":
        if b_cs:
            per = (k // 4) // tk
            b_spec = pl.BlockSpec((None, tn, tk), lambda i, j, kk: (kk // per, j, kk % per))
        else:
            b_spec = pl.BlockSpec((tn, tk), lambda i, j, kk: (j, kk))
    else:
        b_spec = pl.BlockSpec((tk, tn), lambda i, j, kk: (kk, j))
    if o_cs:
        per = (n // 4) // tn
        o_spec = pl.BlockSpec((None, tm, tn), lambda i, j, kk: (j // per, i, j % per))
        o_shape = jax.ShapeDtypeStruct((4, m, n // 4), out_dtype)
    else:
        o_spec = pl.BlockSpec((tm, tn), lambda i, j, kk: (i, j))
        o_shape = jax.ShapeDtypeStruct((m, n), out_dtype)
    ex_specs = [pl.BlockSpec((tm, tn), lambda i, j, kk: (i, j)) for _ in extras]
    return pl.pallas_call(
        body, name=name, grid=(m // tm, n // tn, nk), in_specs=[a_spec, b_spec] + ex_specs, out_specs=o_spec, out_shape=o_shape,
        scratch_shapes=[pltpu.VMEM((tm, tn), F32)] if nk > 1 else [],
        compiler_params=_cparams(("parallel", "parallel", "arbitrary")),
    )(a, b, *extras)


def _place():
    x, y, c = lax.axis_index("x"), lax.axis_index("y"), lax.axis_index("c")
    return x, y, c, [(1 - x, y), (x, 1 - y), (1 - x, 1 - y)]


def _allgather8(name, v):
    m_per, n = v.shape

    def body(x_ref, out_ref, send_sems, recv_sems, local_sem):
        x, y, c, chips = _place()
        me, sibling = (x, y, c), (x, y, 1 - c)

        def rows(px, py, pc):
            return out_ref.at[pl.ds((4 * px + 2 * py + pc) * m_per, m_per), :]

        def copy(k, block, to, src=None):
            return pltpu.make_async_remote_copy(
                src_ref=rows(*block) if src is None else src, dst_ref=rows(*block), send_sem=send_sems.at[k],
                recv_sem=recv_sems.at[k], device_id=to, device_id_type=MESH)

        mine = pltpu.make_async_copy(x_ref, rows(*me), local_sem)
        mine.start()
        first = [copy(0, me, sibling, src=x_ref)]
        first += [copy(1 + j, me, (*chip, c), src=x_ref) for j, chip in enumerate(chips)]
        for cp in first:
            cp.start()
        passed = [copy(4 + j, (*chip, c), sibling) for j, chip in enumerate(chips)]
        for j, chip in enumerate(chips):
            copy(1 + j, (*chip, c), me).wait_recv()
            passed[j].start()
        copy(0, sibling, me).wait_recv()
        for j, chip in enumerate(chips):
            copy(4 + j, (*chip, 1 - c), me).wait_recv()
        for cp in first + passed:
            cp.wait_send()
        mine.wait()

    return pl.pallas_call(
        body, name=name, out_shape=jax.ShapeDtypeStruct((8 * m_per, n), v.dtype),
        in_specs=[pl.BlockSpec(memory_space=pltpu.VMEM)], out_specs=pl.BlockSpec(memory_space=pltpu.VMEM),
        scratch_shapes=[pltpu.SemaphoreType.DMA((7,)), pltpu.SemaphoreType.DMA((7,)), pltpu.SemaphoreType.DMA],
        compiler_params=_cparams(),
    )(v)


def _gather_weights(stacks, after):
    nw = len(stacks)

    def body(*refs):
        outs = refs[nw + 1 : 2 * nw + 1]
        send_sems, recv_sems = refs[2 * nw + 1 :]
        x, y, c, chips = _place()
        me, sibling = (x, y, c), (x, y, 1 - c)

        def copy(w, k, block, to):
            px, py, pc = block
            half = outs[w].shape[1] // 2
            rows = outs[w].at[2 * px + py, pl.ds(pc * half, half), :]
            return pltpu.make_async_remote_copy(src_ref=rows, dst_ref=rows, send_sem=send_sems.at[6 * w + k],
                                                recv_sem=recv_sems.at[6 * w + k], device_id=to, device_id_type=MESH)

        sent = [copy(w, j, me, (*chip, c)) for w in range(nw) for j, chip in enumerate(chips)]
        for cp in sent:
            cp.start()
        for w in range(nw):
            for j, chip in enumerate(chips):
                copy(w, j, (*chip, c), me).wait_recv()
                passed = copy(w, 3 + j, (*chip, c), sibling)
                passed.start()
                sent.append(passed)
        for w in range(nw):
            for j, chip in enumerate(chips):
                copy(w, 3 + j, (*chip, 1 - c), me).wait_recv()
        for cp in sent:
            cp.wait_send()

    hbm = pl.BlockSpec(memory_space=pl.ANY)
    return pl.pallas_call(
        body, name="gather_weights", out_shape=[jax.ShapeDtypeStruct(w.shape, w.dtype) for w in stacks],
        in_specs=[hbm] * (nw + 1), out_specs=[hbm] * nw, input_output_aliases={i: i for i in range(nw)},
        scratch_shapes=[pltpu.SemaphoreType.DMA((6 * nw,)), pltpu.SemaphoreType.DMA((6 * nw,))],
        compiler_params=_cparams(),
    )(*stacks, after)


def _swap_halves(gs, name):
    nw = len(gs)

    def body(*refs):
        ins, gots = refs[:nw], refs[nw : 2 * nw]
        send_sems, recv_sems = refs[2 * nw :]
        x, y, c, _ = _place()
        cps = []
        for w in range(nw):
            half = ins[w].shape[1] // 2
            cps.append(pltpu.make_async_remote_copy(
                src_ref=ins[w].at[:, pl.ds((1 - c) * half, half), :], dst_ref=gots[w], send_sem=send_sems.at[w],
                recv_sem=recv_sems.at[w], device_id=(x, y, 1 - c), device_id_type=MESH))
            cps[-1].start()
        for cp in cps:
            cp.wait()

    hbm = pl.BlockSpec(memory_space=pl.ANY)
    return pl.pallas_call(
        body, name=name, out_shape=[jax.ShapeDtypeStruct((4, g.shape[1] // 2, g.shape[2]), g.dtype) for g in gs],
        in_specs=[hbm] * nw, out_specs=[hbm] * nw,
        scratch_shapes=[pltpu.SemaphoreType.DMA((nw,)), pltpu.SemaphoreType.DMA((nw,))], compiler_params=_cparams(),
    )(*gs)


def _scatter_chips(ps):
    nw = len(ps)

    def body(*refs):
        ins, gots = refs[:nw], refs[nw : 2 * nw]
        send_sems, recv_sems = refs[2 * nw :]
        x, y, c, chips = _place()
        cps = []
        for w in range(nw):
            for j, (px, py) in enumerate(chips):
                cps.append(pltpu.make_async_remote_copy(
                    src_ref=ins[w].at[2 * px + py], dst_ref=gots[w].at[j], send_sem=send_sems.at[3 * w + j],
                    recv_sem=recv_sems.at[3 * w + j], device_id=(px, py, c), device_id_type=MESH))
                cps[-1].start()
        for cp in cps:
            cp.wait()

    hbm = pl.BlockSpec(memory_space=pl.ANY)
    return pl.pallas_call(
        body, name="scatter_chips", out_shape=[jax.ShapeDtypeStruct((3,) + p.shape[1:], p.dtype) for p in ps],
        in_specs=[hbm] * nw, out_specs=[hbm] * nw,
        scratch_shapes=[pltpu.SemaphoreType.DMA((3 * nw,)), pltpu.SemaphoreType.DMA((3 * nw,))], compiler_params=_cparams(),
    )(*ps)


def _split_start(name, bufs, plan, n, extra=()):
    n_plan = len(bufs)
    bufs = list(bufs) + list(extra)
    nb = len(bufs)

    def body(*refs):
        send_sems, recv_sems, token = refs[nb], refs[nb + 1], refs[-1]
        for k, (src, dst, to, _) in enumerate(plan(refs[:n_plan])):
            pltpu.make_async_remote_copy(src_ref=src, dst_ref=dst, send_sem=send_sems.at[k], recv_sem=recv_sems.at[k],
                                         device_id=to, device_id_type=MESH).start()
        token[...] = jnp.zeros(token.shape, token.dtype)

    hbm, sem = pl.BlockSpec(memory_space=pltpu.HBM), pl.BlockSpec(memory_space=pltpu.SEMAPHORE)
    res = pl.pallas_call(
        body, name=name,
        out_shape=(pltpu.SemaphoreType.DMA((n,)), pltpu.SemaphoreType.DMA((n,)), *[pltpu.HBM(b.shape, b.dtype) for b in bufs],
                   jax.ShapeDtypeStruct((SUBLANES, 128), F32)),
        in_specs=(hbm,) * nb, out_specs=(sem, sem) + (hbm,) * nb + (pl.BlockSpec(memory_space=pltpu.VMEM),),
        input_output_aliases={i: 2 + i for i in range(nb)},
        compiler_params=pltpu.CompilerParams(has_side_effects=pltpu.SideEffectType.DATAFLOW_SIDE_EFFECTING),
    )(*[pltpu.with_memory_space_constraint(b, pltpu.HBM) for b in bufs])
    return res[0], res[1], list(res[2 : 2 + nb]), res[-1]


def _split_wait(name, bufs, send_sems, recv_sems, after, plan):
    nb = len(bufs)

    def body(*refs):
        send, recv = refs[nb], refs[nb + 1]
        for k, (src, _, to, land) in enumerate(plan(refs[:nb])):
            cp = pltpu.make_async_remote_copy(src_ref=src, dst_ref=land, send_sem=send.at[k], recv_sem=recv.at[k],
                                              device_id=to, device_id_type=MESH)
            cp.wait_send()
            cp.wait_recv()

    hbm, sem = pl.BlockSpec(memory_space=pltpu.HBM), pl.BlockSpec(memory_space=pltpu.SEMAPHORE)
    return pl.pallas_call(
        body, name=name, out_shape=tuple(pltpu.HBM(b.shape, b.dtype) for b in bufs),
        in_specs=(hbm,) * nb + (sem, sem, pl.BlockSpec(memory_space=pl.ANY)), out_specs=(hbm,) * nb,
        input_output_aliases={i: i for i in range(nb)},
        compiler_params=pltpu.CompilerParams(has_side_effects=pltpu.SideEffectType.DATAFLOW_SIDE_EFFECTING),
    )(*bufs, send_sems, recv_sems, after)


def _indexed(name, fn, idx, ins, out, grid):
    def body(idx_ref, *refs):
        refs[-1][...] = fn(*[r[...] for r in refs[:-1]]).astype(refs[-1].dtype)

    (o_shape, o_dtype), o_block, o_map = out
    return pl.pallas_call(
        body, name=name, out_shape=jax.ShapeDtypeStruct(o_shape, o_dtype),
        grid_spec=pltpu.PrefetchScalarGridSpec(
            num_scalar_prefetch=1, grid=grid, in_specs=[pl.BlockSpec(blk, mp) for (_, blk, mp) in ins],
            out_specs=pl.BlockSpec(o_block, o_map)),
        compiler_params=_cparams(("arbitrary",) * len(grid)),
    )(idx, *[a for (a, _, _) in ins])


def _join_halves(fs):
    nw = len(fs)

    def body(*refs):
        ins, outs, lands = refs[:nw], refs[nw : 2 * nw], refs[2 * nw : 3 * nw]
        send_sems, recv_sems, local_sems = refs[3 * nw :]
        x, y, c, _ = _place()
        sibling = (x, y, 1 - c)
        cps = [pltpu.make_async_remote_copy(src_ref=ins[w], dst_ref=lands[w], send_sem=send_sems.at[w], recv_sem=recv_sems.at[w],
                                            device_id=sibling, device_id_type=MESH) for w in range(nw)]
        outs_own = [pltpu.make_async_copy(ins[w], outs[w].at[c], local_sems.at[w]) for w in range(nw)]
        outs_got = [pltpu.make_async_copy(lands[w], outs[w].at[1 - c], local_sems.at[nw + w]) for w in range(nw)]
        for cp in cps + outs_own:
            cp.start()
        for w in range(nw):
            cps[w].wait_recv()
            outs_got[w].start()
        for w in range(nw):
            cps[w].wait_send()
            outs_own[w].wait()
            outs_got[w].wait()

    hbm, vm = pl.BlockSpec(memory_space=pl.ANY), pl.BlockSpec(memory_space=pltpu.VMEM)
    return pl.pallas_call(
        body, name="join_halves", out_shape=[jax.ShapeDtypeStruct((2,) + f.shape, f.dtype) for f in fs],
        in_specs=[vm] * nw, out_specs=[hbm] * nw,
        scratch_shapes=[pltpu.VMEM(f.shape, f.dtype) for f in fs]
        + [pltpu.SemaphoreType.DMA((nw,)), pltpu.SemaphoreType.DMA((nw,)), pltpu.SemaphoreType.DMA((2 * nw,))],
        compiler_params=pltpu.CompilerParams(vmem_limit_bytes=JOIN_VMEM_LIMIT),
    )(*fs)


def _rstd(v):
    return lax.rsqrt(jnp.mean(v * v, axis=-1, keepdims=True) + EPS)


def _rms_bwd(dy, xh, r):
    return r * (dy - xh * jnp.mean(dy * xh, axis=-1, keepdims=True))


def _heads(v):
    return [v[:, h * HEAD_DIM : (h + 1) * HEAD_DIM] for h in range(v.shape[1] // HEAD_DIM)]


def _gelu(v):
    k = math.sqrt(2.0 / math.pi)
    return 0.5 * v * (1.0 + jnp.tanh(k * (v + 0.044715 * v * v * v)))


def _gelu_grad(v):
    k = math.sqrt(2.0 / math.pi)
    t = jnp.tanh(k * (v + 0.044715 * v * v * v))
    return 0.5 * (1.0 + t) + 0.5 * v * (1.0 - t * t) * k * (1.0 + 3 * 0.044715 * v * v)


def _sigmoid(v):
    return 1.0 / (1.0 + jnp.exp(-v))


def _slopes(n_heads):
    return [2.0 ** (-8.0 * (h + 1.0) / n_heads) for h in range(n_heads)]


def _band_iotas():
    ri = lax.broadcasted_iota(jnp.int32, (BAND, BAND), 0)
    ci = lax.broadcasted_iota(jnp.int32, (BAND, BAND), 1)
    return ri, ci


def _attn_fwd(name, q, k, v, bps, dil):
    s, aw = q.shape
    n_heads, nb = aw // HEAD_DIM, s // BAND
    scale = HEAD_DIM**-0.5
    slopes = _slopes(n_heads)

    def body(q_ref, kp_ref, kc_ref, vp_ref, vc_ref, o_ref, l_ref):
        b = pl.program_id(0)
        first = (b % bps) == 0
        ri, ci = _band_iotas()
        dist_c = (ri - ci).astype(F32)
        dist_p = (ri - ci + BAND).astype(F32)
        valid_c = ci <= ri
        valid_p = ci >= ri + jnp.where(first, BAND, 0)
        hs = range(n_heads)
        sls = [slice(h * HEAD_DIM, (h + 1) * HEAD_DIM) for h in hs]
        qs = [q_ref[:, sl] for sl in sls]
        sc = [lax.dot_general(qs[h], kc_ref[:, sls[h]], _NT, preferred_element_type=F32) for h in hs]
        sp = [lax.dot_general(qs[h], kp_ref[:, sls[h]], _NT, preferred_element_type=F32) for h in hs]
        sc = [jnp.where(valid_c, sc[h] * scale - (slopes[h] * dil) * dist_c, NEG) for h in hs]
        sp = [jnp.where(valid_p, sp[h] * scale - (slopes[h] * dil) * dist_p, NEG) for h in hs]
        m = [jnp.maximum(jnp.max(sc[h], axis=1, keepdims=True), jnp.max(sp[h], axis=1, keepdims=True)) for h in hs]
        pc = [jnp.exp(sc[h] - m[h]) for h in hs]
        pp = [jnp.exp(sp[h] - m[h]) for h in hs]
        den = [jnp.sum(pc[h], axis=1, keepdims=True) + jnp.sum(pp[h], axis=1, keepdims=True) for h in hs]
        acc = [jnp.dot(pc[h].astype(BF16), vc_ref[:, sls[h]], preferred_element_type=F32) for h in hs]
        acc = [acc[h] + jnp.dot(pp[h].astype(BF16), vp_ref[:, sls[h]], preferred_element_type=F32) for h in hs]
        o_ref[...] = jnp.concatenate([acc[h] / den[h] for h in hs], axis=1)
        l_ref[...] = jnp.concatenate([jnp.broadcast_to(m[h] + jnp.log(den[h]), (BAND, HEAD_DIM)) for h in hs], axis=1)

    cur = pl.BlockSpec((BAND, aw), lambda b: (b, 0))
    prev = pl.BlockSpec((BAND, aw), lambda b: (jnp.maximum(b - 1, 0), 0))
    return pl.pallas_call(
        body, name=name, grid=(nb,), in_specs=[cur, prev, cur, prev, cur], out_specs=[cur, cur],
        out_shape=[jax.ShapeDtypeStruct((s, aw), F32)] * 2, compiler_params=_cparams(("arbitrary",)),
    )(q, k, k, v, v)


def _attn_bwd(name, q, k, v, do, lse, delta, bps, dil):
    s, aw = q.shape
    n_heads, nb = aw // HEAD_DIM, s // BAND
    scale = HEAD_DIM**-0.5
    slopes = _slopes(n_heads)

    def body(qc_ref, qn_ref, doc_ref, don_ref, lc_ref, ln_ref, dc_ref, dn_ref, k_ref, v_ref, dq_ref, dk_ref, dv_ref, carry):
        j = pl.program_id(0)

        @pl.when(j == 0)
        def _():
            carry[...] = jnp.zeros(carry.shape, F32)

        has_next = jnp.logical_and((j + 1) % bps != 0, j + 1 < nb)
        ri, ci = _band_iotas()
        dist_c = (ri - ci).astype(F32)
        dist_p = (ri - ci + BAND).astype(F32)
        valid_c = ci <= ri
        valid_n = ci >= ri + jnp.where(has_next, 0, BAND)
        carried = carry[...]
        hs = range(n_heads)
        sls = [slice(h * HEAD_DIM, (h + 1) * HEAD_DIM) for h in hs]
        ks, vs = [k_ref[:, sl] for sl in sls], [v_ref[:, sl] for sl in sls]
        sides = [(qc_ref, doc_ref, lc_ref, dc_ref, dist_c, valid_c), (qn_ref, don_ref, ln_ref, dn_ref, dist_p, valid_n)]
        res = []
        for q_ref, do_ref, l_ref, d_ref, dist, valid in sides:
            qs, dos = [q_ref[:, sl] for sl in sls], [do_ref[:, sl] for sl in sls]
            sc = [lax.dot_general(qs[h], ks[h], _NT, preferred_element_type=F32) for h in hs]
            dp = [lax.dot_general(dos[h], vs[h], _NT, preferred_element_type=F32) for h in hs]
            p = [jnp.where(valid, jnp.exp(sc[h] * scale - (slopes[h] * dil) * dist - l_ref[:, sls[h]]), 0.0) for h in hs]
            ds = [(p[h] * (dp[h] - d_ref[:, sls[h]])).astype(BF16) for h in hs]
            pb = [p[h].astype(BF16) for h in hs]
            dv = [lax.dot_general(pb[h], dos[h], _TN, preferred_element_type=F32) for h in hs]
            dk = [lax.dot_general(ds[h], qs[h], _TN, preferred_element_type=F32) for h in hs]
            dq = [jnp.dot(ds[h], ks[h], preferred_element_type=F32) for h in hs]
            res.append((dq, dk, dv))
        (dq_c, dk_c, dv_c), (dq_n, dk_n, dv_n) = res
        dq_ref[...] = carried + jnp.concatenate(dq_c, axis=1) * scale
        carry[...] = jnp.concatenate(dq_n, axis=1) * scale
        dk_ref[...] = jnp.concatenate([dk_c[h] + dk_n[h] for h in hs], axis=1) * scale
        dv_ref[...] = jnp.concatenate([dv_c[h] + dv_n[h] for h in hs], axis=1)

    cur = pl.BlockSpec((BAND, aw), lambda b: (b, 0))
    nxt = pl.BlockSpec((BAND, aw), lambda b: (jnp.minimum(b + 1, nb - 1), 0))
    return pl.pallas_call(
        body, name=name, grid=(nb,), in_specs=[cur, nxt, cur, nxt, cur, nxt, cur, nxt, cur, cur], out_specs=[cur] * 3,
        out_shape=[jax.ShapeDtypeStruct((s, aw), F32)] * 3, scratch_shapes=[pltpu.VMEM((BAND, aw), F32)],
        compiler_params=_cparams(("arbitrary",)),
    )(q, q, do, do, lse, lse, delta, delta, k, v)


def _to_streams(v, dil):
    if dil == 1:
        return v
    s, w = v.shape
    return v.reshape(s // dil, dil, w).transpose(1, 0, 2).reshape(s, w)


def _from_streams(v, dil):
    if dil == 1:
        return v
    s, w = v.shape
    return v.reshape(dil, s // dil, w).transpose(1, 0, 2).reshape(s, w)


SCAN_LANES = 1024
SSM_ROWS = 128


def _cmul(ar, ai, br, bi):
    return ar * br - ai * bi, ar * bi + ai * br


def _cmul_conj(ar, ai, br, bi):
    return ar * br + ai * bi, ar * bi - ai * br


def _ssm_fwd(u_src, bre, bim, cre, cim, consts, d_skip):
    u_arr, sw, ucb = u_src
    s = u_arr.shape[0]
    nblk = bre.shape[0]
    nst = nblk * 512
    t = _blk(s, SSM_ROWS)
    w = _blk(nst, SCAN_LANES)

    def body(u_ref, bre_ref, bim_ref, cre_ref, cim_ref, k_ref, ds_ref, y_ref, hr_ref, hi_ref, xr, xi, car_r, car_i):
        @pl.when(pl.program_id(0) == 0)
        def _():
            car_r[...] = jnp.zeros(car_r.shape, F32)
            car_i[...] = jnp.zeros(car_i.shape, F32)

        u = u_ref[...]
        ub = u.astype(BF16)
        for b in range(nblk):
            ch, st = slice(b * 128, (b + 1) * 128), slice(b * 512, (b + 1) * 512)
            xr[:, st] = jnp.dot(ub[:, ch], bre_ref[b], preferred_element_type=F32)
            xi[:, st] = jnp.dot(ub[:, ch], bim_ref[b], preferred_element_type=F32)
        for c in range(nst // w):
            ls = slice(c * w, (c + 1) * w)
            mult = [k_ref[i, :, ls] for i in range(8)]

            def step(i, carry, ls=ls, mult=mult):
                cr, ci = carry
                r0 = pl.multiple_of(i * SUBLANES, SUBLANES)
                hr, hi = xr[pl.ds(r0, SUBLANES), ls], xi[pl.ds(r0, SUBLANES), ls]
                for n, sh in enumerate((1, 2, 4)):
                    pr, pi = _cmul(mult[2 * n], mult[2 * n + 1], pltpu.roll(hr, sh, 0), pltpu.roll(hi, sh, 0))
                    hr, hi = hr + pr, hi + pi
                pr, pi = _cmul(mult[6], mult[7], cr, ci)
                hr, hi = hr + pr, hi + pi
                xr[pl.ds(r0, SUBLANES), ls] = hr
                xi[pl.ds(r0, SUBLANES), ls] = hi
                return jnp.broadcast_to(hr[7:8, :], hr.shape), jnp.broadcast_to(hi[7:8, :], hi.shape)

            cr, ci = lax.fori_loop(0, t // SUBLANES, step, (car_r[:, ls], car_i[:, ls]))
            car_r[:, ls] = cr
            car_i[:, ls] = ci
        hrb, hib = xr[...].astype(BF16), xi[...].astype(BF16)
        hr_ref[...] = hrb
        hi_ref[...] = hib
        for b in range(nblk):
            ch, st = slice(b * 128, (b + 1) * 128), slice(b * 512, (b + 1) * 512)
            yb = jnp.dot(hrb[:, st], cre_ref[b], preferred_element_type=F32)
            yb -= jnp.dot(hib[:, st], cim_ref[b], preferred_element_type=F32)
            y_ref[:, ch] = yb + ds_ref[:, ch] * u[:, ch]

    whole = lambda a: pl.BlockSpec(a.shape, lambda i, nd=a.ndim: (0,) * nd)
    return pl.pallas_call(
        body, name="ssm_fwd", grid=(s // t,),
        in_specs=[pl.BlockSpec((t, sw), lambda i: (i, ucb))] + [whole(a) for a in (bre, bim, cre, cim, consts, d_skip)],
        out_specs=[pl.BlockSpec((t, sw), lambda i: (i, 0)), pl.BlockSpec((t, nst), lambda i: (i, 0)), pl.BlockSpec((t, nst), lambda i: (i, 0))],
        out_shape=[jax.ShapeDtypeStruct((s, sw), F32), jax.ShapeDtypeStruct((s, nst), BF16), jax.ShapeDtypeStruct((s, nst), BF16)],
        scratch_shapes=[pltpu.VMEM((t, nst), F32), pltpu.VMEM((t, nst), F32), pltpu.VMEM((SUBLANES, nst), F32), pltpu.VMEM((SUBLANES, nst), F32)],
        compiler_params=_cparams(("arbitrary",)),
    )(u_arr, bre, bim, cre, cim, consts, d_skip)


def _ssm_bwd(dy, u_src, hr, hi, ctre, ctim, btre, btim, consts, d_skip):
    u_arr, sw, ucb = u_src
    s = dy.shape[0]
    nblk = ctre.shape[0]
    nst = nblk * 512
    t = _blk(s, SSM_ROWS)
    w = _blk(nst, SCAN_LANES)
    nt = s // t

    def body(dy_ref, u_ref, hr_ref, hi_ref, ctre_ref, ctim_ref, btre_ref, btim_ref, k_ref, ds_ref,
             du_ref, db_ref, dc_ref, da_ref, dd_ref, lr, li, hfr, hfi, car_r, car_i):
        @pl.when(pl.program_id(0) == 0)
        def _():
            car_r[...] = jnp.zeros(car_r.shape, F32)
            car_i[...] = jnp.zeros(car_i.shape, F32)
            db_ref[...] = jnp.zeros(db_ref.shape, F32)
            dc_ref[...] = jnp.zeros(dc_ref.shape, F32)
            da_ref[...] = jnp.zeros(da_ref.shape, F32)
            dd_ref[...] = jnp.zeros(dd_ref.shape, F32)

        dyv, u = dy_ref[...], u_ref[...]
        dyb, ub = dyv.astype(BF16), u.astype(BF16)
        hrb, hib = hr_ref[...], hi_ref[...]
        hfr[...] = hrb.astype(F32)
        hfi[...] = hib.astype(F32)
        for b in range(nblk):
            ch, st = slice(b * 128, (b + 1) * 128), slice(b * 512, (b + 1) * 512)
            lr[:, st] = jnp.dot(dyb[:, ch], ctre_ref[b], preferred_element_type=F32)
            li[:, st] = -jnp.dot(dyb[:, ch], ctim_ref[b], preferred_element_type=F32)
        last_row = lax.broadcasted_iota(jnp.int32, (SUBLANES, w), 0) == SUBLANES - 1
        for c in range(nst // w):
            ls = slice(c * w, (c + 1) * w)
            mult = [k_ref[i, :, ls] for i in range(8)]

            def step(i, carry, ls=ls, mult=mult):
                cr, ci, ar, ai = carry
                r0 = pl.multiple_of((t // SUBLANES - 1 - i) * SUBLANES, SUBLANES)
                gr, gi = lr[pl.ds(r0, SUBLANES), ls], li[pl.ds(r0, SUBLANES), ls]
                for n, sh in enumerate((1, 2, 4)):
                    pr, pi = _cmul_conj(mult[2 * n], mult[2 * n + 1], pltpu.roll(gr, SUBLANES - sh, 0), pltpu.roll(gi, SUBLANES - sh, 0))
                    gr, gi = gr + pr, gi + pi
                pr, pi = _cmul_conj(mult[6], mult[7], cr, ci)
                gr, gi = gr + pr, gi + pi
                lr[pl.ds(r0, SUBLANES), ls] = gr
                li[pl.ds(r0, SUBLANES), ls] = gi
                nr = jnp.where(last_row, cr, pltpu.roll(gr, SUBLANES - 1, 0))
                ni = jnp.where(last_row, ci, pltpu.roll(gi, SUBLANES - 1, 0))
                fr, fi = hfr[pl.ds(r0, SUBLANES), ls], hfi[pl.ds(r0, SUBLANES), ls]
                pr, pi = _cmul_conj(fr, fi, nr, ni)
                return (jnp.broadcast_to(gr[0:1, :], gr.shape), jnp.broadcast_to(gi[0:1, :], gi.shape), ar + pr, ai + pi)

            zero = jnp.zeros((SUBLANES, w), F32)
            cr, ci, ar, ai = lax.fori_loop(0, t // SUBLANES, step, (car_r[:, ls], car_i[:, ls], zero, zero))
            car_r[:, ls] = cr
            car_i[:, ls] = ci
            da_ref[0, :, ls] += ar
            da_ref[1, :, ls] += ai
        lrb, lib = lr[...].astype(BF16), li[...].astype(BF16)
        for b in range(nblk):
            ch, st = slice(b * 128, (b + 1) * 128), slice(b * 512, (b + 1) * 512)
            dub = jnp.dot(lrb[:, st], btre_ref[b], preferred_element_type=F32)
            dub += jnp.dot(lib[:, st], btim_ref[b], preferred_element_type=F32)
            du_ref[:, ch] = dub + ds_ref[:, ch] * dyv[:, ch]
            db_ref[0, b] += lax.dot_general(ub[:, ch], lrb[:, st], _TN, preferred_element_type=F32)
            db_ref[1, b] += lax.dot_general(ub[:, ch], lib[:, st], _TN, preferred_element_type=F32)
            dc_ref[0, b] += lax.dot_general(dyb[:, ch], hrb[:, st], _TN, preferred_element_type=F32)
            dc_ref[1, b] -= lax.dot_general(dyb[:, ch], hib[:, st], _TN, preferred_element_type=F32)
        dd_ref[...] += _colsum8(dyv * u)

    whole = lambda a: pl.BlockSpec(a.shape, lambda i, nd=a.ndim: (0,) * nd)
    rev = lambda wd, cb=0: pl.BlockSpec((t, wd), lambda i, cb=cb: (nt - 1 - i, cb))
    acc = lambda sh: pl.BlockSpec(sh, lambda i, nd=len(sh): (0,) * nd)
    db_shape, da_shape, dd_shape = (2, nblk, 128, 512), (2, SUBLANES, nst), (SUBLANES, sw)
    return pl.pallas_call(
        body, name="ssm_bwd", grid=(nt,),
        in_specs=[rev(sw), rev(sw, ucb), rev(nst), rev(nst)] + [whole(a) for a in (ctre, ctim, btre, btim, consts, d_skip)],
        out_specs=[rev(sw), acc(db_shape), acc(db_shape), acc(da_shape), acc(dd_shape)],
        out_shape=[jax.ShapeDtypeStruct((s, sw), F32), jax.ShapeDtypeStruct(db_shape, F32), jax.ShapeDtypeStruct(db_shape, F32),
                   jax.ShapeDtypeStruct(da_shape, F32), jax.ShapeDtypeStruct(dd_shape, F32)],
        scratch_shapes=[pltpu.VMEM((t, nst), F32)] * 4 + [pltpu.VMEM((SUBLANES, nst), F32)] * 2,
        compiler_params=_cparams(("arbitrary",)),
    )(dy, u_arr, hr, hi, ctre, ctim, btre, btim, consts, d_skip)


def _ssm_discretise(lam_re, lam_im, log_step):
    step = jnp.exp(log_step)
    e = jnp.exp(lam_re * step)
    ar, ai = e * jnp.cos(lam_im * step), e * jnp.sin(lam_im * step)
    den = lam_re * lam_re + lam_im * lam_im
    inv_r, inv_i = lam_re / den, -lam_im / den
    fr, fi = _cmul(ar - 1.0, ai, inv_r, inv_i)
    return step, ar, ai, inv_r, inv_i, fr, fi


def _ssm_prep_fn(lam_re, lam_im, log_step):
    _, ar, ai, _, _, fr, fi = _ssm_discretise(lam_re, lam_im, log_step)
    pw = [(ar, ai)]
    for n in range(1, 8):
        pw.append(_cmul(*pw[n - 1], ar, ai))
    return [p[0] for p in pw] + [p[1] for p in pw] + [fr, fi]


def _ssm_param_grad_fn(lam_re, lam_im, log_step, da_r, da_i, df_r, df_i):
    step, ar, ai, inv_r, inv_i, fr, fi = _ssm_discretise(lam_re, lam_im, log_step)
    pr, pi = _cmul_conj(inv_r, inv_i, df_r, df_i)
    dat_r, dat_i = da_r + pr, da_i + pi
    wr, wi = _cmul(fr, fi, inv_r, inv_i)
    dl_r, dl_i = _cmul_conj(-wr, -wi, df_r, df_i)
    dz_r, dz_i = _cmul_conj(ar, ai, dat_r, dat_i)
    dl_r, dl_i = dl_r + step * dz_r, dl_i + step * dz_i
    dstep = jnp.sum(dz_r * lam_re + dz_i * lam_im, axis=-1, keepdims=True)
    return [dl_r, dl_i, step * dstep]


def _block_diag(v, rows_first):
    g, a, b = v.shape
    nb = g // GROUPS_PER_BLOCK
    eye = jnp.eye(GROUPS_PER_BLOCK, dtype=v.dtype)
    v = v.reshape(nb, GROUPS_PER_BLOCK, a, 1, b) * eye[None, :, None, :, None]
    return v.reshape(nb, GROUPS_PER_BLOCK * a, GROUPS_PER_BLOCK * b)


def _block_diag_take(v, a, b):
    nb = v.shape[0]
    v = v.reshape(nb, GROUPS_PER_BLOCK, a, GROUPS_PER_BLOCK, b)
    return jnp.stack([v[:, g, :, g, :] for g in range(GROUPS_PER_BLOCK)], axis=1).reshape(nb * GROUPS_PER_BLOCK, a, b)


def _adamw(w, g, m, v):
    m = ADAM_B1 * m + (1.0 - ADAM_B1) * g
    v = ADAM_B2 * v + (1.0 - ADAM_B2) * (g * g)
    m_hat = m / (1.0 - ADAM_B1**ADAM_STEP)
    v_hat = v / (1.0 - ADAM_B2**ADAM_STEP)
    delta = -ADAM_LR * (m_hat / (jnp.sqrt(v_hat) + ADAM_EPS) + ADAM_WD * w)
    return delta, m, v


def _adamw_rows(name, w, g, m, v):
    wd = w.shape[1]
    return _rowwise(name, lambda a, b, c, d: (list(_adamw(a, b, c, d)), []), [w, g, m, v], [], [(wd, F32)] * 3, [], tm=128)


SMALL = ["b_ada", "norm1_g", "q_norm_g", "k_norm_g", "lam_re", "lam_im", "log_step", "b_re", "b_im", "c_re", "c_im",
         "d_skip", "b_glu", "attn_out_g", "ssm_out_g", "norm2_g"]
LARGE = ["w_in", "w_glu", "w_out", "w_ff1", "w_ff2"]
ORDER = ["w_ada", "b_ada", "norm1_g", "w_in", "q_norm_g", "k_norm_g", "lam_re", "lam_im", "log_step", "b_re", "b_im", "c_re",
         "c_im", "d_skip", "w_glu", "b_glu", "attn_out_g", "ssm_out_g", "w_out", "norm2_g", "w_ff1", "w_ff2"]


def _pack(arrs):
    flat = jnp.concatenate([a.reshape(-1) for a in arrs])
    rows = -(-flat.shape[0] // PACK_LANES)
    rows = -(-rows // SUBLANES) * SUBLANES
    return jnp.pad(flat, (0, rows * PACK_LANES - flat.shape[0])).reshape(rows, PACK_LANES)


def _unpack(packed, shapes):
    flat, out, o = packed.reshape(-1), [], 0
    for sh in shapes:
        n = int(np.prod(sh))
        out.append(flat[o : o + n].reshape(sh))
        o += n
    return out


def kernel(x, c, w_ada, b_ada, norm1_g, w_in, q_norm_g, k_norm_g, lam_re, lam_im, log_step, b_re, b_im, c_re, c_im, d_skip, w_glu, b_glu, attn_out_g, ssm_out_g, w_out, norm2_g, w_ff1, w_ff2, loss_target, m_w_ada, m_b_ada, m_norm1_g, m_w_in, m_q_norm_g, m_k_norm_g, m_lam_re, m_lam_im, m_log_step, m_b_re, m_b_im, m_c_re, m_c_im, m_d_skip, m_w_glu, m_b_glu, m_attn_out_g, m_ssm_out_g, m_w_out, m_norm2_g, m_w_ff1, m_w_ff2, v_w_ada, v_b_ada, v_norm1_g, v_w_in, v_q_norm_g, v_k_norm_g, v_lam_re, v_lam_im, v_log_step, v_b_re, v_b_im, v_c_re, v_c_im, v_d_skip, v_w_glu, v_b_glu, v_attn_out_g, v_ssm_out_g, v_w_out, v_norm2_g, v_w_ff1, v_w_ff2):
    args = dict(locals())
    wts = {n: args[n] for n in ORDER}
    mom = {n: args["m_" + n] for n in ORDER}
    var = {n: args["v_" + n] for n in ORDER}

    x2, tgt = x[0], loss_target[0]
    s, d = x2.shape
    aw = d // 2
    sw = d - aw
    n_groups = sw // SSM_GROUP
    nst = n_groups * STATE_DIM
    chip = 2 * lax.axis_index("x") + lax.axis_index("y")

    chip_idx = jnp.reshape(chip, (1,)).astype(jnp.int32)
    core_idx = jnp.reshape(lax.axis_index("c"), (1,)).astype(jnp.int32)

    def cast_into_stack(n):
        w2 = wts[n][0]
        rows, cols = w2.shape
        tr = _blk(rows, 256)
        return _indexed("cast_" + n, lambda a: a, chip_idx, [(w2, (tr, cols), lambda r, ix: (r, 0))],
                        (((4, rows, cols), BF16), (None, tr, cols), lambda r, ix: (ix[0], r, 0)), (rows // tr,))

    stacks = {n: cast_into_stack(n) for n in LARGE}
    dff = 4 * wts["w_ff2"].shape[1]

    c_all = _allgather8("gather_c", jnp.pad(c, ((0, SUBLANES - 1), (0, 0))))[::SUBLANES]
    n_mod = w_ada.shape[2]
    b_ada_mine = lax.dynamic_slice(b_ada, (0, chip * n_mod), (1, n_mod))
    sil_c = _single("silu_c", lambda v: [v * _sigmoid(v)], [c_all], [(c_all.shape, F32)])[0]
    mod_part = _mm("ada_fwd", sil_c, w_ada[0], "nn", F32, tn=512, epi=lambda acc, b: acc + b[0:1, :],
                   extras=[jnp.broadcast_to(b_ada_mine, (SUBLANES, n_mod))])
    mod_all = _allgather8("gather_mod", mod_part)
    me = 2 * chip + lax.axis_index("c")
    mod = jnp.concatenate([lax.dynamic_slice(mod_all, (16 * k + me, 0), (1, n_mod)) for k in range(4)], axis=1)
    sh1, sc1, g1, sh2, sc2, g2 = [mod[:, i * d : (i + 1) * d] for i in range(6)]

    (w_in_s,) = _gather_weights([stacks["w_in"]], mod_all)

    def gather_plan(refs):
        px0, py0, pc, chips = _place()
        slot = 2 * px0 + py0
        return [(r.at[slot], r.at[slot], (px, py, pc), r.at[2 * px + py]) for r in refs for (px, py) in chips]

    mix_sems_s, mix_sems_r, mix_stacks, mix_token = _split_start(
        "gather_mix_start", [stacks["w_glu"], stacks["w_out"]], gather_plan, 6, extra=[w_in_s])
    w_in_s = mix_stacks.pop()
    ff_sems_s, ff_sems_r, ff_stacks, ff_token = _split_start(
        "gather_ff_start", [stacks["w_ff1"], stacks["w_ff2"]], gather_plan, 6, extra=[w_in_s])
    w_in_s = ff_stacks.pop()
    sc1 = sc1 + (mix_token[0:1, 0:1] + ff_token[0:1, 0:1])

    def norm_mod(v, g, sc, sh):
        return [v * _rstd(v) * g * (1.0 + sc) + sh], []

    (h1b,) = _rowwise("norm1", norm_mod, [x2], [norm1_g, sc1, sh1], [(d, BF16)], [])
    proj = _mm("proj_in", h1b, w_in_s, "nn", F32, b_cs=True)
    assert aw == sw
    q_src, k_src, v_src, u_src = [(proj, aw, i) for i in range(3)] + [(proj, sw, 3)]

    def qk_prep(q, k, v, gq, gk):
        qn = jnp.concatenate([h * _rstd(h) * gq for h in _heads(q)], axis=1)
        kn = jnp.concatenate([h * _rstd(h) * gk for h in _heads(k)], axis=1)
        return [qn, kn, v], []

    qn, kn, vb = _rowwise("qk_prep", qk_prep, [q_src, k_src, v_src], [q_norm_g, k_norm_g], [(aw, BF16)] * 3, [])

    pats = []
    for window, dil in DILATION_PATTERNS:
        assert window // dil == BAND and s % (dil * BAND) == 0
        bps = s // dil // BAND
        qp, kp, vp = [_to_streams(t, dil) for t in (qn, kn, vb)]
        o_p, l_p = _attn_fwd(f"attn_fwd_d{dil}", qp, kp, vp, bps, dil)
        pats.append((dil, bps, qp, kp, vp, _from_streams(o_p, dil), _from_streams(l_p, dil)))

    def combine(o1, o2, o3, l1, l2, l3):
        m = jnp.maximum(jnp.maximum(l1, l2), l3)
        tot = m + jnp.log(jnp.exp(l1 - m) + jnp.exp(l2 - m) + jnp.exp(l3 - m))
        return [jnp.exp(l1 - tot) * o1 + jnp.exp(l2 - tot) * o2 + jnp.exp(l3 - tot) * o3, tot], []

    attn, lse = _rowwise("attn_mix", combine, [p[5] for p in pats] + [p[6] for p in pats], [], [(aw, F32)] * 2, [])

    lam_re2, lam_im2, log_step2 = lam_re[0], lam_im[0], log_step[0].reshape(n_groups, 1)
    prep = _single("ssm_prep", _ssm_prep_fn, [lam_re2, lam_im2, log_step2], [((n_groups, STATE_DIM), F32)] * 18)
    p_r = jnp.stack([p.reshape(nst) for p in prep[:8]])
    p_i = jnp.stack([p.reshape(nst) for p in prep[8:16]])
    f_ri = jnp.stack(prep[16:])
    row = jnp.arange(SUBLANES)[:, None]
    zero = jnp.zeros((SUBLANES, nst), F32)
    sel = lambda cond, v: jnp.where(cond, jnp.broadcast_to(v[None, :], (SUBLANES, nst)), zero)
    fwd_consts = jnp.stack([sel(row >= 1, p_r[0]), sel(row >= 1, p_i[0]), sel(row >= 2, p_r[1]), sel(row >= 2, p_i[1]),
                            sel(row >= 4, p_r[3]), sel(row >= 4, p_i[3]), p_r, p_i])
    bwd_consts = jnp.stack([sel(row <= 6, p_r[0]), sel(row <= 6, p_i[0]), sel(row <= 5, p_r[1]), sel(row <= 5, p_i[1]),
                            sel(row <= 3, p_r[3]), sel(row <= 3, p_i[3]), p_r[::-1], p_i[::-1]])
    nblk = n_groups // GROUPS_PER_BLOCK
    f_rows = f_ri.reshape(2, nblk, 1, 512)
    braw_r = _block_diag(b_re[0].transpose(0, 2, 1), True)
    braw_i = _block_diag(b_im[0].transpose(0, 2, 1), True)

    def bbar_fn(br, bi, f):
        r, i = _cmul(f[0], f[1], br, bi)
        return [r, i]

    bbar_r, bbar_i = _single("ssm_bbar", bbar_fn, [braw_r, braw_i, f_rows], [(braw_r.shape, BF16)] * 2)
    ct_r, ct_i = _block_diag(c_re[0], True).astype(BF16), _block_diag(c_im[0], True).astype(BF16)
    ds2 = d_skip
    y_ssm, h_r, h_i = _ssm_fwd(u_src, bbar_r, bbar_i, ct_r.transpose(0, 2, 1), ct_i.transpose(0, 2, 1), fwd_consts, ds2)

    (yb,) = _rowwise("gelu", lambda v: ([_gelu(v)], []), [y_ssm], [], [(sw, BF16)], [])
    w_glu_s, w_out_s = _split_wait("gather_mix_wait", mix_stacks, mix_sems_s, mix_sems_r, yb, gather_plan)
    w_glu_f = w_glu_s.reshape(sw, sw)
    w_out_f = w_out_s.reshape(d, d)
    z = _mm("glu_fwd", yb, w_glu_f, "nn", F32)

    def glu(ys, zz, b):
        return [_gelu(ys) * _sigmoid(zz + b)], []

    (ssm,) = _rowwise("glu", glu, [y_ssm, z], [b_glu], [(sw, F32)], [])

    def cat_norm(a, sm, ga, gs):
        return [jnp.concatenate([a * _rstd(a) * ga, sm * _rstd(sm) * gs], axis=1)], []

    (catb,) = _rowwise("cat_norm", cat_norm, [attn, ssm], [attn_out_g, ssm_out_g], [(d, BF16)], [])
    mixed = _mm("mix_out", catb, w_out_f, "nn", F32)

    def resid_norm(xv, mx, g1v, g, sc, sh):
        x1v = xv + g1v * mx
        return [x1v, x1v * _rstd(x1v) * g * (1.0 + sc) + sh], []

    x1, h2b = _rowwise("resid_norm2", resid_norm, [x2, mixed], [g1, norm2_g, sc2, sh2], [(d, F32), (d, BF16)], [])
    w_ff1_s, w_ff2_s = _split_wait("gather_ff_wait", ff_stacks, ff_sems_s, ff_sems_r, x1, gather_plan)
    w_ff2_f = w_ff2_s.reshape(dff, d)
    a_ff = _mm("ff1", h2b, w_ff1_s, "nn", BF16, b_cs=True)
    sq_relu = lambda v: jnp.square(jnp.maximum(v.astype(F32), 0.0)).astype(BF16)
    ff = _mm("ff2", a_ff, w_ff2_f, "nn", F32, a_pro=sq_relu)

    def loss_fn(x1v, ffv, tg, g2v):
        diff = x1v + g2v * ffv - tg
        dout = diff * (1.0 / d)
        return [dout, dout * g2v], [_colsum8(0.5 * diff * dout), _colsum8(dout * ffv)]

    dout, dffb, loss_acc, dg2_acc = _rowwise("loss", loss_fn, [x1, ff, tgt], [g2], [(d, F32), (d, BF16)], [(SUBLANES, d)] * 2)

    da_ff = _mm("ff2_bwd", dffb, w_ff2_f, "nt", BF16, epi=lambda acc, av: acc * (2.0 * jnp.maximum(av.astype(F32), 0.0)), extras=[a_ff])
    gw = {}
    gw["w_ff2"] = _mm("ff2_wgrad", a_ff, dffb, "tn", BF16, a_pro=sq_relu).reshape(4, dff // 4, d)
    gw["w_ff1"] = _mm("ff1_wgrad", h2b, da_ff, "tn", BF16, o_cs=True)

    def add2(n, g, b):
        _, half, cols = b.shape
        tr = _blk(half, 256)
        nrb = half // tr
        blk = (None, tr, cols)
        return _indexed("grad_add2_" + n, lambda p, q: p.astype(F32) + q.astype(F32), core_idx,
                        [(g, blk, lambda k, r, ix: (k, ix[0] * nrb + r, 0)), (b, blk, lambda k, r, ix: (k, r, 0))],
                        ((b.shape, BF16), blk, lambda k, r, ix: (k, r, 0)), (4, nrb))

    def scatter_plan(refs):
        px0, py0, pc, chips = _place()
        nw = len(refs) // 2
        return [(refs[w].at[2 * px + py], refs[nw + w].at[j], (px, py, pc), refs[nw + w].at[j])
                for w in range(nw) for j, (px, py) in enumerate(chips)]

    early = ["w_ff2", "w_ff1"]
    parts = {n: add2(n, gw[n], b) for n, b in zip(early, _swap_halves([gw[n] for n in early], "swap_halves_ff"))}
    lands = [lax.empty((3,) + parts[n].shape[1:], BF16) for n in early]
    sc_sems_s, sc_sems_r, sc_bufs, sc_token = _split_start("scatter_ff_start", [parts[n] for n in early] + lands, scatter_plan, 6)
    sc2_late = sc2 + sc_token[0:1, 0:1]

    dh2 = _mm("ff1_bwd", da_ff, w_ff1_s, "nt", F32, b_cs=True)

    def norm2_bwd(dh, x1v, do, mx, g, sc, g1v):
        r = _rstd(x1v)
        xh = x1v * r
        dn = dh * (1.0 + sc)
        dx = do + _rms_bwd(dn * g, xh, r)
        return [dx, dx * g1v], [_colsum8(dh * xh * g), _colsum8(dh), _colsum8(dn * xh), _colsum8(dx * mx)]

    dx1, dmixb, dsc2_acc, dsh2_acc, dn2g_acc, dg1_acc = _rowwise(
        "norm2_bwd", norm2_bwd, [dh2, x1, dout, mixed], [norm2_g, sc2_late, g1], [(d, F32), (d, BF16)], [(SUBLANES, d)] * 4)

    dcat = _mm("mix_out_bwd", dmixb, w_out_f, "nt", F32)
    gw["w_out"] = _mm("mix_out_wgrad", catb, dmixb, "tn", BF16).reshape(4, d // 4, d)

    def cat_bwd(da, dsm, a, sm, ga, gs):
        ra, rs = _rstd(a), _rstd(sm)
        ah, sh = a * ra, sm * rs
        return [_rms_bwd(da * ga, ah, ra), _rms_bwd(dsm * gs, sh, rs)], [_colsum8(da * ah), _colsum8(dsm * sh)]

    dattn, dssm, dga_acc, dgs_acc = _rowwise(
        "cat_norm_bwd", cat_bwd, [(dcat, aw, 0), (dcat, sw, 1), attn, ssm], [attn_out_g, ssm_out_g], [(aw, F32), (sw, F32)],
        [(SUBLANES, aw), (SUBLANES, sw)])

    def glu_bwd(dsm, ys, zz, b):
        sg = _sigmoid(zz + b)
        dz = dsm * _gelu(ys) * sg * (1.0 - sg)
        return [dz, dsm * sg], [_colsum8(dz)]

    dzb, dy_direct, dbglu_acc = _rowwise("glu_bwd", glu_bwd, [dssm, y_ssm, z], [b_glu], [(sw, BF16), (sw, F32)], [(SUBLANES, sw)])
    gw["w_glu"] = _mm("glu_wgrad", yb, dzb, "tn", BF16).reshape(4, sw // 4, sw)
    dys = _mm("glu_bwd_mm", dzb, w_glu_f, "nt", F32, epi=lambda acc, direct, ys: (acc + direct) * _gelu_grad(ys), extras=[dy_direct, y_ssm])
    du, db_bar, dc_bd, da_acc, dds_acc = _ssm_bwd(dys, u_src, h_r, h_i, ct_r, ct_i, bbar_r.transpose(0, 2, 1),
                                                  bbar_i.transpose(0, 2, 1), bwd_consts, ds2)

    def delta_fn(da, a):
        dl = jnp.concatenate([jnp.broadcast_to(jnp.sum(p * q, axis=1, keepdims=True), p.shape) for p, q in zip(_heads(da), _heads(a))], axis=1)
        return [da, dl], []

    dob, delta = _rowwise("attn_delta", delta_fn, [dattn, attn], [], [(aw, BF16), (aw, F32)], [])
    dqs, dks, dvs = [], [], []
    for dil, bps, qp, kp, vp, _, _ in pats:
        dop, lsp, dlp = [_to_streams(t, dil) for t in (dob, lse, delta)]
        dq_p, dk_p, dv_p = _attn_bwd(f"attn_bwd_d{dil}", qp, kp, vp, dop, lsp, dlp, bps, dil)
        dqs.append(_from_streams(dq_p, dil))
        dks.append(_from_streams(dk_p, dil))
        dvs.append(_from_streams(dv_p, dil))

    def qk_bwd(dq1, dq2, dq3, dk1, dk2, dk3, dv1, dv2, dv3, duv, q, k, gq, gk):
        def one(dn, raw, g):
            outs, acc = [], 0.0
            for dh, h in zip(_heads(dn), _heads(raw)):
                r = _rstd(h)
                hh = h * r
                outs.append(_rms_bwd(dh * g, hh, r))
                acc = acc + _colsum8(dh * hh)
            return jnp.concatenate(outs, axis=1), acc

        dq, gq_acc = one(dq1 + dq2 + dq3, q, gq)
        dk, gk_acc = one(dk1 + dk2 + dk3, k, gk)
        return [jnp.concatenate([dq, dk, dv1 + dv2 + dv3, duv], axis=1)], [gq_acc, gk_acc]

    dprojb, dgq_acc, dgk_acc = _rowwise("qk_norm_bwd", qk_bwd, dqs + dks + dvs + [du, q_src, k_src], [q_norm_g, k_norm_g],
                                        [(3 * aw + sw, BF16)], [(SUBLANES, HEAD_DIM)] * 2, tm=128)
    gw["w_in"] = _mm("proj_in_wgrad", h1b, dprojb, "tn", BF16, o_cs=True)
    dh1 = _mm("proj_in_bwd", dprojb, w_in_s, "nt", F32, b_cs=True)

    def norm1_bwd(dh, xv, dxr, g, sc):
        r = _rstd(xv)
        xh = xv * r
        dn = dh * (1.0 + sc)
        return [dxr + _rms_bwd(dn * g, xh, r)], [_colsum8(dh * xh * g), _colsum8(dh), _colsum8(dn * xh)]

    grad_x, dsc1_acc, dsh1_acc, dn1g_acc = _rowwise("norm1_bwd", norm1_bwd, [dh1, x2, dx1], [norm1_g, sc1], [(d, F32)], [(SUBLANES, d)] * 3)

    def b_grad_fn(dbb, br, bi, f):
        fr, fi = f[0], f[1]
        dbr, dbi = _cmul_conj(fr, fi, dbb[0], dbb[1])
        dfr = jnp.sum(dbb[0] * br + dbb[1] * bi, axis=1, keepdims=True)
        dfi = jnp.sum(dbb[1] * br - dbb[0] * bi, axis=1, keepdims=True)
        return [dbr, dbi, dfr, dfi]

    dbr_bd, dbi_bd, df_r, df_i = _single("ssm_b_grad", b_grad_fn, [db_bar, braw_r, braw_i, f_rows],
                                         [(braw_r.shape, F32)] * 2 + [((nblk, 1, 512), F32)] * 2)

    def sums_fn(*accs):
        return [jnp.sum(a, axis=-2, keepdims=True) for a in accs]

    acc_list = [dsh1_acc, dsc1_acc, dg1_acc, dsh2_acc, dsc2_acc, dg2_acc, dn1g_acc, dgq_acc, dgk_acc, dds_acc, dbglu_acc, dga_acc,
                dgs_acc, dn2g_acc, loss_acc, da_acc]
    sums = _single("small_sums", sums_fn, acc_list, [(a.shape[:-2] + (1, a.shape[-1]), F32) for a in acc_list])
    (dsh1, dsc1, dg1, dsh2, dsc2, dg2, dn1g, dgq, dgk, dds, dbglu, dga, dgs, dn2g, loss_row, da_sum) = sums
    loss_dev = _single("loss_sum", lambda v: [jnp.sum(v, axis=1, keepdims=True)], [loss_row], [((1, 1), F32)])[0]
    gp = (n_groups, STATE_DIM)
    dlam_re, dlam_im, dlog_step = _single(
        "ssm_param_grad", _ssm_param_grad_fn,
        [lam_re2, lam_im2, log_step2, da_sum[0].reshape(gp), da_sum[1].reshape(gp), df_r.reshape(gp), df_i.reshape(gp)],
        [(gp, F32), (gp, F32), ((n_groups, 1), F32)])
    small_dev = {
        "b_ada": jnp.concatenate([dsh1, dsc1, dg1, dsh2, dsc2, dg2], axis=1),
        "norm1_g": dn1g, "q_norm_g": dgq, "k_norm_g": dgk, "lam_re": dlam_re, "lam_im": dlam_im, "log_step": dlog_step,
        "b_re": _block_diag_take(dbr_bd, SSM_GROUP, STATE_DIM).transpose(0, 2, 1),
        "b_im": _block_diag_take(dbi_bd, SSM_GROUP, STATE_DIM).transpose(0, 2, 1),
        "c_re": _block_diag_take(dc_bd[0], SSM_GROUP, STATE_DIM), "c_im": _block_diag_take(dc_bd[1], SSM_GROUP, STATE_DIM),
        "d_skip": dds, "b_glu": dbglu, "attn_out_g": dga, "ssm_out_g": dgs, "norm2_g": dn2g,
    }

    packed = _pack([small_dev[n] for n in SMALL])
    rows = packed.shape[0]
    gathered = _allgather8("gather_small", packed).reshape(8, rows, PACK_LANES)

    def sum8_fn(g):
        acc = g[0]
        for i in range(1, 8):
            acc = acc + g[i]
        return [acc]

    g_small = _single("sum_small", sum8_fn, [gathered], [((rows, PACK_LANES), F32)])[0]
    w_small, m_small, v_small = [_pack([src[n] for n in SMALL]) for src in (wts, mom, var)]
    upd_small = _adamw_rows("adamw_small", w_small, g_small, m_small, v_small)
    shapes = [wts[n].shape for n in SMALL]
    out = {}
    for kind, arr in zip(("grad", "delta", "new_m", "new_v"), (g_small,) + tuple(upd_small)):
        for n, val in zip(SMALL, _unpack(arr, shapes)):
            out[kind, n] = val

    dmod_all = gathered[:, 0 : -(-6 * d // PACK_LANES), :].reshape(8, -1)[:, : 6 * d]
    dmod_mine = lax.dynamic_slice(dmod_all, (0, chip * n_mod), (8, n_mod))
    g_ada = _mm("ada_wgrad", sil_c, dmod_mine, "tn", F32, tm=512, tn=512)
    upd = _adamw_rows("adamw_w_ada", w_ada[0], g_ada, m_w_ada[0], v_w_ada[0])
    for kind, val in zip(("grad", "delta", "new_m", "new_v"), (g_ada,) + tuple(upd)):
        out[kind, "w_ada"] = val[None]

    late = [n for n in LARGE if n not in early]
    parts.update({n: add2(n, gw[n], b) for n, b in zip(late, _swap_halves([gw[n] for n in late], "swap_halves_rest"))})
    got2 = dict(zip(late, _scatter_chips([parts[n] for n in late])))
    sc_bufs = _split_wait("scatter_ff_wait", sc_bufs, sc_sems_s, sc_sems_r, grad_x, scatter_plan)
    parts.update(zip(early, sc_bufs[: len(early)]))
    got2.update(zip(early, sc_bufs[len(early) :]))

    def add4(n, p, b):
        _, half, cols = p.shape
        tr = _blk(half, 256)
        blk = (None, tr, cols)
        fn = lambda own, q0, q1, q2: ((own.astype(F32) + q0.astype(F32)) + q1.astype(F32)) + q2.astype(F32)
        srcs = [(p, blk, lambda r, ix: (ix[0], r, 0))] + [(b, blk, lambda r, ix, j=j: (j, r, 0)) for j in range(3)]
        return _indexed("grad_add4_" + n, fn, chip_idx, srcs, (((half, cols), F32), (tr, cols), lambda r, ix: (r, 0)), (half // tr,))

    finals = [add4(n, parts[n], got2[n]) for n in LARGE]
    for n, full in zip(LARGE, _join_halves(finals)):
        g = full.reshape(wts[n].shape[1:])
        upd = _adamw_rows("adamw_" + n, wts[n][0], g, mom[n][0], var[n][0])
        for kind, val in zip(("grad", "delta", "new_m", "new_v"), (g,) + tuple(upd)):
            out[kind, n] = val[None]

    loss = lax.psum(loss_dev[0, 0], ("x", "y", "c"))
    return (loss, grad_x[None], *[out[kind, n] for kind in ("grad", "delta", "new_m", "new_v") for n in ORDER])
```

```python
import functools
import math

import jax
import jax.numpy as jnp
import numpy as np
from jax import lax
from jax.experimental import pallas as pl
from jax.experimental.pallas import tpu as pltpu

F32 = jnp.float32
BF16 = jnp.bfloat16
MESH = pl.DeviceIdType.MESH

EPS = 1e-6
HEAD_DIM = 128
SSM_GROUP = 16
STATE_DIM = 64
GROUPS_PER_BLOCK = 8
DILATION_PATTERNS = ((128, 1), (512, 4), (2048, 16))
BAND = 128
NEG = -1e30

ADAM_LR, ADAM_B1, ADAM_B2, ADAM_EPS, ADAM_WD, ADAM_STEP = 0.001, 0.9, 0.999, 1e-08, 0.01, 10

V7X_VMEM_BYTES = 64 * 2**20
VMEM_LIMIT = 48 * 2**20
JOIN_VMEM_LIMIT = 56 * 2**20
SUBLANES = 8
PACK_LANES = 1024


def _cparams(sem=None):
    return pltpu.CompilerParams(dimension_semantics=sem, vmem_limit_bytes=VMEM_LIMIT)


def _blk(n, want):
    b = min(n, want)
    while n % b:
        b //= 2
    return b


def _colsum8(v):
    tm, w = v.shape
    return v.reshape(tm // SUBLANES, SUBLANES, w).sum(axis=0)


def _rowwise(name, fn, rows, vecs, out_rows, out_accs, tm=256, n_rows=None):
    rows = [r if isinstance(r, tuple) else (r, r.shape[1], 0) for r in rows]
    rows = [r if len(r) == 4 else r + (0,) for r in rows]
    s = rows[0][0].shape[0] if n_rows is None else n_rows
    tm = _blk(s, tm)
    assert all(r[3] % tm == 0 for r in rows)
    n_in, n_or = len(rows) + len(vecs), len(out_rows)

    def body(*refs):
        outs, accs = fn(*[r[...] for r in refs[:n_in]])
        for r, v in zip(refs[n_in : n_in + n_or], outs):
            r[...] = v.astype(r.dtype)
        if out_accs:
            acc_refs = refs[n_in + n_or :]

            @pl.when(pl.program_id(0) == 0)
            def _():
                for r in acc_refs:
                    r[...] = jnp.zeros(r.shape, r.dtype)

            for r, v in zip(acc_refs, accs):
                r[...] += v

    in_specs = [pl.BlockSpec((tm, w), lambda i, cb=cb, rb=r0 // tm: (i + rb, cb)) for (_, w, cb, r0) in rows]
    in_specs += [pl.BlockSpec(v.shape, lambda i, nd=v.ndim: (0,) * nd) for v in vecs]
    out_specs = [pl.BlockSpec((tm, w), lambda i: (i, 0)) for (w, _) in out_rows]
    out_specs += [pl.BlockSpec(sh, lambda i, nd=len(sh): (0,) * nd) for sh in out_accs]
    out_shape = [jax.ShapeDtypeStruct((s, w), dt) for (w, dt) in out_rows]
    out_shape += [jax.ShapeDtypeStruct(sh, F32) for sh in out_accs]
    res = pl.pallas_call(
        body, name=name, grid=(s // tm,), in_specs=in_specs, out_specs=out_specs, out_shape=out_shape,
        compiler_params=_cparams(("arbitrary",)),
    )(*[r[0] for r in rows], *vecs)
    return res


def _single(name, fn, ins, out_shapes):
    n_in = len(ins)

    def body(*refs):
        outs = fn(*[r[...] for r in refs[:n_in]])
        for r, v in zip(refs[n_in:], outs):
            r[...] = v.astype(r.dtype)

    vm = pl.BlockSpec(memory_space=pltpu.VMEM)
    return pl.pallas_call(
        body, name=name, in_specs=[vm] * n_in, out_specs=[vm] * len(out_shapes),
        out_shape=[jax.ShapeDtypeStruct(sh, dt) for sh, dt in out_shapes], compiler_params=_cparams(),
    )(*ins)


_NN = (((1,), (0,)), ((), ()))
_NT = (((1,), (1,)), ((), ()))
_TN = (((0,), (0,)), ((), ()))


def _mm(name, a, b, mode, out_dtype, *, tm=512, tn=1024, tk=2048, b_cs=False, o_cs=False, epi=None, extras=(), a_pro=None):
    if mode == "tn":
        k, m = a.shape
        n = b.shape[1]
    else:
        m, k = a.shape
        if mode == "nn":
            n = b.shape[2] * 4 if b_cs else b.shape[1]
        else:
            n = b.shape[1] if b_cs else b.shape[0]
    tm = _blk(m, tm)
    tn = _blk(n // 4 if (b_cs and mode == "nn") or o_cs else n, tn)
    tk = _blk(k // 4 if (b_cs and mode == "nt") else k, tk)
    nk = k // tk
    dims = {"nn": _NN, "nt": _NT, "tn": _TN}[mode]
    n_ex = len(extras)

    def body(*refs):
        a_ref, b_ref = refs[0], refs[1]
        ex_refs, o_ref = refs[2 : 2 + n_ex], refs[2 + n_ex]

        def finish(acc):
            if epi is not None:
                acc = epi(acc, *[r[...] for r in ex_refs])
            o_ref[...] = acc.astype(o_ref.dtype)

        a_blk = a_ref[...] if a_pro is None else a_pro(a_ref[...])
        part = lax.dot_general(a_blk, b_ref[...], dims, preferred_element_type=F32)
        if nk == 1:
            finish(part)
        else:
            acc_ref = refs[3 + n_ex]
            kk = pl.program_id(2)

            @pl.when(kk == 0)
            def _():
                acc_ref[...] = part

            @pl.when(kk > 0)
            def _():
                acc_ref[...] += part

            @pl.when(kk == nk - 1)
            def _():
                finish(acc_ref[...])

    a_spec = pl.BlockSpec((tk, tm), lambda i, j, kk: (kk, i)) if mode == "tn" else pl.BlockSpec((tm, tk), lambda i, j, kk: (i, kk))
    if mode == "nn":
        if b_cs:
            per = (n // 4) // tn
            b_spec = pl.BlockSpec((None, tk, tn), lambda i, j, kk: (j // per, kk, j % per))
        else:
            b_spec = pl.BlockSpec((tk, tn), lambda i, j, kk: (kk, j))
    elif mode == "nt":
        if b_cs:
            per = (k // 4) // tk
            b_spec = pl.BlockSpec((None, tn, tk), lambda i, j, kk: (kk // per, j, kk % per))
        else:
            b_spec = pl.BlockSpec((tn, tk), lambda i, j, kk: (j, kk))
    else:
        b_spec = pl.BlockSpec((tk, tn), lambda i, j, kk: (kk, j))
    if o_cs:
        per = (n // 4) // tn
        o_spec = pl.BlockSpec((None, tm, tn), lambda i, j, kk: (j // per, i, j % per))
        o_shape = jax.ShapeDtypeStruct((4, m, n // 4), out_dtype)
    else:
        o_spec = pl.BlockSpec((tm, tn), lambda i, j, kk: (i, j))
        o_shape = jax.ShapeDtypeStruct((m, n), out_dtype)
    ex_specs = [pl.BlockSpec((tm, tn), lambda i, j, kk: (i, j)) for _ in extras]
    return pl.pallas_call(
        body, name=name, grid=(m // tm, n // tn, nk), in_specs=[a_spec, b_spec] + ex_specs, out_specs=o_spec, out_shape=o_shape,
        scratch_shapes=[pltpu.VMEM((tm, tn), F32)] if nk > 1 else [],
        compiler_params=_cparams(("parallel", "parallel", "arbitrary")),
    )(a, b, *extras)


def _place():
    x, y, c = lax.axis_index("x"), lax.axis_index("y"), lax.axis_index("c")
    return x, y, c, [(1 - x, y), (x, 1 - y), (1 - x, 1 - y)]


def _allgather8(name, v):
    m_per, n = v.shape

    def body(x_ref, out_ref, send_sems, recv_sems, local_sem):
        x, y, c, chips = _place()
        me, sibling = (x, y, c), (x, y, 1 - c)

        def rows(px, py, pc):
            return out_ref.at[pl.ds((4 * px + 2 * py + pc) * m_per, m_per), :]

        def copy(k, block, to, src=None):
            return pltpu.make_async_remote_copy(
                src_ref=rows(*block) if src is None else src, dst_ref=rows(*block), send_sem=send_sems.at[k],
                recv_sem=recv_sems.at[k], device_id=to, device_id_type=MESH)

        mine = pltpu.make_async_copy(x_ref, rows(*me), local_sem)
        mine.start()
        first = [copy(0, me, sibling, src=x_ref)]
        first += [copy(1 + j, me, (*chip, c), src=x_ref) for j, chip in enumerate(chips)]
        for cp in first:
            cp.start()
        passed = [copy(4 + j, (*chip, c), sibling) for j, chip in enumerate(chips)]
        for j, chip in enumerate(chips):
            copy(1 + j, (*chip, c), me).wait_recv()
            passed[j].start()
        copy(0, sibling, me).wait_recv()
        for j, chip in enumerate(chips):
            copy(4 + j, (*chip, 1 - c), me).wait_recv()
        for cp in first + passed:
            cp.wait_send()
        mine.wait()

    return pl.pallas_call(
        body, name=name, out_shape=jax.ShapeDtypeStruct((8 * m_per, n), v.dtype),
        in_specs=[pl.BlockSpec(memory_space=pltpu.VMEM)], out_specs=pl.BlockSpec(memory_space=pltpu.VMEM),
        scratch_shapes=[pltpu.SemaphoreType.DMA((7,)), pltpu.SemaphoreType.DMA((7,)), pltpu.SemaphoreType.DMA],
        compiler_params=_cparams(),
    )(v)


def _gather_weights(stacks, after):
    nw = len(stacks)

    def body(*refs):
        outs = refs[nw + 1 : 2 * nw + 1]
        send_sems, recv_sems = refs[2 * nw + 1 :]
        x, y, c, chips = _place()
        me, sibling = (x, y, c), (x, y, 1 - c)

        def copy(w, k, block, to):
            px, py, pc = block
            half = outs[w].shape[1] // 2
            rows = outs[w].at[2 * px + py, pl.ds(pc * half, half), :]
            return pltpu.make_async_remote_copy(src_ref=rows, dst_ref=rows, send_sem=send_sems.at[6 * w + k],
                                                recv_sem=recv_sems.at[6 * w + k], device_id=to, device_id_type=MESH)

        sent = [copy(w, j, me, (*chip, c)) for w in range(nw) for j, chip in enumerate(chips)]
        for cp in sent:
            cp.start()
        for w in range(nw):
            for j, chip in enumerate(chips):
                copy(w, j, (*chip, c), me).wait_recv()
                passed = copy(w, 3 + j, (*chip, c), sibling)
                passed.start()
                sent.append(passed)
        for w in range(nw):
            for j, chip in enumerate(chips):
                copy(w, 3 + j, (*chip, 1 - c), me).wait_recv()
        for cp in sent:
            cp.wait_send()

    hbm = pl.BlockSpec(memory_space=pl.ANY)
    return pl.pallas_call(
        body, name="gather_weights", out_shape=[jax.ShapeDtypeStruct(w.shape, w.dtype) for w in stacks],
        in_specs=[hbm] * (nw + 1), out_specs=[hbm] * nw, input_output_aliases={i: i for i in range(nw)},
        scratch_shapes=[pltpu.SemaphoreType.DMA((6 * nw,)), pltpu.SemaphoreType.DMA((6 * nw,))],
        compiler_params=_cparams(),
    )(*stacks, after)


def _swap_halves(gs, name):
    nw = len(gs)

    def body(*refs):
        ins, gots = refs[:nw], refs[nw : 2 * nw]
        send_sems, recv_sems = refs[2 * nw :]
        x, y, c, _ = _place()
        cps = []
        for w in range(nw):
            half = ins[w].shape[1] // 2
            cps.append(pltpu.make_async_remote_copy(
                src_ref=ins[w].at[:, pl.ds((1 - c) * half, half), :], dst_ref=gots[w], send_sem=send_sems.at[w],
                recv_sem=recv_sems.at[w], device_id=(x, y, 1 - c), device_id_type=MESH))
            cps[-1].start()
        for cp in cps:
            cp.wait()

    hbm = pl.BlockSpec(memory_space=pl.ANY)
    return pl.pallas_call(
        body, name=name, out_shape=[jax.ShapeDtypeStruct((4, g.shape[1] // 2, g.shape[2]), g.dtype) for g in gs],
        in_specs=[hbm] * nw, out_specs=[hbm] * nw,
        scratch_shapes=[pltpu.SemaphoreType.DMA((nw,)), pltpu.SemaphoreType.DMA((nw,))], compiler_params=_cparams(),
    )(*gs)


def _scatter_chips(ps):
    nw = len(ps)

    def body(*refs):
        ins, gots = refs[:nw], refs[nw : 2 * nw]
        send_sems, recv_sems = refs[2 * nw :]
        x, y, c, chips = _place()
        cps = []
        for w in range(nw):
            for j, (px, py) in enumerate(chips):
                cps.append(pltpu.make_async_remote_copy(
                    src_ref=ins[w].at[2 * px + py], dst_ref=gots[w].at[j], send_sem=send_sems.at[3 * w + j],
                    recv_sem=recv_sems.at[3 * w + j], device_id=(px, py, c), device_id_type=MESH))
                cps[-1].start()
        for cp in cps:
            cp.wait()

    hbm = pl.BlockSpec(memory_space=pl.ANY)
    return pl.pallas_call(
        body, name="scatter_chips", out_shape=[jax.ShapeDtypeStruct((3,) + p.shape[1:], p.dtype) for p in ps],
        in_specs=[hbm] * nw, out_specs=[hbm] * nw,
        scratch_shapes=[pltpu.SemaphoreType.DMA((3 * nw,)), pltpu.SemaphoreType.DMA((3 * nw,))], compiler_params=_cparams(),
    )(*ps)


def _split_start(name, bufs, plan, n, extra=()):
    n_plan = len(bufs)
    bufs = list(bufs) + list(extra)
    nb = len(bufs)

    def body(*refs):
        send_sems, recv_sems, token = refs[nb], refs[nb + 1], refs[-1]
        for k, (src, dst, to, _) in enumerate(plan(refs[:n_plan])):
            pltpu.make_async_remote_copy(src_ref=src, dst_ref=dst, send_sem=send_sems.at[k], recv_sem=recv_sems.at[k],
                                         device_id=to, device_id_type=MESH).start()
        token[...] = jnp.zeros(token.shape, token.dtype)

    hbm, sem = pl.BlockSpec(memory_space=pltpu.HBM), pl.BlockSpec(memory_space=pltpu.SEMAPHORE)
    res = pl.pallas_call(
        body, name=name,
        out_shape=(pltpu.SemaphoreType.DMA((n,)), pltpu.SemaphoreType.DMA((n,)), *[pltpu.HBM(b.shape, b.dtype) for b in bufs],
                   jax.ShapeDtypeStruct((SUBLANES, 128), F32)),
        in_specs=(hbm,) * nb, out_specs=(sem, sem) + (hbm,) * nb + (pl.BlockSpec(memory_space=pltpu.VMEM),),
        input_output_aliases={i: 2 + i for i in range(nb)},
        compiler_params=pltpu.CompilerParams(has_side_effects=pltpu.SideEffectType.DATAFLOW_SIDE_EFFECTING),
    )(*[pltpu.with_memory_space_constraint(b, pltpu.HBM) for b in bufs])
    return res[0], res[1], list(res[2 : 2 + nb]), res[-1]


def _split_wait(name, bufs, send_sems, recv_sems, after, plan):
    nb = len(bufs)

    def body(*refs):
        send, recv = refs[nb], refs[nb + 1]
        for k, (src, _, to, land) in enumerate(plan(refs[:nb])):
            cp = pltpu.make_async_remote_copy(src_ref=src, dst_ref=land, send_sem=send.at[k], recv_sem=recv.at[k],
                                              device_id=to, device_id_type=MESH)
            cp.wait_send()
            cp.wait_recv()

    hbm, sem = pl.BlockSpec(memory_space=pltpu.HBM), pl.BlockSpec(memory_space=pltpu.SEMAPHORE)
    return pl.pallas_call(
        body, name=name, out_shape=tuple(pltpu.HBM(b.shape, b.dtype) for b in bufs),
        in_specs=(hbm,) * nb + (sem, sem, pl.BlockSpec(memory_space=pl.ANY)), out_specs=(hbm,) * nb,
        input_output_aliases={i: i for i in range(nb)},
        compiler_params=pltpu.CompilerParams(has_side_effects=pltpu.SideEffectType.DATAFLOW_SIDE_EFFECTING),
    )(*bufs, send_sems, recv_sems, after)


def _indexed(name, fn, idx, ins, out, grid):
    def body(idx_ref, *refs):
        refs[-1][...] = fn(*[r[...] for r in refs[:-1]]).astype(refs[-1].dtype)

    (o_shape, o_dtype), o_block, o_map = out
    return pl.pallas_call(
        body, name=name, out_shape=jax.ShapeDtypeStruct(o_shape, o_dtype),
        grid_spec=pltpu.PrefetchScalarGridSpec(
            num_scalar_prefetch=1, grid=grid, in_specs=[pl.BlockSpec(blk, mp) for (_, blk, mp) in ins],
            out_specs=pl.BlockSpec(o_block, o_map)),
        compiler_params=_cparams(("arbitrary",) * len(grid)),
    )(idx, *[a for (a, _, _) in ins])


def _join_halves(fs):
    nw = len(fs)

    def body(*refs):
        ins, outs, lands = refs[:nw], refs[nw : 2 * nw], refs[2 * nw : 3 * nw]
        send_sems, recv_sems, local_sems = refs[3 * nw :]
        x, y, c, _ = _place()
        sibling = (x, y, 1 - c)
        cps = [pltpu.make_async_remote_copy(src_ref=ins[w], dst_ref=lands[w], send_sem=send_sems.at[w], recv_sem=recv_sems.at[w],
                                            device_id=sibling, device_id_type=MESH) for w in range(nw)]
        outs_own = [pltpu.make_async_copy(ins[w], outs[w].at[c], local_sems.at[w]) for w in range(nw)]
        outs_got = [pltpu.make_async_copy(lands[w], outs[w].at[1 - c], local_sems.at[nw + w]) for w in range(nw)]
        for cp in cps + outs_own:
            cp.start()
        for w in range(nw):
            cps[w].wait_recv()
            outs_got[w].start()
        for w in range(nw):
            cps[w].wait_send()
            outs_own[w].wait()
            outs_got[w].wait()

    hbm, vm = pl.BlockSpec(memory_space=pl.ANY), pl.BlockSpec(memory_space=pltpu.VMEM)
    return pl.pallas_call(
        body, name="join_halves", out_shape=[jax.ShapeDtypeStruct((2,) + f.shape, f.dtype) for f in fs],
        in_specs=[vm] * nw, out_specs=[hbm] * nw,
        scratch_shapes=[pltpu.VMEM(f.shape, f.dtype) for f in fs]
        + [pltpu.SemaphoreType.DMA((nw,)), pltpu.SemaphoreType.DMA((nw,)), pltpu.SemaphoreType.DMA((2 * nw,))],
        compiler_params=pltpu.CompilerParams(vmem_limit_bytes=JOIN_VMEM_LIMIT),
    )(*fs)


def _rstd(v):
    return lax.rsqrt(jnp.mean(v * v, axis=-1, keepdims=True) + EPS)


def _rms_bwd(dy, xh, r):
    return r * (dy - xh * jnp.mean(dy * xh, axis=-1, keepdims=True))


def _heads(v):
    return [v[:, h * HEAD_DIM : (h + 1) * HEAD_DIM] for h in range(v.shape[1] // HEAD_DIM)]


def _gelu(v):
    k = math.sqrt(2.0 / math.pi)
    return 0.5 * v * (1.0 + jnp.tanh(k * (v + 0.044715 * v * v * v)))


def _gelu_grad(v):
    k = math.sqrt(2.0 / math.pi)
    t = jnp.tanh(k * (v + 0.044715 * v * v * v))
    return 0.5 * (1.0 + t) + 0.5 * v * (1.0 - t * t) * k * (1.0 + 3 * 0.044715 * v * v)


def _sigmoid(v):
    return 1.0 / (1.0 + jnp.exp(-v))


def _slopes(n_heads):
    return [2.0 ** (-8.0 * (h + 1.0) / n_heads) for h in range(n_heads)]


def _band_iotas():
    ri = lax.broadcasted_iota(jnp.int32, (BAND, BAND), 0)
    ci = lax.broadcasted_iota(jnp.int32, (BAND, BAND), 1)
    return ri, ci


def _to_lanes(cols):
    rows = cols[0].shape[0]
    lane = lax.broadcasted_iota(jnp.int32, (rows, HEAD_DIM), 1)
    out = jnp.zeros((rows, HEAD_DIM), F32)
    for h, col in enumerate(cols):
        out = jnp.where(lane == h, col, out)
    return out


def _from_lanes(v, h, width):
    return jnp.broadcast_to(v[:, h : h + 1], (v.shape[0], width))


def _attn_fwd(name, q, k, v, bps, dil):
    s, aw = q.shape
    n_heads, nb = aw // HEAD_DIM, s // BAND
    scale = HEAD_DIM**-0.5
    slopes = _slopes(n_heads)

    def body(q_ref, kp_ref, kc_ref, vp_ref, vc_ref, o_ref, l_ref):
        b = pl.program_id(0)
        first = (b % bps) == 0
        ri, ci = _band_iotas()
        dist_c = (ri - ci).astype(F32)
        dist_p = (ri - ci + BAND).astype(F32)
        valid_c = ci <= ri
        valid_p = ci >= ri + jnp.where(first, BAND, 0)
        hs = range(n_heads)
        sls = [slice(h * HEAD_DIM, (h + 1) * HEAD_DIM) for h in hs]
        qs = [q_ref[:, sl] for sl in sls]
        sc = [lax.dot_general(qs[h], kc_ref[:, sls[h]], _NT, preferred_element_type=F32) for h in hs]
        sp = [lax.dot_general(qs[h], kp_ref[:, sls[h]], _NT, preferred_element_type=F32) for h in hs]
        sc = [jnp.where(valid_c, sc[h] * scale - (slopes[h] * dil) * dist_c, NEG) for h in hs]
        sp = [jnp.where(valid_p, sp[h] * scale - (slopes[h] * dil) * dist_p, NEG) for h in hs]
        m = [jnp.maximum(jnp.max(sc[h], axis=1, keepdims=True), jnp.max(sp[h], axis=1, keepdims=True)) for h in hs]
        pc = [jnp.exp(sc[h] - m[h]) for h in hs]
        pp = [jnp.exp(sp[h] - m[h]) for h in hs]
        den = [jnp.sum(pc[h], axis=1, keepdims=True) + jnp.sum(pp[h], axis=1, keepdims=True) for h in hs]
        acc = [jnp.dot(pc[h].astype(BF16), vc_ref[:, sls[h]], preferred_element_type=F32) for h in hs]
        acc = [acc[h] + jnp.dot(pp[h].astype(BF16), vp_ref[:, sls[h]], preferred_element_type=F32) for h in hs]
        o_ref[...] = jnp.concatenate([acc[h] / den[h] for h in hs], axis=1).astype(o_ref.dtype)
        l_ref[...] = _to_lanes([m[h] + jnp.log(den[h]) for h in hs])

    cur = pl.BlockSpec((BAND, aw), lambda b: (b, 0))
    prev = pl.BlockSpec((BAND, aw), lambda b: (jnp.maximum(b - 1, 0), 0))
    return pl.pallas_call(
        body, name=name, grid=(nb,), in_specs=[cur, prev, cur, prev, cur], out_specs=[cur, pl.BlockSpec((BAND, HEAD_DIM), lambda b: (b, 0))],
        out_shape=[jax.ShapeDtypeStruct((s, aw), BF16), jax.ShapeDtypeStruct((s, HEAD_DIM), F32)],
        compiler_params=_cparams(("arbitrary",)),
    )(q, k, k, v, v)


def _attn_bwd(name, q, k, v, do, lse, delta, bps, dil):
    s, aw = q.shape
    n_heads, nb = aw // HEAD_DIM, s // BAND
    scale = HEAD_DIM**-0.5
    slopes = _slopes(n_heads)

    def body(qc_ref, qn_ref, doc_ref, don_ref, lc_ref, ln_ref, dc_ref, dn_ref, k_ref, v_ref, dq_ref, dk_ref, dv_ref, carry):
        j = pl.program_id(0)

        @pl.when(j == 0)
        def _():
            carry[...] = jnp.zeros(carry.shape, F32)

        has_next = jnp.logical_and((j + 1) % bps != 0, j + 1 < nb)
        ri, ci = _band_iotas()
        dist_c = (ri - ci).astype(F32)
        dist_p = (ri - ci + BAND).astype(F32)
        valid_c = ci <= ri
        valid_n = ci >= ri + jnp.where(has_next, 0, BAND)
        carried = carry[...]
        hs = range(n_heads)
        sls = [slice(h * HEAD_DIM, (h + 1) * HEAD_DIM) for h in hs]
        ks, vs = [k_ref[:, sl] for sl in sls], [v_ref[:, sl] for sl in sls]
        sides = [(qc_ref, doc_ref, lc_ref, dc_ref, dist_c, valid_c), (qn_ref, don_ref, ln_ref, dn_ref, dist_p, valid_n)]
        res = []
        for q_ref, do_ref, l_ref, d_ref, dist, valid in sides:
            qs, dos = [q_ref[:, sl] for sl in sls], [do_ref[:, sl] for sl in sls]
            sc = [lax.dot_general(qs[h], ks[h], _NT, preferred_element_type=F32) for h in hs]
            dp = [lax.dot_general(dos[h], vs[h], _NT, preferred_element_type=F32) for h in hs]
            lse_v, delta_v = l_ref[...], d_ref[...]
            p = [jnp.where(valid, jnp.exp(sc[h] * scale - (slopes[h] * dil) * dist - _from_lanes(lse_v, h, BAND)), 0.0) for h in hs]
            ds = [(p[h] * (dp[h] - _from_lanes(delta_v, h, BAND))).astype(BF16) for h in hs]
            pb = [p[h].astype(BF16) for h in hs]
            dv = [lax.dot_general(pb[h], dos[h], _TN, preferred_element_type=F32) for h in hs]
            dk = [lax.dot_general(ds[h], qs[h], _TN, preferred_element_type=F32) for h in hs]
            dq = [jnp.dot(ds[h], ks[h], preferred_element_type=F32) for h in hs]
            res.append((dq, dk, dv))
        (dq_c, dk_c, dv_c), (dq_n, dk_n, dv_n) = res
        dq_ref[...] = (carried + jnp.concatenate(dq_c, axis=1) * scale).astype(dq_ref.dtype)
        carry[...] = jnp.concatenate(dq_n, axis=1) * scale
        dk_ref[...] = (jnp.concatenate([dk_c[h] + dk_n[h] for h in hs], axis=1) * scale).astype(dk_ref.dtype)
        dv_ref[...] = jnp.concatenate([dv_c[h] + dv_n[h] for h in hs], axis=1).astype(dv_ref.dtype)

    cur = pl.BlockSpec((BAND, aw), lambda b: (b, 0))
    nxt = pl.BlockSpec((BAND, aw), lambda b: (jnp.minimum(b + 1, nb - 1), 0))
    cur1 = pl.BlockSpec((BAND, HEAD_DIM), lambda b: (b, 0))
    nxt1 = pl.BlockSpec((BAND, HEAD_DIM), lambda b: (jnp.minimum(b + 1, nb - 1), 0))
    return pl.pallas_call(
        body, name=name, grid=(nb,), in_specs=[cur, nxt, cur, nxt, cur1, nxt1, cur1, nxt1, cur, cur], out_specs=[cur] * 3,
        out_shape=[jax.ShapeDtypeStruct((s, aw), BF16)] * 3, scratch_shapes=[pltpu.VMEM((BAND, aw), F32)],
        compiler_params=_cparams(("arbitrary",)),
    )(q, q, do, do, lse, lse, delta, delta, k, v)


def _to_streams(v, dil):
    if dil == 1:
        return v
    s, w = v.shape
    return v.reshape(s // dil, dil, w).transpose(1, 0, 2).reshape(s, w)


def _from_streams(v, dil):
    if dil == 1:
        return v
    s, w = v.shape
    return v.reshape(dil, s // dil, w).transpose(1, 0, 2).reshape(s, w)


SCAN_LANES = 1024
SSM_ROWS = 128
SSM_FWD_ROWS = 256
FF_TM = 1024


def _cmul(ar, ai, br, bi):
    return ar * br - ai * bi, ar * bi + ai * br


def _cmul_conj(ar, ai, br, bi):
    return ar * br + ai * bi, ar * bi - ai * br


def _ssm_fwd(u_src, bre, bim, cre, cim, consts, d_skip):
    u_arr, sw, ucb = u_src
    s = u_arr.shape[0]
    nblk = bre.shape[0]
    nst = nblk * 512
    t = _blk(s, SSM_FWD_ROWS)
    w = _blk(nst, SCAN_LANES)

    def body(u_ref, bre_ref, bim_ref, cre_ref, cim_ref, k_ref, ds_ref, y_ref, hr_ref, hi_ref, xr, xi, car_r, car_i):
        @pl.when(pl.program_id(0) == 0)
        def _():
            car_r[...] = jnp.zeros(car_r.shape, F32)
            car_i[...] = jnp.zeros(car_i.shape, F32)

        u = u_ref[...]
        ub = u.astype(BF16)
        for b in range(nblk):
            ch, st = slice(b * 128, (b + 1) * 128), slice(b * 512, (b + 1) * 512)
            xr[:, st] = jnp.dot(ub[:, ch], bre_ref[b], preferred_element_type=F32)
            xi[:, st] = jnp.dot(ub[:, ch], bim_ref[b], preferred_element_type=F32)
        for c in range(nst // w):
            ls = slice(c * w, (c + 1) * w)
            mult = [k_ref[i, :, ls] for i in range(8)]

            def step(i, carry, ls=ls, mult=mult):
                cr, ci = carry
                r0 = pl.multiple_of(i * SUBLANES, SUBLANES)
                hr, hi = xr[pl.ds(r0, SUBLANES), ls], xi[pl.ds(r0, SUBLANES), ls]
                for n, sh in enumerate((1, 2, 4)):
                    pr, pi = _cmul(mult[2 * n], mult[2 * n + 1], pltpu.roll(hr, sh, 0), pltpu.roll(hi, sh, 0))
                    hr, hi = hr + pr, hi + pi
                pr, pi = _cmul(mult[6], mult[7], cr, ci)
                hr, hi = hr + pr, hi + pi
                xr[pl.ds(r0, SUBLANES), ls] = hr
                xi[pl.ds(r0, SUBLANES), ls] = hi
                return jnp.broadcast_to(hr[7:8, :], hr.shape), jnp.broadcast_to(hi[7:8, :], hi.shape)

            cr, ci = lax.fori_loop(0, t // SUBLANES, step, (car_r[:, ls], car_i[:, ls]))
            car_r[:, ls] = cr
            car_i[:, ls] = ci
        hrb, hib = xr[...].astype(BF16), xi[...].astype(BF16)
        hr_ref[...] = hrb
        hi_ref[...] = hib
        for b in range(nblk):
            ch, st = slice(b * 128, (b + 1) * 128), slice(b * 512, (b + 1) * 512)
            yb = jnp.dot(hrb[:, st], cre_ref[b], preferred_element_type=F32)
            yb -= jnp.dot(hib[:, st], cim_ref[b], preferred_element_type=F32)
            y_ref[:, ch] = yb + ds_ref[:, ch] * u[:, ch]

    whole = lambda a: pl.BlockSpec(a.shape, lambda i, nd=a.ndim: (0,) * nd)
    return pl.pallas_call(
        body, name="ssm_fwd", grid=(s // t,),
        in_specs=[pl.BlockSpec((t, sw), lambda i: (i, ucb))] + [whole(a) for a in (bre, bim, cre, cim, consts, d_skip)],
        out_specs=[pl.BlockSpec((t, sw), lambda i: (i, 0)), pl.BlockSpec((t, nst), lambda i: (i, 0)), pl.BlockSpec((t, nst), lambda i: (i, 0))],
        out_shape=[jax.ShapeDtypeStruct((s, sw), F32), jax.ShapeDtypeStruct((s, nst), BF16), jax.ShapeDtypeStruct((s, nst), BF16)],
        scratch_shapes=[pltpu.VMEM((t, nst), F32), pltpu.VMEM((t, nst), F32), pltpu.VMEM((SUBLANES, nst), F32), pltpu.VMEM((SUBLANES, nst), F32)],
        compiler_params=_cparams(("arbitrary",)),
    )(u_arr, bre, bim, cre, cim, consts, d_skip)


def _ssm_bwd(dy, u_src, hr, hi, ctre, ctim, btre, btim, consts, d_skip):
    u_arr, sw, ucb = u_src
    s = dy.shape[0]
    nblk = ctre.shape[0]
    nst = nblk * 512
    t = _blk(s, SSM_ROWS)
    w = _blk(nst, SCAN_LANES)
    nt = s // t

    def body(dy_ref, u_ref, hr_ref, hi_ref, ctre_ref, ctim_ref, btre_ref, btim_ref, k_ref, ds_ref,
             du_ref, db_ref, dc_ref, da_ref, dd_ref, lr, li, hfr, hfi, car_r, car_i):
        @pl.when(pl.program_id(0) == 0)
        def _():
            car_r[...] = jnp.zeros(car_r.shape, F32)
            car_i[...] = jnp.zeros(car_i.shape, F32)
            db_ref[...] = jnp.zeros(db_ref.shape, F32)
            dc_ref[...] = jnp.zeros(dc_ref.shape, F32)
            da_ref[...] = jnp.zeros(da_ref.shape, F32)
            dd_ref[...] = jnp.zeros(dd_ref.shape, F32)

        dyv, u = dy_ref[...], u_ref[...]
        dyb, ub = dyv.astype(BF16), u.astype(BF16)
        hrb, hib = hr_ref[...], hi_ref[...]
        hfr[...] = hrb.astype(F32)
        hfi[...] = hib.astype(F32)
        for b in range(nblk):
            ch, st = slice(b * 128, (b + 1) * 128), slice(b * 512, (b + 1) * 512)
            lr[:, st] = jnp.dot(dyb[:, ch], ctre_ref[b], preferred_element_type=F32)
            li[:, st] = -jnp.dot(dyb[:, ch], ctim_ref[b], preferred_element_type=F32)
        last_row = lax.broadcasted_iota(jnp.int32, (SUBLANES, w), 0) == SUBLANES - 1
        for c in range(nst // w):
            ls = slice(c * w, (c + 1) * w)
            mult = [k_ref[i, :, ls] for i in range(8)]

            def step(i, carry, ls=ls, mult=mult):
                cr, ci, ar, ai = carry
                r0 = pl.multiple_of((t // SUBLANES - 1 - i) * SUBLANES, SUBLANES)
                gr, gi = lr[pl.ds(r0, SUBLANES), ls], li[pl.ds(r0, SUBLANES), ls]
                for n, sh in enumerate((1, 2, 4)):
                    pr, pi = _cmul_conj(mult[2 * n], mult[2 * n + 1], pltpu.roll(gr, SUBLANES - sh, 0), pltpu.roll(gi, SUBLANES - sh, 0))
                    gr, gi = gr + pr, gi + pi
                pr, pi = _cmul_conj(mult[6], mult[7], cr, ci)
                gr, gi = gr + pr, gi + pi
                lr[pl.ds(r0, SUBLANES), ls] = gr
                li[pl.ds(r0, SUBLANES), ls] = gi
                nr = jnp.where(last_row, cr, pltpu.roll(gr, SUBLANES - 1, 0))
                ni = jnp.where(last_row, ci, pltpu.roll(gi, SUBLANES - 1, 0))
                fr, fi = hfr[pl.ds(r0, SUBLANES), ls], hfi[pl.ds(r0, SUBLANES), ls]
                pr, pi = _cmul_conj(fr, fi, nr, ni)
                return (jnp.broadcast_to(gr[0:1, :], gr.shape), jnp.broadcast_to(gi[0:1, :], gi.shape), ar + pr, ai + pi)

            zero = jnp.zeros((SUBLANES, w), F32)
            cr, ci, ar, ai = lax.fori_loop(0, t // SUBLANES, step, (car_r[:, ls], car_i[:, ls], zero, zero))
            car_r[:, ls] = cr
            car_i[:, ls] = ci
            da_ref[0, :, ls] += ar
            da_ref[1, :, ls] += ai
        lrb, lib = lr[...].astype(BF16), li[...].astype(BF16)
        for b in range(nblk):
            ch, st = slice(b * 128, (b + 1) * 128), slice(b * 512, (b + 1) * 512)
            dub = jnp.dot(lrb[:, st], btre_ref[b], preferred_element_type=F32)
            dub += jnp.dot(lib[:, st], btim_ref[b], preferred_element_type=F32)
            du_ref[:, ch] = dub + ds_ref[:, ch] * dyv[:, ch]
            db_ref[0, b] += lax.dot_general(ub[:, ch], lrb[:, st], _TN, preferred_element_type=F32)
            db_ref[1, b] += lax.dot_general(ub[:, ch], lib[:, st], _TN, preferred_element_type=F32)
            dc_ref[0, b] += lax.dot_general(dyb[:, ch], hrb[:, st], _TN, preferred_element_type=F32)
            dc_ref[1, b] -= lax.dot_general(dyb[:, ch], hib[:, st], _TN, preferred_element_type=F32)
        dd_ref[...] += _colsum8(dyv * u)

    whole = lambda a: pl.BlockSpec(a.shape, lambda i, nd=a.ndim: (0,) * nd)
    rev = lambda wd, cb=0: pl.BlockSpec((t, wd), lambda i, cb=cb: (nt - 1 - i, cb))
    acc = lambda sh: pl.BlockSpec(sh, lambda i, nd=len(sh): (0,) * nd)
    db_shape, da_shape, dd_shape = (2, nblk, 128, 512), (2, SUBLANES, nst), (SUBLANES, sw)
    return pl.pallas_call(
        body, name="ssm_bwd", grid=(nt,),
        in_specs=[rev(sw), rev(sw, ucb), rev(nst), rev(nst)] + [whole(a) for a in (ctre, ctim, btre, btim, consts, d_skip)],
        out_specs=[rev(sw), acc(db_shape), acc(db_shape), acc(da_shape), acc(dd_shape)],
        out_shape=[jax.ShapeDtypeStruct((s, sw), F32), jax.ShapeDtypeStruct(db_shape, F32), jax.ShapeDtypeStruct(db_shape, F32),
                   jax.ShapeDtypeStruct(da_shape, F32), jax.ShapeDtypeStruct(dd_shape, F32)],
        scratch_shapes=[pltpu.VMEM((t, nst), F32)] * 4 + [pltpu.VMEM((SUBLANES, nst), F32)] * 2,
        compiler_params=_cparams(("arbitrary",)),
    )(dy, u_arr, hr, hi, ctre, ctim, btre, btim, consts, d_skip)


def _ssm_discretise(lam_re, lam_im, log_step):
    step = jnp.exp(log_step)
    e = jnp.exp(lam_re * step)
    ar, ai = e * jnp.cos(lam_im * step), e * jnp.sin(lam_im * step)
    den = lam_re * lam_re + lam_im * lam_im
    inv_r, inv_i = lam_re / den, -lam_im / den
    fr, fi = _cmul(ar - 1.0, ai, inv_r, inv_i)
    return step, ar, ai, inv_r, inv_i, fr, fi


def _ssm_prep_fn(lam_re, lam_im, log_step):
    _, ar, ai, _, _, fr, fi = _ssm_discretise(lam_re, lam_im, log_step)
    pw = [(ar, ai)]
    for n in range(1, 8):
        pw.append(_cmul(*pw[n - 1], ar, ai))
    return [p[0] for p in pw] + [p[1] for p in pw] + [fr, fi]


def _ssm_param_grad_fn(lam_re, lam_im, log_step, da_r, da_i, df_r, df_i):
    step, ar, ai, inv_r, inv_i, fr, fi = _ssm_discretise(lam_re, lam_im, log_step)
    pr, pi = _cmul_conj(inv_r, inv_i, df_r, df_i)
    dat_r, dat_i = da_r + pr, da_i + pi
    wr, wi = _cmul(fr, fi, inv_r, inv_i)
    dl_r, dl_i = _cmul_conj(-wr, -wi, df_r, df_i)
    dz_r, dz_i = _cmul_conj(ar, ai, dat_r, dat_i)
    dl_r, dl_i = dl_r + step * dz_r, dl_i + step * dz_i
    dstep = jnp.sum(dz_r * lam_re + dz_i * lam_im, axis=-1, keepdims=True)
    return [dl_r, dl_i, step * dstep]


def _block_diag(v, rows_first):
    g, a, b = v.shape
    nb = g // GROUPS_PER_BLOCK
    eye = jnp.eye(GROUPS_PER_BLOCK, dtype=v.dtype)
    v = v.reshape(nb, GROUPS_PER_BLOCK, a, 1, b) * eye[None, :, None, :, None]
    return v.reshape(nb, GROUPS_PER_BLOCK * a, GROUPS_PER_BLOCK * b)


def _block_diag_take(v, a, b):
    nb = v.shape[0]
    v = v.reshape(nb, GROUPS_PER_BLOCK, a, GROUPS_PER_BLOCK, b)
    return jnp.stack([v[:, g, :, g, :] for g in range(GROUPS_PER_BLOCK)], axis=1).reshape(nb * GROUPS_PER_BLOCK, a, b)


def _adamw(w, g, m, v):
    m = ADAM_B1 * m + (1.0 - ADAM_B1) * g
    v = ADAM_B2 * v + (1.0 - ADAM_B2) * (g * g)
    m_hat = m / (1.0 - ADAM_B1**ADAM_STEP)
    v_hat = v / (1.0 - ADAM_B2**ADAM_STEP)
    delta = -ADAM_LR * (m_hat / (jnp.sqrt(v_hat) + ADAM_EPS) + ADAM_WD * w)
    return delta, m, v


def _adamw_rows(name, w, g, m, v):
    wd = w.shape[1]
    return _rowwise(name, lambda a, b, c, d: (list(_adamw(a, b, c, d)), []), [w, g, m, v], [], [(wd, F32)] * 3, [], tm=128)


SMALL = ["b_ada", "norm1_g", "q_norm_g", "k_norm_g", "lam_re", "lam_im", "log_step", "b_re", "b_im", "c_re", "c_im",
         "d_skip", "b_glu", "attn_out_g", "ssm_out_g", "norm2_g"]
LARGE = ["w_in", "w_glu", "w_out", "w_ff1", "w_ff2"]
ORDER = ["w_ada", "b_ada", "norm1_g", "w_in", "q_norm_g", "k_norm_g", "lam_re", "lam_im", "log_step", "b_re", "b_im", "c_re",
         "c_im", "d_skip", "w_glu", "b_glu", "attn_out_g", "ssm_out_g", "w_out", "norm2_g", "w_ff1", "w_ff2"]


def _pack(arrs):
    flat = jnp.concatenate([a.reshape(-1) for a in arrs])
    rows = -(-flat.shape[0] // PACK_LANES)
    rows = -(-rows // SUBLANES) * SUBLANES
    return jnp.pad(flat, (0, rows * PACK_LANES - flat.shape[0])).reshape(rows, PACK_LANES)


def _unpack(packed, shapes):
    flat, out, o = packed.reshape(-1), [], 0
    for sh in shapes:
        n = int(np.prod(sh))
        out.append(flat[o : o + n].reshape(sh))
        o += n
    return out


def kernel(x, c, w_ada, b_ada, norm1_g, w_in, q_norm_g, k_norm_g, lam_re, lam_im, log_step, b_re, b_im, c_re, c_im, d_skip, w_glu, b_glu, attn_out_g, ssm_out_g, w_out, norm2_g, w_ff1, w_ff2, loss_target, m_w_ada, m_b_ada, m_norm1_g, m_w_in, m_q_norm_g, m_k_norm_g, m_lam_re, m_lam_im, m_log_step, m_b_re, m_b_im, m_c_re, m_c_im, m_d_skip, m_w_glu, m_b_glu, m_attn_out_g, m_ssm_out_g, m_w_out, m_norm2_g, m_w_ff1, m_w_ff2, v_w_ada, v_b_ada, v_norm1_g, v_w_in, v_q_norm_g, v_k_norm_g, v_lam_re, v_lam_im, v_log_step, v_b_re, v_b_im, v_c_re, v_c_im, v_d_skip, v_w_glu, v_b_glu, v_attn_out_g, v_ssm_out_g, v_w_out, v_norm2_g, v_w_ff1, v_w_ff2):
    args = dict(locals())
    wts = {n: args[n] for n in ORDER}
    mom = {n: args["m_" + n] for n in ORDER}
    var = {n: args["v_" + n] for n in ORDER}

    x2, tgt = x[0], loss_target[0]
    s, d = x2.shape
    aw = d // 2
    sw = d - aw
    n_groups = sw // SSM_GROUP
    nst = n_groups * STATE_DIM
    chip = 2 * lax.axis_index("x") + lax.axis_index("y")

    chip_idx = jnp.reshape(chip, (1,)).astype(jnp.int32)
    core_idx = jnp.reshape(lax.axis_index("c"), (1,)).astype(jnp.int32)

    def cast_into_stack(n):
        w2 = wts[n][0]
        rows, cols = w2.shape
        tr = _blk(rows, 256)
        return _indexed("cast_" + n, lambda a: a, chip_idx, [(w2, (tr, cols), lambda r, ix: (r, 0))],
                        (((4, rows, cols), BF16), (None, tr, cols), lambda r, ix: (ix[0], r, 0)), (rows // tr,))

    stacks = {n: cast_into_stack(n) for n in LARGE}
    dff = 4 * wts["w_ff2"].shape[1]

    c_all = _allgather8("gather_c", jnp.pad(c, ((0, SUBLANES - 1), (0, 0))))[::SUBLANES]
    n_mod = w_ada.shape[2]
    b_ada_mine = lax.dynamic_slice(b_ada, (0, chip * n_mod), (1, n_mod))
    sil_c = _single("silu_c", lambda v: [v * _sigmoid(v)], [c_all], [(c_all.shape, F32)])[0]
    mod_part = _mm("ada_fwd", sil_c, w_ada[0], "nn", F32, tn=512, epi=lambda acc, b: acc + b[0:1, :],
                   extras=[jnp.broadcast_to(b_ada_mine, (SUBLANES, n_mod))])
    mod_all = _allgather8("gather_mod", mod_part)
    me = 2 * chip + lax.axis_index("c")
    mod = jnp.concatenate([lax.dynamic_slice(mod_all, (16 * k + me, 0), (1, n_mod)) for k in range(4)], axis=1)
    sh1, sc1, g1, sh2, sc2, g2 = [mod[:, i * d : (i + 1) * d] for i in range(6)]

    (w_in_s,) = _gather_weights([stacks["w_in"]], mod_all)

    def gather_plan(refs):
        px0, py0, pc, chips = _place()
        slot = 2 * px0 + py0
        return [(r.at[slot], r.at[slot], (px, py, pc), r.at[2 * px + py]) for r in refs for (px, py) in chips]

    mix_sems_s, mix_sems_r, mix_stacks, mix_token = _split_start(
        "gather_mix_start", [stacks["w_glu"], stacks["w_out"]], gather_plan, 6, extra=[w_in_s])
    w_in_s = mix_stacks.pop()
    ff_sems_s, ff_sems_r, ff_stacks, ff_token = _split_start(
        "gather_ff_start", [stacks["w_ff1"], stacks["w_ff2"]], gather_plan, 6, extra=[w_in_s])
    w_in_s = ff_stacks.pop()
    sc1 = sc1 + (mix_token[0:1, 0:1] + ff_token[0:1, 0:1])

    def norm_mod(v, g, sc, sh):
        return [v * _rstd(v) * g * (1.0 + sc) + sh], []

    (h1b,) = _rowwise("norm1", norm_mod, [x2], [norm1_g, sc1, sh1], [(d, BF16)], [])
    proj = _mm("proj_in", h1b, w_in_s, "nn", F32, b_cs=True)
    assert aw == sw
    q_src, k_src, v_src, u_src = [(proj, aw, i) for i in range(3)] + [(proj, sw, 3)]

    def qk_prep(q, k, v, gq, gk):
        qn = jnp.concatenate([h * _rstd(h) * gq for h in _heads(q)], axis=1)
        kn = jnp.concatenate([h * _rstd(h) * gk for h in _heads(k)], axis=1)
        return [qn, kn, v], []

    qn, kn, vb = _rowwise("qk_prep", qk_prep, [q_src, k_src, v_src], [q_norm_g, k_norm_g], [(aw, BF16)] * 3, [])

    pats = []
    for window, dil in DILATION_PATTERNS:
        assert window // dil == BAND and s % (dil * BAND) == 0
        bps = s // dil // BAND
        qp, kp, vp = [_to_streams(t, dil) for t in (qn, kn, vb)]
        o_p, l_p = _attn_fwd(f"attn_fwd_d{dil}", qp, kp, vp, bps, dil)
        pats.append((dil, bps, qp, kp, vp, _from_streams(o_p, dil), _from_streams(l_p, dil)))

    def combine(o1, o2, o3, l1, l2, l3):
        m = jnp.maximum(jnp.maximum(l1, l2), l3)
        tot = m + jnp.log(jnp.exp(l1 - m) + jnp.exp(l2 - m) + jnp.exp(l3 - m))
        wts3 = [jnp.exp(l - tot) for l in (l1, l2, l3)]
        mixed_heads = []
        for h, (a1, a2, a3) in enumerate(zip(_heads(o1), _heads(o2), _heads(o3))):
            w1, w2, w3 = [_from_lanes(w, h, HEAD_DIM) for w in wts3]
            mixed_heads.append(w1 * a1.astype(F32) + w2 * a2.astype(F32) + w3 * a3.astype(F32))
        return [jnp.concatenate(mixed_heads, axis=1), tot], []

    attn, lse = _rowwise("attn_mix", combine, [p[5] for p in pats] + [p[6] for p in pats], [], [(aw, F32), (HEAD_DIM, F32)], [])

    lam_re2, lam_im2, log_step2 = lam_re[0], lam_im[0], log_step[0].reshape(n_groups, 1)
    prep = _single("ssm_prep", _ssm_prep_fn, [lam_re2, lam_im2, log_step2], [((n_groups, STATE_DIM), F32)] * 18)
    p_r = jnp.stack([p.reshape(nst) for p in prep[:8]])
    p_i = jnp.stack([p.reshape(nst) for p in prep[8:16]])
    f_ri = jnp.stack(prep[16:])
    row = jnp.arange(SUBLANES)[:, None]
    zero = jnp.zeros((SUBLANES, nst), F32)
    sel = lambda cond, v: jnp.where(cond, jnp.broadcast_to(v[None, :], (SUBLANES, nst)), zero)
    fwd_consts = jnp.stack([sel(row >= 1, p_r[0]), sel(row >= 1, p_i[0]), sel(row >= 2, p_r[1]), sel(row >= 2, p_i[1]),
                            sel(row >= 4, p_r[3]), sel(row >= 4, p_i[3]), p_r, p_i])
    bwd_consts = jnp.stack([sel(row <= 6, p_r[0]), sel(row <= 6, p_i[0]), sel(row <= 5, p_r[1]), sel(row <= 5, p_i[1]),
                            sel(row <= 3, p_r[3]), sel(row <= 3, p_i[3]), p_r[::-1], p_i[::-1]])
    nblk = n_groups // GROUPS_PER_BLOCK
    f_rows = f_ri.reshape(2, nblk, 1, 512)
    braw_r = _block_diag(b_re[0].transpose(0, 2, 1), True)
    braw_i = _block_diag(b_im[0].transpose(0, 2, 1), True)

    def bbar_fn(br, bi, f):
        r, i = _cmul(f[0], f[1], br, bi)
        return [r, i]

    bbar_r, bbar_i = _single("ssm_bbar", bbar_fn, [braw_r, braw_i, f_rows], [(braw_r.shape, BF16)] * 2)
    ct_r, ct_i = _block_diag(c_re[0], True).astype(BF16), _block_diag(c_im[0], True).astype(BF16)
    ds2 = d_skip
    y_ssm, h_r, h_i = _ssm_fwd(u_src, bbar_r, bbar_i, ct_r.transpose(0, 2, 1), ct_i.transpose(0, 2, 1), fwd_consts, ds2)

    (yb,) = _rowwise("gelu", lambda v: ([_gelu(v)], []), [y_ssm], [], [(sw, BF16)], [])
    w_glu_s, w_out_s = _split_wait("gather_mix_wait", mix_stacks, mix_sems_s, mix_sems_r, yb, gather_plan)
    w_glu_f = w_glu_s.reshape(sw, sw)
    w_out_f = w_out_s.reshape(d, d)
    z = _mm("glu_fwd", yb, w_glu_f, "nn", F32)

    def glu(ys, zz, b):
        return [_gelu(ys) * _sigmoid(zz + b)], []

    (ssm,) = _rowwise("glu", glu, [y_ssm, z], [b_glu], [(sw, F32)], [])

    def cat_norm(a, sm, ga, gs):
        return [jnp.concatenate([a * _rstd(a) * ga, sm * _rstd(sm) * gs], axis=1)], []

    (catb,) = _rowwise("cat_norm", cat_norm, [attn, ssm], [attn_out_g, ssm_out_g], [(d, BF16)], [])
    mixed = _mm("mix_out", catb, w_out_f, "nn", F32)

    def resid_norm(xv, mx, g1v, g, sc, sh):
        x1v = xv + g1v * mx
        return [x1v, x1v * _rstd(x1v) * g * (1.0 + sc) + sh], []

    x1, h2b = _rowwise("resid_norm2", resid_norm, [x2, mixed], [g1, norm2_g, sc2, sh2], [(d, F32), (d, BF16)], [])
    w_ff1_s, w_ff2_s = _split_wait("gather_ff_wait", ff_stacks, ff_sems_s, ff_sems_r, x1, gather_plan)
    w_ff2_f = w_ff2_s.reshape(dff, d)
    a_ff = _mm("ff1", h2b, w_ff1_s, "nn", BF16, b_cs=True, tm=FF_TM)
    sq_relu = lambda v: jnp.square(jnp.maximum(v.astype(F32), 0.0)).astype(BF16)
    ff = _mm("ff2", a_ff, w_ff2_f, "nn", F32, a_pro=sq_relu, tm=FF_TM)

    def loss_fn(x1v, ffv, tg, g2v):
        diff = x1v + g2v * ffv - tg
        dout = diff * (1.0 / d)
        return [dout, dout * g2v], [_colsum8(0.5 * diff * dout), _colsum8(dout * ffv)]

    dout, dffb, loss_acc, dg2_acc = _rowwise("loss", loss_fn, [x1, ff, tgt], [g2], [(d, F32), (d, BF16)], [(SUBLANES, d)] * 2)

    da_ff = _mm("ff2_bwd", dffb, w_ff2_f, "nt", BF16, epi=lambda acc, av: acc * (2.0 * jnp.maximum(av.astype(F32), 0.0)), extras=[a_ff], tm=FF_TM)
    gw = {}
    gw["w_ff2"] = _mm("ff2_wgrad", a_ff, dffb, "tn", BF16, a_pro=sq_relu, tm=FF_TM).reshape(4, dff // 4, d)
    gw["w_ff1"] = _mm("ff1_wgrad", h2b, da_ff, "tn", BF16, o_cs=True, tm=FF_TM)

    def add2(n, g, b):
        _, half, cols = b.shape
        tr = _blk(half, 256)
        nrb = half // tr
        blk = (None, tr, cols)
        return _indexed("grad_add2_" + n, lambda p, q: p.astype(F32) + q.astype(F32), core_idx,
                        [(g, blk, lambda k, r, ix: (k, ix[0] * nrb + r, 0)), (b, blk, lambda k, r, ix: (k, r, 0))],
                        ((b.shape, BF16), blk, lambda k, r, ix: (k, r, 0)), (4, nrb))

    def scatter_plan(refs):
        px0, py0, pc, chips = _place()
        nw = len(refs) // 2
        return [(refs[w].at[2 * px + py], refs[nw + w].at[j], (px, py, pc), refs[nw + w].at[j])
                for w in range(nw) for j, (px, py) in enumerate(chips)]

    early = ["w_ff2", "w_ff1"]
    parts = {n: add2(n, gw[n], b) for n, b in zip(early, _swap_halves([gw[n] for n in early], "swap_halves_ff"))}
    lands = [lax.empty((3,) + parts[n].shape[1:], BF16) for n in early]
    sc_sems_s, sc_sems_r, sc_bufs, sc_token = _split_start("scatter_ff_start", [parts[n] for n in early] + lands, scatter_plan, 6)
    sc2_late = sc2 + sc_token[0:1, 0:1]

    dh2 = _mm("ff1_bwd", da_ff, w_ff1_s, "nt", F32, b_cs=True, tm=FF_TM)

    def norm2_bwd(dh, x1v, do, mx, g, sc, g1v):
        r = _rstd(x1v)
        xh = x1v * r
        dn = dh * (1.0 + sc)
        dx = do + _rms_bwd(dn * g, xh, r)
        return [dx, dx * g1v], [_colsum8(dh * xh * g), _colsum8(dh), _colsum8(dn * xh), _colsum8(dx * mx)]

    dx1, dmixb, dsc2_acc, dsh2_acc, dn2g_acc, dg1_acc = _rowwise(
        "norm2_bwd", norm2_bwd, [dh2, x1, dout, mixed], [norm2_g, sc2_late, g1], [(d, F32), (d, BF16)], [(SUBLANES, d)] * 4)

    dcat = _mm("mix_out_bwd", dmixb, w_out_f, "nt", F32)
    gw["w_out"] = _mm("mix_out_wgrad", catb, dmixb, "tn", BF16).reshape(4, d // 4, d)

    def cat_bwd(da, dsm, a, sm, ga, gs):
        ra, rs = _rstd(a), _rstd(sm)
        ah, sh = a * ra, sm * rs
        return [_rms_bwd(da * ga, ah, ra), _rms_bwd(dsm * gs, sh, rs)], [_colsum8(da * ah), _colsum8(dsm * sh)]

    dattn, dssm, dga_acc, dgs_acc = _rowwise(
        "cat_norm_bwd", cat_bwd, [(dcat, aw, 0), (dcat, sw, 1), attn, ssm], [attn_out_g, ssm_out_g], [(aw, F32), (sw, F32)],
        [(SUBLANES, aw), (SUBLANES, sw)])

    def glu_bwd(dsm, ys, zz, b):
        sg = _sigmoid(zz + b)
        dz = dsm * _gelu(ys) * sg * (1.0 - sg)
        return [dz, dsm * sg], [_colsum8(dz)]

    dzb, dy_direct, dbglu_acc = _rowwise("glu_bwd", glu_bwd, [dssm, y_ssm, z], [b_glu], [(sw, BF16), (sw, F32)], [(SUBLANES, sw)])
    gw["w_glu"] = _mm("glu_wgrad", yb, dzb, "tn", BF16).reshape(4, sw // 4, sw)
    dys = _mm("glu_bwd_mm", dzb, w_glu_f, "nt", F32, epi=lambda acc, direct, ys: (acc + direct) * _gelu_grad(ys), extras=[dy_direct, y_ssm])
    du, db_bar, dc_bd, da_acc, dds_acc = _ssm_bwd(dys, u_src, h_r, h_i, ct_r, ct_i, bbar_r.transpose(0, 2, 1),
                                                  bbar_i.transpose(0, 2, 1), bwd_consts, ds2)

    def delta_fn(da, a):
        return [da, _to_lanes([jnp.sum(p * q, axis=1, keepdims=True) for p, q in zip(_heads(da), _heads(a))])], []

    dob, delta = _rowwise("attn_delta", delta_fn, [dattn, attn], [], [(aw, BF16), (HEAD_DIM, F32)], [])
    dqs, dks, dvs = [], [], []
    for dil, bps, qp, kp, vp, _, _ in pats:
        dop, lsp, dlp = [_to_streams(t, dil) for t in (dob, lse, delta)]
        dq_p, dk_p, dv_p = _attn_bwd(f"attn_bwd_d{dil}", qp, kp, vp, dop, lsp, dlp, bps, dil)
        dqs.append(_from_streams(dq_p, dil))
        dks.append(_from_streams(dk_p, dil))
        dvs.append(_from_streams(dv_p, dil))

    def qk_bwd(dq1, dq2, dq3, dk1, dk2, dk3, dv1, dv2, dv3, duv, q, k, gq, gk):
        def one(dn, raw, g):
            outs, acc = [], 0.0
            for dh, h in zip(_heads(dn), _heads(raw)):
                r = _rstd(h)
                hh = h * r
                outs.append(_rms_bwd(dh * g, hh, r))
                acc = acc + _colsum8(dh * hh)
            return jnp.concatenate(outs, axis=1), acc

        tot = lambda a, b, c: (a.astype(F32) + b.astype(F32)) + c.astype(F32)
        dq, gq_acc = one(tot(dq1, dq2, dq3), q, gq)
        dk, gk_acc = one(tot(dk1, dk2, dk3), k, gk)
        return [jnp.concatenate([dq, dk, tot(dv1, dv2, dv3), duv], axis=1)], [gq_acc, gk_acc]

    dprojb, dgq_acc, dgk_acc = _rowwise("qk_norm_bwd", qk_bwd, dqs + dks + dvs + [du, q_src, k_src], [q_norm_g, k_norm_g],
                                        [(3 * aw + sw, BF16)], [(SUBLANES, HEAD_DIM)] * 2, tm=128)
    gw["w_in"] = _mm("proj_in_wgrad", h1b, dprojb, "tn", BF16, o_cs=True)
    dh1 = _mm("proj_in_bwd", dprojb, w_in_s, "nt", F32, b_cs=True)

    def norm1_bwd(dh, xv, dxr, g, sc):
        r = _rstd(xv)
        xh = xv * r
        dn = dh * (1.0 + sc)
        return [dxr + _rms_bwd(dn * g, xh, r)], [_colsum8(dh * xh * g), _colsum8(dh), _colsum8(dn * xh)]

    grad_x, dsc1_acc, dsh1_acc, dn1g_acc = _rowwise("norm1_bwd", norm1_bwd, [dh1, x2, dx1], [norm1_g, sc1], [(d, F32)], [(SUBLANES, d)] * 3)

    def b_grad_fn(dbb, br, bi, f):
        fr, fi = f[0], f[1]
        dbr, dbi = _cmul_conj(fr, fi, dbb[0], dbb[1])
        dfr = jnp.sum(dbb[0] * br + dbb[1] * bi, axis=1, keepdims=True)
        dfi = jnp.sum(dbb[1] * br - dbb[0] * bi, axis=1, keepdims=True)
        return [dbr, dbi, dfr, dfi]

    dbr_bd, dbi_bd, df_r, df_i = _single("ssm_b_grad", b_grad_fn, [db_bar, braw_r, braw_i, f_rows],
                                         [(braw_r.shape, F32)] * 2 + [((nblk, 1, 512), F32)] * 2)

    def sums_fn(*accs):
        return [jnp.sum(a, axis=-2, keepdims=True) for a in accs]

    acc_list = [dsh1_acc, dsc1_acc, dg1_acc, dsh2_acc, dsc2_acc, dg2_acc, dn1g_acc, dgq_acc, dgk_acc, dds_acc, dbglu_acc, dga_acc,
                dgs_acc, dn2g_acc, loss_acc, da_acc]
    sums = _single("small_sums", sums_fn, acc_list, [(a.shape[:-2] + (1, a.shape[-1]), F32) for a in acc_list])
    (dsh1, dsc1, dg1, dsh2, dsc2, dg2, dn1g, dgq, dgk, dds, dbglu, dga, dgs, dn2g, loss_row, da_sum) = sums
    loss_dev = _single("loss_sum", lambda v: [jnp.sum(v, axis=1, keepdims=True)], [loss_row], [((1, 1), F32)])[0]
    gp = (n_groups, STATE_DIM)
    dlam_re, dlam_im, dlog_step = _single(
        "ssm_param_grad", _ssm_param_grad_fn,
        [lam_re2, lam_im2, log_step2, da_sum[0].reshape(gp), da_sum[1].reshape(gp), df_r.reshape(gp), df_i.reshape(gp)],
        [(gp, F32), (gp, F32), ((n_groups, 1), F32)])
    small_dev = {
        "b_ada": jnp.concatenate([dsh1, dsc1, dg1, dsh2, dsc2, dg2], axis=1),
        "norm1_g": dn1g, "q_norm_g": dgq, "k_norm_g": dgk, "lam_re": dlam_re, "lam_im": dlam_im, "log_step": dlog_step,
        "b_re": _block_diag_take(dbr_bd, SSM_GROUP, STATE_DIM).transpose(0, 2, 1),
        "b_im": _block_diag_take(dbi_bd, SSM_GROUP, STATE_DIM).transpose(0, 2, 1),
        "c_re": _block_diag_take(dc_bd[0], SSM_GROUP, STATE_DIM), "c_im": _block_diag_take(dc_bd[1], SSM_GROUP, STATE_DIM),
        "d_skip": dds, "b_glu": dbglu, "attn_out_g": dga, "ssm_out_g": dgs, "norm2_g": dn2g,
    }

    packed = _pack([small_dev[n] for n in SMALL])
    rows = packed.shape[0]
    gathered = _allgather8("gather_small", packed).reshape(8, rows, PACK_LANES)

    def sum8_fn(g):
        acc = g[0]
        for i in range(1, 8):
            acc = acc + g[i]
        return [acc]

    g_small = _single("sum_small", sum8_fn, [gathered], [((rows, PACK_LANES), F32)])[0]
    w_small, m_small, v_small = [_pack([src[n] for n in SMALL]) for src in (wts, mom, var)]
    upd_small = _adamw_rows("adamw_small", w_small, g_small, m_small, v_small)
    shapes = [wts[n].shape for n in SMALL]
    out = {}
    for kind, arr in zip(("grad", "delta", "new_m", "new_v"), (g_small,) + tuple(upd_small)):
        for n, val in zip(SMALL, _unpack(arr, shapes)):
            out[kind, n] = val

    dmod_all = gathered[:, 0 : -(-6 * d // PACK_LANES), :].reshape(8, -1)[:, : 6 * d]
    dmod_mine = lax.dynamic_slice(dmod_all, (0, chip * n_mod), (8, n_mod))
    g_ada = _mm("ada_wgrad", sil_c, dmod_mine, "tn", F32, tm=512, tn=512)
    upd = _adamw_rows("adamw_w_ada", w_ada[0], g_ada, m_w_ada[0], v_w_ada[0])
    for kind, val in zip(("grad", "delta", "new_m", "new_v"), (g_ada,) + tuple(upd)):
        out[kind, "w_ada"] = val[None]

    late = [n for n in LARGE if n not in early]
    parts.update({n: add2(n, gw[n], b) for n, b in zip(late, _swap_halves([gw[n] for n in late], "swap_halves_rest"))})
    got2 = dict(zip(late, _scatter_chips([parts[n] for n in late])))
    sc_bufs = _split_wait("scatter_ff_wait", sc_bufs, sc_sems_s, sc_sems_r, grad_x, scatter_plan)
    parts.update(zip(early, sc_bufs[: len(early)]))
    got2.update(zip(early, sc_bufs[len(early) :]))

    def add4(n, p, b):
        _, half, cols = p.shape
        tr = _blk(half, 256)
        blk = (None, tr, cols)
        fn = lambda own, q0, q1, q2: ((own.astype(F32) + q0.astype(F32)) + q1.astype(F32)) + q2.astype(F32)
        srcs = [(p, blk, lambda r, ix: (ix[0], r, 0))] + [(b, blk, lambda r, ix, j=j: (j, r, 0)) for j in range(3)]
        return _indexed("grad_add4_" + n, fn, chip_idx, srcs, (((half, cols), F32), (tr, cols), lambda r, ix: (r, 0)), (half // tr,))

    finals = [add4(n, parts[n], got2[n]) for n in LARGE]
    for n, full in zip(LARGE, _join_halves(finals)):
        g = full.reshape(wts[n].shape[1:])
        upd = _adamw_rows("adamw_" + n, wts[n][0], g, mom[n][0], var[n][0])
        for kind, val in zip(("grad", "delta", "new_m", "new_v"), (g,) + tuple(upd)):
            out[kind, n] = val[None]

    loss = lax.psum(loss_dev[0, 0], ("x", "y", "c"))
    return (loss, grad_x[None], *[out[kind, n] for kind in ("grad", "delta", "new_m", "new_v") for n in ORDER])
```

```python
import functools
import math

import jax
import jax.numpy as jnp
import numpy as np
from jax import lax
from jax.experimental import pallas as pl
from jax.experimental.pallas import tpu as pltpu

F32 = jnp.float32
BF16 = jnp.bfloat16
MESH = pl.DeviceIdType.MESH

EPS = 1e-6
HEAD_DIM = 128
SSM_GROUP = 16
STATE_DIM = 64
GROUPS_PER_BLOCK = 8
DILATION_PATTERNS = ((128, 1), (512, 4), (2048, 16))
BAND = 128
NEG = -1e30

ADAM_LR, ADAM_B1, ADAM_B2, ADAM_EPS, ADAM_WD, ADAM_STEP = 0.001, 0.9, 0.999, 1e-08, 0.01, 10

V7X_VMEM_BYTES = 64 * 2**20
VMEM_LIMIT = 48 * 2**20
JOIN_VMEM_LIMIT = 56 * 2**20
SUBLANES = 8
PACK_LANES = 1024


def _cparams(sem=None):
    return pltpu.CompilerParams(dimension_semantics=sem, vmem_limit_bytes=VMEM_LIMIT)


def _blk(n, want):
    b = min(n, want)
    while n % b:
        b //= 2
    return b


def _colsum8(v):
    tm, w = v.shape
    return v.reshape(tm // SUBLANES, SUBLANES, w).sum(axis=0)


def _rowwise(name, fn, rows, vecs, out_rows, out_accs, tm=256, n_rows=None):
    rows = [r if isinstance(r, tuple) else (r, r.shape[1], 0) for r in rows]
    rows = [r if len(r) == 4 else r + (0,) for r in rows]
    s = rows[0][0].shape[0] if n_rows is None else n_rows
    tm = _blk(s, tm)
    assert all(r[3] % tm == 0 for r in rows)
    n_in, n_or = len(rows) + len(vecs), len(out_rows)

    def body(*refs):
        outs, accs = fn(*[r[...] for r in refs[:n_in]])
        for r, v in zip(refs[n_in : n_in + n_or], outs):
            r[...] = v.astype(r.dtype)
        if out_accs:
            acc_refs = refs[n_in + n_or :]

            @pl.when(pl.program_id(0) == 0)
            def _():
                for r in acc_refs:
                    r[...] = jnp.zeros(r.shape, r.dtype)

            for r, v in zip(acc_refs, accs):
                r[...] += v

    in_specs = [pl.BlockSpec((tm, w), lambda i, cb=cb, rb=r0 // tm: (i + rb, cb)) for (_, w, cb, r0) in rows]
    in_specs += [pl.BlockSpec(v.shape, lambda i, nd=v.ndim: (0,) * nd) for v in vecs]
    out_specs = [pl.BlockSpec((tm, w), lambda i: (i, 0)) for (w, _) in out_rows]
    out_specs += [pl.BlockSpec(sh, lambda i, nd=len(sh): (0,) * nd) for sh in out_accs]
    out_shape = [jax.ShapeDtypeStruct((s, w), dt) for (w, dt) in out_rows]
    out_shape += [jax.ShapeDtypeStruct(sh, F32) for sh in out_accs]
    res = pl.pallas_call(
        body, name=name, grid=(s // tm,), in_specs=in_specs, out_specs=out_specs, out_shape=out_shape,
        compiler_params=_cparams(("arbitrary",)),
    )(*[r[0] for r in rows], *vecs)
    return res


def _single(name, fn, ins, out_shapes):
    n_in = len(ins)

    def body(*refs):
        outs = fn(*[r[...] for r in refs[:n_in]])
        for r, v in zip(refs[n_in:], outs):
            r[...] = v.astype(r.dtype)

    vm = pl.BlockSpec(memory_space=pltpu.VMEM)
    return pl.pallas_call(
        body, name=name, in_specs=[vm] * n_in, out_specs=[vm] * len(out_shapes),
        out_shape=[jax.ShapeDtypeStruct(sh, dt) for sh, dt in out_shapes], compiler_params=_cparams(),
    )(*ins)


_NN = (((1,), (0,)), ((), ()))
_NT = (((1,), (1,)), ((), ()))
_TN = (((0,), (0,)), ((), ()))


def _mm(name, a, b, mode, out_dtype, *, tm=512, tn=1024, tk=2048, b_cs=False, o_cs=False, epi=None, extras=(), a_pro=None):
    if mode == "tn":
        k, m = a.shape
        n = b.shape[1]
    else:
        m, k = a.shape
        if mode == "nn":
            n = b.shape[2] * 4 if b_cs else b.shape[1]
        else:
            n = b.shape[1] if b_cs else b.shape[0]
    tm = _blk(m, tm)
    tn = _blk(n // 4 if (b_cs and mode == "nn") or o_cs else n, tn)
    tk = _blk(k // 4 if (b_cs and mode == "nt") else k, tk)
    nk = k // tk
    dims = {"nn": _NN, "nt": _NT, "tn": _TN}[mode]
    n_ex = len(extras)

    def body(*refs):
        a_ref, b_ref = refs[0], refs[1]
        ex_refs, o_ref = refs[2 : 2 + n_ex], refs[2 + n_ex]

        def finish(acc):
            if epi is not None:
                acc = epi(acc, *[r[...] for r in ex_refs])
            o_ref[...] = acc.astype(o_ref.dtype)

        a_blk = a_ref[...] if a_pro is None else a_pro(a_ref[...])
        part = lax.dot_general(a_blk, b_ref[...], dims, preferred_element_type=F32)
        if nk == 1:
            finish(part)
        else:
            acc_ref = refs[3 + n_ex]
            kk = pl.program_id(2)

            @pl.when(kk == 0)
            def _():
                acc_ref[...] = part

            @pl.when(kk > 0)
            def _():
                acc_ref[...] += part

            @pl.when(kk == nk - 1)
            def _():
                finish(acc_ref[...])

    a_spec = pl.BlockSpec((tk, tm), lambda i, j, kk: (kk, i)) if mode == "tn" else pl.BlockSpec((tm, tk), lambda i, j, kk: (i, kk))
    if mode == "nn":
        if b_cs:
            per = (n // 4) // tn
            b_spec = pl.BlockSpec((None, tk, tn), lambda i, j, kk: (j // per, kk, j % per))
        else:
            b_spec = pl.BlockSpec((tk, tn), lambda i, j, kk: (kk, j))
    elif mode == "nt":
        if b_cs:
            per = (k // 4) // tk
            b_spec = pl.BlockSpec((None, tn, tk), lambda i, j, kk: (kk // per, j, kk % per))
        else:
            b_spec = pl.BlockSpec((tn, tk), lambda i, j, kk: (j, kk))
    else:
        b_spec = pl.BlockSpec((tk, tn), lambda i, j, kk: (kk, j))
    if o_cs:
        per = (n // 4) // tn
        o_spec = pl.BlockSpec((None, tm, tn), lambda i, j, kk: (j // per, i, j % per))
        o_shape = jax.ShapeDtypeStruct((4, m, n // 4), out_dtype)
    else:
        o_spec = pl.BlockSpec((tm, tn), lambda i, j, kk: (i, j))
        o_shape = jax.ShapeDtypeStruct((m, n), out_dtype)
    ex_specs = [pl.BlockSpec((tm, tn), lambda i, j, kk: (i, j)) for _ in extras]
    return pl.pallas_call(
        body, name=name, grid=(m // tm, n // tn, nk), in_specs=[a_spec, b_spec] + ex_specs, out_specs=o_spec, out_shape=o_shape,
        scratch_shapes=[pltpu.VMEM((tm, tn), F32)] if nk > 1 else [],
        compiler_params=_cparams(("parallel", "parallel", "arbitrary")),
    )(a, b, *extras)


def _place():
    x, y, c = lax.axis_index("x"), lax.axis_index("y"), lax.axis_index("c")
    return x, y, c, [(1 - x, y), (x, 1 - y), (1 - x, 1 - y)]


def _allgather8(name, v):
    m_per, n = v.shape

    def body(x_ref, out_ref, send_sems, recv_sems, local_sem):
        x, y, c, chips = _place()
        me, sibling = (x, y, c), (x, y, 1 - c)

        def rows(px, py, pc):
            return out_ref.at[pl.ds((4 * px + 2 * py + pc) * m_per, m_per), :]

        def copy(k, block, to, src=None):
            return pltpu.make_async_remote_copy(
                src_ref=rows(*block) if src is None else src, dst_ref=rows(*block), send_sem=send_sems.at[k],
                recv_sem=recv_sems.at[k], device_id=to, device_id_type=MESH)

        mine = pltpu.make_async_copy(x_ref, rows(*me), local_sem)
        mine.start()
        first = [copy(0, me, sibling, src=x_ref)]
        first += [copy(1 + j, me, (*chip, c), src=x_ref) for j, chip in enumerate(chips)]
        for cp in first:
            cp.start()
        passed = [copy(4 + j, (*chip, c), sibling) for j, chip in enumerate(chips)]
        for j, chip in enumerate(chips):
            copy(1 + j, (*chip, c), me).wait_recv()
            passed[j].start()
        copy(0, sibling, me).wait_recv()
        for j, chip in enumerate(chips):
            copy(4 + j, (*chip, 1 - c), me).wait_recv()
        for cp in first + passed:
            cp.wait_send()
        mine.wait()

    return pl.pallas_call(
        body, name=name, out_shape=jax.ShapeDtypeStruct((8 * m_per, n), v.dtype),
        in_specs=[pl.BlockSpec(memory_space=pltpu.VMEM)], out_specs=pl.BlockSpec(memory_space=pltpu.VMEM),
        scratch_shapes=[pltpu.SemaphoreType.DMA((7,)), pltpu.SemaphoreType.DMA((7,)), pltpu.SemaphoreType.DMA],
        compiler_params=_cparams(),
    )(v)


def _gather_weights(stacks, after):
    nw = len(stacks)

    def body(*refs):
        outs = refs[nw + 1 : 2 * nw + 1]
        send_sems, recv_sems = refs[2 * nw + 1 :]
        x, y, c, chips = _place()
        me, sibling = (x, y, c), (x, y, 1 - c)

        def copy(w, k, block, to):
            px, py, pc = block
            half = outs[w].shape[1] // 2
            rows = outs[w].at[2 * px + py, pl.ds(pc * half, half), :]
            return pltpu.make_async_remote_copy(src_ref=rows, dst_ref=rows, send_sem=send_sems.at[6 * w + k],
                                                recv_sem=recv_sems.at[6 * w + k], device_id=to, device_id_type=MESH)

        sent = [copy(w, j, me, (*chip, c)) for w in range(nw) for j, chip in enumerate(chips)]
        for cp in sent:
            cp.start()
        for w in range(nw):
            for j, chip in enumerate(chips):
                copy(w, j, (*chip, c), me).wait_recv()
                passed = copy(w, 3 + j, (*chip, c), sibling)
                passed.start()
                sent.append(passed)
        for w in range(nw):
            for j, chip in enumerate(chips):
                copy(w, 3 + j, (*chip, 1 - c), me).wait_recv()
        for cp in sent:
            cp.wait_send()

    hbm = pl.BlockSpec(memory_space=pl.ANY)
    return pl.pallas_call(
        body, name="gather_weights", out_shape=[jax.ShapeDtypeStruct(w.shape, w.dtype) for w in stacks],
        in_specs=[hbm] * (nw + 1), out_specs=[hbm] * nw, input_output_aliases={i: i for i in range(nw)},
        scratch_shapes=[pltpu.SemaphoreType.DMA((6 * nw,)), pltpu.SemaphoreType.DMA((6 * nw,))],
        compiler_params=_cparams(),
    )(*stacks, after)


def _swap_halves(gs, name):
    nw = len(gs)

    def body(*refs):
        ins, gots = refs[:nw], refs[nw : 2 * nw]
        send_sems, recv_sems = refs[2 * nw :]
        x, y, c, _ = _place()
        cps = []
        for w in range(nw):
            half = ins[w].shape[1] // 2
            cps.append(pltpu.make_async_remote_copy(
                src_ref=ins[w].at[:, pl.ds((1 - c) * half, half), :], dst_ref=gots[w], send_sem=send_sems.at[w],
                recv_sem=recv_sems.at[w], device_id=(x, y, 1 - c), device_id_type=MESH))
            cps[-1].start()
        for cp in cps:
            cp.wait()

    hbm = pl.BlockSpec(memory_space=pl.ANY)
    return pl.pallas_call(
        body, name=name, out_shape=[jax.ShapeDtypeStruct((4, g.shape[1] // 2, g.shape[2]), g.dtype) for g in gs],
        in_specs=[hbm] * nw, out_specs=[hbm] * nw,
        scratch_shapes=[pltpu.SemaphoreType.DMA((nw,)), pltpu.SemaphoreType.DMA((nw,))], compiler_params=_cparams(),
    )(*gs)


def _split_start(name, bufs, plan, n, extra=()):
    n_plan = len(bufs)
    bufs = list(bufs) + list(extra)
    nb = len(bufs)

    def body(*refs):
        send_sems, recv_sems, token = refs[nb], refs[nb + 1], refs[-1]
        for k, (src, dst, to, _) in enumerate(plan(refs[:n_plan])):
            pltpu.make_async_remote_copy(src_ref=src, dst_ref=dst, send_sem=send_sems.at[k], recv_sem=recv_sems.at[k],
                                         device_id=to, device_id_type=MESH).start()
        token[...] = jnp.zeros(token.shape, token.dtype)

    hbm, sem = pl.BlockSpec(memory_space=pltpu.HBM), pl.BlockSpec(memory_space=pltpu.SEMAPHORE)
    res = pl.pallas_call(
        body, name=name,
        out_shape=(pltpu.SemaphoreType.DMA((n,)), pltpu.SemaphoreType.DMA((n,)), *[pltpu.HBM(b.shape, b.dtype) for b in bufs],
                   jax.ShapeDtypeStruct((SUBLANES, 128), F32)),
        in_specs=(hbm,) * nb, out_specs=(sem, sem) + (hbm,) * nb + (pl.BlockSpec(memory_space=pltpu.VMEM),),
        input_output_aliases={i: 2 + i for i in range(nb)},
        compiler_params=pltpu.CompilerParams(has_side_effects=pltpu.SideEffectType.DATAFLOW_SIDE_EFFECTING),
    )(*[pltpu.with_memory_space_constraint(b, pltpu.HBM) for b in bufs])
    return res[0], res[1], list(res[2 : 2 + nb]), res[-1]


def _split_wait(name, bufs, send_sems, recv_sems, after, plan):
    nb = len(bufs)

    def body(*refs):
        send, recv = refs[nb], refs[nb + 1]
        for k, (src, _, to, land) in enumerate(plan(refs[:nb])):
            cp = pltpu.make_async_remote_copy(src_ref=src, dst_ref=land, send_sem=send.at[k], recv_sem=recv.at[k],
                                              device_id=to, device_id_type=MESH)
            cp.wait_send()
            cp.wait_recv()

    hbm, sem = pl.BlockSpec(memory_space=pltpu.HBM), pl.BlockSpec(memory_space=pltpu.SEMAPHORE)
    return pl.pallas_call(
        body, name=name, out_shape=tuple(pltpu.HBM(b.shape, b.dtype) for b in bufs),
        in_specs=(hbm,) * nb + (sem, sem, pl.BlockSpec(memory_space=pl.ANY)), out_specs=(hbm,) * nb,
        input_output_aliases={i: i for i in range(nb)},
        compiler_params=pltpu.CompilerParams(has_side_effects=pltpu.SideEffectType.DATAFLOW_SIDE_EFFECTING),
    )(*bufs, send_sems, recv_sems, after)


def _indexed(name, fn, idx, ins, out, grid):
    def body(idx_ref, *refs):
        refs[-1][...] = fn(*[r[...] for r in refs[:-1]]).astype(refs[-1].dtype)

    (o_shape, o_dtype), o_block, o_map = out
    return pl.pallas_call(
        body, name=name, out_shape=jax.ShapeDtypeStruct(o_shape, o_dtype),
        grid_spec=pltpu.PrefetchScalarGridSpec(
            num_scalar_prefetch=1, grid=grid, in_specs=[pl.BlockSpec(blk, mp) for (_, blk, mp) in ins],
            out_specs=pl.BlockSpec(o_block, o_map)),
        compiler_params=_cparams(("arbitrary",) * len(grid)),
    )(idx, *[a for (a, _, _) in ins])


def _join_halves(fs):
    nw = len(fs)

    def body(*refs):
        ins, outs, lands = refs[:nw], refs[nw : 2 * nw], refs[2 * nw : 3 * nw]
        send_sems, recv_sems, local_sems = refs[3 * nw :]
        x, y, c, _ = _place()
        sibling = (x, y, 1 - c)
        cps = [pltpu.make_async_remote_copy(src_ref=ins[w], dst_ref=lands[w], send_sem=send_sems.at[w], recv_sem=recv_sems.at[w],
                                            device_id=sibling, device_id_type=MESH) for w in range(nw)]
        outs_own = [pltpu.make_async_copy(ins[w], outs[w].at[c], local_sems.at[w]) for w in range(nw)]
        outs_got = [pltpu.make_async_copy(lands[w], outs[w].at[1 - c], local_sems.at[nw + w]) for w in range(nw)]
        for cp in cps + outs_own:
            cp.start()
        for w in range(nw):
            cps[w].wait_recv()
            outs_got[w].start()
        for w in range(nw):
            cps[w].wait_send()
            outs_own[w].wait()
            outs_got[w].wait()

    hbm, vm = pl.BlockSpec(memory_space=pl.ANY), pl.BlockSpec(memory_space=pltpu.VMEM)
    return pl.pallas_call(
        body, name="join_halves", out_shape=[jax.ShapeDtypeStruct((2,) + f.shape, f.dtype) for f in fs],
        in_specs=[vm] * nw, out_specs=[hbm] * nw,
        scratch_shapes=[pltpu.VMEM(f.shape, f.dtype) for f in fs]
        + [pltpu.SemaphoreType.DMA((nw,)), pltpu.SemaphoreType.DMA((nw,)), pltpu.SemaphoreType.DMA((2 * nw,))],
        compiler_params=pltpu.CompilerParams(vmem_limit_bytes=JOIN_VMEM_LIMIT),
    )(*fs)


def _rstd(v):
    return lax.rsqrt(jnp.mean(v * v, axis=-1, keepdims=True) + EPS)


def _rms_bwd(dy, xh, r):
    return r * (dy - xh * jnp.mean(dy * xh, axis=-1, keepdims=True))


def _heads(v):
    return [v[:, h * HEAD_DIM : (h + 1) * HEAD_DIM] for h in range(v.shape[1] // HEAD_DIM)]


def _gelu(v):
    k = math.sqrt(2.0 / math.pi)
    return 0.5 * v * (1.0 + jnp.tanh(k * (v + 0.044715 * v * v * v)))


def _gelu_grad(v):
    k = math.sqrt(2.0 / math.pi)
    t = jnp.tanh(k * (v + 0.044715 * v * v * v))
    return 0.5 * (1.0 + t) + 0.5 * v * (1.0 - t * t) * k * (1.0 + 3 * 0.044715 * v * v)


def _sigmoid(v):
    return 1.0 / (1.0 + jnp.exp(-v))


def _slopes(n_heads):
    return [2.0 ** (-8.0 * (h + 1.0) / n_heads) for h in range(n_heads)]


def _band_iotas():
    ri = lax.broadcasted_iota(jnp.int32, (BAND, BAND), 0)
    ci = lax.broadcasted_iota(jnp.int32, (BAND, BAND), 1)
    return ri, ci


def _to_lanes(cols):
    rows = cols[0].shape[0]
    lane = lax.broadcasted_iota(jnp.int32, (rows, HEAD_DIM), 1)
    out = jnp.zeros((rows, HEAD_DIM), F32)
    for h, col in enumerate(cols):
        out = jnp.where(lane == h, col, out)
    return out


def _from_lanes(v, h, width):
    return jnp.broadcast_to(v[:, h : h + 1], (v.shape[0], width))


def _attn_fwd(name, q, k, v, bps, dil):
    s, aw = q.shape
    n_heads, nb = aw // HEAD_DIM, s // BAND
    scale = HEAD_DIM**-0.5
    slopes = _slopes(n_heads)

    def body(q_ref, kp_ref, kc_ref, vp_ref, vc_ref, o_ref, l_ref):
        b = pl.program_id(0)
        first = (b % bps) == 0
        ri, ci = _band_iotas()
        dist_c = (ri - ci).astype(F32)
        dist_p = (ri - ci + BAND).astype(F32)
        valid_c = ci <= ri
        valid_p = ci >= ri + jnp.where(first, BAND, 0)
        hs = range(n_heads)
        sls = [slice(h * HEAD_DIM, (h + 1) * HEAD_DIM) for h in hs]
        qs = [q_ref[:, sl] for sl in sls]
        sc = [lax.dot_general(qs[h], kc_ref[:, sls[h]], _NT, preferred_element_type=F32) for h in hs]
        sp = [lax.dot_general(qs[h], kp_ref[:, sls[h]], _NT, preferred_element_type=F32) for h in hs]
        sc = [jnp.where(valid_c, sc[h] * scale - (slopes[h] * dil) * dist_c, NEG) for h in hs]
        sp = [jnp.where(valid_p, sp[h] * scale - (slopes[h] * dil) * dist_p, NEG) for h in hs]
        m = [jnp.maximum(jnp.max(sc[h], axis=1, keepdims=True), jnp.max(sp[h], axis=1, keepdims=True)) for h in hs]
        pc = [jnp.exp(sc[h] - m[h]) for h in hs]
        pp = [jnp.exp(sp[h] - m[h]) for h in hs]
        den = [jnp.sum(pc[h], axis=1, keepdims=True) + jnp.sum(pp[h], axis=1, keepdims=True) for h in hs]
        acc = [jnp.dot(pc[h].astype(BF16), vc_ref[:, sls[h]], preferred_element_type=F32) for h in hs]
        acc = [acc[h] + jnp.dot(pp[h].astype(BF16), vp_ref[:, sls[h]], preferred_element_type=F32) for h in hs]
        o_ref[...] = jnp.concatenate([acc[h] / den[h] for h in hs], axis=1).astype(o_ref.dtype)
        l_ref[...] = _to_lanes([m[h] + jnp.log(den[h]) for h in hs])

    cur = pl.BlockSpec((BAND, aw), lambda b: (b, 0))
    prev = pl.BlockSpec((BAND, aw), lambda b: (jnp.maximum(b - 1, 0), 0))
    return pl.pallas_call(
        body, name=name, grid=(nb,), in_specs=[cur, prev, cur, prev, cur], out_specs=[cur, pl.BlockSpec((BAND, HEAD_DIM), lambda b: (b, 0))],
        out_shape=[jax.ShapeDtypeStruct((s, aw), BF16), jax.ShapeDtypeStruct((s, HEAD_DIM), F32)],
        compiler_params=_cparams(("arbitrary",)),
    )(q, k, k, v, v)


def _attn_bwd(name, q, k, v, do, lse, delta, bps, dil):
    s, aw = q.shape
    n_heads, nb = aw // HEAD_DIM, s // BAND
    scale = HEAD_DIM**-0.5
    slopes = _slopes(n_heads)

    def body(qc_ref, qn_ref, doc_ref, don_ref, lc_ref, ln_ref, dc_ref, dn_ref, k_ref, v_ref, dq_ref, dk_ref, dv_ref, carry):
        j = pl.program_id(0)

        @pl.when(j == 0)
        def _():
            carry[...] = jnp.zeros(carry.shape, F32)

        has_next = jnp.logical_and((j + 1) % bps != 0, j + 1 < nb)
        ri, ci = _band_iotas()
        dist_c = (ri - ci).astype(F32)
        dist_p = (ri - ci + BAND).astype(F32)
        valid_c = ci <= ri
        valid_n = ci >= ri + jnp.where(has_next, 0, BAND)
        carried = carry[...]
        hs = range(n_heads)
        sls = [slice(h * HEAD_DIM, (h + 1) * HEAD_DIM) for h in hs]
        ks, vs = [k_ref[:, sl] for sl in sls], [v_ref[:, sl] for sl in sls]
        sides = [(qc_ref, doc_ref, lc_ref, dc_ref, dist_c, valid_c), (qn_ref, don_ref, ln_ref, dn_ref, dist_p, valid_n)]
        res = []
        for q_ref, do_ref, l_ref, d_ref, dist, valid in sides:
            qs, dos = [q_ref[:, sl] for sl in sls], [do_ref[:, sl] for sl in sls]
            sc = [lax.dot_general(qs[h], ks[h], _NT, preferred_element_type=F32) for h in hs]
            dp = [lax.dot_general(dos[h], vs[h], _NT, preferred_element_type=F32) for h in hs]
            lse_v, delta_v = l_ref[...], d_ref[...]
            p = [jnp.where(valid, jnp.exp(sc[h] * scale - (slopes[h] * dil) * dist - _from_lanes(lse_v, h, BAND)), 0.0) for h in hs]
            ds = [(p[h] * (dp[h] - _from_lanes(delta_v, h, BAND))).astype(BF16) for h in hs]
            pb = [p[h].astype(BF16) for h in hs]
            dv = [lax.dot_general(pb[h], dos[h], _TN, preferred_element_type=F32) for h in hs]
            dk = [lax.dot_general(ds[h], qs[h], _TN, preferred_element_type=F32) for h in hs]
            dq = [jnp.dot(ds[h], ks[h], preferred_element_type=F32) for h in hs]
            res.append((dq, dk, dv))
        (dq_c, dk_c, dv_c), (dq_n, dk_n, dv_n) = res
        dq_ref[...] = (carried + jnp.concatenate(dq_c, axis=1) * scale).astype(dq_ref.dtype)
        carry[...] = jnp.concatenate(dq_n, axis=1) * scale
        dk_ref[...] = (jnp.concatenate([dk_c[h] + dk_n[h] for h in hs], axis=1) * scale).astype(dk_ref.dtype)
        dv_ref[...] = jnp.concatenate([dv_c[h] + dv_n[h] for h in hs], axis=1).astype(dv_ref.dtype)

    cur = pl.BlockSpec((BAND, aw), lambda b: (b, 0))
    nxt = pl.BlockSpec((BAND, aw), lambda b: (jnp.minimum(b + 1, nb - 1), 0))
    cur1 = pl.BlockSpec((BAND, HEAD_DIM), lambda b: (b, 0))
    nxt1 = pl.BlockSpec((BAND, HEAD_DIM), lambda b: (jnp.minimum(b + 1, nb - 1), 0))
    return pl.pallas_call(
        body, name=name, grid=(nb,), in_specs=[cur, nxt, cur, nxt, cur1, nxt1, cur1, nxt1, cur, cur], out_specs=[cur] * 3,
        out_shape=[jax.ShapeDtypeStruct((s, aw), BF16)] * 3, scratch_shapes=[pltpu.VMEM((BAND, aw), F32)],
        compiler_params=_cparams(("arbitrary",)),
    )(q, q, do, do, lse, lse, delta, delta, k, v)


def _to_streams(v, dil):
    if dil == 1:
        return v
    s, w = v.shape
    return v.reshape(s // dil, dil, w).transpose(1, 0, 2).reshape(s, w)


def _from_streams(v, dil):
    if dil == 1:
        return v
    s, w = v.shape
    return v.reshape(dil, s // dil, w).transpose(1, 0, 2).reshape(s, w)


SCAN_LANES = 1024
SSM_ROWS = 128
SSM_FWD_ROWS = 256
FF_TM = 1024


def _cmul(ar, ai, br, bi):
    return ar * br - ai * bi, ar * bi + ai * br


def _cmul_conj(ar, ai, br, bi):
    return ar * br + ai * bi, ar * bi - ai * br


def _ssm_fwd(u_src, bre, bim, cre, cim, consts, d_skip):
    u_arr, sw, ucb = u_src
    s = u_arr.shape[0]
    nblk = bre.shape[0]
    nst = nblk * 512
    t = _blk(s, SSM_FWD_ROWS)
    w = _blk(nst, SCAN_LANES)

    def body(u_ref, bre_ref, bim_ref, cre_ref, cim_ref, k_ref, ds_ref, y_ref, hr_ref, hi_ref, xr, xi, car_r, car_i):
        @pl.when(pl.program_id(0) == 0)
        def _():
            car_r[...] = jnp.zeros(car_r.shape, F32)
            car_i[...] = jnp.zeros(car_i.shape, F32)

        u = u_ref[...]
        ub = u.astype(BF16)
        for b in range(nblk):
            ch, st = slice(b * 128, (b + 1) * 128), slice(b * 512, (b + 1) * 512)
            xr[:, st] = jnp.dot(ub[:, ch], bre_ref[b], preferred_element_type=F32)
            xi[:, st] = jnp.dot(ub[:, ch], bim_ref[b], preferred_element_type=F32)
        for c in range(nst // w):
            ls = slice(c * w, (c + 1) * w)
            mult = [k_ref[i, :, ls] for i in range(8)]

            def step(i, carry, ls=ls, mult=mult):
                cr, ci = carry
                r0 = pl.multiple_of(i * SUBLANES, SUBLANES)
                hr, hi = xr[pl.ds(r0, SUBLANES), ls], xi[pl.ds(r0, SUBLANES), ls]
                for n, sh in enumerate((1, 2, 4)):
                    pr, pi = _cmul(mult[2 * n], mult[2 * n + 1], pltpu.roll(hr, sh, 0), pltpu.roll(hi, sh, 0))
                    hr, hi = hr + pr, hi + pi
                pr, pi = _cmul(mult[6], mult[7], cr, ci)
                hr, hi = hr + pr, hi + pi
                xr[pl.ds(r0, SUBLANES), ls] = hr
                xi[pl.ds(r0, SUBLANES), ls] = hi
                return jnp.broadcast_to(hr[7:8, :], hr.shape), jnp.broadcast_to(hi[7:8, :], hi.shape)

            cr, ci = lax.fori_loop(0, t // SUBLANES, step, (car_r[:, ls], car_i[:, ls]))
            car_r[:, ls] = cr
            car_i[:, ls] = ci
        hrb, hib = xr[...].astype(BF16), xi[...].astype(BF16)
        hr_ref[...] = hrb
        hi_ref[...] = hib
        for b in range(nblk):
            ch, st = slice(b * 128, (b + 1) * 128), slice(b * 512, (b + 1) * 512)
            yb = jnp.dot(hrb[:, st], cre_ref[b], preferred_element_type=F32)
            yb -= jnp.dot(hib[:, st], cim_ref[b], preferred_element_type=F32)
            y_ref[:, ch] = yb + ds_ref[:, ch] * u[:, ch]

    whole = lambda a: pl.BlockSpec(a.shape, lambda i, nd=a.ndim: (0,) * nd)
    return pl.pallas_call(
        body, name="ssm_fwd", grid=(s // t,),
        in_specs=[pl.BlockSpec((t, sw), lambda i: (i, ucb))] + [whole(a) for a in (bre, bim, cre, cim, consts, d_skip)],
        out_specs=[pl.BlockSpec((t, sw), lambda i: (i, 0)), pl.BlockSpec((t, nst), lambda i: (i, 0)), pl.BlockSpec((t, nst), lambda i: (i, 0))],
        out_shape=[jax.ShapeDtypeStruct((s, sw), F32), jax.ShapeDtypeStruct((s, nst), BF16), jax.ShapeDtypeStruct((s, nst), BF16)],
        scratch_shapes=[pltpu.VMEM((t, nst), F32), pltpu.VMEM((t, nst), F32), pltpu.VMEM((SUBLANES, nst), F32), pltpu.VMEM((SUBLANES, nst), F32)],
        compiler_params=_cparams(("arbitrary",)),
    )(u_arr, bre, bim, cre, cim, consts, d_skip)


def _ssm_bwd(dy, u_src, hr, hi, ctre, ctim, btre, btim, consts, d_skip):
    u_arr, sw, ucb = u_src
    s = dy.shape[0]
    nblk = ctre.shape[0]
    nst = nblk * 512
    t = _blk(s, SSM_ROWS)
    w = _blk(nst, SCAN_LANES)
    nt = s // t

    def body(dy_ref, u_ref, hr_ref, hi_ref, ctre_ref, ctim_ref, btre_ref, btim_ref, k_ref, ds_ref,
             du_ref, db_ref, dc_ref, da_ref, dd_ref, lr, li, hfr, hfi, car_r, car_i):
        @pl.when(pl.program_id(0) == 0)
        def _():
            car_r[...] = jnp.zeros(car_r.shape, F32)
            car_i[...] = jnp.zeros(car_i.shape, F32)
            db_ref[...] = jnp.zeros(db_ref.shape, F32)
            dc_ref[...] = jnp.zeros(dc_ref.shape, F32)
            da_ref[...] = jnp.zeros(da_ref.shape, F32)
            dd_ref[...] = jnp.zeros(dd_ref.shape, F32)

        dyv, u = dy_ref[...], u_ref[...]
        dyb, ub = dyv.astype(BF16), u.astype(BF16)
        hrb, hib = hr_ref[...], hi_ref[...]
        hfr[...] = hrb.astype(F32)
        hfi[...] = hib.astype(F32)
        for b in range(nblk):
            ch, st = slice(b * 128, (b + 1) * 128), slice(b * 512, (b + 1) * 512)
            lr[:, st] = jnp.dot(dyb[:, ch], ctre_ref[b], preferred_element_type=F32)
            li[:, st] = -jnp.dot(dyb[:, ch], ctim_ref[b], preferred_element_type=F32)
        last_row = lax.broadcasted_iota(jnp.int32, (SUBLANES, w), 0) == SUBLANES - 1
        for c in range(nst // w):
            ls = slice(c * w, (c + 1) * w)
            mult = [k_ref[i, :, ls] for i in range(8)]

            def step(i, carry, ls=ls, mult=mult):
                cr, ci, ar, ai = carry
                r0 = pl.multiple_of((t // SUBLANES - 1 - i) * SUBLANES, SUBLANES)
                gr, gi = lr[pl.ds(r0, SUBLANES), ls], li[pl.ds(r0, SUBLANES), ls]
                for n, sh in enumerate((1, 2, 4)):
                    pr, pi = _cmul_conj(mult[2 * n], mult[2 * n + 1], pltpu.roll(gr, SUBLANES - sh, 0), pltpu.roll(gi, SUBLANES - sh, 0))
                    gr, gi = gr + pr, gi + pi
                pr, pi = _cmul_conj(mult[6], mult[7], cr, ci)
                gr, gi = gr + pr, gi + pi
                lr[pl.ds(r0, SUBLANES), ls] = gr
                li[pl.ds(r0, SUBLANES), ls] = gi
                nr = jnp.where(last_row, cr, pltpu.roll(gr, SUBLANES - 1, 0))
                ni = jnp.where(last_row, ci, pltpu.roll(gi, SUBLANES - 1, 0))
                fr, fi = hfr[pl.ds(r0, SUBLANES), ls], hfi[pl.ds(r0, SUBLANES), ls]
                pr, pi = _cmul_conj(fr, fi, nr, ni)
                return (jnp.broadcast_to(gr[0:1, :], gr.shape), jnp.broadcast_to(gi[0:1, :], gi.shape), ar + pr, ai + pi)

            zero = jnp.zeros((SUBLANES, w), F32)
            cr, ci, ar, ai = lax.fori_loop(0, t // SUBLANES, step, (car_r[:, ls], car_i[:, ls], zero, zero))
            car_r[:, ls] = cr
            car_i[:, ls] = ci
            da_ref[0, :, ls] += ar
            da_ref[1, :, ls] += ai
        lrb, lib = lr[...].astype(BF16), li[...].astype(BF16)
        for b in range(nblk):
            ch, st = slice(b * 128, (b + 1) * 128), slice(b * 512, (b + 1) * 512)
            dub = jnp.dot(lrb[:, st], btre_ref[b], preferred_element_type=F32)
            dub += jnp.dot(lib[:, st], btim_ref[b], preferred_element_type=F32)
            du_ref[:, ch] = dub + ds_ref[:, ch] * dyv[:, ch]
            db_ref[0, b] += lax.dot_general(ub[:, ch], lrb[:, st], _TN, preferred_element_type=F32)
            db_ref[1, b] += lax.dot_general(ub[:, ch], lib[:, st], _TN, preferred_element_type=F32)
            dc_ref[0, b] += lax.dot_general(dyb[:, ch], hrb[:, st], _TN, preferred_element_type=F32)
            dc_ref[1, b] -= lax.dot_general(dyb[:, ch], hib[:, st], _TN, preferred_element_type=F32)
        dd_ref[...] += _colsum8(dyv * u)

    whole = lambda a: pl.BlockSpec(a.shape, lambda i, nd=a.ndim: (0,) * nd)
    rev = lambda wd, cb=0: pl.BlockSpec((t, wd), lambda i, cb=cb: (nt - 1 - i, cb))
    acc = lambda sh: pl.BlockSpec(sh, lambda i, nd=len(sh): (0,) * nd)
    db_shape, da_shape, dd_shape = (2, nblk, 128, 512), (2, SUBLANES, nst), (SUBLANES, sw)
    return pl.pallas_call(
        body, name="ssm_bwd", grid=(nt,),
        in_specs=[rev(sw), rev(sw, ucb), rev(nst), rev(nst)] + [whole(a) for a in (ctre, ctim, btre, btim, consts, d_skip)],
        out_specs=[rev(sw), acc(db_shape), acc(db_shape), acc(da_shape), acc(dd_shape)],
        out_shape=[jax.ShapeDtypeStruct((s, sw), F32), jax.ShapeDtypeStruct(db_shape, F32), jax.ShapeDtypeStruct(db_shape, F32),
                   jax.ShapeDtypeStruct(da_shape, F32), jax.ShapeDtypeStruct(dd_shape, F32)],
        scratch_shapes=[pltpu.VMEM((t, nst), F32)] * 4 + [pltpu.VMEM((SUBLANES, nst), F32)] * 2,
        compiler_params=_cparams(("arbitrary",)),
    )(dy, u_arr, hr, hi, ctre, ctim, btre, btim, consts, d_skip)


def _ssm_discretise(lam_re, lam_im, log_step):
    step = jnp.exp(log_step)
    e = jnp.exp(lam_re * step)
    ar, ai = e * jnp.cos(lam_im * step), e * jnp.sin(lam_im * step)
    den = lam_re * lam_re + lam_im * lam_im
    inv_r, inv_i = lam_re / den, -lam_im / den
    fr, fi = _cmul(ar - 1.0, ai, inv_r, inv_i)
    return step, ar, ai, inv_r, inv_i, fr, fi


def _ssm_prep_fn(lam_re, lam_im, log_step):
    _, ar, ai, _, _, fr, fi = _ssm_discretise(lam_re, lam_im, log_step)
    pw = [(ar, ai)]
    for n in range(1, 8):
        pw.append(_cmul(*pw[n - 1], ar, ai))
    return [p[0] for p in pw] + [p[1] for p in pw] + [fr, fi]


def _ssm_param_grad_fn(lam_re, lam_im, log_step, da_r, da_i, df_r, df_i):
    step, ar, ai, inv_r, inv_i, fr, fi = _ssm_discretise(lam_re, lam_im, log_step)
    pr, pi = _cmul_conj(inv_r, inv_i, df_r, df_i)
    dat_r, dat_i = da_r + pr, da_i + pi
    wr, wi = _cmul(fr, fi, inv_r, inv_i)
    dl_r, dl_i = _cmul_conj(-wr, -wi, df_r, df_i)
    dz_r, dz_i = _cmul_conj(ar, ai, dat_r, dat_i)
    dl_r, dl_i = dl_r + step * dz_r, dl_i + step * dz_i
    dstep = jnp.sum(dz_r * lam_re + dz_i * lam_im, axis=-1, keepdims=True)
    return [dl_r, dl_i, step * dstep]


def _block_diag(v, rows_first):
    g, a, b = v.shape
    nb = g // GROUPS_PER_BLOCK
    eye = jnp.eye(GROUPS_PER_BLOCK, dtype=v.dtype)
    v = v.reshape(nb, GROUPS_PER_BLOCK, a, 1, b) * eye[None, :, None, :, None]
    return v.reshape(nb, GROUPS_PER_BLOCK * a, GROUPS_PER_BLOCK * b)


def _block_diag_take(v, a, b):
    nb = v.shape[0]
    v = v.reshape(nb, GROUPS_PER_BLOCK, a, GROUPS_PER_BLOCK, b)
    return jnp.stack([v[:, g, :, g, :] for g in range(GROUPS_PER_BLOCK)], axis=1).reshape(nb * GROUPS_PER_BLOCK, a, b)


def _adamw(w, g, m, v):
    m = ADAM_B1 * m + (1.0 - ADAM_B1) * g
    v = ADAM_B2 * v + (1.0 - ADAM_B2) * (g * g)
    m_hat = m / (1.0 - ADAM_B1**ADAM_STEP)
    v_hat = v / (1.0 - ADAM_B2**ADAM_STEP)
    delta = -ADAM_LR * (m_hat / (jnp.sqrt(v_hat) + ADAM_EPS) + ADAM_WD * w)
    return delta, m, v


def _adamw_rows(name, w, g, m, v):
    wd = w.shape[1]
    return _rowwise(name, lambda a, b, c, d: (list(_adamw(a, b, c, d)), []), [w, g, m, v], [], [(wd, F32)] * 3, [], tm=128)


SMALL = ["b_ada", "norm1_g", "q_norm_g", "k_norm_g", "lam_re", "lam_im", "log_step", "b_re", "b_im", "c_re", "c_im",
         "d_skip", "b_glu", "attn_out_g", "ssm_out_g", "norm2_g"]
LARGE = ["w_in", "w_glu", "w_out", "w_ff1", "w_ff2"]
ORDER = ["w_ada", "b_ada", "norm1_g", "w_in", "q_norm_g", "k_norm_g", "lam_re", "lam_im", "log_step", "b_re", "b_im", "c_re",
         "c_im", "d_skip", "w_glu", "b_glu", "attn_out_g", "ssm_out_g", "w_out", "norm2_g", "w_ff1", "w_ff2"]


def _pack(arrs):
    flat = jnp.concatenate([a.reshape(-1) for a in arrs])
    rows = -(-flat.shape[0] // PACK_LANES)
    rows = -(-rows // SUBLANES) * SUBLANES
    return jnp.pad(flat, (0, rows * PACK_LANES - flat.shape[0])).reshape(rows, PACK_LANES)


def _unpack(packed, shapes):
    flat, out, o = packed.reshape(-1), [], 0
    for sh in shapes:
        n = int(np.prod(sh))
        out.append(flat[o : o + n].reshape(sh))
        o += n
    return out


def kernel(x, c, w_ada, b_ada, norm1_g, w_in, q_norm_g, k_norm_g, lam_re, lam_im, log_step, b_re, b_im, c_re, c_im, d_skip, w_glu, b_glu, attn_out_g, ssm_out_g, w_out, norm2_g, w_ff1, w_ff2, loss_target, m_w_ada, m_b_ada, m_norm1_g, m_w_in, m_q_norm_g, m_k_norm_g, m_lam_re, m_lam_im, m_log_step, m_b_re, m_b_im, m_c_re, m_c_im, m_d_skip, m_w_glu, m_b_glu, m_attn_out_g, m_ssm_out_g, m_w_out, m_norm2_g, m_w_ff1, m_w_ff2, v_w_ada, v_b_ada, v_norm1_g, v_w_in, v_q_norm_g, v_k_norm_g, v_lam_re, v_lam_im, v_log_step, v_b_re, v_b_im, v_c_re, v_c_im, v_d_skip, v_w_glu, v_b_glu, v_attn_out_g, v_ssm_out_g, v_w_out, v_norm2_g, v_w_ff1, v_w_ff2):
    args = dict(locals())
    wts = {n: args[n] for n in ORDER}
    mom = {n: args["m_" + n] for n in ORDER}
    var = {n: args["v_" + n] for n in ORDER}

    x2, tgt = x[0], loss_target[0]
    s, d = x2.shape
    aw = d // 2
    sw = d - aw
    n_groups = sw // SSM_GROUP
    nst = n_groups * STATE_DIM
    chip = 2 * lax.axis_index("x") + lax.axis_index("y")

    chip_idx = jnp.reshape(chip, (1,)).astype(jnp.int32)
    core_idx = jnp.reshape(lax.axis_index("c"), (1,)).astype(jnp.int32)

    def cast_into_stack(n):
        w2 = wts[n][0]
        rows, cols = w2.shape
        tr = _blk(rows, 256)
        return _indexed("cast_" + n, lambda a: a, chip_idx, [(w2, (tr, cols), lambda r, ix: (r, 0))],
                        (((4, rows, cols), BF16), (None, tr, cols), lambda r, ix: (ix[0], r, 0)), (rows // tr,))

    stacks = {n: cast_into_stack(n) for n in LARGE}
    dff = 4 * wts["w_ff2"].shape[1]

    c_all = _allgather8("gather_c", jnp.pad(c, ((0, SUBLANES - 1), (0, 0))))[::SUBLANES]
    n_mod = w_ada.shape[2]
    b_ada_mine = lax.dynamic_slice(b_ada, (0, chip * n_mod), (1, n_mod))
    sil_c = _single("silu_c", lambda v: [v * _sigmoid(v)], [c_all], [(c_all.shape, F32)])[0]
    mod_part = _mm("ada_fwd", sil_c, w_ada[0], "nn", F32, tn=512, epi=lambda acc, b: acc + b[0:1, :],
                   extras=[jnp.broadcast_to(b_ada_mine, (SUBLANES, n_mod))])
    mod_all = _allgather8("gather_mod", mod_part)
    me = 2 * chip + lax.axis_index("c")
    mod = jnp.concatenate([lax.dynamic_slice(mod_all, (16 * k + me, 0), (1, n_mod)) for k in range(4)], axis=1)
    sh1, sc1, g1, sh2, sc2, g2 = [mod[:, i * d : (i + 1) * d] for i in range(6)]

    (w_in_s,) = _gather_weights([stacks["w_in"]], mod_all)

    def gather_plan(refs):
        px0, py0, pc, chips = _place()
        slot = 2 * px0 + py0
        return [(r.at[slot], r.at[slot], (px, py, pc), r.at[2 * px + py]) for r in refs for (px, py) in chips]

    mix_sems_s, mix_sems_r, mix_stacks, mix_token = _split_start(
        "gather_mix_start", [stacks["w_glu"], stacks["w_out"]], gather_plan, 6, extra=[w_in_s])
    w_in_s = mix_stacks.pop()
    ff_sems_s, ff_sems_r, ff_stacks, ff_token = _split_start(
        "gather_ff_start", [stacks["w_ff1"], stacks["w_ff2"]], gather_plan, 6, extra=[w_in_s])
    w_in_s = ff_stacks.pop()
    sc1 = sc1 + (mix_token[0:1, 0:1] + ff_token[0:1, 0:1])

    def norm_mod(v, g, sc, sh):
        return [v * _rstd(v) * g * (1.0 + sc) + sh], []

    (h1b,) = _rowwise("norm1", norm_mod, [x2], [norm1_g, sc1, sh1], [(d, BF16)], [])
    proj = _mm("proj_in", h1b, w_in_s, "nn", F32, b_cs=True, tm=FF_TM)
    assert aw == sw
    q_src, k_src, v_src, u_src = [(proj, aw, i) for i in range(3)] + [(proj, sw, 3)]

    def qk_prep(q, k, v, gq, gk):
        qn = jnp.concatenate([h * _rstd(h) * gq for h in _heads(q)], axis=1)
        kn = jnp.concatenate([h * _rstd(h) * gk for h in _heads(k)], axis=1)
        return [qn, kn, v], []

    qn, kn, vb = _rowwise("qk_prep", qk_prep, [q_src, k_src, v_src], [q_norm_g, k_norm_g], [(aw, BF16)] * 3, [])

    pats = []
    for window, dil in DILATION_PATTERNS:
        assert window // dil == BAND and s % (dil * BAND) == 0
        bps = s // dil // BAND
        qp, kp, vp = [_to_streams(t, dil) for t in (qn, kn, vb)]
        o_p, l_p = _attn_fwd(f"attn_fwd_d{dil}", qp, kp, vp, bps, dil)
        pats.append((dil, bps, qp, kp, vp, _from_streams(o_p, dil), _from_streams(l_p, dil)))

    def combine(o1, o2, o3, l1, l2, l3):
        m = jnp.maximum(jnp.maximum(l1, l2), l3)
        tot = m + jnp.log(jnp.exp(l1 - m) + jnp.exp(l2 - m) + jnp.exp(l3 - m))
        wts3 = [jnp.exp(l - tot) for l in (l1, l2, l3)]
        mixed_heads = []
        for h, (a1, a2, a3) in enumerate(zip(_heads(o1), _heads(o2), _heads(o3))):
            w1, w2, w3 = [_from_lanes(w, h, HEAD_DIM) for w in wts3]
            mixed_heads.append(w1 * a1.astype(F32) + w2 * a2.astype(F32) + w3 * a3.astype(F32))
        return [jnp.concatenate(mixed_heads, axis=1), tot], []

    attn, lse = _rowwise("attn_mix", combine, [p[5] for p in pats] + [p[6] for p in pats], [], [(aw, F32), (HEAD_DIM, F32)], [])

    lam_re2, lam_im2, log_step2 = lam_re[0], lam_im[0], log_step[0].reshape(n_groups, 1)
    prep = _single("ssm_prep", _ssm_prep_fn, [lam_re2, lam_im2, log_step2], [((n_groups, STATE_DIM), F32)] * 18)
    p_r = jnp.stack([p.reshape(nst) for p in prep[:8]])
    p_i = jnp.stack([p.reshape(nst) for p in prep[8:16]])
    f_ri = jnp.stack(prep[16:])
    row = jnp.arange(SUBLANES)[:, None]
    zero = jnp.zeros((SUBLANES, nst), F32)
    sel = lambda cond, v: jnp.where(cond, jnp.broadcast_to(v[None, :], (SUBLANES, nst)), zero)
    fwd_consts = jnp.stack([sel(row >= 1, p_r[0]), sel(row >= 1, p_i[0]), sel(row >= 2, p_r[1]), sel(row >= 2, p_i[1]),
                            sel(row >= 4, p_r[3]), sel(row >= 4, p_i[3]), p_r, p_i])
    bwd_consts = jnp.stack([sel(row <= 6, p_r[0]), sel(row <= 6, p_i[0]), sel(row <= 5, p_r[1]), sel(row <= 5, p_i[1]),
                            sel(row <= 3, p_r[3]), sel(row <= 3, p_i[3]), p_r[::-1], p_i[::-1]])
    nblk = n_groups // GROUPS_PER_BLOCK
    f_rows = f_ri.reshape(2, nblk, 1, 512)
    braw_r = _block_diag(b_re[0].transpose(0, 2, 1), True)
    braw_i = _block_diag(b_im[0].transpose(0, 2, 1), True)

    def bbar_fn(br, bi, f):
        r, i = _cmul(f[0], f[1], br, bi)
        return [r, i]

    bbar_r, bbar_i = _single("ssm_bbar", bbar_fn, [braw_r, braw_i, f_rows], [(braw_r.shape, BF16)] * 2)
    ct_r, ct_i = _block_diag(c_re[0], True).astype(BF16), _block_diag(c_im[0], True).astype(BF16)
    ds2 = d_skip
    y_ssm, h_r, h_i = _ssm_fwd(u_src, bbar_r, bbar_i, ct_r.transpose(0, 2, 1), ct_i.transpose(0, 2, 1), fwd_consts, ds2)

    (yb,) = _rowwise("gelu", lambda v: ([_gelu(v)], []), [y_ssm], [], [(sw, BF16)], [])
    w_glu_s, w_out_s = _split_wait("gather_mix_wait", mix_stacks, mix_sems_s, mix_sems_r, yb, gather_plan)
    w_glu_f = w_glu_s.reshape(sw, sw)
    w_out_f = w_out_s.reshape(d, d)
    z = _mm("glu_fwd", yb, w_glu_f, "nn", F32)

    def glu(ys, zz, b):
        return [_gelu(ys) * _sigmoid(zz + b)], []

    (ssm,) = _rowwise("glu", glu, [y_ssm, z], [b_glu], [(sw, F32)], [])

    def cat_norm(a, sm, ga, gs):
        return [jnp.concatenate([a * _rstd(a) * ga, sm * _rstd(sm) * gs], axis=1)], []

    (catb,) = _rowwise("cat_norm", cat_norm, [attn, ssm], [attn_out_g, ssm_out_g], [(d, BF16)], [])
    mixed = _mm("mix_out", catb, w_out_f, "nn", F32, tm=FF_TM)

    def resid_norm(xv, mx, g1v, g, sc, sh):
        x1v = xv + g1v * mx
        return [x1v, x1v * _rstd(x1v) * g * (1.0 + sc) + sh], []

    x1, h2b = _rowwise("resid_norm2", resid_norm, [x2, mixed], [g1, norm2_g, sc2, sh2], [(d, F32), (d, BF16)], [])
    w_ff1_s, w_ff2_s = _split_wait("gather_ff_wait", ff_stacks, ff_sems_s, ff_sems_r, x1, gather_plan)
    w_ff2_f = w_ff2_s.reshape(dff, d)
    a_ff = _mm("ff1", h2b, w_ff1_s, "nn", BF16, b_cs=True, tm=FF_TM)
    sq_relu = lambda v: jnp.square(jnp.maximum(v.astype(F32), 0.0)).astype(BF16)
    ff = _mm("ff2", a_ff, w_ff2_f, "nn", F32, a_pro=sq_relu, tm=FF_TM)

    def loss_fn(x1v, ffv, tg, g2v):
        diff = x1v + g2v * ffv - tg
        dout = diff * (1.0 / d)
        return [dout, dout * g2v], [_colsum8(0.5 * diff * dout), _colsum8(dout * ffv)]

    dout, dffb, loss_acc, dg2_acc = _rowwise("loss", loss_fn, [x1, ff, tgt], [g2], [(d, F32), (d, BF16)], [(SUBLANES, d)] * 2)

    da_ff = _mm("ff2_bwd", dffb, w_ff2_f, "nt", BF16, epi=lambda acc, av: acc * (2.0 * jnp.maximum(av.astype(F32), 0.0)), extras=[a_ff], tm=FF_TM)
    gw = {}
    gw["w_ff2"] = _mm("ff2_wgrad", a_ff, dffb, "tn", BF16, a_pro=sq_relu, tm=FF_TM).reshape(4, dff // 4, d)
    gw["w_ff1"] = _mm("ff1_wgrad", h2b, da_ff, "tn", BF16, o_cs=True, tm=FF_TM)

    def add2(n, g, b):
        _, half, cols = b.shape
        tr = _blk(half, 256)
        nrb = half // tr
        blk = (None, tr, cols)
        return _indexed("grad_add2_" + n, lambda p, q: p.astype(F32) + q.astype(F32), core_idx,
                        [(g, blk, lambda k, r, ix: (k, ix[0] * nrb + r, 0)), (b, blk, lambda k, r, ix: (k, r, 0))],
                        ((b.shape, BF16), blk, lambda k, r, ix: (k, r, 0)), (4, nrb))

    def scatter_plan(refs):
        px0, py0, pc, chips = _place()
        nw = len(refs) // 2
        return [(refs[w].at[2 * px + py], refs[nw + w].at[j], (px, py, pc), refs[nw + w].at[j])
                for w in range(nw) for j, (px, py) in enumerate(chips)]

    early = ["w_ff2", "w_ff1"]
    parts = {n: add2(n, gw[n], b) for n, b in zip(early, _swap_halves([gw[n] for n in early], "swap_halves_ff"))}
    lands = [lax.empty((3,) + parts[n].shape[1:], BF16) for n in early]
    sc_sems_s, sc_sems_r, sc_bufs, sc_token = _split_start("scatter_ff_start", [parts[n] for n in early] + lands, scatter_plan, 6)
    sc2_late = sc2 + sc_token[0:1, 0:1]

    dh2 = _mm("ff1_bwd", da_ff, w_ff1_s, "nt", F32, b_cs=True, tm=FF_TM)

    def norm2_bwd(dh, x1v, do, mx, g, sc, g1v):
        r = _rstd(x1v)
        xh = x1v * r
        dn = dh * (1.0 + sc)
        dx = do + _rms_bwd(dn * g, xh, r)
        return [dx, dx * g1v], [_colsum8(dh * xh * g), _colsum8(dh), _colsum8(dn * xh), _colsum8(dx * mx)]

    dx1, dmixb, dsc2_acc, dsh2_acc, dn2g_acc, dg1_acc = _rowwise(
        "norm2_bwd", norm2_bwd, [dh2, x1, dout, mixed], [norm2_g, sc2_late, g1], [(d, F32), (d, BF16)], [(SUBLANES, d)] * 4)

    dcat = _mm("mix_out_bwd", dmixb, w_out_f, "nt", F32, tm=FF_TM)
    gw["w_out"] = _mm("mix_out_wgrad", catb, dmixb, "tn", BF16, tm=FF_TM).reshape(4, d // 4, d)

    def cat_bwd(da, dsm, a, sm, ga, gs):
        ra, rs = _rstd(a), _rstd(sm)
        ah, sh = a * ra, sm * rs
        return [_rms_bwd(da * ga, ah, ra), _rms_bwd(dsm * gs, sh, rs)], [_colsum8(da * ah), _colsum8(dsm * sh)]

    dattn, dssm, dga_acc, dgs_acc = _rowwise(
        "cat_norm_bwd", cat_bwd, [(dcat, aw, 0), (dcat, sw, 1), attn, ssm], [attn_out_g, ssm_out_g], [(aw, F32), (sw, F32)],
        [(SUBLANES, aw), (SUBLANES, sw)])

    def glu_bwd(dsm, ys, zz, b):
        sg = _sigmoid(zz + b)
        dz = dsm * _gelu(ys) * sg * (1.0 - sg)
        return [dz, dsm * sg], [_colsum8(dz)]

    dzb, dy_direct, dbglu_acc = _rowwise("glu_bwd", glu_bwd, [dssm, y_ssm, z], [b_glu], [(sw, BF16), (sw, F32)], [(SUBLANES, sw)])
    gw["w_glu"] = _mm("glu_wgrad", yb, dzb, "tn", BF16).reshape(4, sw // 4, sw)
    dys = _mm("glu_bwd_mm", dzb, w_glu_f, "nt", F32, epi=lambda acc, direct, ys: (acc + direct) * _gelu_grad(ys), extras=[dy_direct, y_ssm])
    du, db_bar, dc_bd, da_acc, dds_acc = _ssm_bwd(dys, u_src, h_r, h_i, ct_r, ct_i, bbar_r.transpose(0, 2, 1),
                                                  bbar_i.transpose(0, 2, 1), bwd_consts, ds2)

    def delta_fn(da, a):
        return [da, _to_lanes([jnp.sum(p * q, axis=1, keepdims=True) for p, q in zip(_heads(da), _heads(a))])], []

    dob, delta = _rowwise("attn_delta", delta_fn, [dattn, attn], [], [(aw, BF16), (HEAD_DIM, F32)], [])
    dqs, dks, dvs = [], [], []
    for dil, bps, qp, kp, vp, _, _ in pats:
        dop, lsp, dlp = [_to_streams(t, dil) for t in (dob, lse, delta)]
        dq_p, dk_p, dv_p = _attn_bwd(f"attn_bwd_d{dil}", qp, kp, vp, dop, lsp, dlp, bps, dil)
        dqs.append(_from_streams(dq_p, dil))
        dks.append(_from_streams(dk_p, dil))
        dvs.append(_from_streams(dv_p, dil))

    def qk_bwd(dq1, dq2, dq3, dk1, dk2, dk3, dv1, dv2, dv3, duv, q, k, gq, gk):
        def one(dn, raw, g):
            outs, acc = [], 0.0
            for dh, h in zip(_heads(dn), _heads(raw)):
                r = _rstd(h)
                hh = h * r
                outs.append(_rms_bwd(dh * g, hh, r))
                acc = acc + _colsum8(dh * hh)
            return jnp.concatenate(outs, axis=1), acc

        tot = lambda a, b, c: (a.astype(F32) + b.astype(F32)) + c.astype(F32)
        dq, gq_acc = one(tot(dq1, dq2, dq3), q, gq)
        dk, gk_acc = one(tot(dk1, dk2, dk3), k, gk)
        return [jnp.concatenate([dq, dk, tot(dv1, dv2, dv3), duv], axis=1)], [gq_acc, gk_acc]

    dprojb, dgq_acc, dgk_acc = _rowwise("qk_norm_bwd", qk_bwd, dqs + dks + dvs + [du, q_src, k_src], [q_norm_g, k_norm_g],
                                        [(3 * aw + sw, BF16)], [(SUBLANES, HEAD_DIM)] * 2, tm=128)
    gw["w_in"] = _mm("proj_in_wgrad", h1b, dprojb, "tn", BF16, o_cs=True, tm=FF_TM)
    late = [n for n in LARGE if n not in early]
    parts.update({n: add2(n, gw[n], b) for n, b in zip(late, _swap_halves([gw[n] for n in late], "swap_halves_rest"))})
    lands_late = [lax.empty((3,) + parts[n].shape[1:], BF16) for n in late]
    rest_sems_s, rest_sems_r, rest_bufs, rest_token = _split_start(
        "scatter_rest_start", [parts[n] for n in late] + lands_late, scatter_plan, 3 * len(late))
    sc1_late = sc1 + rest_token[0:1, 0:1]
    dh1 = _mm("proj_in_bwd", dprojb, w_in_s, "nt", F32, b_cs=True, tm=FF_TM)

    def norm1_bwd(dh, xv, dxr, g, sc):
        r = _rstd(xv)
        xh = xv * r
        dn = dh * (1.0 + sc)
        return [dxr + _rms_bwd(dn * g, xh, r)], [_colsum8(dh * xh * g), _colsum8(dh), _colsum8(dn * xh)]

    grad_x, dsc1_acc, dsh1_acc, dn1g_acc = _rowwise("norm1_bwd", norm1_bwd, [dh1, x2, dx1], [norm1_g, sc1_late], [(d, F32)], [(SUBLANES, d)] * 3)

    def b_grad_fn(dbb, br, bi, f):
        fr, fi = f[0], f[1]
        dbr, dbi = _cmul_conj(fr, fi, dbb[0], dbb[1])
        dfr = jnp.sum(dbb[0] * br + dbb[1] * bi, axis=1, keepdims=True)
        dfi = jnp.sum(dbb[1] * br - dbb[0] * bi, axis=1, keepdims=True)
        return [dbr, dbi, dfr, dfi]

    dbr_bd, dbi_bd, df_r, df_i = _single("ssm_b_grad", b_grad_fn, [db_bar, braw_r, braw_i, f_rows],
                                         [(braw_r.shape, F32)] * 2 + [((nblk, 1, 512), F32)] * 2)

    def sums_fn(*accs):
        return [jnp.sum(a, axis=-2, keepdims=True) for a in accs]

    acc_list = [dsh1_acc, dsc1_acc, dg1_acc, dsh2_acc, dsc2_acc, dg2_acc, dn1g_acc, dgq_acc, dgk_acc, dds_acc, dbglu_acc, dga_acc,
                dgs_acc, dn2g_acc, loss_acc, da_acc]
    sums = _single("small_sums", sums_fn, acc_list, [(a.shape[:-2] + (1, a.shape[-1]), F32) for a in acc_list])
    (dsh1, dsc1, dg1, dsh2, dsc2, dg2, dn1g, dgq, dgk, dds, dbglu, dga, dgs, dn2g, loss_row, da_sum) = sums
    loss_dev = _single("loss_sum", lambda v: [jnp.sum(v, axis=1, keepdims=True)], [loss_row], [((1, 1), F32)])[0]
    gp = (n_groups, STATE_DIM)
    dlam_re, dlam_im, dlog_step = _single(
        "ssm_param_grad", _ssm_param_grad_fn,
        [lam_re2, lam_im2, log_step2, da_sum[0].reshape(gp), da_sum[1].reshape(gp), df_r.reshape(gp), df_i.reshape(gp)],
        [(gp, F32), (gp, F32), ((n_groups, 1), F32)])
    small_dev = {
        "b_ada": jnp.concatenate([dsh1, dsc1, dg1, dsh2, dsc2, dg2], axis=1),
        "norm1_g": dn1g, "q_norm_g": dgq, "k_norm_g": dgk, "lam_re": dlam_re, "lam_im": dlam_im, "log_step": dlog_step,
        "b_re": _block_diag_take(dbr_bd, SSM_GROUP, STATE_DIM).transpose(0, 2, 1),
        "b_im": _block_diag_take(dbi_bd, SSM_GROUP, STATE_DIM).transpose(0, 2, 1),
        "c_re": _block_diag_take(dc_bd[0], SSM_GROUP, STATE_DIM), "c_im": _block_diag_take(dc_bd[1], SSM_GROUP, STATE_DIM),
        "d_skip": dds, "b_glu": dbglu, "attn_out_g": dga, "ssm_out_g": dgs, "norm2_g": dn2g,
    }

    packed = _pack([small_dev[n] for n in SMALL])
    rows = packed.shape[0]
    gathered = _allgather8("gather_small", packed).reshape(8, rows, PACK_LANES)

    def sum8_fn(g):
        acc = g[0]
        for i in range(1, 8):
            acc = acc + g[i]
        return [acc]

    g_small = _single("sum_small", sum8_fn, [gathered], [((rows, PACK_LANES), F32)])[0]
    w_small, m_small, v_small = [_pack([src[n] for n in SMALL]) for src in (wts, mom, var)]
    upd_small = _adamw_rows("adamw_small", w_small, g_small, m_small, v_small)
    shapes = [wts[n].shape for n in SMALL]
    out = {}
    for kind, arr in zip(("grad", "delta", "new_m", "new_v"), (g_small,) + tuple(upd_small)):
        for n, val in zip(SMALL, _unpack(arr, shapes)):
            out[kind, n] = val

    dmod_all = gathered[:, 0 : -(-6 * d // PACK_LANES), :].reshape(8, -1)[:, : 6 * d]
    dmod_mine = lax.dynamic_slice(dmod_all, (0, chip * n_mod), (8, n_mod))
    g_ada = _mm("ada_wgrad", sil_c, dmod_mine, "tn", F32, tm=512, tn=512)
    upd = _adamw_rows("adamw_w_ada", w_ada[0], g_ada, m_w_ada[0], v_w_ada[0])
    for kind, val in zip(("grad", "delta", "new_m", "new_v"), (g_ada,) + tuple(upd)):
        out[kind, "w_ada"] = val[None]

    sc_bufs = _split_wait("scatter_ff_wait", sc_bufs, sc_sems_s, sc_sems_r, grad_x, scatter_plan)
    parts.update(zip(early, sc_bufs[: len(early)]))
    got2 = dict(zip(early, sc_bufs[len(early) :]))
    rest_bufs = _split_wait("scatter_rest_wait", rest_bufs, rest_sems_s, rest_sems_r, g_small, scatter_plan)
    parts.update(zip(late, rest_bufs[: len(late)]))
    got2.update(zip(late, rest_bufs[len(late) :]))

    def add4(n, p, b):
        _, half, cols = p.shape
        tr = _blk(half, 256)
        blk = (None, tr, cols)
        fn = lambda own, q0, q1, q2: ((own.astype(F32) + q0.astype(F32)) + q1.astype(F32)) + q2.astype(F32)
        srcs = [(p, blk, lambda r, ix: (ix[0], r, 0))] + [(b, blk, lambda r, ix, j=j: (j, r, 0)) for j in range(3)]
        return _indexed("grad_add4_" + n, fn, chip_idx, srcs, (((half, cols), F32), (tr, cols), lambda r, ix: (r, 0)), (half // tr,))

    finals = [add4(n, parts[n], got2[n]) for n in LARGE]
    for n, full in zip(LARGE, _join_halves(finals)):
        g = full.reshape(wts[n].shape[1:])
        upd = _adamw_rows("adamw_" + n, wts[n][0], g, mom[n][0], var[n][0])
        for kind, val in zip(("grad", "delta", "new_m", "new_v"), (g,) + tuple(upd)):
            out[kind, n] = val[None]

    loss = lax.psum(loss_dev[0, 0], ("x", "y", "c"))
    return (loss, grad_x[None], *[out[kind, n] for kind in ("grad", "delta", "new_m", "new_v") for n in ORDER])
```

```python
import functools
import math

import jax
import jax.numpy as jnp
import numpy as np
from jax import lax
from jax.experimental import pallas as pl
from jax.experimental.pallas import tpu as pltpu

F32 = jnp.float32
BF16 = jnp.bfloat16
MESH = pl.DeviceIdType.MESH

EPS = 1e-6
HEAD_DIM = 128
SSM_GROUP = 16
STATE_DIM = 64
GROUPS_PER_BLOCK = 8
DILATION_PATTERNS = ((128, 1), (512, 4), (2048, 16))
BAND = 128
NEG = -1e30

ADAM_LR, ADAM_B1, ADAM_B2, ADAM_EPS, ADAM_WD, ADAM_STEP = 0.001, 0.9, 0.999, 1e-08, 0.01, 10

V7X_VMEM_BYTES = 64 * 2**20
VMEM_LIMIT = 48 * 2**20
JOIN_VMEM_LIMIT = 56 * 2**20
SUBLANES = 8
PACK_LANES = 1024


def _cparams(sem=None):
    return pltpu.CompilerParams(dimension_semantics=sem, vmem_limit_bytes=VMEM_LIMIT)


def _blk(n, want):
    b = min(n, want)
    while n % b:
        b //= 2
    return b


def _colsum8(v):
    tm, w = v.shape
    return v.reshape(tm // SUBLANES, SUBLANES, w).sum(axis=0)


def _rowwise(name, fn, rows, vecs, out_rows, out_accs, tm=256, n_rows=None):
    rows = [r if isinstance(r, tuple) else (r, r.shape[1], 0) for r in rows]
    rows = [r if len(r) == 4 else r + (0,) for r in rows]
    s = rows[0][0].shape[0] if n_rows is None else n_rows
    tm = _blk(s, tm)
    assert all(r[3] % tm == 0 for r in rows)
    n_in, n_or = len(rows) + len(vecs), len(out_rows)

    def body(*refs):
        outs, accs = fn(*[r[...] for r in refs[:n_in]])
        for r, v in zip(refs[n_in : n_in + n_or], outs):
            r[...] = v.astype(r.dtype)
        if out_accs:
            acc_refs = refs[n_in + n_or :]

            @pl.when(pl.program_id(0) == 0)
            def _():
                for r in acc_refs:
                    r[...] = jnp.zeros(r.shape, r.dtype)

            for r, v in zip(acc_refs, accs):
                r[...] += v

    in_specs = [pl.BlockSpec((tm, w), lambda i, cb=cb, rb=r0 // tm: (i + rb, cb)) for (_, w, cb, r0) in rows]
    in_specs += [pl.BlockSpec(v.shape, lambda i, nd=v.ndim: (0,) * nd) for v in vecs]
    out_specs = [pl.BlockSpec((tm, w), lambda i: (i, 0)) for (w, _) in out_rows]
    out_specs += [pl.BlockSpec(sh, lambda i, nd=len(sh): (0,) * nd) for sh in out_accs]
    out_shape = [jax.ShapeDtypeStruct((s, w), dt) for (w, dt) in out_rows]
    out_shape += [jax.ShapeDtypeStruct(sh, F32) for sh in out_accs]
    res = pl.pallas_call(
        body, name=name, grid=(s // tm,), in_specs=in_specs, out_specs=out_specs, out_shape=out_shape,
        compiler_params=_cparams(("arbitrary",)),
    )(*[r[0] for r in rows], *vecs)
    return res


def _single(name, fn, ins, out_shapes):
    n_in = len(ins)

    def body(*refs):
        outs = fn(*[r[...] for r in refs[:n_in]])
        for r, v in zip(refs[n_in:], outs):
            r[...] = v.astype(r.dtype)

    vm = pl.BlockSpec(memory_space=pltpu.VMEM)
    return pl.pallas_call(
        body, name=name, in_specs=[vm] * n_in, out_specs=[vm] * len(out_shapes),
        out_shape=[jax.ShapeDtypeStruct(sh, dt) for sh, dt in out_shapes], compiler_params=_cparams(),
    )(*ins)


_NN = (((1,), (0,)), ((), ()))
_NT = (((1,), (1,)), ((), ()))
_TN = (((0,), (0,)), ((), ()))


def _mm(name, a, b, mode, out_dtype, *, tm=512, tn=1024, tk=2048, b_cs=False, o_cs=False, epi=None, extras=(), a_pro=None):
    if mode == "tn":
        k, m = a.shape
        n = b.shape[1]
    else:
        m, k = a.shape
        if mode == "nn":
            n = b.shape[2] * 4 if b_cs else b.shape[1]
        else:
            n = b.shape[1] if b_cs else b.shape[0]
    tm = _blk(m, tm)
    tn = _blk(n // 4 if (b_cs and mode == "nn") or o_cs else n, tn)
    tk = _blk(k // 4 if (b_cs and mode == "nt") else k, tk)
    nk = k // tk
    dims = {"nn": _NN, "nt": _NT, "tn": _TN}[mode]
    n_ex = len(extras)

    def body(*refs):
        a_ref, b_ref = refs[0], refs[1]
        ex_refs, o_ref = refs[2 : 2 + n_ex], refs[2 + n_ex]

        def finish(acc):
            if epi is not None:
                acc = epi(acc, *[r[...] for r in ex_refs])
            o_ref[...] = acc.astype(o_ref.dtype)

        a_blk = a_ref[...] if a_pro is None else a_pro(a_ref[...])
        part = lax.dot_general(a_blk, b_ref[...], dims, preferred_element_type=F32)
        if nk == 1:
            finish(part)
        else:
            acc_ref = refs[3 + n_ex]
            kk = pl.program_id(2)

            @pl.when(kk == 0)
            def _():
                acc_ref[...] = part

            @pl.when(kk > 0)
            def _():
                acc_ref[...] += part

            @pl.when(kk == nk - 1)
            def _():
                finish(acc_ref[...])

    a_spec = pl.BlockSpec((tk, tm), lambda i, j, kk: (kk, i)) if mode == "tn" else pl.BlockSpec((tm, tk), lambda i, j, kk: (i, kk))
    if mode == "nn":
        if b_cs:
            per = (n // 4) // tn
            b_spec = pl.BlockSpec((None, tk, tn), lambda i, j, kk: (j // per, kk, j % per))
        else:
            b_spec = pl.BlockSpec((tk, tn), lambda i, j, kk: (kk, j))
    elif mode == "nt":
        if b_cs:
            per = (k // 4) // tk
            b_spec = pl.BlockSpec((None, tn, tk), lambda i, j, kk: (kk // per, j, kk % per))
        else:
            b_spec = pl.BlockSpec((tn, tk), lambda i, j, kk: (j, kk))
    else:
        b_spec = pl.BlockSpec((tk, tn), lambda i, j, kk: (kk, j))
    if o_cs:
        per = (n // 4) // tn
        o_spec = pl.BlockSpec((None, tm, tn), lambda i, j, kk: (j // per, i, j % per))
        o_shape = jax.ShapeDtypeStruct((4, m, n // 4), out_dtype)
    else:
        o_spec = pl.BlockSpec((tm, tn), lambda i, j, kk: (i, j))
        o_shape = jax.ShapeDtypeStruct((m, n), out_dtype)
    ex_specs = [pl.BlockSpec((tm, tn), lambda i, j, kk: (i, j)) for _ in extras]
    return pl.pallas_call(
        body, name=name, grid=(m // tm, n // tn, nk), in_specs=[a_spec, b_spec] + ex_specs, out_specs=o_spec, out_shape=o_shape,
        scratch_shapes=[pltpu.VMEM((tm, tn), F32)] if nk > 1 else [],
        compiler_params=_cparams(("parallel", "parallel", "arbitrary")),
    )(a, b, *extras)


def _place():
    x, y, c = lax.axis_index("x"), lax.axis_index("y"), lax.axis_index("c")
    return x, y, c, [(1 - x, y), (x, 1 - y), (1 - x, 1 - y)]


def _allgather8(name, v):
    m_per, n = v.shape

    def body(x_ref, out_ref, send_sems, recv_sems, local_sem):
        x, y, c, chips = _place()
        me, sibling = (x, y, c), (x, y, 1 - c)

        def rows(px, py, pc):
            return out_ref.at[pl.ds((4 * px + 2 * py + pc) * m_per, m_per), :]

        def copy(k, block, to, src=None):
            return pltpu.make_async_remote_copy(
                src_ref=rows(*block) if src is None else src, dst_ref=rows(*block), send_sem=send_sems.at[k],
                recv_sem=recv_sems.at[k], device_id=to, device_id_type=MESH)

        mine = pltpu.make_async_copy(x_ref, rows(*me), local_sem)
        mine.start()
        first = [copy(0, me, sibling, src=x_ref)]
        first += [copy(1 + j, me, (*chip, c), src=x_ref) for j, chip in enumerate(chips)]
        for cp in first:
            cp.start()
        passed = [copy(4 + j, (*chip, c), sibling) for j, chip in enumerate(chips)]
        for j, chip in enumerate(chips):
            copy(1 + j, (*chip, c), me).wait_recv()
            passed[j].start()
        copy(0, sibling, me).wait_recv()
        for j, chip in enumerate(chips):
            copy(4 + j, (*chip, 1 - c), me).wait_recv()
        for cp in first + passed:
            cp.wait_send()
        mine.wait()

    return pl.pallas_call(
        body, name=name, out_shape=jax.ShapeDtypeStruct((8 * m_per, n), v.dtype),
        in_specs=[pl.BlockSpec(memory_space=pltpu.VMEM)], out_specs=pl.BlockSpec(memory_space=pltpu.VMEM),
        scratch_shapes=[pltpu.SemaphoreType.DMA((7,)), pltpu.SemaphoreType.DMA((7,)), pltpu.SemaphoreType.DMA],
        compiler_params=_cparams(),
    )(v)


def _gather_weights(stacks, after):
    nw = len(stacks)

    def body(*refs):
        outs = refs[nw + 1 : 2 * nw + 1]
        send_sems, recv_sems = refs[2 * nw + 1 :]
        x, y, c, chips = _place()
        me, sibling = (x, y, c), (x, y, 1 - c)

        def copy(w, k, block, to):
            px, py, pc = block
            half = outs[w].shape[1] // 2
            rows = outs[w].at[2 * px + py, pl.ds(pc * half, half), :]
            return pltpu.make_async_remote_copy(src_ref=rows, dst_ref=rows, send_sem=send_sems.at[6 * w + k],
                                                recv_sem=recv_sems.at[6 * w + k], device_id=to, device_id_type=MESH)

        sent = [copy(w, j, me, (*chip, c)) for w in range(nw) for j, chip in enumerate(chips)]
        for cp in sent:
            cp.start()
        for w in range(nw):
            for j, chip in enumerate(chips):
                copy(w, j, (*chip, c), me).wait_recv()
                passed = copy(w, 3 + j, (*chip, c), sibling)
                passed.start()
                sent.append(passed)
        for w in range(nw):
            for j, chip in enumerate(chips):
                copy(w, 3 + j, (*chip, 1 - c), me).wait_recv()
        for cp in sent:
            cp.wait_send()

    hbm = pl.BlockSpec(memory_space=pl.ANY)
    return pl.pallas_call(
        body, name="gather_weights", out_shape=[jax.ShapeDtypeStruct(w.shape, w.dtype) for w in stacks],
        in_specs=[hbm] * (nw + 1), out_specs=[hbm] * nw, input_output_aliases={i: i for i in range(nw)},
        scratch_shapes=[pltpu.SemaphoreType.DMA((6 * nw,)), pltpu.SemaphoreType.DMA((6 * nw,))],
        compiler_params=_cparams(),
    )(*stacks, after)


def _swap_halves(gs, name):
    nw = len(gs)

    def body(*refs):
        ins, gots = refs[:nw], refs[nw : 2 * nw]
        send_sems, recv_sems = refs[2 * nw :]
        x, y, c, _ = _place()
        cps = []
        for w in range(nw):
            half = ins[w].shape[1] // 2
            cps.append(pltpu.make_async_remote_copy(
                src_ref=ins[w].at[:, pl.ds((1 - c) * half, half), :], dst_ref=gots[w], send_sem=send_sems.at[w],
                recv_sem=recv_sems.at[w], device_id=(x, y, 1 - c), device_id_type=MESH))
            cps[-1].start()
        for cp in cps:
            cp.wait()

    hbm = pl.BlockSpec(memory_space=pl.ANY)
    return pl.pallas_call(
        body, name=name, out_shape=[jax.ShapeDtypeStruct((4, g.shape[1] // 2, g.shape[2]), g.dtype) for g in gs],
        in_specs=[hbm] * nw, out_specs=[hbm] * nw,
        scratch_shapes=[pltpu.SemaphoreType.DMA((nw,)), pltpu.SemaphoreType.DMA((nw,))], compiler_params=_cparams(),
    )(*gs)


def _split_start(name, bufs, plan, n, extra=()):
    n_plan = len(bufs)
    bufs = list(bufs) + list(extra)
    nb = len(bufs)

    def body(*refs):
        send_sems, recv_sems, token = refs[nb], refs[nb + 1], refs[-1]
        for k, (src, dst, to, _) in enumerate(plan(refs[:n_plan])):
            pltpu.make_async_remote_copy(src_ref=src, dst_ref=dst, send_sem=send_sems.at[k], recv_sem=recv_sems.at[k],
                                         device_id=to, device_id_type=MESH).start()
        token[...] = jnp.zeros(token.shape, token.dtype)

    hbm, sem = pl.BlockSpec(memory_space=pltpu.HBM), pl.BlockSpec(memory_space=pltpu.SEMAPHORE)
    res = pl.pallas_call(
        body, name=name,
        out_shape=(pltpu.SemaphoreType.DMA((n,)), pltpu.SemaphoreType.DMA((n,)), *[pltpu.HBM(b.shape, b.dtype) for b in bufs],
                   jax.ShapeDtypeStruct((SUBLANES, 128), F32)),
        in_specs=(hbm,) * nb, out_specs=(sem, sem) + (hbm,) * nb + (pl.BlockSpec(memory_space=pltpu.VMEM),),
        input_output_aliases={i: 2 + i for i in range(nb)},
        compiler_params=pltpu.CompilerParams(has_side_effects=pltpu.SideEffectType.DATAFLOW_SIDE_EFFECTING),
    )(*[pltpu.with_memory_space_constraint(b, pltpu.HBM) for b in bufs])
    return res[0], res[1], list(res[2 : 2 + nb]), res[-1]


def _split_wait(name, bufs, send_sems, recv_sems, after, plan):
    nb = len(bufs)

    def body(*refs):
        send, recv = refs[nb], refs[nb + 1]
        for k, (src, _, to, land) in enumerate(plan(refs[:nb])):
            cp = pltpu.make_async_remote_copy(src_ref=src, dst_ref=land, send_sem=send.at[k], recv_sem=recv.at[k],
                                              device_id=to, device_id_type=MESH)
            cp.wait_send()
            cp.wait_recv()

    hbm, sem = pl.BlockSpec(memory_space=pltpu.HBM), pl.BlockSpec(memory_space=pltpu.SEMAPHORE)
    return pl.pallas_call(
        body, name=name, out_shape=tuple(pltpu.HBM(b.shape, b.dtype) for b in bufs),
        in_specs=(hbm,) * nb + (sem, sem, pl.BlockSpec(memory_space=pl.ANY)), out_specs=(hbm,) * nb,
        input_output_aliases={i: i for i in range(nb)},
        compiler_params=pltpu.CompilerParams(has_side_effects=pltpu.SideEffectType.DATAFLOW_SIDE_EFFECTING),
    )(*bufs, send_sems, recv_sems, after)


def _indexed(name, fn, idx, ins, out, grid):
    def body(idx_ref, *refs):
        refs[-1][...] = fn(*[r[...] for r in refs[:-1]]).astype(refs[-1].dtype)

    (o_shape, o_dtype), o_block, o_map = out
    return pl.pallas_call(
        body, name=name, out_shape=jax.ShapeDtypeStruct(o_shape, o_dtype),
        grid_spec=pltpu.PrefetchScalarGridSpec(
            num_scalar_prefetch=1, grid=grid, in_specs=[pl.BlockSpec(blk, mp) for (_, blk, mp) in ins],
            out_specs=pl.BlockSpec(o_block, o_map)),
        compiler_params=_cparams(("arbitrary",) * len(grid)),
    )(idx, *[a for (a, _, _) in ins])


def _join_halves(fs):
    nw = len(fs)

    def body(*refs):
        ins, outs, lands = refs[:nw], refs[nw : 2 * nw], refs[2 * nw : 3 * nw]
        send_sems, recv_sems, local_sems = refs[3 * nw :]
        x, y, c, _ = _place()
        sibling = (x, y, 1 - c)
        cps = [pltpu.make_async_remote_copy(src_ref=ins[w], dst_ref=lands[w], send_sem=send_sems.at[w], recv_sem=recv_sems.at[w],
                                            device_id=sibling, device_id_type=MESH) for w in range(nw)]
        outs_own = [pltpu.make_async_copy(ins[w], outs[w].at[c], local_sems.at[w]) for w in range(nw)]
        outs_got = [pltpu.make_async_copy(lands[w], outs[w].at[1 - c], local_sems.at[nw + w]) for w in range(nw)]
        for cp in cps + outs_own:
            cp.start()
        for w in range(nw):
            cps[w].wait_recv()
            outs_got[w].start()
        for w in range(nw):
            cps[w].wait_send()
            outs_own[w].wait()
            outs_got[w].wait()

    hbm, vm = pl.BlockSpec(memory_space=pl.ANY), pl.BlockSpec(memory_space=pltpu.VMEM)
    return pl.pallas_call(
        body, name="join_halves", out_shape=[jax.ShapeDtypeStruct((2,) + f.shape, f.dtype) for f in fs],
        in_specs=[vm] * nw, out_specs=[hbm] * nw,
        scratch_shapes=[pltpu.VMEM(f.shape, f.dtype) for f in fs]
        + [pltpu.SemaphoreType.DMA((nw,)), pltpu.SemaphoreType.DMA((nw,)), pltpu.SemaphoreType.DMA((2 * nw,))],
        compiler_params=pltpu.CompilerParams(vmem_limit_bytes=JOIN_VMEM_LIMIT),
    )(*fs)


def _rstd(v):
    return lax.rsqrt(jnp.mean(v * v, axis=-1, keepdims=True) + EPS)


def _rms_bwd(dy, xh, r):
    return r * (dy - xh * jnp.mean(dy * xh, axis=-1, keepdims=True))


def _heads(v):
    return [v[:, h * HEAD_DIM : (h + 1) * HEAD_DIM] for h in range(v.shape[1] // HEAD_DIM)]


def _gelu(v):
    k = math.sqrt(2.0 / math.pi)
    return 0.5 * v * (1.0 + jnp.tanh(k * (v + 0.044715 * v * v * v)))


def _gelu_grad(v):
    k = math.sqrt(2.0 / math.pi)
    t = jnp.tanh(k * (v + 0.044715 * v * v * v))
    return 0.5 * (1.0 + t) + 0.5 * v * (1.0 - t * t) * k * (1.0 + 3 * 0.044715 * v * v)


def _sigmoid(v):
    return 1.0 / (1.0 + jnp.exp(-v))


def _slopes(n_heads):
    return [2.0 ** (-8.0 * (h + 1.0) / n_heads) for h in range(n_heads)]


def _band_iotas():
    ri = lax.broadcasted_iota(jnp.int32, (BAND, BAND), 0)
    ci = lax.broadcasted_iota(jnp.int32, (BAND, BAND), 1)
    return ri, ci


def _to_lanes(cols):
    rows = cols[0].shape[0]
    lane = lax.broadcasted_iota(jnp.int32, (rows, HEAD_DIM), 1)
    out = jnp.zeros((rows, HEAD_DIM), F32)
    for h, col in enumerate(cols):
        out = jnp.where(lane == h, col, out)
    return out


def _from_lanes(v, h, width):
    return jnp.broadcast_to(v[:, h : h + 1], (v.shape[0], width))


def _attn_fwd(name, q, k, v, bps, dil):
    s, aw = q.shape
    n_heads, nb = aw // HEAD_DIM, s // BAND
    scale = HEAD_DIM**-0.5
    slopes = _slopes(n_heads)

    def body(q_ref, kp_ref, kc_ref, vp_ref, vc_ref, o_ref, l_ref):
        b = pl.program_id(0)
        first = (b % bps) == 0
        ri, ci = _band_iotas()
        dist_c = (ri - ci).astype(F32)
        dist_p = (ri - ci + BAND).astype(F32)
        valid_c = ci <= ri
        valid_p = ci >= ri + jnp.where(first, BAND, 0)
        hs = range(n_heads)
        sls = [slice(h * HEAD_DIM, (h + 1) * HEAD_DIM) for h in hs]
        qs = [q_ref[:, sl] for sl in sls]
        sc = [lax.dot_general(qs[h], kc_ref[:, sls[h]], _NT, preferred_element_type=F32) for h in hs]
        sp = [lax.dot_general(qs[h], kp_ref[:, sls[h]], _NT, preferred_element_type=F32) for h in hs]
        sc = [jnp.where(valid_c, sc[h] * scale - (slopes[h] * dil) * dist_c, NEG) for h in hs]
        sp = [jnp.where(valid_p, sp[h] * scale - (slopes[h] * dil) * dist_p, NEG) for h in hs]
        m = [jnp.maximum(jnp.max(sc[h], axis=1, keepdims=True), jnp.max(sp[h], axis=1, keepdims=True)) for h in hs]
        pc = [jnp.exp(sc[h] - m[h]) for h in hs]
        pp = [jnp.exp(sp[h] - m[h]) for h in hs]
        den = [jnp.sum(pc[h], axis=1, keepdims=True) + jnp.sum(pp[h], axis=1, keepdims=True) for h in hs]
        acc = [jnp.dot(pc[h].astype(BF16), vc_ref[:, sls[h]], preferred_element_type=F32) for h in hs]
        acc = [acc[h] + jnp.dot(pp[h].astype(BF16), vp_ref[:, sls[h]], preferred_element_type=F32) for h in hs]
        o_ref[...] = jnp.concatenate([acc[h] / den[h] for h in hs], axis=1).astype(o_ref.dtype)
        l_ref[...] = _to_lanes([m[h] + jnp.log(den[h]) for h in hs])

    cur = pl.BlockSpec((BAND, aw), lambda b: (b, 0))
    prev = pl.BlockSpec((BAND, aw), lambda b: (jnp.maximum(b - 1, 0), 0))
    return pl.pallas_call(
        body, name=name, grid=(nb,), in_specs=[cur, prev, cur, prev, cur], out_specs=[cur, pl.BlockSpec((BAND, HEAD_DIM), lambda b: (b, 0))],
        out_shape=[jax.ShapeDtypeStruct((s, aw), BF16), jax.ShapeDtypeStruct((s, HEAD_DIM), F32)],
        compiler_params=_cparams(("arbitrary",)),
    )(q, k, k, v, v)


def _attn_bwd(name, q, k, v, do, lse, delta, bps, dil):
    s, aw = q.shape
    n_heads, nb = aw // HEAD_DIM, s // BAND
    scale = HEAD_DIM**-0.5
    slopes = _slopes(n_heads)

    def body(qc_ref, qn_ref, doc_ref, don_ref, lc_ref, ln_ref, dc_ref, dn_ref, k_ref, v_ref, dq_ref, dk_ref, dv_ref, carry):
        j = pl.program_id(0)

        @pl.when(j == 0)
        def _():
            carry[...] = jnp.zeros(carry.shape, F32)

        has_next = jnp.logical_and((j + 1) % bps != 0, j + 1 < nb)
        ri, ci = _band_iotas()
        dist_c = (ri - ci).astype(F32)
        dist_p = (ri - ci + BAND).astype(F32)
        valid_c = ci <= ri
        valid_n = ci >= ri + jnp.where(has_next, 0, BAND)
        carried = carry[...]
        hs = range(n_heads)
        sls = [slice(h * HEAD_DIM, (h + 1) * HEAD_DIM) for h in hs]
        ks, vs = [k_ref[:, sl] for sl in sls], [v_ref[:, sl] for sl in sls]
        sides = [(qc_ref, doc_ref, lc_ref, dc_ref, dist_c, valid_c), (qn_ref, don_ref, ln_ref, dn_ref, dist_p, valid_n)]
        res = []
        for q_ref, do_ref, l_ref, d_ref, dist, valid in sides:
            qs, dos = [q_ref[:, sl] for sl in sls], [do_ref[:, sl] for sl in sls]
            sc = [lax.dot_general(qs[h], ks[h], _NT, preferred_element_type=F32) for h in hs]
            dp = [lax.dot_general(dos[h], vs[h], _NT, preferred_element_type=F32) for h in hs]
            lse_v, delta_v = l_ref[...], d_ref[...]
            p = [jnp.where(valid, jnp.exp(sc[h] * scale - (slopes[h] * dil) * dist - _from_lanes(lse_v, h, BAND)), 0.0) for h in hs]
            ds = [(p[h] * (dp[h] - _from_lanes(delta_v, h, BAND))).astype(BF16) for h in hs]
            pb = [p[h].astype(BF16) for h in hs]
            dv = [lax.dot_general(pb[h], dos[h], _TN, preferred_element_type=F32) for h in hs]
            dk = [lax.dot_general(ds[h], qs[h], _TN, preferred_element_type=F32) for h in hs]
            dq = [jnp.dot(ds[h], ks[h], preferred_element_type=F32) for h in hs]
            res.append((dq, dk, dv))
        (dq_c, dk_c, dv_c), (dq_n, dk_n, dv_n) = res
        dq_ref[...] = (carried + jnp.concatenate(dq_c, axis=1) * scale).astype(dq_ref.dtype)
        carry[...] = jnp.concatenate(dq_n, axis=1) * scale
        dk_ref[...] = (jnp.concatenate([dk_c[h] + dk_n[h] for h in hs], axis=1) * scale).astype(dk_ref.dtype)
        dv_ref[...] = jnp.concatenate([dv_c[h] + dv_n[h] for h in hs], axis=1).astype(dv_ref.dtype)

    cur = pl.BlockSpec((BAND, aw), lambda b: (b, 0))
    nxt = pl.BlockSpec((BAND, aw), lambda b: (jnp.minimum(b + 1, nb - 1), 0))
    cur1 = pl.BlockSpec((BAND, HEAD_DIM), lambda b: (b, 0))
    nxt1 = pl.BlockSpec((BAND, HEAD_DIM), lambda b: (jnp.minimum(b + 1, nb - 1), 0))
    return pl.pallas_call(
        body, name=name, grid=(nb,), in_specs=[cur, nxt, cur, nxt, cur1, nxt1, cur1, nxt1, cur, cur], out_specs=[cur] * 3,
        out_shape=[jax.ShapeDtypeStruct((s, aw), BF16)] * 3, scratch_shapes=[pltpu.VMEM((BAND, aw), F32)],
        compiler_params=_cparams(("arbitrary",)),
    )(q, q, do, do, lse, lse, delta, delta, k, v)


STREAM_ROWS = BAND * max(dil for _, dil in DILATION_PATTERNS)


def _stream_orders(name, fn, srcs, vecs, n_vals, n_heads):
    s = srcs[0][0].shape[0]
    dils = [dil for _, dil in DILATION_PATTERNS]
    n_in = len(srcs) + len(vecs)

    def body(*refs):
        outs, scr = refs[n_in:-1], refs[-1]
        for vi, val in enumerate(fn(*[r[...] for r in refs[:n_in]])):
            scr[...] = val
            for di, dil in enumerate(dils):
                o = outs[vi * len(dils) + di]
                if dil == 1:
                    o[...] = val.astype(o.dtype)
                    continue
                for part in range(STREAM_ROWS // (dil * BAND)):
                    for r in range(dil):
                        rows = scr[pl.ds(part * dil * BAND + r, BAND, stride=dil), :]
                        o[r, part * BAND : (part + 1) * BAND, :] = rows.astype(o.dtype)

    in_specs = [pl.BlockSpec((STREAM_ROWS, HEAD_DIM), lambda b, h, cb=cb: (b, cb + h)) for (_, cb) in srcs]
    in_specs += [pl.BlockSpec(v.shape, lambda b, h, nd=v.ndim: (0,) * nd) for v in vecs]
    out_specs, out_shape = [], []
    for _ in range(n_vals):
        for dil in dils:
            if dil == 1:
                out_specs.append(pl.BlockSpec((STREAM_ROWS, HEAD_DIM), lambda b, h: (b, h)))
                out_shape.append(jax.ShapeDtypeStruct((s, n_heads * HEAD_DIM), BF16))
            else:
                out_specs.append(pl.BlockSpec((dil, STREAM_ROWS // dil, HEAD_DIM), lambda b, h: (0, b, h)))
                out_shape.append(jax.ShapeDtypeStruct((dil, s // dil, n_heads * HEAD_DIM), BF16))
    res = pl.pallas_call(
        body, name=name, grid=(s // STREAM_ROWS, n_heads), in_specs=in_specs, out_specs=out_specs, out_shape=out_shape,
        scratch_shapes=[pltpu.VMEM((STREAM_ROWS, HEAD_DIM), F32)], compiler_params=_cparams(("arbitrary", "arbitrary")),
    )(*[a for (a, _) in srcs], *vecs)
    res = [r.reshape(s, n_heads * HEAD_DIM) for r in res]
    return [res[vi * len(dils) : (vi + 1) * len(dils)] for vi in range(n_vals)]


def _natural_order(name, fn, vals, extras, out_dtypes, n_heads):
    s = vals[0][0].shape[0]
    dils = [dil for _, dil in DILATION_PATTERNS]
    n_in = len(vals) * len(dils) + len(extras)

    def body(*refs):
        outs, scr = refs[n_in:-1], refs[-1]
        blocks = []
        for vi in range(len(vals)):
            per_pattern = []
            for di, dil in enumerate(dils):
                x = refs[vi * len(dils) + di]
                if dil == 1:
                    per_pattern.append(x[...].astype(F32))
                    continue
                for part in range(STREAM_ROWS // (dil * BAND)):
                    for r in range(dil):
                        scr[pl.ds(part * dil * BAND + r, BAND, stride=dil), :] = x[r, part * BAND : (part + 1) * BAND, :].astype(F32)
                per_pattern.append(scr[...])
            blocks.append(per_pattern)
        res = fn(pl.program_id(1), blocks, *[r[...] for r in refs[len(vals) * len(dils) : n_in]])
        for o, v in zip(outs, res):
            o[...] = v.astype(o.dtype)

    in_specs, ins = [], []
    for per_pattern in vals:
        for arr, dil in zip(per_pattern, dils):
            if dil == 1:
                in_specs.append(pl.BlockSpec((STREAM_ROWS, HEAD_DIM), lambda b, h: (b, h)))
                ins.append(arr)
            else:
                in_specs.append(pl.BlockSpec((dil, STREAM_ROWS // dil, HEAD_DIM), lambda b, h: (0, b, h)))
                ins.append(arr.reshape(dil, s // dil, n_heads * HEAD_DIM))
    in_specs += [pl.BlockSpec((STREAM_ROWS, HEAD_DIM), lambda b, h: (b, 0)) for _ in extras]
    return pl.pallas_call(
        body, name=name, grid=(s // STREAM_ROWS, n_heads), in_specs=in_specs,
        out_specs=[pl.BlockSpec((STREAM_ROWS, HEAD_DIM), lambda b, h: (b, h)) for _ in out_dtypes],
        out_shape=[jax.ShapeDtypeStruct((s, n_heads * HEAD_DIM), dt) for dt in out_dtypes],
        scratch_shapes=[pltpu.VMEM((STREAM_ROWS, HEAD_DIM), F32)], compiler_params=_cparams(("arbitrary", "arbitrary")),
    )(*ins, *extras)


def _to_streams(v, dil):
    if dil == 1:
        return v
    s, w = v.shape
    return v.reshape(s // dil, dil, w).transpose(1, 0, 2).reshape(s, w)


def _from_streams(v, dil):
    if dil == 1:
        return v
    s, w = v.shape
    return v.reshape(dil, s // dil, w).transpose(1, 0, 2).reshape(s, w)


SCAN_LANES = 1024
SSM_ROWS = 128
SSM_FWD_ROWS = 256
FF_TM = 1024


def _cmul(ar, ai, br, bi):
    return ar * br - ai * bi, ar * bi + ai * br


def _cmul_conj(ar, ai, br, bi):
    return ar * br + ai * bi, ar * bi - ai * br


def _ssm_fwd(u_src, bre, bim, cre, cim, consts, d_skip):
    u_arr, sw, ucb = u_src
    s = u_arr.shape[0]
    nblk = bre.shape[0]
    nst = nblk * 512
    t = _blk(s, SSM_FWD_ROWS)
    w = _blk(nst, SCAN_LANES)

    def body(u_ref, bre_ref, bim_ref, cre_ref, cim_ref, k_ref, ds_ref, y_ref, hr_ref, hi_ref, xr, xi, car_r, car_i):
        @pl.when(pl.program_id(0) == 0)
        def _():
            car_r[...] = jnp.zeros(car_r.shape, F32)
            car_i[...] = jnp.zeros(car_i.shape, F32)

        u = u_ref[...]
        ub = u.astype(BF16)
        for b in range(nblk):
            ch, st = slice(b * 128, (b + 1) * 128), slice(b * 512, (b + 1) * 512)
            xr[:, st] = jnp.dot(ub[:, ch], bre_ref[b], preferred_element_type=F32)
            xi[:, st] = jnp.dot(ub[:, ch], bim_ref[b], preferred_element_type=F32)
        for c in range(nst // w):
            ls = slice(c * w, (c + 1) * w)
            mult = [k_ref[i, :, ls] for i in range(8)]

            def step(i, carry, ls=ls, mult=mult):
                cr, ci = carry
                r0 = pl.multiple_of(i * SUBLANES, SUBLANES)
                hr, hi = xr[pl.ds(r0, SUBLANES), ls], xi[pl.ds(r0, SUBLANES), ls]
                for n, sh in enumerate((1, 2, 4)):
                    pr, pi = _cmul(mult[2 * n], mult[2 * n + 1], pltpu.roll(hr, sh, 0), pltpu.roll(hi, sh, 0))
                    hr, hi = hr + pr, hi + pi
                pr, pi = _cmul(mult[6], mult[7], cr, ci)
                hr, hi = hr + pr, hi + pi
                xr[pl.ds(r0, SUBLANES), ls] = hr
                xi[pl.ds(r0, SUBLANES), ls] = hi
                return jnp.broadcast_to(hr[7:8, :], hr.shape), jnp.broadcast_to(hi[7:8, :], hi.shape)

            cr, ci = lax.fori_loop(0, t // SUBLANES, step, (car_r[:, ls], car_i[:, ls]))
            car_r[:, ls] = cr
            car_i[:, ls] = ci
        hrb, hib = xr[...].astype(BF16), xi[...].astype(BF16)
        hr_ref[...] = hrb
        hi_ref[...] = hib
        for b in range(nblk):
            ch, st = slice(b * 128, (b + 1) * 128), slice(b * 512, (b + 1) * 512)
            yb = jnp.dot(hrb[:, st], cre_ref[b], preferred_element_type=F32)
            yb -= jnp.dot(hib[:, st], cim_ref[b], preferred_element_type=F32)
            y_ref[:, ch] = yb + ds_ref[:, ch] * u[:, ch]

    whole = lambda a: pl.BlockSpec(a.shape, lambda i, nd=a.ndim: (0,) * nd)
    return pl.pallas_call(
        body, name="ssm_fwd", grid=(s // t,),
        in_specs=[pl.BlockSpec((t, sw), lambda i: (i, ucb))] + [whole(a) for a in (bre, bim, cre, cim, consts, d_skip)],
        out_specs=[pl.BlockSpec((t, sw), lambda i: (i, 0)), pl.BlockSpec((t, nst), lambda i: (i, 0)), pl.BlockSpec((t, nst), lambda i: (i, 0))],
        out_shape=[jax.ShapeDtypeStruct((s, sw), F32), jax.ShapeDtypeStruct((s, nst), BF16), jax.ShapeDtypeStruct((s, nst), BF16)],
        scratch_shapes=[pltpu.VMEM((t, nst), F32), pltpu.VMEM((t, nst), F32), pltpu.VMEM((SUBLANES, nst), F32), pltpu.VMEM((SUBLANES, nst), F32)],
        compiler_params=_cparams(("arbitrary",)),
    )(u_arr, bre, bim, cre, cim, consts, d_skip)


def _ssm_bwd(dy, u_src, hr, hi, ctre, ctim, btre, btim, consts, d_skip):
    u_arr, sw, ucb = u_src
    s = dy.shape[0]
    nblk = ctre.shape[0]
    nst = nblk * 512
    t = _blk(s, SSM_ROWS)
    w = _blk(nst, SCAN_LANES)
    nt = s // t

    def body(dy_ref, u_ref, hr_ref, hi_ref, ctre_ref, ctim_ref, btre_ref, btim_ref, k_ref, ds_ref,
             du_ref, db_ref, dc_ref, da_ref, dd_ref, lr, li, hfr, hfi, car_r, car_i):
        @pl.when(pl.program_id(0) == 0)
        def _():
            car_r[...] = jnp.zeros(car_r.shape, F32)
            car_i[...] = jnp.zeros(car_i.shape, F32)
            db_ref[...] = jnp.zeros(db_ref.shape, F32)
            dc_ref[...] = jnp.zeros(dc_ref.shape, F32)
            da_ref[...] = jnp.zeros(da_ref.shape, F32)
            dd_ref[...] = jnp.zeros(dd_ref.shape, F32)

        dyv, u = dy_ref[...].astype(F32), u_ref[...]
        dyb, ub = dyv.astype(BF16), u.astype(BF16)
        hrb, hib = hr_ref[...], hi_ref[...]
        hfr[...] = hrb.astype(F32)
        hfi[...] = hib.astype(F32)
        for b in range(nblk):
            ch, st = slice(b * 128, (b + 1) * 128), slice(b * 512, (b + 1) * 512)
            lr[:, st] = jnp.dot(dyb[:, ch], ctre_ref[b], preferred_element_type=F32)
            li[:, st] = -jnp.dot(dyb[:, ch], ctim_ref[b], preferred_element_type=F32)
        last_row = lax.broadcasted_iota(jnp.int32, (SUBLANES, w), 0) == SUBLANES - 1
        for c in range(nst // w):
            ls = slice(c * w, (c + 1) * w)
            mult = [k_ref[i, :, ls] for i in range(8)]

            def step(i, carry, ls=ls, mult=mult):
                cr, ci, ar, ai = carry
                r0 = pl.multiple_of((t // SUBLANES - 1 - i) * SUBLANES, SUBLANES)
                gr, gi = lr[pl.ds(r0, SUBLANES), ls], li[pl.ds(r0, SUBLANES), ls]
                for n, sh in enumerate((1, 2, 4)):
                    pr, pi = _cmul_conj(mult[2 * n], mult[2 * n + 1], pltpu.roll(gr, SUBLANES - sh, 0), pltpu.roll(gi, SUBLANES - sh, 0))
                    gr, gi = gr + pr, gi + pi
                pr, pi = _cmul_conj(mult[6], mult[7], cr, ci)
                gr, gi = gr + pr, gi + pi
                lr[pl.ds(r0, SUBLANES), ls] = gr
                li[pl.ds(r0, SUBLANES), ls] = gi
                nr = jnp.where(last_row, cr, pltpu.roll(gr, SUBLANES - 1, 0))
                ni = jnp.where(last_row, ci, pltpu.roll(gi, SUBLANES - 1, 0))
                fr, fi = hfr[pl.ds(r0, SUBLANES), ls], hfi[pl.ds(r0, SUBLANES), ls]
                pr, pi = _cmul_conj(fr, fi, nr, ni)
                return (jnp.broadcast_to(gr[0:1, :], gr.shape), jnp.broadcast_to(gi[0:1, :], gi.shape), ar + pr, ai + pi)

            zero = jnp.zeros((SUBLANES, w), F32)
            cr, ci, ar, ai = lax.fori_loop(0, t // SUBLANES, step, (car_r[:, ls], car_i[:, ls], zero, zero))
            car_r[:, ls] = cr
            car_i[:, ls] = ci
            da_ref[0, :, ls] += ar
            da_ref[1, :, ls] += ai
        lrb, lib = lr[...].astype(BF16), li[...].astype(BF16)
        for b in range(nblk):
            ch, st = slice(b * 128, (b + 1) * 128), slice(b * 512, (b + 1) * 512)
            dub = jnp.dot(lrb[:, st], btre_ref[b], preferred_element_type=F32)
            dub += jnp.dot(lib[:, st], btim_ref[b], preferred_element_type=F32)
            du_ref[:, ch] = (dub + ds_ref[:, ch] * dyv[:, ch]).astype(du_ref.dtype)
            db_ref[0, b] += lax.dot_general(ub[:, ch], lrb[:, st], _TN, preferred_element_type=F32)
            db_ref[1, b] += lax.dot_general(ub[:, ch], lib[:, st], _TN, preferred_element_type=F32)
            dc_ref[0, b] += lax.dot_general(dyb[:, ch], hrb[:, st], _TN, preferred_element_type=F32)
            dc_ref[1, b] -= lax.dot_general(dyb[:, ch], hib[:, st], _TN, preferred_element_type=F32)
        dd_ref[...] += _colsum8(dyv * u)

    whole = lambda a: pl.BlockSpec(a.shape, lambda i, nd=a.ndim: (0,) * nd)
    rev = lambda wd, cb=0: pl.BlockSpec((t, wd), lambda i, cb=cb: (nt - 1 - i, cb))
    acc = lambda sh: pl.BlockSpec(sh, lambda i, nd=len(sh): (0,) * nd)
    db_shape, da_shape, dd_shape = (2, nblk, 128, 512), (2, SUBLANES, nst), (SUBLANES, sw)
    return pl.pallas_call(
        body, name="ssm_bwd", grid=(nt,),
        in_specs=[rev(sw), rev(sw, ucb), rev(nst), rev(nst)] + [whole(a) for a in (ctre, ctim, btre, btim, consts, d_skip)],
        out_specs=[rev(sw), acc(db_shape), acc(db_shape), acc(da_shape), acc(dd_shape)],
        out_shape=[jax.ShapeDtypeStruct((s, sw), BF16), jax.ShapeDtypeStruct(db_shape, F32), jax.ShapeDtypeStruct(db_shape, F32),
                   jax.ShapeDtypeStruct(da_shape, F32), jax.ShapeDtypeStruct(dd_shape, F32)],
        scratch_shapes=[pltpu.VMEM((t, nst), F32)] * 4 + [pltpu.VMEM((SUBLANES, nst), F32)] * 2,
        compiler_params=_cparams(("arbitrary",)),
    )(dy, u_arr, hr, hi, ctre, ctim, btre, btim, consts, d_skip)


def _ssm_discretise(lam_re, lam_im, log_step):
    step = jnp.exp(log_step)
    e = jnp.exp(lam_re * step)
    ar, ai = e * jnp.cos(lam_im * step), e * jnp.sin(lam_im * step)
    den = lam_re * lam_re + lam_im * lam_im
    inv_r, inv_i = lam_re / den, -lam_im / den
    fr, fi = _cmul(ar - 1.0, ai, inv_r, inv_i)
    return step, ar, ai, inv_r, inv_i, fr, fi


def _ssm_prep_fn(lam_re, lam_im, log_step):
    _, ar, ai, _, _, fr, fi = _ssm_discretise(lam_re, lam_im, log_step)
    pw = [(ar, ai)]
    for n in range(1, 8):
        pw.append(_cmul(*pw[n - 1], ar, ai))
    return [p[0] for p in pw] + [p[1] for p in pw] + [fr, fi]


def _ssm_param_grad_fn(lam_re, lam_im, log_step, da_r, da_i, df_r, df_i):
    step, ar, ai, inv_r, inv_i, fr, fi = _ssm_discretise(lam_re, lam_im, log_step)
    pr, pi = _cmul_conj(inv_r, inv_i, df_r, df_i)
    dat_r, dat_i = da_r + pr, da_i + pi
    wr, wi = _cmul(fr, fi, inv_r, inv_i)
    dl_r, dl_i = _cmul_conj(-wr, -wi, df_r, df_i)
    dz_r, dz_i = _cmul_conj(ar, ai, dat_r, dat_i)
    dl_r, dl_i = dl_r + step * dz_r, dl_i + step * dz_i
    dstep = jnp.sum(dz_r * lam_re + dz_i * lam_im, axis=-1, keepdims=True)
    return [dl_r, dl_i, step * dstep]


def _block_diag(v, rows_first):
    g, a, b = v.shape
    nb = g // GROUPS_PER_BLOCK
    eye = jnp.eye(GROUPS_PER_BLOCK, dtype=v.dtype)
    v = v.reshape(nb, GROUPS_PER_BLOCK, a, 1, b) * eye[None, :, None, :, None]
    return v.reshape(nb, GROUPS_PER_BLOCK * a, GROUPS_PER_BLOCK * b)


def _block_diag_take(v, a, b):
    nb = v.shape[0]
    v = v.reshape(nb, GROUPS_PER_BLOCK, a, GROUPS_PER_BLOCK, b)
    return jnp.stack([v[:, g, :, g, :] for g in range(GROUPS_PER_BLOCK)], axis=1).reshape(nb * GROUPS_PER_BLOCK, a, b)


def _adamw(w, g, m, v):
    m = ADAM_B1 * m + (1.0 - ADAM_B1) * g
    v = ADAM_B2 * v + (1.0 - ADAM_B2) * (g * g)
    m_hat = m / (1.0 - ADAM_B1**ADAM_STEP)
    v_hat = v / (1.0 - ADAM_B2**ADAM_STEP)
    delta = -ADAM_LR * (m_hat / (jnp.sqrt(v_hat) + ADAM_EPS) + ADAM_WD * w)
    return delta, m, v


def _adamw_rows(name, w, g, m, v):
    wd = w.shape[1]
    return _rowwise(name, lambda a, b, c, d: (list(_adamw(a, b, c, d)), []), [w, g, m, v], [], [(wd, F32)] * 3, [], tm=128)


SMALL = ["b_ada", "norm1_g", "q_norm_g", "k_norm_g", "lam_re", "lam_im", "log_step", "b_re", "b_im", "c_re", "c_im",
         "d_skip", "b_glu", "attn_out_g", "ssm_out_g", "norm2_g"]
LARGE = ["w_in", "w_glu", "w_out", "w_ff1", "w_ff2"]
ORDER = ["w_ada", "b_ada", "norm1_g", "w_in", "q_norm_g", "k_norm_g", "lam_re", "lam_im", "log_step", "b_re", "b_im", "c_re",
         "c_im", "d_skip", "w_glu", "b_glu", "attn_out_g", "ssm_out_g", "w_out", "norm2_g", "w_ff1", "w_ff2"]


def _pack(arrs):
    flat = jnp.concatenate([a.reshape(-1) for a in arrs])
    rows = -(-flat.shape[0] // PACK_LANES)
    rows = -(-rows // SUBLANES) * SUBLANES
    return jnp.pad(flat, (0, rows * PACK_LANES - flat.shape[0])).reshape(rows, PACK_LANES)


def _unpack(packed, shapes):
    flat, out, o = packed.reshape(-1), [], 0
    for sh in shapes:
        n = int(np.prod(sh))
        out.append(flat[o : o + n].reshape(sh))
        o += n
    return out


def kernel(x, c, w_ada, b_ada, norm1_g, w_in, q_norm_g, k_norm_g, lam_re, lam_im, log_step, b_re, b_im, c_re, c_im, d_skip, w_glu, b_glu, attn_out_g, ssm_out_g, w_out, norm2_g, w_ff1, w_ff2, loss_target, m_w_ada, m_b_ada, m_norm1_g, m_w_in, m_q_norm_g, m_k_norm_g, m_lam_re, m_lam_im, m_log_step, m_b_re, m_b_im, m_c_re, m_c_im, m_d_skip, m_w_glu, m_b_glu, m_attn_out_g, m_ssm_out_g, m_w_out, m_norm2_g, m_w_ff1, m_w_ff2, v_w_ada, v_b_ada, v_norm1_g, v_w_in, v_q_norm_g, v_k_norm_g, v_lam_re, v_lam_im, v_log_step, v_b_re, v_b_im, v_c_re, v_c_im, v_d_skip, v_w_glu, v_b_glu, v_attn_out_g, v_ssm_out_g, v_w_out, v_norm2_g, v_w_ff1, v_w_ff2):
    args = dict(locals())
    wts = {n: args[n] for n in ORDER}
    mom = {n: args["m_" + n] for n in ORDER}
    var = {n: args["v_" + n] for n in ORDER}

    x2, tgt = x[0], loss_target[0]
    s, d = x2.shape
    aw = d // 2
    sw = d - aw
    n_groups = sw // SSM_GROUP
    nst = n_groups * STATE_DIM
    chip = 2 * lax.axis_index("x") + lax.axis_index("y")

    chip_idx = jnp.reshape(chip, (1,)).astype(jnp.int32)
    core_idx = jnp.reshape(lax.axis_index("c"), (1,)).astype(jnp.int32)

    def cast_into_stack(n):
        w2 = wts[n][0]
        rows, cols = w2.shape
        tr = _blk(rows, 256)
        return _indexed("cast_" + n, lambda a: a, chip_idx, [(w2, (tr, cols), lambda r, ix: (r, 0))],
                        (((4, rows, cols), BF16), (None, tr, cols), lambda r, ix: (ix[0], r, 0)), (rows // tr,))

    stacks = {n: cast_into_stack(n) for n in LARGE}
    dff = 4 * wts["w_ff2"].shape[1]

    c_all = _allgather8("gather_c", jnp.pad(c, ((0, SUBLANES - 1), (0, 0))))[::SUBLANES]
    n_mod = w_ada.shape[2]
    b_ada_mine = lax.dynamic_slice(b_ada, (0, chip * n_mod), (1, n_mod))
    sil_c = _single("silu_c", lambda v: [v * _sigmoid(v)], [c_all], [(c_all.shape, F32)])[0]
    mod_part = _mm("ada_fwd", sil_c, w_ada[0], "nn", F32, tn=512, epi=lambda acc, b: acc + b[0:1, :],
                   extras=[jnp.broadcast_to(b_ada_mine, (SUBLANES, n_mod))])
    mod_all = _allgather8("gather_mod", mod_part)
    me = 2 * chip + lax.axis_index("c")
    mod = jnp.concatenate([lax.dynamic_slice(mod_all, (16 * k + me, 0), (1, n_mod)) for k in range(4)], axis=1)
    sh1, sc1, g1, sh2, sc2, g2 = [mod[:, i * d : (i + 1) * d] for i in range(6)]

    (w_in_s,) = _gather_weights([stacks["w_in"]], mod_all)

    def gather_plan(refs):
        px0, py0, pc, chips = _place()
        slot = 2 * px0 + py0
        return [(r.at[slot], r.at[slot], (px, py, pc), r.at[2 * px + py]) for r in refs for (px, py) in chips]

    mix_sems_s, mix_sems_r, mix_stacks, mix_token = _split_start(
        "gather_mix_start", [stacks["w_glu"], stacks["w_out"]], gather_plan, 6, extra=[w_in_s])
    w_in_s = mix_stacks.pop()
    ff_sems_s, ff_sems_r, ff_stacks, ff_token = _split_start(
        "gather_ff_start", [stacks["w_ff1"], stacks["w_ff2"]], gather_plan, 6, extra=[w_in_s])
    w_in_s = ff_stacks.pop()
    sc1 = sc1 + (mix_token[0:1, 0:1] + ff_token[0:1, 0:1])

    def norm_mod(v, g, sc, sh):
        return [v * _rstd(v) * g * (1.0 + sc) + sh], []

    (h1b,) = _rowwise("norm1", norm_mod, [x2], [norm1_g, sc1, sh1], [(d, BF16)], [])
    proj = _mm("proj_in", h1b, w_in_s, "nn", F32, b_cs=True, tm=FF_TM)
    assert aw == sw
    q_src, k_src, v_src, u_src = [(proj, aw, i) for i in range(3)] + [(proj, sw, 3)]

    n_heads = aw // HEAD_DIM
    qk_orders = _stream_orders("qk_prep", lambda q, k, v, gq, gk: [q * _rstd(q) * gq, k * _rstd(k) * gk, v],
                               [(proj, 0), (proj, n_heads), (proj, 2 * n_heads)], [q_norm_g, k_norm_g], 3, n_heads)

    pats = []
    for pi, (window, dil) in enumerate(DILATION_PATTERNS):
        assert window // dil == BAND and s % STREAM_ROWS == 0
        bps = s // dil // BAND
        qp, kp, vp = [order[pi] for order in qk_orders]
        o_p, l_p = _attn_fwd(f"attn_fwd_d{dil}", qp, kp, vp, bps, dil)
        pats.append((dil, bps, qp, kp, vp, o_p, _from_streams(l_p, dil)))

    def lse_total(l1, l2, l3):
        m = jnp.maximum(jnp.maximum(l1, l2), l3)
        return m + jnp.log(jnp.exp(l1 - m) + jnp.exp(l2 - m) + jnp.exp(l3 - m))

    (lse,) = _rowwise("attn_lse", lambda l1, l2, l3: ([lse_total(l1, l2, l3)], []), [p[6] for p in pats], [], [(HEAD_DIM, F32)], [])

    def combine(h, blocks, l1, l2, l3):
        tot = lse_total(l1, l2, l3)
        lane = lax.broadcasted_iota(jnp.int32, l1.shape, 1)
        mix = 0.0
        for o_blk, l in zip(blocks[0], (l1, l2, l3)):
            w = jnp.sum(jnp.where(lane == h, jnp.exp(l - tot), 0.0), axis=1, keepdims=True)
            mix = mix + w * o_blk
        return [mix]

    (attn,) = _natural_order("attn_mix", combine, [[p[5] for p in pats]], [p[6] for p in pats], [F32], n_heads)

    lam_re2, lam_im2, log_step2 = lam_re[0], lam_im[0], log_step[0].reshape(n_groups, 1)
    prep = _single("ssm_prep", _ssm_prep_fn, [lam_re2, lam_im2, log_step2], [((n_groups, STATE_DIM), F32)] * 18)
    p_r = jnp.stack([p.reshape(nst) for p in prep[:8]])
    p_i = jnp.stack([p.reshape(nst) for p in prep[8:16]])
    f_ri = jnp.stack(prep[16:])
    row = jnp.arange(SUBLANES)[:, None]
    zero = jnp.zeros((SUBLANES, nst), F32)
    sel = lambda cond, v: jnp.where(cond, jnp.broadcast_to(v[None, :], (SUBLANES, nst)), zero)
    fwd_consts = jnp.stack([sel(row >= 1, p_r[0]), sel(row >= 1, p_i[0]), sel(row >= 2, p_r[1]), sel(row >= 2, p_i[1]),
                            sel(row >= 4, p_r[3]), sel(row >= 4, p_i[3]), p_r, p_i])
    bwd_consts = jnp.stack([sel(row <= 6, p_r[0]), sel(row <= 6, p_i[0]), sel(row <= 5, p_r[1]), sel(row <= 5, p_i[1]),
                            sel(row <= 3, p_r[3]), sel(row <= 3, p_i[3]), p_r[::-1], p_i[::-1]])
    nblk = n_groups // GROUPS_PER_BLOCK
    f_rows = f_ri.reshape(2, nblk, 1, 512)
    braw_r = _block_diag(b_re[0].transpose(0, 2, 1), True)
    braw_i = _block_diag(b_im[0].transpose(0, 2, 1), True)

    def bbar_fn(br, bi, f):
        r, i = _cmul(f[0], f[1], br, bi)
        return [r, i]

    bbar_r, bbar_i = _single("ssm_bbar", bbar_fn, [braw_r, braw_i, f_rows], [(braw_r.shape, BF16)] * 2)
    ct_r, ct_i = _block_diag(c_re[0], True).astype(BF16), _block_diag(c_im[0], True).astype(BF16)
    ds2 = d_skip
    y_ssm, h_r, h_i = _ssm_fwd(u_src, bbar_r, bbar_i, ct_r.transpose(0, 2, 1), ct_i.transpose(0, 2, 1), fwd_consts, ds2)

    (yb,) = _rowwise("gelu", lambda v: ([_gelu(v)], []), [y_ssm], [], [(sw, BF16)], [])
    w_glu_s, w_out_s = _split_wait("gather_mix_wait", mix_stacks, mix_sems_s, mix_sems_r, yb, gather_plan)
    w_glu_f = w_glu_s.reshape(sw, sw)
    w_out_f = w_out_s.reshape(d, d)
    z = _mm("glu_fwd", yb, w_glu_f, "nn", BF16)

    def glu(ys, zz, b):
        return [_gelu(ys) * _sigmoid(zz.astype(F32) + b)], []

    (ssm,) = _rowwise("glu", glu, [y_ssm, z], [b_glu], [(sw, F32)], [])

    def cat_norm(a, sm, ga, gs):
        return [jnp.concatenate([a * _rstd(a) * ga, sm * _rstd(sm) * gs], axis=1)], []

    (catb,) = _rowwise("cat_norm", cat_norm, [attn, ssm], [attn_out_g, ssm_out_g], [(d, BF16)], [])
    mixed = _mm("mix_out", catb, w_out_f, "nn", F32, tm=FF_TM)

    def resid_norm(xv, mx, g1v, g, sc, sh):
        x1v = xv + g1v * mx
        return [x1v, x1v * _rstd(x1v) * g * (1.0 + sc) + sh], []

    x1, h2b = _rowwise("resid_norm2", resid_norm, [x2, mixed], [g1, norm2_g, sc2, sh2], [(d, F32), (d, BF16)], [])
    w_ff1_s, w_ff2_s = _split_wait("gather_ff_wait", ff_stacks, ff_sems_s, ff_sems_r, x1, gather_plan)
    w_ff2_f = w_ff2_s.reshape(dff, d)
    a_ff = _mm("ff1", h2b, w_ff1_s, "nn", BF16, b_cs=True, tm=FF_TM)
    sq_relu = lambda v: jnp.square(jnp.maximum(v.astype(F32), 0.0)).astype(BF16)
    ff = _mm("ff2", a_ff, w_ff2_f, "nn", F32, a_pro=sq_relu, tm=FF_TM)

    def loss_fn(x1v, ffv, tg, g2v):
        diff = x1v + g2v * ffv - tg
        dout = diff * (1.0 / d)
        return [dout, dout * g2v], [_colsum8(0.5 * diff * dout), _colsum8(dout * ffv)]

    dout, dffb, loss_acc, dg2_acc = _rowwise("loss", loss_fn, [x1, ff, tgt], [g2], [(d, F32), (d, BF16)], [(SUBLANES, d)] * 2)

    da_ff = _mm("ff2_bwd", dffb, w_ff2_f, "nt", BF16, epi=lambda acc, av: acc * (2.0 * jnp.maximum(av.astype(F32), 0.0)), extras=[a_ff], tm=FF_TM)
    gw = {}
    gw["w_ff2"] = _mm("ff2_wgrad", a_ff, dffb, "tn", BF16, a_pro=sq_relu, tm=FF_TM).reshape(4, dff // 4, d)
    gw["w_ff1"] = _mm("ff1_wgrad", h2b, da_ff, "tn", BF16, o_cs=True, tm=FF_TM)

    def add2(n, g, b):
        _, half, cols = b.shape
        tr = _blk(half, 256)
        nrb = half // tr
        blk = (None, tr, cols)
        return _indexed("grad_add2_" + n, lambda p, q: p.astype(F32) + q.astype(F32), core_idx,
                        [(g, blk, lambda k, r, ix: (k, ix[0] * nrb + r, 0)), (b, blk, lambda k, r, ix: (k, r, 0))],
                        ((b.shape, BF16), blk, lambda k, r, ix: (k, r, 0)), (4, nrb))

    def scatter_plan(refs):
        px0, py0, pc, chips = _place()
        nw = len(refs) // 2
        return [(refs[w].at[2 * px + py], refs[nw + w].at[j], (px, py, pc), refs[nw + w].at[j])
                for w in range(nw) for j, (px, py) in enumerate(chips)]

    early = ["w_ff2", "w_ff1"]
    parts = {n: add2(n, gw[n], b) for n, b in zip(early, _swap_halves([gw[n] for n in early], "swap_halves_ff"))}
    lands = [lax.empty((3,) + parts[n].shape[1:], BF16) for n in early]
    sc_sems_s, sc_sems_r, sc_bufs, sc_token = _split_start("scatter_ff_start", [parts[n] for n in early] + lands, scatter_plan, 6)
    sc2_late = sc2 + sc_token[0:1, 0:1]

    dh2 = _mm("ff1_bwd", da_ff, w_ff1_s, "nt", BF16, b_cs=True, tm=FF_TM)

    def norm2_bwd(dh, x1v, do, mx, g, sc, g1v):
        dh = dh.astype(F32)
        r = _rstd(x1v)
        xh = x1v * r
        dn = dh * (1.0 + sc)
        dx = do + _rms_bwd(dn * g, xh, r)
        return [dx, dx * g1v], [_colsum8(dh * xh * g), _colsum8(dh), _colsum8(dn * xh), _colsum8(dx * mx)]

    dx1, dmixb, dsc2_acc, dsh2_acc, dn2g_acc, dg1_acc = _rowwise(
        "norm2_bwd", norm2_bwd, [dh2, x1, dout, mixed], [norm2_g, sc2_late, g1], [(d, F32), (d, BF16)], [(SUBLANES, d)] * 4)

    dcat = _mm("mix_out_bwd", dmixb, w_out_f, "nt", BF16, tm=FF_TM)
    gw["w_out"] = _mm("mix_out_wgrad", catb, dmixb, "tn", BF16, tm=FF_TM).reshape(4, d // 4, d)

    def cat_bwd(da, dsm, a, sm, ga, gs):
        da, dsm = da.astype(F32), dsm.astype(F32)
        ra, rs = _rstd(a), _rstd(sm)
        ah, sh = a * ra, sm * rs
        return [_rms_bwd(da * ga, ah, ra), _rms_bwd(dsm * gs, sh, rs)], [_colsum8(da * ah), _colsum8(dsm * sh)]

    dattn, dssm, dga_acc, dgs_acc = _rowwise(
        "cat_norm_bwd", cat_bwd, [(dcat, aw, 0), (dcat, sw, 1), attn, ssm], [attn_out_g, ssm_out_g], [(aw, F32), (sw, F32)],
        [(SUBLANES, aw), (SUBLANES, sw)])

    def glu_bwd(dsm, ys, zz, b):
        sg = _sigmoid(zz.astype(F32) + b)
        dz = dsm * _gelu(ys) * sg * (1.0 - sg)
        return [dz, dsm * sg], [_colsum8(dz)]

    dzb, dy_direct, dbglu_acc = _rowwise("glu_bwd", glu_bwd, [dssm, y_ssm, z], [b_glu], [(sw, BF16), (sw, BF16)], [(SUBLANES, sw)])
    gw["w_glu"] = _mm("glu_wgrad", yb, dzb, "tn", BF16).reshape(4, sw // 4, sw)
    dys = _mm("glu_bwd_mm", dzb, w_glu_f, "nt", BF16, epi=lambda acc, direct, ys: (acc + direct.astype(F32)) * _gelu_grad(ys), extras=[dy_direct, y_ssm])
    du, db_bar, dc_bd, da_acc, dds_acc = _ssm_bwd(dys, u_src, h_r, h_i, ct_r, ct_i, bbar_r.transpose(0, 2, 1),
                                                  bbar_i.transpose(0, 2, 1), bwd_consts, ds2)

    def delta_fn(da, a):
        return [_to_lanes([jnp.sum(p * q, axis=1, keepdims=True) for p, q in zip(_heads(da), _heads(a))])], []

    (delta,) = _rowwise("attn_delta", delta_fn, [dattn, attn], [], [(HEAD_DIM, F32)], [])
    (do_orders,) = _stream_orders("attn_do", lambda da: [da], [(dattn, 0)], [], 1, n_heads)
    dqs, dks, dvs = [], [], []
    for pi, (dil, bps, qp, kp, vp, _, _) in enumerate(pats):
        lsp, dlp = [_to_streams(t, dil) for t in (lse, delta)]
        dq_p, dk_p, dv_p = _attn_bwd(f"attn_bwd_d{dil}", qp, kp, vp, do_orders[pi], lsp, dlp, bps, dil)
        dqs.append(dq_p)
        dks.append(dk_p)
        dvs.append(dv_p)
    dq_t, dk_t, dv_t = _natural_order("attn_grad_sum", lambda h, blocks: [(b[0] + b[1]) + b[2] for b in blocks],
                                      [dqs, dks, dvs], [], [BF16] * 3, n_heads)

    def qk_bwd(dq_sum, dk_sum, dv_sum, duv, q, k, gq, gk):
        def one(dn, raw, g):
            outs, acc = [], 0.0
            for dh, h in zip(_heads(dn), _heads(raw)):
                r = _rstd(h)
                hh = h * r
                outs.append(_rms_bwd(dh * g, hh, r))
                acc = acc + _colsum8(dh * hh)
            return jnp.concatenate(outs, axis=1), acc

        dq, gq_acc = one(dq_sum.astype(F32), q, gq)
        dk, gk_acc = one(dk_sum.astype(F32), k, gk)
        return [jnp.concatenate([dq, dk, dv_sum.astype(F32), duv.astype(F32)], axis=1)], [gq_acc, gk_acc]

    dprojb, dgq_acc, dgk_acc = _rowwise("qk_norm_bwd", qk_bwd, [dq_t, dk_t, dv_t, du, q_src, k_src], [q_norm_g, k_norm_g],
                                        [(3 * aw + sw, BF16)], [(SUBLANES, HEAD_DIM)] * 2)
    gw["w_in"] = _mm("proj_in_wgrad", h1b, dprojb, "tn", BF16, o_cs=True, tm=FF_TM)
    late = [n for n in LARGE if n not in early]
    parts.update({n: add2(n, gw[n], b) for n, b in zip(late, _swap_halves([gw[n] for n in late], "swap_halves_rest"))})
    lands_late = [lax.empty((3,) + parts[n].shape[1:], BF16) for n in late]
    rest_sems_s, rest_sems_r, rest_bufs, rest_token = _split_start(
        "scatter_rest_start", [parts[n] for n in late] + lands_late, scatter_plan, 3 * len(late))
    sc1_late = sc1 + rest_token[0:1, 0:1]
    dh1 = _mm("proj_in_bwd", dprojb, w_in_s, "nt", BF16, b_cs=True, tm=FF_TM)

    def norm1_bwd(dh, xv, dxr, g, sc):
        dh = dh.astype(F32)
        r = _rstd(xv)
        xh = xv * r
        dn = dh * (1.0 + sc)
        return [dxr + _rms_bwd(dn * g, xh, r)], [_colsum8(dh * xh * g), _colsum8(dh), _colsum8(dn * xh)]

    grad_x, dsc1_acc, dsh1_acc, dn1g_acc = _rowwise("norm1_bwd", norm1_bwd, [dh1, x2, dx1], [norm1_g, sc1_late], [(d, F32)], [(SUBLANES, d)] * 3)

    def b_grad_fn(dbb, br, bi, f):
        fr, fi = f[0], f[1]
        dbr, dbi = _cmul_conj(fr, fi, dbb[0], dbb[1])
        dfr = jnp.sum(dbb[0] * br + dbb[1] * bi, axis=1, keepdims=True)
        dfi = jnp.sum(dbb[1] * br - dbb[0] * bi, axis=1, keepdims=True)
        return [dbr, dbi, dfr, dfi]

    dbr_bd, dbi_bd, df_r, df_i = _single("ssm_b_grad", b_grad_fn, [db_bar, braw_r, braw_i, f_rows],
                                         [(braw_r.shape, F32)] * 2 + [((nblk, 1, 512), F32)] * 2)

    def sums_fn(*accs):
        return [jnp.sum(a, axis=-2, keepdims=True) for a in accs]

    acc_list = [dsh1_acc, dsc1_acc, dg1_acc, dsh2_acc, dsc2_acc, dg2_acc, dn1g_acc, dgq_acc, dgk_acc, dds_acc, dbglu_acc, dga_acc,
                dgs_acc, dn2g_acc, loss_acc, da_acc]
    sums = _single("small_sums", sums_fn, acc_list, [(a.shape[:-2] + (1, a.shape[-1]), F32) for a in acc_list])
    (dsh1, dsc1, dg1, dsh2, dsc2, dg2, dn1g, dgq, dgk, dds, dbglu, dga, dgs, dn2g, loss_row, da_sum) = sums
    loss_dev = _single("loss_sum", lambda v: [jnp.sum(v, axis=1, keepdims=True)], [loss_row], [((1, 1), F32)])[0]
    gp = (n_groups, STATE_DIM)
    dlam_re, dlam_im, dlog_step = _single(
        "ssm_param_grad", _ssm_param_grad_fn,
        [lam_re2, lam_im2, log_step2, da_sum[0].reshape(gp), da_sum[1].reshape(gp), df_r.reshape(gp), df_i.reshape(gp)],
        [(gp, F32), (gp, F32), ((n_groups, 1), F32)])
    small_dev = {
        "b_ada": jnp.concatenate([dsh1, dsc1, dg1, dsh2, dsc2, dg2], axis=1),
        "norm1_g": dn1g, "q_norm_g": dgq, "k_norm_g": dgk, "lam_re": dlam_re, "lam_im": dlam_im, "log_step": dlog_step,
        "b_re": _block_diag_take(dbr_bd, SSM_GROUP, STATE_DIM).transpose(0, 2, 1),
        "b_im": _block_diag_take(dbi_bd, SSM_GROUP, STATE_DIM).transpose(0, 2, 1),
        "c_re": _block_diag_take(dc_bd[0], SSM_GROUP, STATE_DIM), "c_im": _block_diag_take(dc_bd[1], SSM_GROUP, STATE_DIM),
        "d_skip": dds, "b_glu": dbglu, "attn_out_g": dga, "ssm_out_g": dgs, "norm2_g": dn2g,
    }

    packed = _pack([small_dev[n] for n in SMALL])
    rows = packed.shape[0]
    gathered = _allgather8("gather_small", packed).reshape(8, rows, PACK_LANES)

    def sum8_fn(g):
        acc = g[0]
        for i in range(1, 8):
            acc = acc + g[i]
        return [acc]

    g_small = _single("sum_small", sum8_fn, [gathered], [((rows, PACK_LANES), F32)])[0]
    w_small, m_small, v_small = [_pack([src[n] for n in SMALL]) for src in (wts, mom, var)]
    upd_small = _adamw_rows("adamw_small", w_small, g_small, m_small, v_small)
    shapes = [wts[n].shape for n in SMALL]
    out = {}
    for kind, arr in zip(("grad", "delta", "new_m", "new_v"), (g_small,) + tuple(upd_small)):
        for n, val in zip(SMALL, _unpack(arr, shapes)):
            out[kind, n] = val

    dmod_all = gathered[:, 0 : -(-6 * d // PACK_LANES), :].reshape(8, -1)[:, : 6 * d]
    dmod_mine = lax.dynamic_slice(dmod_all, (0, chip * n_mod), (8, n_mod))
    g_ada = _mm("ada_wgrad", sil_c, dmod_mine, "tn", F32, tm=512, tn=512)
    upd = _adamw_rows("adamw_w_ada", w_ada[0], g_ada, m_w_ada[0], v_w_ada[0])
    for kind, val in zip(("grad", "delta", "new_m", "new_v"), (g_ada,) + tuple(upd)):
        out[kind, "w_ada"] = val[None]

    sc_bufs = _split_wait("scatter_ff_wait", sc_bufs, sc_sems_s, sc_sems_r, grad_x, scatter_plan)
    parts.update(zip(early, sc_bufs[: len(early)]))
    got2 = dict(zip(early, sc_bufs[len(early) :]))
    rest_bufs = _split_wait("scatter_rest_wait", rest_bufs, rest_sems_s, rest_sems_r, g_small, scatter_plan)
    parts.update(zip(late, rest_bufs[: len(late)]))
    got2.update(zip(late, rest_bufs[len(late) :]))

    def add4(n, p, b):
        _, half, cols = p.shape
        tr = _blk(half, 256)
        blk = (None, tr, cols)
        fn = lambda own, q0, q1, q2: ((own.astype(F32) + q0.astype(F32)) + q1.astype(F32)) + q2.astype(F32)
        srcs = [(p, blk, lambda r, ix: (ix[0], r, 0))] + [(b, blk, lambda r, ix, j=j: (j, r, 0)) for j in range(3)]
        return _indexed("grad_add4_" + n, fn, chip_idx, srcs, (((half, cols), F32), (tr, cols), lambda r, ix: (r, 0)), (half // tr,))

    finals = [add4(n, parts[n], got2[n]) for n in LARGE]
    for n, full in zip(LARGE, _join_halves(finals)):
        g = full.reshape(wts[n].shape[1:])
        upd = _adamw_rows("adamw_" + n, wts[n][0], g, mom[n][0], var[n][0])
        for kind, val in zip(("grad", "delta", "new_m", "new_v"), (g,) + tuple(upd)):
            out[kind, n] = val[None]

    loss = lax.psum(loss_dev[0, 0], ("x", "y", "c"))
    return (loss, grad_x[None], *[out[kind, n] for kind in ("grad", "delta", "new_m", "new_v") for n in ORDER])
```

```python
import functools
import math

import jax
import jax.numpy as jnp
import numpy as np
from jax import lax
from jax.experimental import pallas as pl
from jax.experimental.pallas import tpu as pltpu

F32 = jnp.float32
BF16 = jnp.bfloat16
MESH = pl.DeviceIdType.MESH

EPS = 1e-6
HEAD_DIM = 128
SSM_GROUP = 16
STATE_DIM = 64
GROUPS_PER_BLOCK = 8
DILATION_PATTERNS = ((128, 1), (512, 4), (2048, 16))
BAND = 128
NEG = -1e30

ADAM_LR, ADAM_B1, ADAM_B2, ADAM_EPS, ADAM_WD, ADAM_STEP = 0.001, 0.9, 0.999, 1e-08, 0.01, 10

V7X_VMEM_BYTES = 64 * 2**20
VMEM_LIMIT = 48 * 2**20
SUBLANES = 8
PACK_LANES = 1024


def _cparams(sem=None):
    return pltpu.CompilerParams(dimension_semantics=sem, vmem_limit_bytes=VMEM_LIMIT)


def _blk(n, want):
    b = min(n, want)
    while n % b:
        b //= 2
    return b


def _colsum8(v):
    tm, w = v.shape
    return v.reshape(tm // SUBLANES, SUBLANES, w).sum(axis=0)


def _rowwise(name, fn, rows, vecs, out_rows, out_accs, tm=256, n_rows=None):
    rows = [r if isinstance(r, tuple) else (r, r.shape[1], 0) for r in rows]
    rows = [r if len(r) == 4 else r + (0,) for r in rows]
    s = rows[0][0].shape[0] if n_rows is None else n_rows
    tm = _blk(s, tm)
    assert all(r[3] % tm == 0 for r in rows)
    n_in, n_or = len(rows) + len(vecs), len(out_rows)

    def body(*refs):
        outs, accs = fn(*[r[...] for r in refs[:n_in]])
        for r, v in zip(refs[n_in : n_in + n_or], outs):
            r[...] = v.astype(r.dtype)
        if out_accs:
            acc_refs = refs[n_in + n_or :]

            @pl.when(pl.program_id(0) == 0)
            def _():
                for r in acc_refs:
                    r[...] = jnp.zeros(r.shape, r.dtype)

            for r, v in zip(acc_refs, accs):
                r[...] += v

    in_specs = [pl.BlockSpec((tm, w), lambda i, cb=cb, rb=r0 // tm: (i + rb, cb)) for (_, w, cb, r0) in rows]
    in_specs += [pl.BlockSpec(v.shape, lambda i, nd=v.ndim: (0,) * nd) for v in vecs]
    out_specs = [pl.BlockSpec((tm, w), lambda i: (i, 0)) for (w, _) in out_rows]
    out_specs += [pl.BlockSpec(sh, lambda i, nd=len(sh): (0,) * nd) for sh in out_accs]
    out_shape = [jax.ShapeDtypeStruct((s, w), dt) for (w, dt) in out_rows]
    out_shape += [jax.ShapeDtypeStruct(sh, F32) for sh in out_accs]
    res = pl.pallas_call(
        body, name=name, grid=(s // tm,), in_specs=in_specs, out_specs=out_specs, out_shape=out_shape,
        compiler_params=_cparams(("arbitrary",)),
    )(*[r[0] for r in rows], *vecs)
    return res


def _single(name, fn, ins, out_shapes):
    n_in = len(ins)

    def body(*refs):
        outs = fn(*[r[...] for r in refs[:n_in]])
        for r, v in zip(refs[n_in:], outs):
            r[...] = v.astype(r.dtype)

    vm = pl.BlockSpec(memory_space=pltpu.VMEM)
    return pl.pallas_call(
        body, name=name, in_specs=[vm] * n_in, out_specs=[vm] * len(out_shapes),
        out_shape=[jax.ShapeDtypeStruct(sh, dt) for sh, dt in out_shapes], compiler_params=_cparams(),
    )(*ins)


_NN = (((1,), (0,)), ((), ()))
_NT = (((1,), (1,)), ((), ()))
_TN = (((0,), (0,)), ((), ()))


def _mm(name, a, b, mode, out_dtype, *, tm=512, tn=1024, tk=2048, b_cs=False, o_cs=False, epi=None, extras=(), a_pro=None):
    if mode == "tn":
        k, m = a.shape
        n = b.shape[1]
    else:
        m, k = a.shape
        if mode == "nn":
            n = b.shape[2] * 4 if b_cs else b.shape[1]
        else:
            n = b.shape[1] if b_cs else b.shape[0]
    tm = _blk(m, tm)
    tn = _blk(n // 4 if (b_cs and mode == "nn") or o_cs else n, tn)
    tk = _blk(k // 4 if (b_cs and mode == "nt") else k, tk)
    nk = k // tk
    dims = {"nn": _NN, "nt": _NT, "tn": _TN}[mode]
    n_ex = len(extras)

    def body(*refs):
        a_ref, b_ref = refs[0], refs[1]
        ex_refs, o_ref = refs[2 : 2 + n_ex], refs[2 + n_ex]

        def finish(acc):
            if epi is not None:
                acc = epi(acc, *[r[...] for r in ex_refs])
            o_ref[...] = acc.astype(o_ref.dtype)

        a_blk = a_ref[...] if a_pro is None else a_pro(a_ref[...])
        part = lax.dot_general(a_blk, b_ref[...], dims, preferred_element_type=F32)
        if nk == 1:
            finish(part)
        else:
            acc_ref = refs[3 + n_ex]
            kk = pl.program_id(2)

            @pl.when(kk == 0)
            def _():
                acc_ref[...] = part

            @pl.when(kk > 0)
            def _():
                acc_ref[...] += part

            @pl.when(kk == nk - 1)
            def _():
                finish(acc_ref[...])

    a_spec = pl.BlockSpec((tk, tm), lambda i, j, kk: (kk, i)) if mode == "tn" else pl.BlockSpec((tm, tk), lambda i, j, kk: (i, kk))
    if mode == "nn":
        if b_cs:
            per = (n // 4) // tn
            b_spec = pl.BlockSpec((None, tk, tn), lambda i, j, kk: (j // per, kk, j % per))
        else:
            b_spec = pl.BlockSpec((tk, tn), lambda i, j, kk: (kk, j))
    elif mode == "nt":
        if b_cs:
            per = (k // 4) // tk
            b_spec = pl.BlockSpec((None, tn, tk), lambda i, j, kk: (kk // per, j, kk % per))
        else:
            b_spec = pl.BlockSpec((tn, tk), lambda i, j, kk: (j, kk))
    else:
        b_spec = pl.BlockSpec((tk, tn), lambda i, j, kk: (kk, j))
    if o_cs:
        per = (n // 4) // tn
        o_spec = pl.BlockSpec((None, tm, tn), lambda i, j, kk: (j // per, i, j % per))
        o_shape = jax.ShapeDtypeStruct((4, m, n // 4), out_dtype)
    else:
        o_spec = pl.BlockSpec((tm, tn), lambda i, j, kk: (i, j))
        o_shape = jax.ShapeDtypeStruct((m, n), out_dtype)
    ex_specs = [pl.BlockSpec((tm, tn), lambda i, j, kk: (i, j)) for _ in extras]
    return pl.pallas_call(
        body, name=name, grid=(m // tm, n // tn, nk), in_specs=[a_spec, b_spec] + ex_specs, out_specs=o_spec, out_shape=o_shape,
        scratch_shapes=[pltpu.VMEM((tm, tn), F32)] if nk > 1 else [],
        compiler_params=_cparams(("parallel", "parallel", "arbitrary")),
    )(a, b, *extras)


def _place():
    x, y, c = lax.axis_index("x"), lax.axis_index("y"), lax.axis_index("c")
    return x, y, c, [(1 - x, y), (x, 1 - y), (1 - x, 1 - y)]


def _allgather8(name, v):
    m_per, n = v.shape

    def body(x_ref, out_ref, send_sems, recv_sems, local_sem):
        x, y, c, chips = _place()
        me, sibling = (x, y, c), (x, y, 1 - c)

        def rows(px, py, pc):
            return out_ref.at[pl.ds((4 * px + 2 * py + pc) * m_per, m_per), :]

        def copy(k, block, to, src=None):
            return pltpu.make_async_remote_copy(
                src_ref=rows(*block) if src is None else src, dst_ref=rows(*block), send_sem=send_sems.at[k],
                recv_sem=recv_sems.at[k], device_id=to, device_id_type=MESH)

        mine = pltpu.make_async_copy(x_ref, rows(*me), local_sem)
        mine.start()
        first = [copy(0, me, sibling, src=x_ref)]
        first += [copy(1 + j, me, (*chip, c), src=x_ref) for j, chip in enumerate(chips)]
        for cp in first:
            cp.start()
        passed = [copy(4 + j, (*chip, c), sibling) for j, chip in enumerate(chips)]
        for j, chip in enumerate(chips):
            copy(1 + j, (*chip, c), me).wait_recv()
            passed[j].start()
        copy(0, sibling, me).wait_recv()
        for j, chip in enumerate(chips):
            copy(4 + j, (*chip, 1 - c), me).wait_recv()
        for cp in first + passed:
            cp.wait_send()
        mine.wait()

    return pl.pallas_call(
        body, name=name, out_shape=jax.ShapeDtypeStruct((8 * m_per, n), v.dtype),
        in_specs=[pl.BlockSpec(memory_space=pltpu.VMEM)], out_specs=pl.BlockSpec(memory_space=pltpu.VMEM),
        scratch_shapes=[pltpu.SemaphoreType.DMA((7,)), pltpu.SemaphoreType.DMA((7,)), pltpu.SemaphoreType.DMA],
        compiler_params=_cparams(),
    )(v)


def _gather_weights(stacks, after):
    nw = len(stacks)

    def body(*refs):
        outs = refs[nw + 1 : 2 * nw + 1]
        send_sems, recv_sems = refs[2 * nw + 1 :]
        x, y, c, chips = _place()
        me, sibling = (x, y, c), (x, y, 1 - c)

        def copy(w, k, block, to):
            px, py, pc = block
            half = outs[w].shape[1] // 2
            rows = outs[w].at[2 * px + py, pl.ds(pc * half, half), :]
            return pltpu.make_async_remote_copy(src_ref=rows, dst_ref=rows, send_sem=send_sems.at[6 * w + k],
                                                recv_sem=recv_sems.at[6 * w + k], device_id=to, device_id_type=MESH)

        sent = [copy(w, j, me, (*chip, c)) for w in range(nw) for j, chip in enumerate(chips)]
        for cp in sent:
            cp.start()
        for w in range(nw):
            for j, chip in enumerate(chips):
                copy(w, j, (*chip, c), me).wait_recv()
                passed = copy(w, 3 + j, (*chip, c), sibling)
                passed.start()
                sent.append(passed)
        for w in range(nw):
            for j, chip in enumerate(chips):
                copy(w, 3 + j, (*chip, 1 - c), me).wait_recv()
        for cp in sent:
            cp.wait_send()

    hbm = pl.BlockSpec(memory_space=pl.ANY)
    return pl.pallas_call(
        body, name="gather_weights", out_shape=[jax.ShapeDtypeStruct(w.shape, w.dtype) for w in stacks],
        in_specs=[hbm] * (nw + 1), out_specs=[hbm] * nw, input_output_aliases={i: i for i in range(nw)},
        scratch_shapes=[pltpu.SemaphoreType.DMA((6 * nw,)), pltpu.SemaphoreType.DMA((6 * nw,))],
        compiler_params=_cparams(),
    )(*stacks, after)


def _swap_halves(gs, name):
    nw = len(gs)

    def body(*refs):
        ins, gots = refs[:nw], refs[nw : 2 * nw]
        send_sems, recv_sems = refs[2 * nw :]
        x, y, c, _ = _place()
        cps = []
        for w in range(nw):
            half = ins[w].shape[1] // 2
            cps.append(pltpu.make_async_remote_copy(
                src_ref=ins[w].at[:, pl.ds((1 - c) * half, half), :], dst_ref=gots[w], send_sem=send_sems.at[w],
                recv_sem=recv_sems.at[w], device_id=(x, y, 1 - c), device_id_type=MESH))
            cps[-1].start()
        for cp in cps:
            cp.wait()

    hbm = pl.BlockSpec(memory_space=pl.ANY)
    return pl.pallas_call(
        body, name=name, out_shape=[jax.ShapeDtypeStruct((4, g.shape[1] // 2, g.shape[2]), g.dtype) for g in gs],
        in_specs=[hbm] * nw, out_specs=[hbm] * nw,
        scratch_shapes=[pltpu.SemaphoreType.DMA((nw,)), pltpu.SemaphoreType.DMA((nw,))], compiler_params=_cparams(),
    )(*gs)


def _split_start(name, bufs, plan, n, extra=()):
    n_plan = len(bufs)
    bufs = list(bufs) + list(extra)
    nb = len(bufs)

    def body(*refs):
        send_sems, recv_sems, token = refs[nb], refs[nb + 1], refs[-1]
        for k, (src, dst, to, _) in enumerate(plan(refs[:n_plan])):
            pltpu.make_async_remote_copy(src_ref=src, dst_ref=dst, send_sem=send_sems.at[k], recv_sem=recv_sems.at[k],
                                         device_id=to, device_id_type=MESH).start()
        token[...] = jnp.zeros(token.shape, token.dtype)

    hbm, sem = pl.BlockSpec(memory_space=pltpu.HBM), pl.BlockSpec(memory_space=pltpu.SEMAPHORE)
    res = pl.pallas_call(
        body, name=name,
        out_shape=(pltpu.SemaphoreType.DMA((n,)), pltpu.SemaphoreType.DMA((n,)), *[pltpu.HBM(b.shape, b.dtype) for b in bufs],
                   jax.ShapeDtypeStruct((SUBLANES, 128), F32)),
        in_specs=(hbm,) * nb, out_specs=(sem, sem) + (hbm,) * nb + (pl.BlockSpec(memory_space=pltpu.VMEM),),
        input_output_aliases={i: 2 + i for i in range(nb)},
        compiler_params=pltpu.CompilerParams(has_side_effects=pltpu.SideEffectType.DATAFLOW_SIDE_EFFECTING),
    )(*[pltpu.with_memory_space_constraint(b, pltpu.HBM) for b in bufs])
    return res[0], res[1], list(res[2 : 2 + nb]), res[-1]


def _split_wait(name, bufs, send_sems, recv_sems, after, plan):
    nb = len(bufs)

    def body(*refs):
        send, recv = refs[nb], refs[nb + 1]
        for k, (src, _, to, land) in enumerate(plan(refs[:nb])):
            cp = pltpu.make_async_remote_copy(src_ref=src, dst_ref=land, send_sem=send.at[k], recv_sem=recv.at[k],
                                              device_id=to, device_id_type=MESH)
            cp.wait_send()
            cp.wait_recv()

    hbm, sem = pl.BlockSpec(memory_space=pltpu.HBM), pl.BlockSpec(memory_space=pltpu.SEMAPHORE)
    return pl.pallas_call(
        body, name=name, out_shape=tuple(pltpu.HBM(b.shape, b.dtype) for b in bufs),
        in_specs=(hbm,) * nb + (sem, sem, pl.BlockSpec(memory_space=pl.ANY)), out_specs=(hbm,) * nb,
        input_output_aliases={i: i for i in range(nb)},
        compiler_params=pltpu.CompilerParams(has_side_effects=pltpu.SideEffectType.DATAFLOW_SIDE_EFFECTING),
    )(*bufs, send_sems, recv_sems, after)


def _indexed(name, fn, idx, ins, out, grid):
    def body(idx_ref, *refs):
        refs[-1][...] = fn(*[r[...] for r in refs[:-1]]).astype(refs[-1].dtype)

    (o_shape, o_dtype), o_block, o_map = out
    return pl.pallas_call(
        body, name=name, out_shape=jax.ShapeDtypeStruct(o_shape, o_dtype),
        grid_spec=pltpu.PrefetchScalarGridSpec(
            num_scalar_prefetch=1, grid=grid, in_specs=[pl.BlockSpec(blk, mp) for (_, blk, mp) in ins],
            out_specs=pl.BlockSpec(o_block, o_map)),
        compiler_params=_cparams(("arbitrary",) * len(grid)),
    )(idx, *[a for (a, _, _) in ins])


def _rstd(v):
    return lax.rsqrt(jnp.mean(v * v, axis=-1, keepdims=True) + EPS)


def _rms_bwd(dy, xh, r):
    return r * (dy - xh * jnp.mean(dy * xh, axis=-1, keepdims=True))


def _heads(v):
    return [v[:, h * HEAD_DIM : (h + 1) * HEAD_DIM] for h in range(v.shape[1] // HEAD_DIM)]


def _gelu(v):
    k = math.sqrt(2.0 / math.pi)
    return 0.5 * v * (1.0 + jnp.tanh(k * (v + 0.044715 * v * v * v)))


def _gelu_grad(v):
    k = math.sqrt(2.0 / math.pi)
    t = jnp.tanh(k * (v + 0.044715 * v * v * v))
    return 0.5 * (1.0 + t) + 0.5 * v * (1.0 - t * t) * k * (1.0 + 3 * 0.044715 * v * v)


def _sigmoid(v):
    return 1.0 / (1.0 + jnp.exp(-v))


def _slopes(n_heads):
    return [2.0 ** (-8.0 * (h + 1.0) / n_heads) for h in range(n_heads)]


def _band_iotas():
    ri = lax.broadcasted_iota(jnp.int32, (BAND, BAND), 0)
    ci = lax.broadcasted_iota(jnp.int32, (BAND, BAND), 1)
    return ri, ci


def _to_lanes(cols):
    rows = cols[0].shape[0]
    lane = lax.broadcasted_iota(jnp.int32, (rows, HEAD_DIM), 1)
    out = jnp.zeros((rows, HEAD_DIM), F32)
    for h, col in enumerate(cols):
        out = jnp.where(lane == h, col, out)
    return out


def _from_lanes(v, h, width):
    return jnp.broadcast_to(v[:, h : h + 1], (v.shape[0], width))


def _attn_fwd(name, q, k, v, bps, dil):
    s, aw = q.shape
    n_heads, nb = aw // HEAD_DIM, s // BAND
    scale = HEAD_DIM**-0.5
    slopes = _slopes(n_heads)

    def body(q_ref, kp_ref, kc_ref, vp_ref, vc_ref, o_ref, l_ref):
        b = pl.program_id(0)
        first = (b % bps) == 0
        ri, ci = _band_iotas()
        dist_c = (ri - ci).astype(F32)
        dist_p = (ri - ci + BAND).astype(F32)
        valid_c = ci <= ri
        valid_p = ci >= ri + jnp.where(first, BAND, 0)
        hs = range(n_heads)
        sls = [slice(h * HEAD_DIM, (h + 1) * HEAD_DIM) for h in hs]
        qs = [q_ref[:, sl] for sl in sls]
        sc = [lax.dot_general(qs[h], kc_ref[:, sls[h]], _NT, preferred_element_type=F32) for h in hs]
        sp = [lax.dot_general(qs[h], kp_ref[:, sls[h]], _NT, preferred_element_type=F32) for h in hs]
        sc = [jnp.where(valid_c, sc[h] * scale - (slopes[h] * dil) * dist_c, NEG) for h in hs]
        sp = [jnp.where(valid_p, sp[h] * scale - (slopes[h] * dil) * dist_p, NEG) for h in hs]
        m = [jnp.maximum(jnp.max(sc[h], axis=1, keepdims=True), jnp.max(sp[h], axis=1, keepdims=True)) for h in hs]
        pc = [jnp.exp(sc[h] - m[h]) for h in hs]
        pp = [jnp.exp(sp[h] - m[h]) for h in hs]
        den = [jnp.sum(pc[h], axis=1, keepdims=True) + jnp.sum(pp[h], axis=1, keepdims=True) for h in hs]
        acc = [jnp.dot(pc[h].astype(BF16), vc_ref[:, sls[h]], preferred_element_type=F32) for h in hs]
        acc = [acc[h] + jnp.dot(pp[h].astype(BF16), vp_ref[:, sls[h]], preferred_element_type=F32) for h in hs]
        o_ref[...] = jnp.concatenate([acc[h] / den[h] for h in hs], axis=1).astype(o_ref.dtype)
        l_ref[...] = _to_lanes([m[h] + jnp.log(den[h]) for h in hs])

    cur = pl.BlockSpec((BAND, aw), lambda b: (b, 0))
    prev = pl.BlockSpec((BAND, aw), lambda b: (jnp.maximum(b - 1, 0), 0))
    return pl.pallas_call(
        body, name=name, grid=(nb,), in_specs=[cur, prev, cur, prev, cur], out_specs=[cur, pl.BlockSpec((BAND, HEAD_DIM), lambda b: (b, 0))],
        out_shape=[jax.ShapeDtypeStruct((s, aw), BF16), jax.ShapeDtypeStruct((s, HEAD_DIM), F32)],
        compiler_params=_cparams(("arbitrary",)),
    )(q, k, k, v, v)


def _attn_bwd(name, q, k, v, do, lse, delta, bps, dil):
    s, aw = q.shape
    n_heads, nb = aw // HEAD_DIM, s // BAND
    scale = HEAD_DIM**-0.5
    slopes = _slopes(n_heads)

    def body(qc_ref, qn_ref, doc_ref, don_ref, lc_ref, ln_ref, dc_ref, dn_ref, k_ref, v_ref, dq_ref, dk_ref, dv_ref, carry):
        j = pl.program_id(0)

        @pl.when(j == 0)
        def _():
            carry[...] = jnp.zeros(carry.shape, F32)

        has_next = jnp.logical_and((j + 1) % bps != 0, j + 1 < nb)
        ri, ci = _band_iotas()
        dist_c = (ri - ci).astype(F32)
        dist_p = (ri - ci + BAND).astype(F32)
        valid_c = ci <= ri
        valid_n = ci >= ri + jnp.where(has_next, 0, BAND)
        carried = carry[...]
        hs = range(n_heads)
        sls = [slice(h * HEAD_DIM, (h + 1) * HEAD_DIM) for h in hs]
        ks, vs = [k_ref[:, sl] for sl in sls], [v_ref[:, sl] for sl in sls]
        sides = [(qc_ref, doc_ref, lc_ref, dc_ref, dist_c, valid_c), (qn_ref, don_ref, ln_ref, dn_ref, dist_p, valid_n)]
        res = []
        for q_ref, do_ref, l_ref, d_ref, dist, valid in sides:
            qs, dos = [q_ref[:, sl] for sl in sls], [do_ref[:, sl] for sl in sls]
            sc = [lax.dot_general(qs[h], ks[h], _NT, preferred_element_type=F32) for h in hs]
            dp = [lax.dot_general(dos[h], vs[h], _NT, preferred_element_type=F32) for h in hs]
            lse_v, delta_v = l_ref[...], d_ref[...]
            p = [jnp.where(valid, jnp.exp(sc[h] * scale - (slopes[h] * dil) * dist - _from_lanes(lse_v, h, BAND)), 0.0) for h in hs]
            ds = [(p[h] * (dp[h] - _from_lanes(delta_v, h, BAND))).astype(BF16) for h in hs]
            pb = [p[h].astype(BF16) for h in hs]
            dv = [lax.dot_general(pb[h], dos[h], _TN, preferred_element_type=F32) for h in hs]
            dk = [lax.dot_general(ds[h], qs[h], _TN, preferred_element_type=F32) for h in hs]
            dq = [jnp.dot(ds[h], ks[h], preferred_element_type=F32) for h in hs]
            res.append((dq, dk, dv))
        (dq_c, dk_c, dv_c), (dq_n, dk_n, dv_n) = res
        dq_ref[...] = (carried + jnp.concatenate(dq_c, axis=1) * scale).astype(dq_ref.dtype)
        carry[...] = jnp.concatenate(dq_n, axis=1) * scale
        dk_ref[...] = (jnp.concatenate([dk_c[h] + dk_n[h] for h in hs], axis=1) * scale).astype(dk_ref.dtype)
        dv_ref[...] = jnp.concatenate([dv_c[h] + dv_n[h] for h in hs], axis=1).astype(dv_ref.dtype)

    cur = pl.BlockSpec((BAND, aw), lambda b: (b, 0))
    nxt = pl.BlockSpec((BAND, aw), lambda b: (jnp.minimum(b + 1, nb - 1), 0))
    cur1 = pl.BlockSpec((BAND, HEAD_DIM), lambda b: (b, 0))
    nxt1 = pl.BlockSpec((BAND, HEAD_DIM), lambda b: (jnp.minimum(b + 1, nb - 1), 0))
    return pl.pallas_call(
        body, name=name, grid=(nb,), in_specs=[cur, nxt, cur, nxt, cur1, nxt1, cur1, nxt1, cur, cur], out_specs=[cur] * 3,
        out_shape=[jax.ShapeDtypeStruct((s, aw), BF16)] * 3, scratch_shapes=[pltpu.VMEM((BAND, aw), F32)],
        compiler_params=_cparams(("arbitrary",)),
    )(q, q, do, do, lse, lse, delta, delta, k, v)


STREAM_ROWS = BAND * max(dil for _, dil in DILATION_PATTERNS)


def _stream_orders(name, fn, srcs, vecs, n_vals, n_heads):
    s = srcs[0][0].shape[0]
    dils = [dil for _, dil in DILATION_PATTERNS]
    n_in = len(srcs) + len(vecs)

    def body(*refs):
        outs, scr = refs[n_in:-1], refs[-1]
        for vi, val in enumerate(fn(*[r[...] for r in refs[:n_in]])):
            scr[...] = val
            for di, dil in enumerate(dils):
                o = outs[vi * len(dils) + di]
                if dil == 1:
                    o[...] = val.astype(o.dtype)
                    continue
                for part in range(STREAM_ROWS // (dil * BAND)):
                    for r in range(dil):
                        rows = scr[pl.ds(part * dil * BAND + r, BAND, stride=dil), :]
                        o[r, part * BAND : (part + 1) * BAND, :] = rows.astype(o.dtype)

    in_specs = [pl.BlockSpec((STREAM_ROWS, HEAD_DIM), lambda b, h, cb=cb: (b, cb + h)) for (_, cb) in srcs]
    in_specs += [pl.BlockSpec(v.shape, lambda b, h, nd=v.ndim: (0,) * nd) for v in vecs]
    out_specs, out_shape = [], []
    for _ in range(n_vals):
        for dil in dils:
            if dil == 1:
                out_specs.append(pl.BlockSpec((STREAM_ROWS, HEAD_DIM), lambda b, h: (b, h)))
                out_shape.append(jax.ShapeDtypeStruct((s, n_heads * HEAD_DIM), BF16))
            else:
                out_specs.append(pl.BlockSpec((dil, STREAM_ROWS // dil, HEAD_DIM), lambda b, h: (0, b, h)))
                out_shape.append(jax.ShapeDtypeStruct((dil, s // dil, n_heads * HEAD_DIM), BF16))
    res = pl.pallas_call(
        body, name=name, grid=(s // STREAM_ROWS, n_heads), in_specs=in_specs, out_specs=out_specs, out_shape=out_shape,
        scratch_shapes=[pltpu.VMEM((STREAM_ROWS, HEAD_DIM), F32)], compiler_params=_cparams(("arbitrary", "arbitrary")),
    )(*[a for (a, _) in srcs], *vecs)
    res = [r.reshape(s, n_heads * HEAD_DIM) for r in res]
    return [res[vi * len(dils) : (vi + 1) * len(dils)] for vi in range(n_vals)]


def _natural_order(name, fn, vals, extras, out_dtypes, n_heads):
    s = vals[0][0].shape[0]
    dils = [dil for _, dil in DILATION_PATTERNS]
    n_in = len(vals) * len(dils) + len(extras)

    def body(*refs):
        outs, scr = refs[n_in:-1], refs[-1]
        blocks = []
        for vi in range(len(vals)):
            per_pattern = []
            for di, dil in enumerate(dils):
                x = refs[vi * len(dils) + di]
                if dil == 1:
                    per_pattern.append(x[...].astype(F32))
                    continue
                for part in range(STREAM_ROWS // (dil * BAND)):
                    for r in range(dil):
                        scr[pl.ds(part * dil * BAND + r, BAND, stride=dil), :] = x[r, part * BAND : (part + 1) * BAND, :].astype(F32)
                per_pattern.append(scr[...])
            blocks.append(per_pattern)
        res = fn(pl.program_id(1), blocks, *[r[...] for r in refs[len(vals) * len(dils) : n_in]])
        for o, v in zip(outs, res):
            o[...] = v.astype(o.dtype)

    in_specs, ins = [], []
    for per_pattern in vals:
        for arr, dil in zip(per_pattern, dils):
            if dil == 1:
                in_specs.append(pl.BlockSpec((STREAM_ROWS, HEAD_DIM), lambda b, h: (b, h)))
                ins.append(arr)
            else:
                in_specs.append(pl.BlockSpec((dil, STREAM_ROWS // dil, HEAD_DIM), lambda b, h: (0, b, h)))
                ins.append(arr.reshape(dil, s // dil, n_heads * HEAD_DIM))
    in_specs += [pl.BlockSpec((STREAM_ROWS, HEAD_DIM), lambda b, h: (b, 0)) for _ in extras]
    return pl.pallas_call(
        body, name=name, grid=(s // STREAM_ROWS, n_heads), in_specs=in_specs,
        out_specs=[pl.BlockSpec((STREAM_ROWS, HEAD_DIM), lambda b, h: (b, h)) for _ in out_dtypes],
        out_shape=[jax.ShapeDtypeStruct((s, n_heads * HEAD_DIM), dt) for dt in out_dtypes],
        scratch_shapes=[pltpu.VMEM((STREAM_ROWS, HEAD_DIM), F32)], compiler_params=_cparams(("arbitrary", "arbitrary")),
    )(*ins, *extras)


def _to_streams(v, dil):
    if dil == 1:
        return v
    s, w = v.shape
    return v.reshape(s // dil, dil, w).transpose(1, 0, 2).reshape(s, w)


def _from_streams(v, dil):
    if dil == 1:
        return v
    s, w = v.shape
    return v.reshape(dil, s // dil, w).transpose(1, 0, 2).reshape(s, w)


SCAN_LANES = 1024
SSM_ROWS = 128
SSM_FWD_ROWS = 256
FF_TM = 1024


def _cmul(ar, ai, br, bi):
    return ar * br - ai * bi, ar * bi + ai * br


def _cmul_conj(ar, ai, br, bi):
    return ar * br + ai * bi, ar * bi - ai * br


def _ssm_fwd(u_src, bre, bim, cre, cim, consts, d_skip):
    u_arr, sw, ucb = u_src
    s = u_arr.shape[0]
    nblk = bre.shape[0]
    nst = nblk * 512
    t = _blk(s, SSM_FWD_ROWS)
    w = _blk(nst, SCAN_LANES)

    def body(u_ref, bre_ref, bim_ref, cre_ref, cim_ref, k_ref, ds_ref, y_ref, hr_ref, hi_ref, xr, xi, car_r, car_i):
        @pl.when(pl.program_id(0) == 0)
        def _():
            car_r[...] = jnp.zeros(car_r.shape, F32)
            car_i[...] = jnp.zeros(car_i.shape, F32)

        u = u_ref[...]
        ub = u.astype(BF16)
        for b in range(nblk):
            ch, st = slice(b * 128, (b + 1) * 128), slice(b * 512, (b + 1) * 512)
            xr[:, st] = jnp.dot(ub[:, ch], bre_ref[b], preferred_element_type=F32)
            xi[:, st] = jnp.dot(ub[:, ch], bim_ref[b], preferred_element_type=F32)
        for c in range(nst // w):
            ls = slice(c * w, (c + 1) * w)
            mult = [k_ref[i, :, ls] for i in range(8)]

            def step(i, carry, ls=ls, mult=mult):
                cr, ci = carry
                r0 = pl.multiple_of(i * SUBLANES, SUBLANES)
                hr, hi = xr[pl.ds(r0, SUBLANES), ls], xi[pl.ds(r0, SUBLANES), ls]
                for n, sh in enumerate((1, 2, 4)):
                    pr, pi = _cmul(mult[2 * n], mult[2 * n + 1], pltpu.roll(hr, sh, 0), pltpu.roll(hi, sh, 0))
                    hr, hi = hr + pr, hi + pi
                pr, pi = _cmul(mult[6], mult[7], cr, ci)
                hr, hi = hr + pr, hi + pi
                xr[pl.ds(r0, SUBLANES), ls] = hr
                xi[pl.ds(r0, SUBLANES), ls] = hi
                return jnp.broadcast_to(hr[7:8, :], hr.shape), jnp.broadcast_to(hi[7:8, :], hi.shape)

            cr, ci = lax.fori_loop(0, t // SUBLANES, step, (car_r[:, ls], car_i[:, ls]))
            car_r[:, ls] = cr
            car_i[:, ls] = ci
        hrb, hib = xr[...].astype(BF16), xi[...].astype(BF16)
        hr_ref[...] = hrb
        hi_ref[...] = hib
        for b in range(nblk):
            ch, st = slice(b * 128, (b + 1) * 128), slice(b * 512, (b + 1) * 512)
            yb = jnp.dot(hrb[:, st], cre_ref[b], preferred_element_type=F32)
            yb -= jnp.dot(hib[:, st], cim_ref[b], preferred_element_type=F32)
            y_ref[:, ch] = yb + ds_ref[:, ch] * u[:, ch]

    whole = lambda a: pl.BlockSpec(a.shape, lambda i, nd=a.ndim: (0,) * nd)
    return pl.pallas_call(
        body, name="ssm_fwd", grid=(s // t,),
        in_specs=[pl.BlockSpec((t, sw), lambda i: (i, ucb))] + [whole(a) for a in (bre, bim, cre, cim, consts, d_skip)],
        out_specs=[pl.BlockSpec((t, sw), lambda i: (i, 0)), pl.BlockSpec((t, nst), lambda i: (i, 0)), pl.BlockSpec((t, nst), lambda i: (i, 0))],
        out_shape=[jax.ShapeDtypeStruct((s, sw), F32), jax.ShapeDtypeStruct((s, nst), BF16), jax.ShapeDtypeStruct((s, nst), BF16)],
        scratch_shapes=[pltpu.VMEM((t, nst), F32), pltpu.VMEM((t, nst), F32), pltpu.VMEM((SUBLANES, nst), F32), pltpu.VMEM((SUBLANES, nst), F32)],
        compiler_params=_cparams(("arbitrary",)),
    )(u_arr, bre, bim, cre, cim, consts, d_skip)


def _ssm_bwd(dy, u_src, hr, hi, ctre, ctim, btre, btim, consts, d_skip):
    u_arr, sw, ucb = u_src
    s = dy.shape[0]
    nblk = ctre.shape[0]
    nst = nblk * 512
    t = _blk(s, SSM_ROWS)
    w = _blk(nst, SCAN_LANES)
    nt = s // t

    def body(dy_ref, u_ref, hr_ref, hi_ref, ctre_ref, ctim_ref, btre_ref, btim_ref, k_ref, ds_ref,
             du_ref, db_ref, dc_ref, da_ref, dd_ref, lr, li, hfr, hfi, car_r, car_i):
        @pl.when(pl.program_id(0) == 0)
        def _():
            car_r[...] = jnp.zeros(car_r.shape, F32)
            car_i[...] = jnp.zeros(car_i.shape, F32)
            db_ref[...] = jnp.zeros(db_ref.shape, F32)
            dc_ref[...] = jnp.zeros(dc_ref.shape, F32)
            da_ref[...] = jnp.zeros(da_ref.shape, F32)
            dd_ref[...] = jnp.zeros(dd_ref.shape, F32)

        dyv, u = dy_ref[...].astype(F32), u_ref[...]
        dyb, ub = dyv.astype(BF16), u.astype(BF16)
        hrb, hib = hr_ref[...], hi_ref[...]
        hfr[...] = hrb.astype(F32)
        hfi[...] = hib.astype(F32)
        for b in range(nblk):
            ch, st = slice(b * 128, (b + 1) * 128), slice(b * 512, (b + 1) * 512)
            lr[:, st] = jnp.dot(dyb[:, ch], ctre_ref[b], preferred_element_type=F32)
            li[:, st] = -jnp.dot(dyb[:, ch], ctim_ref[b], preferred_element_type=F32)
        last_row = lax.broadcasted_iota(jnp.int32, (SUBLANES, w), 0) == SUBLANES - 1
        for c in range(nst // w):
            ls = slice(c * w, (c + 1) * w)
            mult = [k_ref[i, :, ls] for i in range(8)]

            def step(i, carry, ls=ls, mult=mult):
                cr, ci, ar, ai = carry
                r0 = pl.multiple_of((t // SUBLANES - 1 - i) * SUBLANES, SUBLANES)
                gr, gi = lr[pl.ds(r0, SUBLANES), ls], li[pl.ds(r0, SUBLANES), ls]
                for n, sh in enumerate((1, 2, 4)):
                    pr, pi = _cmul_conj(mult[2 * n], mult[2 * n + 1], pltpu.roll(gr, SUBLANES - sh, 0), pltpu.roll(gi, SUBLANES - sh, 0))
                    gr, gi = gr + pr, gi + pi
                pr, pi = _cmul_conj(mult[6], mult[7], cr, ci)
                gr, gi = gr + pr, gi + pi
                lr[pl.ds(r0, SUBLANES), ls] = gr
                li[pl.ds(r0, SUBLANES), ls] = gi
                nr = jnp.where(last_row, cr, pltpu.roll(gr, SUBLANES - 1, 0))
                ni = jnp.where(last_row, ci, pltpu.roll(gi, SUBLANES - 1, 0))
                fr, fi = hfr[pl.ds(r0, SUBLANES), ls], hfi[pl.ds(r0, SUBLANES), ls]
                pr, pi = _cmul_conj(fr, fi, nr, ni)
                return (jnp.broadcast_to(gr[0:1, :], gr.shape), jnp.broadcast_to(gi[0:1, :], gi.shape), ar + pr, ai + pi)

            zero = jnp.zeros((SUBLANES, w), F32)
            cr, ci, ar, ai = lax.fori_loop(0, t // SUBLANES, step, (car_r[:, ls], car_i[:, ls], zero, zero))
            car_r[:, ls] = cr
            car_i[:, ls] = ci
            da_ref[0, :, ls] += ar
            da_ref[1, :, ls] += ai
        lrb, lib = lr[...].astype(BF16), li[...].astype(BF16)
        for b in range(nblk):
            ch, st = slice(b * 128, (b + 1) * 128), slice(b * 512, (b + 1) * 512)
            dub = jnp.dot(lrb[:, st], btre_ref[b], preferred_element_type=F32)
            dub += jnp.dot(lib[:, st], btim_ref[b], preferred_element_type=F32)
            du_ref[:, ch] = (dub + ds_ref[:, ch] * dyv[:, ch]).astype(du_ref.dtype)
            db_ref[0, b] += lax.dot_general(ub[:, ch], lrb[:, st], _TN, preferred_element_type=F32)
            db_ref[1, b] += lax.dot_general(ub[:, ch], lib[:, st], _TN, preferred_element_type=F32)
            dc_ref[0, b] += lax.dot_general(dyb[:, ch], hrb[:, st], _TN, preferred_element_type=F32)
            dc_ref[1, b] -= lax.dot_general(dyb[:, ch], hib[:, st], _TN, preferred_element_type=F32)
        dd_ref[...] += _colsum8(dyv * u)

    whole = lambda a: pl.BlockSpec(a.shape, lambda i, nd=a.ndim: (0,) * nd)
    rev = lambda wd, cb=0: pl.BlockSpec((t, wd), lambda i, cb=cb: (nt - 1 - i, cb))
    acc = lambda sh: pl.BlockSpec(sh, lambda i, nd=len(sh): (0,) * nd)
    db_shape, da_shape, dd_shape = (2, nblk, 128, 512), (2, SUBLANES, nst), (SUBLANES, sw)
    return pl.pallas_call(
        body, name="ssm_bwd", grid=(nt,),
        in_specs=[rev(sw), rev(sw, ucb), rev(nst), rev(nst)] + [whole(a) for a in (ctre, ctim, btre, btim, consts, d_skip)],
        out_specs=[rev(sw), acc(db_shape), acc(db_shape), acc(da_shape), acc(dd_shape)],
        out_shape=[jax.ShapeDtypeStruct((s, sw), BF16), jax.ShapeDtypeStruct(db_shape, F32), jax.ShapeDtypeStruct(db_shape, F32),
                   jax.ShapeDtypeStruct(da_shape, F32), jax.ShapeDtypeStruct(dd_shape, F32)],
        scratch_shapes=[pltpu.VMEM((t, nst), F32)] * 4 + [pltpu.VMEM((SUBLANES, nst), F32)] * 2,
        compiler_params=_cparams(("arbitrary",)),
    )(dy, u_arr, hr, hi, ctre, ctim, btre, btim, consts, d_skip)


def _ssm_discretise(lam_re, lam_im, log_step):
    step = jnp.exp(log_step)
    e = jnp.exp(lam_re * step)
    ar, ai = e * jnp.cos(lam_im * step), e * jnp.sin(lam_im * step)
    den = lam_re * lam_re + lam_im * lam_im
    inv_r, inv_i = lam_re / den, -lam_im / den
    fr, fi = _cmul(ar - 1.0, ai, inv_r, inv_i)
    return step, ar, ai, inv_r, inv_i, fr, fi


def _ssm_prep_fn(lam_re, lam_im, log_step):
    _, ar, ai, _, _, fr, fi = _ssm_discretise(lam_re, lam_im, log_step)
    pw = [(ar, ai)]
    for n in range(1, 8):
        pw.append(_cmul(*pw[n - 1], ar, ai))
    return [p[0] for p in pw] + [p[1] for p in pw] + [fr, fi]


def _ssm_param_grad_fn(lam_re, lam_im, log_step, da_r, da_i, df_r, df_i):
    step, ar, ai, inv_r, inv_i, fr, fi = _ssm_discretise(lam_re, lam_im, log_step)
    pr, pi = _cmul_conj(inv_r, inv_i, df_r, df_i)
    dat_r, dat_i = da_r + pr, da_i + pi
    wr, wi = _cmul(fr, fi, inv_r, inv_i)
    dl_r, dl_i = _cmul_conj(-wr, -wi, df_r, df_i)
    dz_r, dz_i = _cmul_conj(ar, ai, dat_r, dat_i)
    dl_r, dl_i = dl_r + step * dz_r, dl_i + step * dz_i
    dstep = jnp.sum(dz_r * lam_re + dz_i * lam_im, axis=-1, keepdims=True)
    return [dl_r, dl_i, step * dstep]


def _block_diag(v, rows_first):
    g, a, b = v.shape
    nb = g // GROUPS_PER_BLOCK
    eye = jnp.eye(GROUPS_PER_BLOCK, dtype=v.dtype)
    v = v.reshape(nb, GROUPS_PER_BLOCK, a, 1, b) * eye[None, :, None, :, None]
    return v.reshape(nb, GROUPS_PER_BLOCK * a, GROUPS_PER_BLOCK * b)


def _block_diag_take(v, a, b):
    nb = v.shape[0]
    v = v.reshape(nb, GROUPS_PER_BLOCK, a, GROUPS_PER_BLOCK, b)
    return jnp.stack([v[:, g, :, g, :] for g in range(GROUPS_PER_BLOCK)], axis=1).reshape(nb * GROUPS_PER_BLOCK, a, b)


def _adamw(w, g, m, v):
    m = ADAM_B1 * m + (1.0 - ADAM_B1) * g
    v = ADAM_B2 * v + (1.0 - ADAM_B2) * (g * g)
    m_hat = m / (1.0 - ADAM_B1**ADAM_STEP)
    v_hat = v / (1.0 - ADAM_B2**ADAM_STEP)
    delta = -ADAM_LR * (m_hat / (jnp.sqrt(v_hat) + ADAM_EPS) + ADAM_WD * w)
    return delta, m, v


def _adamw_rows(name, w, g, m, v):
    wd = w.shape[1]
    return _rowwise(name, lambda a, b, c, d: (list(_adamw(a, b, c, d)), []), [w, g, m, v], [], [(wd, F32)] * 3, [], tm=128)


SMALL = ["b_ada", "norm1_g", "q_norm_g", "k_norm_g", "lam_re", "lam_im", "log_step", "b_re", "b_im", "c_re", "c_im",
         "d_skip", "b_glu", "attn_out_g", "ssm_out_g", "norm2_g"]
LARGE = ["w_in", "w_glu", "w_out", "w_ff1", "w_ff2"]
ORDER = ["w_ada", "b_ada", "norm1_g", "w_in", "q_norm_g", "k_norm_g", "lam_re", "lam_im", "log_step", "b_re", "b_im", "c_re",
         "c_im", "d_skip", "w_glu", "b_glu", "attn_out_g", "ssm_out_g", "w_out", "norm2_g", "w_ff1", "w_ff2"]


def _pack(arrs):
    flat = jnp.concatenate([a.reshape(-1) for a in arrs])
    rows = -(-flat.shape[0] // PACK_LANES)
    rows = -(-rows // SUBLANES) * SUBLANES
    return jnp.pad(flat, (0, rows * PACK_LANES - flat.shape[0])).reshape(rows, PACK_LANES)


def _unpack(packed, shapes):
    flat, out, o = packed.reshape(-1), [], 0
    for sh in shapes:
        n = int(np.prod(sh))
        out.append(flat[o : o + n].reshape(sh))
        o += n
    return out


def kernel(x, c, w_ada, b_ada, norm1_g, w_in, q_norm_g, k_norm_g, lam_re, lam_im, log_step, b_re, b_im, c_re, c_im, d_skip, w_glu, b_glu, attn_out_g, ssm_out_g, w_out, norm2_g, w_ff1, w_ff2, loss_target, m_w_ada, m_b_ada, m_norm1_g, m_w_in, m_q_norm_g, m_k_norm_g, m_lam_re, m_lam_im, m_log_step, m_b_re, m_b_im, m_c_re, m_c_im, m_d_skip, m_w_glu, m_b_glu, m_attn_out_g, m_ssm_out_g, m_w_out, m_norm2_g, m_w_ff1, m_w_ff2, v_w_ada, v_b_ada, v_norm1_g, v_w_in, v_q_norm_g, v_k_norm_g, v_lam_re, v_lam_im, v_log_step, v_b_re, v_b_im, v_c_re, v_c_im, v_d_skip, v_w_glu, v_b_glu, v_attn_out_g, v_ssm_out_g, v_w_out, v_norm2_g, v_w_ff1, v_w_ff2):
    args = dict(locals())
    wts = {n: args[n] for n in ORDER}
    mom = {n: args["m_" + n] for n in ORDER}
    var = {n: args["v_" + n] for n in ORDER}

    x2, tgt = x[0], loss_target[0]
    s, d = x2.shape
    aw = d // 2
    sw = d - aw
    n_groups = sw // SSM_GROUP
    nst = n_groups * STATE_DIM
    chip = 2 * lax.axis_index("x") + lax.axis_index("y")

    chip_idx = jnp.reshape(chip, (1,)).astype(jnp.int32)
    core_idx = jnp.reshape(lax.axis_index("c"), (1,)).astype(jnp.int32)

    def cast_into_stack(n):
        w2 = wts[n][0]
        rows, cols = w2.shape
        tr = _blk(rows, 256)
        return _indexed("cast_" + n, lambda a: a, chip_idx, [(w2, (tr, cols), lambda r, ix: (r, 0))],
                        (((4, rows, cols), BF16), (None, tr, cols), lambda r, ix: (ix[0], r, 0)), (rows // tr,))

    stacks = {n: cast_into_stack(n) for n in LARGE}
    dff = 4 * wts["w_ff2"].shape[1]

    c_all = _allgather8("gather_c", jnp.pad(c, ((0, SUBLANES - 1), (0, 0))))[::SUBLANES]
    n_mod = w_ada.shape[2]
    b_ada_mine = lax.dynamic_slice(b_ada, (0, chip * n_mod), (1, n_mod))
    sil_c = _single("silu_c", lambda v: [v * _sigmoid(v)], [c_all], [(c_all.shape, F32)])[0]
    mod_part = _mm("ada_fwd", sil_c, w_ada[0], "nn", F32, tn=512, epi=lambda acc, b: acc + b[0:1, :],
                   extras=[jnp.broadcast_to(b_ada_mine, (SUBLANES, n_mod))])
    mod_all = _allgather8("gather_mod", mod_part)
    me = 2 * chip + lax.axis_index("c")
    mod = jnp.concatenate([lax.dynamic_slice(mod_all, (16 * k + me, 0), (1, n_mod)) for k in range(4)], axis=1)
    sh1, sc1, g1, sh2, sc2, g2 = [mod[:, i * d : (i + 1) * d] for i in range(6)]

    (w_in_s,) = _gather_weights([stacks["w_in"]], mod_all)

    def gather_plan(refs):
        px0, py0, pc, chips = _place()
        slot = 2 * px0 + py0
        return [(r.at[slot], r.at[slot], (px, py, pc), r.at[2 * px + py]) for r in refs for (px, py) in chips]

    mix_sems_s, mix_sems_r, mix_stacks, mix_token = _split_start(
        "gather_mix_start", [stacks["w_glu"], stacks["w_out"]], gather_plan, 6, extra=[w_in_s])
    w_in_s = mix_stacks.pop()
    ff_sems_s, ff_sems_r, ff_stacks, ff_token = _split_start(
        "gather_ff_start", [stacks["w_ff1"], stacks["w_ff2"]], gather_plan, 6, extra=[w_in_s])
    w_in_s = ff_stacks.pop()
    sc1 = sc1 + (mix_token[0:1, 0:1] + ff_token[0:1, 0:1])

    def norm_mod(v, g, sc, sh):
        return [v * _rstd(v) * g * (1.0 + sc) + sh], []

    (h1b,) = _rowwise("norm1", norm_mod, [x2], [norm1_g, sc1, sh1], [(d, BF16)], [])
    proj = _mm("proj_in", h1b, w_in_s, "nn", F32, b_cs=True, tm=FF_TM)
    assert aw == sw
    q_src, k_src, v_src, u_src = [(proj, aw, i) for i in range(3)] + [(proj, sw, 3)]

    n_heads = aw // HEAD_DIM
    qk_orders = _stream_orders("qk_prep", lambda q, k, v, gq, gk: [q * _rstd(q) * gq, k * _rstd(k) * gk, v],
                               [(proj, 0), (proj, n_heads), (proj, 2 * n_heads)], [q_norm_g, k_norm_g], 3, n_heads)

    pats = []
    for pi, (window, dil) in enumerate(DILATION_PATTERNS):
        assert window // dil == BAND and s % STREAM_ROWS == 0
        bps = s // dil // BAND
        qp, kp, vp = [order[pi] for order in qk_orders]
        o_p, l_p = _attn_fwd(f"attn_fwd_d{dil}", qp, kp, vp, bps, dil)
        pats.append((dil, bps, qp, kp, vp, o_p, _from_streams(l_p, dil)))

    def lse_total(l1, l2, l3):
        m = jnp.maximum(jnp.maximum(l1, l2), l3)
        return m + jnp.log(jnp.exp(l1 - m) + jnp.exp(l2 - m) + jnp.exp(l3 - m))

    (lse,) = _rowwise("attn_lse", lambda l1, l2, l3: ([lse_total(l1, l2, l3)], []), [p[6] for p in pats], [], [(HEAD_DIM, F32)], [])

    def combine(h, blocks, l1, l2, l3):
        tot = lse_total(l1, l2, l3)
        lane = lax.broadcasted_iota(jnp.int32, l1.shape, 1)
        mix = 0.0
        for o_blk, l in zip(blocks[0], (l1, l2, l3)):
            w = jnp.sum(jnp.where(lane == h, jnp.exp(l - tot), 0.0), axis=1, keepdims=True)
            mix = mix + w * o_blk
        return [mix]

    (attn,) = _natural_order("attn_mix", combine, [[p[5] for p in pats]], [p[6] for p in pats], [F32], n_heads)

    lam_re2, lam_im2, log_step2 = lam_re[0], lam_im[0], log_step[0].reshape(n_groups, 1)
    prep = _single("ssm_prep", _ssm_prep_fn, [lam_re2, lam_im2, log_step2], [((n_groups, STATE_DIM), F32)] * 18)
    p_r = jnp.stack([p.reshape(nst) for p in prep[:8]])
    p_i = jnp.stack([p.reshape(nst) for p in prep[8:16]])
    f_ri = jnp.stack(prep[16:])
    row = jnp.arange(SUBLANES)[:, None]
    zero = jnp.zeros((SUBLANES, nst), F32)
    sel = lambda cond, v: jnp.where(cond, jnp.broadcast_to(v[None, :], (SUBLANES, nst)), zero)
    fwd_consts = jnp.stack([sel(row >= 1, p_r[0]), sel(row >= 1, p_i[0]), sel(row >= 2, p_r[1]), sel(row >= 2, p_i[1]),
                            sel(row >= 4, p_r[3]), sel(row >= 4, p_i[3]), p_r, p_i])
    bwd_consts = jnp.stack([sel(row <= 6, p_r[0]), sel(row <= 6, p_i[0]), sel(row <= 5, p_r[1]), sel(row <= 5, p_i[1]),
                            sel(row <= 3, p_r[3]), sel(row <= 3, p_i[3]), p_r[::-1], p_i[::-1]])
    nblk = n_groups // GROUPS_PER_BLOCK
    f_rows = f_ri.reshape(2, nblk, 1, 512)
    braw_r = _block_diag(b_re[0].transpose(0, 2, 1), True)
    braw_i = _block_diag(b_im[0].transpose(0, 2, 1), True)

    def bbar_fn(br, bi, f):
        r, i = _cmul(f[0], f[1], br, bi)
        return [r, i]

    bbar_r, bbar_i = _single("ssm_bbar", bbar_fn, [braw_r, braw_i, f_rows], [(braw_r.shape, BF16)] * 2)
    ct_r, ct_i = _block_diag(c_re[0], True).astype(BF16), _block_diag(c_im[0], True).astype(BF16)
    ds2 = d_skip
    y_ssm, h_r, h_i = _ssm_fwd(u_src, bbar_r, bbar_i, ct_r.transpose(0, 2, 1), ct_i.transpose(0, 2, 1), fwd_consts, ds2)

    (yb,) = _rowwise("gelu", lambda v: ([_gelu(v)], []), [y_ssm], [], [(sw, BF16)], [])
    w_glu_s, w_out_s = _split_wait("gather_mix_wait", mix_stacks, mix_sems_s, mix_sems_r, yb, gather_plan)
    w_glu_f = w_glu_s.reshape(sw, sw)
    w_out_f = w_out_s.reshape(d, d)
    z = _mm("glu_fwd", yb, w_glu_f, "nn", BF16)

    def glu(ys, zz, b):
        return [_gelu(ys) * _sigmoid(zz.astype(F32) + b)], []

    (ssm,) = _rowwise("glu", glu, [y_ssm, z], [b_glu], [(sw, F32)], [])

    def cat_norm(a, sm, ga, gs):
        return [jnp.concatenate([a * _rstd(a) * ga, sm * _rstd(sm) * gs], axis=1)], []

    (catb,) = _rowwise("cat_norm", cat_norm, [attn, ssm], [attn_out_g, ssm_out_g], [(d, BF16)], [])
    mixed = _mm("mix_out", catb, w_out_f, "nn", F32, tm=FF_TM)

    def resid_norm(xv, mx, g1v, g, sc, sh):
        x1v = xv + g1v * mx
        return [x1v, x1v * _rstd(x1v) * g * (1.0 + sc) + sh], []

    x1, h2b = _rowwise("resid_norm2", resid_norm, [x2, mixed], [g1, norm2_g, sc2, sh2], [(d, F32), (d, BF16)], [])
    w_ff1_s, w_ff2_s = _split_wait("gather_ff_wait", ff_stacks, ff_sems_s, ff_sems_r, x1, gather_plan)
    w_ff2_f = w_ff2_s.reshape(dff, d)
    a_ff = _mm("ff1", h2b, w_ff1_s, "nn", BF16, b_cs=True, tm=FF_TM)
    sq_relu = lambda v: jnp.square(jnp.maximum(v.astype(F32), 0.0)).astype(BF16)
    ff = _mm("ff2", a_ff, w_ff2_f, "nn", F32, a_pro=sq_relu, tm=FF_TM)

    def loss_fn(x1v, ffv, tg, g2v):
        diff = x1v + g2v * ffv - tg
        dout = diff * (1.0 / d)
        return [dout, dout * g2v], [_colsum8(0.5 * diff * dout), _colsum8(dout * ffv)]

    dout, dffb, loss_acc, dg2_acc = _rowwise("loss", loss_fn, [x1, ff, tgt], [g2], [(d, F32), (d, BF16)], [(SUBLANES, d)] * 2)

    da_ff = _mm("ff2_bwd", dffb, w_ff2_f, "nt", BF16, epi=lambda acc, av: acc * (2.0 * jnp.maximum(av.astype(F32), 0.0)), extras=[a_ff], tm=FF_TM)
    gw = {}
    gw["w_ff2"] = _mm("ff2_wgrad", a_ff, dffb, "tn", BF16, a_pro=sq_relu, tm=FF_TM).reshape(4, dff // 4, d)
    gw["w_ff1"] = _mm("ff1_wgrad", h2b, da_ff, "tn", BF16, o_cs=True, tm=FF_TM)

    def add2(n, g, b):
        _, half, cols = b.shape
        tr = _blk(half, 256)
        nrb = half // tr
        blk = (None, tr, cols)
        return _indexed("grad_add2_" + n, lambda p, q: p.astype(F32) + q.astype(F32), core_idx,
                        [(g, blk, lambda k, r, ix: (k, ix[0] * nrb + r, 0)), (b, blk, lambda k, r, ix: (k, r, 0))],
                        ((b.shape, BF16), blk, lambda k, r, ix: (k, r, 0)), (4, nrb))

    def scatter_plan(refs):
        px0, py0, pc, chips = _place()
        nw = len(refs) // 2
        return [(refs[w].at[2 * px + py], refs[nw + w].at[j], (px, py, pc), refs[nw + w].at[j])
                for w in range(nw) for j, (px, py) in enumerate(chips)]

    def swap_plan(refs):
        px0, py0, pc, _ = _place()
        nw = len(refs) // 2
        plan = []
        for w in range(nw):
            half = refs[w].shape[1] // 2
            plan.append((refs[w].at[:, pl.ds((1 - pc) * half, half), :], refs[nw + w], (px0, py0, 1 - pc), refs[nw + w]))
        return plan

    early = ["w_ff2", "w_ff1"]
    sw_lands = [lax.empty((4, gw[n].shape[1] // 2, gw[n].shape[2]), BF16) for n in early]
    sw_sems_s, sw_sems_r, sw_bufs, sw_token = _split_start("swap_ff_start", [gw[n] for n in early] + sw_lands, swap_plan, len(early))
    sc2_late = sc2 + sw_token[0:1, 0:1]

    dh2 = _mm("ff1_bwd", da_ff, w_ff1_s, "nt", BF16, b_cs=True, tm=FF_TM)

    def norm2_bwd(dh, x1v, do, mx, g, sc, g1v):
        dh = dh.astype(F32)
        r = _rstd(x1v)
        xh = x1v * r
        dn = dh * (1.0 + sc)
        dx = do + _rms_bwd(dn * g, xh, r)
        return [dx, dx * g1v], [_colsum8(dh * xh * g), _colsum8(dh), _colsum8(dn * xh), _colsum8(dx * mx)]

    dx1, dmixb, dsc2_acc, dsh2_acc, dn2g_acc, dg1_acc = _rowwise(
        "norm2_bwd", norm2_bwd, [dh2, x1, dout, mixed], [norm2_g, sc2_late, g1], [(d, F32), (d, BF16)], [(SUBLANES, d)] * 4)

    sw_bufs = _split_wait("swap_ff_wait", sw_bufs, sw_sems_s, sw_sems_r, dmixb, swap_plan)
    parts = {n: add2(n, g, b) for n, g, b in zip(early, sw_bufs[: len(early)], sw_bufs[len(early) :])}
    lands = [lax.empty((3,) + parts[n].shape[1:], BF16) for n in early]
    sc_sems_s, sc_sems_r, sc_bufs, sc_token = _split_start("scatter_ff_start", [parts[n] for n in early] + lands, scatter_plan, 6)
    attn_out_g_late = attn_out_g + sc_token[0:1, 0:1]

    dcat = _mm("mix_out_bwd", dmixb, w_out_f, "nt", BF16, tm=FF_TM)
    gw["w_out"] = _mm("mix_out_wgrad", catb, dmixb, "tn", BF16, tm=FF_TM).reshape(4, d // 4, d)

    def cat_bwd(da, dsm, a, sm, ga, gs):
        da, dsm = da.astype(F32), dsm.astype(F32)
        ra, rs = _rstd(a), _rstd(sm)
        ah, sh = a * ra, sm * rs
        return [_rms_bwd(da * ga, ah, ra), _rms_bwd(dsm * gs, sh, rs)], [_colsum8(da * ah), _colsum8(dsm * sh)]

    dattn, dssm, dga_acc, dgs_acc = _rowwise(
        "cat_norm_bwd", cat_bwd, [(dcat, aw, 0), (dcat, sw, 1), attn, ssm], [attn_out_g_late, ssm_out_g], [(aw, F32), (sw, F32)],
        [(SUBLANES, aw), (SUBLANES, sw)])

    def glu_bwd(dsm, ys, zz, b):
        sg = _sigmoid(zz.astype(F32) + b)
        dz = dsm * _gelu(ys) * sg * (1.0 - sg)
        return [dz, dsm * sg], [_colsum8(dz)]

    dzb, dy_direct, dbglu_acc = _rowwise("glu_bwd", glu_bwd, [dssm, y_ssm, z], [b_glu], [(sw, BF16), (sw, BF16)], [(SUBLANES, sw)])
    gw["w_glu"] = _mm("glu_wgrad", yb, dzb, "tn", BF16).reshape(4, sw // 4, sw)
    dys = _mm("glu_bwd_mm", dzb, w_glu_f, "nt", BF16, epi=lambda acc, direct, ys: (acc + direct.astype(F32)) * _gelu_grad(ys), extras=[dy_direct, y_ssm])
    du, db_bar, dc_bd, da_acc, dds_acc = _ssm_bwd(dys, u_src, h_r, h_i, ct_r, ct_i, bbar_r.transpose(0, 2, 1),
                                                  bbar_i.transpose(0, 2, 1), bwd_consts, ds2)

    def delta_fn(da, a):
        return [_to_lanes([jnp.sum(p * q, axis=1, keepdims=True) for p, q in zip(_heads(da), _heads(a))])], []

    (delta,) = _rowwise("attn_delta", delta_fn, [dattn, attn], [], [(HEAD_DIM, F32)], [])
    (do_orders,) = _stream_orders("attn_do", lambda da: [da], [(dattn, 0)], [], 1, n_heads)
    dqs, dks, dvs = [], [], []
    for pi, (dil, bps, qp, kp, vp, _, _) in enumerate(pats):
        lsp, dlp = [_to_streams(t, dil) for t in (lse, delta)]
        dq_p, dk_p, dv_p = _attn_bwd(f"attn_bwd_d{dil}", qp, kp, vp, do_orders[pi], lsp, dlp, bps, dil)
        dqs.append(dq_p)
        dks.append(dk_p)
        dvs.append(dv_p)
    dq_t, dk_t, dv_t = _natural_order("attn_grad_sum", lambda h, blocks: [(b[0] + b[1]) + b[2] for b in blocks],
                                      [dqs, dks, dvs], [], [BF16] * 3, n_heads)

    def qk_bwd(dq_sum, dk_sum, dv_sum, duv, q, k, gq, gk):
        def one(dn, raw, g):
            outs, acc = [], 0.0
            for dh, h in zip(_heads(dn), _heads(raw)):
                r = _rstd(h)
                hh = h * r
                outs.append(_rms_bwd(dh * g, hh, r))
                acc = acc + _colsum8(dh * hh)
            return jnp.concatenate(outs, axis=1), acc

        dq, gq_acc = one(dq_sum.astype(F32), q, gq)
        dk, gk_acc = one(dk_sum.astype(F32), k, gk)
        return [jnp.concatenate([dq, dk, dv_sum.astype(F32), duv.astype(F32)], axis=1)], [gq_acc, gk_acc]

    dprojb, dgq_acc, dgk_acc = _rowwise("qk_norm_bwd", qk_bwd, [dq_t, dk_t, dv_t, du, q_src, k_src], [q_norm_g, k_norm_g],
                                        [(3 * aw + sw, BF16)], [(SUBLANES, HEAD_DIM)] * 2)
    gw["w_in"] = _mm("proj_in_wgrad", h1b, dprojb, "tn", BF16, o_cs=True, tm=FF_TM)
    late = [n for n in LARGE if n not in early]
    parts.update({n: add2(n, gw[n], b) for n, b in zip(late, _swap_halves([gw[n] for n in late], "swap_halves_rest"))})
    lands_late = [lax.empty((3,) + parts[n].shape[1:], BF16) for n in late]
    rest_sems_s, rest_sems_r, rest_bufs, rest_token = _split_start(
        "scatter_rest_start", [parts[n] for n in late] + lands_late, scatter_plan, 3 * len(late))
    sc1_late = sc1 + rest_token[0:1, 0:1]
    dh1 = _mm("proj_in_bwd", dprojb, w_in_s, "nt", BF16, b_cs=True, tm=FF_TM)

    def norm1_bwd(dh, xv, dxr, g, sc):
        dh = dh.astype(F32)
        r = _rstd(xv)
        xh = xv * r
        dn = dh * (1.0 + sc)
        return [dxr + _rms_bwd(dn * g, xh, r)], [_colsum8(dh * xh * g), _colsum8(dh), _colsum8(dn * xh)]

    grad_x, dsc1_acc, dsh1_acc, dn1g_acc = _rowwise("norm1_bwd", norm1_bwd, [dh1, x2, dx1], [norm1_g, sc1_late], [(d, F32)], [(SUBLANES, d)] * 3)

    def b_grad_fn(dbb, br, bi, f):
        fr, fi = f[0], f[1]
        dbr, dbi = _cmul_conj(fr, fi, dbb[0], dbb[1])
        dfr = jnp.sum(dbb[0] * br + dbb[1] * bi, axis=1, keepdims=True)
        dfi = jnp.sum(dbb[1] * br - dbb[0] * bi, axis=1, keepdims=True)
        return [dbr, dbi, dfr, dfi]

    dbr_bd, dbi_bd, df_r, df_i = _single("ssm_b_grad", b_grad_fn, [db_bar, braw_r, braw_i, f_rows],
                                         [(braw_r.shape, F32)] * 2 + [((nblk, 1, 512), F32)] * 2)

    def sums_fn(*accs):
        return [jnp.sum(a, axis=-2, keepdims=True) for a in accs]

    acc_list = [dsh1_acc, dsc1_acc, dg1_acc, dsh2_acc, dsc2_acc, dg2_acc, dn1g_acc, dgq_acc, dgk_acc, dds_acc, dbglu_acc, dga_acc,
                dgs_acc, dn2g_acc, loss_acc, da_acc]
    sums = _single("small_sums", sums_fn, acc_list, [(a.shape[:-2] + (1, a.shape[-1]), F32) for a in acc_list])
    (dsh1, dsc1, dg1, dsh2, dsc2, dg2, dn1g, dgq, dgk, dds, dbglu, dga, dgs, dn2g, loss_row, da_sum) = sums
    loss_dev = _single("loss_sum", lambda v: [jnp.sum(v, axis=1, keepdims=True)], [loss_row], [((1, 1), F32)])[0]
    gp = (n_groups, STATE_DIM)
    dlam_re, dlam_im, dlog_step = _single(
        "ssm_param_grad", _ssm_param_grad_fn,
        [lam_re2, lam_im2, log_step2, da_sum[0].reshape(gp), da_sum[1].reshape(gp), df_r.reshape(gp), df_i.reshape(gp)],
        [(gp, F32), (gp, F32), ((n_groups, 1), F32)])
    small_dev = {
        "b_ada": jnp.concatenate([dsh1, dsc1, dg1, dsh2, dsc2, dg2], axis=1),
        "norm1_g": dn1g, "q_norm_g": dgq, "k_norm_g": dgk, "lam_re": dlam_re, "lam_im": dlam_im, "log_step": dlog_step,
        "b_re": _block_diag_take(dbr_bd, SSM_GROUP, STATE_DIM).transpose(0, 2, 1),
        "b_im": _block_diag_take(dbi_bd, SSM_GROUP, STATE_DIM).transpose(0, 2, 1),
        "c_re": _block_diag_take(dc_bd[0], SSM_GROUP, STATE_DIM), "c_im": _block_diag_take(dc_bd[1], SSM_GROUP, STATE_DIM),
        "d_skip": dds, "b_glu": dbglu, "attn_out_g": dga, "ssm_out_g": dgs, "norm2_g": dn2g,
    }

    packed = _pack([small_dev[n] for n in SMALL])
    rows = packed.shape[0]
    gathered = _allgather8("gather_small", packed).reshape(8, rows, PACK_LANES)

    def sum8_fn(g):
        acc = g[0]
        for i in range(1, 8):
            acc = acc + g[i]
        return [acc]

    g_small = _single("sum_small", sum8_fn, [gathered], [((rows, PACK_LANES), F32)])[0]

    sc_bufs = _split_wait("scatter_ff_wait", sc_bufs, sc_sems_s, sc_sems_r, grad_x, scatter_plan)
    parts.update(zip(early, sc_bufs[: len(early)]))
    got2 = dict(zip(early, sc_bufs[len(early) :]))
    rest_bufs = _split_wait("scatter_rest_wait", rest_bufs, rest_sems_s, rest_sems_r, g_small, scatter_plan)
    parts.update(zip(late, rest_bufs[: len(late)]))
    got2.update(zip(late, rest_bufs[len(late) :]))
    place_idx = jnp.stack([chip, lax.axis_index("c")]).astype(jnp.int32)

    def add4(n, p, b):
        _, half, cols = p.shape
        tr = _blk(half, 256)
        blk = (None, tr, cols)
        fn = lambda own, q0, q1, q2: ((own.astype(F32) + q0.astype(F32)) + q1.astype(F32)) + q2.astype(F32)
        srcs = [(p, blk, lambda r, ix: (ix[0], r, 0))] + [(b, blk, lambda r, ix, j=j: (j, r, 0)) for j in range(3)]
        return _indexed("grad_add4_" + n, fn, place_idx, srcs, (((2, half, cols), F32), blk, lambda r, ix: (ix[1], r, 0)), (half // tr,))

    def join_plan(refs):
        px0, py0, pc, _ = _place()
        return [(r.at[pc], r.at[pc], (px0, py0, 1 - pc), r.at[1 - pc]) for r in refs]

    join_sems_s, join_sems_r, fulls, join_token = _split_start(
        "join_start", [add4(n, parts[n], got2[n]) for n in LARGE], join_plan, len(LARGE))
    g_small = g_small + join_token[0:1, 0:1]

    w_small, m_small, v_small = [_pack([src[n] for n in SMALL]) for src in (wts, mom, var)]
    upd_small = _adamw_rows("adamw_small", w_small, g_small, m_small, v_small)
    shapes = [wts[n].shape for n in SMALL]
    out = {}
    for kind, arr in zip(("grad", "delta", "new_m", "new_v"), (g_small,) + tuple(upd_small)):
        for n, val in zip(SMALL, _unpack(arr, shapes)):
            out[kind, n] = val

    dmod_all = gathered[:, 0 : -(-6 * d // PACK_LANES), :].reshape(8, -1)[:, : 6 * d]
    dmod_mine = lax.dynamic_slice(dmod_all, (0, chip * n_mod), (8, n_mod)) + join_token[0:1, 0:1]
    g_ada = _mm("ada_wgrad", sil_c, dmod_mine, "tn", F32, tm=512, tn=512)
    upd = _adamw_rows("adamw_w_ada", w_ada[0], g_ada, m_w_ada[0], v_w_ada[0])
    for kind, val in zip(("grad", "delta", "new_m", "new_v"), (g_ada,) + tuple(upd)):
        out[kind, "w_ada"] = val[None]

    fulls = _split_wait("join_wait", fulls, join_sems_s, join_sems_r, upd[0], join_plan)
    for n, full in zip(LARGE, fulls):
        g = full.reshape(wts[n].shape[1:])
        upd = _adamw_rows("adamw_" + n, wts[n][0], g, mom[n][0], var[n][0])
        for kind, val in zip(("grad", "delta", "new_m", "new_v"), (g,) + tuple(upd)):
            out[kind, n] = val[None]

    loss = lax.psum(loss_dev[0, 0], ("x", "y", "c"))
    return (loss, grad_x[None], *[out[kind, n] for kind in ("grad", "delta", "new_m", "new_v") for n in ORDER])
```

```python
import functools
import math

import jax
import jax.numpy as jnp
import numpy as np
from jax import lax
from jax.experimental import pallas as pl
from jax.experimental.pallas import tpu as pltpu

F32 = jnp.float32
BF16 = jnp.bfloat16
MESH = pl.DeviceIdType.MESH

EPS = 1e-6
HEAD_DIM = 128
SSM_GROUP = 16
STATE_DIM = 64
GROUPS_PER_BLOCK = 8
DILATION_PATTERNS = ((128, 1), (512, 4), (2048, 16))
BAND = 128
NEG = -1e30

ADAM_LR, ADAM_B1, ADAM_B2, ADAM_EPS, ADAM_WD, ADAM_STEP = 0.001, 0.9, 0.999, 1e-08, 0.01, 10

V7X_VMEM_BYTES = 64 * 2**20
VMEM_LIMIT = 48 * 2**20
SUBLANES = 8
PACK_LANES = 1024


def _cparams(sem=None):
    return pltpu.CompilerParams(dimension_semantics=sem, vmem_limit_bytes=VMEM_LIMIT)


def _blk(n, want):
    b = min(n, want)
    while n % b:
        b //= 2
    return b


def _colsum8(v):
    tm, w = v.shape
    return v.reshape(tm // SUBLANES, SUBLANES, w).sum(axis=0)


def _rowwise(name, fn, rows, vecs, out_rows, out_accs, tm=256, n_rows=None):
    rows = [r if isinstance(r, tuple) else (r, r.shape[1], 0) for r in rows]
    rows = [r if len(r) == 4 else r + (0,) for r in rows]
    s = rows[0][0].shape[0] if n_rows is None else n_rows
    tm = _blk(s, tm)
    assert all(r[3] % tm == 0 for r in rows)
    n_in, n_or = len(rows) + len(vecs), len(out_rows)

    def body(*refs):
        outs, accs = fn(*[r[...] for r in refs[:n_in]])
        for r, v in zip(refs[n_in : n_in + n_or], outs):
            r[...] = v.astype(r.dtype)
        if out_accs:
            acc_refs = refs[n_in + n_or :]

            @pl.when(pl.program_id(0) == 0)
            def _():
                for r in acc_refs:
                    r[...] = jnp.zeros(r.shape, r.dtype)

            for r, v in zip(acc_refs, accs):
                r[...] += v

    in_specs = [pl.BlockSpec((tm, w), lambda i, cb=cb, rb=r0 // tm: (i + rb, cb)) for (_, w, cb, r0) in rows]
    in_specs += [pl.BlockSpec(v.shape, lambda i, nd=v.ndim: (0,) * nd) for v in vecs]
    out_specs = [pl.BlockSpec((tm, w), lambda i: (i, 0)) for (w, _) in out_rows]
    out_specs += [pl.BlockSpec(sh, lambda i, nd=len(sh): (0,) * nd) for sh in out_accs]
    out_shape = [jax.ShapeDtypeStruct((s, w), dt) for (w, dt) in out_rows]
    out_shape += [jax.ShapeDtypeStruct(sh, F32) for sh in out_accs]
    res = pl.pallas_call(
        body, name=name, grid=(s // tm,), in_specs=in_specs, out_specs=out_specs, out_shape=out_shape,
        compiler_params=_cparams(("arbitrary",)),
    )(*[r[0] for r in rows], *vecs)
    return res


def _single(name, fn, ins, out_shapes):
    n_in = len(ins)

    def body(*refs):
        outs = fn(*[r[...] for r in refs[:n_in]])
        for r, v in zip(refs[n_in:], outs):
            r[...] = v.astype(r.dtype)

    vm = pl.BlockSpec(memory_space=pltpu.VMEM)
    return pl.pallas_call(
        body, name=name, in_specs=[vm] * n_in, out_specs=[vm] * len(out_shapes),
        out_shape=[jax.ShapeDtypeStruct(sh, dt) for sh, dt in out_shapes], compiler_params=_cparams(),
    )(*ins)


_NN = (((1,), (0,)), ((), ()))
_NT = (((1,), (1,)), ((), ()))
_TN = (((0,), (0,)), ((), ()))


def _mm(name, a, b, mode, out_dtype, *, tm=512, tn=1024, tk=2048, b_cs=False, o_cs=False, epi=None, extras=(), a_pro=None):
    if mode == "tn":
        k, m = a.shape
        n = b.shape[1]
    else:
        m, k = a.shape
        if mode == "nn":
            n = b.shape[2] * 4 if b_cs else b.shape[1]
        else:
            n = b.shape[1] if b_cs else b.shape[0]
    tm = _blk(m, tm)
    tn = _blk(n // 4 if (b_cs and mode == "nn") or o_cs else n, tn)
    tk = _blk(k // 4 if (b_cs and mode == "nt") else k, tk)
    nk = k // tk
    dims = {"nn": _NN, "nt": _NT, "tn": _TN}[mode]
    n_ex = len(extras)

    def body(*refs):
        a_ref, b_ref = refs[0], refs[1]
        ex_refs, o_ref = refs[2 : 2 + n_ex], refs[2 + n_ex]

        def finish(acc):
            if epi is not None:
                acc = epi(acc, *[r[...] for r in ex_refs])
            o_ref[...] = acc.astype(o_ref.dtype)

        a_blk = a_ref[...] if a_pro is None else a_pro(a_ref[...])
        part = lax.dot_general(a_blk, b_ref[...], dims, preferred_element_type=F32)
        if nk == 1:
            finish(part)
        else:
            acc_ref = refs[3 + n_ex]
            kk = pl.program_id(2)

            @pl.when(kk == 0)
            def _():
                acc_ref[...] = part

            @pl.when(kk > 0)
            def _():
                acc_ref[...] += part

            @pl.when(kk == nk - 1)
            def _():
                finish(acc_ref[...])

    a_spec = pl.BlockSpec((tk, tm), lambda i, j, kk: (kk, i)) if mode == "tn" else pl.BlockSpec((tm, tk), lambda i, j, kk: (i, kk))
    if mode == "nn":
        if b_cs:
            per = (n // 4) // tn
            b_spec = pl.BlockSpec((None, tk, tn), lambda i, j, kk: (j // per, kk, j % per))
        else:
            b_spec = pl.BlockSpec((tk, tn), lambda i, j, kk: (kk, j))
    elif mode == "nt":
        if b_cs:
            per = (k // 4) // tk
            b_spec = pl.BlockSpec((None, tn, tk), lambda i, j, kk: (kk // per, j, kk % per))
        else:
            b_spec = pl.BlockSpec((tn, tk), lambda i, j, kk: (j, kk))
    else:
        b_spec = pl.BlockSpec((tk, tn), lambda i, j, kk: (kk, j))
    if o_cs:
        per = (n // 4) // tn
        o_spec = pl.BlockSpec((None, tm, tn), lambda i, j, kk: (j // per, i, j % per))
        o_shape = jax.ShapeDtypeStruct((4, m, n // 4), out_dtype)
    else:
        o_spec = pl.BlockSpec((tm, tn), lambda i, j, kk: (i, j))
        o_shape = jax.ShapeDtypeStruct((m, n), out_dtype)
    ex_specs = [pl.BlockSpec((tm, tn), lambda i, j, kk: (i, j)) for _ in extras]
    return pl.pallas_call(
        body, name=name, grid=(m // tm, n // tn, nk), in_specs=[a_spec, b_spec] + ex_specs, out_specs=o_spec, out_shape=o_shape,
        scratch_shapes=[pltpu.VMEM((tm, tn), F32)] if nk > 1 else [],
        compiler_params=_cparams(("parallel", "parallel", "arbitrary")),
    )(a, b, *extras)


def _place():
    x, y, c = lax.axis_index("x"), lax.axis_index("y"), lax.axis_index("c")
    return x, y, c, [(1 - x, y), (x, 1 - y), (1 - x, 1 - y)]


def _allgather8(name, v):
    m_per, n = v.shape

    def body(x_ref, out_ref, send_sems, recv_sems, local_sem):
        x, y, c, chips = _place()
        me, sibling = (x, y, c), (x, y, 1 - c)

        def rows(px, py, pc):
            return out_ref.at[pl.ds((4 * px + 2 * py + pc) * m_per, m_per), :]

        def copy(k, block, to, src=None):
            return pltpu.make_async_remote_copy(
                src_ref=rows(*block) if src is None else src, dst_ref=rows(*block), send_sem=send_sems.at[k],
                recv_sem=recv_sems.at[k], device_id=to, device_id_type=MESH)

        mine = pltpu.make_async_copy(x_ref, rows(*me), local_sem)
        mine.start()
        first = [copy(0, me, sibling, src=x_ref)]
        first += [copy(1 + j, me, (*chip, c), src=x_ref) for j, chip in enumerate(chips)]
        for cp in first:
            cp.start()
        passed = [copy(4 + j, (*chip, c), sibling) for j, chip in enumerate(chips)]
        for j, chip in enumerate(chips):
            copy(1 + j, (*chip, c), me).wait_recv()
            passed[j].start()
        copy(0, sibling, me).wait_recv()
        for j, chip in enumerate(chips):
            copy(4 + j, (*chip, 1 - c), me).wait_recv()
        for cp in first + passed:
            cp.wait_send()
        mine.wait()

    return pl.pallas_call(
        body, name=name, out_shape=jax.ShapeDtypeStruct((8 * m_per, n), v.dtype),
        in_specs=[pl.BlockSpec(memory_space=pltpu.VMEM)], out_specs=pl.BlockSpec(memory_space=pltpu.VMEM),
        scratch_shapes=[pltpu.SemaphoreType.DMA((7,)), pltpu.SemaphoreType.DMA((7,)), pltpu.SemaphoreType.DMA],
        compiler_params=_cparams(),
    )(v)


def _gather_weights(stacks, after):
    nw = len(stacks)

    def body(*refs):
        outs = refs[nw + 1 : 2 * nw + 1]
        send_sems, recv_sems = refs[2 * nw + 1 :]
        x, y, c, chips = _place()
        me, sibling = (x, y, c), (x, y, 1 - c)

        def copy(w, k, block, to):
            px, py, pc = block
            half = outs[w].shape[1] // 2
            rows = outs[w].at[2 * px + py, pl.ds(pc * half, half), :]
            return pltpu.make_async_remote_copy(src_ref=rows, dst_ref=rows, send_sem=send_sems.at[6 * w + k],
                                                recv_sem=recv_sems.at[6 * w + k], device_id=to, device_id_type=MESH)

        sent = [copy(w, j, me, (*chip, c)) for w in range(nw) for j, chip in enumerate(chips)]
        for cp in sent:
            cp.start()
        for w in range(nw):
            for j, chip in enumerate(chips):
                copy(w, j, (*chip, c), me).wait_recv()
                passed = copy(w, 3 + j, (*chip, c), sibling)
                passed.start()
                sent.append(passed)
        for w in range(nw):
            for j, chip in enumerate(chips):
                copy(w, 3 + j, (*chip, 1 - c), me).wait_recv()
        for cp in sent:
            cp.wait_send()

    hbm = pl.BlockSpec(memory_space=pl.ANY)
    return pl.pallas_call(
        body, name="gather_weights", out_shape=[jax.ShapeDtypeStruct(w.shape, w.dtype) for w in stacks],
        in_specs=[hbm] * (nw + 1), out_specs=[hbm] * nw, input_output_aliases={i: i for i in range(nw)},
        scratch_shapes=[pltpu.SemaphoreType.DMA((6 * nw,)), pltpu.SemaphoreType.DMA((6 * nw,))],
        compiler_params=_cparams(),
    )(*stacks, after)


def _swap_halves(gs, name):
    nw = len(gs)

    def body(*refs):
        ins, gots = refs[:nw], refs[nw : 2 * nw]
        send_sems, recv_sems = refs[2 * nw :]
        x, y, c, _ = _place()
        cps = []
        for w in range(nw):
            half = ins[w].shape[1] // 2
            cps.append(pltpu.make_async_remote_copy(
                src_ref=ins[w].at[:, pl.ds((1 - c) * half, half), :], dst_ref=gots[w], send_sem=send_sems.at[w],
                recv_sem=recv_sems.at[w], device_id=(x, y, 1 - c), device_id_type=MESH))
            cps[-1].start()
        for cp in cps:
            cp.wait()

    hbm = pl.BlockSpec(memory_space=pl.ANY)
    return pl.pallas_call(
        body, name=name, out_shape=[jax.ShapeDtypeStruct((4, g.shape[1] // 2, g.shape[2]), g.dtype) for g in gs],
        in_specs=[hbm] * nw, out_specs=[hbm] * nw,
        scratch_shapes=[pltpu.SemaphoreType.DMA((nw,)), pltpu.SemaphoreType.DMA((nw,))], compiler_params=_cparams(),
    )(*gs)


def _split_start(name, bufs, plan, n, extra=()):
    n_plan = len(bufs)
    bufs = list(bufs) + list(extra)
    nb = len(bufs)

    def body(*refs):
        send_sems, recv_sems, token = refs[nb], refs[nb + 1], refs[-1]
        for k, (src, dst, to, _) in enumerate(plan(refs[:n_plan])):
            pltpu.make_async_remote_copy(src_ref=src, dst_ref=dst, send_sem=send_sems.at[k], recv_sem=recv_sems.at[k],
                                         device_id=to, device_id_type=MESH).start()
        token[...] = jnp.zeros(token.shape, token.dtype)

    hbm, sem = pl.BlockSpec(memory_space=pltpu.HBM), pl.BlockSpec(memory_space=pltpu.SEMAPHORE)
    res = pl.pallas_call(
        body, name=name,
        out_shape=(pltpu.SemaphoreType.DMA((n,)), pltpu.SemaphoreType.DMA((n,)), *[pltpu.HBM(b.shape, b.dtype) for b in bufs],
                   jax.ShapeDtypeStruct((SUBLANES, 128), F32)),
        in_specs=(hbm,) * nb, out_specs=(sem, sem) + (hbm,) * nb + (pl.BlockSpec(memory_space=pltpu.VMEM),),
        input_output_aliases={i: 2 + i for i in range(nb)},
        compiler_params=pltpu.CompilerParams(has_side_effects=pltpu.SideEffectType.DATAFLOW_SIDE_EFFECTING),
    )(*[pltpu.with_memory_space_constraint(b, pltpu.HBM) for b in bufs])
    return res[0], res[1], list(res[2 : 2 + nb]), res[-1]


def _split_wait(name, bufs, send_sems, recv_sems, after, plan):
    nb = len(bufs)

    def body(*refs):
        send, recv = refs[nb], refs[nb + 1]
        for k, (src, _, to, land) in enumerate(plan(refs[:nb])):
            cp = pltpu.make_async_remote_copy(src_ref=src, dst_ref=land, send_sem=send.at[k], recv_sem=recv.at[k],
                                              device_id=to, device_id_type=MESH)
            cp.wait_send()
            cp.wait_recv()

    hbm, sem = pl.BlockSpec(memory_space=pltpu.HBM), pl.BlockSpec(memory_space=pltpu.SEMAPHORE)
    return pl.pallas_call(
        body, name=name, out_shape=tuple(pltpu.HBM(b.shape, b.dtype) for b in bufs),
        in_specs=(hbm,) * nb + (sem, sem, pl.BlockSpec(memory_space=pl.ANY)), out_specs=(hbm,) * nb,
        input_output_aliases={i: i for i in range(nb)},
        compiler_params=pltpu.CompilerParams(has_side_effects=pltpu.SideEffectType.DATAFLOW_SIDE_EFFECTING),
    )(*bufs, send_sems, recv_sems, after)


def _indexed(name, fn, idx, ins, out, grid):
    def body(idx_ref, *refs):
        refs[-1][...] = fn(*[r[...] for r in refs[:-1]]).astype(refs[-1].dtype)

    (o_shape, o_dtype), o_block, o_map = out
    return pl.pallas_call(
        body, name=name, out_shape=jax.ShapeDtypeStruct(o_shape, o_dtype),
        grid_spec=pltpu.PrefetchScalarGridSpec(
            num_scalar_prefetch=1, grid=grid, in_specs=[pl.BlockSpec(blk, mp) for (_, blk, mp) in ins],
            out_specs=pl.BlockSpec(o_block, o_map)),
        compiler_params=_cparams(("arbitrary",) * len(grid)),
    )(idx, *[a for (a, _, _) in ins])


def _rstd(v):
    return lax.rsqrt(jnp.mean(v * v, axis=-1, keepdims=True) + EPS)


def _rms_bwd(dy, xh, r):
    return r * (dy - xh * jnp.mean(dy * xh, axis=-1, keepdims=True))


def _heads(v):
    return [v[:, h * HEAD_DIM : (h + 1) * HEAD_DIM] for h in range(v.shape[1] // HEAD_DIM)]


def _gelu(v):
    k = math.sqrt(2.0 / math.pi)
    return 0.5 * v * (1.0 + jnp.tanh(k * (v + 0.044715 * v * v * v)))


def _gelu_grad(v):
    k = math.sqrt(2.0 / math.pi)
    t = jnp.tanh(k * (v + 0.044715 * v * v * v))
    return 0.5 * (1.0 + t) + 0.5 * v * (1.0 - t * t) * k * (1.0 + 3 * 0.044715 * v * v)


def _sigmoid(v):
    return 1.0 / (1.0 + jnp.exp(-v))


def _slopes(n_heads):
    return [2.0 ** (-8.0 * (h + 1.0) / n_heads) for h in range(n_heads)]


def _band_iotas():
    ri = lax.broadcasted_iota(jnp.int32, (BAND, BAND), 0)
    ci = lax.broadcasted_iota(jnp.int32, (BAND, BAND), 1)
    return ri, ci


def _to_lanes(cols):
    rows = cols[0].shape[0]
    lane = lax.broadcasted_iota(jnp.int32, (rows, HEAD_DIM), 1)
    out = jnp.zeros((rows, HEAD_DIM), F32)
    for h, col in enumerate(cols):
        out = jnp.where(lane == h, col, out)
    return out


def _from_lanes(v, h, width):
    return jnp.broadcast_to(v[:, h : h + 1], (v.shape[0], width))


def _attn_fwd(name, q, k, v, bps, dil):
    s, aw = q.shape
    n_heads, nb = aw // HEAD_DIM, s // BAND
    scale = HEAD_DIM**-0.5
    slopes = _slopes(n_heads)

    def body(q_ref, kp_ref, kc_ref, vp_ref, vc_ref, o_ref, l_ref):
        b = pl.program_id(0)
        first = (b % bps) == 0
        ri, ci = _band_iotas()
        dist_c = (ri - ci).astype(F32)
        dist_p = (ri - ci + BAND).astype(F32)
        valid_c = ci <= ri
        valid_p = ci >= ri + jnp.where(first, BAND, 0)
        hs = range(n_heads)
        sls = [slice(h * HEAD_DIM, (h + 1) * HEAD_DIM) for h in hs]
        qs = [q_ref[:, sl] for sl in sls]
        sc = [lax.dot_general(qs[h], kc_ref[:, sls[h]], _NT, preferred_element_type=F32) for h in hs]
        sp = [lax.dot_general(qs[h], kp_ref[:, sls[h]], _NT, preferred_element_type=F32) for h in hs]
        sc = [jnp.where(valid_c, sc[h] * scale - (slopes[h] * dil) * dist_c, NEG) for h in hs]
        sp = [jnp.where(valid_p, sp[h] * scale - (slopes[h] * dil) * dist_p, NEG) for h in hs]
        m = [jnp.maximum(jnp.max(sc[h], axis=1, keepdims=True), jnp.max(sp[h], axis=1, keepdims=True)) for h in hs]
        pc = [jnp.exp(sc[h] - m[h]) for h in hs]
        pp = [jnp.exp(sp[h] - m[h]) for h in hs]
        den = [jnp.sum(pc[h], axis=1, keepdims=True) + jnp.sum(pp[h], axis=1, keepdims=True) for h in hs]
        acc = [jnp.dot(pc[h].astype(BF16), vc_ref[:, sls[h]], preferred_element_type=F32) for h in hs]
        acc = [acc[h] + jnp.dot(pp[h].astype(BF16), vp_ref[:, sls[h]], preferred_element_type=F32) for h in hs]
        o_ref[...] = jnp.concatenate([acc[h] / den[h] for h in hs], axis=1).astype(o_ref.dtype)
        l_ref[...] = _to_lanes([m[h] + jnp.log(den[h]) for h in hs])

    cur = pl.BlockSpec((BAND, aw), lambda b: (b, 0))
    prev = pl.BlockSpec((BAND, aw), lambda b: (jnp.maximum(b - 1, 0), 0))
    return pl.pallas_call(
        body, name=name, grid=(nb,), in_specs=[cur, prev, cur, prev, cur], out_specs=[cur, pl.BlockSpec((BAND, HEAD_DIM), lambda b: (b, 0))],
        out_shape=[jax.ShapeDtypeStruct((s, aw), BF16), jax.ShapeDtypeStruct((s, HEAD_DIM), F32)],
        compiler_params=_cparams(("arbitrary",)),
    )(q, k, k, v, v)


def _attn_bwd(name, q, k, v, do, lse, delta, bps, dil):
    s, aw = q.shape
    n_heads, nb = aw // HEAD_DIM, s // BAND
    scale = HEAD_DIM**-0.5
    slopes = _slopes(n_heads)

    def body(qc_ref, qn_ref, doc_ref, don_ref, lc_ref, ln_ref, dc_ref, dn_ref, k_ref, v_ref, dq_ref, dk_ref, dv_ref, carry):
        j = pl.program_id(0)

        @pl.when(j == 0)
        def _():
            carry[...] = jnp.zeros(carry.shape, F32)

        has_next = jnp.logical_and((j + 1) % bps != 0, j + 1 < nb)
        ri, ci = _band_iotas()
        dist_c = (ri - ci).astype(F32)
        dist_p = (ri - ci + BAND).astype(F32)
        valid_c = ci <= ri
        valid_n = ci >= ri + jnp.where(has_next, 0, BAND)
        carried = carry[...]
        hs = range(n_heads)
        sls = [slice(h * HEAD_DIM, (h + 1) * HEAD_DIM) for h in hs]
        ks, vs = [k_ref[:, sl] for sl in sls], [v_ref[:, sl] for sl in sls]
        sides = [(qc_ref, doc_ref, lc_ref, dc_ref, dist_c, valid_c), (qn_ref, don_ref, ln_ref, dn_ref, dist_p, valid_n)]
        res = []
        for q_ref, do_ref, l_ref, d_ref, dist, valid in sides:
            qs, dos = [q_ref[:, sl] for sl in sls], [do_ref[:, sl] for sl in sls]
            sc = [lax.dot_general(qs[h], ks[h], _NT, preferred_element_type=F32) for h in hs]
            dp = [lax.dot_general(dos[h], vs[h], _NT, preferred_element_type=F32) for h in hs]
            lse_v, delta_v = l_ref[...], d_ref[...]
            p = [jnp.where(valid, jnp.exp(sc[h] * scale - (slopes[h] * dil) * dist - _from_lanes(lse_v, h, BAND)), 0.0) for h in hs]
            ds = [(p[h] * (dp[h] - _from_lanes(delta_v, h, BAND))).astype(BF16) for h in hs]
            pb = [p[h].astype(BF16) for h in hs]
            dv = [lax.dot_general(pb[h], dos[h], _TN, preferred_element_type=F32) for h in hs]
            dk = [lax.dot_general(ds[h], qs[h], _TN, preferred_element_type=F32) for h in hs]
            dq = [jnp.dot(ds[h], ks[h], preferred_element_type=F32) for h in hs]
            res.append((dq, dk, dv))
        (dq_c, dk_c, dv_c), (dq_n, dk_n, dv_n) = res
        dq_ref[...] = (carried + jnp.concatenate(dq_c, axis=1) * scale).astype(dq_ref.dtype)
        carry[...] = jnp.concatenate(dq_n, axis=1) * scale
        dk_ref[...] = (jnp.concatenate([dk_c[h] + dk_n[h] for h in hs], axis=1) * scale).astype(dk_ref.dtype)
        dv_ref[...] = jnp.concatenate([dv_c[h] + dv_n[h] for h in hs], axis=1).astype(dv_ref.dtype)

    cur = pl.BlockSpec((BAND, aw), lambda b: (b, 0))
    nxt = pl.BlockSpec((BAND, aw), lambda b: (jnp.minimum(b + 1, nb - 1), 0))
    cur1 = pl.BlockSpec((BAND, HEAD_DIM), lambda b: (b, 0))
    nxt1 = pl.BlockSpec((BAND, HEAD_DIM), lambda b: (jnp.minimum(b + 1, nb - 1), 0))
    return pl.pallas_call(
        body, name=name, grid=(nb,), in_specs=[cur, nxt, cur, nxt, cur1, nxt1, cur1, nxt1, cur, cur], out_specs=[cur] * 3,
        out_shape=[jax.ShapeDtypeStruct((s, aw), BF16)] * 3, scratch_shapes=[pltpu.VMEM((BAND, aw), F32)],
        compiler_params=_cparams(("arbitrary",)),
    )(q, q, do, do, lse, lse, delta, delta, k, v)


STREAM_ROWS = BAND * max(dil for _, dil in DILATION_PATTERNS)


def _stream_orders(name, fn, srcs, vecs, n_vals, n_heads):
    s = srcs[0][0].shape[0]
    dils = [dil for _, dil in DILATION_PATTERNS]
    n_in = len(srcs) + len(vecs)

    def body(*refs):
        outs, scr = refs[n_in:-1], refs[-1]
        for vi, val in enumerate(fn(*[r[...] for r in refs[:n_in]])):
            scr[...] = val
            for di, dil in enumerate(dils):
                o = outs[vi * len(dils) + di]
                if dil == 1:
                    o[...] = val.astype(o.dtype)
                    continue
                for part in range(STREAM_ROWS // (dil * BAND)):
                    for r in range(dil):
                        rows = scr[pl.ds(part * dil * BAND + r, BAND, stride=dil), :]
                        o[r, part * BAND : (part + 1) * BAND, :] = rows.astype(o.dtype)

    in_specs = [pl.BlockSpec((STREAM_ROWS, HEAD_DIM), lambda b, h, cb=cb: (b, cb + h)) for (_, cb) in srcs]
    in_specs += [pl.BlockSpec(v.shape, lambda b, h, nd=v.ndim: (0,) * nd) for v in vecs]
    out_specs, out_shape = [], []
    for _ in range(n_vals):
        for dil in dils:
            if dil == 1:
                out_specs.append(pl.BlockSpec((STREAM_ROWS, HEAD_DIM), lambda b, h: (b, h)))
                out_shape.append(jax.ShapeDtypeStruct((s, n_heads * HEAD_DIM), BF16))
            else:
                out_specs.append(pl.BlockSpec((dil, STREAM_ROWS // dil, HEAD_DIM), lambda b, h: (0, b, h)))
                out_shape.append(jax.ShapeDtypeStruct((dil, s // dil, n_heads * HEAD_DIM), BF16))
    res = pl.pallas_call(
        body, name=name, grid=(s // STREAM_ROWS, n_heads), in_specs=in_specs, out_specs=out_specs, out_shape=out_shape,
        scratch_shapes=[pltpu.VMEM((STREAM_ROWS, HEAD_DIM), F32)], compiler_params=_cparams(("arbitrary", "arbitrary")),
    )(*[a for (a, _) in srcs], *vecs)
    res = [r.reshape(s, n_heads * HEAD_DIM) for r in res]
    return [res[vi * len(dils) : (vi + 1) * len(dils)] for vi in range(n_vals)]


def _natural_order(name, fn, vals, extras, out_dtypes, n_heads):
    s = vals[0][0].shape[0]
    dils = [dil for _, dil in DILATION_PATTERNS]
    n_in = len(vals) * len(dils) + len(extras)

    def body(*refs):
        outs, scr = refs[n_in:-1], refs[-1]
        blocks = []
        for vi in range(len(vals)):
            per_pattern = []
            for di, dil in enumerate(dils):
                x = refs[vi * len(dils) + di]
                if dil == 1:
                    per_pattern.append(x[...].astype(F32))
                    continue
                for part in range(STREAM_ROWS // (dil * BAND)):
                    for r in range(dil):
                        scr[pl.ds(part * dil * BAND + r, BAND, stride=dil), :] = x[r, part * BAND : (part + 1) * BAND, :].astype(F32)
                per_pattern.append(scr[...])
            blocks.append(per_pattern)
        res = fn(pl.program_id(1), blocks, *[r[...] for r in refs[len(vals) * len(dils) : n_in]])
        for o, v in zip(outs, res):
            o[...] = v.astype(o.dtype)

    in_specs, ins = [], []
    for per_pattern in vals:
        for arr, dil in zip(per_pattern, dils):
            if dil == 1:
                in_specs.append(pl.BlockSpec((STREAM_ROWS, HEAD_DIM), lambda b, h: (b, h)))
                ins.append(arr)
            else:
                in_specs.append(pl.BlockSpec((dil, STREAM_ROWS // dil, HEAD_DIM), lambda b, h: (0, b, h)))
                ins.append(arr.reshape(dil, s // dil, n_heads * HEAD_DIM))
    in_specs += [pl.BlockSpec((STREAM_ROWS, HEAD_DIM), lambda b, h: (b, 0)) for _ in extras]
    return pl.pallas_call(
        body, name=name, grid=(s // STREAM_ROWS, n_heads), in_specs=in_specs,
        out_specs=[pl.BlockSpec((STREAM_ROWS, HEAD_DIM), lambda b, h: (b, h)) for _ in out_dtypes],
        out_shape=[jax.ShapeDtypeStruct((s, n_heads * HEAD_DIM), dt) for dt in out_dtypes],
        scratch_shapes=[pltpu.VMEM((STREAM_ROWS, HEAD_DIM), F32)], compiler_params=_cparams(("arbitrary", "arbitrary")),
    )(*ins, *extras)


def _to_streams(v, dil):
    if dil == 1:
        return v
    s, w = v.shape
    return v.reshape(s // dil, dil, w).transpose(1, 0, 2).reshape(s, w)


def _from_streams(v, dil):
    if dil == 1:
        return v
    s, w = v.shape
    return v.reshape(dil, s // dil, w).transpose(1, 0, 2).reshape(s, w)


SCAN_LANES = 1024
SSM_ROWS = 128
SSM_FWD_ROWS = 256
FF_TM = 1024


def _cmul(ar, ai, br, bi):
    return ar * br - ai * bi, ar * bi + ai * br


def _cmul_conj(ar, ai, br, bi):
    return ar * br + ai * bi, ar * bi - ai * br


def _ssm_fwd(u_src, bre, bim, cre, cim, consts, d_skip):
    u_arr, sw, ucb = u_src
    s = u_arr.shape[0]
    nblk = bre.shape[0]
    nst = nblk * 512
    t = _blk(s, SSM_FWD_ROWS)
    w = _blk(nst, SCAN_LANES)

    def body(u_ref, bre_ref, bim_ref, cre_ref, cim_ref, k_ref, ds_ref, y_ref, hr_ref, hi_ref, xr, xi, car_r, car_i):
        @pl.when(pl.program_id(0) == 0)
        def _():
            car_r[...] = jnp.zeros(car_r.shape, F32)
            car_i[...] = jnp.zeros(car_i.shape, F32)

        u = u_ref[...].astype(F32)
        ub = u_ref[...].astype(BF16)
        for b in range(nblk):
            ch, st = slice(b * 128, (b + 1) * 128), slice(b * 512, (b + 1) * 512)
            xr[:, st] = jnp.dot(ub[:, ch], bre_ref[b], preferred_element_type=F32)
            xi[:, st] = jnp.dot(ub[:, ch], bim_ref[b], preferred_element_type=F32)
        for c in range(nst // w):
            ls = slice(c * w, (c + 1) * w)
            mult = [k_ref[i, :, ls] for i in range(8)]

            def step(i, carry, ls=ls, mult=mult):
                cr, ci = carry
                r0 = pl.multiple_of(i * SUBLANES, SUBLANES)
                hr, hi = xr[pl.ds(r0, SUBLANES), ls], xi[pl.ds(r0, SUBLANES), ls]
                for n, sh in enumerate((1, 2, 4)):
                    pr, pi = _cmul(mult[2 * n], mult[2 * n + 1], pltpu.roll(hr, sh, 0), pltpu.roll(hi, sh, 0))
                    hr, hi = hr + pr, hi + pi
                pr, pi = _cmul(mult[6], mult[7], cr, ci)
                hr, hi = hr + pr, hi + pi
                xr[pl.ds(r0, SUBLANES), ls] = hr
                xi[pl.ds(r0, SUBLANES), ls] = hi
                return jnp.broadcast_to(hr[7:8, :], hr.shape), jnp.broadcast_to(hi[7:8, :], hi.shape)

            cr, ci = lax.fori_loop(0, t // SUBLANES, step, (car_r[:, ls], car_i[:, ls]))
            car_r[:, ls] = cr
            car_i[:, ls] = ci
        hrb, hib = xr[...].astype(BF16), xi[...].astype(BF16)
        hr_ref[...] = hrb
        hi_ref[...] = hib
        for b in range(nblk):
            ch, st = slice(b * 128, (b + 1) * 128), slice(b * 512, (b + 1) * 512)
            yb = jnp.dot(hrb[:, st], cre_ref[b], preferred_element_type=F32)
            yb -= jnp.dot(hib[:, st], cim_ref[b], preferred_element_type=F32)
            y_ref[:, ch] = yb + ds_ref[:, ch] * u[:, ch]

    whole = lambda a: pl.BlockSpec(a.shape, lambda i, nd=a.ndim: (0,) * nd)
    return pl.pallas_call(
        body, name="ssm_fwd", grid=(s // t,),
        in_specs=[pl.BlockSpec((t, sw), lambda i: (i, ucb))] + [whole(a) for a in (bre, bim, cre, cim, consts, d_skip)],
        out_specs=[pl.BlockSpec((t, sw), lambda i: (i, 0)), pl.BlockSpec((t, nst), lambda i: (i, 0)), pl.BlockSpec((t, nst), lambda i: (i, 0))],
        out_shape=[jax.ShapeDtypeStruct((s, sw), F32), jax.ShapeDtypeStruct((s, nst), BF16), jax.ShapeDtypeStruct((s, nst), BF16)],
        scratch_shapes=[pltpu.VMEM((t, nst), F32), pltpu.VMEM((t, nst), F32), pltpu.VMEM((SUBLANES, nst), F32), pltpu.VMEM((SUBLANES, nst), F32)],
        compiler_params=_cparams(("arbitrary",)),
    )(u_arr, bre, bim, cre, cim, consts, d_skip)


def _ssm_bwd(dy, u_src, hr, hi, ctre, ctim, btre, btim, consts, d_skip):
    u_arr, sw, ucb = u_src
    s = dy.shape[0]
    nblk = ctre.shape[0]
    nst = nblk * 512
    t = _blk(s, SSM_ROWS)
    w = _blk(nst, SCAN_LANES)
    nt = s // t

    def body(dy_ref, u_ref, hr_ref, hi_ref, ctre_ref, ctim_ref, btre_ref, btim_ref, k_ref, ds_ref,
             du_ref, db_ref, dc_ref, da_ref, dd_ref, lr, li, hfr, hfi, car_r, car_i):
        @pl.when(pl.program_id(0) == 0)
        def _():
            car_r[...] = jnp.zeros(car_r.shape, F32)
            car_i[...] = jnp.zeros(car_i.shape, F32)
            db_ref[...] = jnp.zeros(db_ref.shape, F32)
            dc_ref[...] = jnp.zeros(dc_ref.shape, F32)
            da_ref[...] = jnp.zeros(da_ref.shape, F32)
            dd_ref[...] = jnp.zeros(dd_ref.shape, F32)

        dyv, u = dy_ref[...].astype(F32), u_ref[...].astype(F32)
        dyb, ub = dyv.astype(BF16), u.astype(BF16)
        hrb, hib = hr_ref[...], hi_ref[...]
        hfr[...] = hrb.astype(F32)
        hfi[...] = hib.astype(F32)
        for b in range(nblk):
            ch, st = slice(b * 128, (b + 1) * 128), slice(b * 512, (b + 1) * 512)
            lr[:, st] = jnp.dot(dyb[:, ch], ctre_ref[b], preferred_element_type=F32)
            li[:, st] = -jnp.dot(dyb[:, ch], ctim_ref[b], preferred_element_type=F32)
        last_row = lax.broadcasted_iota(jnp.int32, (SUBLANES, w), 0) == SUBLANES - 1
        for c in range(nst // w):
            ls = slice(c * w, (c + 1) * w)
            mult = [k_ref[i, :, ls] for i in range(8)]

            def step(i, carry, ls=ls, mult=mult):
                cr, ci, ar, ai = carry
                r0 = pl.multiple_of((t // SUBLANES - 1 - i) * SUBLANES, SUBLANES)
                gr, gi = lr[pl.ds(r0, SUBLANES), ls], li[pl.ds(r0, SUBLANES), ls]
                for n, sh in enumerate((1, 2, 4)):
                    pr, pi = _cmul_conj(mult[2 * n], mult[2 * n + 1], pltpu.roll(gr, SUBLANES - sh, 0), pltpu.roll(gi, SUBLANES - sh, 0))
                    gr, gi = gr + pr, gi + pi
                pr, pi = _cmul_conj(mult[6], mult[7], cr, ci)
                gr, gi = gr + pr, gi + pi
                lr[pl.ds(r0, SUBLANES), ls] = gr
                li[pl.ds(r0, SUBLANES), ls] = gi
                nr = jnp.where(last_row, cr, pltpu.roll(gr, SUBLANES - 1, 0))
                ni = jnp.where(last_row, ci, pltpu.roll(gi, SUBLANES - 1, 0))
                fr, fi = hfr[pl.ds(r0, SUBLANES), ls], hfi[pl.ds(r0, SUBLANES), ls]
                pr, pi = _cmul_conj(fr, fi, nr, ni)
                return (jnp.broadcast_to(gr[0:1, :], gr.shape), jnp.broadcast_to(gi[0:1, :], gi.shape), ar + pr, ai + pi)

            zero = jnp.zeros((SUBLANES, w), F32)
            cr, ci, ar, ai = lax.fori_loop(0, t // SUBLANES, step, (car_r[:, ls], car_i[:, ls], zero, zero))
            car_r[:, ls] = cr
            car_i[:, ls] = ci
            da_ref[0, :, ls] += ar
            da_ref[1, :, ls] += ai
        lrb, lib = lr[...].astype(BF16), li[...].astype(BF16)
        for b in range(nblk):
            ch, st = slice(b * 128, (b + 1) * 128), slice(b * 512, (b + 1) * 512)
            dub = jnp.dot(lrb[:, st], btre_ref[b], preferred_element_type=F32)
            dub += jnp.dot(lib[:, st], btim_ref[b], preferred_element_type=F32)
            du_ref[:, ch] = (dub + ds_ref[:, ch] * dyv[:, ch]).astype(du_ref.dtype)
            db_ref[0, b] += lax.dot_general(ub[:, ch], lrb[:, st], _TN, preferred_element_type=F32)
            db_ref[1, b] += lax.dot_general(ub[:, ch], lib[:, st], _TN, preferred_element_type=F32)
            dc_ref[0, b] += lax.dot_general(dyb[:, ch], hrb[:, st], _TN, preferred_element_type=F32)
            dc_ref[1, b] -= lax.dot_general(dyb[:, ch], hib[:, st], _TN, preferred_element_type=F32)
        dd_ref[...] += _colsum8(dyv * u)

    whole = lambda a: pl.BlockSpec(a.shape, lambda i, nd=a.ndim: (0,) * nd)
    rev = lambda wd, cb=0: pl.BlockSpec((t, wd), lambda i, cb=cb: (nt - 1 - i, cb))
    acc = lambda sh: pl.BlockSpec(sh, lambda i, nd=len(sh): (0,) * nd)
    db_shape, da_shape, dd_shape = (2, nblk, 128, 512), (2, SUBLANES, nst), (SUBLANES, sw)
    return pl.pallas_call(
        body, name="ssm_bwd", grid=(nt,),
        in_specs=[rev(sw), rev(sw, ucb), rev(nst), rev(nst)] + [whole(a) for a in (ctre, ctim, btre, btim, consts, d_skip)],
        out_specs=[rev(sw), acc(db_shape), acc(db_shape), acc(da_shape), acc(dd_shape)],
        out_shape=[jax.ShapeDtypeStruct((s, sw), BF16), jax.ShapeDtypeStruct(db_shape, F32), jax.ShapeDtypeStruct(db_shape, F32),
                   jax.ShapeDtypeStruct(da_shape, F32), jax.ShapeDtypeStruct(dd_shape, F32)],
        scratch_shapes=[pltpu.VMEM((t, nst), F32)] * 4 + [pltpu.VMEM((SUBLANES, nst), F32)] * 2,
        compiler_params=_cparams(("arbitrary",)),
    )(dy, u_arr, hr, hi, ctre, ctim, btre, btim, consts, d_skip)


def _ssm_discretise(lam_re, lam_im, log_step):
    step = jnp.exp(log_step)
    e = jnp.exp(lam_re * step)
    ar, ai = e * jnp.cos(lam_im * step), e * jnp.sin(lam_im * step)
    den = lam_re * lam_re + lam_im * lam_im
    inv_r, inv_i = lam_re / den, -lam_im / den
    fr, fi = _cmul(ar - 1.0, ai, inv_r, inv_i)
    return step, ar, ai, inv_r, inv_i, fr, fi


def _ssm_prep_fn(lam_re, lam_im, log_step):
    _, ar, ai, _, _, fr, fi = _ssm_discretise(lam_re, lam_im, log_step)
    pw = [(ar, ai)]
    for n in range(1, 8):
        pw.append(_cmul(*pw[n - 1], ar, ai))
    return [p[0] for p in pw] + [p[1] for p in pw] + [fr, fi]


def _ssm_param_grad_fn(lam_re, lam_im, log_step, da_r, da_i, df_r, df_i):
    step, ar, ai, inv_r, inv_i, fr, fi = _ssm_discretise(lam_re, lam_im, log_step)
    pr, pi = _cmul_conj(inv_r, inv_i, df_r, df_i)
    dat_r, dat_i = da_r + pr, da_i + pi
    wr, wi = _cmul(fr, fi, inv_r, inv_i)
    dl_r, dl_i = _cmul_conj(-wr, -wi, df_r, df_i)
    dz_r, dz_i = _cmul_conj(ar, ai, dat_r, dat_i)
    dl_r, dl_i = dl_r + step * dz_r, dl_i + step * dz_i
    dstep = jnp.sum(dz_r * lam_re + dz_i * lam_im, axis=-1, keepdims=True)
    return [dl_r, dl_i, step * dstep]


def _block_diag(v, rows_first):
    g, a, b = v.shape
    nb = g // GROUPS_PER_BLOCK
    eye = jnp.eye(GROUPS_PER_BLOCK, dtype=v.dtype)
    v = v.reshape(nb, GROUPS_PER_BLOCK, a, 1, b) * eye[None, :, None, :, None]
    return v.reshape(nb, GROUPS_PER_BLOCK * a, GROUPS_PER_BLOCK * b)


def _block_diag_take(v, a, b):
    nb = v.shape[0]
    v = v.reshape(nb, GROUPS_PER_BLOCK, a, GROUPS_PER_BLOCK, b)
    return jnp.stack([v[:, g, :, g, :] for g in range(GROUPS_PER_BLOCK)], axis=1).reshape(nb * GROUPS_PER_BLOCK, a, b)


def _adamw(w, g, m, v):
    m = ADAM_B1 * m + (1.0 - ADAM_B1) * g
    v = ADAM_B2 * v + (1.0 - ADAM_B2) * (g * g)
    m_hat = m / (1.0 - ADAM_B1**ADAM_STEP)
    v_hat = v / (1.0 - ADAM_B2**ADAM_STEP)
    delta = -ADAM_LR * (m_hat / (jnp.sqrt(v_hat) + ADAM_EPS) + ADAM_WD * w)
    return delta, m, v


def _adamw_rows(name, w, g, m, v):
    wd = w.shape[1]
    return _rowwise(name, lambda a, b, c, d: (list(_adamw(a, b, c, d)), []), [w, g, m, v], [], [(wd, F32)] * 3, [], tm=128)


SMALL = ["b_ada", "norm1_g", "q_norm_g", "k_norm_g", "lam_re", "lam_im", "log_step", "b_re", "b_im", "c_re", "c_im",
         "d_skip", "b_glu", "attn_out_g", "ssm_out_g", "norm2_g"]
LARGE = ["w_in", "w_glu", "w_out", "w_ff1", "w_ff2"]
ORDER = ["w_ada", "b_ada", "norm1_g", "w_in", "q_norm_g", "k_norm_g", "lam_re", "lam_im", "log_step", "b_re", "b_im", "c_re",
         "c_im", "d_skip", "w_glu", "b_glu", "attn_out_g", "ssm_out_g", "w_out", "norm2_g", "w_ff1", "w_ff2"]


def _pack(arrs):
    flat = jnp.concatenate([a.reshape(-1) for a in arrs])
    rows = -(-flat.shape[0] // PACK_LANES)
    rows = -(-rows // SUBLANES) * SUBLANES
    return jnp.pad(flat, (0, rows * PACK_LANES - flat.shape[0])).reshape(rows, PACK_LANES)


def _unpack(packed, shapes):
    flat, out, o = packed.reshape(-1), [], 0
    for sh in shapes:
        n = int(np.prod(sh))
        out.append(flat[o : o + n].reshape(sh))
        o += n
    return out


def kernel(x, c, w_ada, b_ada, norm1_g, w_in, q_norm_g, k_norm_g, lam_re, lam_im, log_step, b_re, b_im, c_re, c_im, d_skip, w_glu, b_glu, attn_out_g, ssm_out_g, w_out, norm2_g, w_ff1, w_ff2, loss_target, m_w_ada, m_b_ada, m_norm1_g, m_w_in, m_q_norm_g, m_k_norm_g, m_lam_re, m_lam_im, m_log_step, m_b_re, m_b_im, m_c_re, m_c_im, m_d_skip, m_w_glu, m_b_glu, m_attn_out_g, m_ssm_out_g, m_w_out, m_norm2_g, m_w_ff1, m_w_ff2, v_w_ada, v_b_ada, v_norm1_g, v_w_in, v_q_norm_g, v_k_norm_g, v_lam_re, v_lam_im, v_log_step, v_b_re, v_b_im, v_c_re, v_c_im, v_d_skip, v_w_glu, v_b_glu, v_attn_out_g, v_ssm_out_g, v_w_out, v_norm2_g, v_w_ff1, v_w_ff2):
    args = dict(locals())
    wts = {n: args[n] for n in ORDER}
    mom = {n: args["m_" + n] for n in ORDER}
    var = {n: args["v_" + n] for n in ORDER}

    x2, tgt = x[0], loss_target[0]
    s, d = x2.shape
    aw = d // 2
    sw = d - aw
    n_groups = sw // SSM_GROUP
    nst = n_groups * STATE_DIM
    chip = 2 * lax.axis_index("x") + lax.axis_index("y")

    chip_idx = jnp.reshape(chip, (1,)).astype(jnp.int32)
    core_idx = jnp.reshape(lax.axis_index("c"), (1,)).astype(jnp.int32)

    def cast_into_stack(n):
        w2 = wts[n][0]
        rows, cols = w2.shape
        tr = _blk(rows, 256)
        return _indexed("cast_" + n, lambda a: a, chip_idx, [(w2, (tr, cols), lambda r, ix: (r, 0))],
                        (((4, rows, cols), BF16), (None, tr, cols), lambda r, ix: (ix[0], r, 0)), (rows // tr,))

    stacks = {n: cast_into_stack(n) for n in LARGE}
    dff = 4 * wts["w_ff2"].shape[1]

    c_all = _allgather8("gather_c", jnp.pad(c, ((0, SUBLANES - 1), (0, 0))))[::SUBLANES]
    n_mod = w_ada.shape[2]
    b_ada_mine = lax.dynamic_slice(b_ada, (0, chip * n_mod), (1, n_mod))
    sil_c = _single("silu_c", lambda v: [v * _sigmoid(v)], [c_all], [(c_all.shape, F32)])[0]
    mod_part = _mm("ada_fwd", sil_c, w_ada[0], "nn", F32, tn=512, epi=lambda acc, b: acc + b[0:1, :],
                   extras=[jnp.broadcast_to(b_ada_mine, (SUBLANES, n_mod))])
    mod_all = _allgather8("gather_mod", mod_part)
    me = 2 * chip + lax.axis_index("c")
    mod = jnp.concatenate([lax.dynamic_slice(mod_all, (16 * k + me, 0), (1, n_mod)) for k in range(4)], axis=1)
    sh1, sc1, g1, sh2, sc2, g2 = [mod[:, i * d : (i + 1) * d] for i in range(6)]

    (w_in_s,) = _gather_weights([stacks["w_in"]], mod_all)

    def gather_plan(refs):
        px0, py0, pc, chips = _place()
        slot = 2 * px0 + py0
        return [(r.at[slot], r.at[slot], (px, py, pc), r.at[2 * px + py]) for r in refs for (px, py) in chips]

    mix_sems_s, mix_sems_r, mix_stacks, mix_token = _split_start(
        "gather_mix_start", [stacks["w_glu"], stacks["w_out"]], gather_plan, 6, extra=[w_in_s])
    w_in_s = mix_stacks.pop()
    ff_sems_s, ff_sems_r, ff_stacks, ff_token = _split_start(
        "gather_ff_start", [stacks["w_ff1"], stacks["w_ff2"]], gather_plan, 6, extra=[w_in_s])
    w_in_s = ff_stacks.pop()
    sc1 = sc1 + (mix_token[0:1, 0:1] + ff_token[0:1, 0:1])

    def norm_mod(v, g, sc, sh):
        return [v * _rstd(v) * g * (1.0 + sc) + sh], []

    (h1b,) = _rowwise("norm1", norm_mod, [x2], [norm1_g, sc1, sh1], [(d, BF16)], [])
    proj = _mm("proj_in", h1b, w_in_s, "nn", BF16, b_cs=True, tm=FF_TM)
    assert aw == sw
    q_src, k_src, v_src, u_src = [(proj, aw, i) for i in range(3)] + [(proj, sw, 3)]

    n_heads = aw // HEAD_DIM
    up = lambda v: v.astype(F32)
    qk_orders = _stream_orders("qk_prep", lambda q, k, v, gq, gk: [up(q) * _rstd(up(q)) * gq, up(k) * _rstd(up(k)) * gk, up(v)],
                               [(proj, 0), (proj, n_heads), (proj, 2 * n_heads)], [q_norm_g, k_norm_g], 3, n_heads)

    pats = []
    for pi, (window, dil) in enumerate(DILATION_PATTERNS):
        assert window // dil == BAND and s % STREAM_ROWS == 0
        bps = s // dil // BAND
        qp, kp, vp = [order[pi] for order in qk_orders]
        o_p, l_p = _attn_fwd(f"attn_fwd_d{dil}", qp, kp, vp, bps, dil)
        pats.append((dil, bps, qp, kp, vp, o_p, _from_streams(l_p, dil)))

    def lse_total(l1, l2, l3):
        m = jnp.maximum(jnp.maximum(l1, l2), l3)
        return m + jnp.log(jnp.exp(l1 - m) + jnp.exp(l2 - m) + jnp.exp(l3 - m))

    (lse,) = _rowwise("attn_lse", lambda l1, l2, l3: ([lse_total(l1, l2, l3)], []), [p[6] for p in pats], [], [(HEAD_DIM, F32)], [])

    def combine(h, blocks, l1, l2, l3):
        tot = lse_total(l1, l2, l3)
        lane = lax.broadcasted_iota(jnp.int32, l1.shape, 1)
        mix = 0.0
        for o_blk, l in zip(blocks[0], (l1, l2, l3)):
            w = jnp.sum(jnp.where(lane == h, jnp.exp(l - tot), 0.0), axis=1, keepdims=True)
            mix = mix + w * o_blk
        return [mix]

    (attn,) = _natural_order("attn_mix", combine, [[p[5] for p in pats]], [p[6] for p in pats], [F32], n_heads)

    lam_re2, lam_im2, log_step2 = lam_re[0], lam_im[0], log_step[0].reshape(n_groups, 1)
    prep = _single("ssm_prep", _ssm_prep_fn, [lam_re2, lam_im2, log_step2], [((n_groups, STATE_DIM), F32)] * 18)
    p_r = jnp.stack([p.reshape(nst) for p in prep[:8]])
    p_i = jnp.stack([p.reshape(nst) for p in prep[8:16]])
    f_ri = jnp.stack(prep[16:])
    row = jnp.arange(SUBLANES)[:, None]
    zero = jnp.zeros((SUBLANES, nst), F32)
    sel = lambda cond, v: jnp.where(cond, jnp.broadcast_to(v[None, :], (SUBLANES, nst)), zero)
    fwd_consts = jnp.stack([sel(row >= 1, p_r[0]), sel(row >= 1, p_i[0]), sel(row >= 2, p_r[1]), sel(row >= 2, p_i[1]),
                            sel(row >= 4, p_r[3]), sel(row >= 4, p_i[3]), p_r, p_i])
    bwd_consts = jnp.stack([sel(row <= 6, p_r[0]), sel(row <= 6, p_i[0]), sel(row <= 5, p_r[1]), sel(row <= 5, p_i[1]),
                            sel(row <= 3, p_r[3]), sel(row <= 3, p_i[3]), p_r[::-1], p_i[::-1]])
    nblk = n_groups // GROUPS_PER_BLOCK
    f_rows = f_ri.reshape(2, nblk, 1, 512)
    braw_r = _block_diag(b_re[0].transpose(0, 2, 1), True)
    braw_i = _block_diag(b_im[0].transpose(0, 2, 1), True)

    def bbar_fn(br, bi, f):
        r, i = _cmul(f[0], f[1], br, bi)
        return [r, i]

    bbar_r, bbar_i = _single("ssm_bbar", bbar_fn, [braw_r, braw_i, f_rows], [(braw_r.shape, BF16)] * 2)
    ct_r, ct_i = _block_diag(c_re[0], True).astype(BF16), _block_diag(c_im[0], True).astype(BF16)
    ds2 = d_skip
    y_ssm, h_r, h_i = _ssm_fwd(u_src, bbar_r, bbar_i, ct_r.transpose(0, 2, 1), ct_i.transpose(0, 2, 1), fwd_consts, ds2)

    (yb,) = _rowwise("gelu", lambda v: ([_gelu(v)], []), [y_ssm], [], [(sw, BF16)], [])
    w_glu_s, w_out_s = _split_wait("gather_mix_wait", mix_stacks, mix_sems_s, mix_sems_r, yb, gather_plan)
    w_glu_f = w_glu_s.reshape(sw, sw)
    w_out_f = w_out_s.reshape(d, d)
    z = _mm("glu_fwd", yb, w_glu_f, "nn", BF16)

    def glu(ys, zz, b):
        return [_gelu(ys) * _sigmoid(zz.astype(F32) + b)], []

    (ssm,) = _rowwise("glu", glu, [y_ssm, z], [b_glu], [(sw, F32)], [])

    def cat_norm(a, sm, ga, gs):
        return [jnp.concatenate([a * _rstd(a) * ga, sm * _rstd(sm) * gs], axis=1)], []

    (catb,) = _rowwise("cat_norm", cat_norm, [attn, ssm], [attn_out_g, ssm_out_g], [(d, BF16)], [])
    mixed = _mm("mix_out", catb, w_out_f, "nn", BF16, tm=FF_TM)

    def resid_norm(xv, mx, g1v, g, sc, sh):
        x1v = xv + g1v * mx.astype(F32)
        return [x1v, x1v * _rstd(x1v) * g * (1.0 + sc) + sh], []

    x1, h2b = _rowwise("resid_norm2", resid_norm, [x2, mixed], [g1, norm2_g, sc2, sh2], [(d, F32), (d, BF16)], [])
    w_ff1_s, w_ff2_s = _split_wait("gather_ff_wait", ff_stacks, ff_sems_s, ff_sems_r, x1, gather_plan)
    w_ff2_f = w_ff2_s.reshape(dff, d)
    a_ff = _mm("ff1", h2b, w_ff1_s, "nn", BF16, b_cs=True, tm=FF_TM)
    sq_relu = lambda v: jnp.square(jnp.maximum(v.astype(F32), 0.0)).astype(BF16)
    ff = _mm("ff2", a_ff, w_ff2_f, "nn", BF16, a_pro=sq_relu, tm=FF_TM)

    def loss_fn(x1v, ffv, tg, g2v):
        ffv = ffv.astype(F32)
        diff = x1v + g2v * ffv - tg
        dout = diff * (1.0 / d)
        return [dout, dout * g2v], [_colsum8(0.5 * diff * dout), _colsum8(dout * ffv)]

    dout, dffb, loss_acc, dg2_acc = _rowwise("loss", loss_fn, [x1, ff, tgt], [g2], [(d, F32), (d, BF16)], [(SUBLANES, d)] * 2)

    da_ff = _mm("ff2_bwd", dffb, w_ff2_f, "nt", BF16, epi=lambda acc, av: acc * (2.0 * jnp.maximum(av.astype(F32), 0.0)), extras=[a_ff], tm=FF_TM)
    gw = {}
    gw["w_ff2"] = _mm("ff2_wgrad", a_ff, dffb, "tn", BF16, a_pro=sq_relu, tm=FF_TM).reshape(4, dff // 4, d)
    gw["w_ff1"] = _mm("ff1_wgrad", h2b, da_ff, "tn", BF16, o_cs=True, tm=FF_TM)

    def add2(n, g, b):
        _, half, cols = b.shape
        tr = _blk(half, 256)
        nrb = half // tr
        blk = (None, tr, cols)
        return _indexed("grad_add2_" + n, lambda p, q: p.astype(F32) + q.astype(F32), core_idx,
                        [(g, blk, lambda k, r, ix: (k, ix[0] * nrb + r, 0)), (b, blk, lambda k, r, ix: (k, r, 0))],
                        ((b.shape, BF16), blk, lambda k, r, ix: (k, r, 0)), (4, nrb))

    def scatter_plan(refs):
        px0, py0, pc, chips = _place()
        nw = len(refs) // 2
        return [(refs[w].at[2 * px + py], refs[nw + w].at[j], (px, py, pc), refs[nw + w].at[j])
                for w in range(nw) for j, (px, py) in enumerate(chips)]

    def swap_plan(refs):
        px0, py0, pc, _ = _place()
        nw = len(refs) // 2
        plan = []
        for w in range(nw):
            half = refs[w].shape[1] // 2
            plan.append((refs[w].at[:, pl.ds((1 - pc) * half, half), :], refs[nw + w], (px0, py0, 1 - pc), refs[nw + w]))
        return plan

    early = ["w_ff2", "w_ff1"]
    sw_lands = [lax.empty((4, gw[n].shape[1] // 2, gw[n].shape[2]), BF16) for n in early]
    sw_sems_s, sw_sems_r, sw_bufs, sw_token = _split_start("swap_ff_start", [gw[n] for n in early] + sw_lands, swap_plan, len(early))
    sc2_late = sc2 + sw_token[0:1, 0:1]

    dh2 = _mm("ff1_bwd", da_ff, w_ff1_s, "nt", BF16, b_cs=True, tm=FF_TM)

    def norm2_bwd(dh, x1v, do, mx, g, sc, g1v):
        dh, mx = dh.astype(F32), mx.astype(F32)
        r = _rstd(x1v)
        xh = x1v * r
        dn = dh * (1.0 + sc)
        dx = do + _rms_bwd(dn * g, xh, r)
        return [dx, dx * g1v], [_colsum8(dh * xh * g), _colsum8(dh), _colsum8(dn * xh), _colsum8(dx * mx)]

    dx1, dmixb, dsc2_acc, dsh2_acc, dn2g_acc, dg1_acc = _rowwise(
        "norm2_bwd", norm2_bwd, [dh2, x1, dout, mixed], [norm2_g, sc2_late, g1], [(d, F32), (d, BF16)], [(SUBLANES, d)] * 4)

    sw_bufs = _split_wait("swap_ff_wait", sw_bufs, sw_sems_s, sw_sems_r, dmixb, swap_plan)
    parts = {n: add2(n, g, b) for n, g, b in zip(early, sw_bufs[: len(early)], sw_bufs[len(early) :])}
    lands = [lax.empty((3,) + parts[n].shape[1:], BF16) for n in early]
    sc_sems_s, sc_sems_r, sc_bufs, sc_token = _split_start("scatter_ff_start", [parts[n] for n in early] + lands, scatter_plan, 6)
    attn_out_g_late = attn_out_g + sc_token[0:1, 0:1]

    dcat = _mm("mix_out_bwd", dmixb, w_out_f, "nt", BF16, tm=FF_TM)
    gw["w_out"] = _mm("mix_out_wgrad", catb, dmixb, "tn", BF16, tm=FF_TM).reshape(4, d // 4, d)

    def cat_bwd(da, dsm, a, sm, ga, gs):
        da, dsm = da.astype(F32), dsm.astype(F32)
        ra, rs = _rstd(a), _rstd(sm)
        ah, sh = a * ra, sm * rs
        return [_rms_bwd(da * ga, ah, ra), _rms_bwd(dsm * gs, sh, rs)], [_colsum8(da * ah), _colsum8(dsm * sh)]

    dattn, dssm, dga_acc, dgs_acc = _rowwise(
        "cat_norm_bwd", cat_bwd, [(dcat, aw, 0), (dcat, sw, 1), attn, ssm], [attn_out_g_late, ssm_out_g], [(aw, F32), (sw, F32)],
        [(SUBLANES, aw), (SUBLANES, sw)])

    def glu_bwd(dsm, ys, zz, b):
        sg = _sigmoid(zz.astype(F32) + b)
        dz = dsm * _gelu(ys) * sg * (1.0 - sg)
        return [dz, dsm * sg], [_colsum8(dz)]

    dzb, dy_direct, dbglu_acc = _rowwise("glu_bwd", glu_bwd, [dssm, y_ssm, z], [b_glu], [(sw, BF16), (sw, BF16)], [(SUBLANES, sw)])
    gw["w_glu"] = _mm("glu_wgrad", yb, dzb, "tn", BF16).reshape(4, sw // 4, sw)
    dys = _mm("glu_bwd_mm", dzb, w_glu_f, "nt", BF16, epi=lambda acc, direct, ys: (acc + direct.astype(F32)) * _gelu_grad(ys), extras=[dy_direct, y_ssm])
    du, db_bar, dc_bd, da_acc, dds_acc = _ssm_bwd(dys, u_src, h_r, h_i, ct_r, ct_i, bbar_r.transpose(0, 2, 1),
                                                  bbar_i.transpose(0, 2, 1), bwd_consts, ds2)

    def delta_fn(da, a):
        return [_to_lanes([jnp.sum(p * q, axis=1, keepdims=True) for p, q in zip(_heads(da), _heads(a))])], []

    (delta,) = _rowwise("attn_delta", delta_fn, [dattn, attn], [], [(HEAD_DIM, F32)], [])
    (do_orders,) = _stream_orders("attn_do", lambda da: [da], [(dattn, 0)], [], 1, n_heads)
    dqs, dks, dvs = [], [], []
    for pi, (dil, bps, qp, kp, vp, _, _) in enumerate(pats):
        lsp, dlp = [_to_streams(t, dil) for t in (lse, delta)]
        dq_p, dk_p, dv_p = _attn_bwd(f"attn_bwd_d{dil}", qp, kp, vp, do_orders[pi], lsp, dlp, bps, dil)
        dqs.append(dq_p)
        dks.append(dk_p)
        dvs.append(dv_p)
    dq_t, dk_t, dv_t = _natural_order("attn_grad_sum", lambda h, blocks: [(b[0] + b[1]) + b[2] for b in blocks],
                                      [dqs, dks, dvs], [], [BF16] * 3, n_heads)

    def qk_bwd(dq_sum, dk_sum, dv_sum, duv, q, k, gq, gk):
        def one(dn, raw, g):
            outs, acc = [], 0.0
            for dh, h in zip(_heads(dn), _heads(raw)):
                r = _rstd(h)
                hh = h * r
                outs.append(_rms_bwd(dh * g, hh, r))
                acc = acc + _colsum8(dh * hh)
            return jnp.concatenate(outs, axis=1), acc

        dq, gq_acc = one(dq_sum.astype(F32), q.astype(F32), gq)
        dk, gk_acc = one(dk_sum.astype(F32), k.astype(F32), gk)
        return [jnp.concatenate([dq, dk, dv_sum.astype(F32), duv.astype(F32)], axis=1)], [gq_acc, gk_acc]

    dprojb, dgq_acc, dgk_acc = _rowwise("qk_norm_bwd", qk_bwd, [dq_t, dk_t, dv_t, du, q_src, k_src], [q_norm_g, k_norm_g],
                                        [(3 * aw + sw, BF16)], [(SUBLANES, HEAD_DIM)] * 2)
    gw["w_in"] = _mm("proj_in_wgrad", h1b, dprojb, "tn", BF16, o_cs=True, tm=FF_TM)
    late = [n for n in LARGE if n not in early]
    parts.update({n: add2(n, gw[n], b) for n, b in zip(late, _swap_halves([gw[n] for n in late], "swap_halves_rest"))})
    lands_late = [lax.empty((3,) + parts[n].shape[1:], BF16) for n in late]
    rest_sems_s, rest_sems_r, rest_bufs, rest_token = _split_start(
        "scatter_rest_start", [parts[n] for n in late] + lands_late, scatter_plan, 3 * len(late))
    sc1_late = sc1 + rest_token[0:1, 0:1]
    dh1 = _mm("proj_in_bwd", dprojb, w_in_s, "nt", BF16, b_cs=True, tm=FF_TM)

    def norm1_bwd(dh, xv, dxr, g, sc):
        dh = dh.astype(F32)
        r = _rstd(xv)
        xh = xv * r
        dn = dh * (1.0 + sc)
        return [dxr + _rms_bwd(dn * g, xh, r)], [_colsum8(dh * xh * g), _colsum8(dh), _colsum8(dn * xh)]

    grad_x, dsc1_acc, dsh1_acc, dn1g_acc = _rowwise("norm1_bwd", norm1_bwd, [dh1, x2, dx1], [norm1_g, sc1_late], [(d, F32)], [(SUBLANES, d)] * 3)

    def b_grad_fn(dbb, br, bi, f):
        fr, fi = f[0], f[1]
        dbr, dbi = _cmul_conj(fr, fi, dbb[0], dbb[1])
        dfr = jnp.sum(dbb[0] * br + dbb[1] * bi, axis=1, keepdims=True)
        dfi = jnp.sum(dbb[1] * br - dbb[0] * bi, axis=1, keepdims=True)
        return [dbr, dbi, dfr, dfi]

    dbr_bd, dbi_bd, df_r, df_i = _single("ssm_b_grad", b_grad_fn, [db_bar, braw_r, braw_i, f_rows],
                                         [(braw_r.shape, F32)] * 2 + [((nblk, 1, 512), F32)] * 2)

    def sums_fn(*accs):
        return [jnp.sum(a, axis=-2, keepdims=True) for a in accs]

    acc_list = [dsh1_acc, dsc1_acc, dg1_acc, dsh2_acc, dsc2_acc, dg2_acc, dn1g_acc, dgq_acc, dgk_acc, dds_acc, dbglu_acc, dga_acc,
                dgs_acc, dn2g_acc, loss_acc, da_acc]
    sums = _single("small_sums", sums_fn, acc_list, [(a.shape[:-2] + (1, a.shape[-1]), F32) for a in acc_list])
    (dsh1, dsc1, dg1, dsh2, dsc2, dg2, dn1g, dgq, dgk, dds, dbglu, dga, dgs, dn2g, loss_row, da_sum) = sums
    loss_dev = _single("loss_sum", lambda v: [jnp.sum(v, axis=1, keepdims=True)], [loss_row], [((1, 1), F32)])[0]
    gp = (n_groups, STATE_DIM)
    dlam_re, dlam_im, dlog_step = _single(
        "ssm_param_grad", _ssm_param_grad_fn,
        [lam_re2, lam_im2, log_step2, da_sum[0].reshape(gp), da_sum[1].reshape(gp), df_r.reshape(gp), df_i.reshape(gp)],
        [(gp, F32), (gp, F32), ((n_groups, 1), F32)])
    small_dev = {
        "b_ada": jnp.concatenate([dsh1, dsc1, dg1, dsh2, dsc2, dg2], axis=1),
        "norm1_g": dn1g, "q_norm_g": dgq, "k_norm_g": dgk, "lam_re": dlam_re, "lam_im": dlam_im, "log_step": dlog_step,
        "b_re": _block_diag_take(dbr_bd, SSM_GROUP, STATE_DIM).transpose(0, 2, 1),
        "b_im": _block_diag_take(dbi_bd, SSM_GROUP, STATE_DIM).transpose(0, 2, 1),
        "c_re": _block_diag_take(dc_bd[0], SSM_GROUP, STATE_DIM), "c_im": _block_diag_take(dc_bd[1], SSM_GROUP, STATE_DIM),
        "d_skip": dds, "b_glu": dbglu, "attn_out_g": dga, "ssm_out_g": dgs, "norm2_g": dn2g,
    }

    packed = _pack([small_dev[n] for n in SMALL])
    rows = packed.shape[0]
    tr = _blk(rows, 256)
    me_idx = jnp.reshape(me, (1,)).astype(jnp.int32)
    gathered = _indexed("pack_small", lambda a: a, me_idx, [(packed, (tr, PACK_LANES), lambda r, ix: (r, 0))],
                        (((8, rows, PACK_LANES), F32), (None, tr, PACK_LANES), lambda r, ix: (ix[0], r, 0)), (rows // tr,))

    def small_plan(refs):
        (g,) = refs
        px0, py0, pc0, _ = _place()
        flip = lambda v, bit: 1 - v if bit else v
        plan = []
        for mask in range(1, 8):
            px, py, pc = flip(px0, mask & 4), flip(py0, mask & 2), flip(pc0, mask & 1)
            mine = g.at[4 * px0 + 2 * py0 + pc0]
            plan.append((mine, mine, (px, py, pc), g.at[4 * px + 2 * py + pc]))
        return plan

    small_sems_s, small_sems_r, (gathered,), small_token = _split_start("gather_small_start", [gathered], small_plan, 7)

    sc_bufs = _split_wait("scatter_ff_wait", sc_bufs, sc_sems_s, sc_sems_r, small_token, scatter_plan)
    parts.update(zip(early, sc_bufs[: len(early)]))
    got2 = dict(zip(early, sc_bufs[len(early) :]))
    rest_bufs = _split_wait("scatter_rest_wait", rest_bufs, rest_sems_s, rest_sems_r, small_token, scatter_plan)
    parts.update(zip(late, rest_bufs[: len(late)]))
    got2.update(zip(late, rest_bufs[len(late) :]))
    place_idx = jnp.stack([chip, lax.axis_index("c")]).astype(jnp.int32)

    def add4(n, p, b):
        _, half, cols = p.shape
        tr = _blk(half, 256)
        blk = (None, tr, cols)
        fn = lambda own, q0, q1, q2: ((own.astype(F32) + q0.astype(F32)) + q1.astype(F32)) + q2.astype(F32)
        srcs = [(p, blk, lambda r, ix: (ix[0], r, 0))] + [(b, blk, lambda r, ix, j=j: (j, r, 0)) for j in range(3)]
        return _indexed("grad_add4_" + n, fn, place_idx, srcs, (((2, half, cols), F32), blk, lambda r, ix: (ix[1], r, 0)), (half // tr,))

    def join_plan(refs):
        px0, py0, pc, _ = _place()
        return [(r.at[pc], r.at[pc], (px0, py0, 1 - pc), r.at[1 - pc]) for r in refs]

    join_sems_s, join_sems_r, fulls, join_token = _split_start(
        "join_start", [add4(n, parts[n], got2[n]) for n in LARGE], join_plan, len(LARGE))
    (gathered,) = _split_wait("gather_small_wait", [gathered], small_sems_s, small_sems_r, join_token, small_plan)

    def sum8_fn(g):
        acc = g[0]
        for i in range(1, 8):
            acc = acc + g[i]
        return [acc]

    g_small = _single("sum_small", sum8_fn, [gathered], [((rows, PACK_LANES), F32)])[0]

    w_small, m_small, v_small = [_pack([src[n] for n in SMALL]) for src in (wts, mom, var)]
    upd_small = _adamw_rows("adamw_small", w_small, g_small, m_small, v_small)
    shapes = [wts[n].shape for n in SMALL]
    out = {}
    for kind, arr in zip(("grad", "delta", "new_m", "new_v"), (g_small,) + tuple(upd_small)):
        for n, val in zip(SMALL, _unpack(arr, shapes)):
            out[kind, n] = val

    dmod_all = gathered[:, 0 : -(-6 * d // PACK_LANES), :].reshape(8, -1)[:, : 6 * d]
    dmod_mine = lax.dynamic_slice(dmod_all, (0, chip * n_mod), (8, n_mod))
    g_ada = _mm("ada_wgrad", sil_c, dmod_mine, "tn", F32, tm=512, tn=512)
    upd = _adamw_rows("adamw_w_ada", w_ada[0], g_ada, m_w_ada[0], v_w_ada[0])
    for kind, val in zip(("grad", "delta", "new_m", "new_v"), (g_ada,) + tuple(upd)):
        out[kind, "w_ada"] = val[None]

    fulls = _split_wait("join_wait", fulls, join_sems_s, join_sems_r, upd[0], join_plan)
    for n, full in zip(LARGE, fulls):
        g = full.reshape(wts[n].shape[1:])
        upd = _adamw_rows("adamw_" + n, wts[n][0], g, mom[n][0], var[n][0])
        for kind, val in zip(("grad", "delta", "new_m", "new_v"), (g,) + tuple(upd)):
            out[kind, n] = val[None]

    loss = lax.psum(loss_dev[0, 0], ("x", "y", "c"))
    return (loss, grad_x[None], *[out[kind, n] for kind in ("grad", "delta", "new_m", "new_v") for n in ORDER])
```

```python
import functools
import math

import jax
import jax.numpy as jnp
import numpy as np
from jax import lax
from jax.experimental import pallas as pl
from jax.experimental.pallas import tpu as pltpu

F32 = jnp.float32
BF16 = jnp.bfloat16
MESH = pl.DeviceIdType.MESH

EPS = 1e-6
HEAD_DIM = 128
SSM_GROUP = 16
STATE_DIM = 64
GROUPS_PER_BLOCK = 8
DILATION_PATTERNS = ((128, 1), (512, 4), (2048, 16))
BAND = 128
NEG = -1e30

ADAM_LR, ADAM_B1, ADAM_B2, ADAM_EPS, ADAM_WD, ADAM_STEP = 0.001, 0.9, 0.999, 1e-08, 0.01, 10

V7X_VMEM_BYTES = 64 * 2**20
VMEM_LIMIT = 48 * 2**20
SUBLANES = 8
PACK_LANES = 1024


def _cparams(sem=None):
    return pltpu.CompilerParams(dimension_semantics=sem, vmem_limit_bytes=VMEM_LIMIT)


def _blk(n, want):
    b = min(n, want)
    while n % b:
        b //= 2
    return b


def _colsum8(v):
    tm, w = v.shape
    return v.reshape(tm // SUBLANES, SUBLANES, w).sum(axis=0)


def _rowwise(name, fn, rows, vecs, out_rows, out_accs, tm=256, n_rows=None):
    rows = [r if isinstance(r, tuple) else (r, r.shape[1], 0) for r in rows]
    rows = [r if len(r) == 4 else r + (0,) for r in rows]
    s = rows[0][0].shape[0] if n_rows is None else n_rows
    tm = _blk(s, tm)
    assert all(r[3] % tm == 0 for r in rows)
    n_in, n_or = len(rows) + len(vecs), len(out_rows)

    def body(*refs):
        outs, accs = fn(*[r[...] for r in refs[:n_in]])
        for r, v in zip(refs[n_in : n_in + n_or], outs):
            r[...] = v.astype(r.dtype)
        if out_accs:
            acc_refs = refs[n_in + n_or :]

            @pl.when(pl.program_id(0) == 0)
            def _():
                for r in acc_refs:
                    r[...] = jnp.zeros(r.shape, r.dtype)

            for r, v in zip(acc_refs, accs):
                r[...] += v

    in_specs = [pl.BlockSpec((tm, w), lambda i, cb=cb, rb=r0 // tm: (i + rb, cb)) for (_, w, cb, r0) in rows]
    in_specs += [pl.BlockSpec(v.shape, lambda i, nd=v.ndim: (0,) * nd) for v in vecs]
    out_specs = [pl.BlockSpec((tm, w), lambda i: (i, 0)) for (w, _) in out_rows]
    out_specs += [pl.BlockSpec(sh, lambda i, nd=len(sh): (0,) * nd) for sh in out_accs]
    out_shape = [jax.ShapeDtypeStruct((s, w), dt) for (w, dt) in out_rows]
    out_shape += [jax.ShapeDtypeStruct(sh, F32) for sh in out_accs]
    res = pl.pallas_call(
        body, name=name, grid=(s // tm,), in_specs=in_specs, out_specs=out_specs, out_shape=out_shape,
        compiler_params=_cparams(("arbitrary",)),
    )(*[r[0] for r in rows], *vecs)
    return res


def _single(name, fn, ins, out_shapes):
    n_in = len(ins)

    def body(*refs):
        outs = fn(*[r[...] for r in refs[:n_in]])
        for r, v in zip(refs[n_in:], outs):
            r[...] = v.astype(r.dtype)

    vm = pl.BlockSpec(memory_space=pltpu.VMEM)
    return pl.pallas_call(
        body, name=name, in_specs=[vm] * n_in, out_specs=[vm] * len(out_shapes),
        out_shape=[jax.ShapeDtypeStruct(sh, dt) for sh, dt in out_shapes], compiler_params=_cparams(),
    )(*ins)


_NN = (((1,), (0,)), ((), ()))
_NT = (((1,), (1,)), ((), ()))
_TN = (((0,), (0,)), ((), ()))


def _mm(name, a, b, mode, out_dtype, *, tm=512, tn=1024, tk=2048, b_cs=False, o_cs=False, epi=None, extras=(), a_pro=None):
    if mode == "tn":
        k, m = a.shape
        n = b.shape[1]
    else:
        m, k = a.shape
        if mode == "nn":
            n = b.shape[2] * 4 if b_cs else b.shape[1]
        else:
            n = b.shape[1] if b_cs else b.shape[0]
    tm = _blk(m, tm)
    tn = _blk(n // 4 if (b_cs and mode == "nn") or o_cs else n, tn)
    tk = _blk(k // 4 if (b_cs and mode == "nt") else k, tk)
    nk = k // tk
    dims = {"nn": _NN, "nt": _NT, "tn": _TN}[mode]
    n_ex = len(extras)

    def body(*refs):
        a_ref, b_ref = refs[0], refs[1]
        ex_refs, o_ref = refs[2 : 2 + n_ex], refs[2 + n_ex]

        def finish(acc):
            if epi is not None:
                acc = epi(acc, *[r[...] for r in ex_refs])
            o_ref[...] = acc.astype(o_ref.dtype)

        a_blk = a_ref[...] if a_pro is None else a_pro(a_ref[...])
        part = lax.dot_general(a_blk, b_ref[...], dims, preferred_element_type=F32)
        if nk == 1:
            finish(part)
        else:
            acc_ref = refs[3 + n_ex]
            kk = pl.program_id(2)

            @pl.when(kk == 0)
            def _():
                acc_ref[...] = part

            @pl.when(kk > 0)
            def _():
                acc_ref[...] += part

            @pl.when(kk == nk - 1)
            def _():
                finish(acc_ref[...])

    a_spec = pl.BlockSpec((tk, tm), lambda i, j, kk: (kk, i)) if mode == "tn" else pl.BlockSpec((tm, tk), lambda i, j, kk: (i, kk))
    if mode == "nn":
        if b_cs:
            per = (n // 4) // tn
            b_spec = pl.BlockSpec((None, tk, tn), lambda i, j, kk: (j // per, kk, j % per))
        else:
            b_spec = pl.BlockSpec((tk, tn), lambda i, j, kk: (kk, j))
    elif mode == "nt":
        if b_cs:
            per = (k // 4) // tk
            b_spec = pl.BlockSpec((None, tn, tk), lambda i, j, kk: (kk // per, j, kk % per))
        else:
            b_spec = pl.BlockSpec((tn, tk), lambda i, j, kk: (j, kk))
    else:
        b_spec = pl.BlockSpec((tk, tn), lambda i, j, kk: (kk, j))
    if o_cs:
        per = (n // 4) // tn
        o_spec = pl.BlockSpec((None, tm, tn), lambda i, j, kk: (j // per, i, j % per))
        o_shape = jax.ShapeDtypeStruct((4, m, n // 4), out_dtype)
    else:
        o_spec = pl.BlockSpec((tm, tn), lambda i, j, kk: (i, j))
        o_shape = jax.ShapeDtypeStruct((m, n), out_dtype)
    ex_specs = [pl.BlockSpec((tm, tn), lambda i, j, kk: (i, j)) for _ in extras]
    return pl.pallas_call(
        body, name=name, grid=(m // tm, n // tn, nk), in_specs=[a_spec, b_spec] + ex_specs, out_specs=o_spec, out_shape=o_shape,
        scratch_shapes=[pltpu.VMEM((tm, tn), F32)] if nk > 1 else [],
        compiler_params=_cparams(("parallel", "parallel", "arbitrary")),
    )(a, b, *extras)


def _place():
    x, y, c = lax.axis_index("x"), lax.axis_index("y"), lax.axis_index("c")
    return x, y, c, [(1 - x, y), (x, 1 - y), (1 - x, 1 - y)]


def _allgather8(name, v):
    m_per, n = v.shape

    def body(x_ref, out_ref, send_sems, recv_sems, local_sem):
        x, y, c, chips = _place()
        me, sibling = (x, y, c), (x, y, 1 - c)

        def rows(px, py, pc):
            return out_ref.at[pl.ds((4 * px + 2 * py + pc) * m_per, m_per), :]

        def copy(k, block, to, src=None):
            return pltpu.make_async_remote_copy(
                src_ref=rows(*block) if src is None else src, dst_ref=rows(*block), send_sem=send_sems.at[k],
                recv_sem=recv_sems.at[k], device_id=to, device_id_type=MESH)

        mine = pltpu.make_async_copy(x_ref, rows(*me), local_sem)
        mine.start()
        first = [copy(0, me, sibling, src=x_ref)]
        first += [copy(1 + j, me, (*chip, c), src=x_ref) for j, chip in enumerate(chips)]
        for cp in first:
            cp.start()
        passed = [copy(4 + j, (*chip, c), sibling) for j, chip in enumerate(chips)]
        for j, chip in enumerate(chips):
            copy(1 + j, (*chip, c), me).wait_recv()
            passed[j].start()
        copy(0, sibling, me).wait_recv()
        for j, chip in enumerate(chips):
            copy(4 + j, (*chip, 1 - c), me).wait_recv()
        for cp in first + passed:
            cp.wait_send()
        mine.wait()

    return pl.pallas_call(
        body, name=name, out_shape=jax.ShapeDtypeStruct((8 * m_per, n), v.dtype),
        in_specs=[pl.BlockSpec(memory_space=pltpu.VMEM)], out_specs=pl.BlockSpec(memory_space=pltpu.VMEM),
        scratch_shapes=[pltpu.SemaphoreType.DMA((7,)), pltpu.SemaphoreType.DMA((7,)), pltpu.SemaphoreType.DMA],
        compiler_params=_cparams(),
    )(v)


def _gather_weights(stacks, after):
    nw = len(stacks)

    def body(*refs):
        outs = refs[nw + 1 : 2 * nw + 1]
        send_sems, recv_sems = refs[2 * nw + 1 :]
        x, y, c, chips = _place()
        me, sibling = (x, y, c), (x, y, 1 - c)

        def copy(w, k, block, to):
            px, py, pc = block
            half = outs[w].shape[1] // 2
            rows = outs[w].at[2 * px + py, pl.ds(pc * half, half), :]
            return pltpu.make_async_remote_copy(src_ref=rows, dst_ref=rows, send_sem=send_sems.at[6 * w + k],
                                                recv_sem=recv_sems.at[6 * w + k], device_id=to, device_id_type=MESH)

        sent = [copy(w, j, me, (*chip, c)) for w in range(nw) for j, chip in enumerate(chips)]
        for cp in sent:
            cp.start()
        for w in range(nw):
            for j, chip in enumerate(chips):
                copy(w, j, (*chip, c), me).wait_recv()
                passed = copy(w, 3 + j, (*chip, c), sibling)
                passed.start()
                sent.append(passed)
        for w in range(nw):
            for j, chip in enumerate(chips):
                copy(w, 3 + j, (*chip, 1 - c), me).wait_recv()
        for cp in sent:
            cp.wait_send()

    hbm = pl.BlockSpec(memory_space=pl.ANY)
    return pl.pallas_call(
        body, name="gather_weights", out_shape=[jax.ShapeDtypeStruct(w.shape, w.dtype) for w in stacks],
        in_specs=[hbm] * (nw + 1), out_specs=[hbm] * nw, input_output_aliases={i: i for i in range(nw)},
        scratch_shapes=[pltpu.SemaphoreType.DMA((6 * nw,)), pltpu.SemaphoreType.DMA((6 * nw,))],
        compiler_params=_cparams(),
    )(*stacks, after)


def _swap_halves(gs, name):
    nw = len(gs)

    def body(*refs):
        ins, gots = refs[:nw], refs[nw : 2 * nw]
        send_sems, recv_sems = refs[2 * nw :]
        x, y, c, _ = _place()
        cps = []
        for w in range(nw):
            half = ins[w].shape[1] // 2
            cps.append(pltpu.make_async_remote_copy(
                src_ref=ins[w].at[:, pl.ds((1 - c) * half, half), :], dst_ref=gots[w], send_sem=send_sems.at[w],
                recv_sem=recv_sems.at[w], device_id=(x, y, 1 - c), device_id_type=MESH))
            cps[-1].start()
        for cp in cps:
            cp.wait()

    hbm = pl.BlockSpec(memory_space=pl.ANY)
    return pl.pallas_call(
        body, name=name, out_shape=[jax.ShapeDtypeStruct((4, g.shape[1] // 2, g.shape[2]), g.dtype) for g in gs],
        in_specs=[hbm] * nw, out_specs=[hbm] * nw,
        scratch_shapes=[pltpu.SemaphoreType.DMA((nw,)), pltpu.SemaphoreType.DMA((nw,))], compiler_params=_cparams(),
    )(*gs)


def _split_start(name, bufs, plan, n, extra=()):
    n_plan = len(bufs)
    bufs = list(bufs) + list(extra)
    nb = len(bufs)

    def body(*refs):
        send_sems, recv_sems, token = refs[nb], refs[nb + 1], refs[-1]
        for k, (src, dst, to, _) in enumerate(plan(refs[:n_plan])):
            pltpu.make_async_remote_copy(src_ref=src, dst_ref=dst, send_sem=send_sems.at[k], recv_sem=recv_sems.at[k],
                                         device_id=to, device_id_type=MESH).start()
        token[...] = jnp.zeros(token.shape, token.dtype)

    hbm, sem = pl.BlockSpec(memory_space=pltpu.HBM), pl.BlockSpec(memory_space=pltpu.SEMAPHORE)
    res = pl.pallas_call(
        body, name=name,
        out_shape=(pltpu.SemaphoreType.DMA((n,)), pltpu.SemaphoreType.DMA((n,)), *[pltpu.HBM(b.shape, b.dtype) for b in bufs],
                   jax.ShapeDtypeStruct((SUBLANES, 128), F32)),
        in_specs=(hbm,) * nb, out_specs=(sem, sem) + (hbm,) * nb + (pl.BlockSpec(memory_space=pltpu.VMEM),),
        input_output_aliases={i: 2 + i for i in range(nb)},
        compiler_params=pltpu.CompilerParams(has_side_effects=pltpu.SideEffectType.DATAFLOW_SIDE_EFFECTING),
    )(*[pltpu.with_memory_space_constraint(b, pltpu.HBM) for b in bufs])
    return res[0], res[1], list(res[2 : 2 + nb]), res[-1]


def _split_wait(name, bufs, send_sems, recv_sems, after, plan):
    nb = len(bufs)

    def body(*refs):
        send, recv = refs[nb], refs[nb + 1]
        for k, (src, _, to, land) in enumerate(plan(refs[:nb])):
            cp = pltpu.make_async_remote_copy(src_ref=src, dst_ref=land, send_sem=send.at[k], recv_sem=recv.at[k],
                                              device_id=to, device_id_type=MESH)
            cp.wait_send()
            cp.wait_recv()

    hbm, sem = pl.BlockSpec(memory_space=pltpu.HBM), pl.BlockSpec(memory_space=pltpu.SEMAPHORE)
    return pl.pallas_call(
        body, name=name, out_shape=tuple(pltpu.HBM(b.shape, b.dtype) for b in bufs),
        in_specs=(hbm,) * nb + (sem, sem, pl.BlockSpec(memory_space=pl.ANY)), out_specs=(hbm,) * nb,
        input_output_aliases={i: i for i in range(nb)},
        compiler_params=pltpu.CompilerParams(has_side_effects=pltpu.SideEffectType.DATAFLOW_SIDE_EFFECTING),
    )(*bufs, send_sems, recv_sems, after)


def _indexed(name, fn, idx, ins, out, grid):
    def body(idx_ref, *refs):
        refs[-1][...] = fn(*[r[...] for r in refs[:-1]]).astype(refs[-1].dtype)

    (o_shape, o_dtype), o_block, o_map = out
    return pl.pallas_call(
        body, name=name, out_shape=jax.ShapeDtypeStruct(o_shape, o_dtype),
        grid_spec=pltpu.PrefetchScalarGridSpec(
            num_scalar_prefetch=1, grid=grid, in_specs=[pl.BlockSpec(blk, mp) for (_, blk, mp) in ins],
            out_specs=pl.BlockSpec(o_block, o_map)),
        compiler_params=_cparams(("arbitrary",) * len(grid)),
    )(idx, *[a for (a, _, _) in ins])


def _rstd(v):
    return lax.rsqrt(jnp.mean(v * v, axis=-1, keepdims=True) + EPS)


def _rms_bwd(dy, xh, r):
    return r * (dy - xh * jnp.mean(dy * xh, axis=-1, keepdims=True))


def _heads(v):
    return [v[:, h * HEAD_DIM : (h + 1) * HEAD_DIM] for h in range(v.shape[1] // HEAD_DIM)]


def _gelu(v):
    k = math.sqrt(2.0 / math.pi)
    return 0.5 * v * (1.0 + jnp.tanh(k * (v + 0.044715 * v * v * v)))


def _gelu_grad(v):
    k = math.sqrt(2.0 / math.pi)
    t = jnp.tanh(k * (v + 0.044715 * v * v * v))
    return 0.5 * (1.0 + t) + 0.5 * v * (1.0 - t * t) * k * (1.0 + 3 * 0.044715 * v * v)


def _sigmoid(v):
    return 1.0 / (1.0 + jnp.exp(-v))


def _slopes(n_heads):
    return [2.0 ** (-8.0 * (h + 1.0) / n_heads) for h in range(n_heads)]


def _band_iotas():
    ri = lax.broadcasted_iota(jnp.int32, (BAND, BAND), 0)
    ci = lax.broadcasted_iota(jnp.int32, (BAND, BAND), 1)
    return ri, ci


def _to_lanes(cols):
    rows = cols[0].shape[0]
    lane = lax.broadcasted_iota(jnp.int32, (rows, HEAD_DIM), 1)
    out = jnp.zeros((rows, HEAD_DIM), F32)
    for h, col in enumerate(cols):
        out = jnp.where(lane == h, col, out)
    return out


def _from_lanes(v, h, width):
    return jnp.broadcast_to(v[:, h : h + 1], (v.shape[0], width))


def _attn_fwd(name, q, k, v, bps, dil):
    s, aw = q.shape
    n_heads, nb = aw // HEAD_DIM, s // BAND
    scale = HEAD_DIM**-0.5
    slopes = _slopes(n_heads)

    def body(q_ref, kp_ref, kc_ref, vp_ref, vc_ref, o_ref, l_ref):
        b = pl.program_id(0)
        first = (b % bps) == 0
        ri, ci = _band_iotas()
        dist_c = (ri - ci).astype(F32)
        dist_p = (ri - ci + BAND).astype(F32)
        valid_c = ci <= ri
        valid_p = ci >= ri + jnp.where(first, BAND, 0)
        hs = range(n_heads)
        sls = [slice(h * HEAD_DIM, (h + 1) * HEAD_DIM) for h in hs]
        qs = [q_ref[:, sl] for sl in sls]
        sc = [lax.dot_general(qs[h], kc_ref[:, sls[h]], _NT, preferred_element_type=F32) for h in hs]
        sp = [lax.dot_general(qs[h], kp_ref[:, sls[h]], _NT, preferred_element_type=F32) for h in hs]
        sc = [jnp.where(valid_c, sc[h] * scale - (slopes[h] * dil) * dist_c, NEG) for h in hs]
        sp = [jnp.where(valid_p, sp[h] * scale - (slopes[h] * dil) * dist_p, NEG) for h in hs]
        m = [jnp.maximum(jnp.max(sc[h], axis=1, keepdims=True), jnp.max(sp[h], axis=1, keepdims=True)) for h in hs]
        pc = [jnp.exp(sc[h] - m[h]) for h in hs]
        pp = [jnp.exp(sp[h] - m[h]) for h in hs]
        den = [jnp.sum(pc[h], axis=1, keepdims=True) + jnp.sum(pp[h], axis=1, keepdims=True) for h in hs]
        acc = [jnp.dot(pc[h].astype(BF16), vc_ref[:, sls[h]], preferred_element_type=F32) for h in hs]
        acc = [acc[h] + jnp.dot(pp[h].astype(BF16), vp_ref[:, sls[h]], preferred_element_type=F32) for h in hs]
        o_ref[...] = jnp.concatenate([acc[h] / den[h] for h in hs], axis=1).astype(o_ref.dtype)
        l_ref[...] = _to_lanes([m[h] + jnp.log(den[h]) for h in hs])

    cur = pl.BlockSpec((BAND, aw), lambda b: (b, 0))
    prev = pl.BlockSpec((BAND, aw), lambda b: (jnp.maximum(b - 1, 0), 0))
    return pl.pallas_call(
        body, name=name, grid=(nb,), in_specs=[cur, prev, cur, prev, cur], out_specs=[cur, pl.BlockSpec((BAND, HEAD_DIM), lambda b: (b, 0))],
        out_shape=[jax.ShapeDtypeStruct((s, aw), BF16), jax.ShapeDtypeStruct((s, HEAD_DIM), F32)],
        compiler_params=_cparams(("arbitrary",)),
    )(q, k, k, v, v)


def _attn_bwd(name, q, k, v, do, lse, delta, bps, dil):
    s, aw = q.shape
    n_heads, nb = aw // HEAD_DIM, s // BAND
    scale = HEAD_DIM**-0.5
    slopes = _slopes(n_heads)

    def body(qc_ref, qn_ref, doc_ref, don_ref, lc_ref, ln_ref, dc_ref, dn_ref, k_ref, v_ref, dq_ref, dk_ref, dv_ref, carry):
        j = pl.program_id(0)

        @pl.when(j == 0)
        def _():
            carry[...] = jnp.zeros(carry.shape, F32)

        has_next = jnp.logical_and((j + 1) % bps != 0, j + 1 < nb)
        ri, ci = _band_iotas()
        dist_c = (ri - ci).astype(F32)
        dist_p = (ri - ci + BAND).astype(F32)
        valid_c = ci <= ri
        valid_n = ci >= ri + jnp.where(has_next, 0, BAND)
        carried = carry[...]
        hs = range(n_heads)
        sls = [slice(h * HEAD_DIM, (h + 1) * HEAD_DIM) for h in hs]
        ks, vs = [k_ref[:, sl] for sl in sls], [v_ref[:, sl] for sl in sls]
        sides = [(qc_ref, doc_ref, lc_ref, dc_ref, dist_c, valid_c), (qn_ref, don_ref, ln_ref, dn_ref, dist_p, valid_n)]
        res = []
        for q_ref, do_ref, l_ref, d_ref, dist, valid in sides:
            qs, dos = [q_ref[:, sl] for sl in sls], [do_ref[:, sl] for sl in sls]
            sc = [lax.dot_general(qs[h], ks[h], _NT, preferred_element_type=F32) for h in hs]
            dp = [lax.dot_general(dos[h], vs[h], _NT, preferred_element_type=F32) for h in hs]
            lse_v, delta_v = l_ref[...], d_ref[...]
            p = [jnp.where(valid, jnp.exp(sc[h] * scale - (slopes[h] * dil) * dist - _from_lanes(lse_v, h, BAND)), 0.0) for h in hs]
            ds = [(p[h] * (dp[h] - _from_lanes(delta_v, h, BAND))).astype(BF16) for h in hs]
            pb = [p[h].astype(BF16) for h in hs]
            dv = [lax.dot_general(pb[h], dos[h], _TN, preferred_element_type=F32) for h in hs]
            dk = [lax.dot_general(ds[h], qs[h], _TN, preferred_element_type=F32) for h in hs]
            dq = [jnp.dot(ds[h], ks[h], preferred_element_type=F32) for h in hs]
            res.append((dq, dk, dv))
        (dq_c, dk_c, dv_c), (dq_n, dk_n, dv_n) = res
        dq_ref[...] = (carried + jnp.concatenate(dq_c, axis=1) * scale).astype(dq_ref.dtype)
        carry[...] = jnp.concatenate(dq_n, axis=1) * scale
        dk_ref[...] = (jnp.concatenate([dk_c[h] + dk_n[h] for h in hs], axis=1) * scale).astype(dk_ref.dtype)
        dv_ref[...] = jnp.concatenate([dv_c[h] + dv_n[h] for h in hs], axis=1).astype(dv_ref.dtype)

    cur = pl.BlockSpec((BAND, aw), lambda b: (b, 0))
    nxt = pl.BlockSpec((BAND, aw), lambda b: (jnp.minimum(b + 1, nb - 1), 0))
    cur1 = pl.BlockSpec((BAND, HEAD_DIM), lambda b: (b, 0))
    nxt1 = pl.BlockSpec((BAND, HEAD_DIM), lambda b: (jnp.minimum(b + 1, nb - 1), 0))
    return pl.pallas_call(
        body, name=name, grid=(nb,), in_specs=[cur, nxt, cur, nxt, cur1, nxt1, cur1, nxt1, cur, cur], out_specs=[cur] * 3,
        out_shape=[jax.ShapeDtypeStruct((s, aw), BF16)] * 3, scratch_shapes=[pltpu.VMEM((BAND, aw), F32)],
        compiler_params=_cparams(("arbitrary",)),
    )(q, q, do, do, lse, lse, delta, delta, k, v)


STREAM_ROWS = BAND * max(dil for _, dil in DILATION_PATTERNS)


def _stream_orders(name, fn, srcs, vecs, n_vals, n_heads):
    s = srcs[0][0].shape[0]
    dils = [dil for _, dil in DILATION_PATTERNS]
    n_in = len(srcs) + len(vecs)

    def body(*refs):
        outs, scr = refs[n_in:-1], refs[-1]
        for vi, val in enumerate(fn(*[r[...] for r in refs[:n_in]])):
            scr[...] = val
            for di, dil in enumerate(dils):
                o = outs[vi * len(dils) + di]
                if dil == 1:
                    o[...] = val.astype(o.dtype)
                    continue
                for part in range(STREAM_ROWS // (dil * BAND)):
                    for r in range(dil):
                        rows = scr[pl.ds(part * dil * BAND + r, BAND, stride=dil), :]
                        o[r, part * BAND : (part + 1) * BAND, :] = rows.astype(o.dtype)

    in_specs = [pl.BlockSpec((STREAM_ROWS, HEAD_DIM), lambda b, h, cb=cb: (b, cb + h)) for (_, cb) in srcs]
    in_specs += [pl.BlockSpec(v.shape, lambda b, h, nd=v.ndim: (0,) * nd) for v in vecs]
    out_specs, out_shape = [], []
    for _ in range(n_vals):
        for dil in dils:
            if dil == 1:
                out_specs.append(pl.BlockSpec((STREAM_ROWS, HEAD_DIM), lambda b, h: (b, h)))
                out_shape.append(jax.ShapeDtypeStruct((s, n_heads * HEAD_DIM), BF16))
            else:
                out_specs.append(pl.BlockSpec((dil, STREAM_ROWS // dil, HEAD_DIM), lambda b, h: (0, b, h)))
                out_shape.append(jax.ShapeDtypeStruct((dil, s // dil, n_heads * HEAD_DIM), BF16))
    res = pl.pallas_call(
        body, name=name, grid=(s // STREAM_ROWS, n_heads), in_specs=in_specs, out_specs=out_specs, out_shape=out_shape,
        scratch_shapes=[pltpu.VMEM((STREAM_ROWS, HEAD_DIM), F32)], compiler_params=_cparams(("arbitrary", "arbitrary")),
    )(*[a for (a, _) in srcs], *vecs)
    res = [r.reshape(s, n_heads * HEAD_DIM) for r in res]
    return [res[vi * len(dils) : (vi + 1) * len(dils)] for vi in range(n_vals)]


def _natural_order(name, fn, vals, extras, out_dtypes, n_heads):
    s = vals[0][0].shape[0]
    dils = [dil for _, dil in DILATION_PATTERNS]
    n_in = len(vals) * len(dils) + len(extras)

    def body(*refs):
        outs, scr = refs[n_in:-1], refs[-1]
        blocks = []
        for vi in range(len(vals)):
            per_pattern = []
            for di, dil in enumerate(dils):
                x = refs[vi * len(dils) + di]
                if dil == 1:
                    per_pattern.append(x[...].astype(F32))
                    continue
                for part in range(STREAM_ROWS // (dil * BAND)):
                    for r in range(dil):
                        scr[pl.ds(part * dil * BAND + r, BAND, stride=dil), :] = x[r, part * BAND : (part + 1) * BAND, :].astype(F32)
                per_pattern.append(scr[...])
            blocks.append(per_pattern)
        res = fn(pl.program_id(1), blocks, *[r[...] for r in refs[len(vals) * len(dils) : n_in]])
        for o, v in zip(outs, res):
            o[...] = v.astype(o.dtype)

    in_specs, ins = [], []
    for per_pattern in vals:
        for arr, dil in zip(per_pattern, dils):
            if dil == 1:
                in_specs.append(pl.BlockSpec((STREAM_ROWS, HEAD_DIM), lambda b, h: (b, h)))
                ins.append(arr)
            else:
                in_specs.append(pl.BlockSpec((dil, STREAM_ROWS // dil, HEAD_DIM), lambda b, h: (0, b, h)))
                ins.append(arr.reshape(dil, s // dil, n_heads * HEAD_DIM))
    in_specs += [pl.BlockSpec((STREAM_ROWS, HEAD_DIM), lambda b, h: (b, 0)) for _ in extras]
    return pl.pallas_call(
        body, name=name, grid=(s // STREAM_ROWS, n_heads), in_specs=in_specs,
        out_specs=[pl.BlockSpec((STREAM_ROWS, HEAD_DIM), lambda b, h: (b, h)) for _ in out_dtypes],
        out_shape=[jax.ShapeDtypeStruct((s, n_heads * HEAD_DIM), dt) for dt in out_dtypes],
        scratch_shapes=[pltpu.VMEM((STREAM_ROWS, HEAD_DIM), F32)], compiler_params=_cparams(("arbitrary", "arbitrary")),
    )(*ins, *extras)


def _to_streams(v, dil):
    if dil == 1:
        return v
    s, w = v.shape
    return v.reshape(s // dil, dil, w).transpose(1, 0, 2).reshape(s, w)


def _from_streams(v, dil):
    if dil == 1:
        return v
    s, w = v.shape
    return v.reshape(dil, s // dil, w).transpose(1, 0, 2).reshape(s, w)


SCAN_LANES = 1024
SSM_ROWS = 128
SSM_FWD_ROWS = 256
FF_TM = 1024
FF_TN = 2048


def _cmul(ar, ai, br, bi):
    return ar * br - ai * bi, ar * bi + ai * br


def _cmul_conj(ar, ai, br, bi):
    return ar * br + ai * bi, ar * bi - ai * br


def _ssm_fwd(u_src, bre, bim, cre, cim, consts, d_skip):
    u_arr, sw, ucb = u_src
    s = u_arr.shape[0]
    nblk = bre.shape[0]
    nst = nblk * 512
    t = _blk(s, SSM_FWD_ROWS)
    w = _blk(nst, SCAN_LANES)

    def body(u_ref, bre_ref, bim_ref, cre_ref, cim_ref, k_ref, ds_ref, y_ref, hr_ref, hi_ref, xr, xi, car_r, car_i):
        @pl.when(pl.program_id(0) == 0)
        def _():
            car_r[...] = jnp.zeros(car_r.shape, F32)
            car_i[...] = jnp.zeros(car_i.shape, F32)

        u = u_ref[...].astype(F32)
        ub = u_ref[...].astype(BF16)
        for b in range(nblk):
            ch, st = slice(b * 128, (b + 1) * 128), slice(b * 512, (b + 1) * 512)
            xr[:, st] = jnp.dot(ub[:, ch], bre_ref[b], preferred_element_type=F32)
            xi[:, st] = jnp.dot(ub[:, ch], bim_ref[b], preferred_element_type=F32)
        for c in range(nst // w):
            ls = slice(c * w, (c + 1) * w)
            mult = [k_ref[i, :, ls] for i in range(8)]

            def step(i, carry, ls=ls, mult=mult):
                cr, ci = carry
                r0 = pl.multiple_of(i * SUBLANES, SUBLANES)
                hr, hi = xr[pl.ds(r0, SUBLANES), ls], xi[pl.ds(r0, SUBLANES), ls]
                for n, sh in enumerate((1, 2, 4)):
                    pr, pi = _cmul(mult[2 * n], mult[2 * n + 1], pltpu.roll(hr, sh, 0), pltpu.roll(hi, sh, 0))
                    hr, hi = hr + pr, hi + pi
                pr, pi = _cmul(mult[6], mult[7], cr, ci)
                hr, hi = hr + pr, hi + pi
                xr[pl.ds(r0, SUBLANES), ls] = hr
                xi[pl.ds(r0, SUBLANES), ls] = hi
                return jnp.broadcast_to(hr[7:8, :], hr.shape), jnp.broadcast_to(hi[7:8, :], hi.shape)

            cr, ci = lax.fori_loop(0, t // SUBLANES, step, (car_r[:, ls], car_i[:, ls]))
            car_r[:, ls] = cr
            car_i[:, ls] = ci
        hrb, hib = xr[...].astype(BF16), xi[...].astype(BF16)
        hr_ref[...] = hrb
        hi_ref[...] = hib
        for b in range(nblk):
            ch, st = slice(b * 128, (b + 1) * 128), slice(b * 512, (b + 1) * 512)
            yb = jnp.dot(hrb[:, st], cre_ref[b], preferred_element_type=F32)
            yb -= jnp.dot(hib[:, st], cim_ref[b], preferred_element_type=F32)
            y_ref[:, ch] = yb + ds_ref[:, ch] * u[:, ch]

    whole = lambda a: pl.BlockSpec(a.shape, lambda i, nd=a.ndim: (0,) * nd)
    return pl.pallas_call(
        body, name="ssm_fwd", grid=(s // t,),
        in_specs=[pl.BlockSpec((t, sw), lambda i: (i, ucb))] + [whole(a) for a in (bre, bim, cre, cim, consts, d_skip)],
        out_specs=[pl.BlockSpec((t, sw), lambda i: (i, 0)), pl.BlockSpec((t, nst), lambda i: (i, 0)), pl.BlockSpec((t, nst), lambda i: (i, 0))],
        out_shape=[jax.ShapeDtypeStruct((s, sw), F32), jax.ShapeDtypeStruct((s, nst), BF16), jax.ShapeDtypeStruct((s, nst), BF16)],
        scratch_shapes=[pltpu.VMEM((t, nst), F32), pltpu.VMEM((t, nst), F32), pltpu.VMEM((SUBLANES, nst), F32), pltpu.VMEM((SUBLANES, nst), F32)],
        compiler_params=_cparams(("arbitrary",)),
    )(u_arr, bre, bim, cre, cim, consts, d_skip)


def _ssm_bwd(dy, u_src, hr, hi, ctre, ctim, btre, btim, consts, d_skip):
    u_arr, sw, ucb = u_src
    s = dy.shape[0]
    nblk = ctre.shape[0]
    nst = nblk * 512
    t = _blk(s, SSM_ROWS)
    w = _blk(nst, SCAN_LANES)
    nt = s // t

    def body(dy_ref, u_ref, hr_ref, hi_ref, ctre_ref, ctim_ref, btre_ref, btim_ref, k_ref, ds_ref,
             du_ref, db_ref, dc_ref, da_ref, dd_ref, lr, li, hfr, hfi, car_r, car_i):
        @pl.when(pl.program_id(0) == 0)
        def _():
            car_r[...] = jnp.zeros(car_r.shape, F32)
            car_i[...] = jnp.zeros(car_i.shape, F32)
            db_ref[...] = jnp.zeros(db_ref.shape, F32)
            dc_ref[...] = jnp.zeros(dc_ref.shape, F32)
            da_ref[...] = jnp.zeros(da_ref.shape, F32)
            dd_ref[...] = jnp.zeros(dd_ref.shape, F32)

        dyv, u = dy_ref[...].astype(F32), u_ref[...].astype(F32)
        dyb, ub = dyv.astype(BF16), u.astype(BF16)
        hrb, hib = hr_ref[...], hi_ref[...]
        hfr[...] = hrb.astype(F32)
        hfi[...] = hib.astype(F32)
        for b in range(nblk):
            ch, st = slice(b * 128, (b + 1) * 128), slice(b * 512, (b + 1) * 512)
            lr[:, st] = jnp.dot(dyb[:, ch], ctre_ref[b], preferred_element_type=F32)
            li[:, st] = -jnp.dot(dyb[:, ch], ctim_ref[b], preferred_element_type=F32)
        last_row = lax.broadcasted_iota(jnp.int32, (SUBLANES, w), 0) == SUBLANES - 1
        for c in range(nst // w):
            ls = slice(c * w, (c + 1) * w)
            mult = [k_ref[i, :, ls] for i in range(8)]

            def step(i, carry, ls=ls, mult=mult):
                cr, ci, ar, ai = carry
                r0 = pl.multiple_of((t // SUBLANES - 1 - i) * SUBLANES, SUBLANES)
                gr, gi = lr[pl.ds(r0, SUBLANES), ls], li[pl.ds(r0, SUBLANES), ls]
                for n, sh in enumerate((1, 2, 4)):
                    pr, pi = _cmul_conj(mult[2 * n], mult[2 * n + 1], pltpu.roll(gr, SUBLANES - sh, 0), pltpu.roll(gi, SUBLANES - sh, 0))
                    gr, gi = gr + pr, gi + pi
                pr, pi = _cmul_conj(mult[6], mult[7], cr, ci)
                gr, gi = gr + pr, gi + pi
                lr[pl.ds(r0, SUBLANES), ls] = gr
                li[pl.ds(r0, SUBLANES), ls] = gi
                nr = jnp.where(last_row, cr, pltpu.roll(gr, SUBLANES - 1, 0))
                ni = jnp.where(last_row, ci, pltpu.roll(gi, SUBLANES - 1, 0))
                fr, fi = hfr[pl.ds(r0, SUBLANES), ls], hfi[pl.ds(r0, SUBLANES), ls]
                pr, pi = _cmul_conj(fr, fi, nr, ni)
                return (jnp.broadcast_to(gr[0:1, :], gr.shape), jnp.broadcast_to(gi[0:1, :], gi.shape), ar + pr, ai + pi)

            zero = jnp.zeros((SUBLANES, w), F32)
            cr, ci, ar, ai = lax.fori_loop(0, t // SUBLANES, step, (car_r[:, ls], car_i[:, ls], zero, zero))
            car_r[:, ls] = cr
            car_i[:, ls] = ci
            da_ref[0, :, ls] += ar
            da_ref[1, :, ls] += ai
        lrb, lib = lr[...].astype(BF16), li[...].astype(BF16)
        for b in range(nblk):
            ch, st = slice(b * 128, (b + 1) * 128), slice(b * 512, (b + 1) * 512)
            dub = jnp.dot(lrb[:, st], btre_ref[b], preferred_element_type=F32)
            dub += jnp.dot(lib[:, st], btim_ref[b], preferred_element_type=F32)
            du_ref[:, ch] = (dub + ds_ref[:, ch] * dyv[:, ch]).astype(du_ref.dtype)
            db_ref[0, b] += lax.dot_general(ub[:, ch], lrb[:, st], _TN, preferred_element_type=F32)
            db_ref[1, b] += lax.dot_general(ub[:, ch], lib[:, st], _TN, preferred_element_type=F32)
            dc_ref[0, b] += lax.dot_general(dyb[:, ch], hrb[:, st], _TN, preferred_element_type=F32)
            dc_ref[1, b] -= lax.dot_general(dyb[:, ch], hib[:, st], _TN, preferred_element_type=F32)
        dd_ref[...] += _colsum8(dyv * u)

    whole = lambda a: pl.BlockSpec(a.shape, lambda i, nd=a.ndim: (0,) * nd)
    rev = lambda wd, cb=0: pl.BlockSpec((t, wd), lambda i, cb=cb: (nt - 1 - i, cb))
    acc = lambda sh: pl.BlockSpec(sh, lambda i, nd=len(sh): (0,) * nd)
    db_shape, da_shape, dd_shape = (2, nblk, 128, 512), (2, SUBLANES, nst), (SUBLANES, sw)
    return pl.pallas_call(
        body, name="ssm_bwd", grid=(nt,),
        in_specs=[rev(sw), rev(sw, ucb), rev(nst), rev(nst)] + [whole(a) for a in (ctre, ctim, btre, btim, consts, d_skip)],
        out_specs=[rev(sw), acc(db_shape), acc(db_shape), acc(da_shape), acc(dd_shape)],
        out_shape=[jax.ShapeDtypeStruct((s, sw), BF16), jax.ShapeDtypeStruct(db_shape, F32), jax.ShapeDtypeStruct(db_shape, F32),
                   jax.ShapeDtypeStruct(da_shape, F32), jax.ShapeDtypeStruct(dd_shape, F32)],
        scratch_shapes=[pltpu.VMEM((t, nst), F32)] * 4 + [pltpu.VMEM((SUBLANES, nst), F32)] * 2,
        compiler_params=_cparams(("arbitrary",)),
    )(dy, u_arr, hr, hi, ctre, ctim, btre, btim, consts, d_skip)


def _ssm_discretise(lam_re, lam_im, log_step):
    step = jnp.exp(log_step)
    e = jnp.exp(lam_re * step)
    ar, ai = e * jnp.cos(lam_im * step), e * jnp.sin(lam_im * step)
    den = lam_re * lam_re + lam_im * lam_im
    inv_r, inv_i = lam_re / den, -lam_im / den
    fr, fi = _cmul(ar - 1.0, ai, inv_r, inv_i)
    return step, ar, ai, inv_r, inv_i, fr, fi


def _ssm_prep_fn(lam_re, lam_im, log_step):
    _, ar, ai, _, _, fr, fi = _ssm_discretise(lam_re, lam_im, log_step)
    pw = [(ar, ai)]
    for n in range(1, 8):
        pw.append(_cmul(*pw[n - 1], ar, ai))
    return [p[0] for p in pw] + [p[1] for p in pw] + [fr, fi]


def _ssm_param_grad_fn(lam_re, lam_im, log_step, da_r, da_i, df_r, df_i):
    step, ar, ai, inv_r, inv_i, fr, fi = _ssm_discretise(lam_re, lam_im, log_step)
    pr, pi = _cmul_conj(inv_r, inv_i, df_r, df_i)
    dat_r, dat_i = da_r + pr, da_i + pi
    wr, wi = _cmul(fr, fi, inv_r, inv_i)
    dl_r, dl_i = _cmul_conj(-wr, -wi, df_r, df_i)
    dz_r, dz_i = _cmul_conj(ar, ai, dat_r, dat_i)
    dl_r, dl_i = dl_r + step * dz_r, dl_i + step * dz_i
    dstep = jnp.sum(dz_r * lam_re + dz_i * lam_im, axis=-1, keepdims=True)
    return [dl_r, dl_i, step * dstep]


def _block_diag(v, rows_first):
    g, a, b = v.shape
    nb = g // GROUPS_PER_BLOCK
    eye = jnp.eye(GROUPS_PER_BLOCK, dtype=v.dtype)
    v = v.reshape(nb, GROUPS_PER_BLOCK, a, 1, b) * eye[None, :, None, :, None]
    return v.reshape(nb, GROUPS_PER_BLOCK * a, GROUPS_PER_BLOCK * b)


def _block_diag_take(v, a, b):
    nb = v.shape[0]
    v = v.reshape(nb, GROUPS_PER_BLOCK, a, GROUPS_PER_BLOCK, b)
    return jnp.stack([v[:, g, :, g, :] for g in range(GROUPS_PER_BLOCK)], axis=1).reshape(nb * GROUPS_PER_BLOCK, a, b)


def _adamw(w, g, m, v):
    m = ADAM_B1 * m + (1.0 - ADAM_B1) * g
    v = ADAM_B2 * v + (1.0 - ADAM_B2) * (g * g)
    m_hat = m / (1.0 - ADAM_B1**ADAM_STEP)
    v_hat = v / (1.0 - ADAM_B2**ADAM_STEP)
    delta = -ADAM_LR * (m_hat / (jnp.sqrt(v_hat) + ADAM_EPS) + ADAM_WD * w)
    return delta, m, v


def _adamw_rows(name, w, g, m, v):
    wd = w.shape[1]
    return _rowwise(name, lambda a, b, c, d: (list(_adamw(a, b, c, d)), []), [w, g, m, v], [], [(wd, F32)] * 3, [], tm=128)


SMALL = ["b_ada", "norm1_g", "q_norm_g", "k_norm_g", "lam_re", "lam_im", "log_step", "b_re", "b_im", "c_re", "c_im",
         "d_skip", "b_glu", "attn_out_g", "ssm_out_g", "norm2_g"]
LARGE = ["w_in", "w_glu", "w_out", "w_ff1", "w_ff2"]
ORDER = ["w_ada", "b_ada", "norm1_g", "w_in", "q_norm_g", "k_norm_g", "lam_re", "lam_im", "log_step", "b_re", "b_im", "c_re",
         "c_im", "d_skip", "w_glu", "b_glu", "attn_out_g", "ssm_out_g", "w_out", "norm2_g", "w_ff1", "w_ff2"]


def _pack(arrs):
    flat = jnp.concatenate([a.reshape(-1) for a in arrs])
    rows = -(-flat.shape[0] // PACK_LANES)
    rows = -(-rows // SUBLANES) * SUBLANES
    return jnp.pad(flat, (0, rows * PACK_LANES - flat.shape[0])).reshape(rows, PACK_LANES)


def _unpack(packed, shapes):
    flat, out, o = packed.reshape(-1), [], 0
    for sh in shapes:
        n = int(np.prod(sh))
        out.append(flat[o : o + n].reshape(sh))
        o += n
    return out


def kernel(x, c, w_ada, b_ada, norm1_g, w_in, q_norm_g, k_norm_g, lam_re, lam_im, log_step, b_re, b_im, c_re, c_im, d_skip, w_glu, b_glu, attn_out_g, ssm_out_g, w_out, norm2_g, w_ff1, w_ff2, loss_target, m_w_ada, m_b_ada, m_norm1_g, m_w_in, m_q_norm_g, m_k_norm_g, m_lam_re, m_lam_im, m_log_step, m_b_re, m_b_im, m_c_re, m_c_im, m_d_skip, m_w_glu, m_b_glu, m_attn_out_g, m_ssm_out_g, m_w_out, m_norm2_g, m_w_ff1, m_w_ff2, v_w_ada, v_b_ada, v_norm1_g, v_w_in, v_q_norm_g, v_k_norm_g, v_lam_re, v_lam_im, v_log_step, v_b_re, v_b_im, v_c_re, v_c_im, v_d_skip, v_w_glu, v_b_glu, v_attn_out_g, v_ssm_out_g, v_w_out, v_norm2_g, v_w_ff1, v_w_ff2):
    args = dict(locals())
    wts = {n: args[n] for n in ORDER}
    mom = {n: args["m_" + n] for n in ORDER}
    var = {n: args["v_" + n] for n in ORDER}

    x2, tgt = x[0], loss_target[0]
    s, d = x2.shape
    aw = d // 2
    sw = d - aw
    n_groups = sw // SSM_GROUP
    nst = n_groups * STATE_DIM
    chip = 2 * lax.axis_index("x") + lax.axis_index("y")

    chip_idx = jnp.reshape(chip, (1,)).astype(jnp.int32)
    core_idx = jnp.reshape(lax.axis_index("c"), (1,)).astype(jnp.int32)

    def cast_into_stack(n):
        w2 = wts[n][0]
        rows, cols = w2.shape
        tr = _blk(rows, 256)
        return _indexed("cast_" + n, lambda a: a, chip_idx, [(w2, (tr, cols), lambda r, ix: (r, 0))],
                        (((4, rows, cols), BF16), (None, tr, cols), lambda r, ix: (ix[0], r, 0)), (rows // tr,))

    stacks = {n: cast_into_stack(n) for n in LARGE}
    dff = 4 * wts["w_ff2"].shape[1]

    c_all = _allgather8("gather_c", jnp.pad(c, ((0, SUBLANES - 1), (0, 0))))[::SUBLANES]
    n_mod = w_ada.shape[2]
    b_ada_mine = lax.dynamic_slice(b_ada, (0, chip * n_mod), (1, n_mod))
    sil_c = _single("silu_c", lambda v: [v * _sigmoid(v)], [c_all], [(c_all.shape, F32)])[0]
    mod_part = _mm("ada_fwd", sil_c, w_ada[0], "nn", F32, tn=512, epi=lambda acc, b: acc + b[0:1, :],
                   extras=[jnp.broadcast_to(b_ada_mine, (SUBLANES, n_mod))])
    mod_all = _allgather8("gather_mod", mod_part)
    me = 2 * chip + lax.axis_index("c")
    mod = jnp.concatenate([lax.dynamic_slice(mod_all, (16 * k + me, 0), (1, n_mod)) for k in range(4)], axis=1)
    sh1, sc1, g1, sh2, sc2, g2 = [mod[:, i * d : (i + 1) * d] for i in range(6)]

    (w_in_s,) = _gather_weights([stacks["w_in"]], mod_all)

    def gather_plan(refs):
        px0, py0, pc, chips = _place()
        slot = 2 * px0 + py0
        return [(r.at[slot], r.at[slot], (px, py, pc), r.at[2 * px + py]) for r in refs for (px, py) in chips]

    mix_sems_s, mix_sems_r, mix_stacks, mix_token = _split_start(
        "gather_mix_start", [stacks["w_glu"], stacks["w_out"]], gather_plan, 6, extra=[w_in_s])
    w_in_s = mix_stacks.pop()
    ff_sems_s, ff_sems_r, ff_stacks, ff_token = _split_start(
        "gather_ff_start", [stacks["w_ff1"], stacks["w_ff2"]], gather_plan, 6, extra=[w_in_s])
    w_in_s = ff_stacks.pop()
    sc1 = sc1 + (mix_token[0:1, 0:1] + ff_token[0:1, 0:1])

    def norm_mod(v, g, sc, sh):
        return [v * _rstd(v) * g * (1.0 + sc) + sh], []

    (h1b,) = _rowwise("norm1", norm_mod, [x2], [norm1_g, sc1, sh1], [(d, BF16)], [])
    proj = _mm("proj_in", h1b, w_in_s, "nn", BF16, b_cs=True, tm=FF_TM)
    assert aw == sw
    q_src, k_src, v_src, u_src = [(proj, aw, i) for i in range(3)] + [(proj, sw, 3)]

    n_heads = aw // HEAD_DIM
    up = lambda v: v.astype(F32)
    qk_orders = _stream_orders("qk_prep", lambda q, k, v, gq, gk: [up(q) * _rstd(up(q)) * gq, up(k) * _rstd(up(k)) * gk, up(v)],
                               [(proj, 0), (proj, n_heads), (proj, 2 * n_heads)], [q_norm_g, k_norm_g], 3, n_heads)

    pats = []
    for pi, (window, dil) in enumerate(DILATION_PATTERNS):
        assert window // dil == BAND and s % STREAM_ROWS == 0
        bps = s // dil // BAND
        qp, kp, vp = [order[pi] for order in qk_orders]
        o_p, l_p = _attn_fwd(f"attn_fwd_d{dil}", qp, kp, vp, bps, dil)
        pats.append((dil, bps, qp, kp, vp, o_p, _from_streams(l_p, dil)))

    def lse_total(l1, l2, l3):
        m = jnp.maximum(jnp.maximum(l1, l2), l3)
        return m + jnp.log(jnp.exp(l1 - m) + jnp.exp(l2 - m) + jnp.exp(l3 - m))

    (lse,) = _rowwise("attn_lse", lambda l1, l2, l3: ([lse_total(l1, l2, l3)], []), [p[6] for p in pats], [], [(HEAD_DIM, F32)], [])

    def combine(h, blocks, l1, l2, l3):
        tot = lse_total(l1, l2, l3)
        lane = lax.broadcasted_iota(jnp.int32, l1.shape, 1)
        mix = 0.0
        for o_blk, l in zip(blocks[0], (l1, l2, l3)):
            w = jnp.sum(jnp.where(lane == h, jnp.exp(l - tot), 0.0), axis=1, keepdims=True)
            mix = mix + w * o_blk
        return [mix]

    (attn,) = _natural_order("attn_mix", combine, [[p[5] for p in pats]], [p[6] for p in pats], [F32], n_heads)

    lam_re2, lam_im2, log_step2 = lam_re[0], lam_im[0], log_step[0].reshape(n_groups, 1)
    prep = _single("ssm_prep", _ssm_prep_fn, [lam_re2, lam_im2, log_step2], [((n_groups, STATE_DIM), F32)] * 18)
    p_r = jnp.stack([p.reshape(nst) for p in prep[:8]])
    p_i = jnp.stack([p.reshape(nst) for p in prep[8:16]])
    f_ri = jnp.stack(prep[16:])
    row = jnp.arange(SUBLANES)[:, None]
    zero = jnp.zeros((SUBLANES, nst), F32)
    sel = lambda cond, v: jnp.where(cond, jnp.broadcast_to(v[None, :], (SUBLANES, nst)), zero)
    fwd_consts = jnp.stack([sel(row >= 1, p_r[0]), sel(row >= 1, p_i[0]), sel(row >= 2, p_r[1]), sel(row >= 2, p_i[1]),
                            sel(row >= 4, p_r[3]), sel(row >= 4, p_i[3]), p_r, p_i])
    bwd_consts = jnp.stack([sel(row <= 6, p_r[0]), sel(row <= 6, p_i[0]), sel(row <= 5, p_r[1]), sel(row <= 5, p_i[1]),
                            sel(row <= 3, p_r[3]), sel(row <= 3, p_i[3]), p_r[::-1], p_i[::-1]])
    nblk = n_groups // GROUPS_PER_BLOCK
    f_rows = f_ri.reshape(2, nblk, 1, 512)
    braw_r = _block_diag(b_re[0].transpose(0, 2, 1), True)
    braw_i = _block_diag(b_im[0].transpose(0, 2, 1), True)

    def bbar_fn(br, bi, f):
        r, i = _cmul(f[0], f[1], br, bi)
        return [r, i]

    bbar_r, bbar_i = _single("ssm_bbar", bbar_fn, [braw_r, braw_i, f_rows], [(braw_r.shape, BF16)] * 2)
    ct_r, ct_i = _block_diag(c_re[0], True).astype(BF16), _block_diag(c_im[0], True).astype(BF16)
    ds2 = d_skip
    y_ssm, h_r, h_i = _ssm_fwd(u_src, bbar_r, bbar_i, ct_r.transpose(0, 2, 1), ct_i.transpose(0, 2, 1), fwd_consts, ds2)

    (yb,) = _rowwise("gelu", lambda v: ([_gelu(v)], []), [y_ssm], [], [(sw, BF16)], [])
    w_glu_s, w_out_s = _split_wait("gather_mix_wait", mix_stacks, mix_sems_s, mix_sems_r, yb, gather_plan)
    w_glu_f = w_glu_s.reshape(sw, sw)
    w_out_f = w_out_s.reshape(d, d)
    z = _mm("glu_fwd", yb, w_glu_f, "nn", BF16)

    def glu(ys, zz, b):
        return [_gelu(ys) * _sigmoid(zz.astype(F32) + b)], []

    (ssm,) = _rowwise("glu", glu, [y_ssm, z], [b_glu], [(sw, F32)], [])

    def cat_norm(a, sm, ga, gs):
        return [jnp.concatenate([a * _rstd(a) * ga, sm * _rstd(sm) * gs], axis=1)], []

    (catb,) = _rowwise("cat_norm", cat_norm, [attn, ssm], [attn_out_g, ssm_out_g], [(d, BF16)], [])
    mixed = _mm("mix_out", catb, w_out_f, "nn", BF16, tm=FF_TM)

    def resid_norm(xv, mx, g1v, g, sc, sh):
        x1v = xv + g1v * mx.astype(F32)
        return [x1v, x1v * _rstd(x1v) * g * (1.0 + sc) + sh], []

    x1, h2b = _rowwise("resid_norm2", resid_norm, [x2, mixed], [g1, norm2_g, sc2, sh2], [(d, F32), (d, BF16)], [])
    w_ff1_s, w_ff2_s = _split_wait("gather_ff_wait", ff_stacks, ff_sems_s, ff_sems_r, x1, gather_plan)
    w_ff2_f = w_ff2_s.reshape(dff, d)
    a_ff = _mm("ff1", h2b, w_ff1_s, "nn", BF16, b_cs=True, tm=FF_TM, tn=FF_TN)
    sq_relu = lambda v: jnp.square(jnp.maximum(v.astype(F32), 0.0)).astype(BF16)
    ff = _mm("ff2", a_ff, w_ff2_f, "nn", BF16, a_pro=sq_relu, tm=FF_TM)

    def loss_fn(x1v, ffv, tg, g2v):
        ffv = ffv.astype(F32)
        diff = x1v + g2v * ffv - tg
        dout = diff * (1.0 / d)
        return [dout, dout * g2v], [_colsum8(0.5 * diff * dout), _colsum8(dout * ffv)]

    dout, dffb, loss_acc, dg2_acc = _rowwise("loss", loss_fn, [x1, ff, tgt], [g2], [(d, F32), (d, BF16)], [(SUBLANES, d)] * 2)

    da_ff = _mm("ff2_bwd", dffb, w_ff2_f, "nt", BF16, epi=lambda acc, av: acc * (2.0 * jnp.maximum(av.astype(F32), 0.0)), extras=[a_ff], tm=FF_TM, tn=FF_TN)
    gw = {}
    gw["w_ff2"] = _mm("ff2_wgrad", a_ff, dffb, "tn", BF16, a_pro=sq_relu, tm=FF_TM).reshape(4, dff // 4, d)
    gw["w_ff1"] = _mm("ff1_wgrad", h2b, da_ff, "tn", BF16, o_cs=True, tm=FF_TM)

    def add2(n, g, b):
        _, half, cols = b.shape
        tr = _blk(half, 256)
        nrb = half // tr
        blk = (None, tr, cols)
        return _indexed("grad_add2_" + n, lambda p, q: p.astype(F32) + q.astype(F32), core_idx,
                        [(g, blk, lambda k, r, ix: (k, ix[0] * nrb + r, 0)), (b, blk, lambda k, r, ix: (k, r, 0))],
                        ((b.shape, BF16), blk, lambda k, r, ix: (k, r, 0)), (4, nrb))

    def scatter_plan(refs):
        px0, py0, pc, chips = _place()
        nw = len(refs) // 2
        return [(refs[w].at[2 * px + py], refs[nw + w].at[j], (px, py, pc), refs[nw + w].at[j])
                for w in range(nw) for j, (px, py) in enumerate(chips)]

    def swap_plan(refs):
        px0, py0, pc, _ = _place()
        nw = len(refs) // 2
        plan = []
        for w in range(nw):
            half = refs[w].shape[1] // 2
            plan.append((refs[w].at[:, pl.ds((1 - pc) * half, half), :], refs[nw + w], (px0, py0, 1 - pc), refs[nw + w]))
        return plan

    early = ["w_ff2", "w_ff1"]
    sw_lands = [lax.empty((4, gw[n].shape[1] // 2, gw[n].shape[2]), BF16) for n in early]
    sw_sems_s, sw_sems_r, sw_bufs, sw_token = _split_start("swap_ff_start", [gw[n] for n in early] + sw_lands, swap_plan, len(early))
    sc2_late = sc2 + sw_token[0:1, 0:1]

    dh2 = _mm("ff1_bwd", da_ff, w_ff1_s, "nt", BF16, b_cs=True, tm=FF_TM)

    def norm2_bwd(dh, x1v, do, mx, g, sc, g1v):
        dh, mx = dh.astype(F32), mx.astype(F32)
        r = _rstd(x1v)
        xh = x1v * r
        dn = dh * (1.0 + sc)
        dx = do + _rms_bwd(dn * g, xh, r)
        return [dx, dx * g1v], [_colsum8(dh * xh * g), _colsum8(dh), _colsum8(dn * xh), _colsum8(dx * mx)]

    dx1, dmixb, dsc2_acc, dsh2_acc, dn2g_acc, dg1_acc = _rowwise(
        "norm2_bwd", norm2_bwd, [dh2, x1, dout, mixed], [norm2_g, sc2_late, g1], [(d, F32), (d, BF16)], [(SUBLANES, d)] * 4)

    sw_bufs = _split_wait("swap_ff_wait", sw_bufs, sw_sems_s, sw_sems_r, dmixb, swap_plan)
    parts = {n: add2(n, g, b) for n, g, b in zip(early, sw_bufs[: len(early)], sw_bufs[len(early) :])}
    lands = [lax.empty((3,) + parts[n].shape[1:], BF16) for n in early]
    sc_sems_s, sc_sems_r, sc_bufs, sc_token = _split_start("scatter_ff_start", [parts[n] for n in early] + lands, scatter_plan, 6)
    attn_out_g_late = attn_out_g + sc_token[0:1, 0:1]

    dcat = _mm("mix_out_bwd", dmixb, w_out_f, "nt", BF16, tm=FF_TM)
    gw["w_out"] = _mm("mix_out_wgrad", catb, dmixb, "tn", BF16, tm=FF_TM).reshape(4, d // 4, d)

    def cat_bwd(da, dsm, a, sm, ga, gs):
        da, dsm = da.astype(F32), dsm.astype(F32)
        ra, rs = _rstd(a), _rstd(sm)
        ah, sh = a * ra, sm * rs
        return [_rms_bwd(da * ga, ah, ra), _rms_bwd(dsm * gs, sh, rs)], [_colsum8(da * ah), _colsum8(dsm * sh)]

    dattn, dssm, dga_acc, dgs_acc = _rowwise(
        "cat_norm_bwd", cat_bwd, [(dcat, aw, 0), (dcat, sw, 1), attn, ssm], [attn_out_g_late, ssm_out_g], [(aw, F32), (sw, F32)],
        [(SUBLANES, aw), (SUBLANES, sw)])

    def glu_bwd(dsm, ys, zz, b):
        sg = _sigmoid(zz.astype(F32) + b)
        dz = dsm * _gelu(ys) * sg * (1.0 - sg)
        return [dz, dsm * sg], [_colsum8(dz)]

    dzb, dy_direct, dbglu_acc = _rowwise("glu_bwd", glu_bwd, [dssm, y_ssm, z], [b_glu], [(sw, BF16), (sw, BF16)], [(SUBLANES, sw)])
    gw["w_glu"] = _mm("glu_wgrad", yb, dzb, "tn", BF16).reshape(4, sw // 4, sw)
    dys = _mm("glu_bwd_mm", dzb, w_glu_f, "nt", BF16, epi=lambda acc, direct, ys: (acc + direct.astype(F32)) * _gelu_grad(ys), extras=[dy_direct, y_ssm])
    du, db_bar, dc_bd, da_acc, dds_acc = _ssm_bwd(dys, u_src, h_r, h_i, ct_r, ct_i, bbar_r.transpose(0, 2, 1),
                                                  bbar_i.transpose(0, 2, 1), bwd_consts, ds2)

    def delta_fn(da, a):
        return [_to_lanes([jnp.sum(p * q, axis=1, keepdims=True) for p, q in zip(_heads(da), _heads(a))])], []

    (delta,) = _rowwise("attn_delta", delta_fn, [dattn, attn], [], [(HEAD_DIM, F32)], [])
    (do_orders,) = _stream_orders("attn_do", lambda da: [da], [(dattn, 0)], [], 1, n_heads)
    dqs, dks, dvs = [], [], []
    for pi, (dil, bps, qp, kp, vp, _, _) in enumerate(pats):
        lsp, dlp = [_to_streams(t, dil) for t in (lse, delta)]
        dq_p, dk_p, dv_p = _attn_bwd(f"attn_bwd_d{dil}", qp, kp, vp, do_orders[pi], lsp, dlp, bps, dil)
        dqs.append(dq_p)
        dks.append(dk_p)
        dvs.append(dv_p)
    dq_t, dk_t, dv_t = _natural_order("attn_grad_sum", lambda h, blocks: [(b[0] + b[1]) + b[2] for b in blocks],
                                      [dqs, dks, dvs], [], [BF16] * 3, n_heads)

    def qk_bwd(dq_sum, dk_sum, dv_sum, duv, q, k, gq, gk):
        def one(dn, raw, g):
            outs, acc = [], 0.0
            for dh, h in zip(_heads(dn), _heads(raw)):
                r = _rstd(h)
                hh = h * r
                outs.append(_rms_bwd(dh * g, hh, r))
                acc = acc + _colsum8(dh * hh)
            return jnp.concatenate(outs, axis=1), acc

        dq, gq_acc = one(dq_sum.astype(F32), q.astype(F32), gq)
        dk, gk_acc = one(dk_sum.astype(F32), k.astype(F32), gk)
        return [jnp.concatenate([dq, dk, dv_sum.astype(F32), duv.astype(F32)], axis=1)], [gq_acc, gk_acc]

    dprojb, dgq_acc, dgk_acc = _rowwise("qk_norm_bwd", qk_bwd, [dq_t, dk_t, dv_t, du, q_src, k_src], [q_norm_g, k_norm_g],
                                        [(3 * aw + sw, BF16)], [(SUBLANES, HEAD_DIM)] * 2)
    gw["w_in"] = _mm("proj_in_wgrad", h1b, dprojb, "tn", BF16, o_cs=True, tm=FF_TM)
    late = [n for n in LARGE if n not in early]
    parts.update({n: add2(n, gw[n], b) for n, b in zip(late, _swap_halves([gw[n] for n in late], "swap_halves_rest"))})
    lands_late = [lax.empty((3,) + parts[n].shape[1:], BF16) for n in late]
    rest_sems_s, rest_sems_r, rest_bufs, rest_token = _split_start(
        "scatter_rest_start", [parts[n] for n in late] + lands_late, scatter_plan, 3 * len(late))
    sc1_late = sc1 + rest_token[0:1, 0:1]
    dh1 = _mm("proj_in_bwd", dprojb, w_in_s, "nt", BF16, b_cs=True, tm=FF_TM)

    def norm1_bwd(dh, xv, dxr, g, sc):
        dh = dh.astype(F32)
        r = _rstd(xv)
        xh = xv * r
        dn = dh * (1.0 + sc)
        return [dxr + _rms_bwd(dn * g, xh, r)], [_colsum8(dh * xh * g), _colsum8(dh), _colsum8(dn * xh)]

    grad_x, dsc1_acc, dsh1_acc, dn1g_acc = _rowwise("norm1_bwd", norm1_bwd, [dh1, x2, dx1], [norm1_g, sc1_late], [(d, F32)], [(SUBLANES, d)] * 3)

    def b_grad_fn(dbb, br, bi, f):
        fr, fi = f[0], f[1]
        dbr, dbi = _cmul_conj(fr, fi, dbb[0], dbb[1])
        dfr = jnp.sum(dbb[0] * br + dbb[1] * bi, axis=1, keepdims=True)
        dfi = jnp.sum(dbb[1] * br - dbb[0] * bi, axis=1, keepdims=True)
        return [dbr, dbi, dfr, dfi]

    dbr_bd, dbi_bd, df_r, df_i = _single("ssm_b_grad", b_grad_fn, [db_bar, braw_r, braw_i, f_rows],
                                         [(braw_r.shape, F32)] * 2 + [((nblk, 1, 512), F32)] * 2)

    def sums_fn(*accs):
        return [jnp.sum(a, axis=-2, keepdims=True) for a in accs]

    acc_list = [dsh1_acc, dsc1_acc, dg1_acc, dsh2_acc, dsc2_acc, dg2_acc, dn1g_acc, dgq_acc, dgk_acc, dds_acc, dbglu_acc, dga_acc,
                dgs_acc, dn2g_acc, loss_acc, da_acc]
    sums = _single("small_sums", sums_fn, acc_list, [(a.shape[:-2] + (1, a.shape[-1]), F32) for a in acc_list])
    (dsh1, dsc1, dg1, dsh2, dsc2, dg2, dn1g, dgq, dgk, dds, dbglu, dga, dgs, dn2g, loss_row, da_sum) = sums
    loss_dev = _single("loss_sum", lambda v: [jnp.sum(v, axis=1, keepdims=True)], [loss_row], [((1, 1), F32)])[0]
    gp = (n_groups, STATE_DIM)
    dlam_re, dlam_im, dlog_step = _single(
        "ssm_param_grad", _ssm_param_grad_fn,
        [lam_re2, lam_im2, log_step2, da_sum[0].reshape(gp), da_sum[1].reshape(gp), df_r.reshape(gp), df_i.reshape(gp)],
        [(gp, F32), (gp, F32), ((n_groups, 1), F32)])
    small_dev = {
        "b_ada": jnp.concatenate([dsh1, dsc1, dg1, dsh2, dsc2, dg2], axis=1),
        "norm1_g": dn1g, "q_norm_g": dgq, "k_norm_g": dgk, "lam_re": dlam_re, "lam_im": dlam_im, "log_step": dlog_step,
        "b_re": _block_diag_take(dbr_bd, SSM_GROUP, STATE_DIM).transpose(0, 2, 1),
        "b_im": _block_diag_take(dbi_bd, SSM_GROUP, STATE_DIM).transpose(0, 2, 1),
        "c_re": _block_diag_take(dc_bd[0], SSM_GROUP, STATE_DIM), "c_im": _block_diag_take(dc_bd[1], SSM_GROUP, STATE_DIM),
        "d_skip": dds, "b_glu": dbglu, "attn_out_g": dga, "ssm_out_g": dgs, "norm2_g": dn2g,
    }

    packed = _pack([small_dev[n] for n in SMALL])
    rows = packed.shape[0]
    tr = _blk(rows, 256)
    me_idx = jnp.reshape(me, (1,)).astype(jnp.int32)
    gathered = _indexed("pack_small", lambda a: a, me_idx, [(packed, (tr, PACK_LANES), lambda r, ix: (r, 0))],
                        (((8, rows, PACK_LANES), F32), (None, tr, PACK_LANES), lambda r, ix: (ix[0], r, 0)), (rows // tr,))

    def small_plan(refs):
        (g,) = refs
        px0, py0, pc0, _ = _place()
        flip = lambda v, bit: 1 - v if bit else v
        plan = []
        for mask in range(1, 8):
            px, py, pc = flip(px0, mask & 4), flip(py0, mask & 2), flip(pc0, mask & 1)
            mine = g.at[4 * px0 + 2 * py0 + pc0]
            plan.append((mine, mine, (px, py, pc), g.at[4 * px + 2 * py + pc]))
        return plan

    small_sems_s, small_sems_r, (gathered,), small_token = _split_start("gather_small_start", [gathered], small_plan, 7)

    sc_bufs = _split_wait("scatter_ff_wait", sc_bufs, sc_sems_s, sc_sems_r, small_token, scatter_plan)
    parts.update(zip(early, sc_bufs[: len(early)]))
    got2 = dict(zip(early, sc_bufs[len(early) :]))
    rest_bufs = _split_wait("scatter_rest_wait", rest_bufs, rest_sems_s, rest_sems_r, small_token, scatter_plan)
    parts.update(zip(late, rest_bufs[: len(late)]))
    got2.update(zip(late, rest_bufs[len(late) :]))
    place_idx = jnp.stack([chip, lax.axis_index("c")]).astype(jnp.int32)

    def add4(n, p, b):
        _, half, cols = p.shape
        tr = _blk(half, 256)
        blk = (None, tr, cols)
        fn = lambda own, q0, q1, q2: ((own.astype(F32) + q0.astype(F32)) + q1.astype(F32)) + q2.astype(F32)
        srcs = [(p, blk, lambda r, ix: (ix[0], r, 0))] + [(b, blk, lambda r, ix, j=j: (j, r, 0)) for j in range(3)]
        return _indexed("grad_add4_" + n, fn, place_idx, srcs, (((2, half, cols), F32), blk, lambda r, ix: (ix[1], r, 0)), (half // tr,))

    def join_plan(refs):
        px0, py0, pc, _ = _place()
        return [(r.at[pc], r.at[pc], (px0, py0, 1 - pc), r.at[1 - pc]) for r in refs]

    out = {}

    def update_large(names, fulls):
        last = None
        for n, full in zip(names, fulls):
            g = full.reshape(wts[n].shape[1:])
            last = _adamw_rows("adamw_" + n, wts[n][0], g, mom[n][0], var[n][0])
            for kind, val in zip(("grad", "delta", "new_m", "new_v"), (g,) + tuple(last)):
                out[kind, n] = val[None]
        return last[0]

    rest = [n for n in LARGE if n not in early]
    ja_s, ja_r, fulls_a, _ = _split_start("join_ff_start", [add4(n, parts[n], got2[n]) for n in early], join_plan, len(early))
    jb_s, jb_r, fulls_b, jb_token = _split_start("join_rest_start", [add4(n, parts[n], got2[n]) for n in rest], join_plan, len(rest))
    done_a = update_large(early, _split_wait("join_ff_wait", fulls_a, ja_s, ja_r, jb_token, join_plan))
    done_b = update_large(rest, _split_wait("join_rest_wait", fulls_b, jb_s, jb_r, done_a, join_plan))
    (gathered,) = _split_wait("gather_small_wait", [gathered], small_sems_s, small_sems_r, done_b, small_plan)

    def sum8_fn(g):
        acc = g[0]
        for i in range(1, 8):
            acc = acc + g[i]
        return [acc]

    g_small = _single("sum_small", sum8_fn, [gathered], [((rows, PACK_LANES), F32)])[0]

    w_small, m_small, v_small = [_pack([src[n] for n in SMALL]) for src in (wts, mom, var)]
    upd_small = _adamw_rows("adamw_small", w_small, g_small, m_small, v_small)
    shapes = [wts[n].shape for n in SMALL]
    for kind, arr in zip(("grad", "delta", "new_m", "new_v"), (g_small,) + tuple(upd_small)):
        for n, val in zip(SMALL, _unpack(arr, shapes)):
            out[kind, n] = val

    dmod_all = gathered[:, 0 : -(-6 * d // PACK_LANES), :].reshape(8, -1)[:, : 6 * d]
    dmod_mine = lax.dynamic_slice(dmod_all, (0, chip * n_mod), (8, n_mod))
    g_ada = _mm("ada_wgrad", sil_c, dmod_mine, "tn", F32, tm=512, tn=512)
    upd = _adamw_rows("adamw_w_ada", w_ada[0], g_ada, m_w_ada[0], v_w_ada[0])
    for kind, val in zip(("grad", "delta", "new_m", "new_v"), (g_ada,) + tuple(upd)):
        out[kind, "w_ada"] = val[None]

    loss = lax.psum(loss_dev[0, 0], ("x", "y", "c"))
    return (loss, grad_x[None], *[out[kind, n] for kind in ("grad", "delta", "new_m", "new_v") for n in ORDER])
```

```python
import functools
import math

import jax
import jax.numpy as jnp
import numpy as np
from jax import lax
from jax.experimental import pallas as pl
from jax.experimental.pallas import tpu as pltpu

F32 = jnp.float32
BF16 = jnp.bfloat16
MESH = pl.DeviceIdType.MESH

EPS = 1e-6
HEAD_DIM = 128
SSM_GROUP = 16
STATE_DIM = 64
GROUPS_PER_BLOCK = 8
DILATION_PATTERNS = ((128, 1), (512, 4), (2048, 16))
BAND = 128
NEG = -1e30

ADAM_LR, ADAM_B1, ADAM_B2, ADAM_EPS, ADAM_WD, ADAM_STEP = 0.001, 0.9, 0.999, 1e-08, 0.01, 10

V7X_VMEM_BYTES = 64 * 2**20
VMEM_LIMIT = 48 * 2**20
SUBLANES = 8
PACK_LANES = 1024


def _cparams(sem=None):
    return pltpu.CompilerParams(dimension_semantics=sem, vmem_limit_bytes=VMEM_LIMIT)


def _blk(n, want):
    b = min(n, want)
    while n % b:
        b //= 2
    return b


def _colsum8(v):
    tm, w = v.shape
    return v.reshape(tm // SUBLANES, SUBLANES, w).sum(axis=0)


def _rowwise(name, fn, rows, vecs, out_rows, out_accs, tm=256, n_rows=None):
    rows = [r if isinstance(r, tuple) else (r, r.shape[1], 0) for r in rows]
    rows = [r if len(r) == 4 else r + (0,) for r in rows]
    s = rows[0][0].shape[0] if n_rows is None else n_rows
    tm = _blk(s, tm)
    assert all(r[3] % tm == 0 for r in rows)
    n_in, n_or = len(rows) + len(vecs), len(out_rows)

    def body(*refs):
        outs, accs = fn(*[r[...] for r in refs[:n_in]])
        for r, v in zip(refs[n_in : n_in + n_or], outs):
            r[...] = v.astype(r.dtype)
        if out_accs:
            acc_refs = refs[n_in + n_or :]

            @pl.when(pl.program_id(0) == 0)
            def _():
                for r in acc_refs:
                    r[...] = jnp.zeros(r.shape, r.dtype)

            for r, v in zip(acc_refs, accs):
                r[...] += v

    in_specs = [pl.BlockSpec((tm, w), lambda i, cb=cb, rb=r0 // tm: (i + rb, cb)) for (_, w, cb, r0) in rows]
    in_specs += [pl.BlockSpec(v.shape, lambda i, nd=v.ndim: (0,) * nd) for v in vecs]
    out_specs = [pl.BlockSpec((tm, w), lambda i: (i, 0)) for (w, _) in out_rows]
    out_specs += [pl.BlockSpec(sh, lambda i, nd=len(sh): (0,) * nd) for sh in out_accs]
    out_shape = [jax.ShapeDtypeStruct((s, w), dt) for (w, dt) in out_rows]
    out_shape += [jax.ShapeDtypeStruct(sh, F32) for sh in out_accs]
    res = pl.pallas_call(
        body, name=name, grid=(s // tm,), in_specs=in_specs, out_specs=out_specs, out_shape=out_shape,
        compiler_params=_cparams(("arbitrary",)),
    )(*[r[0] for r in rows], *vecs)
    return res


def _single(name, fn, ins, out_shapes):
    n_in = len(ins)

    def body(*refs):
        outs = fn(*[r[...] for r in refs[:n_in]])
        for r, v in zip(refs[n_in:], outs):
            r[...] = v.astype(r.dtype)

    vm = pl.BlockSpec(memory_space=pltpu.VMEM)
    return pl.pallas_call(
        body, name=name, in_specs=[vm] * n_in, out_specs=[vm] * len(out_shapes),
        out_shape=[jax.ShapeDtypeStruct(sh, dt) for sh, dt in out_shapes], compiler_params=_cparams(),
    )(*ins)


_NN = (((1,), (0,)), ((), ()))
_NT = (((1,), (1,)), ((), ()))
_TN = (((0,), (0,)), ((), ()))


def _mm(name, a, b, mode, out_dtype, *, tm=512, tn=1024, tk=2048, b_cs=False, o_cs=False, epi=None, extras=(), a_pro=None):
    if mode == "tn":
        k, m = a.shape
        n = b.shape[1]
    else:
        m, k = a.shape
        if mode == "nn":
            n = b.shape[2] * 4 if b_cs else b.shape[1]
        else:
            n = b.shape[1] if b_cs else b.shape[0]
    tm = _blk(m, tm)
    tn = _blk(n // 4 if (b_cs and mode == "nn") or o_cs else n, tn)
    tk = _blk(k // 4 if (b_cs and mode == "nt") else k, tk)
    nk = k // tk
    dims = {"nn": _NN, "nt": _NT, "tn": _TN}[mode]
    n_ex = len(extras)

    def body(*refs):
        a_ref, b_ref = refs[0], refs[1]
        ex_refs, o_ref = refs[2 : 2 + n_ex], refs[2 + n_ex]

        def finish(acc):
            if epi is not None:
                acc = epi(acc, *[r[...] for r in ex_refs])
            o_ref[...] = acc.astype(o_ref.dtype)

        a_blk = a_ref[...] if a_pro is None else a_pro(a_ref[...])
        part = lax.dot_general(a_blk, b_ref[...], dims, preferred_element_type=F32)
        if nk == 1:
            finish(part)
        else:
            acc_ref = refs[3 + n_ex]
            kk = pl.program_id(2)

            @pl.when(kk == 0)
            def _():
                acc_ref[...] = part

            @pl.when(kk > 0)
            def _():
                acc_ref[...] += part

            @pl.when(kk == nk - 1)
            def _():
                finish(acc_ref[...])

    a_spec = pl.BlockSpec((tk, tm), lambda i, j, kk: (kk, i)) if mode == "tn" else pl.BlockSpec((tm, tk), lambda i, j, kk: (i, kk))
    if mode == "nn":
        if b_cs:
            per = (n // 4) // tn
            b_spec = pl.BlockSpec((None, tk, tn), lambda i, j, kk: (j // per, kk, j % per))
        else:
            b_spec = pl.BlockSpec((tk, tn), lambda i, j, kk: (kk, j))
    elif mode == "nt":
        if b_cs:
            per = (k // 4) // tk
            b_spec = pl.BlockSpec((None, tn, tk), lambda i, j, kk: (kk // per, j, kk % per))
        else:
            b_spec = pl.BlockSpec((tn, tk), lambda i, j, kk: (j, kk))
    else:
        b_spec = pl.BlockSpec((tk, tn), lambda i, j, kk: (kk, j))
    if o_cs:
        per = (n // 4) // tn
        o_spec = pl.BlockSpec((None, tm, tn), lambda i, j, kk: (j // per, i, j % per))
        o_shape = jax.ShapeDtypeStruct((4, m, n // 4), out_dtype)
    else:
        o_spec = pl.BlockSpec((tm, tn), lambda i, j, kk: (i, j))
        o_shape = jax.ShapeDtypeStruct((m, n), out_dtype)
    ex_specs = [pl.BlockSpec((tm, tn), lambda i, j, kk: (i, j)) for _ in extras]
    return pl.pallas_call(
        body, name=name, grid=(m // tm, n // tn, nk), in_specs=[a_spec, b_spec] + ex_specs, out_specs=o_spec, out_shape=o_shape,
        scratch_shapes=[pltpu.VMEM((tm, tn), F32)] if nk > 1 else [],
        compiler_params=_cparams(("parallel", "parallel", "arbitrary")),
    )(a, b, *extras)


def _place():
    x, y, c = lax.axis_index("x"), lax.axis_index("y"), lax.axis_index("c")
    return x, y, c, [(1 - x, y), (x, 1 - y), (1 - x, 1 - y)]


def _allgather8(name, v):
    m_per, n = v.shape

    def body(x_ref, out_ref, send_sems, recv_sems, local_sem):
        x, y, c, chips = _place()
        me, sibling = (x, y, c), (x, y, 1 - c)

        def rows(px, py, pc):
            return out_ref.at[pl.ds((4 * px + 2 * py + pc) * m_per, m_per), :]

        def copy(k, block, to, src=None):
            return pltpu.make_async_remote_copy(
                src_ref=rows(*block) if src is None else src, dst_ref=rows(*block), send_sem=send_sems.at[k],
                recv_sem=recv_sems.at[k], device_id=to, device_id_type=MESH)

        mine = pltpu.make_async_copy(x_ref, rows(*me), local_sem)
        mine.start()
        first = [copy(0, me, sibling, src=x_ref)]
        first += [copy(1 + j, me, (*chip, c), src=x_ref) for j, chip in enumerate(chips)]
        for cp in first:
            cp.start()
        passed = [copy(4 + j, (*chip, c), sibling) for j, chip in enumerate(chips)]
        for j, chip in enumerate(chips):
            copy(1 + j, (*chip, c), me).wait_recv()
            passed[j].start()
        copy(0, sibling, me).wait_recv()
        for j, chip in enumerate(chips):
            copy(4 + j, (*chip, 1 - c), me).wait_recv()
        for cp in first + passed:
            cp.wait_send()
        mine.wait()

    return pl.pallas_call(
        body, name=name, out_shape=jax.ShapeDtypeStruct((8 * m_per, n), v.dtype),
        in_specs=[pl.BlockSpec(memory_space=pltpu.VMEM)], out_specs=pl.BlockSpec(memory_space=pltpu.VMEM),
        scratch_shapes=[pltpu.SemaphoreType.DMA((7,)), pltpu.SemaphoreType.DMA((7,)), pltpu.SemaphoreType.DMA],
        compiler_params=_cparams(),
    )(v)


def _gather_weights(stacks, after):
    nw = len(stacks)

    def body(*refs):
        outs = refs[nw + 1 : 2 * nw + 1]
        send_sems, recv_sems = refs[2 * nw + 1 :]
        x, y, c, chips = _place()
        me, sibling = (x, y, c), (x, y, 1 - c)

        def copy(w, k, block, to):
            px, py, pc = block
            half = outs[w].shape[1] // 2
            rows = outs[w].at[2 * px + py, pl.ds(pc * half, half), :]
            return pltpu.make_async_remote_copy(src_ref=rows, dst_ref=rows, send_sem=send_sems.at[6 * w + k],
                                                recv_sem=recv_sems.at[6 * w + k], device_id=to, device_id_type=MESH)

        sent = [copy(w, j, me, (*chip, c)) for w in range(nw) for j, chip in enumerate(chips)]
        for cp in sent:
            cp.start()
        for w in range(nw):
            for j, chip in enumerate(chips):
                copy(w, j, (*chip, c), me).wait_recv()
                passed = copy(w, 3 + j, (*chip, c), sibling)
                passed.start()
                sent.append(passed)
        for w in range(nw):
            for j, chip in enumerate(chips):
                copy(w, 3 + j, (*chip, 1 - c), me).wait_recv()
        for cp in sent:
            cp.wait_send()

    hbm = pl.BlockSpec(memory_space=pl.ANY)
    return pl.pallas_call(
        body, name="gather_weights", out_shape=[jax.ShapeDtypeStruct(w.shape, w.dtype) for w in stacks],
        in_specs=[hbm] * (nw + 1), out_specs=[hbm] * nw, input_output_aliases={i: i for i in range(nw)},
        scratch_shapes=[pltpu.SemaphoreType.DMA((6 * nw,)), pltpu.SemaphoreType.DMA((6 * nw,))],
        compiler_params=_cparams(),
    )(*stacks, after)


def _swap_halves(gs, name):
    nw = len(gs)

    def body(*refs):
        ins, gots = refs[:nw], refs[nw : 2 * nw]
        send_sems, recv_sems = refs[2 * nw :]
        x, y, c, _ = _place()
        cps = []
        for w in range(nw):
            half = ins[w].shape[1] // 2
            cps.append(pltpu.make_async_remote_copy(
                src_ref=ins[w].at[:, pl.ds((1 - c) * half, half), :], dst_ref=gots[w], send_sem=send_sems.at[w],
                recv_sem=recv_sems.at[w], device_id=(x, y, 1 - c), device_id_type=MESH))
            cps[-1].start()
        for cp in cps:
            cp.wait()

    hbm = pl.BlockSpec(memory_space=pl.ANY)
    return pl.pallas_call(
        body, name=name, out_shape=[jax.ShapeDtypeStruct((4, g.shape[1] // 2, g.shape[2]), g.dtype) for g in gs],
        in_specs=[hbm] * nw, out_specs=[hbm] * nw,
        scratch_shapes=[pltpu.SemaphoreType.DMA((nw,)), pltpu.SemaphoreType.DMA((nw,))], compiler_params=_cparams(),
    )(*gs)


def _split_start(name, bufs, plan, n, extra=()):
    n_plan = len(bufs)
    bufs = list(bufs) + list(extra)
    nb = len(bufs)

    def body(*refs):
        send_sems, recv_sems, token = refs[nb], refs[nb + 1], refs[-1]
        for k, (src, dst, to, _) in enumerate(plan(refs[:n_plan])):
            pltpu.make_async_remote_copy(src_ref=src, dst_ref=dst, send_sem=send_sems.at[k], recv_sem=recv_sems.at[k],
                                         device_id=to, device_id_type=MESH).start()
        token[...] = jnp.zeros(token.shape, token.dtype)

    hbm, sem = pl.BlockSpec(memory_space=pltpu.HBM), pl.BlockSpec(memory_space=pltpu.SEMAPHORE)
    res = pl.pallas_call(
        body, name=name,
        out_shape=(pltpu.SemaphoreType.DMA((n,)), pltpu.SemaphoreType.DMA((n,)), *[pltpu.HBM(b.shape, b.dtype) for b in bufs],
                   jax.ShapeDtypeStruct((SUBLANES, 128), F32)),
        in_specs=(hbm,) * nb, out_specs=(sem, sem) + (hbm,) * nb + (pl.BlockSpec(memory_space=pltpu.VMEM),),
        input_output_aliases={i: 2 + i for i in range(nb)},
        compiler_params=pltpu.CompilerParams(has_side_effects=pltpu.SideEffectType.DATAFLOW_SIDE_EFFECTING),
    )(*[pltpu.with_memory_space_constraint(b, pltpu.HBM) for b in bufs])
    return res[0], res[1], list(res[2 : 2 + nb]), res[-1]


def _split_wait(name, bufs, send_sems, recv_sems, after, plan):
    nb = len(bufs)

    def body(*refs):
        send, recv = refs[nb], refs[nb + 1]
        for k, (src, _, to, land) in enumerate(plan(refs[:nb])):
            cp = pltpu.make_async_remote_copy(src_ref=src, dst_ref=land, send_sem=send.at[k], recv_sem=recv.at[k],
                                              device_id=to, device_id_type=MESH)
            cp.wait_send()
            cp.wait_recv()

    hbm, sem = pl.BlockSpec(memory_space=pltpu.HBM), pl.BlockSpec(memory_space=pltpu.SEMAPHORE)
    return pl.pallas_call(
        body, name=name, out_shape=tuple(pltpu.HBM(b.shape, b.dtype) for b in bufs),
        in_specs=(hbm,) * nb + (sem, sem, pl.BlockSpec(memory_space=pl.ANY)), out_specs=(hbm,) * nb,
        input_output_aliases={i: i for i in range(nb)},
        compiler_params=pltpu.CompilerParams(has_side_effects=pltpu.SideEffectType.DATAFLOW_SIDE_EFFECTING),
    )(*bufs, send_sems, recv_sems, after)


def _indexed(name, fn, idx, ins, out, grid):
    def body(idx_ref, *refs):
        refs[-1][...] = fn(*[r[...] for r in refs[:-1]]).astype(refs[-1].dtype)

    (o_shape, o_dtype), o_block, o_map = out
    return pl.pallas_call(
        body, name=name, out_shape=jax.ShapeDtypeStruct(o_shape, o_dtype),
        grid_spec=pltpu.PrefetchScalarGridSpec(
            num_scalar_prefetch=1, grid=grid, in_specs=[pl.BlockSpec(blk, mp) for (_, blk, mp) in ins],
            out_specs=pl.BlockSpec(o_block, o_map)),
        compiler_params=_cparams(("arbitrary",) * len(grid)),
    )(idx, *[a for (a, _, _) in ins])


def _rstd(v):
    return lax.rsqrt(jnp.mean(v * v, axis=-1, keepdims=True) + EPS)


def _rms_bwd(dy, xh, r):
    return r * (dy - xh * jnp.mean(dy * xh, axis=-1, keepdims=True))


def _heads(v):
    return [v[:, h * HEAD_DIM : (h + 1) * HEAD_DIM] for h in range(v.shape[1] // HEAD_DIM)]


def _gelu(v):
    k = math.sqrt(2.0 / math.pi)
    return 0.5 * v * (1.0 + jnp.tanh(k * (v + 0.044715 * v * v * v)))


def _gelu_grad(v):
    k = math.sqrt(2.0 / math.pi)
    t = jnp.tanh(k * (v + 0.044715 * v * v * v))
    return 0.5 * (1.0 + t) + 0.5 * v * (1.0 - t * t) * k * (1.0 + 3 * 0.044715 * v * v)


def _sigmoid(v):
    return 1.0 / (1.0 + jnp.exp(-v))


def _slopes(n_heads):
    return [2.0 ** (-8.0 * (h + 1.0) / n_heads) for h in range(n_heads)]


def _band_iotas():
    ri = lax.broadcasted_iota(jnp.int32, (BAND, BAND), 0)
    ci = lax.broadcasted_iota(jnp.int32, (BAND, BAND), 1)
    return ri, ci


def _to_lanes(cols):
    rows = cols[0].shape[0]
    lane = lax.broadcasted_iota(jnp.int32, (rows, HEAD_DIM), 1)
    out = jnp.zeros((rows, HEAD_DIM), F32)
    for h, col in enumerate(cols):
        out = jnp.where(lane == h, col, out)
    return out


def _from_lanes(v, h, width):
    return jnp.broadcast_to(v[:, h : h + 1], (v.shape[0], width))


def _attn_fwd(name, q, k, v, bps, dil):
    s, aw = q.shape
    n_heads, nb = aw // HEAD_DIM, s // BAND
    scale = HEAD_DIM**-0.5
    slopes = _slopes(n_heads)

    assert nb % 2 == 0

    def body(q_ref, kp_ref, kc_ref, vp_ref, vc_ref, o_ref, l_ref):
        i = pl.program_id(0)
        ri, ci = _band_iotas()
        dist_c = (ri - ci).astype(F32)
        dist_p = (ri - ci + BAND).astype(F32)
        valid_c = ci <= ri
        hs = range(n_heads)
        sls = [slice(h * HEAD_DIM, (h + 1) * HEAD_DIM) for h in hs]
        lo, hi = slice(0, BAND), slice(BAND, 2 * BAND)
        chains = []
        for sub, rows in enumerate((lo, hi)):
            valid_p = ci >= ri + jnp.where((2 * i + sub) % bps == 0, BAND, 0)
            for h in hs:
                k_prev = kp_ref[:, sls[h]] if sub == 0 else kc_ref[lo, sls[h]]
                v_prev = vp_ref[:, sls[h]] if sub == 0 else vc_ref[lo, sls[h]]
                chains.append((q_ref[rows, sls[h]], kc_ref[rows, sls[h]], k_prev, vc_ref[rows, sls[h]], v_prev, h, valid_p))
        cs = range(len(chains))
        sc = [lax.dot_general(chains[c][0], chains[c][1], _NT, preferred_element_type=F32) for c in cs]
        sp = [lax.dot_general(chains[c][0], chains[c][2], _NT, preferred_element_type=F32) for c in cs]
        sc = [jnp.where(valid_c, sc[c] * scale - (slopes[chains[c][5]] * dil) * dist_c, NEG) for c in cs]
        sp = [jnp.where(chains[c][6], sp[c] * scale - (slopes[chains[c][5]] * dil) * dist_p, NEG) for c in cs]
        m = [jnp.maximum(jnp.max(sc[c], axis=1, keepdims=True), jnp.max(sp[c], axis=1, keepdims=True)) for c in cs]
        pc = [jnp.exp(sc[c] - m[c]) for c in cs]
        pp = [jnp.exp(sp[c] - m[c]) for c in cs]
        den = [jnp.sum(pc[c], axis=1, keepdims=True) + jnp.sum(pp[c], axis=1, keepdims=True) for c in cs]
        acc = [jnp.dot(pc[c].astype(BF16), chains[c][3], preferred_element_type=F32) for c in cs]
        acc = [acc[c] + jnp.dot(pp[c].astype(BF16), chains[c][4], preferred_element_type=F32) for c in cs]
        out = [jnp.concatenate([acc[c] / den[c] for c in cs[sub * n_heads : (sub + 1) * n_heads]], axis=1) for sub in range(2)]
        o_ref[...] = jnp.concatenate(out, axis=0).astype(o_ref.dtype)
        lse = [_to_lanes([m[c] + jnp.log(den[c]) for c in cs[sub * n_heads : (sub + 1) * n_heads]]) for sub in range(2)]
        l_ref[...] = jnp.concatenate(lse, axis=0)

    cur = pl.BlockSpec((2 * BAND, aw), lambda i: (i, 0))
    prev = pl.BlockSpec((BAND, aw), lambda i: (jnp.maximum(2 * i - 1, 0), 0))
    return pl.pallas_call(
        body, name=name, grid=(nb // 2,), in_specs=[cur, prev, cur, prev, cur],
        out_specs=[cur, pl.BlockSpec((2 * BAND, HEAD_DIM), lambda i: (i, 0))],
        out_shape=[jax.ShapeDtypeStruct((s, aw), BF16), jax.ShapeDtypeStruct((s, HEAD_DIM), F32)],
        compiler_params=_cparams(("arbitrary",)),
    )(q, k, k, v, v)


def _attn_bwd(name, q, k, v, do, lse, delta, bps, dil):
    s, aw = q.shape
    n_heads, nb = aw // HEAD_DIM, s // BAND
    scale = HEAD_DIM**-0.5
    slopes = _slopes(n_heads)

    assert nb % 2 == 0

    def body(qc_ref, qn_ref, doc_ref, don_ref, lc_ref, ln_ref, dc_ref, dn_ref, k_ref, v_ref, dq_ref, dk_ref, dv_ref, carry):
        i = pl.program_id(0)

        @pl.when(i == 0)
        def _():
            carry[...] = jnp.zeros(carry.shape, F32)

        ri, ci = _band_iotas()
        dist_c = (ri - ci).astype(F32)
        dist_p = (ri - ci + BAND).astype(F32)
        valid_c = ci <= ri
        valid_ab = ci >= ri + jnp.where((2 * i + 1) % bps != 0, 0, BAND)
        valid_bn = ci >= ri + jnp.where(jnp.logical_and((2 * i + 2) % bps != 0, 2 * i + 2 < nb), 0, BAND)
        carried = carry[...]
        hs = range(n_heads)
        sls = [slice(h * HEAD_DIM, (h + 1) * HEAD_DIM) for h in hs]
        lo, hi, al = slice(0, BAND), slice(BAND, 2 * BAND), slice(None)
        sides = [((qc_ref, doc_ref, lc_ref, dc_ref), lo, lo, dist_c, valid_c), ((qc_ref, doc_ref, lc_ref, dc_ref), hi, lo, dist_p, valid_ab),
                 ((qc_ref, doc_ref, lc_ref, dc_ref), hi, hi, dist_c, valid_c), ((qn_ref, don_ref, ln_ref, dn_ref), al, hi, dist_p, valid_bn)]
        res = []
        for (q_ref, do_ref, l_ref, d_ref), qr, kr, dist, valid in sides:
            qs, dos = [q_ref[qr, sl] for sl in sls], [do_ref[qr, sl] for sl in sls]
            ks, vs = [k_ref[kr, sl] for sl in sls], [v_ref[kr, sl] for sl in sls]
            sc = [lax.dot_general(qs[h], ks[h], _NT, preferred_element_type=F32) for h in hs]
            dp = [lax.dot_general(dos[h], vs[h], _NT, preferred_element_type=F32) for h in hs]
            lse_v, delta_v = l_ref[qr, :], d_ref[qr, :]
            p = [jnp.where(valid, jnp.exp(sc[h] * scale - (slopes[h] * dil) * dist - _from_lanes(lse_v, h, BAND)), 0.0) for h in hs]
            ds = [(p[h] * (dp[h] - _from_lanes(delta_v, h, BAND))).astype(BF16) for h in hs]
            pb = [p[h].astype(BF16) for h in hs]
            dv = [lax.dot_general(pb[h], dos[h], _TN, preferred_element_type=F32) for h in hs]
            dk = [lax.dot_general(ds[h], qs[h], _TN, preferred_element_type=F32) for h in hs]
            dq = [jnp.dot(ds[h], ks[h], preferred_element_type=F32) for h in hs]
            res.append((dq, dk, dv))
        (dq_aa, dk_aa, dv_aa), (dq_ba, dk_ba, dv_ba), (dq_bb, dk_bb, dv_bb), (dq_nb, dk_nb, dv_nb) = res
        cat = lambda parts: jnp.concatenate(parts, axis=1)
        dq_a = carried + cat(dq_aa) * scale
        dq_b = cat([dq_ba[h] + dq_bb[h] for h in hs]) * scale
        dq_ref[...] = jnp.concatenate([dq_a, dq_b], axis=0).astype(dq_ref.dtype)
        carry[...] = cat(dq_nb) * scale
        dk_a, dk_b = cat([dk_aa[h] + dk_ba[h] for h in hs]), cat([dk_bb[h] + dk_nb[h] for h in hs])
        dk_ref[...] = (jnp.concatenate([dk_a, dk_b], axis=0) * scale).astype(dk_ref.dtype)
        dv_a, dv_b = cat([dv_aa[h] + dv_ba[h] for h in hs]), cat([dv_bb[h] + dv_nb[h] for h in hs])
        dv_ref[...] = jnp.concatenate([dv_a, dv_b], axis=0).astype(dv_ref.dtype)

    cur = pl.BlockSpec((2 * BAND, aw), lambda i: (i, 0))
    nxt = pl.BlockSpec((BAND, aw), lambda i: (jnp.minimum(2 * i + 2, nb - 1), 0))
    cur1 = pl.BlockSpec((2 * BAND, HEAD_DIM), lambda i: (i, 0))
    nxt1 = pl.BlockSpec((BAND, HEAD_DIM), lambda i: (jnp.minimum(2 * i + 2, nb - 1), 0))
    return pl.pallas_call(
        body, name=name, grid=(nb // 2,), in_specs=[cur, nxt, cur, nxt, cur1, nxt1, cur1, nxt1, cur, cur], out_specs=[cur] * 3,
        out_shape=[jax.ShapeDtypeStruct((s, aw), BF16)] * 3, scratch_shapes=[pltpu.VMEM((BAND, aw), F32)],
        compiler_params=_cparams(("arbitrary",)),
    )(q, q, do, do, lse, lse, delta, delta, k, v)


STREAM_ROWS = BAND * max(dil for _, dil in DILATION_PATTERNS)


def _stream_orders(name, fn, srcs, vecs, n_vals, n_heads):
    s = srcs[0][0].shape[0]
    dils = [dil for _, dil in DILATION_PATTERNS]
    n_in = len(srcs) + len(vecs)

    def body(*refs):
        outs, scr = refs[n_in:-1], refs[-1]
        for vi, val in enumerate(fn(*[r[...] for r in refs[:n_in]])):
            scr[...] = val
            for di, dil in enumerate(dils):
                o = outs[vi * len(dils) + di]
                if dil == 1:
                    o[...] = val.astype(o.dtype)
                    continue
                for part in range(STREAM_ROWS // (dil * BAND)):
                    for r in range(dil):
                        rows = scr[pl.ds(part * dil * BAND + r, BAND, stride=dil), :]
                        o[r, part * BAND : (part + 1) * BAND, :] = rows.astype(o.dtype)

    in_specs = [pl.BlockSpec((STREAM_ROWS, HEAD_DIM), lambda b, h, cb=cb: (b, cb + h)) for (_, cb) in srcs]
    in_specs += [pl.BlockSpec(v.shape, lambda b, h, nd=v.ndim: (0,) * nd) for v in vecs]
    out_specs, out_shape = [], []
    for _ in range(n_vals):
        for dil in dils:
            if dil == 1:
                out_specs.append(pl.BlockSpec((STREAM_ROWS, HEAD_DIM), lambda b, h: (b, h)))
                out_shape.append(jax.ShapeDtypeStruct((s, n_heads * HEAD_DIM), BF16))
            else:
                out_specs.append(pl.BlockSpec((dil, STREAM_ROWS // dil, HEAD_DIM), lambda b, h: (0, b, h)))
                out_shape.append(jax.ShapeDtypeStruct((dil, s // dil, n_heads * HEAD_DIM), BF16))
    res = pl.pallas_call(
        body, name=name, grid=(s // STREAM_ROWS, n_heads), in_specs=in_specs, out_specs=out_specs, out_shape=out_shape,
        scratch_shapes=[pltpu.VMEM((STREAM_ROWS, HEAD_DIM), F32)], compiler_params=_cparams(("arbitrary", "arbitrary")),
    )(*[a for (a, _) in srcs], *vecs)
    res = [r.reshape(s, n_heads * HEAD_DIM) for r in res]
    return [res[vi * len(dils) : (vi + 1) * len(dils)] for vi in range(n_vals)]


def _natural_order(name, fn, vals, extras, out_dtypes, n_heads):
    s = vals[0][0].shape[0]
    dils = [dil for _, dil in DILATION_PATTERNS]
    n_in = len(vals) * len(dils) + len(extras)

    def body(*refs):
        outs, scr = refs[n_in:-1], refs[-1]
        blocks = []
        for vi in range(len(vals)):
            per_pattern = []
            for di, dil in enumerate(dils):
                x = refs[vi * len(dils) + di]
                if dil == 1:
                    per_pattern.append(x[...].astype(F32))
                    continue
                for part in range(STREAM_ROWS // (dil * BAND)):
                    for r in range(dil):
                        scr[pl.ds(part * dil * BAND + r, BAND, stride=dil), :] = x[r, part * BAND : (part + 1) * BAND, :].astype(F32)
                per_pattern.append(scr[...])
            blocks.append(per_pattern)
        res = fn(pl.program_id(1), blocks, *[r[...] for r in refs[len(vals) * len(dils) : n_in]])
        for o, v in zip(outs, res):
            o[...] = v.astype(o.dtype)

    in_specs, ins = [], []
    for per_pattern in vals:
        for arr, dil in zip(per_pattern, dils):
            if dil == 1:
                in_specs.append(pl.BlockSpec((STREAM_ROWS, HEAD_DIM), lambda b, h: (b, h)))
                ins.append(arr)
            else:
                in_specs.append(pl.BlockSpec((dil, STREAM_ROWS // dil, HEAD_DIM), lambda b, h: (0, b, h)))
                ins.append(arr.reshape(dil, s // dil, n_heads * HEAD_DIM))
    in_specs += [pl.BlockSpec((STREAM_ROWS, HEAD_DIM), lambda b, h: (b, 0)) for _ in extras]
    return pl.pallas_call(
        body, name=name, grid=(s // STREAM_ROWS, n_heads), in_specs=in_specs,
        out_specs=[pl.BlockSpec((STREAM_ROWS, HEAD_DIM), lambda b, h: (b, h)) for _ in out_dtypes],
        out_shape=[jax.ShapeDtypeStruct((s, n_heads * HEAD_DIM), dt) for dt in out_dtypes],
        scratch_shapes=[pltpu.VMEM((STREAM_ROWS, HEAD_DIM), F32)], compiler_params=_cparams(("arbitrary", "arbitrary")),
    )(*ins, *extras)


def _to_streams(v, dil):
    if dil == 1:
        return v
    s, w = v.shape
    return v.reshape(s // dil, dil, w).transpose(1, 0, 2).reshape(s, w)


def _from_streams(v, dil):
    if dil == 1:
        return v
    s, w = v.shape
    return v.reshape(dil, s // dil, w).transpose(1, 0, 2).reshape(s, w)


SCAN_LANES = 1024
SSM_ROWS = 128
SSM_FWD_ROWS = 256
FF_TM = 1024
FF_TN = 2048


def _cmul(ar, ai, br, bi):
    return ar * br - ai * bi, ar * bi + ai * br


def _cmul_conj(ar, ai, br, bi):
    return ar * br + ai * bi, ar * bi - ai * br


def _ssm_fwd(u_src, bre, bim, cre, cim, consts, d_skip):
    u_arr, sw, ucb = u_src
    s = u_arr.shape[0]
    nblk = bre.shape[0]
    nst = nblk * 512
    t = _blk(s, SSM_FWD_ROWS)
    w = _blk(nst, SCAN_LANES)

    def body(u_ref, bre_ref, bim_ref, cre_ref, cim_ref, k_ref, ds_ref, y_ref, hr_ref, hi_ref, xr, xi, car_r, car_i):
        @pl.when(pl.program_id(0) == 0)
        def _():
            car_r[...] = jnp.zeros(car_r.shape, F32)
            car_i[...] = jnp.zeros(car_i.shape, F32)

        u = u_ref[...].astype(F32)
        ub = u_ref[...].astype(BF16)
        for b in range(nblk):
            ch, st = slice(b * 128, (b + 1) * 128), slice(b * 512, (b + 1) * 512)
            xr[:, st] = jnp.dot(ub[:, ch], bre_ref[b], preferred_element_type=F32)
            xi[:, st] = jnp.dot(ub[:, ch], bim_ref[b], preferred_element_type=F32)
        for c in range(nst // w):
            ls = slice(c * w, (c + 1) * w)
            mult = [k_ref[i, :, ls] for i in range(8)]

            def step(i, carry, ls=ls, mult=mult):
                cr, ci = carry
                r0 = pl.multiple_of(i * SUBLANES, SUBLANES)
                hr, hi = xr[pl.ds(r0, SUBLANES), ls], xi[pl.ds(r0, SUBLANES), ls]
                for n, sh in enumerate((1, 2, 4)):
                    pr, pi = _cmul(mult[2 * n], mult[2 * n + 1], pltpu.roll(hr, sh, 0), pltpu.roll(hi, sh, 0))
                    hr, hi = hr + pr, hi + pi
                pr, pi = _cmul(mult[6], mult[7], cr, ci)
                hr, hi = hr + pr, hi + pi
                xr[pl.ds(r0, SUBLANES), ls] = hr
                xi[pl.ds(r0, SUBLANES), ls] = hi
                return jnp.broadcast_to(hr[7:8, :], hr.shape), jnp.broadcast_to(hi[7:8, :], hi.shape)

            cr, ci = lax.fori_loop(0, t // SUBLANES, step, (car_r[:, ls], car_i[:, ls]))
            car_r[:, ls] = cr
            car_i[:, ls] = ci
        hrb, hib = xr[...].astype(BF16), xi[...].astype(BF16)
        hr_ref[...] = hrb
        hi_ref[...] = hib
        for b in range(nblk):
            ch, st = slice(b * 128, (b + 1) * 128), slice(b * 512, (b + 1) * 512)
            yb = jnp.dot(hrb[:, st], cre_ref[b], preferred_element_type=F32)
            yb -= jnp.dot(hib[:, st], cim_ref[b], preferred_element_type=F32)
            y_ref[:, ch] = yb + ds_ref[:, ch] * u[:, ch]

    whole = lambda a: pl.BlockSpec(a.shape, lambda i, nd=a.ndim: (0,) * nd)
    return pl.pallas_call(
        body, name="ssm_fwd", grid=(s // t,),
        in_specs=[pl.BlockSpec((t, sw), lambda i: (i, ucb))] + [whole(a) for a in (bre, bim, cre, cim, consts, d_skip)],
        out_specs=[pl.BlockSpec((t, sw), lambda i: (i, 0)), pl.BlockSpec((t, nst), lambda i: (i, 0)), pl.BlockSpec((t, nst), lambda i: (i, 0))],
        out_shape=[jax.ShapeDtypeStruct((s, sw), F32), jax.ShapeDtypeStruct((s, nst), BF16), jax.ShapeDtypeStruct((s, nst), BF16)],
        scratch_shapes=[pltpu.VMEM((t, nst), F32), pltpu.VMEM((t, nst), F32), pltpu.VMEM((SUBLANES, nst), F32), pltpu.VMEM((SUBLANES, nst), F32)],
        compiler_params=_cparams(("arbitrary",)),
    )(u_arr, bre, bim, cre, cim, consts, d_skip)


def _ssm_bwd(dy, u_src, hr, hi, ctre, ctim, btre, btim, consts, d_skip):
    u_arr, sw, ucb = u_src
    s = dy.shape[0]
    nblk = ctre.shape[0]
    nst = nblk * 512
    t = _blk(s, SSM_ROWS)
    w = _blk(nst, SCAN_LANES)
    nt = s // t

    def body(dy_ref, u_ref, hr_ref, hi_ref, ctre_ref, ctim_ref, btre_ref, btim_ref, k_ref, ds_ref,
             du_ref, db_ref, dc_ref, da_ref, dd_ref, lr, li, hfr, hfi, car_r, car_i):
        @pl.when(pl.program_id(0) == 0)
        def _():
            car_r[...] = jnp.zeros(car_r.shape, F32)
            car_i[...] = jnp.zeros(car_i.shape, F32)
            db_ref[...] = jnp.zeros(db_ref.shape, F32)
            dc_ref[...] = jnp.zeros(dc_ref.shape, F32)
            da_ref[...] = jnp.zeros(da_ref.shape, F32)
            dd_ref[...] = jnp.zeros(dd_ref.shape, F32)

        dyv, u = dy_ref[...].astype(F32), u_ref[...].astype(F32)
        dyb, ub = dyv.astype(BF16), u.astype(BF16)
        hrb, hib = hr_ref[...], hi_ref[...]
        hfr[...] = hrb.astype(F32)
        hfi[...] = hib.astype(F32)
        for b in range(nblk):
            ch, st = slice(b * 128, (b + 1) * 128), slice(b * 512, (b + 1) * 512)
            lr[:, st] = jnp.dot(dyb[:, ch], ctre_ref[b], preferred_element_type=F32)
            li[:, st] = -jnp.dot(dyb[:, ch], ctim_ref[b], preferred_element_type=F32)
        last_row = lax.broadcasted_iota(jnp.int32, (SUBLANES, w), 0) == SUBLANES - 1
        for c in range(nst // w):
            ls = slice(c * w, (c + 1) * w)
            mult = [k_ref[i, :, ls] for i in range(8)]

            def step(i, carry, ls=ls, mult=mult):
                cr, ci, ar, ai = carry
                r0 = pl.multiple_of((t // SUBLANES - 1 - i) * SUBLANES, SUBLANES)
                gr, gi = lr[pl.ds(r0, SUBLANES), ls], li[pl.ds(r0, SUBLANES), ls]
                for n, sh in enumerate((1, 2, 4)):
                    pr, pi = _cmul_conj(mult[2 * n], mult[2 * n + 1], pltpu.roll(gr, SUBLANES - sh, 0), pltpu.roll(gi, SUBLANES - sh, 0))
                    gr, gi = gr + pr, gi + pi
                pr, pi = _cmul_conj(mult[6], mult[7], cr, ci)
                gr, gi = gr + pr, gi + pi
                lr[pl.ds(r0, SUBLANES), ls] = gr
                li[pl.ds(r0, SUBLANES), ls] = gi
                nr = jnp.where(last_row, cr, pltpu.roll(gr, SUBLANES - 1, 0))
                ni = jnp.where(last_row, ci, pltpu.roll(gi, SUBLANES - 1, 0))
                fr, fi = hfr[pl.ds(r0, SUBLANES), ls], hfi[pl.ds(r0, SUBLANES), ls]
                pr, pi = _cmul_conj(fr, fi, nr, ni)
                return (jnp.broadcast_to(gr[0:1, :], gr.shape), jnp.broadcast_to(gi[0:1, :], gi.shape), ar + pr, ai + pi)

            zero = jnp.zeros((SUBLANES, w), F32)
            cr, ci, ar, ai = lax.fori_loop(0, t // SUBLANES, step, (car_r[:, ls], car_i[:, ls], zero, zero))
            car_r[:, ls] = cr
            car_i[:, ls] = ci
            da_ref[0, :, ls] += ar
            da_ref[1, :, ls] += ai
        lrb, lib = lr[...].astype(BF16), li[...].astype(BF16)
        for b in range(nblk):
            ch, st = slice(b * 128, (b + 1) * 128), slice(b * 512, (b + 1) * 512)
            dub = jnp.dot(lrb[:, st], btre_ref[b], preferred_element_type=F32)
            dub += jnp.dot(lib[:, st], btim_ref[b], preferred_element_type=F32)
            du_ref[:, ch] = (dub + ds_ref[:, ch] * dyv[:, ch]).astype(du_ref.dtype)
            db_ref[0, b] += lax.dot_general(ub[:, ch], lrb[:, st], _TN, preferred_element_type=F32)
            db_ref[1, b] += lax.dot_general(ub[:, ch], lib[:, st], _TN, preferred_element_type=F32)
            dc_ref[0, b] += lax.dot_general(dyb[:, ch], hrb[:, st], _TN, preferred_element_type=F32)
            dc_ref[1, b] -= lax.dot_general(dyb[:, ch], hib[:, st], _TN, preferred_element_type=F32)
        dd_ref[...] += _colsum8(dyv * u)

    whole = lambda a: pl.BlockSpec(a.shape, lambda i, nd=a.ndim: (0,) * nd)
    rev = lambda wd, cb=0: pl.BlockSpec((t, wd), lambda i, cb=cb: (nt - 1 - i, cb))
    acc = lambda sh: pl.BlockSpec(sh, lambda i, nd=len(sh): (0,) * nd)
    db_shape, da_shape, dd_shape = (2, nblk, 128, 512), (2, SUBLANES, nst), (SUBLANES, sw)
    return pl.pallas_call(
        body, name="ssm_bwd", grid=(nt,),
        in_specs=[rev(sw), rev(sw, ucb), rev(nst), rev(nst)] + [whole(a) for a in (ctre, ctim, btre, btim, consts, d_skip)],
        out_specs=[rev(sw), acc(db_shape), acc(db_shape), acc(da_shape), acc(dd_shape)],
        out_shape=[jax.ShapeDtypeStruct((s, sw), BF16), jax.ShapeDtypeStruct(db_shape, F32), jax.ShapeDtypeStruct(db_shape, F32),
                   jax.ShapeDtypeStruct(da_shape, F32), jax.ShapeDtypeStruct(dd_shape, F32)],
        scratch_shapes=[pltpu.VMEM((t, nst), F32)] * 4 + [pltpu.VMEM((SUBLANES, nst), F32)] * 2,
        compiler_params=_cparams(("arbitrary",)),
    )(dy, u_arr, hr, hi, ctre, ctim, btre, btim, consts, d_skip)


def _ssm_discretise(lam_re, lam_im, log_step):
    step = jnp.exp(log_step)
    e = jnp.exp(lam_re * step)
    ar, ai = e * jnp.cos(lam_im * step), e * jnp.sin(lam_im * step)
    den = lam_re * lam_re + lam_im * lam_im
    inv_r, inv_i = lam_re / den, -lam_im / den
    fr, fi = _cmul(ar - 1.0, ai, inv_r, inv_i)
    return step, ar, ai, inv_r, inv_i, fr, fi


def _ssm_prep_fn(lam_re, lam_im, log_step):
    _, ar, ai, _, _, fr, fi = _ssm_discretise(lam_re, lam_im, log_step)
    pw = [(ar, ai)]
    for n in range(1, 8):
        pw.append(_cmul(*pw[n - 1], ar, ai))
    return [p[0] for p in pw] + [p[1] for p in pw] + [fr, fi]


def _ssm_param_grad_fn(lam_re, lam_im, log_step, da_r, da_i, df_r, df_i):
    step, ar, ai, inv_r, inv_i, fr, fi = _ssm_discretise(lam_re, lam_im, log_step)
    pr, pi = _cmul_conj(inv_r, inv_i, df_r, df_i)
    dat_r, dat_i = da_r + pr, da_i + pi
    wr, wi = _cmul(fr, fi, inv_r, inv_i)
    dl_r, dl_i = _cmul_conj(-wr, -wi, df_r, df_i)
    dz_r, dz_i = _cmul_conj(ar, ai, dat_r, dat_i)
    dl_r, dl_i = dl_r + step * dz_r, dl_i + step * dz_i
    dstep = jnp.sum(dz_r * lam_re + dz_i * lam_im, axis=-1, keepdims=True)
    return [dl_r, dl_i, step * dstep]


def _block_diag(v, rows_first):
    g, a, b = v.shape
    nb = g // GROUPS_PER_BLOCK
    eye = jnp.eye(GROUPS_PER_BLOCK, dtype=v.dtype)
    v = v.reshape(nb, GROUPS_PER_BLOCK, a, 1, b) * eye[None, :, None, :, None]
    return v.reshape(nb, GROUPS_PER_BLOCK * a, GROUPS_PER_BLOCK * b)


def _block_diag_take(v, a, b):
    nb = v.shape[0]
    v = v.reshape(nb, GROUPS_PER_BLOCK, a, GROUPS_PER_BLOCK, b)
    return jnp.stack([v[:, g, :, g, :] for g in range(GROUPS_PER_BLOCK)], axis=1).reshape(nb * GROUPS_PER_BLOCK, a, b)


def _adamw(w, g, m, v):
    m = ADAM_B1 * m + (1.0 - ADAM_B1) * g
    v = ADAM_B2 * v + (1.0 - ADAM_B2) * (g * g)
    m_hat = m / (1.0 - ADAM_B1**ADAM_STEP)
    v_hat = v / (1.0 - ADAM_B2**ADAM_STEP)
    delta = -ADAM_LR * (m_hat / (jnp.sqrt(v_hat) + ADAM_EPS) + ADAM_WD * w)
    return delta, m, v


def _adamw_rows(name, w, g, m, v):
    wd = w.shape[1]
    return _rowwise(name, lambda a, b, c, d: (list(_adamw(a, b, c, d)), []), [w, g, m, v], [], [(wd, F32)] * 3, [], tm=128)


SMALL = ["b_ada", "norm1_g", "q_norm_g", "k_norm_g", "lam_re", "lam_im", "log_step", "b_re", "b_im", "c_re", "c_im",
         "d_skip", "b_glu", "attn_out_g", "ssm_out_g", "norm2_g"]
LARGE = ["w_in", "w_glu", "w_out", "w_ff1", "w_ff2"]
ORDER = ["w_ada", "b_ada", "norm1_g", "w_in", "q_norm_g", "k_norm_g", "lam_re", "lam_im", "log_step", "b_re", "b_im", "c_re",
         "c_im", "d_skip", "w_glu", "b_glu", "attn_out_g", "ssm_out_g", "w_out", "norm2_g", "w_ff1", "w_ff2"]


def _pack(arrs):
    flat = jnp.concatenate([a.reshape(-1) for a in arrs])
    rows = -(-flat.shape[0] // PACK_LANES)
    rows = -(-rows // SUBLANES) * SUBLANES
    return jnp.pad(flat, (0, rows * PACK_LANES - flat.shape[0])).reshape(rows, PACK_LANES)


def _unpack(packed, shapes):
    flat, out, o = packed.reshape(-1), [], 0
    for sh in shapes:
        n = int(np.prod(sh))
        out.append(flat[o : o + n].reshape(sh))
        o += n
    return out


def kernel(x, c, w_ada, b_ada, norm1_g, w_in, q_norm_g, k_norm_g, lam_re, lam_im, log_step, b_re, b_im, c_re, c_im, d_skip, w_glu, b_glu, attn_out_g, ssm_out_g, w_out, norm2_g, w_ff1, w_ff2, loss_target, m_w_ada, m_b_ada, m_norm1_g, m_w_in, m_q_norm_g, m_k_norm_g, m_lam_re, m_lam_im, m_log_step, m_b_re, m_b_im, m_c_re, m_c_im, m_d_skip, m_w_glu, m_b_glu, m_attn_out_g, m_ssm_out_g, m_w_out, m_norm2_g, m_w_ff1, m_w_ff2, v_w_ada, v_b_ada, v_norm1_g, v_w_in, v_q_norm_g, v_k_norm_g, v_lam_re, v_lam_im, v_log_step, v_b_re, v_b_im, v_c_re, v_c_im, v_d_skip, v_w_glu, v_b_glu, v_attn_out_g, v_ssm_out_g, v_w_out, v_norm2_g, v_w_ff1, v_w_ff2):
    args = dict(locals())
    wts = {n: args[n] for n in ORDER}
    mom = {n: args["m_" + n] for n in ORDER}
    var = {n: args["v_" + n] for n in ORDER}

    x2, tgt = x[0], loss_target[0]
    s, d = x2.shape
    aw = d // 2
    sw = d - aw
    n_groups = sw // SSM_GROUP
    nst = n_groups * STATE_DIM
    chip = 2 * lax.axis_index("x") + lax.axis_index("y")

    chip_idx = jnp.reshape(chip, (1,)).astype(jnp.int32)
    core_idx = jnp.reshape(lax.axis_index("c"), (1,)).astype(jnp.int32)

    def cast_into_stack(n):
        w2 = wts[n][0]
        rows, cols = w2.shape
        tr = _blk(rows, 256)
        return _indexed("cast_" + n, lambda a: a, chip_idx, [(w2, (tr, cols), lambda r, ix: (r, 0))],
                        (((4, rows, cols), BF16), (None, tr, cols), lambda r, ix: (ix[0], r, 0)), (rows // tr,))

    stacks = {n: cast_into_stack(n) for n in LARGE}
    dff = 4 * wts["w_ff2"].shape[1]

    c_all = _allgather8("gather_c", jnp.pad(c, ((0, SUBLANES - 1), (0, 0))))[::SUBLANES]
    n_mod = w_ada.shape[2]
    b_ada_mine = lax.dynamic_slice(b_ada, (0, chip * n_mod), (1, n_mod))
    sil_c = _single("silu_c", lambda v: [v * _sigmoid(v)], [c_all], [(c_all.shape, F32)])[0]
    mod_part = _mm("ada_fwd", sil_c, w_ada[0], "nn", F32, tn=512, epi=lambda acc, b: acc + b[0:1, :],
                   extras=[jnp.broadcast_to(b_ada_mine, (SUBLANES, n_mod))])
    mod_all = _allgather8("gather_mod", mod_part)
    me = 2 * chip + lax.axis_index("c")
    mod = jnp.concatenate([lax.dynamic_slice(mod_all, (16 * k + me, 0), (1, n_mod)) for k in range(4)], axis=1)
    sh1, sc1, g1, sh2, sc2, g2 = [mod[:, i * d : (i + 1) * d] for i in range(6)]

    (w_in_s,) = _gather_weights([stacks["w_in"]], mod_all)

    def gather_plan(refs):
        px0, py0, pc, chips = _place()
        slot = 2 * px0 + py0
        return [(r.at[slot], r.at[slot], (px, py, pc), r.at[2 * px + py]) for r in refs for (px, py) in chips]

    mix_sems_s, mix_sems_r, mix_stacks, mix_token = _split_start(
        "gather_mix_start", [stacks["w_glu"], stacks["w_out"]], gather_plan, 6, extra=[w_in_s])
    w_in_s = mix_stacks.pop()
    ff_sems_s, ff_sems_r, ff_stacks, ff_token = _split_start(
        "gather_ff_start", [stacks["w_ff1"], stacks["w_ff2"]], gather_plan, 6, extra=[w_in_s])
    w_in_s = ff_stacks.pop()
    sc1 = sc1 + (mix_token[0:1, 0:1] + ff_token[0:1, 0:1])

    def norm_mod(v, g, sc, sh):
        return [v * _rstd(v) * g * (1.0 + sc) + sh], []

    (h1b,) = _rowwise("norm1", norm_mod, [x2], [norm1_g, sc1, sh1], [(d, BF16)], [])
    proj = _mm("proj_in", h1b, w_in_s, "nn", BF16, b_cs=True, tm=FF_TM)
    assert aw == sw
    q_src, k_src, v_src, u_src = [(proj, aw, i) for i in range(3)] + [(proj, sw, 3)]

    n_heads = aw // HEAD_DIM
    up = lambda v: v.astype(F32)
    qk_orders = _stream_orders("qk_prep", lambda q, k, v, gq, gk: [up(q) * _rstd(up(q)) * gq, up(k) * _rstd(up(k)) * gk, up(v)],
                               [(proj, 0), (proj, n_heads), (proj, 2 * n_heads)], [q_norm_g, k_norm_g], 3, n_heads)

    pats = []
    for pi, (window, dil) in enumerate(DILATION_PATTERNS):
        assert window // dil == BAND and s % STREAM_ROWS == 0
        bps = s // dil // BAND
        qp, kp, vp = [order[pi] for order in qk_orders]
        o_p, l_p = _attn_fwd(f"attn_fwd_d{dil}", qp, kp, vp, bps, dil)
        pats.append((dil, bps, qp, kp, vp, o_p, _from_streams(l_p, dil)))

    def lse_total(l1, l2, l3):
        m = jnp.maximum(jnp.maximum(l1, l2), l3)
        return m + jnp.log(jnp.exp(l1 - m) + jnp.exp(l2 - m) + jnp.exp(l3 - m))

    (lse,) = _rowwise("attn_lse", lambda l1, l2, l3: ([lse_total(l1, l2, l3)], []), [p[6] for p in pats], [], [(HEAD_DIM, F32)], [])

    def combine(h, blocks, l1, l2, l3):
        tot = lse_total(l1, l2, l3)
        lane = lax.broadcasted_iota(jnp.int32, l1.shape, 1)
        mix = 0.0
        for o_blk, l in zip(blocks[0], (l1, l2, l3)):
            w = jnp.sum(jnp.where(lane == h, jnp.exp(l - tot), 0.0), axis=1, keepdims=True)
            mix = mix + w * o_blk
        return [mix]

    (attn,) = _natural_order("attn_mix", combine, [[p[5] for p in pats]], [p[6] for p in pats], [F32], n_heads)

    lam_re2, lam_im2, log_step2 = lam_re[0], lam_im[0], log_step[0].reshape(n_groups, 1)
    prep = _single("ssm_prep", _ssm_prep_fn, [lam_re2, lam_im2, log_step2], [((n_groups, STATE_DIM), F32)] * 18)
    p_r = jnp.stack([p.reshape(nst) for p in prep[:8]])
    p_i = jnp.stack([p.reshape(nst) for p in prep[8:16]])
    f_ri = jnp.stack(prep[16:])
    row = jnp.arange(SUBLANES)[:, None]
    zero = jnp.zeros((SUBLANES, nst), F32)
    sel = lambda cond, v: jnp.where(cond, jnp.broadcast_to(v[None, :], (SUBLANES, nst)), zero)
    fwd_consts = jnp.stack([sel(row >= 1, p_r[0]), sel(row >= 1, p_i[0]), sel(row >= 2, p_r[1]), sel(row >= 2, p_i[1]),
                            sel(row >= 4, p_r[3]), sel(row >= 4, p_i[3]), p_r, p_i])
    bwd_consts = jnp.stack([sel(row <= 6, p_r[0]), sel(row <= 6, p_i[0]), sel(row <= 5, p_r[1]), sel(row <= 5, p_i[1]),
                            sel(row <= 3, p_r[3]), sel(row <= 3, p_i[3]), p_r[::-1], p_i[::-1]])
    nblk = n_groups // GROUPS_PER_BLOCK
    f_rows = f_ri.reshape(2, nblk, 1, 512)
    braw_r = _block_diag(b_re[0].transpose(0, 2, 1), True)
    braw_i = _block_diag(b_im[0].transpose(0, 2, 1), True)

    def bbar_fn(br, bi, f):
        r, i = _cmul(f[0], f[1], br, bi)
        return [r, i]

    bbar_r, bbar_i = _single("ssm_bbar", bbar_fn, [braw_r, braw_i, f_rows], [(braw_r.shape, BF16)] * 2)
    ct_r, ct_i = _block_diag(c_re[0], True).astype(BF16), _block_diag(c_im[0], True).astype(BF16)
    ds2 = d_skip
    y_ssm, h_r, h_i = _ssm_fwd(u_src, bbar_r, bbar_i, ct_r.transpose(0, 2, 1), ct_i.transpose(0, 2, 1), fwd_consts, ds2)

    (yb,) = _rowwise("gelu", lambda v: ([_gelu(v)], []), [y_ssm], [], [(sw, BF16)], [])
    w_glu_s, w_out_s = _split_wait("gather_mix_wait", mix_stacks, mix_sems_s, mix_sems_r, yb, gather_plan)
    w_glu_f = w_glu_s.reshape(sw, sw)
    w_out_f = w_out_s.reshape(d, d)
    z = _mm("glu_fwd", yb, w_glu_f, "nn", BF16)

    def glu(ys, zz, b):
        return [_gelu(ys) * _sigmoid(zz.astype(F32) + b)], []

    (ssm,) = _rowwise("glu", glu, [y_ssm, z], [b_glu], [(sw, F32)], [])

    def cat_norm(a, sm, ga, gs):
        return [jnp.concatenate([a * _rstd(a) * ga, sm * _rstd(sm) * gs], axis=1)], []

    (catb,) = _rowwise("cat_norm", cat_norm, [attn, ssm], [attn_out_g, ssm_out_g], [(d, BF16)], [])
    mixed = _mm("mix_out", catb, w_out_f, "nn", BF16, tm=FF_TM)

    def resid_norm(xv, mx, g1v, g, sc, sh):
        x1v = xv + g1v * mx.astype(F32)
        return [x1v, x1v * _rstd(x1v) * g * (1.0 + sc) + sh], []

    x1, h2b = _rowwise("resid_norm2", resid_norm, [x2, mixed], [g1, norm2_g, sc2, sh2], [(d, F32), (d, BF16)], [])
    w_ff1_s, w_ff2_s = _split_wait("gather_ff_wait", ff_stacks, ff_sems_s, ff_sems_r, x1, gather_plan)
    w_ff2_f = w_ff2_s.reshape(dff, d)
    a_ff = _mm("ff1", h2b, w_ff1_s, "nn", BF16, b_cs=True, tm=FF_TM, tn=FF_TN)
    sq_relu = lambda v: jnp.square(jnp.maximum(v.astype(F32), 0.0)).astype(BF16)
    ff = _mm("ff2", a_ff, w_ff2_f, "nn", BF16, a_pro=sq_relu, tm=FF_TM)

    def loss_fn(x1v, ffv, tg, g2v):
        ffv = ffv.astype(F32)
        diff = x1v + g2v * ffv - tg
        dout = diff * (1.0 / d)
        return [dout, dout * g2v], [_colsum8(0.5 * diff * dout), _colsum8(dout * ffv)]

    dout, dffb, loss_acc, dg2_acc = _rowwise("loss", loss_fn, [x1, ff, tgt], [g2], [(d, F32), (d, BF16)], [(SUBLANES, d)] * 2)

    da_ff = _mm("ff2_bwd", dffb, w_ff2_f, "nt", BF16, epi=lambda acc, av: acc * (2.0 * jnp.maximum(av.astype(F32), 0.0)), extras=[a_ff], tm=FF_TM, tn=FF_TN)
    gw = {}
    gw["w_ff2"] = _mm("ff2_wgrad", a_ff, dffb, "tn", BF16, a_pro=sq_relu, tm=FF_TM).reshape(4, dff // 4, d)
    gw["w_ff1"] = _mm("ff1_wgrad", h2b, da_ff, "tn", BF16, o_cs=True, tm=FF_TM)

    def add2(n, g, b):
        _, half, cols = b.shape
        tr = _blk(half, 256)
        nrb = half // tr
        blk = (None, tr, cols)
        return _indexed("grad_add2_" + n, lambda p, q: p.astype(F32) + q.astype(F32), core_idx,
                        [(g, blk, lambda k, r, ix: (k, ix[0] * nrb + r, 0)), (b, blk, lambda k, r, ix: (k, r, 0))],
                        ((b.shape, BF16), blk, lambda k, r, ix: (k, r, 0)), (4, nrb))

    def scatter_plan(refs):
        px0, py0, pc, chips = _place()
        nw = len(refs) // 2
        return [(refs[w].at[2 * px + py], refs[nw + w].at[j], (px, py, pc), refs[nw + w].at[j])
                for w in range(nw) for j, (px, py) in enumerate(chips)]

    def swap_plan(refs):
        px0, py0, pc, _ = _place()
        nw = len(refs) // 2
        plan = []
        for w in range(nw):
            half = refs[w].shape[1] // 2
            plan.append((refs[w].at[:, pl.ds((1 - pc) * half, half), :], refs[nw + w], (px0, py0, 1 - pc), refs[nw + w]))
        return plan

    early = ["w_ff2", "w_ff1"]
    sw_lands = [lax.empty((4, gw[n].shape[1] // 2, gw[n].shape[2]), BF16) for n in early]
    sw_sems_s, sw_sems_r, sw_bufs, sw_token = _split_start("swap_ff_start", [gw[n] for n in early] + sw_lands, swap_plan, len(early))
    sc2_late = sc2 + sw_token[0:1, 0:1]

    dh2 = _mm("ff1_bwd", da_ff, w_ff1_s, "nt", BF16, b_cs=True, tm=FF_TM)

    def norm2_bwd(dh, x1v, do, mx, g, sc, g1v):
        dh, mx = dh.astype(F32), mx.astype(F32)
        r = _rstd(x1v)
        xh = x1v * r
        dn = dh * (1.0 + sc)
        dx = do + _rms_bwd(dn * g, xh, r)
        return [dx, dx * g1v], [_colsum8(dh * xh * g), _colsum8(dh), _colsum8(dn * xh), _colsum8(dx * mx)]

    dx1, dmixb, dsc2_acc, dsh2_acc, dn2g_acc, dg1_acc = _rowwise(
        "norm2_bwd", norm2_bwd, [dh2, x1, dout, mixed], [norm2_g, sc2_late, g1], [(d, F32), (d, BF16)], [(SUBLANES, d)] * 4)

    sw_bufs = _split_wait("swap_ff_wait", sw_bufs, sw_sems_s, sw_sems_r, dmixb, swap_plan)
    parts = {n: add2(n, g, b) for n, g, b in zip(early, sw_bufs[: len(early)], sw_bufs[len(early) :])}
    lands = [lax.empty((3,) + parts[n].shape[1:], BF16) for n in early]
    sc_sems_s, sc_sems_r, sc_bufs, sc_token = _split_start("scatter_ff_start", [parts[n] for n in early] + lands, scatter_plan, 6)
    attn_out_g_late = attn_out_g + sc_token[0:1, 0:1]

    dcat = _mm("mix_out_bwd", dmixb, w_out_f, "nt", BF16, tm=FF_TM)
    gw["w_out"] = _mm("mix_out_wgrad", catb, dmixb, "tn", BF16, tm=FF_TM).reshape(4, d // 4, d)

    def cat_bwd(da, dsm, a, sm, ga, gs):
        da, dsm = da.astype(F32), dsm.astype(F32)
        ra, rs = _rstd(a), _rstd(sm)
        ah, sh = a * ra, sm * rs
        return [_rms_bwd(da * ga, ah, ra), _rms_bwd(dsm * gs, sh, rs)], [_colsum8(da * ah), _colsum8(dsm * sh)]

    dattn, dssm, dga_acc, dgs_acc = _rowwise(
        "cat_norm_bwd", cat_bwd, [(dcat, aw, 0), (dcat, sw, 1), attn, ssm], [attn_out_g_late, ssm_out_g], [(aw, F32), (sw, F32)],
        [(SUBLANES, aw), (SUBLANES, sw)])

    def glu_bwd(dsm, ys, zz, b):
        sg = _sigmoid(zz.astype(F32) + b)
        dz = dsm * _gelu(ys) * sg * (1.0 - sg)
        return [dz, dsm * sg], [_colsum8(dz)]

    dzb, dy_direct, dbglu_acc = _rowwise("glu_bwd", glu_bwd, [dssm, y_ssm, z], [b_glu], [(sw, BF16), (sw, BF16)], [(SUBLANES, sw)])
    gw["w_glu"] = _mm("glu_wgrad", yb, dzb, "tn", BF16).reshape(4, sw // 4, sw)
    dys = _mm("glu_bwd_mm", dzb, w_glu_f, "nt", BF16, epi=lambda acc, direct, ys: (acc + direct.astype(F32)) * _gelu_grad(ys), extras=[dy_direct, y_ssm])
    du, db_bar, dc_bd, da_acc, dds_acc = _ssm_bwd(dys, u_src, h_r, h_i, ct_r, ct_i, bbar_r.transpose(0, 2, 1),
                                                  bbar_i.transpose(0, 2, 1), bwd_consts, ds2)

    def delta_fn(da, a):
        return [_to_lanes([jnp.sum(p * q, axis=1, keepdims=True) for p, q in zip(_heads(da), _heads(a))])], []

    (delta,) = _rowwise("attn_delta", delta_fn, [dattn, attn], [], [(HEAD_DIM, F32)], [])
    (do_orders,) = _stream_orders("attn_do", lambda da: [da], [(dattn, 0)], [], 1, n_heads)
    dqs, dks, dvs = [], [], []
    for pi, (dil, bps, qp, kp, vp, _, _) in enumerate(pats):
        lsp, dlp = [_to_streams(t, dil) for t in (lse, delta)]
        dq_p, dk_p, dv_p = _attn_bwd(f"attn_bwd_d{dil}", qp, kp, vp, do_orders[pi], lsp, dlp, bps, dil)
        dqs.append(dq_p)
        dks.append(dk_p)
        dvs.append(dv_p)
    dq_t, dk_t, dv_t = _natural_order("attn_grad_sum", lambda h, blocks: [(b[0] + b[1]) + b[2] for b in blocks],
                                      [dqs, dks, dvs], [], [BF16] * 3, n_heads)

    def qk_bwd(dq_sum, dk_sum, dv_sum, duv, q, k, gq, gk):
        def one(dn, raw, g):
            outs, acc = [], 0.0
            for dh, h in zip(_heads(dn), _heads(raw)):
                r = _rstd(h)
                hh = h * r
                outs.append(_rms_bwd(dh * g, hh, r))
                acc = acc + _colsum8(dh * hh)
            return jnp.concatenate(outs, axis=1), acc

        dq, gq_acc = one(dq_sum.astype(F32), q.astype(F32), gq)
        dk, gk_acc = one(dk_sum.astype(F32), k.astype(F32), gk)
        return [jnp.concatenate([dq, dk, dv_sum.astype(F32), duv.astype(F32)], axis=1)], [gq_acc, gk_acc]

    dprojb, dgq_acc, dgk_acc = _rowwise("qk_norm_bwd", qk_bwd, [dq_t, dk_t, dv_t, du, q_src, k_src], [q_norm_g, k_norm_g],
                                        [(3 * aw + sw, BF16)], [(SUBLANES, HEAD_DIM)] * 2)
    gw["w_in"] = _mm("proj_in_wgrad", h1b, dprojb, "tn", BF16, o_cs=True, tm=FF_TM)
    late = [n for n in LARGE if n not in early]
    parts.update({n: add2(n, gw[n], b) for n, b in zip(late, _swap_halves([gw[n] for n in late], "swap_halves_rest"))})
    lands_late = [lax.empty((3,) + parts[n].shape[1:], BF16) for n in late]
    rest_sems_s, rest_sems_r, rest_bufs, rest_token = _split_start(
        "scatter_rest_start", [parts[n] for n in late] + lands_late, scatter_plan, 3 * len(late))
    sc1_late = sc1 + rest_token[0:1, 0:1]
    dh1 = _mm("proj_in_bwd", dprojb, w_in_s, "nt", BF16, b_cs=True, tm=FF_TM)

    def norm1_bwd(dh, xv, dxr, g, sc):
        dh = dh.astype(F32)
        r = _rstd(xv)
        xh = xv * r
        dn = dh * (1.0 + sc)
        return [dxr + _rms_bwd(dn * g, xh, r)], [_colsum8(dh * xh * g), _colsum8(dh), _colsum8(dn * xh)]

    grad_x, dsc1_acc, dsh1_acc, dn1g_acc = _rowwise("norm1_bwd", norm1_bwd, [dh1, x2, dx1], [norm1_g, sc1_late], [(d, F32)], [(SUBLANES, d)] * 3)

    def b_grad_fn(dbb, br, bi, f):
        fr, fi = f[0], f[1]
        dbr, dbi = _cmul_conj(fr, fi, dbb[0], dbb[1])
        dfr = jnp.sum(dbb[0] * br + dbb[1] * bi, axis=1, keepdims=True)
        dfi = jnp.sum(dbb[1] * br - dbb[0] * bi, axis=1, keepdims=True)
        return [dbr, dbi, dfr, dfi]

    dbr_bd, dbi_bd, df_r, df_i = _single("ssm_b_grad", b_grad_fn, [db_bar, braw_r, braw_i, f_rows],
                                         [(braw_r.shape, F32)] * 2 + [((nblk, 1, 512), F32)] * 2)

    def sums_fn(*accs):
        return [jnp.sum(a, axis=-2, keepdims=True) for a in accs]

    acc_list = [dsh1_acc, dsc1_acc, dg1_acc, dsh2_acc, dsc2_acc, dg2_acc, dn1g_acc, dgq_acc, dgk_acc, dds_acc, dbglu_acc, dga_acc,
                dgs_acc, dn2g_acc, loss_acc, da_acc]
    sums = _single("small_sums", sums_fn, acc_list, [(a.shape[:-2] + (1, a.shape[-1]), F32) for a in acc_list])
    (dsh1, dsc1, dg1, dsh2, dsc2, dg2, dn1g, dgq, dgk, dds, dbglu, dga, dgs, dn2g, loss_row, da_sum) = sums
    loss_dev = _single("loss_sum", lambda v: [jnp.sum(v, axis=1, keepdims=True)], [loss_row], [((1, 1), F32)])[0]
    gp = (n_groups, STATE_DIM)
    dlam_re, dlam_im, dlog_step = _single(
        "ssm_param_grad", _ssm_param_grad_fn,
        [lam_re2, lam_im2, log_step2, da_sum[0].reshape(gp), da_sum[1].reshape(gp), df_r.reshape(gp), df_i.reshape(gp)],
        [(gp, F32), (gp, F32), ((n_groups, 1), F32)])
    small_dev = {
        "b_ada": jnp.concatenate([dsh1, dsc1, dg1, dsh2, dsc2, dg2], axis=1),
        "norm1_g": dn1g, "q_norm_g": dgq, "k_norm_g": dgk, "lam_re": dlam_re, "lam_im": dlam_im, "log_step": dlog_step,
        "b_re": _block_diag_take(dbr_bd, SSM_GROUP, STATE_DIM).transpose(0, 2, 1),
        "b_im": _block_diag_take(dbi_bd, SSM_GROUP, STATE_DIM).transpose(0, 2, 1),
        "c_re": _block_diag_take(dc_bd[0], SSM_GROUP, STATE_DIM), "c_im": _block_diag_take(dc_bd[1], SSM_GROUP, STATE_DIM),
        "d_skip": dds, "b_glu": dbglu, "attn_out_g": dga, "ssm_out_g": dgs, "norm2_g": dn2g,
    }

    packed = _pack([small_dev[n] for n in SMALL])
    rows = packed.shape[0]
    tr = _blk(rows, 256)
    me_idx = jnp.reshape(me, (1,)).astype(jnp.int32)
    gathered = _indexed("pack_small", lambda a: a, me_idx, [(packed, (tr, PACK_LANES), lambda r, ix: (r, 0))],
                        (((8, rows, PACK_LANES), F32), (None, tr, PACK_LANES), lambda r, ix: (ix[0], r, 0)), (rows // tr,))

    def small_plan(refs):
        (g,) = refs
        px0, py0, pc0, _ = _place()
        flip = lambda v, bit: 1 - v if bit else v
        plan = []
        for mask in range(1, 8):
            px, py, pc = flip(px0, mask & 4), flip(py0, mask & 2), flip(pc0, mask & 1)
            mine = g.at[4 * px0 + 2 * py0 + pc0]
            plan.append((mine, mine, (px, py, pc), g.at[4 * px + 2 * py + pc]))
        return plan

    small_sems_s, small_sems_r, (gathered,), small_token = _split_start("gather_small_start", [gathered], small_plan, 7)

    sc_bufs = _split_wait("scatter_ff_wait", sc_bufs, sc_sems_s, sc_sems_r, small_token, scatter_plan)
    parts.update(zip(early, sc_bufs[: len(early)]))
    got2 = dict(zip(early, sc_bufs[len(early) :]))
    rest_bufs = _split_wait("scatter_rest_wait", rest_bufs, rest_sems_s, rest_sems_r, small_token, scatter_plan)
    parts.update(zip(late, rest_bufs[: len(late)]))
    got2.update(zip(late, rest_bufs[len(late) :]))
    place_idx = jnp.stack([chip, lax.axis_index("c")]).astype(jnp.int32)

    def add4(n, p, b):
        _, half, cols = p.shape
        tr = _blk(half, 256)
        blk = (None, tr, cols)
        fn = lambda own, q0, q1, q2: ((own.astype(F32) + q0.astype(F32)) + q1.astype(F32)) + q2.astype(F32)
        srcs = [(p, blk, lambda r, ix: (ix[0], r, 0))] + [(b, blk, lambda r, ix, j=j: (j, r, 0)) for j in range(3)]
        return _indexed("grad_add4_" + n, fn, place_idx, srcs, (((2, half, cols), F32), blk, lambda r, ix: (ix[1], r, 0)), (half // tr,))

    def join_plan(refs):
        px0, py0, pc, _ = _place()
        return [(r.at[pc], r.at[pc], (px0, py0, 1 - pc), r.at[1 - pc]) for r in refs]

    out = {}

    def update_large(names, fulls):
        last = None
        for n, full in zip(names, fulls):
            g = full.reshape(wts[n].shape[1:])
            last = _adamw_rows("adamw_" + n, wts[n][0], g, mom[n][0], var[n][0])
            for kind, val in zip(("grad", "delta", "new_m", "new_v"), (g,) + tuple(last)):
                out[kind, n] = val[None]
        return last[0]

    rest = [n for n in LARGE if n not in early]
    ja_s, ja_r, fulls_a, _ = _split_start("join_ff_start", [add4(n, parts[n], got2[n]) for n in early], join_plan, len(early))
    jb_s, jb_r, fulls_b, jb_token = _split_start("join_rest_start", [add4(n, parts[n], got2[n]) for n in rest], join_plan, len(rest))
    done_a = update_large(early, _split_wait("join_ff_wait", fulls_a, ja_s, ja_r, jb_token, join_plan))
    done_b = update_large(rest, _split_wait("join_rest_wait", fulls_b, jb_s, jb_r, done_a, join_plan))
    (gathered,) = _split_wait("gather_small_wait", [gathered], small_sems_s, small_sems_r, done_b, small_plan)

    def sum8_fn(g):
        acc = g[0]
        for i in range(1, 8):
            acc = acc + g[i]
        return [acc]

    g_small = _single("sum_small", sum8_fn, [gathered], [((rows, PACK_LANES), F32)])[0]

    w_small, m_small, v_small = [_pack([src[n] for n in SMALL]) for src in (wts, mom, var)]
    upd_small = _adamw_rows("adamw_small", w_small, g_small, m_small, v_small)
    shapes = [wts[n].shape for n in SMALL]
    for kind, arr in zip(("grad", "delta", "new_m", "new_v"), (g_small,) + tuple(upd_small)):
        for n, val in zip(SMALL, _unpack(arr, shapes)):
            out[kind, n] = val

    dmod_all = gathered[:, 0 : -(-6 * d // PACK_LANES), :].reshape(8, -1)[:, : 6 * d]
    dmod_mine = lax.dynamic_slice(dmod_all, (0, chip * n_mod), (8, n_mod))
    g_ada = _mm("ada_wgrad", sil_c, dmod_mine, "tn", F32, tm=512, tn=512)
    upd = _adamw_rows("adamw_w_ada", w_ada[0], g_ada, m_w_ada[0], v_w_ada[0])
    for kind, val in zip(("grad", "delta", "new_m", "new_v"), (g_ada,) + tuple(upd)):
        out[kind, "w_ada"] = val[None]

    loss = lax.psum(loss_dev[0, 0], ("x", "y", "c"))
    return (loss, grad_x[None], *[out[kind, n] for kind in ("grad", "delta", "new_m", "new_v") for n in ORDER])
```

```python
import functools
import math

import jax
import jax.numpy as jnp
import numpy as np
from jax import lax
from jax.experimental import pallas as pl
from jax.experimental.pallas import tpu as pltpu

F32 = jnp.float32
BF16 = jnp.bfloat16
MESH = pl.DeviceIdType.MESH

EPS = 1e-6
HEAD_DIM = 128
SSM_GROUP = 16
STATE_DIM = 64
GROUPS_PER_BLOCK = 8
DILATION_PATTERNS = ((128, 1), (512, 4), (2048, 16))
BAND = 128
NEG = -1e30

ADAM_LR, ADAM_B1, ADAM_B2, ADAM_EPS, ADAM_WD, ADAM_STEP = 0.001, 0.9, 0.999, 1e-08, 0.01, 10

V7X_VMEM_BYTES = 64 * 2**20
VMEM_LIMIT = 48 * 2**20
SUBLANES = 8
PACK_LANES = 1024


def _cparams(sem=None):
    return pltpu.CompilerParams(dimension_semantics=sem, vmem_limit_bytes=VMEM_LIMIT)


def _blk(n, want):
    b = min(n, want)
    while n % b:
        b //= 2
    return b


def _colsum8(v):
    tm, w = v.shape
    return v.reshape(tm // SUBLANES, SUBLANES, w).sum(axis=0)


def _rowwise(name, fn, rows, vecs, out_rows, out_accs, tm=256, n_rows=None):
    rows = [r if isinstance(r, tuple) else (r, r.shape[1], 0) for r in rows]
    rows = [r if len(r) == 4 else r + (0,) for r in rows]
    s = rows[0][0].shape[0] if n_rows is None else n_rows
    tm = _blk(s, tm)
    assert all(r[3] % tm == 0 for r in rows)
    n_in, n_or = len(rows) + len(vecs), len(out_rows)

    def body(*refs):
        outs, accs = fn(*[r[...] for r in refs[:n_in]])
        for r, v in zip(refs[n_in : n_in + n_or], outs):
            r[...] = v.astype(r.dtype)
        if out_accs:
            acc_refs = refs[n_in + n_or :]

            @pl.when(pl.program_id(0) == 0)
            def _():
                for r in acc_refs:
                    r[...] = jnp.zeros(r.shape, r.dtype)

            for r, v in zip(acc_refs, accs):
                r[...] += v

    in_specs = [pl.BlockSpec((tm, w), lambda i, cb=cb, rb=r0 // tm: (i + rb, cb)) for (_, w, cb, r0) in rows]
    in_specs += [pl.BlockSpec(v.shape, lambda i, nd=v.ndim: (0,) * nd) for v in vecs]
    out_specs = [pl.BlockSpec((tm, w), lambda i: (i, 0)) for (w, _) in out_rows]
    out_specs += [pl.BlockSpec(sh, lambda i, nd=len(sh): (0,) * nd) for sh in out_accs]
    out_shape = [jax.ShapeDtypeStruct((s, w), dt) for (w, dt) in out_rows]
    out_shape += [jax.ShapeDtypeStruct(sh, F32) for sh in out_accs]
    res = pl.pallas_call(
        body, name=name, grid=(s // tm,), in_specs=in_specs, out_specs=out_specs, out_shape=out_shape,
        compiler_params=_cparams(("arbitrary",)),
    )(*[r[0] for r in rows], *vecs)
    return res


def _single(name, fn, ins, out_shapes):
    n_in = len(ins)

    def body(*refs):
        outs = fn(*[r[...] for r in refs[:n_in]])
        for r, v in zip(refs[n_in:], outs):
            r[...] = v.astype(r.dtype)

    vm = pl.BlockSpec(memory_space=pltpu.VMEM)
    return pl.pallas_call(
        body, name=name, in_specs=[vm] * n_in, out_specs=[vm] * len(out_shapes),
        out_shape=[jax.ShapeDtypeStruct(sh, dt) for sh, dt in out_shapes], compiler_params=_cparams(),
    )(*ins)


_NN = (((1,), (0,)), ((), ()))
_NT = (((1,), (1,)), ((), ()))
_TN = (((0,), (0,)), ((), ()))


def _mm(name, a, b, mode, out_dtype, *, tm=512, tn=1024, tk=2048, b_cs=False, o_cs=False, epi=None, extras=(), a_pro=None):
    if mode == "tn":
        k, m = a.shape
        n = b.shape[1]
    else:
        m, k = a.shape
        if mode == "nn":
            n = b.shape[2] * 4 if b_cs else b.shape[1]
        else:
            n = b.shape[1] if b_cs else b.shape[0]
    tm = _blk(m, tm)
    tn = _blk(n // 4 if (b_cs and mode == "nn") or o_cs else n, tn)
    tk = _blk(k // 4 if (b_cs and mode == "nt") else k, tk)
    nk = k // tk
    dims = {"nn": _NN, "nt": _NT, "tn": _TN}[mode]
    n_ex = len(extras)

    def body(*refs):
        a_ref, b_ref = refs[0], refs[1]
        ex_refs, o_ref = refs[2 : 2 + n_ex], refs[2 + n_ex]

        def finish(acc):
            if epi is not None:
                acc = epi(acc, *[r[...] for r in ex_refs])
            o_ref[...] = acc.astype(o_ref.dtype)

        a_blk = a_ref[...] if a_pro is None else a_pro(a_ref[...])
        part = lax.dot_general(a_blk, b_ref[...], dims, preferred_element_type=F32)
        if nk == 1:
            finish(part)
        else:
            acc_ref = refs[3 + n_ex]
            kk = pl.program_id(2)

            @pl.when(kk == 0)
            def _():
                acc_ref[...] = part

            @pl.when(kk > 0)
            def _():
                acc_ref[...] += part

            @pl.when(kk == nk - 1)
            def _():
                finish(acc_ref[...])

    a_spec = pl.BlockSpec((tk, tm), lambda i, j, kk: (kk, i)) if mode == "tn" else pl.BlockSpec((tm, tk), lambda i, j, kk: (i, kk))
    if mode == "nn":
        if b_cs:
            per = (n // 4) // tn
            b_spec = pl.BlockSpec((None, tk, tn), lambda i, j, kk: (j // per, kk, j % per))
        else:
            b_spec = pl.BlockSpec((tk, tn), lambda i, j, kk: (kk, j))
    elif mode == "nt":
        if b_cs:
            per = (k // 4) // tk
            b_spec = pl.BlockSpec((None, tn, tk), lambda i, j, kk: (kk // per, j, kk % per))
        else:
            b_spec = pl.BlockSpec((tn, tk), lambda i, j, kk: (j, kk))
    else:
        b_spec = pl.BlockSpec((tk, tn), lambda i, j, kk: (kk, j))
    if o_cs:
        per = (n // 4) // tn
        o_spec = pl.BlockSpec((None, tm, tn), lambda i, j, kk: (j // per, i, j % per))
        o_shape = jax.ShapeDtypeStruct((4, m, n // 4), out_dtype)
    else:
        o_spec = pl.BlockSpec((tm, tn), lambda i, j, kk: (i, j))
        o_shape = jax.ShapeDtypeStruct((m, n), out_dtype)
    ex_specs = [pl.BlockSpec((tm, tn), lambda i, j, kk: (i, j)) for _ in extras]
    return pl.pallas_call(
        body, name=name, grid=(m // tm, n // tn, nk), in_specs=[a_spec, b_spec] + ex_specs, out_specs=o_spec, out_shape=o_shape,
        scratch_shapes=[pltpu.VMEM((tm, tn), F32)] if nk > 1 else [],
        compiler_params=_cparams(("parallel", "parallel", "arbitrary")),
    )(a, b, *extras)


def _place():
    x, y, c = lax.axis_index("x"), lax.axis_index("y"), lax.axis_index("c")
    return x, y, c, [(1 - x, y), (x, 1 - y), (1 - x, 1 - y)]


def _allgather8(name, v):
    m_per, n = v.shape

    def body(x_ref, out_ref, send_sems, recv_sems, local_sem):
        x, y, c, chips = _place()
        me, sibling = (x, y, c), (x, y, 1 - c)

        def rows(px, py, pc):
            return out_ref.at[pl.ds((4 * px + 2 * py + pc) * m_per, m_per), :]

        def copy(k, block, to, src=None):
            return pltpu.make_async_remote_copy(
                src_ref=rows(*block) if src is None else src, dst_ref=rows(*block), send_sem=send_sems.at[k],
                recv_sem=recv_sems.at[k], device_id=to, device_id_type=MESH)

        mine = pltpu.make_async_copy(x_ref, rows(*me), local_sem)
        mine.start()
        first = [copy(0, me, sibling, src=x_ref)]
        first += [copy(1 + j, me, (*chip, c), src=x_ref) for j, chip in enumerate(chips)]
        for cp in first:
            cp.start()
        passed = [copy(4 + j, (*chip, c), sibling) for j, chip in enumerate(chips)]
        for j, chip in enumerate(chips):
            copy(1 + j, (*chip, c), me).wait_recv()
            passed[j].start()
        copy(0, sibling, me).wait_recv()
        for j, chip in enumerate(chips):
            copy(4 + j, (*chip, 1 - c), me).wait_recv()
        for cp in first + passed:
            cp.wait_send()
        mine.wait()

    return pl.pallas_call(
        body, name=name, out_shape=jax.ShapeDtypeStruct((8 * m_per, n), v.dtype),
        in_specs=[pl.BlockSpec(memory_space=pltpu.VMEM)], out_specs=pl.BlockSpec(memory_space=pltpu.VMEM),
        scratch_shapes=[pltpu.SemaphoreType.DMA((7,)), pltpu.SemaphoreType.DMA((7,)), pltpu.SemaphoreType.DMA],
        compiler_params=_cparams(),
    )(v)


def _gather_weights(stacks, after):
    nw = len(stacks)

    def body(*refs):
        outs = refs[nw + 1 : 2 * nw + 1]
        send_sems, recv_sems = refs[2 * nw + 1 :]
        x, y, c, chips = _place()
        me, sibling = (x, y, c), (x, y, 1 - c)

        def copy(w, k, block, to):
            px, py, pc = block
            half = outs[w].shape[1] // 2
            rows = outs[w].at[2 * px + py, pl.ds(pc * half, half), :]
            return pltpu.make_async_remote_copy(src_ref=rows, dst_ref=rows, send_sem=send_sems.at[6 * w + k],
                                                recv_sem=recv_sems.at[6 * w + k], device_id=to, device_id_type=MESH)

        sent = [copy(w, j, me, (*chip, c)) for w in range(nw) for j, chip in enumerate(chips)]
        for cp in sent:
            cp.start()
        for w in range(nw):
            for j, chip in enumerate(chips):
                copy(w, j, (*chip, c), me).wait_recv()
                passed = copy(w, 3 + j, (*chip, c), sibling)
                passed.start()
                sent.append(passed)
        for w in range(nw):
            for j, chip in enumerate(chips):
                copy(w, 3 + j, (*chip, 1 - c), me).wait_recv()
        for cp in sent:
            cp.wait_send()

    hbm = pl.BlockSpec(memory_space=pl.ANY)
    return pl.pallas_call(
        body, name="gather_weights", out_shape=[jax.ShapeDtypeStruct(w.shape, w.dtype) for w in stacks],
        in_specs=[hbm] * (nw + 1), out_specs=[hbm] * nw, input_output_aliases={i: i for i in range(nw)},
        scratch_shapes=[pltpu.SemaphoreType.DMA((6 * nw,)), pltpu.SemaphoreType.DMA((6 * nw,))],
        compiler_params=_cparams(),
    )(*stacks, after)


def _swap_halves(gs, name):
    nw = len(gs)

    def body(*refs):
        ins, gots = refs[:nw], refs[nw : 2 * nw]
        send_sems, recv_sems = refs[2 * nw :]
        x, y, c, _ = _place()
        cps = []
        for w in range(nw):
            half = ins[w].shape[1] // 2
            cps.append(pltpu.make_async_remote_copy(
                src_ref=ins[w].at[:, pl.ds((1 - c) * half, half), :], dst_ref=gots[w], send_sem=send_sems.at[w],
                recv_sem=recv_sems.at[w], device_id=(x, y, 1 - c), device_id_type=MESH))
            cps[-1].start()
        for cp in cps:
            cp.wait()

    hbm = pl.BlockSpec(memory_space=pl.ANY)
    return pl.pallas_call(
        body, name=name, out_shape=[jax.ShapeDtypeStruct((4, g.shape[1] // 2, g.shape[2]), g.dtype) for g in gs],
        in_specs=[hbm] * nw, out_specs=[hbm] * nw,
        scratch_shapes=[pltpu.SemaphoreType.DMA((nw,)), pltpu.SemaphoreType.DMA((nw,))], compiler_params=_cparams(),
    )(*gs)


def _split_start(name, bufs, plan, n, extra=()):
    n_plan = len(bufs)
    bufs = list(bufs) + list(extra)
    nb = len(bufs)

    def body(*refs):
        send_sems, recv_sems, token = refs[nb], refs[nb + 1], refs[-1]
        for k, (src, dst, to, _) in enumerate(plan(refs[:n_plan])):
            pltpu.make_async_remote_copy(src_ref=src, dst_ref=dst, send_sem=send_sems.at[k], recv_sem=recv_sems.at[k],
                                         device_id=to, device_id_type=MESH).start()
        token[...] = jnp.zeros(token.shape, token.dtype)

    hbm, sem = pl.BlockSpec(memory_space=pltpu.HBM), pl.BlockSpec(memory_space=pltpu.SEMAPHORE)
    res = pl.pallas_call(
        body, name=name,
        out_shape=(pltpu.SemaphoreType.DMA((n,)), pltpu.SemaphoreType.DMA((n,)), *[pltpu.HBM(b.shape, b.dtype) for b in bufs],
                   jax.ShapeDtypeStruct((SUBLANES, 128), F32)),
        in_specs=(hbm,) * nb, out_specs=(sem, sem) + (hbm,) * nb + (pl.BlockSpec(memory_space=pltpu.VMEM),),
        input_output_aliases={i: 2 + i for i in range(nb)},
        compiler_params=pltpu.CompilerParams(has_side_effects=pltpu.SideEffectType.DATAFLOW_SIDE_EFFECTING),
    )(*[pltpu.with_memory_space_constraint(b, pltpu.HBM) for b in bufs])
    return res[0], res[1], list(res[2 : 2 + nb]), res[-1]


def _split_wait(name, bufs, send_sems, recv_sems, after, plan):
    nb = len(bufs)

    def body(*refs):
        send, recv = refs[nb], refs[nb + 1]
        for k, (src, _, to, land) in enumerate(plan(refs[:nb])):
            cp = pltpu.make_async_remote_copy(src_ref=src, dst_ref=land, send_sem=send.at[k], recv_sem=recv.at[k],
                                              device_id=to, device_id_type=MESH)
            cp.wait_send()
            cp.wait_recv()

    hbm, sem = pl.BlockSpec(memory_space=pltpu.HBM), pl.BlockSpec(memory_space=pltpu.SEMAPHORE)
    return pl.pallas_call(
        body, name=name, out_shape=tuple(pltpu.HBM(b.shape, b.dtype) for b in bufs),
        in_specs=(hbm,) * nb + (sem, sem, pl.BlockSpec(memory_space=pl.ANY)), out_specs=(hbm,) * nb,
        input_output_aliases={i: i for i in range(nb)},
        compiler_params=pltpu.CompilerParams(has_side_effects=pltpu.SideEffectType.DATAFLOW_SIDE_EFFECTING),
    )(*bufs, send_sems, recv_sems, after)


def _indexed(name, fn, idx, ins, out, grid):
    def body(idx_ref, *refs):
        refs[-1][...] = fn(*[r[...] for r in refs[:-1]]).astype(refs[-1].dtype)

    (o_shape, o_dtype), o_block, o_map = out
    return pl.pallas_call(
        body, name=name, out_shape=jax.ShapeDtypeStruct(o_shape, o_dtype),
        grid_spec=pltpu.PrefetchScalarGridSpec(
            num_scalar_prefetch=1, grid=grid, in_specs=[pl.BlockSpec(blk, mp) for (_, blk, mp) in ins],
            out_specs=pl.BlockSpec(o_block, o_map)),
        compiler_params=_cparams(("arbitrary",) * len(grid)),
    )(idx, *[a for (a, _, _) in ins])


def _rstd(v):
    return lax.rsqrt(jnp.mean(v * v, axis=-1, keepdims=True) + EPS)


def _rms_bwd(dy, xh, r):
    return r * (dy - xh * jnp.mean(dy * xh, axis=-1, keepdims=True))


def _heads(v):
    return [v[:, h * HEAD_DIM : (h + 1) * HEAD_DIM] for h in range(v.shape[1] // HEAD_DIM)]


def _gelu(v):
    k = math.sqrt(2.0 / math.pi)
    return 0.5 * v * (1.0 + jnp.tanh(k * (v + 0.044715 * v * v * v)))


def _gelu_grad(v):
    k = math.sqrt(2.0 / math.pi)
    t = jnp.tanh(k * (v + 0.044715 * v * v * v))
    return 0.5 * (1.0 + t) + 0.5 * v * (1.0 - t * t) * k * (1.0 + 3 * 0.044715 * v * v)


def _sigmoid(v):
    return 1.0 / (1.0 + jnp.exp(-v))


def _slopes(n_heads):
    return [2.0 ** (-8.0 * (h + 1.0) / n_heads) for h in range(n_heads)]


def _band_iotas():
    ri = lax.broadcasted_iota(jnp.int32, (BAND, BAND), 0)
    ci = lax.broadcasted_iota(jnp.int32, (BAND, BAND), 1)
    return ri, ci


def _to_lanes(cols):
    rows = cols[0].shape[0]
    lane = lax.broadcasted_iota(jnp.int32, (rows, HEAD_DIM), 1)
    out = jnp.zeros((rows, HEAD_DIM), F32)
    for h, col in enumerate(cols):
        out = jnp.where(lane == h, col, out)
    return out


def _from_lanes(v, h, width):
    return jnp.broadcast_to(v[:, h : h + 1], (v.shape[0], width))


def _attn_fwd(name, q, k, v, bps, dil):
    s, aw = q.shape
    n_heads, nb = aw // HEAD_DIM, s // BAND
    scale = HEAD_DIM**-0.5
    slopes = _slopes(n_heads)

    assert nb % 2 == 0

    def body(q_ref, kp_ref, kc_ref, vp_ref, vc_ref, o_ref, l_ref):
        i = pl.program_id(0)
        ri, ci = _band_iotas()
        dist_c = (ri - ci).astype(F32)
        dist_p = (ri - ci + BAND).astype(F32)
        valid_c = ci <= ri
        hs = range(n_heads)
        sls = [slice(h * HEAD_DIM, (h + 1) * HEAD_DIM) for h in hs]
        lo, hi = slice(0, BAND), slice(BAND, 2 * BAND)
        chains = []
        for sub, rows in enumerate((lo, hi)):
            valid_p = ci >= ri + jnp.where((2 * i + sub) % bps == 0, BAND, 0)
            for h in hs:
                k_prev = kp_ref[:, sls[h]] if sub == 0 else kc_ref[lo, sls[h]]
                v_prev = vp_ref[:, sls[h]] if sub == 0 else vc_ref[lo, sls[h]]
                chains.append((q_ref[rows, sls[h]], kc_ref[rows, sls[h]], k_prev, vc_ref[rows, sls[h]], v_prev, h, valid_p))
        cs = range(len(chains))
        sc = [lax.dot_general(chains[c][0], chains[c][1], _NT, preferred_element_type=F32) for c in cs]
        sp = [lax.dot_general(chains[c][0], chains[c][2], _NT, preferred_element_type=F32) for c in cs]
        sc = [jnp.where(valid_c, sc[c] * scale - (slopes[chains[c][5]] * dil) * dist_c, NEG) for c in cs]
        sp = [jnp.where(chains[c][6], sp[c] * scale - (slopes[chains[c][5]] * dil) * dist_p, NEG) for c in cs]
        m = [jnp.maximum(jnp.max(sc[c], axis=1, keepdims=True), jnp.max(sp[c], axis=1, keepdims=True)) for c in cs]
        pc = [jnp.exp(sc[c] - m[c]) for c in cs]
        pp = [jnp.exp(sp[c] - m[c]) for c in cs]
        den = [jnp.sum(pc[c], axis=1, keepdims=True) + jnp.sum(pp[c], axis=1, keepdims=True) for c in cs]
        acc = [jnp.dot(pc[c].astype(BF16), chains[c][3], preferred_element_type=F32) for c in cs]
        acc = [acc[c] + jnp.dot(pp[c].astype(BF16), chains[c][4], preferred_element_type=F32) for c in cs]
        out = [jnp.concatenate([acc[c] / den[c] for c in cs[sub * n_heads : (sub + 1) * n_heads]], axis=1) for sub in range(2)]
        o_ref[...] = jnp.concatenate(out, axis=0).astype(o_ref.dtype)
        lse = [_to_lanes([m[c] + jnp.log(den[c]) for c in cs[sub * n_heads : (sub + 1) * n_heads]]) for sub in range(2)]
        l_ref[...] = jnp.concatenate(lse, axis=0)

    cur = pl.BlockSpec((2 * BAND, aw), lambda i: (i, 0))
    prev = pl.BlockSpec((BAND, aw), lambda i: (jnp.maximum(2 * i - 1, 0), 0))
    return pl.pallas_call(
        body, name=name, grid=(nb // 2,), in_specs=[cur, prev, cur, prev, cur],
        out_specs=[cur, pl.BlockSpec((2 * BAND, HEAD_DIM), lambda i: (i, 0))],
        out_shape=[jax.ShapeDtypeStruct((s, aw), BF16), jax.ShapeDtypeStruct((s, HEAD_DIM), F32)],
        compiler_params=_cparams(("arbitrary",)),
    )(q, k, k, v, v)


def _attn_bwd(name, q, k, v, do, lse, delta, bps, dil):
    s, aw = q.shape
    n_heads, nb = aw // HEAD_DIM, s // BAND
    scale = HEAD_DIM**-0.5
    slopes = _slopes(n_heads)

    assert nb % 2 == 0

    def body(qc_ref, qn_ref, doc_ref, don_ref, lc_ref, ln_ref, dc_ref, dn_ref, k_ref, v_ref, dq_ref, dk_ref, dv_ref, carry):
        i = pl.program_id(0)

        @pl.when(i == 0)
        def _():
            carry[...] = jnp.zeros(carry.shape, F32)

        ri, ci = _band_iotas()
        dist_c = (ri - ci).astype(F32)
        dist_p = (ri - ci + BAND).astype(F32)
        valid_c = ci <= ri
        valid_ab = ci >= ri + jnp.where((2 * i + 1) % bps != 0, 0, BAND)
        valid_bn = ci >= ri + jnp.where(jnp.logical_and((2 * i + 2) % bps != 0, 2 * i + 2 < nb), 0, BAND)
        carried = carry[...]
        hs = range(n_heads)
        sls = [slice(h * HEAD_DIM, (h + 1) * HEAD_DIM) for h in hs]
        lo, hi, al = slice(0, BAND), slice(BAND, 2 * BAND), slice(None)
        sides = [((qc_ref, doc_ref, lc_ref, dc_ref), lo, lo, dist_c, valid_c), ((qc_ref, doc_ref, lc_ref, dc_ref), hi, lo, dist_p, valid_ab),
                 ((qc_ref, doc_ref, lc_ref, dc_ref), hi, hi, dist_c, valid_c), ((qn_ref, don_ref, ln_ref, dn_ref), al, hi, dist_p, valid_bn)]
        res = []
        for (q_ref, do_ref, l_ref, d_ref), qr, kr, dist, valid in sides:
            qs, dos = [q_ref[qr, sl] for sl in sls], [do_ref[qr, sl] for sl in sls]
            ks, vs = [k_ref[kr, sl] for sl in sls], [v_ref[kr, sl] for sl in sls]
            sc = [lax.dot_general(qs[h], ks[h], _NT, preferred_element_type=F32) for h in hs]
            dp = [lax.dot_general(dos[h], vs[h], _NT, preferred_element_type=F32) for h in hs]
            lse_v, delta_v = l_ref[qr, :], d_ref[qr, :]
            p = [jnp.where(valid, jnp.exp(sc[h] * scale - (slopes[h] * dil) * dist - _from_lanes(lse_v, h, BAND)), 0.0) for h in hs]
            ds = [(p[h] * (dp[h] - _from_lanes(delta_v, h, BAND))).astype(BF16) for h in hs]
            pb = [p[h].astype(BF16) for h in hs]
            dv = [lax.dot_general(pb[h], dos[h], _TN, preferred_element_type=F32) for h in hs]
            dk = [lax.dot_general(ds[h], qs[h], _TN, preferred_element_type=F32) for h in hs]
            dq = [jnp.dot(ds[h], ks[h], preferred_element_type=F32) for h in hs]
            res.append((dq, dk, dv))
        (dq_aa, dk_aa, dv_aa), (dq_ba, dk_ba, dv_ba), (dq_bb, dk_bb, dv_bb), (dq_nb, dk_nb, dv_nb) = res
        cat = lambda parts: jnp.concatenate(parts, axis=1)
        dq_a = carried + cat(dq_aa) * scale
        dq_b = cat([dq_ba[h] + dq_bb[h] for h in hs]) * scale
        dq_ref[...] = jnp.concatenate([dq_a, dq_b], axis=0).astype(dq_ref.dtype)
        carry[...] = cat(dq_nb) * scale
        dk_a, dk_b = cat([dk_aa[h] + dk_ba[h] for h in hs]), cat([dk_bb[h] + dk_nb[h] for h in hs])
        dk_ref[...] = (jnp.concatenate([dk_a, dk_b], axis=0) * scale).astype(dk_ref.dtype)
        dv_a, dv_b = cat([dv_aa[h] + dv_ba[h] for h in hs]), cat([dv_bb[h] + dv_nb[h] for h in hs])
        dv_ref[...] = jnp.concatenate([dv_a, dv_b], axis=0).astype(dv_ref.dtype)

    cur = pl.BlockSpec((2 * BAND, aw), lambda i: (i, 0))
    nxt = pl.BlockSpec((BAND, aw), lambda i: (jnp.minimum(2 * i + 2, nb - 1), 0))
    cur1 = pl.BlockSpec((2 * BAND, HEAD_DIM), lambda i: (i, 0))
    nxt1 = pl.BlockSpec((BAND, HEAD_DIM), lambda i: (jnp.minimum(2 * i + 2, nb - 1), 0))
    return pl.pallas_call(
        body, name=name, grid=(nb // 2,), in_specs=[cur, nxt, cur, nxt, cur1, nxt1, cur1, nxt1, cur, cur], out_specs=[cur] * 3,
        out_shape=[jax.ShapeDtypeStruct((s, aw), BF16)] * 3, scratch_shapes=[pltpu.VMEM((BAND, aw), F32)],
        compiler_params=_cparams(("arbitrary",)),
    )(q, q, do, do, lse, lse, delta, delta, k, v)


STREAM_ROWS = BAND * max(dil for _, dil in DILATION_PATTERNS)


def _stream_orders(name, fn, srcs, vecs, n_vals, n_heads):
    s = srcs[0][0].shape[0]
    dils = [dil for _, dil in DILATION_PATTERNS]
    n_in = len(srcs) + len(vecs)

    def body(*refs):
        outs, scr = refs[n_in:-1], refs[-1]
        for vi, val in enumerate(fn(*[r[...] for r in refs[:n_in]])):
            scr[...] = val
            for di, dil in enumerate(dils):
                o = outs[vi * len(dils) + di]
                if dil == 1:
                    o[...] = val.astype(o.dtype)
                    continue
                for part in range(STREAM_ROWS // (dil * BAND)):
                    for r in range(dil):
                        rows = scr[pl.ds(part * dil * BAND + r, BAND, stride=dil), :]
                        o[r, part * BAND : (part + 1) * BAND, :] = rows.astype(o.dtype)

    in_specs = [pl.BlockSpec((STREAM_ROWS, HEAD_DIM), lambda b, h, cb=cb: (b, cb + h)) for (_, cb) in srcs]
    in_specs += [pl.BlockSpec(v.shape, lambda b, h, nd=v.ndim: (0,) * nd) for v in vecs]
    out_specs, out_shape = [], []
    for _ in range(n_vals):
        for dil in dils:
            if dil == 1:
                out_specs.append(pl.BlockSpec((STREAM_ROWS, HEAD_DIM), lambda b, h: (b, h)))
                out_shape.append(jax.ShapeDtypeStruct((s, n_heads * HEAD_DIM), BF16))
            else:
                out_specs.append(pl.BlockSpec((dil, STREAM_ROWS // dil, HEAD_DIM), lambda b, h: (0, b, h)))
                out_shape.append(jax.ShapeDtypeStruct((dil, s // dil, n_heads * HEAD_DIM), BF16))
    res = pl.pallas_call(
        body, name=name, grid=(s // STREAM_ROWS, n_heads), in_specs=in_specs, out_specs=out_specs, out_shape=out_shape,
        scratch_shapes=[pltpu.VMEM((STREAM_ROWS, HEAD_DIM), F32)], compiler_params=_cparams(("arbitrary", "arbitrary")),
    )(*[a for (a, _) in srcs], *vecs)
    res = [r.reshape(s, n_heads * HEAD_DIM) for r in res]
    return [res[vi * len(dils) : (vi + 1) * len(dils)] for vi in range(n_vals)]


def _natural_order(name, fn, vals, extras, out_dtypes, n_heads):
    s = vals[0][0].shape[0]
    dils = [dil for _, dil in DILATION_PATTERNS]
    n_in = len(vals) * len(dils) + len(extras)

    def body(*refs):
        outs, scr = refs[n_in:-1], refs[-1]
        blocks = []
        for vi in range(len(vals)):
            per_pattern = []
            for di, dil in enumerate(dils):
                x = refs[vi * len(dils) + di]
                if dil == 1:
                    per_pattern.append(x[...].astype(F32))
                    continue
                for part in range(STREAM_ROWS // (dil * BAND)):
                    for r in range(dil):
                        scr[pl.ds(part * dil * BAND + r, BAND, stride=dil), :] = x[r, part * BAND : (part + 1) * BAND, :].astype(F32)
                per_pattern.append(scr[...])
            blocks.append(per_pattern)
        res = fn(pl.program_id(1), blocks, *[r[...] for r in refs[len(vals) * len(dils) : n_in]])
        for o, v in zip(outs, res):
            o[...] = v.astype(o.dtype)

    in_specs, ins = [], []
    for per_pattern in vals:
        for arr, dil in zip(per_pattern, dils):
            if dil == 1:
                in_specs.append(pl.BlockSpec((STREAM_ROWS, HEAD_DIM), lambda b, h: (b, h)))
                ins.append(arr)
            else:
                in_specs.append(pl.BlockSpec((dil, STREAM_ROWS // dil, HEAD_DIM), lambda b, h: (0, b, h)))
                ins.append(arr.reshape(dil, s // dil, n_heads * HEAD_DIM))
    in_specs += [pl.BlockSpec((STREAM_ROWS, HEAD_DIM), lambda b, h: (b, 0)) for _ in extras]
    return pl.pallas_call(
        body, name=name, grid=(s // STREAM_ROWS, n_heads), in_specs=in_specs,
        out_specs=[pl.BlockSpec((STREAM_ROWS, HEAD_DIM), lambda b, h: (b, h)) for _ in out_dtypes],
        out_shape=[jax.ShapeDtypeStruct((s, n_heads * HEAD_DIM), dt) for dt in out_dtypes],
        scratch_shapes=[pltpu.VMEM((STREAM_ROWS, HEAD_DIM), F32)], compiler_params=_cparams(("arbitrary", "arbitrary")),
    )(*ins, *extras)


def _to_streams(v, dil):
    if dil == 1:
        return v
    s, w = v.shape
    return v.reshape(s // dil, dil, w).transpose(1, 0, 2).reshape(s, w)


def _from_streams(v, dil):
    if dil == 1:
        return v
    s, w = v.shape
    return v.reshape(dil, s // dil, w).transpose(1, 0, 2).reshape(s, w)


SCAN_LANES = 1024
SSM_ROWS = 128
SSM_FWD_ROWS = 256
FF_TM = 1024
FF_TN = 2048


def _cmul(ar, ai, br, bi):
    return ar * br - ai * bi, ar * bi + ai * br


def _cmul_conj(ar, ai, br, bi):
    return ar * br + ai * bi, ar * bi - ai * br


def _ssm_fwd(u_src, bre, bim, abre, abim, cre, cim, consts, d_skip):
    u_arr, sw, ucb = u_src
    s = u_arr.shape[0]
    nblk = bre.shape[0]
    nst = nblk * 512
    t = _blk(s, SSM_FWD_ROWS)
    w = _blk(nst, SCAN_LANES)

    def body(u_ref, bre_ref, bim_ref, abre_ref, abim_ref, cre_ref, cim_ref, k_ref, ds_ref, y_ref, hr_ref, hi_ref, xr, xi, car_r, car_i):
        @pl.when(pl.program_id(0) == 0)
        def _():
            car_r[...] = jnp.zeros(car_r.shape, F32)
            car_i[...] = jnp.zeros(car_i.shape, F32)

        u = u_ref[...].astype(F32)
        ub = u_ref[...].astype(BF16)
        row = lax.broadcasted_iota(jnp.int32, u.shape, 0)
        u1b = jnp.where(row % SUBLANES == 0, 0.0, pltpu.roll(u, 1, 0)).astype(BF16)
        for b in range(nblk):
            ch, st = slice(b * 128, (b + 1) * 128), slice(b * 512, (b + 1) * 512)
            xr[:, st] = jnp.dot(ub[:, ch], bre_ref[b], preferred_element_type=F32) + jnp.dot(u1b[:, ch], abre_ref[b], preferred_element_type=F32)
            xi[:, st] = jnp.dot(ub[:, ch], bim_ref[b], preferred_element_type=F32) + jnp.dot(u1b[:, ch], abim_ref[b], preferred_element_type=F32)
        for c in range(nst // w):
            ls = slice(c * w, (c + 1) * w)
            mult = [k_ref[i, :, ls] for i in range(8)]

            def step(i, carry, ls=ls, mult=mult):
                cr, ci = carry
                r0 = pl.multiple_of(i * SUBLANES, SUBLANES)
                hr, hi = xr[pl.ds(r0, SUBLANES), ls], xi[pl.ds(r0, SUBLANES), ls]
                for n, sh in ((1, 2), (2, 4)):
                    pr, pi = _cmul(mult[2 * n], mult[2 * n + 1], pltpu.roll(hr, sh, 0), pltpu.roll(hi, sh, 0))
                    hr, hi = hr + pr, hi + pi
                pr, pi = _cmul(mult[6], mult[7], cr, ci)
                hr, hi = hr + pr, hi + pi
                xr[pl.ds(r0, SUBLANES), ls] = hr
                xi[pl.ds(r0, SUBLANES), ls] = hi
                return jnp.broadcast_to(hr[7:8, :], hr.shape), jnp.broadcast_to(hi[7:8, :], hi.shape)

            cr, ci = lax.fori_loop(0, t // SUBLANES, step, (car_r[:, ls], car_i[:, ls]))
            car_r[:, ls] = cr
            car_i[:, ls] = ci
        hrb, hib = xr[...].astype(BF16), xi[...].astype(BF16)
        hr_ref[...] = hrb
        hi_ref[...] = hib
        for b in range(nblk):
            ch, st = slice(b * 128, (b + 1) * 128), slice(b * 512, (b + 1) * 512)
            yb = jnp.dot(hrb[:, st], cre_ref[b], preferred_element_type=F32)
            yb -= jnp.dot(hib[:, st], cim_ref[b], preferred_element_type=F32)
            y_ref[:, ch] = yb + ds_ref[:, ch] * u[:, ch]

    whole = lambda a: pl.BlockSpec(a.shape, lambda i, nd=a.ndim: (0,) * nd)
    return pl.pallas_call(
        body, name="ssm_fwd", grid=(s // t,),
        in_specs=[pl.BlockSpec((t, sw), lambda i: (i, ucb))] + [whole(a) for a in (bre, bim, abre, abim, cre, cim, consts, d_skip)],
        out_specs=[pl.BlockSpec((t, sw), lambda i: (i, 0)), pl.BlockSpec((t, nst), lambda i: (i, 0)), pl.BlockSpec((t, nst), lambda i: (i, 0))],
        out_shape=[jax.ShapeDtypeStruct((s, sw), F32), jax.ShapeDtypeStruct((s, nst), BF16), jax.ShapeDtypeStruct((s, nst), BF16)],
        scratch_shapes=[pltpu.VMEM((t, nst), F32), pltpu.VMEM((t, nst), F32), pltpu.VMEM((SUBLANES, nst), F32), pltpu.VMEM((SUBLANES, nst), F32)],
        compiler_params=_cparams(("arbitrary",)),
    )(u_arr, bre, bim, abre, abim, cre, cim, consts, d_skip)


def _ssm_bwd(dy, u_src, hr, hi, ctre, ctim, acre, acim, btre, btim, consts, d_skip):
    u_arr, sw, ucb = u_src
    s = dy.shape[0]
    nblk = ctre.shape[0]
    nst = nblk * 512
    t = _blk(s, SSM_ROWS)
    w = _blk(nst, SCAN_LANES)
    nt = s // t

    def body(dy_ref, u_ref, hr_ref, hi_ref, ctre_ref, ctim_ref, acre_ref, acim_ref, btre_ref, btim_ref, k_ref, ds_ref,
             du_ref, db_ref, dc_ref, da_ref, dd_ref, lr, li, hfr, hfi, car_r, car_i):
        @pl.when(pl.program_id(0) == 0)
        def _():
            car_r[...] = jnp.zeros(car_r.shape, F32)
            car_i[...] = jnp.zeros(car_i.shape, F32)
            db_ref[...] = jnp.zeros(db_ref.shape, F32)
            dc_ref[...] = jnp.zeros(dc_ref.shape, F32)
            da_ref[...] = jnp.zeros(da_ref.shape, F32)
            dd_ref[...] = jnp.zeros(dd_ref.shape, F32)

        dyv, u = dy_ref[...].astype(F32), u_ref[...].astype(F32)
        dyb, ub = dyv.astype(BF16), u.astype(BF16)
        hrb, hib = hr_ref[...], hi_ref[...]
        hfr[...] = hrb.astype(F32)
        hfi[...] = hib.astype(F32)
        row = lax.broadcasted_iota(jnp.int32, dyv.shape, 0)
        dy1b = jnp.where(row % SUBLANES == SUBLANES - 1, 0.0, pltpu.roll(dyv, t - 1, 0)).astype(BF16)
        for b in range(nblk):
            ch, st = slice(b * 128, (b + 1) * 128), slice(b * 512, (b + 1) * 512)
            lr[:, st] = jnp.dot(dyb[:, ch], ctre_ref[b], preferred_element_type=F32) + jnp.dot(dy1b[:, ch], acre_ref[b], preferred_element_type=F32)
            li[:, st] = -(jnp.dot(dyb[:, ch], ctim_ref[b], preferred_element_type=F32) + jnp.dot(dy1b[:, ch], acim_ref[b], preferred_element_type=F32))
        last_row = lax.broadcasted_iota(jnp.int32, (SUBLANES, w), 0) == SUBLANES - 1
        for c in range(nst // w):
            ls = slice(c * w, (c + 1) * w)
            mult = [k_ref[i, :, ls] for i in range(8)]

            def step(i, carry, ls=ls, mult=mult):
                cr, ci, ar, ai = carry
                r0 = pl.multiple_of((t // SUBLANES - 1 - i) * SUBLANES, SUBLANES)
                gr, gi = lr[pl.ds(r0, SUBLANES), ls], li[pl.ds(r0, SUBLANES), ls]
                for n, sh in ((1, 2), (2, 4)):
                    pr, pi = _cmul_conj(mult[2 * n], mult[2 * n + 1], pltpu.roll(gr, SUBLANES - sh, 0), pltpu.roll(gi, SUBLANES - sh, 0))
                    gr, gi = gr + pr, gi + pi
                pr, pi = _cmul_conj(mult[6], mult[7], cr, ci)
                gr, gi = gr + pr, gi + pi
                lr[pl.ds(r0, SUBLANES), ls] = gr
                li[pl.ds(r0, SUBLANES), ls] = gi
                nr = jnp.where(last_row, cr, pltpu.roll(gr, SUBLANES - 1, 0))
                ni = jnp.where(last_row, ci, pltpu.roll(gi, SUBLANES - 1, 0))
                fr, fi = hfr[pl.ds(r0, SUBLANES), ls], hfi[pl.ds(r0, SUBLANES), ls]
                pr, pi = _cmul_conj(fr, fi, nr, ni)
                return (jnp.broadcast_to(gr[0:1, :], gr.shape), jnp.broadcast_to(gi[0:1, :], gi.shape), ar + pr, ai + pi)

            zero = jnp.zeros((SUBLANES, w), F32)
            cr, ci, ar, ai = lax.fori_loop(0, t // SUBLANES, step, (car_r[:, ls], car_i[:, ls], zero, zero))
            car_r[:, ls] = cr
            car_i[:, ls] = ci
            da_ref[0, :, ls] += ar
            da_ref[1, :, ls] += ai
        lrb, lib = lr[...].astype(BF16), li[...].astype(BF16)
        for b in range(nblk):
            ch, st = slice(b * 128, (b + 1) * 128), slice(b * 512, (b + 1) * 512)
            dub = jnp.dot(lrb[:, st], btre_ref[b], preferred_element_type=F32)
            dub += jnp.dot(lib[:, st], btim_ref[b], preferred_element_type=F32)
            du_ref[:, ch] = (dub + ds_ref[:, ch] * dyv[:, ch]).astype(du_ref.dtype)
            db_ref[0, b] += lax.dot_general(ub[:, ch], lrb[:, st], _TN, preferred_element_type=F32)
            db_ref[1, b] += lax.dot_general(ub[:, ch], lib[:, st], _TN, preferred_element_type=F32)
            dc_ref[0, b] += lax.dot_general(dyb[:, ch], hrb[:, st], _TN, preferred_element_type=F32)
            dc_ref[1, b] -= lax.dot_general(dyb[:, ch], hib[:, st], _TN, preferred_element_type=F32)
        dd_ref[...] += _colsum8(dyv * u)

    whole = lambda a: pl.BlockSpec(a.shape, lambda i, nd=a.ndim: (0,) * nd)
    rev = lambda wd, cb=0: pl.BlockSpec((t, wd), lambda i, cb=cb: (nt - 1 - i, cb))
    acc = lambda sh: pl.BlockSpec(sh, lambda i, nd=len(sh): (0,) * nd)
    db_shape, da_shape, dd_shape = (2, nblk, 128, 512), (2, SUBLANES, nst), (SUBLANES, sw)
    return pl.pallas_call(
        body, name="ssm_bwd", grid=(nt,),
        in_specs=[rev(sw), rev(sw, ucb), rev(nst), rev(nst)] + [whole(a) for a in (ctre, ctim, acre, acim, btre, btim, consts, d_skip)],
        out_specs=[rev(sw), acc(db_shape), acc(db_shape), acc(da_shape), acc(dd_shape)],
        out_shape=[jax.ShapeDtypeStruct((s, sw), BF16), jax.ShapeDtypeStruct(db_shape, F32), jax.ShapeDtypeStruct(db_shape, F32),
                   jax.ShapeDtypeStruct(da_shape, F32), jax.ShapeDtypeStruct(dd_shape, F32)],
        scratch_shapes=[pltpu.VMEM((t, nst), F32)] * 4 + [pltpu.VMEM((SUBLANES, nst), F32)] * 2,
        compiler_params=_cparams(("arbitrary",)),
    )(dy, u_arr, hr, hi, ctre, ctim, acre, acim, btre, btim, consts, d_skip)


def _ssm_discretise(lam_re, lam_im, log_step):
    step = jnp.exp(log_step)
    e = jnp.exp(lam_re * step)
    ar, ai = e * jnp.cos(lam_im * step), e * jnp.sin(lam_im * step)
    den = lam_re * lam_re + lam_im * lam_im
    inv_r, inv_i = lam_re / den, -lam_im / den
    fr, fi = _cmul(ar - 1.0, ai, inv_r, inv_i)
    return step, ar, ai, inv_r, inv_i, fr, fi


def _ssm_prep_fn(lam_re, lam_im, log_step):
    _, ar, ai, _, _, fr, fi = _ssm_discretise(lam_re, lam_im, log_step)
    pw = [(ar, ai)]
    for n in range(1, 8):
        pw.append(_cmul(*pw[n - 1], ar, ai))
    return [p[0] for p in pw] + [p[1] for p in pw] + [fr, fi]


def _ssm_param_grad_fn(lam_re, lam_im, log_step, da_r, da_i, df_r, df_i):
    step, ar, ai, inv_r, inv_i, fr, fi = _ssm_discretise(lam_re, lam_im, log_step)
    pr, pi = _cmul_conj(inv_r, inv_i, df_r, df_i)
    dat_r, dat_i = da_r + pr, da_i + pi
    wr, wi = _cmul(fr, fi, inv_r, inv_i)
    dl_r, dl_i = _cmul_conj(-wr, -wi, df_r, df_i)
    dz_r, dz_i = _cmul_conj(ar, ai, dat_r, dat_i)
    dl_r, dl_i = dl_r + step * dz_r, dl_i + step * dz_i
    dstep = jnp.sum(dz_r * lam_re + dz_i * lam_im, axis=-1, keepdims=True)
    return [dl_r, dl_i, step * dstep]


def _block_diag(v, rows_first):
    g, a, b = v.shape
    nb = g // GROUPS_PER_BLOCK
    eye = jnp.eye(GROUPS_PER_BLOCK, dtype=v.dtype)
    v = v.reshape(nb, GROUPS_PER_BLOCK, a, 1, b) * eye[None, :, None, :, None]
    return v.reshape(nb, GROUPS_PER_BLOCK * a, GROUPS_PER_BLOCK * b)


def _block_diag_take(v, a, b):
    nb = v.shape[0]
    v = v.reshape(nb, GROUPS_PER_BLOCK, a, GROUPS_PER_BLOCK, b)
    return jnp.stack([v[:, g, :, g, :] for g in range(GROUPS_PER_BLOCK)], axis=1).reshape(nb * GROUPS_PER_BLOCK, a, b)


def _adamw(w, g, m, v):
    m = ADAM_B1 * m + (1.0 - ADAM_B1) * g
    v = ADAM_B2 * v + (1.0 - ADAM_B2) * (g * g)
    m_hat = m / (1.0 - ADAM_B1**ADAM_STEP)
    v_hat = v / (1.0 - ADAM_B2**ADAM_STEP)
    delta = -ADAM_LR * (m_hat / (jnp.sqrt(v_hat) + ADAM_EPS) + ADAM_WD * w)
    return delta, m, v


def _adamw_rows(name, w, g, m, v):
    wd = w.shape[1]
    return _rowwise(name, lambda a, b, c, d: (list(_adamw(a, b, c, d)), []), [w, g, m, v], [], [(wd, F32)] * 3, [], tm=128)


SMALL = ["b_ada", "norm1_g", "q_norm_g", "k_norm_g", "lam_re", "lam_im", "log_step", "b_re", "b_im", "c_re", "c_im",
         "d_skip", "b_glu", "attn_out_g", "ssm_out_g", "norm2_g"]
LARGE = ["w_in", "w_glu", "w_out", "w_ff1", "w_ff2"]
ORDER = ["w_ada", "b_ada", "norm1_g", "w_in", "q_norm_g", "k_norm_g", "lam_re", "lam_im", "log_step", "b_re", "b_im", "c_re",
         "c_im", "d_skip", "w_glu", "b_glu", "attn_out_g", "ssm_out_g", "w_out", "norm2_g", "w_ff1", "w_ff2"]


def _pack(arrs):
    flat = jnp.concatenate([a.reshape(-1) for a in arrs])
    rows = -(-flat.shape[0] // PACK_LANES)
    rows = -(-rows // SUBLANES) * SUBLANES
    return jnp.pad(flat, (0, rows * PACK_LANES - flat.shape[0])).reshape(rows, PACK_LANES)


def _unpack(packed, shapes):
    flat, out, o = packed.reshape(-1), [], 0
    for sh in shapes:
        n = int(np.prod(sh))
        out.append(flat[o : o + n].reshape(sh))
        o += n
    return out


def kernel(x, c, w_ada, b_ada, norm1_g, w_in, q_norm_g, k_norm_g, lam_re, lam_im, log_step, b_re, b_im, c_re, c_im, d_skip, w_glu, b_glu, attn_out_g, ssm_out_g, w_out, norm2_g, w_ff1, w_ff2, loss_target, m_w_ada, m_b_ada, m_norm1_g, m_w_in, m_q_norm_g, m_k_norm_g, m_lam_re, m_lam_im, m_log_step, m_b_re, m_b_im, m_c_re, m_c_im, m_d_skip, m_w_glu, m_b_glu, m_attn_out_g, m_ssm_out_g, m_w_out, m_norm2_g, m_w_ff1, m_w_ff2, v_w_ada, v_b_ada, v_norm1_g, v_w_in, v_q_norm_g, v_k_norm_g, v_lam_re, v_lam_im, v_log_step, v_b_re, v_b_im, v_c_re, v_c_im, v_d_skip, v_w_glu, v_b_glu, v_attn_out_g, v_ssm_out_g, v_w_out, v_norm2_g, v_w_ff1, v_w_ff2):
    args = dict(locals())
    wts = {n: args[n] for n in ORDER}
    mom = {n: args["m_" + n] for n in ORDER}
    var = {n: args["v_" + n] for n in ORDER}

    x2, tgt = x[0], loss_target[0]
    s, d = x2.shape
    aw = d // 2
    sw = d - aw
    n_groups = sw // SSM_GROUP
    nst = n_groups * STATE_DIM
    chip = 2 * lax.axis_index("x") + lax.axis_index("y")

    chip_idx = jnp.reshape(chip, (1,)).astype(jnp.int32)
    core_idx = jnp.reshape(lax.axis_index("c"), (1,)).astype(jnp.int32)

    def cast_into_stack(n):
        w2 = wts[n][0]
        rows, cols = w2.shape
        tr = _blk(rows, 256)
        return _indexed("cast_" + n, lambda a: a, chip_idx, [(w2, (tr, cols), lambda r, ix: (r, 0))],
                        (((4, rows, cols), BF16), (None, tr, cols), lambda r, ix: (ix[0], r, 0)), (rows // tr,))

    stacks = {n: cast_into_stack(n) for n in LARGE}
    dff = 4 * wts["w_ff2"].shape[1]

    c_all = _allgather8("gather_c", jnp.pad(c, ((0, SUBLANES - 1), (0, 0))))[::SUBLANES]
    n_mod = w_ada.shape[2]
    b_ada_mine = lax.dynamic_slice(b_ada, (0, chip * n_mod), (1, n_mod))
    sil_c = _single("silu_c", lambda v: [v * _sigmoid(v)], [c_all], [(c_all.shape, F32)])[0]
    mod_part = _mm("ada_fwd", sil_c, w_ada[0], "nn", F32, tn=512, epi=lambda acc, b: acc + b[0:1, :],
                   extras=[jnp.broadcast_to(b_ada_mine, (SUBLANES, n_mod))])
    mod_all = _allgather8("gather_mod", mod_part)
    me = 2 * chip + lax.axis_index("c")
    mod = jnp.concatenate([lax.dynamic_slice(mod_all, (16 * k + me, 0), (1, n_mod)) for k in range(4)], axis=1)
    sh1, sc1, g1, sh2, sc2, g2 = [mod[:, i * d : (i + 1) * d] for i in range(6)]

    (w_in_s,) = _gather_weights([stacks["w_in"]], mod_all)

    def gather_plan(refs):
        px0, py0, pc, chips = _place()
        slot = 2 * px0 + py0
        return [(r.at[slot], r.at[slot], (px, py, pc), r.at[2 * px + py]) for r in refs for (px, py) in chips]

    mix_sems_s, mix_sems_r, mix_stacks, mix_token = _split_start(
        "gather_mix_start", [stacks["w_glu"], stacks["w_out"]], gather_plan, 6, extra=[w_in_s])
    w_in_s = mix_stacks.pop()
    ff_sems_s, ff_sems_r, ff_stacks, ff_token = _split_start(
        "gather_ff_start", [stacks["w_ff1"], stacks["w_ff2"]], gather_plan, 6, extra=[w_in_s])
    w_in_s = ff_stacks.pop()
    sc1 = sc1 + (mix_token[0:1, 0:1] + ff_token[0:1, 0:1])

    def norm_mod(v, g, sc, sh):
        return [v * _rstd(v) * g * (1.0 + sc) + sh], []

    (h1b,) = _rowwise("norm1", norm_mod, [x2], [norm1_g, sc1, sh1], [(d, BF16)], [])
    proj = _mm("proj_in", h1b, w_in_s, "nn", BF16, b_cs=True, tm=FF_TM)
    assert aw == sw
    q_src, k_src, v_src, u_src = [(proj, aw, i) for i in range(3)] + [(proj, sw, 3)]

    n_heads = aw // HEAD_DIM
    up = lambda v: v.astype(F32)
    qk_orders = _stream_orders("qk_prep", lambda q, k, v, gq, gk: [up(q) * _rstd(up(q)) * gq, up(k) * _rstd(up(k)) * gk, up(v)],
                               [(proj, 0), (proj, n_heads), (proj, 2 * n_heads)], [q_norm_g, k_norm_g], 3, n_heads)

    pats = []
    for pi, (window, dil) in enumerate(DILATION_PATTERNS):
        assert window // dil == BAND and s % STREAM_ROWS == 0
        bps = s // dil // BAND
        qp, kp, vp = [order[pi] for order in qk_orders]
        o_p, l_p = _attn_fwd(f"attn_fwd_d{dil}", qp, kp, vp, bps, dil)
        pats.append((dil, bps, qp, kp, vp, o_p, _from_streams(l_p, dil)))

    def lse_total(l1, l2, l3):
        m = jnp.maximum(jnp.maximum(l1, l2), l3)
        return m + jnp.log(jnp.exp(l1 - m) + jnp.exp(l2 - m) + jnp.exp(l3 - m))

    (lse,) = _rowwise("attn_lse", lambda l1, l2, l3: ([lse_total(l1, l2, l3)], []), [p[6] for p in pats], [], [(HEAD_DIM, F32)], [])

    def combine(h, blocks, l1, l2, l3):
        tot = lse_total(l1, l2, l3)
        lane = lax.broadcasted_iota(jnp.int32, l1.shape, 1)
        mix = 0.0
        for o_blk, l in zip(blocks[0], (l1, l2, l3)):
            w = jnp.sum(jnp.where(lane == h, jnp.exp(l - tot), 0.0), axis=1, keepdims=True)
            mix = mix + w * o_blk
        return [mix]

    (attn,) = _natural_order("attn_mix", combine, [[p[5] for p in pats]], [p[6] for p in pats], [F32], n_heads)

    lam_re2, lam_im2, log_step2 = lam_re[0], lam_im[0], log_step[0].reshape(n_groups, 1)
    prep = _single("ssm_prep", _ssm_prep_fn, [lam_re2, lam_im2, log_step2], [((n_groups, STATE_DIM), F32)] * 18)
    p_r = jnp.stack([p.reshape(nst) for p in prep[:8]])
    p_i = jnp.stack([p.reshape(nst) for p in prep[8:16]])
    f_ri = jnp.stack(prep[16:])
    row = jnp.arange(SUBLANES)[:, None]
    zero = jnp.zeros((SUBLANES, nst), F32)
    sel = lambda cond, v: jnp.where(cond, jnp.broadcast_to(v[None, :], (SUBLANES, nst)), zero)
    fwd_consts = jnp.stack([sel(row >= 1, p_r[0]), sel(row >= 1, p_i[0]), sel(row >= 2, p_r[1]), sel(row >= 2, p_i[1]),
                            sel(row >= 4, p_r[3]), sel(row >= 4, p_i[3]), p_r, p_i])
    bwd_consts = jnp.stack([sel(row <= 6, p_r[0]), sel(row <= 6, p_i[0]), sel(row <= 5, p_r[1]), sel(row <= 5, p_i[1]),
                            sel(row <= 3, p_r[3]), sel(row <= 3, p_i[3]), p_r[::-1], p_i[::-1]])
    nblk = n_groups // GROUPS_PER_BLOCK
    f_rows = f_ri.reshape(2, nblk, 1, 512)
    braw_r = _block_diag(b_re[0].transpose(0, 2, 1), True)
    braw_i = _block_diag(b_im[0].transpose(0, 2, 1), True)

    craw_r, craw_i = _block_diag(c_re[0], True), _block_diag(c_im[0], True)
    a_rows = jnp.stack([p_r[0], p_i[0]]).reshape(2, nblk, 1, 512)

    def bbar_fn(br, bi, f, cr, ci, a):
        r, i = _cmul(f[0], f[1], br, bi)
        ar, ai = _cmul(a[0], a[1], r, i)
        acr, aci = _cmul(a[0], a[1], cr, ci)
        return [r, i, ar, ai, acr, aci]

    bbar_r, bbar_i, abbar_r, abbar_i, ac_r, ac_i = _single(
        "ssm_bbar", bbar_fn, [braw_r, braw_i, f_rows, craw_r, craw_i, a_rows], [(braw_r.shape, BF16)] * 6)
    ct_r, ct_i = craw_r.astype(BF16), craw_i.astype(BF16)
    ds2 = d_skip
    y_ssm, h_r, h_i = _ssm_fwd(u_src, bbar_r, bbar_i, abbar_r, abbar_i, ct_r.transpose(0, 2, 1), ct_i.transpose(0, 2, 1), fwd_consts, ds2)

    (yb,) = _rowwise("gelu", lambda v: ([_gelu(v)], []), [y_ssm], [], [(sw, BF16)], [])
    w_glu_s, w_out_s = _split_wait("gather_mix_wait", mix_stacks, mix_sems_s, mix_sems_r, yb, gather_plan)
    w_glu_f = w_glu_s.reshape(sw, sw)
    w_out_f = w_out_s.reshape(d, d)
    z = _mm("glu_fwd", yb, w_glu_f, "nn", BF16)

    def glu(ys, zz, b):
        return [_gelu(ys) * _sigmoid(zz.astype(F32) + b)], []

    (ssm,) = _rowwise("glu", glu, [y_ssm, z], [b_glu], [(sw, F32)], [])

    def cat_norm(a, sm, ga, gs):
        return [jnp.concatenate([a * _rstd(a) * ga, sm * _rstd(sm) * gs], axis=1)], []

    (catb,) = _rowwise("cat_norm", cat_norm, [attn, ssm], [attn_out_g, ssm_out_g], [(d, BF16)], [])
    mixed = _mm("mix_out", catb, w_out_f, "nn", BF16, tm=FF_TM)

    def resid_norm(xv, mx, g1v, g, sc, sh):
        x1v = xv + g1v * mx.astype(F32)
        return [x1v, x1v * _rstd(x1v) * g * (1.0 + sc) + sh], []

    x1, h2b = _rowwise("resid_norm2", resid_norm, [x2, mixed], [g1, norm2_g, sc2, sh2], [(d, F32), (d, BF16)], [])
    w_ff1_s, w_ff2_s = _split_wait("gather_ff_wait", ff_stacks, ff_sems_s, ff_sems_r, x1, gather_plan)
    w_ff2_f = w_ff2_s.reshape(dff, d)
    a_ff = _mm("ff1", h2b, w_ff1_s, "nn", BF16, b_cs=True, tm=FF_TM, tn=FF_TN)
    sq_relu = lambda v: jnp.square(jnp.maximum(v.astype(F32), 0.0)).astype(BF16)
    ff = _mm("ff2", a_ff, w_ff2_f, "nn", BF16, a_pro=sq_relu, tm=FF_TM)

    def loss_fn(x1v, ffv, tg, g2v):
        ffv = ffv.astype(F32)
        diff = x1v + g2v * ffv - tg
        dout = diff * (1.0 / d)
        return [dout, dout * g2v], [_colsum8(0.5 * diff * dout), _colsum8(dout * ffv)]

    dout, dffb, loss_acc, dg2_acc = _rowwise("loss", loss_fn, [x1, ff, tgt], [g2], [(d, F32), (d, BF16)], [(SUBLANES, d)] * 2)

    da_ff = _mm("ff2_bwd", dffb, w_ff2_f, "nt", BF16, epi=lambda acc, av: acc * (2.0 * jnp.maximum(av.astype(F32), 0.0)), extras=[a_ff], tm=FF_TM, tn=FF_TN)
    gw = {}
    gw["w_ff2"] = _mm("ff2_wgrad", a_ff, dffb, "tn", BF16, a_pro=sq_relu, tm=FF_TM).reshape(4, dff // 4, d)
    gw["w_ff1"] = _mm("ff1_wgrad", h2b, da_ff, "tn", BF16, o_cs=True, tm=FF_TM)

    def add2(n, g, b):
        _, half, cols = b.shape
        tr = _blk(half, 256)
        nrb = half // tr
        blk = (None, tr, cols)
        return _indexed("grad_add2_" + n, lambda p, q: p.astype(F32) + q.astype(F32), core_idx,
                        [(g, blk, lambda k, r, ix: (k, ix[0] * nrb + r, 0)), (b, blk, lambda k, r, ix: (k, r, 0))],
                        ((b.shape, BF16), blk, lambda k, r, ix: (k, r, 0)), (4, nrb))

    def scatter_plan(refs):
        px0, py0, pc, chips = _place()
        nw = len(refs) // 2
        return [(refs[w].at[2 * px + py], refs[nw + w].at[j], (px, py, pc), refs[nw + w].at[j])
                for w in range(nw) for j, (px, py) in enumerate(chips)]

    def swap_plan(refs):
        px0, py0, pc, _ = _place()
        nw = len(refs) // 2
        plan = []
        for w in range(nw):
            half = refs[w].shape[1] // 2
            plan.append((refs[w].at[:, pl.ds((1 - pc) * half, half), :], refs[nw + w], (px0, py0, 1 - pc), refs[nw + w]))
        return plan

    early = ["w_ff2", "w_ff1"]
    sw_lands = [lax.empty((4, gw[n].shape[1] // 2, gw[n].shape[2]), BF16) for n in early]
    sw_sems_s, sw_sems_r, sw_bufs, sw_token = _split_start("swap_ff_start", [gw[n] for n in early] + sw_lands, swap_plan, len(early))
    sc2_late = sc2 + sw_token[0:1, 0:1]

    dh2 = _mm("ff1_bwd", da_ff, w_ff1_s, "nt", BF16, b_cs=True, tm=FF_TM)

    def norm2_bwd(dh, x1v, do, mx, g, sc, g1v):
        dh, mx = dh.astype(F32), mx.astype(F32)
        r = _rstd(x1v)
        xh = x1v * r
        dn = dh * (1.0 + sc)
        dx = do + _rms_bwd(dn * g, xh, r)
        return [dx, dx * g1v], [_colsum8(dh * xh * g), _colsum8(dh), _colsum8(dn * xh), _colsum8(dx * mx)]

    dx1, dmixb, dsc2_acc, dsh2_acc, dn2g_acc, dg1_acc = _rowwise(
        "norm2_bwd", norm2_bwd, [dh2, x1, dout, mixed], [norm2_g, sc2_late, g1], [(d, F32), (d, BF16)], [(SUBLANES, d)] * 4)

    sw_bufs = _split_wait("swap_ff_wait", sw_bufs, sw_sems_s, sw_sems_r, dmixb, swap_plan)
    parts = {n: add2(n, g, b) for n, g, b in zip(early, sw_bufs[: len(early)], sw_bufs[len(early) :])}
    lands = [lax.empty((3,) + parts[n].shape[1:], BF16) for n in early]
    sc_sems_s, sc_sems_r, sc_bufs, sc_token = _split_start("scatter_ff_start", [parts[n] for n in early] + lands, scatter_plan, 6)
    attn_out_g_late = attn_out_g + sc_token[0:1, 0:1]

    dcat = _mm("mix_out_bwd", dmixb, w_out_f, "nt", BF16, tm=FF_TM)
    gw["w_out"] = _mm("mix_out_wgrad", catb, dmixb, "tn", BF16, tm=FF_TM).reshape(4, d // 4, d)

    def cat_bwd(da, dsm, a, sm, ga, gs):
        da, dsm = da.astype(F32), dsm.astype(F32)
        ra, rs = _rstd(a), _rstd(sm)
        ah, sh = a * ra, sm * rs
        return [_rms_bwd(da * ga, ah, ra), _rms_bwd(dsm * gs, sh, rs)], [_colsum8(da * ah), _colsum8(dsm * sh)]

    dattn, dssm, dga_acc, dgs_acc = _rowwise(
        "cat_norm_bwd", cat_bwd, [(dcat, aw, 0), (dcat, sw, 1), attn, ssm], [attn_out_g_late, ssm_out_g], [(aw, F32), (sw, F32)],
        [(SUBLANES, aw), (SUBLANES, sw)])

    def glu_bwd(dsm, ys, zz, b):
        sg = _sigmoid(zz.astype(F32) + b)
        dz = dsm * _gelu(ys) * sg * (1.0 - sg)
        return [dz, dsm * sg], [_colsum8(dz)]

    dzb, dy_direct, dbglu_acc = _rowwise("glu_bwd", glu_bwd, [dssm, y_ssm, z], [b_glu], [(sw, BF16), (sw, BF16)], [(SUBLANES, sw)])
    gw["w_glu"] = _mm("glu_wgrad", yb, dzb, "tn", BF16).reshape(4, sw // 4, sw)
    dys = _mm("glu_bwd_mm", dzb, w_glu_f, "nt", BF16, epi=lambda acc, direct, ys: (acc + direct.astype(F32)) * _gelu_grad(ys), extras=[dy_direct, y_ssm])
    du, db_bar, dc_bd, da_acc, dds_acc = _ssm_bwd(dys, u_src, h_r, h_i, ct_r, ct_i, ac_r, ac_i, bbar_r.transpose(0, 2, 1),
                                                  bbar_i.transpose(0, 2, 1), bwd_consts, ds2)

    def delta_fn(da, a):
        return [_to_lanes([jnp.sum(p * q, axis=1, keepdims=True) for p, q in zip(_heads(da), _heads(a))])], []

    (delta,) = _rowwise("attn_delta", delta_fn, [dattn, attn], [], [(HEAD_DIM, F32)], [])
    (do_orders,) = _stream_orders("attn_do", lambda da: [da], [(dattn, 0)], [], 1, n_heads)
    dqs, dks, dvs = [], [], []
    for pi, (dil, bps, qp, kp, vp, _, _) in enumerate(pats):
        lsp, dlp = [_to_streams(t, dil) for t in (lse, delta)]
        dq_p, dk_p, dv_p = _attn_bwd(f"attn_bwd_d{dil}", qp, kp, vp, do_orders[pi], lsp, dlp, bps, dil)
        dqs.append(dq_p)
        dks.append(dk_p)
        dvs.append(dv_p)
    dq_t, dk_t, dv_t = _natural_order("attn_grad_sum", lambda h, blocks: [(b[0] + b[1]) + b[2] for b in blocks],
                                      [dqs, dks, dvs], [], [BF16] * 3, n_heads)

    def qk_bwd(dq_sum, dk_sum, dv_sum, duv, q, k, gq, gk):
        def one(dn, raw, g):
            outs, acc = [], 0.0
            for dh, h in zip(_heads(dn), _heads(raw)):
                r = _rstd(h)
                hh = h * r
                outs.append(_rms_bwd(dh * g, hh, r))
                acc = acc + _colsum8(dh * hh)
            return jnp.concatenate(outs, axis=1), acc

        dq, gq_acc = one(dq_sum.astype(F32), q.astype(F32), gq)
        dk, gk_acc = one(dk_sum.astype(F32), k.astype(F32), gk)
        return [jnp.concatenate([dq, dk, dv_sum.astype(F32), duv.astype(F32)], axis=1)], [gq_acc, gk_acc]

    dprojb, dgq_acc, dgk_acc = _rowwise("qk_norm_bwd", qk_bwd, [dq_t, dk_t, dv_t, du, q_src, k_src], [q_norm_g, k_norm_g],
                                        [(3 * aw + sw, BF16)], [(SUBLANES, HEAD_DIM)] * 2)
    gw["w_in"] = _mm("proj_in_wgrad", h1b, dprojb, "tn", BF16, o_cs=True, tm=FF_TM)
    late = [n for n in LARGE if n not in early]
    parts.update({n: add2(n, gw[n], b) for n, b in zip(late, _swap_halves([gw[n] for n in late], "swap_halves_rest"))})
    lands_late = [lax.empty((3,) + parts[n].shape[1:], BF16) for n in late]
    rest_sems_s, rest_sems_r, rest_bufs, rest_token = _split_start(
        "scatter_rest_start", [parts[n] for n in late] + lands_late, scatter_plan, 3 * len(late))
    sc1_late = sc1 + rest_token[0:1, 0:1]
    dh1 = _mm("proj_in_bwd", dprojb, w_in_s, "nt", BF16, b_cs=True, tm=FF_TM)

    def norm1_bwd(dh, xv, dxr, g, sc):
        dh = dh.astype(F32)
        r = _rstd(xv)
        xh = xv * r
        dn = dh * (1.0 + sc)
        return [dxr + _rms_bwd(dn * g, xh, r)], [_colsum8(dh * xh * g), _colsum8(dh), _colsum8(dn * xh)]

    grad_x, dsc1_acc, dsh1_acc, dn1g_acc = _rowwise("norm1_bwd", norm1_bwd, [dh1, x2, dx1], [norm1_g, sc1_late], [(d, F32)], [(SUBLANES, d)] * 3)

    def b_grad_fn(dbb, br, bi, f):
        fr, fi = f[0], f[1]
        dbr, dbi = _cmul_conj(fr, fi, dbb[0], dbb[1])
        dfr = jnp.sum(dbb[0] * br + dbb[1] * bi, axis=1, keepdims=True)
        dfi = jnp.sum(dbb[1] * br - dbb[0] * bi, axis=1, keepdims=True)
        return [dbr, dbi, dfr, dfi]

    dbr_bd, dbi_bd, df_r, df_i = _single("ssm_b_grad", b_grad_fn, [db_bar, braw_r, braw_i, f_rows],
                                         [(braw_r.shape, F32)] * 2 + [((nblk, 1, 512), F32)] * 2)

    def sums_fn(*accs):
        return [jnp.sum(a, axis=-2, keepdims=True) for a in accs]

    acc_list = [dsh1_acc, dsc1_acc, dg1_acc, dsh2_acc, dsc2_acc, dg2_acc, dn1g_acc, dgq_acc, dgk_acc, dds_acc, dbglu_acc, dga_acc,
                dgs_acc, dn2g_acc, loss_acc, da_acc]
    sums = _single("small_sums", sums_fn, acc_list, [(a.shape[:-2] + (1, a.shape[-1]), F32) for a in acc_list])
    (dsh1, dsc1, dg1, dsh2, dsc2, dg2, dn1g, dgq, dgk, dds, dbglu, dga, dgs, dn2g, loss_row, da_sum) = sums
    loss_dev = _single("loss_sum", lambda v: [jnp.sum(v, axis=1, keepdims=True)], [loss_row], [((1, 1), F32)])[0]
    gp = (n_groups, STATE_DIM)
    dlam_re, dlam_im, dlog_step = _single(
        "ssm_param_grad", _ssm_param_grad_fn,
        [lam_re2, lam_im2, log_step2, da_sum[0].reshape(gp), da_sum[1].reshape(gp), df_r.reshape(gp), df_i.reshape(gp)],
        [(gp, F32), (gp, F32), ((n_groups, 1), F32)])
    small_dev = {
        "b_ada": jnp.concatenate([dsh1, dsc1, dg1, dsh2, dsc2, dg2], axis=1),
        "norm1_g": dn1g, "q_norm_g": dgq, "k_norm_g": dgk, "lam_re": dlam_re, "lam_im": dlam_im, "log_step": dlog_step,
        "b_re": _block_diag_take(dbr_bd, SSM_GROUP, STATE_DIM).transpose(0, 2, 1),
        "b_im": _block_diag_take(dbi_bd, SSM_GROUP, STATE_DIM).transpose(0, 2, 1),
        "c_re": _block_diag_take(dc_bd[0], SSM_GROUP, STATE_DIM), "c_im": _block_diag_take(dc_bd[1], SSM_GROUP, STATE_DIM),
        "d_skip": dds, "b_glu": dbglu, "attn_out_g": dga, "ssm_out_g": dgs, "norm2_g": dn2g,
    }

    packed = _pack([small_dev[n] for n in SMALL])
    rows = packed.shape[0]
    tr = _blk(rows, 256)
    me_idx = jnp.reshape(me, (1,)).astype(jnp.int32)
    gathered = _indexed("pack_small", lambda a: a, me_idx, [(packed, (tr, PACK_LANES), lambda r, ix: (r, 0))],
                        (((8, rows, PACK_LANES), F32), (None, tr, PACK_LANES), lambda r, ix: (ix[0], r, 0)), (rows // tr,))

    def small_plan(refs):
        (g,) = refs
        px0, py0, pc0, _ = _place()
        flip = lambda v, bit: 1 - v if bit else v
        plan = []
        for mask in range(1, 8):
            px, py, pc = flip(px0, mask & 4), flip(py0, mask & 2), flip(pc0, mask & 1)
            mine = g.at[4 * px0 + 2 * py0 + pc0]
            plan.append((mine, mine, (px, py, pc), g.at[4 * px + 2 * py + pc]))
        return plan

    small_sems_s, small_sems_r, (gathered,), small_token = _split_start("gather_small_start", [gathered], small_plan, 7)

    sc_bufs = _split_wait("scatter_ff_wait", sc_bufs, sc_sems_s, sc_sems_r, small_token, scatter_plan)
    parts.update(zip(early, sc_bufs[: len(early)]))
    got2 = dict(zip(early, sc_bufs[len(early) :]))
    rest_bufs = _split_wait("scatter_rest_wait", rest_bufs, rest_sems_s, rest_sems_r, small_token, scatter_plan)
    parts.update(zip(late, rest_bufs[: len(late)]))
    got2.update(zip(late, rest_bufs[len(late) :]))
    place_idx = jnp.stack([chip, lax.axis_index("c")]).astype(jnp.int32)

    def add4(n, p, b):
        _, half, cols = p.shape
        tr = _blk(half, 256)
        blk = (None, tr, cols)
        fn = lambda own, q0, q1, q2: ((own.astype(F32) + q0.astype(F32)) + q1.astype(F32)) + q2.astype(F32)
        srcs = [(p, blk, lambda r, ix: (ix[0], r, 0))] + [(b, blk, lambda r, ix, j=j: (j, r, 0)) for j in range(3)]
        return _indexed("grad_add4_" + n, fn, place_idx, srcs, (((2, half, cols), F32), blk, lambda r, ix: (ix[1], r, 0)), (half // tr,))

    def join_plan(refs):
        px0, py0, pc, _ = _place()
        return [(r.at[pc], r.at[pc], (px0, py0, 1 - pc), r.at[1 - pc]) for r in refs]

    out = {}

    def update_large(names, fulls):
        last = None
        for n, full in zip(names, fulls):
            g = full.reshape(wts[n].shape[1:])
            last = _adamw_rows("adamw_" + n, wts[n][0], g, mom[n][0], var[n][0])
            for kind, val in zip(("grad", "delta", "new_m", "new_v"), (g,) + tuple(last)):
                out[kind, n] = val[None]
        return last[0]

    rest = [n for n in LARGE if n not in early]
    ja_s, ja_r, fulls_a, _ = _split_start("join_ff_start", [add4(n, parts[n], got2[n]) for n in early], join_plan, len(early))
    jb_s, jb_r, fulls_b, jb_token = _split_start("join_rest_start", [add4(n, parts[n], got2[n]) for n in rest], join_plan, len(rest))
    done_a = update_large(early, _split_wait("join_ff_wait", fulls_a, ja_s, ja_r, jb_token, join_plan))
    done_b = update_large(rest, _split_wait("join_rest_wait", fulls_b, jb_s, jb_r, done_a, join_plan))
    (gathered,) = _split_wait("gather_small_wait", [gathered], small_sems_s, small_sems_r, done_b, small_plan)

    def sum8_fn(g):
        acc = g[0]
        for i in range(1, 8):
            acc = acc + g[i]
        return [acc]

    g_small = _single("sum_small", sum8_fn, [gathered], [((rows, PACK_LANES), F32)])[0]

    w_small, m_small, v_small = [_pack([src[n] for n in SMALL]) for src in (wts, mom, var)]
    upd_small = _adamw_rows("adamw_small", w_small, g_small, m_small, v_small)
    shapes = [wts[n].shape for n in SMALL]
    for kind, arr in zip(("grad", "delta", "new_m", "new_v"), (g_small,) + tuple(upd_small)):
        for n, val in zip(SMALL, _unpack(arr, shapes)):
            out[kind, n] = val

    dmod_all = gathered[:, 0 : -(-6 * d // PACK_LANES), :].reshape(8, -1)[:, : 6 * d]
    dmod_mine = lax.dynamic_slice(dmod_all, (0, chip * n_mod), (8, n_mod))
    g_ada = _mm("ada_wgrad", sil_c, dmod_mine, "tn", F32, tm=512, tn=512)
    upd = _adamw_rows("adamw_w_ada", w_ada[0], g_ada, m_w_ada[0], v_w_ada[0])
    for kind, val in zip(("grad", "delta", "new_m", "new_v"), (g_ada,) + tuple(upd)):
        out[kind, "w_ada"] = val[None]

    loss = lax.psum(loss_dev[0, 0], ("x", "y", "c"))
    return (loss, grad_x[None], *[out[kind, n] for kind in ("grad", "delta", "new_m", "new_v") for n in ORDER])
```

```python
import functools
import math

import jax
import jax.numpy as jnp
import numpy as np
from jax import lax
from jax.experimental import pallas as pl
from jax.experimental.pallas import tpu as pltpu

F32 = jnp.float32
BF16 = jnp.bfloat16
MESH = pl.DeviceIdType.MESH

EPS = 1e-6
HEAD_DIM = 128
SSM_GROUP = 16
STATE_DIM = 64
GROUPS_PER_BLOCK = 8
DILATION_PATTERNS = ((128, 1), (512, 4), (2048, 16))
BAND = 128
NEG = -1e30

ADAM_LR, ADAM_B1, ADAM_B2, ADAM_EPS, ADAM_WD, ADAM_STEP = 0.001, 0.9, 0.999, 1e-08, 0.01, 10

V7X_VMEM_BYTES = 64 * 2**20
VMEM_LIMIT = 48 * 2**20
SUBLANES = 8
PACK_LANES = 1024


def _cparams(sem=None):
    return pltpu.CompilerParams(dimension_semantics=sem, vmem_limit_bytes=VMEM_LIMIT)


def _blk(n, want):
    b = min(n, want)
    while n % b:
        b //= 2
    return b


def _colsum8(v):
    tm, w = v.shape
    return v.reshape(tm // SUBLANES, SUBLANES, w).sum(axis=0)


def _rowwise(name, fn, rows, vecs, out_rows, out_accs, tm=256, n_rows=None):
    rows = [r if isinstance(r, tuple) else (r, r.shape[1], 0) for r in rows]
    rows = [r if len(r) == 4 else r + (0,) for r in rows]
    s = rows[0][0].shape[0] if n_rows is None else n_rows
    tm = _blk(s, tm)
    assert all(r[3] % tm == 0 for r in rows)
    n_in, n_or = len(rows) + len(vecs), len(out_rows)

    def body(*refs):
        outs, accs = fn(*[r[...] for r in refs[:n_in]])
        for r, v in zip(refs[n_in : n_in + n_or], outs):
            r[...] = v.astype(r.dtype)
        if out_accs:
            acc_refs = refs[n_in + n_or :]

            @pl.when(pl.program_id(0) == 0)
            def _():
                for r in acc_refs:
                    r[...] = jnp.zeros(r.shape, r.dtype)

            for r, v in zip(acc_refs, accs):
                r[...] += v

    in_specs = [pl.BlockSpec((tm, w), lambda i, cb=cb, rb=r0 // tm: (i + rb, cb)) for (_, w, cb, r0) in rows]
    in_specs += [pl.BlockSpec(v.shape, lambda i, nd=v.ndim: (0,) * nd) for v in vecs]
    out_specs = [pl.BlockSpec((tm, w), lambda i: (i, 0)) for (w, _) in out_rows]
    out_specs += [pl.BlockSpec(sh, lambda i, nd=len(sh): (0,) * nd) for sh in out_accs]
    out_shape = [jax.ShapeDtypeStruct((s, w), dt) for (w, dt) in out_rows]
    out_shape += [jax.ShapeDtypeStruct(sh, F32) for sh in out_accs]
    res = pl.pallas_call(
        body, name=name, grid=(s // tm,), in_specs=in_specs, out_specs=out_specs, out_shape=out_shape,
        compiler_params=_cparams(("arbitrary",)),
    )(*[r[0] for r in rows], *vecs)
    return res


def _single(name, fn, ins, out_shapes):
    n_in = len(ins)

    def body(*refs):
        outs = fn(*[r[...] for r in refs[:n_in]])
        for r, v in zip(refs[n_in:], outs):
            r[...] = v.astype(r.dtype)

    vm = pl.BlockSpec(memory_space=pltpu.VMEM)
    return pl.pallas_call(
        body, name=name, in_specs=[vm] * n_in, out_specs=[vm] * len(out_shapes),
        out_shape=[jax.ShapeDtypeStruct(sh, dt) for sh, dt in out_shapes], compiler_params=_cparams(),
    )(*ins)


_NN = (((1,), (0,)), ((), ()))
_NT = (((1,), (1,)), ((), ()))
_TN = (((0,), (0,)), ((), ()))


def _mm(name, a, b, mode, out_dtype, *, tm=512, tn=1024, tk=2048, b_cs=False, o_cs=False, epi=None, extras=(), a_pro=None):
    if mode == "tn":
        k, m = a.shape
        n = b.shape[1]
    else:
        m, k = a.shape
        if mode == "nn":
            n = b.shape[2] * 4 if b_cs else b.shape[1]
        else:
            n = b.shape[1] if b_cs else b.shape[0]
    tm = _blk(m, tm)
    tn = _blk(n // 4 if (b_cs and mode == "nn") or o_cs else n, tn)
    tk = _blk(k // 4 if (b_cs and mode == "nt") else k, tk)
    nk = k // tk
    dims = {"nn": _NN, "nt": _NT, "tn": _TN}[mode]
    n_ex = len(extras)

    def body(*refs):
        a_ref, b_ref = refs[0], refs[1]
        ex_refs, o_ref = refs[2 : 2 + n_ex], refs[2 + n_ex]

        def finish(acc):
            if epi is not None:
                acc = epi(acc, *[r[...] for r in ex_refs])
            o_ref[...] = acc.astype(o_ref.dtype)

        a_blk = a_ref[...] if a_pro is None else a_pro(a_ref[...])
        part = lax.dot_general(a_blk, b_ref[...], dims, preferred_element_type=F32)
        if nk == 1:
            finish(part)
        else:
            acc_ref = refs[3 + n_ex]
            kk = pl.program_id(2)

            @pl.when(kk == 0)
            def _():
                acc_ref[...] = part

            @pl.when(kk > 0)
            def _():
                acc_ref[...] += part

            @pl.when(kk == nk - 1)
            def _():
                finish(acc_ref[...])

    a_spec = pl.BlockSpec((tk, tm), lambda i, j, kk: (kk, i)) if mode == "tn" else pl.BlockSpec((tm, tk), lambda i, j, kk: (i, kk))
    if mode == "nn":
        if b_cs:
            per = (n // 4) // tn
            b_spec = pl.BlockSpec((None, tk, tn), lambda i, j, kk: (j // per, kk, j % per))
        else:
            b_spec = pl.BlockSpec((tk, tn), lambda i, j, kk: (kk, j))
    elif mode == "nt":
        if b_cs:
            per = (k // 4) // tk
            b_spec = pl.BlockSpec((None, tn, tk), lambda i, j, kk: (kk // per, j, kk % per))
        else:
            b_spec = pl.BlockSpec((tn, tk), lambda i, j, kk: (j, kk))
    else:
        b_spec = pl.BlockSpec((tk, tn), lambda i, j, kk: (kk, j))
    if o_cs:
        per = (n // 4) // tn
        o_spec = pl.BlockSpec((None, tm, tn), lambda i, j, kk: (j // per, i, j % per))
        o_shape = jax.ShapeDtypeStruct((4, m, n // 4), out_dtype)
    else:
        o_spec = pl.BlockSpec((tm, tn), lambda i, j, kk: (i, j))
        o_shape = jax.ShapeDtypeStruct((m, n), out_dtype)
    ex_specs = [pl.BlockSpec((tm, tn), lambda i, j, kk: (i, j)) for _ in extras]
    return pl.pallas_call(
        body, name=name, grid=(m // tm, n // tn, nk), in_specs=[a_spec, b_spec] + ex_specs, out_specs=o_spec, out_shape=o_shape,
        scratch_shapes=[pltpu.VMEM((tm, tn), F32)] if nk > 1 else [],
        compiler_params=_cparams(("parallel", "parallel", "arbitrary")),
    )(a, b, *extras)


def _place():
    x, y, c = lax.axis_index("x"), lax.axis_index("y"), lax.axis_index("c")
    return x, y, c, [(1 - x, y), (x, 1 - y), (1 - x, 1 - y)]


def _allgather8(name, v):
    m_per, n = v.shape

    def body(x_ref, out_ref, send_sems, recv_sems, local_sem):
        x, y, c, chips = _place()
        me, sibling = (x, y, c), (x, y, 1 - c)

        def rows(px, py, pc):
            return out_ref.at[pl.ds((4 * px + 2 * py + pc) * m_per, m_per), :]

        def copy(k, block, to, src=None):
            return pltpu.make_async_remote_copy(
                src_ref=rows(*block) if src is None else src, dst_ref=rows(*block), send_sem=send_sems.at[k],
                recv_sem=recv_sems.at[k], device_id=to, device_id_type=MESH)

        mine = pltpu.make_async_copy(x_ref, rows(*me), local_sem)
        mine.start()
        first = [copy(0, me, sibling, src=x_ref)]
        first += [copy(1 + j, me, (*chip, c), src=x_ref) for j, chip in enumerate(chips)]
        for cp in first:
            cp.start()
        passed = [copy(4 + j, (*chip, c), sibling) for j, chip in enumerate(chips)]
        for j, chip in enumerate(chips):
            copy(1 + j, (*chip, c), me).wait_recv()
            passed[j].start()
        copy(0, sibling, me).wait_recv()
        for j, chip in enumerate(chips):
            copy(4 + j, (*chip, 1 - c), me).wait_recv()
        for cp in first + passed:
            cp.wait_send()
        mine.wait()

    return pl.pallas_call(
        body, name=name, out_shape=jax.ShapeDtypeStruct((8 * m_per, n), v.dtype),
        in_specs=[pl.BlockSpec(memory_space=pltpu.VMEM)], out_specs=pl.BlockSpec(memory_space=pltpu.VMEM),
        scratch_shapes=[pltpu.SemaphoreType.DMA((7,)), pltpu.SemaphoreType.DMA((7,)), pltpu.SemaphoreType.DMA],
        compiler_params=_cparams(),
    )(v)


def _gather_weights(stacks, after):
    nw = len(stacks)

    def body(*refs):
        outs = refs[nw + 1 : 2 * nw + 1]
        send_sems, recv_sems = refs[2 * nw + 1 :]
        x, y, c, chips = _place()
        me, sibling = (x, y, c), (x, y, 1 - c)

        def copy(w, k, block, to):
            px, py, pc = block
            half = outs[w].shape[1] // 2
            rows = outs[w].at[2 * px + py, pl.ds(pc * half, half), :]
            return pltpu.make_async_remote_copy(src_ref=rows, dst_ref=rows, send_sem=send_sems.at[6 * w + k],
                                                recv_sem=recv_sems.at[6 * w + k], device_id=to, device_id_type=MESH)

        sent = [copy(w, j, me, (*chip, c)) for w in range(nw) for j, chip in enumerate(chips)]
        for cp in sent:
            cp.start()
        for w in range(nw):
            for j, chip in enumerate(chips):
                copy(w, j, (*chip, c), me).wait_recv()
                passed = copy(w, 3 + j, (*chip, c), sibling)
                passed.start()
                sent.append(passed)
        for w in range(nw):
            for j, chip in enumerate(chips):
                copy(w, 3 + j, (*chip, 1 - c), me).wait_recv()
        for cp in sent:
            cp.wait_send()

    hbm = pl.BlockSpec(memory_space=pl.ANY)
    return pl.pallas_call(
        body, name="gather_weights", out_shape=[jax.ShapeDtypeStruct(w.shape, w.dtype) for w in stacks],
        in_specs=[hbm] * (nw + 1), out_specs=[hbm] * nw, input_output_aliases={i: i for i in range(nw)},
        scratch_shapes=[pltpu.SemaphoreType.DMA((6 * nw,)), pltpu.SemaphoreType.DMA((6 * nw,))],
        compiler_params=_cparams(),
    )(*stacks, after)


def _swap_halves(gs, name):
    nw = len(gs)

    def body(*refs):
        ins, gots = refs[:nw], refs[nw : 2 * nw]
        send_sems, recv_sems = refs[2 * nw :]
        x, y, c, _ = _place()
        cps = []
        for w in range(nw):
            half = ins[w].shape[1] // 2
            cps.append(pltpu.make_async_remote_copy(
                src_ref=ins[w].at[:, pl.ds((1 - c) * half, half), :], dst_ref=gots[w], send_sem=send_sems.at[w],
                recv_sem=recv_sems.at[w], device_id=(x, y, 1 - c), device_id_type=MESH))
            cps[-1].start()
        for cp in cps:
            cp.wait()

    hbm = pl.BlockSpec(memory_space=pl.ANY)
    return pl.pallas_call(
        body, name=name, out_shape=[jax.ShapeDtypeStruct((4, g.shape[1] // 2, g.shape[2]), g.dtype) for g in gs],
        in_specs=[hbm] * nw, out_specs=[hbm] * nw,
        scratch_shapes=[pltpu.SemaphoreType.DMA((nw,)), pltpu.SemaphoreType.DMA((nw,))], compiler_params=_cparams(),
    )(*gs)


def _split_start(name, bufs, plan, n, extra=()):
    n_plan = len(bufs)
    bufs = list(bufs) + list(extra)
    nb = len(bufs)

    def body(*refs):
        send_sems, recv_sems, token = refs[nb], refs[nb + 1], refs[-1]
        for k, (src, dst, to, _) in enumerate(plan(refs[:n_plan])):
            pltpu.make_async_remote_copy(src_ref=src, dst_ref=dst, send_sem=send_sems.at[k], recv_sem=recv_sems.at[k],
                                         device_id=to, device_id_type=MESH).start()
        token[...] = jnp.zeros(token.shape, token.dtype)

    hbm, sem = pl.BlockSpec(memory_space=pltpu.HBM), pl.BlockSpec(memory_space=pltpu.SEMAPHORE)
    res = pl.pallas_call(
        body, name=name,
        out_shape=(pltpu.SemaphoreType.DMA((n,)), pltpu.SemaphoreType.DMA((n,)), *[pltpu.HBM(b.shape, b.dtype) for b in bufs],
                   jax.ShapeDtypeStruct((SUBLANES, 128), F32)),
        in_specs=(hbm,) * nb, out_specs=(sem, sem) + (hbm,) * nb + (pl.BlockSpec(memory_space=pltpu.VMEM),),
        input_output_aliases={i: 2 + i for i in range(nb)},
        compiler_params=pltpu.CompilerParams(has_side_effects=pltpu.SideEffectType.DATAFLOW_SIDE_EFFECTING),
    )(*[pltpu.with_memory_space_constraint(b, pltpu.HBM) for b in bufs])
    return res[0], res[1], list(res[2 : 2 + nb]), res[-1]


def _split_wait(name, bufs, send_sems, recv_sems, after, plan):
    nb = len(bufs)

    def body(*refs):
        send, recv = refs[nb], refs[nb + 1]
        for k, (src, _, to, land) in enumerate(plan(refs[:nb])):
            cp = pltpu.make_async_remote_copy(src_ref=src, dst_ref=land, send_sem=send.at[k], recv_sem=recv.at[k],
                                              device_id=to, device_id_type=MESH)
            cp.wait_send()
            cp.wait_recv()

    hbm, sem = pl.BlockSpec(memory_space=pltpu.HBM), pl.BlockSpec(memory_space=pltpu.SEMAPHORE)
    return pl.pallas_call(
        body, name=name, out_shape=tuple(pltpu.HBM(b.shape, b.dtype) for b in bufs),
        in_specs=(hbm,) * nb + (sem, sem, pl.BlockSpec(memory_space=pl.ANY)), out_specs=(hbm,) * nb,
        input_output_aliases={i: i for i in range(nb)},
        compiler_params=pltpu.CompilerParams(has_side_effects=pltpu.SideEffectType.DATAFLOW_SIDE_EFFECTING),
    )(*bufs, send_sems, recv_sems, after)


def _indexed(name, fn, idx, ins, out, grid):
    def body(idx_ref, *refs):
        refs[-1][...] = fn(*[r[...] for r in refs[:-1]]).astype(refs[-1].dtype)

    (o_shape, o_dtype), o_block, o_map = out
    return pl.pallas_call(
        body, name=name, out_shape=jax.ShapeDtypeStruct(o_shape, o_dtype),
        grid_spec=pltpu.PrefetchScalarGridSpec(
            num_scalar_prefetch=1, grid=grid, in_specs=[pl.BlockSpec(blk, mp) for (_, blk, mp) in ins],
            out_specs=pl.BlockSpec(o_block, o_map)),
        compiler_params=_cparams(("arbitrary",) * len(grid)),
    )(idx, *[a for (a, _, _) in ins])


def _rstd(v):
    return lax.rsqrt(jnp.mean(v * v, axis=-1, keepdims=True) + EPS)


def _rms_bwd(dy, xh, r):
    return r * (dy - xh * jnp.mean(dy * xh, axis=-1, keepdims=True))


def _heads(v):
    return [v[:, h * HEAD_DIM : (h + 1) * HEAD_DIM] for h in range(v.shape[1] // HEAD_DIM)]


def _gelu(v):
    k = math.sqrt(2.0 / math.pi)
    return 0.5 * v * (1.0 + jnp.tanh(k * (v + 0.044715 * v * v * v)))


def _gelu_grad(v):
    k = math.sqrt(2.0 / math.pi)
    t = jnp.tanh(k * (v + 0.044715 * v * v * v))
    return 0.5 * (1.0 + t) + 0.5 * v * (1.0 - t * t) * k * (1.0 + 3 * 0.044715 * v * v)


def _sigmoid(v):
    return 1.0 / (1.0 + jnp.exp(-v))


def _slopes(n_heads):
    return [2.0 ** (-8.0 * (h + 1.0) / n_heads) for h in range(n_heads)]


def _band_iotas():
    ri = lax.broadcasted_iota(jnp.int32, (BAND, BAND), 0)
    ci = lax.broadcasted_iota(jnp.int32, (BAND, BAND), 1)
    return ri, ci


def _to_lanes(cols):
    rows = cols[0].shape[0]
    lane = lax.broadcasted_iota(jnp.int32, (rows, HEAD_DIM), 1)
    out = jnp.zeros((rows, HEAD_DIM), F32)
    for h, col in enumerate(cols):
        out = jnp.where(lane == h, col, out)
    return out


def _from_lanes(v, h, width):
    return jnp.broadcast_to(v[:, h : h + 1], (v.shape[0], width))


def _attn_fwd(name, q, k, v, bps, dil):
    s, aw = q.shape
    n_heads, nb = aw // HEAD_DIM, s // BAND
    scale = HEAD_DIM**-0.5
    slopes = _slopes(n_heads)

    assert nb % 2 == 0

    def body(q_ref, kp_ref, kc_ref, vp_ref, vc_ref, o_ref, l_ref):
        i = pl.program_id(0)
        ri, ci = _band_iotas()
        dist_c = (ri - ci).astype(F32)
        dist_p = (ri - ci + BAND).astype(F32)
        valid_c = ci <= ri
        hs = range(n_heads)
        sls = [slice(h * HEAD_DIM, (h + 1) * HEAD_DIM) for h in hs]
        lo, hi = slice(0, BAND), slice(BAND, 2 * BAND)
        chains = []
        for sub, rows in enumerate((lo, hi)):
            valid_p = ci >= ri + jnp.where((2 * i + sub) % bps == 0, BAND, 0)
            for h in hs:
                k_prev = kp_ref[:, sls[h]] if sub == 0 else kc_ref[lo, sls[h]]
                v_prev = vp_ref[:, sls[h]] if sub == 0 else vc_ref[lo, sls[h]]
                chains.append((q_ref[rows, sls[h]], kc_ref[rows, sls[h]], k_prev, vc_ref[rows, sls[h]], v_prev, h, valid_p))
        cs = range(len(chains))
        sc = [lax.dot_general(chains[c][0], chains[c][1], _NT, preferred_element_type=F32) for c in cs]
        sp = [lax.dot_general(chains[c][0], chains[c][2], _NT, preferred_element_type=F32) for c in cs]
        sc = [jnp.where(valid_c, sc[c] * scale - (slopes[chains[c][5]] * dil) * dist_c, NEG) for c in cs]
        sp = [jnp.where(chains[c][6], sp[c] * scale - (slopes[chains[c][5]] * dil) * dist_p, NEG) for c in cs]
        m = [jnp.maximum(jnp.max(sc[c], axis=1, keepdims=True), jnp.max(sp[c], axis=1, keepdims=True)) for c in cs]
        pc = [jnp.exp(sc[c] - m[c]) for c in cs]
        pp = [jnp.exp(sp[c] - m[c]) for c in cs]
        den = [jnp.sum(pc[c], axis=1, keepdims=True) + jnp.sum(pp[c], axis=1, keepdims=True) for c in cs]
        acc = [jnp.dot(pc[c].astype(BF16), chains[c][3], preferred_element_type=F32) for c in cs]
        acc = [acc[c] + jnp.dot(pp[c].astype(BF16), chains[c][4], preferred_element_type=F32) for c in cs]
        out = [jnp.concatenate([acc[c] / den[c] for c in cs[sub * n_heads : (sub + 1) * n_heads]], axis=1) for sub in range(2)]
        o_ref[...] = jnp.concatenate(out, axis=0).astype(o_ref.dtype)
        lse = [_to_lanes([m[c] + jnp.log(den[c]) for c in cs[sub * n_heads : (sub + 1) * n_heads]]) for sub in range(2)]
        l_ref[...] = jnp.concatenate(lse, axis=0)

    cur = pl.BlockSpec((2 * BAND, aw), lambda i: (i, 0))
    prev = pl.BlockSpec((BAND, aw), lambda i: (jnp.maximum(2 * i - 1, 0), 0))
    return pl.pallas_call(
        body, name=name, grid=(nb // 2,), in_specs=[cur, prev, cur, prev, cur],
        out_specs=[cur, pl.BlockSpec((2 * BAND, HEAD_DIM), lambda i: (i, 0))],
        out_shape=[jax.ShapeDtypeStruct((s, aw), BF16), jax.ShapeDtypeStruct((s, HEAD_DIM), F32)],
        compiler_params=_cparams(("arbitrary",)),
    )(q, k, k, v, v)


def _attn_bwd(name, q, k, v, do, lse, delta, bps, dil):
    s, aw = q.shape
    n_heads, nb = aw // HEAD_DIM, s // BAND
    scale = HEAD_DIM**-0.5
    slopes = _slopes(n_heads)

    assert nb % 2 == 0

    def body(qc_ref, qn_ref, doc_ref, don_ref, lc_ref, ln_ref, dc_ref, dn_ref, k_ref, v_ref, dq_ref, dk_ref, dv_ref, carry):
        i = pl.program_id(0)

        @pl.when(i == 0)
        def _():
            carry[...] = jnp.zeros(carry.shape, F32)

        ri, ci = _band_iotas()
        dist_c = (ri - ci).astype(F32)
        dist_p = (ri - ci + BAND).astype(F32)
        valid_c = ci <= ri
        valid_ab = ci >= ri + jnp.where((2 * i + 1) % bps != 0, 0, BAND)
        valid_bn = ci >= ri + jnp.where(jnp.logical_and((2 * i + 2) % bps != 0, 2 * i + 2 < nb), 0, BAND)
        carried = carry[...]
        hs = range(n_heads)
        sls = [slice(h * HEAD_DIM, (h + 1) * HEAD_DIM) for h in hs]
        lo, hi, al = slice(0, BAND), slice(BAND, 2 * BAND), slice(None)
        sides = [((qc_ref, doc_ref, lc_ref, dc_ref), lo, lo, dist_c, valid_c), ((qc_ref, doc_ref, lc_ref, dc_ref), hi, lo, dist_p, valid_ab),
                 ((qc_ref, doc_ref, lc_ref, dc_ref), hi, hi, dist_c, valid_c), ((qn_ref, don_ref, ln_ref, dn_ref), al, hi, dist_p, valid_bn)]
        res = []
        for (q_ref, do_ref, l_ref, d_ref), qr, kr, dist, valid in sides:
            qs, dos = [q_ref[qr, sl] for sl in sls], [do_ref[qr, sl] for sl in sls]
            ks, vs = [k_ref[kr, sl] for sl in sls], [v_ref[kr, sl] for sl in sls]
            sc = [lax.dot_general(qs[h], ks[h], _NT, preferred_element_type=F32) for h in hs]
            dp = [lax.dot_general(dos[h], vs[h], _NT, preferred_element_type=F32) for h in hs]
            lse_v, delta_v = l_ref[qr, :], d_ref[qr, :]
            p = [jnp.where(valid, jnp.exp(sc[h] * scale - (slopes[h] * dil) * dist - _from_lanes(lse_v, h, BAND)), 0.0) for h in hs]
            ds = [(p[h] * (dp[h] - _from_lanes(delta_v, h, BAND))).astype(BF16) for h in hs]
            pb = [p[h].astype(BF16) for h in hs]
            dv = [lax.dot_general(pb[h], dos[h], _TN, preferred_element_type=F32) for h in hs]
            dk = [lax.dot_general(ds[h], qs[h], _TN, preferred_element_type=F32) for h in hs]
            dq = [jnp.dot(ds[h], ks[h], preferred_element_type=F32) for h in hs]
            res.append((dq, dk, dv))
        (dq_aa, dk_aa, dv_aa), (dq_ba, dk_ba, dv_ba), (dq_bb, dk_bb, dv_bb), (dq_nb, dk_nb, dv_nb) = res
        cat = lambda parts: jnp.concatenate(parts, axis=1)
        dq_a = carried + cat(dq_aa) * scale
        dq_b = cat([dq_ba[h] + dq_bb[h] for h in hs]) * scale
        dq_ref[...] = jnp.concatenate([dq_a, dq_b], axis=0).astype(dq_ref.dtype)
        carry[...] = cat(dq_nb) * scale
        dk_a, dk_b = cat([dk_aa[h] + dk_ba[h] for h in hs]), cat([dk_bb[h] + dk_nb[h] for h in hs])
        dk_ref[...] = (jnp.concatenate([dk_a, dk_b], axis=0) * scale).astype(dk_ref.dtype)
        dv_a, dv_b = cat([dv_aa[h] + dv_ba[h] for h in hs]), cat([dv_bb[h] + dv_nb[h] for h in hs])
        dv_ref[...] = jnp.concatenate([dv_a, dv_b], axis=0).astype(dv_ref.dtype)

    cur = pl.BlockSpec((2 * BAND, aw), lambda i: (i, 0))
    nxt = pl.BlockSpec((BAND, aw), lambda i: (jnp.minimum(2 * i + 2, nb - 1), 0))
    cur1 = pl.BlockSpec((2 * BAND, HEAD_DIM), lambda i: (i, 0))
    nxt1 = pl.BlockSpec((BAND, HEAD_DIM), lambda i: (jnp.minimum(2 * i + 2, nb - 1), 0))
    return pl.pallas_call(
        body, name=name, grid=(nb // 2,), in_specs=[cur, nxt, cur, nxt, cur1, nxt1, cur1, nxt1, cur, cur], out_specs=[cur] * 3,
        out_shape=[jax.ShapeDtypeStruct((s, aw), BF16)] * 3, scratch_shapes=[pltpu.VMEM((BAND, aw), F32)],
        compiler_params=_cparams(("arbitrary",)),
    )(q, q, do, do, lse, lse, delta, delta, k, v)


STREAM_ROWS = BAND * max(dil for _, dil in DILATION_PATTERNS)


def _stream_orders(name, fn, srcs, vecs, n_vals, n_heads):
    s = srcs[0][0].shape[0]
    dils = [dil for _, dil in DILATION_PATTERNS]
    n_in = len(srcs) + len(vecs)

    def body(*refs):
        outs, scr = refs[n_in:-1], refs[-1]
        for vi, val in enumerate(fn(*[r[...] for r in refs[:n_in]])):
            scr[...] = val
            for di, dil in enumerate(dils):
                o = outs[vi * len(dils) + di]
                if dil == 1:
                    o[...] = val.astype(o.dtype)
                    continue
                for part in range(STREAM_ROWS // (dil * BAND)):
                    for r in range(dil):
                        rows = scr[pl.ds(part * dil * BAND + r, BAND, stride=dil), :]
                        o[r, part * BAND : (part + 1) * BAND, :] = rows.astype(o.dtype)

    in_specs = [pl.BlockSpec((STREAM_ROWS, HEAD_DIM), lambda b, h, cb=cb: (b, cb + h)) for (_, cb) in srcs]
    in_specs += [pl.BlockSpec(v.shape, lambda b, h, nd=v.ndim: (0,) * nd) for v in vecs]
    out_specs, out_shape = [], []
    for _ in range(n_vals):
        for dil in dils:
            if dil == 1:
                out_specs.append(pl.BlockSpec((STREAM_ROWS, HEAD_DIM), lambda b, h: (b, h)))
                out_shape.append(jax.ShapeDtypeStruct((s, n_heads * HEAD_DIM), BF16))
            else:
                out_specs.append(pl.BlockSpec((dil, STREAM_ROWS // dil, HEAD_DIM), lambda b, h: (0, b, h)))
                out_shape.append(jax.ShapeDtypeStruct((dil, s // dil, n_heads * HEAD_DIM), BF16))
    res = pl.pallas_call(
        body, name=name, grid=(s // STREAM_ROWS, n_heads), in_specs=in_specs, out_specs=out_specs, out_shape=out_shape,
        scratch_shapes=[pltpu.VMEM((STREAM_ROWS, HEAD_DIM), F32)], compiler_params=_cparams(("arbitrary", "arbitrary")),
    )(*[a for (a, _) in srcs], *vecs)
    res = [r.reshape(s, n_heads * HEAD_DIM) for r in res]
    return [res[vi * len(dils) : (vi + 1) * len(dils)] for vi in range(n_vals)]


def _natural_order(name, fn, vals, extras, out_dtypes, n_heads):
    s = vals[0][0].shape[0]
    dils = [dil for _, dil in DILATION_PATTERNS]
    n_in = len(vals) * len(dils) + len(extras)

    def body(*refs):
        outs, scr = refs[n_in:-1], refs[-1]
        blocks = []
        for vi in range(len(vals)):
            per_pattern = []
            for di, dil in enumerate(dils):
                x = refs[vi * len(dils) + di]
                if dil == 1:
                    per_pattern.append(x[...].astype(F32))
                    continue
                for part in range(STREAM_ROWS // (dil * BAND)):
                    for r in range(dil):
                        scr[pl.ds(part * dil * BAND + r, BAND, stride=dil), :] = x[r, part * BAND : (part + 1) * BAND, :].astype(F32)
                per_pattern.append(scr[...])
            blocks.append(per_pattern)
        res = fn(pl.program_id(1), blocks, *[r[...] for r in refs[len(vals) * len(dils) : n_in]])
        for o, v in zip(outs, res):
            o[...] = v.astype(o.dtype)

    in_specs, ins = [], []
    for per_pattern in vals:
        for arr, dil in zip(per_pattern, dils):
            if dil == 1:
                in_specs.append(pl.BlockSpec((STREAM_ROWS, HEAD_DIM), lambda b, h: (b, h)))
                ins.append(arr)
            else:
                in_specs.append(pl.BlockSpec((dil, STREAM_ROWS // dil, HEAD_DIM), lambda b, h: (0, b, h)))
                ins.append(arr.reshape(dil, s // dil, n_heads * HEAD_DIM))
    in_specs += [pl.BlockSpec((STREAM_ROWS, HEAD_DIM), lambda b, h: (b, 0)) for _ in extras]
    return pl.pallas_call(
        body, name=name, grid=(s // STREAM_ROWS, n_heads), in_specs=in_specs,
        out_specs=[pl.BlockSpec((STREAM_ROWS, HEAD_DIM), lambda b, h: (b, h)) for _ in out_dtypes],
        out_shape=[jax.ShapeDtypeStruct((s, n_heads * HEAD_DIM), dt) for dt in out_dtypes],
        scratch_shapes=[pltpu.VMEM((STREAM_ROWS, HEAD_DIM), F32)], compiler_params=_cparams(("arbitrary", "arbitrary")),
    )(*ins, *extras)


def _to_streams(v, dil):
    if dil == 1:
        return v
    s, w = v.shape
    return v.reshape(s // dil, dil, w).transpose(1, 0, 2).reshape(s, w)


def _from_streams(v, dil):
    if dil == 1:
        return v
    s, w = v.shape
    return v.reshape(dil, s // dil, w).transpose(1, 0, 2).reshape(s, w)


SCAN_LANES = 1024
SCAN_LANES_BWD = 512
SSM_ROWS = 128
SSM_FWD_ROWS = 256
FF_TM = 1024
FF_TN = 2048


def _cmul(ar, ai, br, bi):
    return ar * br - ai * bi, ar * bi + ai * br


def _cmul_conj(ar, ai, br, bi):
    return ar * br + ai * bi, ar * bi - ai * br


def _ssm_fwd(u_src, bre, bim, abre, abim, cre, cim, consts, d_skip):
    u_arr, sw, ucb = u_src
    s = u_arr.shape[0]
    nblk = bre.shape[0]
    nst = nblk * 512
    t = _blk(s, SSM_FWD_ROWS)
    w = _blk(nst, SCAN_LANES)

    def body(u_ref, bre_ref, bim_ref, abre_ref, abim_ref, cre_ref, cim_ref, k_ref, ds_ref, y_ref, hr_ref, hi_ref, xr, xi, car_r, car_i):
        @pl.when(pl.program_id(0) == 0)
        def _():
            car_r[...] = jnp.zeros(car_r.shape, F32)
            car_i[...] = jnp.zeros(car_i.shape, F32)

        u = u_ref[...].astype(F32)
        ub = u_ref[...].astype(BF16)
        row = lax.broadcasted_iota(jnp.int32, u.shape, 0)
        u1b = jnp.where(row % SUBLANES == 0, 0.0, pltpu.roll(u, 1, 0)).astype(BF16)
        for b in range(nblk):
            ch, st = slice(b * 128, (b + 1) * 128), slice(b * 512, (b + 1) * 512)
            xr[:, st] = jnp.dot(ub[:, ch], bre_ref[b], preferred_element_type=F32) + jnp.dot(u1b[:, ch], abre_ref[b], preferred_element_type=F32)
            xi[:, st] = jnp.dot(ub[:, ch], bim_ref[b], preferred_element_type=F32) + jnp.dot(u1b[:, ch], abim_ref[b], preferred_element_type=F32)
        for c in range(nst // w):
            ls = slice(c * w, (c + 1) * w)
            mult = [k_ref[i, :, ls] for i in range(8)]

            def step(i, carry, ls=ls, mult=mult):
                cr, ci = carry
                r0 = pl.multiple_of(i * SUBLANES, SUBLANES)
                hr, hi = xr[pl.ds(r0, SUBLANES), ls], xi[pl.ds(r0, SUBLANES), ls]
                for n, sh in ((1, 2), (2, 4)):
                    pr, pi = _cmul(mult[2 * n], mult[2 * n + 1], pltpu.roll(hr, sh, 0), pltpu.roll(hi, sh, 0))
                    hr, hi = hr + pr, hi + pi
                pr, pi = _cmul(mult[6], mult[7], cr, ci)
                hr, hi = hr + pr, hi + pi
                xr[pl.ds(r0, SUBLANES), ls] = hr
                xi[pl.ds(r0, SUBLANES), ls] = hi
                return jnp.broadcast_to(hr[7:8, :], hr.shape), jnp.broadcast_to(hi[7:8, :], hi.shape)

            cr, ci = lax.fori_loop(0, t // SUBLANES, step, (car_r[:, ls], car_i[:, ls]))
            car_r[:, ls] = cr
            car_i[:, ls] = ci
        hrb, hib = xr[...].astype(BF16), xi[...].astype(BF16)
        hr_ref[...] = hrb
        hi_ref[...] = hib
        for b in range(nblk):
            ch, st = slice(b * 128, (b + 1) * 128), slice(b * 512, (b + 1) * 512)
            yb = jnp.dot(hrb[:, st], cre_ref[b], preferred_element_type=F32)
            yb -= jnp.dot(hib[:, st], cim_ref[b], preferred_element_type=F32)
            y_ref[:, ch] = yb + ds_ref[:, ch] * u[:, ch]

    whole = lambda a: pl.BlockSpec(a.shape, lambda i, nd=a.ndim: (0,) * nd)
    return pl.pallas_call(
        body, name="ssm_fwd", grid=(s // t,),
        in_specs=[pl.BlockSpec((t, sw), lambda i: (i, ucb))] + [whole(a) for a in (bre, bim, abre, abim, cre, cim, consts, d_skip)],
        out_specs=[pl.BlockSpec((t, sw), lambda i: (i, 0)), pl.BlockSpec((t, nst), lambda i: (i, 0)), pl.BlockSpec((t, nst), lambda i: (i, 0))],
        out_shape=[jax.ShapeDtypeStruct((s, sw), F32), jax.ShapeDtypeStruct((s, nst), BF16), jax.ShapeDtypeStruct((s, nst), BF16)],
        scratch_shapes=[pltpu.VMEM((t, nst), F32), pltpu.VMEM((t, nst), F32), pltpu.VMEM((SUBLANES, nst), F32), pltpu.VMEM((SUBLANES, nst), F32)],
        compiler_params=_cparams(("arbitrary",)),
    )(u_arr, bre, bim, abre, abim, cre, cim, consts, d_skip)


def _ssm_bwd(dy, u_src, hr, hi, ctre, ctim, acre, acim, btre, btim, consts, d_skip):
    u_arr, sw, ucb = u_src
    s = dy.shape[0]
    nblk = ctre.shape[0]
    nst = nblk * 512
    t = _blk(s, SSM_ROWS)
    w = _blk(nst, SCAN_LANES_BWD)
    nt = s // t

    def body(dy_ref, u_ref, hr_ref, hi_ref, ctre_ref, ctim_ref, acre_ref, acim_ref, btre_ref, btim_ref, k_ref, ds_ref,
             du_ref, db_ref, dc_ref, da_ref, dd_ref, lr, li, hfr, hfi, car_r, car_i):
        @pl.when(pl.program_id(0) == 0)
        def _():
            car_r[...] = jnp.zeros(car_r.shape, F32)
            car_i[...] = jnp.zeros(car_i.shape, F32)
            db_ref[...] = jnp.zeros(db_ref.shape, F32)
            dc_ref[...] = jnp.zeros(dc_ref.shape, F32)
            da_ref[...] = jnp.zeros(da_ref.shape, F32)
            dd_ref[...] = jnp.zeros(dd_ref.shape, F32)

        dyv, u = dy_ref[...].astype(F32), u_ref[...].astype(F32)
        dyb, ub = dyv.astype(BF16), u.astype(BF16)
        hrb, hib = hr_ref[...], hi_ref[...]
        hfr[...] = hrb.astype(F32)
        hfi[...] = hib.astype(F32)
        row = lax.broadcasted_iota(jnp.int32, dyv.shape, 0)
        dy1b = jnp.where(row % SUBLANES == SUBLANES - 1, 0.0, pltpu.roll(dyv, t - 1, 0)).astype(BF16)
        for b in range(nblk):
            ch, st = slice(b * 128, (b + 1) * 128), slice(b * 512, (b + 1) * 512)
            lr[:, st] = jnp.dot(dyb[:, ch], ctre_ref[b], preferred_element_type=F32) + jnp.dot(dy1b[:, ch], acre_ref[b], preferred_element_type=F32)
            li[:, st] = -(jnp.dot(dyb[:, ch], ctim_ref[b], preferred_element_type=F32) + jnp.dot(dy1b[:, ch], acim_ref[b], preferred_element_type=F32))
        last_row = lax.broadcasted_iota(jnp.int32, (SUBLANES, w), 0) == SUBLANES - 1
        for c in range(nst // w):
            ls = slice(c * w, (c + 1) * w)
            mult = [k_ref[i, :, ls] for i in range(8)]

            def step(i, carry, ls=ls, mult=mult):
                cr, ci, ar, ai = carry
                r0 = pl.multiple_of((t // SUBLANES - 1 - i) * SUBLANES, SUBLANES)
                gr, gi = lr[pl.ds(r0, SUBLANES), ls], li[pl.ds(r0, SUBLANES), ls]
                for n, sh in ((1, 2), (2, 4)):
                    pr, pi = _cmul_conj(mult[2 * n], mult[2 * n + 1], pltpu.roll(gr, SUBLANES - sh, 0), pltpu.roll(gi, SUBLANES - sh, 0))
                    gr, gi = gr + pr, gi + pi
                pr, pi = _cmul_conj(mult[6], mult[7], cr, ci)
                gr, gi = gr + pr, gi + pi
                lr[pl.ds(r0, SUBLANES), ls] = gr
                li[pl.ds(r0, SUBLANES), ls] = gi
                nr = jnp.where(last_row, cr, pltpu.roll(gr, SUBLANES - 1, 0))
                ni = jnp.where(last_row, ci, pltpu.roll(gi, SUBLANES - 1, 0))
                fr, fi = hfr[pl.ds(r0, SUBLANES), ls], hfi[pl.ds(r0, SUBLANES), ls]
                pr, pi = _cmul_conj(fr, fi, nr, ni)
                return (jnp.broadcast_to(gr[0:1, :], gr.shape), jnp.broadcast_to(gi[0:1, :], gi.shape), ar + pr, ai + pi)

            zero = jnp.zeros((SUBLANES, w), F32)
            cr, ci, ar, ai = lax.fori_loop(0, t // SUBLANES, step, (car_r[:, ls], car_i[:, ls], zero, zero))
            car_r[:, ls] = cr
            car_i[:, ls] = ci
            da_ref[0, :, ls] += ar
            da_ref[1, :, ls] += ai
        lrb, lib = lr[...].astype(BF16), li[...].astype(BF16)
        for b in range(nblk):
            ch, st = slice(b * 128, (b + 1) * 128), slice(b * 512, (b + 1) * 512)
            dub = jnp.dot(lrb[:, st], btre_ref[b], preferred_element_type=F32)
            dub += jnp.dot(lib[:, st], btim_ref[b], preferred_element_type=F32)
            du_ref[:, ch] = (dub + ds_ref[:, ch] * dyv[:, ch]).astype(du_ref.dtype)
            db_ref[0, b] += lax.dot_general(ub[:, ch], lrb[:, st], _TN, preferred_element_type=F32)
            db_ref[1, b] += lax.dot_general(ub[:, ch], lib[:, st], _TN, preferred_element_type=F32)
            dc_ref[0, b] += lax.dot_general(dyb[:, ch], hrb[:, st], _TN, preferred_element_type=F32)
            dc_ref[1, b] -= lax.dot_general(dyb[:, ch], hib[:, st], _TN, preferred_element_type=F32)
        dd_ref[...] += _colsum8(dyv * u)

    whole = lambda a: pl.BlockSpec(a.shape, lambda i, nd=a.ndim: (0,) * nd)
    rev = lambda wd, cb=0: pl.BlockSpec((t, wd), lambda i, cb=cb: (nt - 1 - i, cb))
    acc = lambda sh: pl.BlockSpec(sh, lambda i, nd=len(sh): (0,) * nd)
    db_shape, da_shape, dd_shape = (2, nblk, 128, 512), (2, SUBLANES, nst), (SUBLANES, sw)
    return pl.pallas_call(
        body, name="ssm_bwd", grid=(nt,),
        in_specs=[rev(sw), rev(sw, ucb), rev(nst), rev(nst)] + [whole(a) for a in (ctre, ctim, acre, acim, btre, btim, consts, d_skip)],
        out_specs=[rev(sw), acc(db_shape), acc(db_shape), acc(da_shape), acc(dd_shape)],
        out_shape=[jax.ShapeDtypeStruct((s, sw), BF16), jax.ShapeDtypeStruct(db_shape, F32), jax.ShapeDtypeStruct(db_shape, F32),
                   jax.ShapeDtypeStruct(da_shape, F32), jax.ShapeDtypeStruct(dd_shape, F32)],
        scratch_shapes=[pltpu.VMEM((t, nst), F32)] * 4 + [pltpu.VMEM((SUBLANES, nst), F32)] * 2,
        compiler_params=_cparams(("arbitrary",)),
    )(dy, u_arr, hr, hi, ctre, ctim, acre, acim, btre, btim, consts, d_skip)


def _ssm_discretise(lam_re, lam_im, log_step):
    step = jnp.exp(log_step)
    e = jnp.exp(lam_re * step)
    ar, ai = e * jnp.cos(lam_im * step), e * jnp.sin(lam_im * step)
    den = lam_re * lam_re + lam_im * lam_im
    inv_r, inv_i = lam_re / den, -lam_im / den
    fr, fi = _cmul(ar - 1.0, ai, inv_r, inv_i)
    return step, ar, ai, inv_r, inv_i, fr, fi


def _ssm_prep_fn(lam_re, lam_im, log_step):
    _, ar, ai, _, _, fr, fi = _ssm_discretise(lam_re, lam_im, log_step)
    pw = [(ar, ai)]
    for n in range(1, 8):
        pw.append(_cmul(*pw[n - 1], ar, ai))
    return [p[0] for p in pw] + [p[1] for p in pw] + [fr, fi]


def _ssm_param_grad_fn(lam_re, lam_im, log_step, da_r, da_i, df_r, df_i):
    step, ar, ai, inv_r, inv_i, fr, fi = _ssm_discretise(lam_re, lam_im, log_step)
    pr, pi = _cmul_conj(inv_r, inv_i, df_r, df_i)
    dat_r, dat_i = da_r + pr, da_i + pi
    wr, wi = _cmul(fr, fi, inv_r, inv_i)
    dl_r, dl_i = _cmul_conj(-wr, -wi, df_r, df_i)
    dz_r, dz_i = _cmul_conj(ar, ai, dat_r, dat_i)
    dl_r, dl_i = dl_r + step * dz_r, dl_i + step * dz_i
    dstep = jnp.sum(dz_r * lam_re + dz_i * lam_im, axis=-1, keepdims=True)
    return [dl_r, dl_i, step * dstep]


def _block_diag(v, rows_first):
    g, a, b = v.shape
    nb = g // GROUPS_PER_BLOCK
    eye = jnp.eye(GROUPS_PER_BLOCK, dtype=v.dtype)
    v = v.reshape(nb, GROUPS_PER_BLOCK, a, 1, b) * eye[None, :, None, :, None]
    return v.reshape(nb, GROUPS_PER_BLOCK * a, GROUPS_PER_BLOCK * b)


def _block_diag_take(v, a, b):
    nb = v.shape[0]
    v = v.reshape(nb, GROUPS_PER_BLOCK, a, GROUPS_PER_BLOCK, b)
    return jnp.stack([v[:, g, :, g, :] for g in range(GROUPS_PER_BLOCK)], axis=1).reshape(nb * GROUPS_PER_BLOCK, a, b)


def _adamw(w, g, m, v):
    m = ADAM_B1 * m + (1.0 - ADAM_B1) * g
    v = ADAM_B2 * v + (1.0 - ADAM_B2) * (g * g)
    m_hat = m / (1.0 - ADAM_B1**ADAM_STEP)
    v_hat = v / (1.0 - ADAM_B2**ADAM_STEP)
    delta = -ADAM_LR * (m_hat / (jnp.sqrt(v_hat) + ADAM_EPS) + ADAM_WD * w)
    return delta, m, v


def _adamw_rows(name, w, g, m, v):
    wd = w.shape[1]
    return _rowwise(name, lambda a, b, c, d: (list(_adamw(a, b, c, d)), []), [w, g, m, v], [], [(wd, F32)] * 3, [], tm=128)


SMALL = ["b_ada", "norm1_g", "q_norm_g", "k_norm_g", "lam_re", "lam_im", "log_step", "b_re", "b_im", "c_re", "c_im",
         "d_skip", "b_glu", "attn_out_g", "ssm_out_g", "norm2_g"]
LARGE = ["w_in", "w_glu", "w_out", "w_ff1", "w_ff2"]
ORDER = ["w_ada", "b_ada", "norm1_g", "w_in", "q_norm_g", "k_norm_g", "lam_re", "lam_im", "log_step", "b_re", "b_im", "c_re",
         "c_im", "d_skip", "w_glu", "b_glu", "attn_out_g", "ssm_out_g", "w_out", "norm2_g", "w_ff1", "w_ff2"]


def _pack(arrs):
    flat = jnp.concatenate([a.reshape(-1) for a in arrs])
    rows = -(-flat.shape[0] // PACK_LANES)
    rows = -(-rows // SUBLANES) * SUBLANES
    return jnp.pad(flat, (0, rows * PACK_LANES - flat.shape[0])).reshape(rows, PACK_LANES)


def _unpack(packed, shapes):
    flat, out, o = packed.reshape(-1), [], 0
    for sh in shapes:
        n = int(np.prod(sh))
        out.append(flat[o : o + n].reshape(sh))
        o += n
    return out


def kernel(x, c, w_ada, b_ada, norm1_g, w_in, q_norm_g, k_norm_g, lam_re, lam_im, log_step, b_re, b_im, c_re, c_im, d_skip, w_glu, b_glu, attn_out_g, ssm_out_g, w_out, norm2_g, w_ff1, w_ff2, loss_target, m_w_ada, m_b_ada, m_norm1_g, m_w_in, m_q_norm_g, m_k_norm_g, m_lam_re, m_lam_im, m_log_step, m_b_re, m_b_im, m_c_re, m_c_im, m_d_skip, m_w_glu, m_b_glu, m_attn_out_g, m_ssm_out_g, m_w_out, m_norm2_g, m_w_ff1, m_w_ff2, v_w_ada, v_b_ada, v_norm1_g, v_w_in, v_q_norm_g, v_k_norm_g, v_lam_re, v_lam_im, v_log_step, v_b_re, v_b_im, v_c_re, v_c_im, v_d_skip, v_w_glu, v_b_glu, v_attn_out_g, v_ssm_out_g, v_w_out, v_norm2_g, v_w_ff1, v_w_ff2):
    args = dict(locals())
    wts = {n: args[n] for n in ORDER}
    mom = {n: args["m_" + n] for n in ORDER}
    var = {n: args["v_" + n] for n in ORDER}

    x2, tgt = x[0], loss_target[0]
    s, d = x2.shape
    aw = d // 2
    sw = d - aw
    n_groups = sw // SSM_GROUP
    nst = n_groups * STATE_DIM
    chip = 2 * lax.axis_index("x") + lax.axis_index("y")

    chip_idx = jnp.reshape(chip, (1,)).astype(jnp.int32)
    core_idx = jnp.reshape(lax.axis_index("c"), (1,)).astype(jnp.int32)

    def cast_into_stack(n):
        w2 = wts[n][0]
        rows, cols = w2.shape
        tr = _blk(rows, 256)
        return _indexed("cast_" + n, lambda a: a, chip_idx, [(w2, (tr, cols), lambda r, ix: (r, 0))],
                        (((4, rows, cols), BF16), (None, tr, cols), lambda r, ix: (ix[0], r, 0)), (rows // tr,))

    stacks = {n: cast_into_stack(n) for n in LARGE}
    dff = 4 * wts["w_ff2"].shape[1]

    c_all = _allgather8("gather_c", jnp.pad(c, ((0, SUBLANES - 1), (0, 0))))[::SUBLANES]
    n_mod = w_ada.shape[2]
    b_ada_mine = lax.dynamic_slice(b_ada, (0, chip * n_mod), (1, n_mod))
    sil_c = _single("silu_c", lambda v: [v * _sigmoid(v)], [c_all], [(c_all.shape, F32)])[0]
    mod_part = _mm("ada_fwd", sil_c, w_ada[0], "nn", F32, tn=512, epi=lambda acc, b: acc + b[0:1, :],
                   extras=[jnp.broadcast_to(b_ada_mine, (SUBLANES, n_mod))])
    mod_all = _allgather8("gather_mod", mod_part)
    me = 2 * chip + lax.axis_index("c")
    mod = jnp.concatenate([lax.dynamic_slice(mod_all, (16 * k + me, 0), (1, n_mod)) for k in range(4)], axis=1)
    sh1, sc1, g1, sh2, sc2, g2 = [mod[:, i * d : (i + 1) * d] for i in range(6)]

    (w_in_s,) = _gather_weights([stacks["w_in"]], mod_all)

    def gather_plan(refs):
        px0, py0, pc, chips = _place()
        slot = 2 * px0 + py0
        return [(r.at[slot], r.at[slot], (px, py, pc), r.at[2 * px + py]) for r in refs for (px, py) in chips]

    mix_sems_s, mix_sems_r, mix_stacks, mix_token = _split_start(
        "gather_mix_start", [stacks["w_glu"], stacks["w_out"]], gather_plan, 6, extra=[w_in_s])
    w_in_s = mix_stacks.pop()
    ff_sems_s, ff_sems_r, ff_stacks, ff_token = _split_start(
        "gather_ff_start", [stacks["w_ff1"], stacks["w_ff2"]], gather_plan, 6, extra=[w_in_s])
    w_in_s = ff_stacks.pop()
    sc1 = sc1 + (mix_token[0:1, 0:1] + ff_token[0:1, 0:1])

    def norm_mod(v, g, sc, sh):
        return [v * _rstd(v) * g * (1.0 + sc) + sh], []

    (h1b,) = _rowwise("norm1", norm_mod, [x2], [norm1_g, sc1, sh1], [(d, BF16)], [])
    proj = _mm("proj_in", h1b, w_in_s, "nn", BF16, b_cs=True, tm=FF_TM)
    assert aw == sw
    q_src, k_src, v_src, u_src = [(proj, aw, i) for i in range(3)] + [(proj, sw, 3)]

    n_heads = aw // HEAD_DIM
    up = lambda v: v.astype(F32)
    qk_orders = _stream_orders("qk_prep", lambda q, k, v, gq, gk: [up(q) * _rstd(up(q)) * gq, up(k) * _rstd(up(k)) * gk, up(v)],
                               [(proj, 0), (proj, n_heads), (proj, 2 * n_heads)], [q_norm_g, k_norm_g], 3, n_heads)

    pats = []
    for pi, (window, dil) in enumerate(DILATION_PATTERNS):
        assert window // dil == BAND and s % STREAM_ROWS == 0
        bps = s // dil // BAND
        qp, kp, vp = [order[pi] for order in qk_orders]
        o_p, l_p = _attn_fwd(f"attn_fwd_d{dil}", qp, kp, vp, bps, dil)
        pats.append((dil, bps, qp, kp, vp, o_p, _from_streams(l_p, dil)))

    def lse_total(l1, l2, l3):
        m = jnp.maximum(jnp.maximum(l1, l2), l3)
        return m + jnp.log(jnp.exp(l1 - m) + jnp.exp(l2 - m) + jnp.exp(l3 - m))

    (lse,) = _rowwise("attn_lse", lambda l1, l2, l3: ([lse_total(l1, l2, l3)], []), [p[6] for p in pats], [], [(HEAD_DIM, F32)], [])

    def combine(h, blocks, l1, l2, l3):
        tot = lse_total(l1, l2, l3)
        lane = lax.broadcasted_iota(jnp.int32, l1.shape, 1)
        mix = 0.0
        for o_blk, l in zip(blocks[0], (l1, l2, l3)):
            w = jnp.sum(jnp.where(lane == h, jnp.exp(l - tot), 0.0), axis=1, keepdims=True)
            mix = mix + w * o_blk
        return [mix]

    (attn,) = _natural_order("attn_mix", combine, [[p[5] for p in pats]], [p[6] for p in pats], [F32], n_heads)

    lam_re2, lam_im2, log_step2 = lam_re[0], lam_im[0], log_step[0].reshape(n_groups, 1)
    prep = _single("ssm_prep", _ssm_prep_fn, [lam_re2, lam_im2, log_step2], [((n_groups, STATE_DIM), F32)] * 18)
    p_r = jnp.stack([p.reshape(nst) for p in prep[:8]])
    p_i = jnp.stack([p.reshape(nst) for p in prep[8:16]])
    f_ri = jnp.stack(prep[16:])
    row = jnp.arange(SUBLANES)[:, None]
    zero = jnp.zeros((SUBLANES, nst), F32)
    sel = lambda cond, v: jnp.where(cond, jnp.broadcast_to(v[None, :], (SUBLANES, nst)), zero)
    fwd_consts = jnp.stack([sel(row >= 1, p_r[0]), sel(row >= 1, p_i[0]), sel(row >= 2, p_r[1]), sel(row >= 2, p_i[1]),
                            sel(row >= 4, p_r[3]), sel(row >= 4, p_i[3]), p_r, p_i])
    bwd_consts = jnp.stack([sel(row <= 6, p_r[0]), sel(row <= 6, p_i[0]), sel(row <= 5, p_r[1]), sel(row <= 5, p_i[1]),
                            sel(row <= 3, p_r[3]), sel(row <= 3, p_i[3]), p_r[::-1], p_i[::-1]])
    nblk = n_groups // GROUPS_PER_BLOCK
    f_rows = f_ri.reshape(2, nblk, 1, 512)
    braw_r = _block_diag(b_re[0].transpose(0, 2, 1), True)
    braw_i = _block_diag(b_im[0].transpose(0, 2, 1), True)

    craw_r, craw_i = _block_diag(c_re[0], True), _block_diag(c_im[0], True)
    a_rows = jnp.stack([p_r[0], p_i[0]]).reshape(2, nblk, 1, 512)

    def bbar_fn(br, bi, f, cr, ci, a):
        r, i = _cmul(f[0], f[1], br, bi)
        ar, ai = _cmul(a[0], a[1], r, i)
        acr, aci = _cmul(a[0], a[1], cr, ci)
        return [r, i, ar, ai, acr, aci]

    bbar_r, bbar_i, abbar_r, abbar_i, ac_r, ac_i = _single(
        "ssm_bbar", bbar_fn, [braw_r, braw_i, f_rows, craw_r, craw_i, a_rows], [(braw_r.shape, BF16)] * 6)
    ct_r, ct_i = craw_r.astype(BF16), craw_i.astype(BF16)
    ds2 = d_skip
    y_ssm, h_r, h_i = _ssm_fwd(u_src, bbar_r, bbar_i, abbar_r, abbar_i, ct_r.transpose(0, 2, 1), ct_i.transpose(0, 2, 1), fwd_consts, ds2)

    (yb,) = _rowwise("gelu", lambda v: ([_gelu(v)], []), [y_ssm], [], [(sw, BF16)], [])
    w_glu_s, w_out_s = _split_wait("gather_mix_wait", mix_stacks, mix_sems_s, mix_sems_r, yb, gather_plan)
    w_glu_f = w_glu_s.reshape(sw, sw)
    w_out_f = w_out_s.reshape(d, d)
    z = _mm("glu_fwd", yb, w_glu_f, "nn", BF16)

    def glu(ys, zz, b):
        return [_gelu(ys) * _sigmoid(zz.astype(F32) + b)], []

    (ssm,) = _rowwise("glu", glu, [y_ssm, z], [b_glu], [(sw, F32)], [])

    def cat_norm(a, sm, ga, gs):
        return [jnp.concatenate([a * _rstd(a) * ga, sm * _rstd(sm) * gs], axis=1)], []

    (catb,) = _rowwise("cat_norm", cat_norm, [attn, ssm], [attn_out_g, ssm_out_g], [(d, BF16)], [])
    mixed = _mm("mix_out", catb, w_out_f, "nn", BF16, tm=FF_TM)

    def resid_norm(xv, mx, g1v, g, sc, sh):
        x1v = xv + g1v * mx.astype(F32)
        return [x1v, x1v * _rstd(x1v) * g * (1.0 + sc) + sh], []

    x1, h2b = _rowwise("resid_norm2", resid_norm, [x2, mixed], [g1, norm2_g, sc2, sh2], [(d, F32), (d, BF16)], [])
    w_ff1_s, w_ff2_s = _split_wait("gather_ff_wait", ff_stacks, ff_sems_s, ff_sems_r, x1, gather_plan)
    w_ff2_f = w_ff2_s.reshape(dff, d)
    a_ff = _mm("ff1", h2b, w_ff1_s, "nn", BF16, b_cs=True, tm=FF_TM, tn=FF_TN)
    sq_relu = lambda v: jnp.square(jnp.maximum(v.astype(F32), 0.0)).astype(BF16)
    ff = _mm("ff2", a_ff, w_ff2_f, "nn", BF16, a_pro=sq_relu, tm=FF_TM)

    def loss_fn(x1v, ffv, tg, g2v):
        ffv = ffv.astype(F32)
        diff = x1v + g2v * ffv - tg
        dout = diff * (1.0 / d)
        return [dout, dout * g2v], [_colsum8(0.5 * diff * dout), _colsum8(dout * ffv)]

    dout, dffb, loss_acc, dg2_acc = _rowwise("loss", loss_fn, [x1, ff, tgt], [g2], [(d, F32), (d, BF16)], [(SUBLANES, d)] * 2)

    da_ff = _mm("ff2_bwd", dffb, w_ff2_f, "nt", BF16, epi=lambda acc, av: acc * (2.0 * jnp.maximum(av.astype(F32), 0.0)), extras=[a_ff], tm=FF_TM, tn=FF_TN)
    gw = {}
    gw["w_ff2"] = _mm("ff2_wgrad", a_ff, dffb, "tn", BF16, a_pro=sq_relu, tm=FF_TM).reshape(4, dff // 4, d)
    gw["w_ff1"] = _mm("ff1_wgrad", h2b, da_ff, "tn", BF16, o_cs=True, tm=FF_TM)

    def add2(n, g, b):
        _, half, cols = b.shape
        tr = _blk(half, 256)
        nrb = half // tr
        blk = (None, tr, cols)
        return _indexed("grad_add2_" + n, lambda p, q: p.astype(F32) + q.astype(F32), core_idx,
                        [(g, blk, lambda k, r, ix: (k, ix[0] * nrb + r, 0)), (b, blk, lambda k, r, ix: (k, r, 0))],
                        ((b.shape, BF16), blk, lambda k, r, ix: (k, r, 0)), (4, nrb))

    def scatter_plan(refs):
        px0, py0, pc, chips = _place()
        nw = len(refs) // 2
        return [(refs[w].at[2 * px + py], refs[nw + w].at[j], (px, py, pc), refs[nw + w].at[j])
                for w in range(nw) for j, (px, py) in enumerate(chips)]

    def swap_plan(refs):
        px0, py0, pc, _ = _place()
        nw = len(refs) // 2
        plan = []
        for w in range(nw):
            half = refs[w].shape[1] // 2
            plan.append((refs[w].at[:, pl.ds((1 - pc) * half, half), :], refs[nw + w], (px0, py0, 1 - pc), refs[nw + w]))
        return plan

    early = ["w_ff2", "w_ff1"]
    sw_lands = [lax.empty((4, gw[n].shape[1] // 2, gw[n].shape[2]), BF16) for n in early]
    sw_sems_s, sw_sems_r, sw_bufs, sw_token = _split_start("swap_ff_start", [gw[n] for n in early] + sw_lands, swap_plan, len(early))
    sc2_late = sc2 + sw_token[0:1, 0:1]

    dh2 = _mm("ff1_bwd", da_ff, w_ff1_s, "nt", BF16, b_cs=True, tm=FF_TM)

    def norm2_bwd(dh, x1v, do, mx, g, sc, g1v):
        dh, mx = dh.astype(F32), mx.astype(F32)
        r = _rstd(x1v)
        xh = x1v * r
        dn = dh * (1.0 + sc)
        dx = do + _rms_bwd(dn * g, xh, r)
        return [dx, dx * g1v], [_colsum8(dh * xh * g), _colsum8(dh), _colsum8(dn * xh), _colsum8(dx * mx)]

    dx1, dmixb, dsc2_acc, dsh2_acc, dn2g_acc, dg1_acc = _rowwise(
        "norm2_bwd", norm2_bwd, [dh2, x1, dout, mixed], [norm2_g, sc2_late, g1], [(d, F32), (d, BF16)], [(SUBLANES, d)] * 4)

    sw_bufs = _split_wait("swap_ff_wait", sw_bufs, sw_sems_s, sw_sems_r, dmixb, swap_plan)
    parts = {n: add2(n, g, b) for n, g, b in zip(early, sw_bufs[: len(early)], sw_bufs[len(early) :])}
    lands = [lax.empty((3,) + parts[n].shape[1:], BF16) for n in early]
    sc_sems_s, sc_sems_r, sc_bufs, sc_token = _split_start("scatter_ff_start", [parts[n] for n in early] + lands, scatter_plan, 6)
    attn_out_g_late = attn_out_g + sc_token[0:1, 0:1]

    dcat = _mm("mix_out_bwd", dmixb, w_out_f, "nt", BF16, tm=FF_TM)
    gw["w_out"] = _mm("mix_out_wgrad", catb, dmixb, "tn", BF16, tm=FF_TM).reshape(4, d // 4, d)

    def cat_bwd(da, dsm, a, sm, ga, gs):
        da, dsm = da.astype(F32), dsm.astype(F32)
        ra, rs = _rstd(a), _rstd(sm)
        ah, sh = a * ra, sm * rs
        return [_rms_bwd(da * ga, ah, ra), _rms_bwd(dsm * gs, sh, rs)], [_colsum8(da * ah), _colsum8(dsm * sh)]

    dattn, dssm, dga_acc, dgs_acc = _rowwise(
        "cat_norm_bwd", cat_bwd, [(dcat, aw, 0), (dcat, sw, 1), attn, ssm], [attn_out_g_late, ssm_out_g], [(aw, F32), (sw, F32)],
        [(SUBLANES, aw), (SUBLANES, sw)])

    def glu_bwd(dsm, ys, zz, b):
        sg = _sigmoid(zz.astype(F32) + b)
        dz = dsm * _gelu(ys) * sg * (1.0 - sg)
        return [dz, dsm * sg], [_colsum8(dz)]

    dzb, dy_direct, dbglu_acc = _rowwise("glu_bwd", glu_bwd, [dssm, y_ssm, z], [b_glu], [(sw, BF16), (sw, BF16)], [(SUBLANES, sw)])
    gw["w_glu"] = _mm("glu_wgrad", yb, dzb, "tn", BF16).reshape(4, sw // 4, sw)
    dys = _mm("glu_bwd_mm", dzb, w_glu_f, "nt", BF16, epi=lambda acc, direct, ys: (acc + direct.astype(F32)) * _gelu_grad(ys), extras=[dy_direct, y_ssm])
    du, db_bar, dc_bd, da_acc, dds_acc = _ssm_bwd(dys, u_src, h_r, h_i, ct_r, ct_i, ac_r, ac_i, bbar_r.transpose(0, 2, 1),
                                                  bbar_i.transpose(0, 2, 1), bwd_consts, ds2)

    def delta_fn(da, a):
        return [_to_lanes([jnp.sum(p * q, axis=1, keepdims=True) for p, q in zip(_heads(da), _heads(a))])], []

    (delta,) = _rowwise("attn_delta", delta_fn, [dattn, attn], [], [(HEAD_DIM, F32)], [])
    (do_orders,) = _stream_orders("attn_do", lambda da: [da], [(dattn, 0)], [], 1, n_heads)
    dqs, dks, dvs = [], [], []
    for pi, (dil, bps, qp, kp, vp, _, _) in enumerate(pats):
        lsp, dlp = [_to_streams(t, dil) for t in (lse, delta)]
        dq_p, dk_p, dv_p = _attn_bwd(f"attn_bwd_d{dil}", qp, kp, vp, do_orders[pi], lsp, dlp, bps, dil)
        dqs.append(dq_p)
        dks.append(dk_p)
        dvs.append(dv_p)
    dq_t, dk_t, dv_t = _natural_order("attn_grad_sum", lambda h, blocks: [(b[0] + b[1]) + b[2] for b in blocks],
                                      [dqs, dks, dvs], [], [BF16] * 3, n_heads)

    def qk_bwd(dq_sum, dk_sum, dv_sum, duv, q, k, gq, gk):
        def one(dn, raw, g):
            outs, acc = [], 0.0
            for dh, h in zip(_heads(dn), _heads(raw)):
                r = _rstd(h)
                hh = h * r
                outs.append(_rms_bwd(dh * g, hh, r))
                acc = acc + _colsum8(dh * hh)
            return jnp.concatenate(outs, axis=1), acc

        dq, gq_acc = one(dq_sum.astype(F32), q.astype(F32), gq)
        dk, gk_acc = one(dk_sum.astype(F32), k.astype(F32), gk)
        return [jnp.concatenate([dq, dk, dv_sum.astype(F32), duv.astype(F32)], axis=1)], [gq_acc, gk_acc]

    dprojb, dgq_acc, dgk_acc = _rowwise("qk_norm_bwd", qk_bwd, [dq_t, dk_t, dv_t, du, q_src, k_src], [q_norm_g, k_norm_g],
                                        [(3 * aw + sw, BF16)], [(SUBLANES, HEAD_DIM)] * 2)
    gw["w_in"] = _mm("proj_in_wgrad", h1b, dprojb, "tn", BF16, o_cs=True, tm=FF_TM)
    late = [n for n in LARGE if n not in early]
    parts.update({n: add2(n, gw[n], b) for n, b in zip(late, _swap_halves([gw[n] for n in late], "swap_halves_rest"))})
    lands_late = [lax.empty((3,) + parts[n].shape[1:], BF16) for n in late]
    rest_sems_s, rest_sems_r, rest_bufs, rest_token = _split_start(
        "scatter_rest_start", [parts[n] for n in late] + lands_late, scatter_plan, 3 * len(late))
    sc1_late = sc1 + rest_token[0:1, 0:1]
    dh1 = _mm("proj_in_bwd", dprojb, w_in_s, "nt", BF16, b_cs=True, tm=FF_TM)

    def norm1_bwd(dh, xv, dxr, g, sc):
        dh = dh.astype(F32)
        r = _rstd(xv)
        xh = xv * r
        dn = dh * (1.0 + sc)
        return [dxr + _rms_bwd(dn * g, xh, r)], [_colsum8(dh * xh * g), _colsum8(dh), _colsum8(dn * xh)]

    grad_x, dsc1_acc, dsh1_acc, dn1g_acc = _rowwise("norm1_bwd", norm1_bwd, [dh1, x2, dx1], [norm1_g, sc1_late], [(d, F32)], [(SUBLANES, d)] * 3)

    def b_grad_fn(dbb, br, bi, f):
        fr, fi = f[0], f[1]
        dbr, dbi = _cmul_conj(fr, fi, dbb[0], dbb[1])
        dfr = jnp.sum(dbb[0] * br + dbb[1] * bi, axis=1, keepdims=True)
        dfi = jnp.sum(dbb[1] * br - dbb[0] * bi, axis=1, keepdims=True)
        return [dbr, dbi, dfr, dfi]

    dbr_bd, dbi_bd, df_r, df_i = _single("ssm_b_grad", b_grad_fn, [db_bar, braw_r, braw_i, f_rows],
                                         [(braw_r.shape, F32)] * 2 + [((nblk, 1, 512), F32)] * 2)

    def sums_fn(*accs):
        return [jnp.sum(a, axis=-2, keepdims=True) for a in accs]

    acc_list = [dsh1_acc, dsc1_acc, dg1_acc, dsh2_acc, dsc2_acc, dg2_acc, dn1g_acc, dgq_acc, dgk_acc, dds_acc, dbglu_acc, dga_acc,
                dgs_acc, dn2g_acc, loss_acc, da_acc]
    sums = _single("small_sums", sums_fn, acc_list, [(a.shape[:-2] + (1, a.shape[-1]), F32) for a in acc_list])
    (dsh1, dsc1, dg1, dsh2, dsc2, dg2, dn1g, dgq, dgk, dds, dbglu, dga, dgs, dn2g, loss_row, da_sum) = sums
    loss_dev = _single("loss_sum", lambda v: [jnp.sum(v, axis=1, keepdims=True)], [loss_row], [((1, 1), F32)])[0]
    gp = (n_groups, STATE_DIM)
    dlam_re, dlam_im, dlog_step = _single(
        "ssm_param_grad", _ssm_param_grad_fn,
        [lam_re2, lam_im2, log_step2, da_sum[0].reshape(gp), da_sum[1].reshape(gp), df_r.reshape(gp), df_i.reshape(gp)],
        [(gp, F32), (gp, F32), ((n_groups, 1), F32)])
    small_dev = {
        "b_ada": jnp.concatenate([dsh1, dsc1, dg1, dsh2, dsc2, dg2], axis=1),
        "norm1_g": dn1g, "q_norm_g": dgq, "k_norm_g": dgk, "lam_re": dlam_re, "lam_im": dlam_im, "log_step": dlog_step,
        "b_re": _block_diag_take(dbr_bd, SSM_GROUP, STATE_DIM).transpose(0, 2, 1),
        "b_im": _block_diag_take(dbi_bd, SSM_GROUP, STATE_DIM).transpose(0, 2, 1),
        "c_re": _block_diag_take(dc_bd[0], SSM_GROUP, STATE_DIM), "c_im": _block_diag_take(dc_bd[1], SSM_GROUP, STATE_DIM),
        "d_skip": dds, "b_glu": dbglu, "attn_out_g": dga, "ssm_out_g": dgs, "norm2_g": dn2g,
    }

    packed = _pack([small_dev[n] for n in SMALL])
    rows = packed.shape[0]
    tr = _blk(rows, 256)
    me_idx = jnp.reshape(me, (1,)).astype(jnp.int32)
    gathered = _indexed("pack_small", lambda a: a, me_idx, [(packed, (tr, PACK_LANES), lambda r, ix: (r, 0))],
                        (((8, rows, PACK_LANES), F32), (None, tr, PACK_LANES), lambda r, ix: (ix[0], r, 0)), (rows // tr,))

    def small_plan(refs):
        (g,) = refs
        px0, py0, pc0, _ = _place()
        flip = lambda v, bit: 1 - v if bit else v
        plan = []
        for mask in range(1, 8):
            px, py, pc = flip(px0, mask & 4), flip(py0, mask & 2), flip(pc0, mask & 1)
            mine = g.at[4 * px0 + 2 * py0 + pc0]
            plan.append((mine, mine, (px, py, pc), g.at[4 * px + 2 * py + pc]))
        return plan

    small_sems_s, small_sems_r, (gathered,), small_token = _split_start("gather_small_start", [gathered], small_plan, 7)

    sc_bufs = _split_wait("scatter_ff_wait", sc_bufs, sc_sems_s, sc_sems_r, small_token, scatter_plan)
    parts.update(zip(early, sc_bufs[: len(early)]))
    got2 = dict(zip(early, sc_bufs[len(early) :]))
    rest_bufs = _split_wait("scatter_rest_wait", rest_bufs, rest_sems_s, rest_sems_r, small_token, scatter_plan)
    parts.update(zip(late, rest_bufs[: len(late)]))
    got2.update(zip(late, rest_bufs[len(late) :]))
    place_idx = jnp.stack([chip, lax.axis_index("c")]).astype(jnp.int32)

    def add4(n, p, b):
        _, half, cols = p.shape
        tr = _blk(half, 256)
        blk = (None, tr, cols)
        fn = lambda own, q0, q1, q2: ((own.astype(F32) + q0.astype(F32)) + q1.astype(F32)) + q2.astype(F32)
        srcs = [(p, blk, lambda r, ix: (ix[0], r, 0))] + [(b, blk, lambda r, ix, j=j: (j, r, 0)) for j in range(3)]
        return _indexed("grad_add4_" + n, fn, place_idx, srcs, (((2, half, cols), F32), blk, lambda r, ix: (ix[1], r, 0)), (half // tr,))

    def join_plan(refs):
        px0, py0, pc, _ = _place()
        return [(r.at[pc], r.at[pc], (px0, py0, 1 - pc), r.at[1 - pc]) for r in refs]

    out = {}

    def update_large(names, fulls):
        last = None
        for n, full in zip(names, fulls):
            g = full.reshape(wts[n].shape[1:])
            last = _adamw_rows("adamw_" + n, wts[n][0], g, mom[n][0], var[n][0])
            for kind, val in zip(("grad", "delta", "new_m", "new_v"), (g,) + tuple(last)):
                out[kind, n] = val[None]
        return last[0]

    rest = [n for n in LARGE if n not in early]
    ja_s, ja_r, fulls_a, _ = _split_start("join_ff_start", [add4(n, parts[n], got2[n]) for n in early], join_plan, len(early))
    jb_s, jb_r, fulls_b, jb_token = _split_start("join_rest_start", [add4(n, parts[n], got2[n]) for n in rest], join_plan, len(rest))
    done_a = update_large(early, _split_wait("join_ff_wait", fulls_a, ja_s, ja_r, jb_token, join_plan))
    done_b = update_large(rest, _split_wait("join_rest_wait", fulls_b, jb_s, jb_r, done_a, join_plan))
    (gathered,) = _split_wait("gather_small_wait", [gathered], small_sems_s, small_sems_r, done_b, small_plan)

    def sum8_fn(g):
        acc = g[0]
        for i in range(1, 8):
            acc = acc + g[i]
        return [acc]

    g_small = _single("sum_small", sum8_fn, [gathered], [((rows, PACK_LANES), F32)])[0]

    w_small, m_small, v_small = [_pack([src[n] for n in SMALL]) for src in (wts, mom, var)]
    upd_small = _adamw_rows("adamw_small", w_small, g_small, m_small, v_small)
    shapes = [wts[n].shape for n in SMALL]
    for kind, arr in zip(("grad", "delta", "new_m", "new_v"), (g_small,) + tuple(upd_small)):
        for n, val in zip(SMALL, _unpack(arr, shapes)):
            out[kind, n] = val

    dmod_all = gathered[:, 0 : -(-6 * d // PACK_LANES), :].reshape(8, -1)[:, : 6 * d]
    dmod_mine = lax.dynamic_slice(dmod_all, (0, chip * n_mod), (8, n_mod))
    g_ada = _mm("ada_wgrad", sil_c, dmod_mine, "tn", F32, tm=512, tn=512)
    upd = _adamw_rows("adamw_w_ada", w_ada[0], g_ada, m_w_ada[0], v_w_ada[0])
    for kind, val in zip(("grad", "delta", "new_m", "new_v"), (g_ada,) + tuple(upd)):
        out[kind, "w_ada"] = val[None]

    loss = lax.psum(loss_dev[0, 0], ("x", "y", "c"))
    return (loss, grad_x[None], *[out[kind, n] for kind in ("grad", "delta", "new_m", "new_v") for n in ORDER])
```

```python
import functools
import math

import jax
import jax.numpy as jnp
import numpy as np
from jax import lax
from jax.experimental import pallas as pl
from jax.experimental.pallas import tpu as pltpu

F32 = jnp.float32
BF16 = jnp.bfloat16
MESH = pl.DeviceIdType.MESH

EPS = 1e-6
HEAD_DIM = 128
SSM_GROUP = 16
STATE_DIM = 64
GROUPS_PER_BLOCK = 8
DILATION_PATTERNS = ((128, 1), (512, 4), (2048, 16))
BAND = 128
NEG = -1e30

ADAM_LR, ADAM_B1, ADAM_B2, ADAM_EPS, ADAM_WD, ADAM_STEP = 0.001, 0.9, 0.999, 1e-08, 0.01, 10

V7X_VMEM_BYTES = 64 * 2**20
VMEM_LIMIT = 48 * 2**20
SUBLANES = 8
PACK_LANES = 1024


def _cparams(sem=None):
    return pltpu.CompilerParams(dimension_semantics=sem, vmem_limit_bytes=VMEM_LIMIT)


def _blk(n, want):
    b = min(n, want)
    while n % b:
        b //= 2
    return b


def _colsum8(v):
    tm, w = v.shape
    return v.reshape(tm // SUBLANES, SUBLANES, w).sum(axis=0)


def _rowwise(name, fn, rows, vecs, out_rows, out_accs, tm=256, n_rows=None):
    rows = [r if isinstance(r, tuple) else (r, r.shape[1], 0) for r in rows]
    rows = [r if len(r) == 4 else r + (0,) for r in rows]
    s = rows[0][0].shape[0] if n_rows is None else n_rows
    tm = _blk(s, tm)
    assert all(r[3] % tm == 0 for r in rows)
    n_in, n_or = len(rows) + len(vecs), len(out_rows)

    def body(*refs):
        outs, accs = fn(*[r[...] for r in refs[:n_in]])
        for r, v in zip(refs[n_in : n_in + n_or], outs):
            r[...] = v.astype(r.dtype)
        if out_accs:
            acc_refs = refs[n_in + n_or :]

            @pl.when(pl.program_id(0) == 0)
            def _():
                for r in acc_refs:
                    r[...] = jnp.zeros(r.shape, r.dtype)

            for r, v in zip(acc_refs, accs):
                r[...] += v

    in_specs = [pl.BlockSpec((tm, w), lambda i, cb=cb, rb=r0 // tm: (i + rb, cb)) for (_, w, cb, r0) in rows]
    in_specs += [pl.BlockSpec(v.shape, lambda i, nd=v.ndim: (0,) * nd) for v in vecs]
    out_specs = [pl.BlockSpec((tm, w), lambda i: (i, 0)) for (w, _) in out_rows]
    out_specs += [pl.BlockSpec(sh, lambda i, nd=len(sh): (0,) * nd) for sh in out_accs]
    out_shape = [jax.ShapeDtypeStruct((s, w), dt) for (w, dt) in out_rows]
    out_shape += [jax.ShapeDtypeStruct(sh, F32) for sh in out_accs]
    res = pl.pallas_call(
        body, name=name, grid=(s // tm,), in_specs=in_specs, out_specs=out_specs, out_shape=out_shape,
        compiler_params=_cparams(("arbitrary",)),
    )(*[r[0] for r in rows], *vecs)
    return res


def _single(name, fn, ins, out_shapes):
    n_in = len(ins)

    def body(*refs):
        outs = fn(*[r[...] for r in refs[:n_in]])
        for r, v in zip(refs[n_in:], outs):
            r[...] = v.astype(r.dtype)

    vm = pl.BlockSpec(memory_space=pltpu.VMEM)
    return pl.pallas_call(
        body, name=name, in_specs=[vm] * n_in, out_specs=[vm] * len(out_shapes),
        out_shape=[jax.ShapeDtypeStruct(sh, dt) for sh, dt in out_shapes], compiler_params=_cparams(),
    )(*ins)


_NN = (((1,), (0,)), ((), ()))
_NT = (((1,), (1,)), ((), ()))
_TN = (((0,), (0,)), ((), ()))


def _mm(name, a, b, mode, out_dtype, *, tm=512, tn=1024, tk=2048, b_cs=False, o_cs=False, epi=None, extras=(), a_pro=None):
    if mode == "tn":
        k, m = a.shape
        n = b.shape[1]
    else:
        m, k = a.shape
        if mode == "nn":
            n = b.shape[2] * 4 if b_cs else b.shape[1]
        else:
            n = b.shape[1] if b_cs else b.shape[0]
    tm = _blk(m, tm)
    tn = _blk(n // 4 if (b_cs and mode == "nn") or o_cs else n, tn)
    tk = _blk(k // 4 if (b_cs and mode == "nt") else k, tk)
    nk = k // tk
    dims = {"nn": _NN, "nt": _NT, "tn": _TN}[mode]
    n_ex = len(extras)

    def body(*refs):
        a_ref, b_ref = refs[0], refs[1]
        ex_refs, o_ref = refs[2 : 2 + n_ex], refs[2 + n_ex]

        def finish(acc):
            if epi is not None:
                acc = epi(acc, *[r[...] for r in ex_refs])
            o_ref[...] = acc.astype(o_ref.dtype)

        a_blk = a_ref[...] if a_pro is None else a_pro(a_ref[...])
        part = lax.dot_general(a_blk, b_ref[...], dims, preferred_element_type=F32)
        if nk == 1:
            finish(part)
        else:
            acc_ref = refs[3 + n_ex]
            kk = pl.program_id(2)

            @pl.when(kk == 0)
            def _():
                acc_ref[...] = part

            @pl.when(kk > 0)
            def _():
                acc_ref[...] += part

            @pl.when(kk == nk - 1)
            def _():
                finish(acc_ref[...])

    a_spec = pl.BlockSpec((tk, tm), lambda i, j, kk: (kk, i)) if mode == "tn" else pl.BlockSpec((tm, tk), lambda i, j, kk: (i, kk))
    if mode == "nn":
        if b_cs:
            per = (n // 4) // tn
            b_spec = pl.BlockSpec((None, tk, tn), lambda i, j, kk: (j // per, kk, j % per))
        else:
            b_spec = pl.BlockSpec((tk, tn), lambda i, j, kk: (kk, j))
    elif mode == "nt":
        if b_cs:
            per = (k // 4) // tk
            b_spec = pl.BlockSpec((None, tn, tk), lambda i, j, kk: (kk // per, j, kk % per))
        else:
            b_spec = pl.BlockSpec((tn, tk), lambda i, j, kk: (j, kk))
    else:
        b_spec = pl.BlockSpec((tk, tn), lambda i, j, kk: (kk, j))
    if o_cs:
        per = (n // 4) // tn
        o_spec = pl.BlockSpec((None, tm, tn), lambda i, j, kk: (j // per, i, j % per))
        o_shape = jax.ShapeDtypeStruct((4, m, n // 4), out_dtype)
    else:
        o_spec = pl.BlockSpec((tm, tn), lambda i, j, kk: (i, j))
        o_shape = jax.ShapeDtypeStruct((m, n), out_dtype)
    ex_specs = [pl.BlockSpec((tm, tn), lambda i, j, kk: (i, j)) for _ in extras]
    return pl.pallas_call(
        body, name=name, grid=(m // tm, n // tn, nk), in_specs=[a_spec, b_spec] + ex_specs, out_specs=o_spec, out_shape=o_shape,
        scratch_shapes=[pltpu.VMEM((tm, tn), F32)] if nk > 1 else [],
        compiler_params=_cparams(("parallel", "parallel", "arbitrary")),
    )(a, b, *extras)


def _place():
    x, y, c = lax.axis_index("x"), lax.axis_index("y"), lax.axis_index("c")
    return x, y, c, [(1 - x, y), (x, 1 - y), (1 - x, 1 - y)]


def _allgather8(name, v):
    m_per, n = v.shape

    def body(x_ref, out_ref, send_sems, recv_sems, local_sem):
        x, y, c, chips = _place()
        me, sibling = (x, y, c), (x, y, 1 - c)

        def rows(px, py, pc):
            return out_ref.at[pl.ds((4 * px + 2 * py + pc) * m_per, m_per), :]

        def copy(k, block, to, src=None):
            return pltpu.make_async_remote_copy(
                src_ref=rows(*block) if src is None else src, dst_ref=rows(*block), send_sem=send_sems.at[k],
                recv_sem=recv_sems.at[k], device_id=to, device_id_type=MESH)

        mine = pltpu.make_async_copy(x_ref, rows(*me), local_sem)
        mine.start()
        first = [copy(0, me, sibling, src=x_ref)]
        first += [copy(1 + j, me, (*chip, c), src=x_ref) for j, chip in enumerate(chips)]
        for cp in first:
            cp.start()
        passed = [copy(4 + j, (*chip, c), sibling) for j, chip in enumerate(chips)]
        for j, chip in enumerate(chips):
            copy(1 + j, (*chip, c), me).wait_recv()
            passed[j].start()
        copy(0, sibling, me).wait_recv()
        for j, chip in enumerate(chips):
            copy(4 + j, (*chip, 1 - c), me).wait_recv()
        for cp in first + passed:
            cp.wait_send()
        mine.wait()

    return pl.pallas_call(
        body, name=name, out_shape=jax.ShapeDtypeStruct((8 * m_per, n), v.dtype),
        in_specs=[pl.BlockSpec(memory_space=pltpu.VMEM)], out_specs=pl.BlockSpec(memory_space=pltpu.VMEM),
        scratch_shapes=[pltpu.SemaphoreType.DMA((7,)), pltpu.SemaphoreType.DMA((7,)), pltpu.SemaphoreType.DMA],
        compiler_params=_cparams(),
    )(v)


def _gather_weights(stacks, after):
    nw = len(stacks)

    def body(*refs):
        outs = refs[nw + 1 : 2 * nw + 1]
        send_sems, recv_sems = refs[2 * nw + 1 :]
        x, y, c, chips = _place()
        me, sibling = (x, y, c), (x, y, 1 - c)

        def copy(w, k, block, to):
            px, py, pc = block
            half = outs[w].shape[1] // 2
            rows = outs[w].at[2 * px + py, pl.ds(pc * half, half), :]
            return pltpu.make_async_remote_copy(src_ref=rows, dst_ref=rows, send_sem=send_sems.at[6 * w + k],
                                                recv_sem=recv_sems.at[6 * w + k], device_id=to, device_id_type=MESH)

        sent = [copy(w, j, me, (*chip, c)) for w in range(nw) for j, chip in enumerate(chips)]
        for cp in sent:
            cp.start()
        for w in range(nw):
            for j, chip in enumerate(chips):
                copy(w, j, (*chip, c), me).wait_recv()
                passed = copy(w, 3 + j, (*chip, c), sibling)
                passed.start()
                sent.append(passed)
        for w in range(nw):
            for j, chip in enumerate(chips):
                copy(w, 3 + j, (*chip, 1 - c), me).wait_recv()
        for cp in sent:
            cp.wait_send()

    hbm = pl.BlockSpec(memory_space=pl.ANY)
    return pl.pallas_call(
        body, name="gather_weights", out_shape=[jax.ShapeDtypeStruct(w.shape, w.dtype) for w in stacks],
        in_specs=[hbm] * (nw + 1), out_specs=[hbm] * nw, input_output_aliases={i: i for i in range(nw)},
        scratch_shapes=[pltpu.SemaphoreType.DMA((6 * nw,)), pltpu.SemaphoreType.DMA((6 * nw,))],
        compiler_params=_cparams(),
    )(*stacks, after)


def _swap_halves(gs, name):
    nw = len(gs)

    def body(*refs):
        ins, gots = refs[:nw], refs[nw : 2 * nw]
        send_sems, recv_sems = refs[2 * nw :]
        x, y, c, _ = _place()
        cps = []
        for w in range(nw):
            half = ins[w].shape[1] // 2
            cps.append(pltpu.make_async_remote_copy(
                src_ref=ins[w].at[:, pl.ds((1 - c) * half, half), :], dst_ref=gots[w], send_sem=send_sems.at[w],
                recv_sem=recv_sems.at[w], device_id=(x, y, 1 - c), device_id_type=MESH))
            cps[-1].start()
        for cp in cps:
            cp.wait()

    hbm = pl.BlockSpec(memory_space=pl.ANY)
    return pl.pallas_call(
        body, name=name, out_shape=[jax.ShapeDtypeStruct((4, g.shape[1] // 2, g.shape[2]), g.dtype) for g in gs],
        in_specs=[hbm] * nw, out_specs=[hbm] * nw,
        scratch_shapes=[pltpu.SemaphoreType.DMA((nw,)), pltpu.SemaphoreType.DMA((nw,))], compiler_params=_cparams(),
    )(*gs)


def _split_start(name, bufs, plan, n, extra=()):
    n_plan = len(bufs)
    bufs = list(bufs) + list(extra)
    nb = len(bufs)

    def body(*refs):
        send_sems, recv_sems, token = refs[nb], refs[nb + 1], refs[-1]
        for k, (src, dst, to, _) in enumerate(plan(refs[:n_plan])):
            pltpu.make_async_remote_copy(src_ref=src, dst_ref=dst, send_sem=send_sems.at[k], recv_sem=recv_sems.at[k],
                                         device_id=to, device_id_type=MESH).start()
        token[...] = jnp.zeros(token.shape, token.dtype)

    hbm, sem = pl.BlockSpec(memory_space=pltpu.HBM), pl.BlockSpec(memory_space=pltpu.SEMAPHORE)
    res = pl.pallas_call(
        body, name=name,
        out_shape=(pltpu.SemaphoreType.DMA((n,)), pltpu.SemaphoreType.DMA((n,)), *[pltpu.HBM(b.shape, b.dtype) for b in bufs],
                   jax.ShapeDtypeStruct((SUBLANES, 128), F32)),
        in_specs=(hbm,) * nb, out_specs=(sem, sem) + (hbm,) * nb + (pl.BlockSpec(memory_space=pltpu.VMEM),),
        input_output_aliases={i: 2 + i for i in range(nb)},
        compiler_params=pltpu.CompilerParams(has_side_effects=pltpu.SideEffectType.DATAFLOW_SIDE_EFFECTING),
    )(*[pltpu.with_memory_space_constraint(b, pltpu.HBM) for b in bufs])
    return res[0], res[1], list(res[2 : 2 + nb]), res[-1]


def _split_wait(name, bufs, send_sems, recv_sems, after, plan):
    nb = len(bufs)

    def body(*refs):
        send, recv = refs[nb], refs[nb + 1]
        for k, (src, _, to, land) in enumerate(plan(refs[:nb])):
            cp = pltpu.make_async_remote_copy(src_ref=src, dst_ref=land, send_sem=send.at[k], recv_sem=recv.at[k],
                                              device_id=to, device_id_type=MESH)
            cp.wait_send()
            cp.wait_recv()

    hbm, sem = pl.BlockSpec(memory_space=pltpu.HBM), pl.BlockSpec(memory_space=pltpu.SEMAPHORE)
    return pl.pallas_call(
        body, name=name, out_shape=tuple(pltpu.HBM(b.shape, b.dtype) for b in bufs),
        in_specs=(hbm,) * nb + (sem, sem, pl.BlockSpec(memory_space=pl.ANY)), out_specs=(hbm,) * nb,
        input_output_aliases={i: i for i in range(nb)},
        compiler_params=pltpu.CompilerParams(has_side_effects=pltpu.SideEffectType.DATAFLOW_SIDE_EFFECTING),
    )(*bufs, send_sems, recv_sems, after)


def _indexed(name, fn, idx, ins, out, grid):
    def body(idx_ref, *refs):
        refs[-1][...] = fn(*[r[...] for r in refs[:-1]]).astype(refs[-1].dtype)

    (o_shape, o_dtype), o_block, o_map = out
    return pl.pallas_call(
        body, name=name, out_shape=jax.ShapeDtypeStruct(o_shape, o_dtype),
        grid_spec=pltpu.PrefetchScalarGridSpec(
            num_scalar_prefetch=1, grid=grid, in_specs=[pl.BlockSpec(blk, mp) for (_, blk, mp) in ins],
            out_specs=pl.BlockSpec(o_block, o_map)),
        compiler_params=_cparams(("arbitrary",) * len(grid)),
    )(idx, *[a for (a, _, _) in ins])


def _rstd(v):
    return lax.rsqrt(jnp.mean(v * v, axis=-1, keepdims=True) + EPS)


def _rms_bwd(dy, xh, r):
    return r * (dy - xh * jnp.mean(dy * xh, axis=-1, keepdims=True))


def _heads(v):
    return [v[:, h * HEAD_DIM : (h + 1) * HEAD_DIM] for h in range(v.shape[1] // HEAD_DIM)]


def _gelu(v):
    k = math.sqrt(2.0 / math.pi)
    return 0.5 * v * (1.0 + jnp.tanh(k * (v + 0.044715 * v * v * v)))


def _gelu_grad(v):
    k = math.sqrt(2.0 / math.pi)
    t = jnp.tanh(k * (v + 0.044715 * v * v * v))
    return 0.5 * (1.0 + t) + 0.5 * v * (1.0 - t * t) * k * (1.0 + 3 * 0.044715 * v * v)


def _sigmoid(v):
    return 1.0 / (1.0 + jnp.exp(-v))


def _slopes(n_heads):
    return [2.0 ** (-8.0 * (h + 1.0) / n_heads) for h in range(n_heads)]


def _band_iotas():
    ri = lax.broadcasted_iota(jnp.int32, (BAND, BAND), 0)
    ci = lax.broadcasted_iota(jnp.int32, (BAND, BAND), 1)
    return ri, ci


def _to_lanes(cols):
    rows = cols[0].shape[0]
    lane = lax.broadcasted_iota(jnp.int32, (rows, HEAD_DIM), 1)
    out = jnp.zeros((rows, HEAD_DIM), F32)
    for h, col in enumerate(cols):
        out = jnp.where(lane == h, col, out)
    return out


def _from_lanes(v, h, width):
    return jnp.broadcast_to(v[:, h : h + 1], (v.shape[0], width))


def _attn_fwd(name, q, k, v, bps, dil):
    s, aw = q.shape
    n_heads, nb = aw // HEAD_DIM, s // BAND
    scale = HEAD_DIM**-0.5
    slopes = _slopes(n_heads)

    assert nb % 2 == 0

    def body(q_ref, kp_ref, kc_ref, vp_ref, vc_ref, o_ref, l_ref):
        i = pl.program_id(0)
        ri, ci = _band_iotas()
        dist_c = (ri - ci).astype(F32)
        dist_p = (ri - ci + BAND).astype(F32)
        valid_c = ci <= ri
        hs = range(n_heads)
        sls = [slice(h * HEAD_DIM, (h + 1) * HEAD_DIM) for h in hs]
        lo, hi = slice(0, BAND), slice(BAND, 2 * BAND)
        chains = []
        for sub, rows in enumerate((lo, hi)):
            valid_p = ci >= ri + jnp.where((2 * i + sub) % bps == 0, BAND, 0)
            for h in hs:
                k_prev = kp_ref[:, sls[h]] if sub == 0 else kc_ref[lo, sls[h]]
                v_prev = vp_ref[:, sls[h]] if sub == 0 else vc_ref[lo, sls[h]]
                chains.append((q_ref[rows, sls[h]], kc_ref[rows, sls[h]], k_prev, vc_ref[rows, sls[h]], v_prev, h, valid_p))
        cs = range(len(chains))
        sc = [lax.dot_general(chains[c][0], chains[c][1], _NT, preferred_element_type=F32) for c in cs]
        sp = [lax.dot_general(chains[c][0], chains[c][2], _NT, preferred_element_type=F32) for c in cs]
        sc = [jnp.where(valid_c, sc[c] * scale - (slopes[chains[c][5]] * dil) * dist_c, NEG) for c in cs]
        sp = [jnp.where(chains[c][6], sp[c] * scale - (slopes[chains[c][5]] * dil) * dist_p, NEG) for c in cs]
        m = [jnp.maximum(jnp.max(sc[c], axis=1, keepdims=True), jnp.max(sp[c], axis=1, keepdims=True)) for c in cs]
        pc = [jnp.exp(sc[c] - m[c]) for c in cs]
        pp = [jnp.exp(sp[c] - m[c]) for c in cs]
        den = [jnp.sum(pc[c], axis=1, keepdims=True) + jnp.sum(pp[c], axis=1, keepdims=True) for c in cs]
        acc = [jnp.dot(pc[c].astype(BF16), chains[c][3], preferred_element_type=F32) for c in cs]
        acc = [acc[c] + jnp.dot(pp[c].astype(BF16), chains[c][4], preferred_element_type=F32) for c in cs]
        out = [jnp.concatenate([acc[c] / den[c] for c in cs[sub * n_heads : (sub + 1) * n_heads]], axis=1) for sub in range(2)]
        o_ref[...] = jnp.concatenate(out, axis=0).astype(o_ref.dtype)
        lse = [_to_lanes([m[c] + jnp.log(den[c]) for c in cs[sub * n_heads : (sub + 1) * n_heads]]) for sub in range(2)]
        l_ref[...] = jnp.concatenate(lse, axis=0)

    cur = pl.BlockSpec((2 * BAND, aw), lambda i: (i, 0))
    prev = pl.BlockSpec((BAND, aw), lambda i: (jnp.maximum(2 * i - 1, 0), 0))
    return pl.pallas_call(
        body, name=name, grid=(nb // 2,), in_specs=[cur, prev, cur, prev, cur],
        out_specs=[cur, pl.BlockSpec((2 * BAND, HEAD_DIM), lambda i: (i, 0))],
        out_shape=[jax.ShapeDtypeStruct((s, aw), BF16), jax.ShapeDtypeStruct((s, HEAD_DIM), F32)],
        compiler_params=_cparams(("arbitrary",)),
    )(q, k, k, v, v)


def _attn_bwd(name, q, k, v, do, lse, delta, bps, dil):
    s, aw = q.shape
    n_heads, nb = aw // HEAD_DIM, s // BAND
    scale = HEAD_DIM**-0.5
    slopes = _slopes(n_heads)

    assert nb % 2 == 0

    def body(qc_ref, qn_ref, doc_ref, don_ref, lc_ref, ln_ref, dc_ref, dn_ref, k_ref, v_ref, dq_ref, dk_ref, dv_ref, carry):
        i = pl.program_id(0)

        @pl.when(i == 0)
        def _():
            carry[...] = jnp.zeros(carry.shape, F32)

        ri, ci = _band_iotas()
        dist_c = (ri - ci).astype(F32)
        dist_p = (ri - ci + BAND).astype(F32)
        valid_c = ci <= ri
        valid_ab = ci >= ri + jnp.where((2 * i + 1) % bps != 0, 0, BAND)
        valid_bn = ci >= ri + jnp.where(jnp.logical_and((2 * i + 2) % bps != 0, 2 * i + 2 < nb), 0, BAND)
        carried = carry[...]
        hs = range(n_heads)
        sls = [slice(h * HEAD_DIM, (h + 1) * HEAD_DIM) for h in hs]
        lo, hi, al = slice(0, BAND), slice(BAND, 2 * BAND), slice(None)
        sides = [((qc_ref, doc_ref, lc_ref, dc_ref), lo, lo, dist_c, valid_c), ((qc_ref, doc_ref, lc_ref, dc_ref), hi, lo, dist_p, valid_ab),
                 ((qc_ref, doc_ref, lc_ref, dc_ref), hi, hi, dist_c, valid_c), ((qn_ref, don_ref, ln_ref, dn_ref), al, hi, dist_p, valid_bn)]
        res = []
        for (q_ref, do_ref, l_ref, d_ref), qr, kr, dist, valid in sides:
            qs, dos = [q_ref[qr, sl] for sl in sls], [do_ref[qr, sl] for sl in sls]
            ks, vs = [k_ref[kr, sl] for sl in sls], [v_ref[kr, sl] for sl in sls]
            sc = [lax.dot_general(qs[h], ks[h], _NT, preferred_element_type=F32) for h in hs]
            dp = [lax.dot_general(dos[h], vs[h], _NT, preferred_element_type=F32) for h in hs]
            lse_v, delta_v = l_ref[qr, :], d_ref[qr, :]
            p = [jnp.where(valid, jnp.exp(sc[h] * scale - (slopes[h] * dil) * dist - _from_lanes(lse_v, h, BAND)), 0.0) for h in hs]
            ds = [(p[h] * (dp[h] - _from_lanes(delta_v, h, BAND))).astype(BF16) for h in hs]
            pb = [p[h].astype(BF16) for h in hs]
            dv = [lax.dot_general(pb[h], dos[h], _TN, preferred_element_type=F32) for h in hs]
            dk = [lax.dot_general(ds[h], qs[h], _TN, preferred_element_type=F32) for h in hs]
            dq = [jnp.dot(ds[h], ks[h], preferred_element_type=F32) for h in hs]
            res.append((dq, dk, dv))
        (dq_aa, dk_aa, dv_aa), (dq_ba, dk_ba, dv_ba), (dq_bb, dk_bb, dv_bb), (dq_nb, dk_nb, dv_nb) = res
        cat = lambda parts: jnp.concatenate(parts, axis=1)
        dq_a = carried + cat(dq_aa) * scale
        dq_b = cat([dq_ba[h] + dq_bb[h] for h in hs]) * scale
        dq_ref[...] = jnp.concatenate([dq_a, dq_b], axis=0).astype(dq_ref.dtype)
        carry[...] = cat(dq_nb) * scale
        dk_a, dk_b = cat([dk_aa[h] + dk_ba[h] for h in hs]), cat([dk_bb[h] + dk_nb[h] for h in hs])
        dk_ref[...] = (jnp.concatenate([dk_a, dk_b], axis=0) * scale).astype(dk_ref.dtype)
        dv_a, dv_b = cat([dv_aa[h] + dv_ba[h] for h in hs]), cat([dv_bb[h] + dv_nb[h] for h in hs])
        dv_ref[...] = jnp.concatenate([dv_a, dv_b], axis=0).astype(dv_ref.dtype)

    cur = pl.BlockSpec((2 * BAND, aw), lambda i: (i, 0))
    nxt = pl.BlockSpec((BAND, aw), lambda i: (jnp.minimum(2 * i + 2, nb - 1), 0))
    cur1 = pl.BlockSpec((2 * BAND, HEAD_DIM), lambda i: (i, 0))
    nxt1 = pl.BlockSpec((BAND, HEAD_DIM), lambda i: (jnp.minimum(2 * i + 2, nb - 1), 0))
    return pl.pallas_call(
        body, name=name, grid=(nb // 2,), in_specs=[cur, nxt, cur, nxt, cur1, nxt1, cur1, nxt1, cur, cur], out_specs=[cur] * 3,
        out_shape=[jax.ShapeDtypeStruct((s, aw), BF16)] * 3, scratch_shapes=[pltpu.VMEM((BAND, aw), F32)],
        compiler_params=_cparams(("arbitrary",)),
    )(q, q, do, do, lse, lse, delta, delta, k, v)


STREAM_ROWS = BAND * max(dil for _, dil in DILATION_PATTERNS)


def _stream_orders(name, fn, srcs, vecs, n_vals, n_heads):
    s = srcs[0][0].shape[0]
    dils = [dil for _, dil in DILATION_PATTERNS]
    n_in = len(srcs) + len(vecs)

    def body(*refs):
        outs, scr = refs[n_in:-1], refs[-1]
        for vi, val in enumerate(fn(*[r[...] for r in refs[:n_in]])):
            scr[...] = val
            for di, dil in enumerate(dils):
                o = outs[vi * len(dils) + di]
                if dil == 1:
                    o[...] = val.astype(o.dtype)
                    continue
                for part in range(STREAM_ROWS // (dil * BAND)):
                    for r in range(dil):
                        rows = scr[pl.ds(part * dil * BAND + r, BAND, stride=dil), :]
                        o[r, part * BAND : (part + 1) * BAND, :] = rows.astype(o.dtype)

    in_specs = [pl.BlockSpec((STREAM_ROWS, HEAD_DIM), lambda b, h, cb=cb: (b, cb + h)) for (_, cb) in srcs]
    in_specs += [pl.BlockSpec(v.shape, lambda b, h, nd=v.ndim: (0,) * nd) for v in vecs]
    out_specs, out_shape = [], []
    for _ in range(n_vals):
        for dil in dils:
            if dil == 1:
                out_specs.append(pl.BlockSpec((STREAM_ROWS, HEAD_DIM), lambda b, h: (b, h)))
                out_shape.append(jax.ShapeDtypeStruct((s, n_heads * HEAD_DIM), BF16))
            else:
                out_specs.append(pl.BlockSpec((dil, STREAM_ROWS // dil, HEAD_DIM), lambda b, h: (0, b, h)))
                out_shape.append(jax.ShapeDtypeStruct((dil, s // dil, n_heads * HEAD_DIM), BF16))
    res = pl.pallas_call(
        body, name=name, grid=(s // STREAM_ROWS, n_heads), in_specs=in_specs, out_specs=out_specs, out_shape=out_shape,
        scratch_shapes=[pltpu.VMEM((STREAM_ROWS, HEAD_DIM), F32)], compiler_params=_cparams(("arbitrary", "arbitrary")),
    )(*[a for (a, _) in srcs], *vecs)
    res = [r.reshape(s, n_heads * HEAD_DIM) for r in res]
    return [res[vi * len(dils) : (vi + 1) * len(dils)] for vi in range(n_vals)]


def _natural_order(name, fn, vals, extras, out_dtypes, n_heads):
    s = vals[0][0].shape[0]
    dils = [dil for _, dil in DILATION_PATTERNS]
    n_in = len(vals) * len(dils) + len(extras)

    def body(*refs):
        outs, scr = refs[n_in:-1], refs[-1]
        blocks = []
        for vi in range(len(vals)):
            per_pattern = []
            for di, dil in enumerate(dils):
                x = refs[vi * len(dils) + di]
                if dil == 1:
                    per_pattern.append(x[...].astype(F32))
                    continue
                for part in range(STREAM_ROWS // (dil * BAND)):
                    for r in range(dil):
                        scr[pl.ds(part * dil * BAND + r, BAND, stride=dil), :] = x[r, part * BAND : (part + 1) * BAND, :].astype(F32)
                per_pattern.append(scr[...])
            blocks.append(per_pattern)
        res = fn(pl.program_id(1), blocks, *[r[...] for r in refs[len(vals) * len(dils) : n_in]])
        for o, v in zip(outs, res):
            o[...] = v.astype(o.dtype)

    in_specs, ins = [], []
    for per_pattern in vals:
        for arr, dil in zip(per_pattern, dils):
            if dil == 1:
                in_specs.append(pl.BlockSpec((STREAM_ROWS, HEAD_DIM), lambda b, h: (b, h)))
                ins.append(arr)
            else:
                in_specs.append(pl.BlockSpec((dil, STREAM_ROWS // dil, HEAD_DIM), lambda b, h: (0, b, h)))
                ins.append(arr.reshape(dil, s // dil, n_heads * HEAD_DIM))
    in_specs += [pl.BlockSpec((STREAM_ROWS, HEAD_DIM), lambda b, h: (b, 0)) for _ in extras]
    return pl.pallas_call(
        body, name=name, grid=(s // STREAM_ROWS, n_heads), in_specs=in_specs,
        out_specs=[pl.BlockSpec((STREAM_ROWS, HEAD_DIM), lambda b, h: (b, h)) for _ in out_dtypes],
        out_shape=[jax.ShapeDtypeStruct((s, n_heads * HEAD_DIM), dt) for dt in out_dtypes],
        scratch_shapes=[pltpu.VMEM((STREAM_ROWS, HEAD_DIM), F32)], compiler_params=_cparams(("arbitrary", "arbitrary")),
    )(*ins, *extras)


def _to_streams(v, dil):
    if dil == 1:
        return v
    s, w = v.shape
    return v.reshape(s // dil, dil, w).transpose(1, 0, 2).reshape(s, w)


def _from_streams(v, dil):
    if dil == 1:
        return v
    s, w = v.shape
    return v.reshape(dil, s // dil, w).transpose(1, 0, 2).reshape(s, w)


SCAN_LANES = 1024
SCAN_LANES_BWD = 512
SSM_ROWS = 128
SSM_FWD_ROWS = 256
FF_TM = 1024
FF_TN = 2048


def _cmul(ar, ai, br, bi):
    return ar * br - ai * bi, ar * bi + ai * br


def _cmul_conj(ar, ai, br, bi):
    return ar * br + ai * bi, ar * bi - ai * br


def _ssm_fwd(u_src, bre, bim, abre, abim, cre, cim, consts, d_skip):
    u_arr, sw, ucb = u_src
    s = u_arr.shape[0]
    nblk = bre.shape[0]
    nst = nblk * 512
    t = _blk(s, SSM_FWD_ROWS)
    w = _blk(nst, SCAN_LANES)

    def body(u_ref, bre_ref, bim_ref, abre_ref, abim_ref, cre_ref, cim_ref, k_ref, ds_ref, y_ref, hr_ref, hi_ref, xr, xi, car_r, car_i):
        @pl.when(pl.program_id(0) == 0)
        def _():
            car_r[...] = jnp.zeros(car_r.shape, F32)
            car_i[...] = jnp.zeros(car_i.shape, F32)

        u = u_ref[...].astype(F32)
        ub = u_ref[...].astype(BF16)
        row = lax.broadcasted_iota(jnp.int32, u.shape, 0)
        u1b = jnp.where(row % SUBLANES == 0, 0.0, pltpu.roll(u, 1, 0)).astype(BF16)
        for b in range(nblk):
            ch, st = slice(b * 128, (b + 1) * 128), slice(b * 512, (b + 1) * 512)
            xr[:, st] = jnp.dot(ub[:, ch], bre_ref[b], preferred_element_type=F32) + jnp.dot(u1b[:, ch], abre_ref[b], preferred_element_type=F32)
            xi[:, st] = jnp.dot(ub[:, ch], bim_ref[b], preferred_element_type=F32) + jnp.dot(u1b[:, ch], abim_ref[b], preferred_element_type=F32)
        for c in range(nst // w):
            ls = slice(c * w, (c + 1) * w)
            mult = [k_ref[i, :, ls] for i in range(8)]

            def step(i, carry, ls=ls, mult=mult):
                cr, ci = carry
                r0 = pl.multiple_of(i * SUBLANES, SUBLANES)
                hr, hi = xr[pl.ds(r0, SUBLANES), ls], xi[pl.ds(r0, SUBLANES), ls]
                for n, sh in ((1, 2), (2, 4)):
                    pr, pi = _cmul(mult[2 * n], mult[2 * n + 1], pltpu.roll(hr, sh, 0), pltpu.roll(hi, sh, 0))
                    hr, hi = hr + pr, hi + pi
                pr, pi = _cmul(mult[6], mult[7], cr, ci)
                hr, hi = hr + pr, hi + pi
                xr[pl.ds(r0, SUBLANES), ls] = hr
                xi[pl.ds(r0, SUBLANES), ls] = hi
                return jnp.broadcast_to(hr[7:8, :], hr.shape), jnp.broadcast_to(hi[7:8, :], hi.shape)

            cr, ci = lax.fori_loop(0, t // SUBLANES, step, (car_r[:, ls], car_i[:, ls]))
            car_r[:, ls] = cr
            car_i[:, ls] = ci
        hrb, hib = xr[...].astype(BF16), xi[...].astype(BF16)
        hr_ref[...] = hrb
        hi_ref[...] = hib
        for b in range(nblk):
            ch, st = slice(b * 128, (b + 1) * 128), slice(b * 512, (b + 1) * 512)
            yb = jnp.dot(hrb[:, st], cre_ref[b], preferred_element_type=F32)
            yb -= jnp.dot(hib[:, st], cim_ref[b], preferred_element_type=F32)
            y_ref[:, ch] = yb + ds_ref[:, ch] * u[:, ch]

    whole = lambda a: pl.BlockSpec(a.shape, lambda i, nd=a.ndim: (0,) * nd)
    return pl.pallas_call(
        body, name="ssm_fwd", grid=(s // t,),
        in_specs=[pl.BlockSpec((t, sw), lambda i: (i, ucb))] + [whole(a) for a in (bre, bim, abre, abim, cre, cim, consts, d_skip)],
        out_specs=[pl.BlockSpec((t, sw), lambda i: (i, 0)), pl.BlockSpec((t, nst), lambda i: (i, 0)), pl.BlockSpec((t, nst), lambda i: (i, 0))],
        out_shape=[jax.ShapeDtypeStruct((s, sw), F32), jax.ShapeDtypeStruct((s, nst), BF16), jax.ShapeDtypeStruct((s, nst), BF16)],
        scratch_shapes=[pltpu.VMEM((t, nst), F32), pltpu.VMEM((t, nst), F32), pltpu.VMEM((SUBLANES, nst), F32), pltpu.VMEM((SUBLANES, nst), F32)],
        compiler_params=_cparams(("arbitrary",)),
    )(u_arr, bre, bim, abre, abim, cre, cim, consts, d_skip)


def _ssm_bwd(dy, u_src, hr, hi, ctre, ctim, acre, acim, btre, btim, consts, d_skip):
    u_arr, sw, ucb = u_src
    s = dy.shape[0]
    nblk = ctre.shape[0]
    nst = nblk * 512
    t = _blk(s, SSM_ROWS)
    w = _blk(nst, SCAN_LANES_BWD)
    nt = s // t

    def body(dy_ref, u_ref, hr_ref, hi_ref, ctre_ref, ctim_ref, acre_ref, acim_ref, btre_ref, btim_ref, k_ref, ds_ref,
             du_ref, db_ref, dc_ref, da_ref, dd_ref, lr, li, hfr, hfi, car_r, car_i):
        @pl.when(pl.program_id(0) == 0)
        def _():
            car_r[...] = jnp.zeros(car_r.shape, F32)
            car_i[...] = jnp.zeros(car_i.shape, F32)
            db_ref[...] = jnp.zeros(db_ref.shape, F32)
            dc_ref[...] = jnp.zeros(dc_ref.shape, F32)
            da_ref[...] = jnp.zeros(da_ref.shape, F32)
            dd_ref[...] = jnp.zeros(dd_ref.shape, F32)

        dyv, u = dy_ref[...].astype(F32), u_ref[...].astype(F32)
        dyb, ub = dyv.astype(BF16), u.astype(BF16)
        hrb, hib = hr_ref[...], hi_ref[...]
        hfr[...] = hrb.astype(F32)
        hfi[...] = hib.astype(F32)
        row = lax.broadcasted_iota(jnp.int32, dyv.shape, 0)
        dy1b = jnp.where(row % SUBLANES == SUBLANES - 1, 0.0, pltpu.roll(dyv, t - 1, 0)).astype(BF16)
        for b in range(nblk):
            ch, st = slice(b * 128, (b + 1) * 128), slice(b * 512, (b + 1) * 512)
            lr[:, st] = jnp.dot(dyb[:, ch], ctre_ref[b], preferred_element_type=F32) + jnp.dot(dy1b[:, ch], acre_ref[b], preferred_element_type=F32)
            li[:, st] = -(jnp.dot(dyb[:, ch], ctim_ref[b], preferred_element_type=F32) + jnp.dot(dy1b[:, ch], acim_ref[b], preferred_element_type=F32))
        last_row = lax.broadcasted_iota(jnp.int32, (SUBLANES, w), 0) == SUBLANES - 1
        for c in range(nst // w):
            ls = slice(c * w, (c + 1) * w)
            mult = [k_ref[i, :, ls] for i in range(8)]

            def step(i, carry, ls=ls, mult=mult):
                cr, ci, ar, ai = carry
                r0 = pl.multiple_of((t // SUBLANES - 1 - i) * SUBLANES, SUBLANES)
                gr, gi = lr[pl.ds(r0, SUBLANES), ls], li[pl.ds(r0, SUBLANES), ls]
                for n, sh in ((1, 2), (2, 4)):
                    pr, pi = _cmul_conj(mult[2 * n], mult[2 * n + 1], pltpu.roll(gr, SUBLANES - sh, 0), pltpu.roll(gi, SUBLANES - sh, 0))
                    gr, gi = gr + pr, gi + pi
                pr, pi = _cmul_conj(mult[6], mult[7], cr, ci)
                gr, gi = gr + pr, gi + pi
                lr[pl.ds(r0, SUBLANES), ls] = gr
                li[pl.ds(r0, SUBLANES), ls] = gi
                nr = jnp.where(last_row, cr, pltpu.roll(gr, SUBLANES - 1, 0))
                ni = jnp.where(last_row, ci, pltpu.roll(gi, SUBLANES - 1, 0))
                fr, fi = hfr[pl.ds(r0, SUBLANES), ls], hfi[pl.ds(r0, SUBLANES), ls]
                pr, pi = _cmul_conj(fr, fi, nr, ni)
                return (jnp.broadcast_to(gr[0:1, :], gr.shape), jnp.broadcast_to(gi[0:1, :], gi.shape), ar + pr, ai + pi)

            zero = jnp.zeros((SUBLANES, w), F32)
            cr, ci, ar, ai = lax.fori_loop(0, t // SUBLANES, step, (car_r[:, ls], car_i[:, ls], zero, zero))
            car_r[:, ls] = cr
            car_i[:, ls] = ci
            da_ref[0, :, ls] += ar
            da_ref[1, :, ls] += ai
        lrb, lib = lr[...].astype(BF16), li[...].astype(BF16)
        for b in range(nblk):
            ch, st = slice(b * 128, (b + 1) * 128), slice(b * 512, (b + 1) * 512)
            dub = jnp.dot(lrb[:, st], btre_ref[b], preferred_element_type=F32)
            dub += jnp.dot(lib[:, st], btim_ref[b], preferred_element_type=F32)
            du_ref[:, ch] = (dub + ds_ref[:, ch] * dyv[:, ch]).astype(du_ref.dtype)
            db_ref[0, b] += lax.dot_general(ub[:, ch], lrb[:, st], _TN, preferred_element_type=F32)
            db_ref[1, b] += lax.dot_general(ub[:, ch], lib[:, st], _TN, preferred_element_type=F32)
            dc_ref[0, b] += lax.dot_general(dyb[:, ch], hrb[:, st], _TN, preferred_element_type=F32)
            dc_ref[1, b] -= lax.dot_general(dyb[:, ch], hib[:, st], _TN, preferred_element_type=F32)
        dd_ref[...] += _colsum8(dyv * u)

    whole = lambda a: pl.BlockSpec(a.shape, lambda i, nd=a.ndim: (0,) * nd)
    rev = lambda wd, cb=0: pl.BlockSpec((t, wd), lambda i, cb=cb: (nt - 1 - i, cb))
    acc = lambda sh: pl.BlockSpec(sh, lambda i, nd=len(sh): (0,) * nd)
    db_shape, da_shape, dd_shape = (2, nblk, 128, 512), (2, SUBLANES, nst), (SUBLANES, sw)
    return pl.pallas_call(
        body, name="ssm_bwd", grid=(nt,),
        in_specs=[rev(sw), rev(sw, ucb), rev(nst), rev(nst)] + [whole(a) for a in (ctre, ctim, acre, acim, btre, btim, consts, d_skip)],
        out_specs=[rev(sw), acc(db_shape), acc(db_shape), acc(da_shape), acc(dd_shape)],
        out_shape=[jax.ShapeDtypeStruct((s, sw), BF16), jax.ShapeDtypeStruct(db_shape, F32), jax.ShapeDtypeStruct(db_shape, F32),
                   jax.ShapeDtypeStruct(da_shape, F32), jax.ShapeDtypeStruct(dd_shape, F32)],
        scratch_shapes=[pltpu.VMEM((t, nst), F32)] * 4 + [pltpu.VMEM((SUBLANES, nst), F32)] * 2,
        compiler_params=_cparams(("arbitrary",)),
    )(dy, u_arr, hr, hi, ctre, ctim, acre, acim, btre, btim, consts, d_skip)


def _ssm_discretise(lam_re, lam_im, log_step):
    step = jnp.exp(log_step)
    e = jnp.exp(lam_re * step)
    ar, ai = e * jnp.cos(lam_im * step), e * jnp.sin(lam_im * step)
    den = lam_re * lam_re + lam_im * lam_im
    inv_r, inv_i = lam_re / den, -lam_im / den
    fr, fi = _cmul(ar - 1.0, ai, inv_r, inv_i)
    return step, ar, ai, inv_r, inv_i, fr, fi


def _ssm_prep_fn(lam_re, lam_im, log_step):
    _, ar, ai, _, _, fr, fi = _ssm_discretise(lam_re, lam_im, log_step)
    pw = [(ar, ai)]
    for n in range(1, 8):
        pw.append(_cmul(*pw[n - 1], ar, ai))
    return [p[0] for p in pw] + [p[1] for p in pw] + [fr, fi]


def _ssm_param_grad_fn(lam_re, lam_im, log_step, da_r, da_i, df_r, df_i):
    step, ar, ai, inv_r, inv_i, fr, fi = _ssm_discretise(lam_re, lam_im, log_step)
    pr, pi = _cmul_conj(inv_r, inv_i, df_r, df_i)
    dat_r, dat_i = da_r + pr, da_i + pi
    wr, wi = _cmul(fr, fi, inv_r, inv_i)
    dl_r, dl_i = _cmul_conj(-wr, -wi, df_r, df_i)
    dz_r, dz_i = _cmul_conj(ar, ai, dat_r, dat_i)
    dl_r, dl_i = dl_r + step * dz_r, dl_i + step * dz_i
    dstep = jnp.sum(dz_r * lam_re + dz_i * lam_im, axis=-1, keepdims=True)
    return [dl_r, dl_i, step * dstep]


def _block_diag(v, rows_first):
    g, a, b = v.shape
    nb = g // GROUPS_PER_BLOCK
    eye = jnp.eye(GROUPS_PER_BLOCK, dtype=v.dtype)
    v = v.reshape(nb, GROUPS_PER_BLOCK, a, 1, b) * eye[None, :, None, :, None]
    return v.reshape(nb, GROUPS_PER_BLOCK * a, GROUPS_PER_BLOCK * b)


def _block_diag_take(v, a, b):
    nb = v.shape[0]
    v = v.reshape(nb, GROUPS_PER_BLOCK, a, GROUPS_PER_BLOCK, b)
    return jnp.stack([v[:, g, :, g, :] for g in range(GROUPS_PER_BLOCK)], axis=1).reshape(nb * GROUPS_PER_BLOCK, a, b)


def _adamw(w, g, m, v):
    m = ADAM_B1 * m + (1.0 - ADAM_B1) * g
    v = ADAM_B2 * v + (1.0 - ADAM_B2) * (g * g)
    m_hat = m / (1.0 - ADAM_B1**ADAM_STEP)
    v_hat = v / (1.0 - ADAM_B2**ADAM_STEP)
    delta = -ADAM_LR * (m_hat / (jnp.sqrt(v_hat) + ADAM_EPS) + ADAM_WD * w)
    return delta, m, v


def _adamw_rows(name, w, g, m, v):
    wd = w.shape[1]
    return _rowwise(name, lambda a, b, c, d: (list(_adamw(a, b, c, d)), []), [w, g, m, v], [], [(wd, F32)] * 3, [], tm=128)


SMALL = ["b_ada", "norm1_g", "q_norm_g", "k_norm_g", "lam_re", "lam_im", "log_step", "b_re", "b_im", "c_re", "c_im",
         "d_skip", "b_glu", "attn_out_g", "ssm_out_g", "norm2_g"]
LARGE = ["w_in", "w_glu", "w_out", "w_ff1", "w_ff2"]
ORDER = ["w_ada", "b_ada", "norm1_g", "w_in", "q_norm_g", "k_norm_g", "lam_re", "lam_im", "log_step", "b_re", "b_im", "c_re",
         "c_im", "d_skip", "w_glu", "b_glu", "attn_out_g", "ssm_out_g", "w_out", "norm2_g", "w_ff1", "w_ff2"]


def _pack(arrs):
    flat = jnp.concatenate([a.reshape(-1) for a in arrs])
    rows = -(-flat.shape[0] // PACK_LANES)
    rows = -(-rows // SUBLANES) * SUBLANES
    return jnp.pad(flat, (0, rows * PACK_LANES - flat.shape[0])).reshape(rows, PACK_LANES)


def _unpack(packed, shapes):
    flat, out, o = packed.reshape(-1), [], 0
    for sh in shapes:
        n = int(np.prod(sh))
        out.append(flat[o : o + n].reshape(sh))
        o += n
    return out


def kernel(x, c, w_ada, b_ada, norm1_g, w_in, q_norm_g, k_norm_g, lam_re, lam_im, log_step, b_re, b_im, c_re, c_im, d_skip, w_glu, b_glu, attn_out_g, ssm_out_g, w_out, norm2_g, w_ff1, w_ff2, loss_target, m_w_ada, m_b_ada, m_norm1_g, m_w_in, m_q_norm_g, m_k_norm_g, m_lam_re, m_lam_im, m_log_step, m_b_re, m_b_im, m_c_re, m_c_im, m_d_skip, m_w_glu, m_b_glu, m_attn_out_g, m_ssm_out_g, m_w_out, m_norm2_g, m_w_ff1, m_w_ff2, v_w_ada, v_b_ada, v_norm1_g, v_w_in, v_q_norm_g, v_k_norm_g, v_lam_re, v_lam_im, v_log_step, v_b_re, v_b_im, v_c_re, v_c_im, v_d_skip, v_w_glu, v_b_glu, v_attn_out_g, v_ssm_out_g, v_w_out, v_norm2_g, v_w_ff1, v_w_ff2):
    args = dict(locals())
    wts = {n: args[n] for n in ORDER}
    mom = {n: args["m_" + n] for n in ORDER}
    var = {n: args["v_" + n] for n in ORDER}

    x2, tgt = x[0], loss_target[0]
    s, d = x2.shape
    aw = d // 2
    sw = d - aw
    n_groups = sw // SSM_GROUP
    nst = n_groups * STATE_DIM
    chip = 2 * lax.axis_index("x") + lax.axis_index("y")

    chip_idx = jnp.reshape(chip, (1,)).astype(jnp.int32)
    core_idx = jnp.reshape(lax.axis_index("c"), (1,)).astype(jnp.int32)

    def cast_into_stack(n):
        w2 = wts[n][0]
        rows, cols = w2.shape
        tr = _blk(rows, 256)
        return _indexed("cast_" + n, lambda a: a, chip_idx, [(w2, (tr, cols), lambda r, ix: (r, 0))],
                        (((4, rows, cols), BF16), (None, tr, cols), lambda r, ix: (ix[0], r, 0)), (rows // tr,))

    stacks = {n: cast_into_stack(n) for n in LARGE}
    dff = 4 * wts["w_ff2"].shape[1]

    c_all = _allgather8("gather_c", jnp.pad(c, ((0, SUBLANES - 1), (0, 0))))[::SUBLANES]
    n_mod = w_ada.shape[2]
    b_ada_mine = lax.dynamic_slice(b_ada, (0, chip * n_mod), (1, n_mod))
    sil_c = _single("silu_c", lambda v: [v * _sigmoid(v)], [c_all], [(c_all.shape, F32)])[0]
    mod_part = _mm("ada_fwd", sil_c, w_ada[0], "nn", F32, tn=512, epi=lambda acc, b: acc + b[0:1, :],
                   extras=[jnp.broadcast_to(b_ada_mine, (SUBLANES, n_mod))])
    mod_all = _allgather8("gather_mod", mod_part)
    me = 2 * chip + lax.axis_index("c")
    mod = jnp.concatenate([lax.dynamic_slice(mod_all, (16 * k + me, 0), (1, n_mod)) for k in range(4)], axis=1)
    sh1, sc1, g1, sh2, sc2, g2 = [mod[:, i * d : (i + 1) * d] for i in range(6)]

    (w_in_s,) = _gather_weights([stacks["w_in"]], mod_all)

    def gather_plan(refs):
        px0, py0, pc, chips = _place()
        slot = 2 * px0 + py0
        return [(r.at[slot], r.at[slot], (px, py, pc), r.at[2 * px + py]) for r in refs for (px, py) in chips]

    mix_sems_s, mix_sems_r, mix_stacks, mix_token = _split_start(
        "gather_mix_start", [stacks["w_glu"], stacks["w_out"]], gather_plan, 6, extra=[w_in_s])
    w_in_s = mix_stacks.pop()
    ff_sems_s, ff_sems_r, ff_stacks, ff_token = _split_start(
        "gather_ff_start", [stacks["w_ff1"], stacks["w_ff2"]], gather_plan, 6, extra=[w_in_s])
    w_in_s = ff_stacks.pop()
    sc1 = sc1 + (mix_token[0:1, 0:1] + ff_token[0:1, 0:1])

    def norm_mod(v, g, sc, sh):
        return [v * _rstd(v) * g * (1.0 + sc) + sh], []

    (h1b,) = _rowwise("norm1", norm_mod, [x2], [norm1_g, sc1, sh1], [(d, BF16)], [])
    proj = _mm("proj_in", h1b, w_in_s, "nn", BF16, b_cs=True, tm=FF_TM)
    assert aw == sw
    q_src, k_src, v_src, u_src = [(proj, aw, i) for i in range(3)] + [(proj, sw, 3)]

    n_heads = aw // HEAD_DIM
    up = lambda v: v.astype(F32)
    qk_orders = _stream_orders("qk_prep", lambda q, k, v, gq, gk: [up(q) * _rstd(up(q)) * gq, up(k) * _rstd(up(k)) * gk, up(v)],
                               [(proj, 0), (proj, n_heads), (proj, 2 * n_heads)], [q_norm_g, k_norm_g], 3, n_heads)

    pats = []
    for pi, (window, dil) in enumerate(DILATION_PATTERNS):
        assert window // dil == BAND and s % STREAM_ROWS == 0
        bps = s // dil // BAND
        qp, kp, vp = [order[pi] for order in qk_orders]
        o_p, l_p = _attn_fwd(f"attn_fwd_d{dil}", qp, kp, vp, bps, dil)
        pats.append((dil, bps, qp, kp, vp, o_p, _from_streams(l_p, dil)))

    def lse_total(l1, l2, l3):
        m = jnp.maximum(jnp.maximum(l1, l2), l3)
        return m + jnp.log(jnp.exp(l1 - m) + jnp.exp(l2 - m) + jnp.exp(l3 - m))

    (lse,) = _rowwise("attn_lse", lambda l1, l2, l3: ([lse_total(l1, l2, l3)], []), [p[6] for p in pats], [], [(HEAD_DIM, F32)], [])

    def combine(h, blocks, l1, l2, l3):
        tot = lse_total(l1, l2, l3)
        lane = lax.broadcasted_iota(jnp.int32, l1.shape, 1)
        mix = 0.0
        for o_blk, l in zip(blocks[0], (l1, l2, l3)):
            w = jnp.sum(jnp.where(lane == h, jnp.exp(l - tot), 0.0), axis=1, keepdims=True)
            mix = mix + w * o_blk
        return [mix]

    (attn,) = _natural_order("attn_mix", combine, [[p[5] for p in pats]], [p[6] for p in pats], [F32], n_heads)

    lam_re2, lam_im2, log_step2 = lam_re[0], lam_im[0], log_step[0].reshape(n_groups, 1)
    prep = _single("ssm_prep", _ssm_prep_fn, [lam_re2, lam_im2, log_step2], [((n_groups, STATE_DIM), F32)] * 18)
    p_r = jnp.stack([p.reshape(nst) for p in prep[:8]])
    p_i = jnp.stack([p.reshape(nst) for p in prep[8:16]])
    f_ri = jnp.stack(prep[16:])
    row = jnp.arange(SUBLANES)[:, None]
    zero = jnp.zeros((SUBLANES, nst), F32)
    sel = lambda cond, v: jnp.where(cond, jnp.broadcast_to(v[None, :], (SUBLANES, nst)), zero)
    fwd_consts = jnp.stack([sel(row >= 1, p_r[0]), sel(row >= 1, p_i[0]), sel(row >= 2, p_r[1]), sel(row >= 2, p_i[1]),
                            sel(row >= 4, p_r[3]), sel(row >= 4, p_i[3]), p_r, p_i])
    bwd_consts = jnp.stack([sel(row <= 6, p_r[0]), sel(row <= 6, p_i[0]), sel(row <= 5, p_r[1]), sel(row <= 5, p_i[1]),
                            sel(row <= 3, p_r[3]), sel(row <= 3, p_i[3]), p_r[::-1], p_i[::-1]])
    nblk = n_groups // GROUPS_PER_BLOCK
    f_rows = f_ri.reshape(2, nblk, 1, 512)
    braw_r = _block_diag(b_re[0].transpose(0, 2, 1), True)
    braw_i = _block_diag(b_im[0].transpose(0, 2, 1), True)

    craw_r, craw_i = _block_diag(c_re[0], True), _block_diag(c_im[0], True)
    a_rows = jnp.stack([p_r[0], p_i[0]]).reshape(2, nblk, 1, 512)

    def bbar_fn(br, bi, f, cr, ci, a):
        r, i = _cmul(f[0], f[1], br, bi)
        ar, ai = _cmul(a[0], a[1], r, i)
        acr, aci = _cmul(a[0], a[1], cr, ci)
        return [r, i, ar, ai, acr, aci]

    bbar_r, bbar_i, abbar_r, abbar_i, ac_r, ac_i = _single(
        "ssm_bbar", bbar_fn, [braw_r, braw_i, f_rows, craw_r, craw_i, a_rows], [(braw_r.shape, BF16)] * 6)
    ct_r, ct_i = craw_r.astype(BF16), craw_i.astype(BF16)
    ds2 = d_skip
    y_ssm, h_r, h_i = _ssm_fwd(u_src, bbar_r, bbar_i, abbar_r, abbar_i, ct_r.transpose(0, 2, 1), ct_i.transpose(0, 2, 1), fwd_consts, ds2)

    (yb,) = _rowwise("gelu", lambda v: ([_gelu(v)], []), [y_ssm], [], [(sw, BF16)], [])
    w_glu_s, w_out_s = _split_wait("gather_mix_wait", mix_stacks, mix_sems_s, mix_sems_r, yb, gather_plan)
    w_glu_f = w_glu_s.reshape(sw, sw)
    w_out_f = w_out_s.reshape(d, d)
    z = _mm("glu_fwd", yb, w_glu_f, "nn", BF16)

    def glu(ys, zz, b):
        return [_gelu(ys) * _sigmoid(zz.astype(F32) + b)], []

    (ssm,) = _rowwise("glu", glu, [y_ssm, z], [b_glu], [(sw, F32)], [])

    def cat_norm(a, sm, ga, gs):
        return [jnp.concatenate([a * _rstd(a) * ga, sm * _rstd(sm) * gs], axis=1)], []

    (catb,) = _rowwise("cat_norm", cat_norm, [attn, ssm], [attn_out_g, ssm_out_g], [(d, BF16)], [])
    mixed = _mm("mix_out", catb, w_out_f, "nn", BF16, tm=FF_TM, tn=FF_TN)

    def resid_norm(xv, mx, g1v, g, sc, sh):
        x1v = xv + g1v * mx.astype(F32)
        return [x1v, x1v * _rstd(x1v) * g * (1.0 + sc) + sh], []

    x1, h2b = _rowwise("resid_norm2", resid_norm, [x2, mixed], [g1, norm2_g, sc2, sh2], [(d, F32), (d, BF16)], [])
    w_ff1_s, w_ff2_s = _split_wait("gather_ff_wait", ff_stacks, ff_sems_s, ff_sems_r, x1, gather_plan)
    w_ff2_f = w_ff2_s.reshape(dff, d)
    a_ff = _mm("ff1", h2b, w_ff1_s, "nn", BF16, b_cs=True, tm=FF_TM, tn=FF_TN)
    sq_relu = lambda v: jnp.square(jnp.maximum(v.astype(F32), 0.0)).astype(BF16)
    ff = _mm("ff2", a_ff, w_ff2_f, "nn", BF16, a_pro=sq_relu, tm=FF_TM)

    def loss_fn(x1v, ffv, tg, g2v):
        ffv = ffv.astype(F32)
        diff = x1v + g2v * ffv - tg
        dout = diff * (1.0 / d)
        return [dout, dout * g2v], [_colsum8(0.5 * diff * dout), _colsum8(dout * ffv)]

    dout, dffb, loss_acc, dg2_acc = _rowwise("loss", loss_fn, [x1, ff, tgt], [g2], [(d, F32), (d, BF16)], [(SUBLANES, d)] * 2)

    da_ff = _mm("ff2_bwd", dffb, w_ff2_f, "nt", BF16, epi=lambda acc, av: acc * (2.0 * jnp.maximum(av.astype(F32), 0.0)), extras=[a_ff], tm=FF_TM, tn=FF_TN)
    gw = {}
    gw["w_ff2"] = _mm("ff2_wgrad", a_ff, dffb, "tn", BF16, a_pro=sq_relu, tm=FF_TM).reshape(4, dff // 4, d)
    gw["w_ff1"] = _mm("ff1_wgrad", h2b, da_ff, "tn", BF16, o_cs=True, tm=FF_TM)

    def add2(n, g, b):
        _, half, cols = b.shape
        tr = _blk(half, 256)
        nrb = half // tr
        blk = (None, tr, cols)
        return _indexed("grad_add2_" + n, lambda p, q: p.astype(F32) + q.astype(F32), core_idx,
                        [(g, blk, lambda k, r, ix: (k, ix[0] * nrb + r, 0)), (b, blk, lambda k, r, ix: (k, r, 0))],
                        ((b.shape, BF16), blk, lambda k, r, ix: (k, r, 0)), (4, nrb))

    def scatter_plan(refs):
        px0, py0, pc, chips = _place()
        nw = len(refs) // 2
        return [(refs[w].at[2 * px + py], refs[nw + w].at[j], (px, py, pc), refs[nw + w].at[j])
                for w in range(nw) for j, (px, py) in enumerate(chips)]

    def swap_plan(refs):
        px0, py0, pc, _ = _place()
        nw = len(refs) // 2
        plan = []
        for w in range(nw):
            half = refs[w].shape[1] // 2
            plan.append((refs[w].at[:, pl.ds((1 - pc) * half, half), :], refs[nw + w], (px0, py0, 1 - pc), refs[nw + w]))
        return plan

    early = ["w_ff2", "w_ff1"]
    sw_lands = [lax.empty((4, gw[n].shape[1] // 2, gw[n].shape[2]), BF16) for n in early]
    sw_sems_s, sw_sems_r, sw_bufs, sw_token = _split_start("swap_ff_start", [gw[n] for n in early] + sw_lands, swap_plan, len(early))
    sc2_late = sc2 + sw_token[0:1, 0:1]

    dh2 = _mm("ff1_bwd", da_ff, w_ff1_s, "nt", BF16, b_cs=True, tm=FF_TM)

    def norm2_bwd(dh, x1v, do, mx, g, sc, g1v):
        dh, mx = dh.astype(F32), mx.astype(F32)
        r = _rstd(x1v)
        xh = x1v * r
        dn = dh * (1.0 + sc)
        dx = do + _rms_bwd(dn * g, xh, r)
        return [dx, dx * g1v], [_colsum8(dh * xh * g), _colsum8(dh), _colsum8(dn * xh), _colsum8(dx * mx)]

    dx1, dmixb, dsc2_acc, dsh2_acc, dn2g_acc, dg1_acc = _rowwise(
        "norm2_bwd", norm2_bwd, [dh2, x1, dout, mixed], [norm2_g, sc2_late, g1], [(d, F32), (d, BF16)], [(SUBLANES, d)] * 4)

    sw_bufs = _split_wait("swap_ff_wait", sw_bufs, sw_sems_s, sw_sems_r, dmixb, swap_plan)
    parts = {n: add2(n, g, b) for n, g, b in zip(early, sw_bufs[: len(early)], sw_bufs[len(early) :])}
    lands = [lax.empty((3,) + parts[n].shape[1:], BF16) for n in early]
    sc_sems_s, sc_sems_r, sc_bufs, sc_token = _split_start("scatter_ff_start", [parts[n] for n in early] + lands, scatter_plan, 6)
    attn_out_g_late = attn_out_g + sc_token[0:1, 0:1]

    dcat = _mm("mix_out_bwd", dmixb, w_out_f, "nt", BF16, tm=FF_TM, tn=FF_TN)
    gw["w_out"] = _mm("mix_out_wgrad", catb, dmixb, "tn", BF16, tm=FF_TM).reshape(4, d // 4, d)

    def cat_bwd(da, dsm, a, sm, ga, gs):
        da, dsm = da.astype(F32), dsm.astype(F32)
        ra, rs = _rstd(a), _rstd(sm)
        ah, sh = a * ra, sm * rs
        return [_rms_bwd(da * ga, ah, ra), _rms_bwd(dsm * gs, sh, rs)], [_colsum8(da * ah), _colsum8(dsm * sh)]

    dattn, dssm, dga_acc, dgs_acc = _rowwise(
        "cat_norm_bwd", cat_bwd, [(dcat, aw, 0), (dcat, sw, 1), attn, ssm], [attn_out_g_late, ssm_out_g], [(aw, F32), (sw, F32)],
        [(SUBLANES, aw), (SUBLANES, sw)])

    def glu_bwd(dsm, ys, zz, b):
        sg = _sigmoid(zz.astype(F32) + b)
        dz = dsm * _gelu(ys) * sg * (1.0 - sg)
        return [dz, dsm * sg], [_colsum8(dz)]

    dzb, dy_direct, dbglu_acc = _rowwise("glu_bwd", glu_bwd, [dssm, y_ssm, z], [b_glu], [(sw, BF16), (sw, BF16)], [(SUBLANES, sw)])
    gw["w_glu"] = _mm("glu_wgrad", yb, dzb, "tn", BF16).reshape(4, sw // 4, sw)
    dys = _mm("glu_bwd_mm", dzb, w_glu_f, "nt", BF16, epi=lambda acc, direct, ys: (acc + direct.astype(F32)) * _gelu_grad(ys), extras=[dy_direct, y_ssm])
    du, db_bar, dc_bd, da_acc, dds_acc = _ssm_bwd(dys, u_src, h_r, h_i, ct_r, ct_i, ac_r, ac_i, bbar_r.transpose(0, 2, 1),
                                                  bbar_i.transpose(0, 2, 1), bwd_consts, ds2)

    def delta_fn(da, a):
        return [_to_lanes([jnp.sum(p * q, axis=1, keepdims=True) for p, q in zip(_heads(da), _heads(a))])], []

    (delta,) = _rowwise("attn_delta", delta_fn, [dattn, attn], [], [(HEAD_DIM, F32)], [])
    (do_orders,) = _stream_orders("attn_do", lambda da: [da], [(dattn, 0)], [], 1, n_heads)
    dqs, dks, dvs = [], [], []
    for pi, (dil, bps, qp, kp, vp, _, _) in enumerate(pats):
        lsp, dlp = [_to_streams(t, dil) for t in (lse, delta)]
        dq_p, dk_p, dv_p = _attn_bwd(f"attn_bwd_d{dil}", qp, kp, vp, do_orders[pi], lsp, dlp, bps, dil)
        dqs.append(dq_p)
        dks.append(dk_p)
        dvs.append(dv_p)
    dq_t, dk_t, dv_t = _natural_order("attn_grad_sum", lambda h, blocks: [(b[0] + b[1]) + b[2] for b in blocks],
                                      [dqs, dks, dvs], [], [BF16] * 3, n_heads)

    def qk_bwd(dq_sum, dk_sum, dv_sum, duv, q, k, gq, gk):
        def one(dn, raw, g):
            outs, acc = [], 0.0
            for dh, h in zip(_heads(dn), _heads(raw)):
                r = _rstd(h)
                hh = h * r
                outs.append(_rms_bwd(dh * g, hh, r))
                acc = acc + _colsum8(dh * hh)
            return jnp.concatenate(outs, axis=1), acc

        dq, gq_acc = one(dq_sum.astype(F32), q.astype(F32), gq)
        dk, gk_acc = one(dk_sum.astype(F32), k.astype(F32), gk)
        return [jnp.concatenate([dq, dk, dv_sum.astype(F32), duv.astype(F32)], axis=1)], [gq_acc, gk_acc]

    dprojb, dgq_acc, dgk_acc = _rowwise("qk_norm_bwd", qk_bwd, [dq_t, dk_t, dv_t, du, q_src, k_src], [q_norm_g, k_norm_g],
                                        [(3 * aw + sw, BF16)], [(SUBLANES, HEAD_DIM)] * 2)
    gw["w_in"] = _mm("proj_in_wgrad", h1b, dprojb, "tn", BF16, o_cs=True, tm=FF_TM)
    late = [n for n in LARGE if n not in early]
    parts.update({n: add2(n, gw[n], b) for n, b in zip(late, _swap_halves([gw[n] for n in late], "swap_halves_rest"))})
    lands_late = [lax.empty((3,) + parts[n].shape[1:], BF16) for n in late]
    rest_sems_s, rest_sems_r, rest_bufs, rest_token = _split_start(
        "scatter_rest_start", [parts[n] for n in late] + lands_late, scatter_plan, 3 * len(late))
    sc1_late = sc1 + rest_token[0:1, 0:1]
    dh1 = _mm("proj_in_bwd", dprojb, w_in_s, "nt", BF16, b_cs=True, tm=FF_TM, tn=FF_TN)

    def norm1_bwd(dh, xv, dxr, g, sc):
        dh = dh.astype(F32)
        r = _rstd(xv)
        xh = xv * r
        dn = dh * (1.0 + sc)
        return [dxr + _rms_bwd(dn * g, xh, r)], [_colsum8(dh * xh * g), _colsum8(dh), _colsum8(dn * xh)]

    grad_x, dsc1_acc, dsh1_acc, dn1g_acc = _rowwise("norm1_bwd", norm1_bwd, [dh1, x2, dx1], [norm1_g, sc1_late], [(d, F32)], [(SUBLANES, d)] * 3)

    def b_grad_fn(dbb, br, bi, f):
        fr, fi = f[0], f[1]
        dbr, dbi = _cmul_conj(fr, fi, dbb[0], dbb[1])
        dfr = jnp.sum(dbb[0] * br + dbb[1] * bi, axis=1, keepdims=True)
        dfi = jnp.sum(dbb[1] * br - dbb[0] * bi, axis=1, keepdims=True)
        return [dbr, dbi, dfr, dfi]

    dbr_bd, dbi_bd, df_r, df_i = _single("ssm_b_grad", b_grad_fn, [db_bar, braw_r, braw_i, f_rows],
                                         [(braw_r.shape, F32)] * 2 + [((nblk, 1, 512), F32)] * 2)

    def sums_fn(*accs):
        return [jnp.sum(a, axis=-2, keepdims=True) for a in accs]

    acc_list = [dsh1_acc, dsc1_acc, dg1_acc, dsh2_acc, dsc2_acc, dg2_acc, dn1g_acc, dgq_acc, dgk_acc, dds_acc, dbglu_acc, dga_acc,
                dgs_acc, dn2g_acc, loss_acc, da_acc]
    sums = _single("small_sums", sums_fn, acc_list, [(a.shape[:-2] + (1, a.shape[-1]), F32) for a in acc_list])
    (dsh1, dsc1, dg1, dsh2, dsc2, dg2, dn1g, dgq, dgk, dds, dbglu, dga, dgs, dn2g, loss_row, da_sum) = sums
    loss_dev = _single("loss_sum", lambda v: [jnp.sum(v, axis=1, keepdims=True)], [loss_row], [((1, 1), F32)])[0]
    gp = (n_groups, STATE_DIM)
    dlam_re, dlam_im, dlog_step = _single(
        "ssm_param_grad", _ssm_param_grad_fn,
        [lam_re2, lam_im2, log_step2, da_sum[0].reshape(gp), da_sum[1].reshape(gp), df_r.reshape(gp), df_i.reshape(gp)],
        [(gp, F32), (gp, F32), ((n_groups, 1), F32)])
    small_dev = {
        "b_ada": jnp.concatenate([dsh1, dsc1, dg1, dsh2, dsc2, dg2], axis=1),
        "norm1_g": dn1g, "q_norm_g": dgq, "k_norm_g": dgk, "lam_re": dlam_re, "lam_im": dlam_im, "log_step": dlog_step,
        "b_re": _block_diag_take(dbr_bd, SSM_GROUP, STATE_DIM).transpose(0, 2, 1),
        "b_im": _block_diag_take(dbi_bd, SSM_GROUP, STATE_DIM).transpose(0, 2, 1),
        "c_re": _block_diag_take(dc_bd[0], SSM_GROUP, STATE_DIM), "c_im": _block_diag_take(dc_bd[1], SSM_GROUP, STATE_DIM),
        "d_skip": dds, "b_glu": dbglu, "attn_out_g": dga, "ssm_out_g": dgs, "norm2_g": dn2g,
    }

    packed = _pack([small_dev[n] for n in SMALL])
    rows = packed.shape[0]
    tr = _blk(rows, 256)
    me_idx = jnp.reshape(me, (1,)).astype(jnp.int32)
    gathered = _indexed("pack_small", lambda a: a, me_idx, [(packed, (tr, PACK_LANES), lambda r, ix: (r, 0))],
                        (((8, rows, PACK_LANES), F32), (None, tr, PACK_LANES), lambda r, ix: (ix[0], r, 0)), (rows // tr,))

    def small_plan(refs):
        (g,) = refs
        px0, py0, pc0, _ = _place()
        flip = lambda v, bit: 1 - v if bit else v
        plan = []
        for mask in range(1, 8):
            px, py, pc = flip(px0, mask & 4), flip(py0, mask & 2), flip(pc0, mask & 1)
            mine = g.at[4 * px0 + 2 * py0 + pc0]
            plan.append((mine, mine, (px, py, pc), g.at[4 * px + 2 * py + pc]))
        return plan

    small_sems_s, small_sems_r, (gathered,), small_token = _split_start("gather_small_start", [gathered], small_plan, 7)

    sc_bufs = _split_wait("scatter_ff_wait", sc_bufs, sc_sems_s, sc_sems_r, small_token, scatter_plan)
    parts.update(zip(early, sc_bufs[: len(early)]))
    got2 = dict(zip(early, sc_bufs[len(early) :]))
    rest_bufs = _split_wait("scatter_rest_wait", rest_bufs, rest_sems_s, rest_sems_r, small_token, scatter_plan)
    parts.update(zip(late, rest_bufs[: len(late)]))
    got2.update(zip(late, rest_bufs[len(late) :]))
    place_idx = jnp.stack([chip, lax.axis_index("c")]).astype(jnp.int32)

    def add4(n, p, b):
        _, half, cols = p.shape
        tr = _blk(half, 256)
        blk = (None, tr, cols)
        fn = lambda own, q0, q1, q2: ((own.astype(F32) + q0.astype(F32)) + q1.astype(F32)) + q2.astype(F32)
        srcs = [(p, blk, lambda r, ix: (ix[0], r, 0))] + [(b, blk, lambda r, ix, j=j: (j, r, 0)) for j in range(3)]
        return _indexed("grad_add4_" + n, fn, place_idx, srcs, (((2, half, cols), F32), blk, lambda r, ix: (ix[1], r, 0)), (half // tr,))

    def join_plan(refs):
        px0, py0, pc, _ = _place()
        return [(r.at[pc], r.at[pc], (px0, py0, 1 - pc), r.at[1 - pc]) for r in refs]

    out = {}

    def update_large(names, fulls):
        last = None
        for n, full in zip(names, fulls):
            g = full.reshape(wts[n].shape[1:])
            last = _adamw_rows("adamw_" + n, wts[n][0], g, mom[n][0], var[n][0])
            for kind, val in zip(("grad", "delta", "new_m", "new_v"), (g,) + tuple(last)):
                out[kind, n] = val[None]
        return last[0]

    rest = [n for n in LARGE if n not in early]
    ja_s, ja_r, fulls_a, _ = _split_start("join_ff_start", [add4(n, parts[n], got2[n]) for n in early], join_plan, len(early))
    jb_s, jb_r, fulls_b, jb_token = _split_start("join_rest_start", [add4(n, parts[n], got2[n]) for n in rest], join_plan, len(rest))
    done_a = update_large(early, _split_wait("join_ff_wait", fulls_a, ja_s, ja_r, jb_token, join_plan))
    done_b = update_large(rest, _split_wait("join_rest_wait", fulls_b, jb_s, jb_r, done_a, join_plan))
    (gathered,) = _split_wait("gather_small_wait", [gathered], small_sems_s, small_sems_r, done_b, small_plan)

    def sum8_fn(g):
        acc = g[0]
        for i in range(1, 8):
            acc = acc + g[i]
        return [acc]

    g_small = _single("sum_small", sum8_fn, [gathered], [((rows, PACK_LANES), F32)])[0]

    w_small, m_small, v_small = [_pack([src[n] for n in SMALL]) for src in (wts, mom, var)]
    upd_small = _adamw_rows("adamw_small", w_small, g_small, m_small, v_small)
    shapes = [wts[n].shape for n in SMALL]
    for kind, arr in zip(("grad", "delta", "new_m", "new_v"), (g_small,) + tuple(upd_small)):
        for n, val in zip(SMALL, _unpack(arr, shapes)):
            out[kind, n] = val

    dmod_all = gathered[:, 0 : -(-6 * d // PACK_LANES), :].reshape(8, -1)[:, : 6 * d]
    dmod_mine = lax.dynamic_slice(dmod_all, (0, chip * n_mod), (8, n_mod))
    g_ada = _mm("ada_wgrad", sil_c, dmod_mine, "tn", F32, tm=512, tn=512)
    upd = _adamw_rows("adamw_w_ada", w_ada[0], g_ada, m_w_ada[0], v_w_ada[0])
    for kind, val in zip(("grad", "delta", "new_m", "new_v"), (g_ada,) + tuple(upd)):
        out[kind, "w_ada"] = val[None]

    loss = lax.psum(loss_dev[0, 0], ("x", "y", "c"))
    return (loss, grad_x[None], *[out[kind, n] for kind in ("grad", "delta", "new_m", "new_v") for n in ORDER])
```

```python
import functools
import math

import jax
import jax.numpy as jnp
import numpy as np
from jax import lax
from jax.experimental import pallas as pl
from jax.experimental.pallas import tpu as pltpu

F32 = jnp.float32
BF16 = jnp.bfloat16
MESH = pl.DeviceIdType.MESH

EPS = 1e-6
HEAD_DIM = 128
SSM_GROUP = 16
STATE_DIM = 64
GROUPS_PER_BLOCK = 8
DILATION_PATTERNS = ((128, 1), (512, 4), (2048, 16))
BAND = 128
NEG = -1e30

ADAM_LR, ADAM_B1, ADAM_B2, ADAM_EPS, ADAM_WD, ADAM_STEP = 0.001, 0.9, 0.999, 1e-08, 0.01, 10

V7X_VMEM_BYTES = 64 * 2**20
VMEM_LIMIT = 48 * 2**20
SUBLANES = 8
PACK_LANES = 1024


def _cparams(sem=None):
    return pltpu.CompilerParams(dimension_semantics=sem, vmem_limit_bytes=VMEM_LIMIT)


def _blk(n, want):
    b = min(n, want)
    while n % b:
        b //= 2
    return b


def _colsum8(v):
    tm, w = v.shape
    return v.reshape(tm // SUBLANES, SUBLANES, w).sum(axis=0)


def _rowwise(name, fn, rows, vecs, out_rows, out_accs, tm=256, n_rows=None):
    rows = [r if isinstance(r, tuple) else (r, r.shape[1], 0) for r in rows]
    rows = [r if len(r) == 4 else r + (0,) for r in rows]
    s = rows[0][0].shape[0] if n_rows is None else n_rows
    tm = _blk(s, tm)
    assert all(r[3] % tm == 0 for r in rows)
    n_in, n_or = len(rows) + len(vecs), len(out_rows)

    def body(*refs):
        outs, accs = fn(*[r[...] for r in refs[:n_in]])
        for r, v in zip(refs[n_in : n_in + n_or], outs):
            r[...] = v.astype(r.dtype)
        if out_accs:
            acc_refs = refs[n_in + n_or :]

            @pl.when(pl.program_id(0) == 0)
            def _():
                for r in acc_refs:
                    r[...] = jnp.zeros(r.shape, r.dtype)

            for r, v in zip(acc_refs, accs):
                r[...] += v

    in_specs = [pl.BlockSpec((tm, w), lambda i, cb=cb, rb=r0 // tm: (i + rb, cb)) for (_, w, cb, r0) in rows]
    in_specs += [pl.BlockSpec(v.shape, lambda i, nd=v.ndim: (0,) * nd) for v in vecs]
    out_specs = [pl.BlockSpec((tm, w), lambda i: (i, 0)) for (w, _) in out_rows]
    out_specs += [pl.BlockSpec(sh, lambda i, nd=len(sh): (0,) * nd) for sh in out_accs]
    out_shape = [jax.ShapeDtypeStruct((s, w), dt) for (w, dt) in out_rows]
    out_shape += [jax.ShapeDtypeStruct(sh, F32) for sh in out_accs]
    res = pl.pallas_call(
        body, name=name, grid=(s // tm,), in_specs=in_specs, out_specs=out_specs, out_shape=out_shape,
        compiler_params=_cparams(("arbitrary",)),
    )(*[r[0] for r in rows], *vecs)
    return res


def _single(name, fn, ins, out_shapes):
    n_in = len(ins)

    def body(*refs):
        outs = fn(*[r[...] for r in refs[:n_in]])
        for r, v in zip(refs[n_in:], outs):
            r[...] = v.astype(r.dtype)

    vm = pl.BlockSpec(memory_space=pltpu.VMEM)
    return pl.pallas_call(
        body, name=name, in_specs=[vm] * n_in, out_specs=[vm] * len(out_shapes),
        out_shape=[jax.ShapeDtypeStruct(sh, dt) for sh, dt in out_shapes], compiler_params=_cparams(),
    )(*ins)


_NN = (((1,), (0,)), ((), ()))
_NT = (((1,), (1,)), ((), ()))
_TN = (((0,), (0,)), ((), ()))


def _mm(name, a, b, mode, out_dtype, *, tm=512, tn=1024, tk=2048, b_cs=False, o_cs=False, epi=None, extras=(), a_pro=None):
    if mode == "tn":
        k, m = a.shape
        n = b.shape[1]
    else:
        m, k = a.shape
        if mode == "nn":
            n = b.shape[2] * 4 if b_cs else b.shape[1]
        else:
            n = b.shape[1] if b_cs else b.shape[0]
    tm = _blk(m, tm)
    tn = _blk(n // 4 if (b_cs and mode == "nn") or o_cs else n, tn)
    tk = _blk(k // 4 if (b_cs and mode == "nt") else k, tk)
    nk = k // tk
    dims = {"nn": _NN, "nt": _NT, "tn": _TN}[mode]
    n_ex = len(extras)

    def body(*refs):
        a_ref, b_ref = refs[0], refs[1]
        ex_refs, o_ref = refs[2 : 2 + n_ex], refs[2 + n_ex]

        def finish(acc):
            if epi is not None:
                acc = epi(acc, *[r[...] for r in ex_refs])
            o_ref[...] = acc.astype(o_ref.dtype)

        a_blk = a_ref[...] if a_pro is None else a_pro(a_ref[...])
        part = lax.dot_general(a_blk, b_ref[...], dims, preferred_element_type=F32)
        if nk == 1:
            finish(part)
        else:
            acc_ref = refs[3 + n_ex]
            kk = pl.program_id(2)

            @pl.when(kk == 0)
            def _():
                acc_ref[...] = part

            @pl.when(kk > 0)
            def _():
                acc_ref[...] += part

            @pl.when(kk == nk - 1)
            def _():
                finish(acc_ref[...])

    a_spec = pl.BlockSpec((tk, tm), lambda i, j, kk: (kk, i)) if mode == "tn" else pl.BlockSpec((tm, tk), lambda i, j, kk: (i, kk))
    if mode == "nn":
        if b_cs:
            per = (n // 4) // tn
            b_spec = pl.BlockSpec((None, tk, tn), lambda i, j, kk: (j // per, kk, j % per))
        else:
            b_spec = pl.BlockSpec((tk, tn), lambda i, j, kk: (kk, j))
    elif mode == "nt":
        if b_cs:
            per = (k // 4) // tk
            b_spec = pl.BlockSpec((None, tn, tk), lambda i, j, kk: (kk // per, j, kk % per))
        else:
            b_spec = pl.BlockSpec((tn, tk), lambda i, j, kk: (j, kk))
    else:
        b_spec = pl.BlockSpec((tk, tn), lambda i, j, kk: (kk, j))
    if o_cs:
        per = (n // 4) // tn
        o_spec = pl.BlockSpec((None, tm, tn), lambda i, j, kk: (j // per, i, j % per))
        o_shape = jax.ShapeDtypeStruct((4, m, n // 4), out_dtype)
    else:
        o_spec = pl.BlockSpec((tm, tn), lambda i, j, kk: (i, j))
        o_shape = jax.ShapeDtypeStruct((m, n), out_dtype)
    ex_specs = [pl.BlockSpec((tm, tn), lambda i, j, kk: (i, j)) for _ in extras]
    return pl.pallas_call(
        body, name=name, grid=(m // tm, n // tn, nk), in_specs=[a_spec, b_spec] + ex_specs, out_specs=o_spec, out_shape=o_shape,
        scratch_shapes=[pltpu.VMEM((tm, tn), F32)] if nk > 1 else [],
        compiler_params=_cparams(("parallel", "parallel", "arbitrary")),
    )(a, b, *extras)


def _place():
    x, y, c = lax.axis_index("x"), lax.axis_index("y"), lax.axis_index("c")
    return x, y, c, [(1 - x, y), (x, 1 - y), (1 - x, 1 - y)]


def _allgather8(name, v):
    m_per, n = v.shape

    def body(x_ref, out_ref, send_sems, recv_sems, local_sem):
        x, y, c, chips = _place()
        me, sibling = (x, y, c), (x, y, 1 - c)

        def rows(px, py, pc):
            return out_ref.at[pl.ds((4 * px + 2 * py + pc) * m_per, m_per), :]

        def copy(k, block, to, src=None):
            return pltpu.make_async_remote_copy(
                src_ref=rows(*block) if src is None else src, dst_ref=rows(*block), send_sem=send_sems.at[k],
                recv_sem=recv_sems.at[k], device_id=to, device_id_type=MESH)

        mine = pltpu.make_async_copy(x_ref, rows(*me), local_sem)
        mine.start()
        first = [copy(0, me, sibling, src=x_ref)]
        first += [copy(1 + j, me, (*chip, c), src=x_ref) for j, chip in enumerate(chips)]
        for cp in first:
            cp.start()
        passed = [copy(4 + j, (*chip, c), sibling) for j, chip in enumerate(chips)]
        for j, chip in enumerate(chips):
            copy(1 + j, (*chip, c), me).wait_recv()
            passed[j].start()
        copy(0, sibling, me).wait_recv()
        for j, chip in enumerate(chips):
            copy(4 + j, (*chip, 1 - c), me).wait_recv()
        for cp in first + passed:
            cp.wait_send()
        mine.wait()

    return pl.pallas_call(
        body, name=name, out_shape=jax.ShapeDtypeStruct((8 * m_per, n), v.dtype),
        in_specs=[pl.BlockSpec(memory_space=pltpu.VMEM)], out_specs=pl.BlockSpec(memory_space=pltpu.VMEM),
        scratch_shapes=[pltpu.SemaphoreType.DMA((7,)), pltpu.SemaphoreType.DMA((7,)), pltpu.SemaphoreType.DMA],
        compiler_params=_cparams(),
    )(v)


def _gather_weights(stacks, after):
    nw = len(stacks)

    def body(*refs):
        outs = refs[nw + 1 : 2 * nw + 1]
        send_sems, recv_sems = refs[2 * nw + 1 :]
        x, y, c, chips = _place()
        me, sibling = (x, y, c), (x, y, 1 - c)

        def copy(w, k, block, to):
            px, py, pc = block
            half = outs[w].shape[1] // 2
            rows = outs[w].at[2 * px + py, pl.ds(pc * half, half), :]
            return pltpu.make_async_remote_copy(src_ref=rows, dst_ref=rows, send_sem=send_sems.at[6 * w + k],
                                                recv_sem=recv_sems.at[6 * w + k], device_id=to, device_id_type=MESH)

        sent = [copy(w, j, me, (*chip, c)) for w in range(nw) for j, chip in enumerate(chips)]
        for cp in sent:
            cp.start()
        for w in range(nw):
            for j, chip in enumerate(chips):
                copy(w, j, (*chip, c), me).wait_recv()
                passed = copy(w, 3 + j, (*chip, c), sibling)
                passed.start()
                sent.append(passed)
        for w in range(nw):
            for j, chip in enumerate(chips):
                copy(w, 3 + j, (*chip, 1 - c), me).wait_recv()
        for cp in sent:
            cp.wait_send()

    hbm = pl.BlockSpec(memory_space=pl.ANY)
    return pl.pallas_call(
        body, name="gather_weights", out_shape=[jax.ShapeDtypeStruct(w.shape, w.dtype) for w in stacks],
        in_specs=[hbm] * (nw + 1), out_specs=[hbm] * nw, input_output_aliases={i: i for i in range(nw)},
        scratch_shapes=[pltpu.SemaphoreType.DMA((6 * nw,)), pltpu.SemaphoreType.DMA((6 * nw,))],
        compiler_params=_cparams(),
    )(*stacks, after)


def _swap_halves(gs, name):
    nw = len(gs)

    def body(*refs):
        ins, gots = refs[:nw], refs[nw : 2 * nw]
        send_sems, recv_sems = refs[2 * nw :]
        x, y, c, _ = _place()
        cps = []
        for w in range(nw):
            half = ins[w].shape[1] // 2
            cps.append(pltpu.make_async_remote_copy(
                src_ref=ins[w].at[:, pl.ds((1 - c) * half, half), :], dst_ref=gots[w], send_sem=send_sems.at[w],
                recv_sem=recv_sems.at[w], device_id=(x, y, 1 - c), device_id_type=MESH))
            cps[-1].start()
        for cp in cps:
            cp.wait()

    hbm = pl.BlockSpec(memory_space=pl.ANY)
    return pl.pallas_call(
        body, name=name, out_shape=[jax.ShapeDtypeStruct((4, g.shape[1] // 2, g.shape[2]), g.dtype) for g in gs],
        in_specs=[hbm] * nw, out_specs=[hbm] * nw,
        scratch_shapes=[pltpu.SemaphoreType.DMA((nw,)), pltpu.SemaphoreType.DMA((nw,))], compiler_params=_cparams(),
    )(*gs)


def _split_start(name, bufs, plan, n, extra=()):
    n_plan = len(bufs)
    bufs = list(bufs) + list(extra)
    nb = len(bufs)

    def body(*refs):
        send_sems, recv_sems, token = refs[nb], refs[nb + 1], refs[-1]
        for k, (src, dst, to, _) in enumerate(plan(refs[:n_plan])):
            pltpu.make_async_remote_copy(src_ref=src, dst_ref=dst, send_sem=send_sems.at[k], recv_sem=recv_sems.at[k],
                                         device_id=to, device_id_type=MESH).start()
        token[...] = jnp.zeros(token.shape, token.dtype)

    hbm, sem = pl.BlockSpec(memory_space=pltpu.HBM), pl.BlockSpec(memory_space=pltpu.SEMAPHORE)
    res = pl.pallas_call(
        body, name=name,
        out_shape=(pltpu.SemaphoreType.DMA((n,)), pltpu.SemaphoreType.DMA((n,)), *[pltpu.HBM(b.shape, b.dtype) for b in bufs],
                   jax.ShapeDtypeStruct((SUBLANES, 128), F32)),
        in_specs=(hbm,) * nb, out_specs=(sem, sem) + (hbm,) * nb + (pl.BlockSpec(memory_space=pltpu.VMEM),),
        input_output_aliases={i: 2 + i for i in range(nb)},
        compiler_params=pltpu.CompilerParams(has_side_effects=pltpu.SideEffectType.DATAFLOW_SIDE_EFFECTING),
    )(*[pltpu.with_memory_space_constraint(b, pltpu.HBM) for b in bufs])
    return res[0], res[1], list(res[2 : 2 + nb]), res[-1]


def _split_wait(name, bufs, send_sems, recv_sems, after, plan):
    nb = len(bufs)

    def body(*refs):
        send, recv = refs[nb], refs[nb + 1]
        for k, (src, _, to, land) in enumerate(plan(refs[:nb])):
            cp = pltpu.make_async_remote_copy(src_ref=src, dst_ref=land, send_sem=send.at[k], recv_sem=recv.at[k],
                                              device_id=to, device_id_type=MESH)
            cp.wait_send()
            cp.wait_recv()

    hbm, sem = pl.BlockSpec(memory_space=pltpu.HBM), pl.BlockSpec(memory_space=pltpu.SEMAPHORE)
    return pl.pallas_call(
        body, name=name, out_shape=tuple(pltpu.HBM(b.shape, b.dtype) for b in bufs),
        in_specs=(hbm,) * nb + (sem, sem, pl.BlockSpec(memory_space=pl.ANY)), out_specs=(hbm,) * nb,
        input_output_aliases={i: i for i in range(nb)},
        compiler_params=pltpu.CompilerParams(has_side_effects=pltpu.SideEffectType.DATAFLOW_SIDE_EFFECTING),
    )(*bufs, send_sems, recv_sems, after)


def _indexed(name, fn, idx, ins, out, grid):
    def body(idx_ref, *refs):
        refs[-1][...] = fn(*[r[...] for r in refs[:-1]]).astype(refs[-1].dtype)

    (o_shape, o_dtype), o_block, o_map = out
    return pl.pallas_call(
        body, name=name, out_shape=jax.ShapeDtypeStruct(o_shape, o_dtype),
        grid_spec=pltpu.PrefetchScalarGridSpec(
            num_scalar_prefetch=1, grid=grid, in_specs=[pl.BlockSpec(blk, mp) for (_, blk, mp) in ins],
            out_specs=pl.BlockSpec(o_block, o_map)),
        compiler_params=_cparams(("arbitrary",) * len(grid)),
    )(idx, *[a for (a, _, _) in ins])


def _rstd(v):
    return lax.rsqrt(jnp.mean(v * v, axis=-1, keepdims=True) + EPS)


def _rms_bwd(dy, xh, r):
    return r * (dy - xh * jnp.mean(dy * xh, axis=-1, keepdims=True))


def _heads(v):
    return [v[:, h * HEAD_DIM : (h + 1) * HEAD_DIM] for h in range(v.shape[1] // HEAD_DIM)]


def _gelu(v):
    k = math.sqrt(2.0 / math.pi)
    return 0.5 * v * (1.0 + jnp.tanh(k * (v + 0.044715 * v * v * v)))


def _gelu_grad(v):
    k = math.sqrt(2.0 / math.pi)
    t = jnp.tanh(k * (v + 0.044715 * v * v * v))
    return 0.5 * (1.0 + t) + 0.5 * v * (1.0 - t * t) * k * (1.0 + 3 * 0.044715 * v * v)


def _sigmoid(v):
    return 1.0 / (1.0 + jnp.exp(-v))


def _slopes(n_heads):
    return [2.0 ** (-8.0 * (h + 1.0) / n_heads) for h in range(n_heads)]


def _band_iotas():
    ri = lax.broadcasted_iota(jnp.int32, (BAND, BAND), 0)
    ci = lax.broadcasted_iota(jnp.int32, (BAND, BAND), 1)
    return ri, ci


def _to_lanes(cols):
    rows = cols[0].shape[0]
    lane = lax.broadcasted_iota(jnp.int32, (rows, HEAD_DIM), 1)
    out = jnp.zeros((rows, HEAD_DIM), F32)
    for h, col in enumerate(cols):
        out = jnp.where(lane == h, col, out)
    return out


def _from_lanes(v, h, width):
    return jnp.broadcast_to(v[:, h : h + 1], (v.shape[0], width))


def _attn_fwd(name, q, k, v, bps, dil):
    s, aw = q.shape
    n_heads, nb = aw // HEAD_DIM, s // BAND
    scale = HEAD_DIM**-0.5
    slopes = _slopes(n_heads)

    assert nb % 2 == 0

    def body(q_ref, kp_ref, kc_ref, vp_ref, vc_ref, o_ref, l_ref):
        i = pl.program_id(0)
        ri, ci = _band_iotas()
        dist_c = (ri - ci).astype(F32)
        dist_p = (ri - ci + BAND).astype(F32)
        valid_c = ci <= ri
        hs = range(n_heads)
        sls = [slice(h * HEAD_DIM, (h + 1) * HEAD_DIM) for h in hs]
        lo, hi = slice(0, BAND), slice(BAND, 2 * BAND)
        chains = []
        for sub, rows in enumerate((lo, hi)):
            valid_p = ci >= ri + jnp.where((2 * i + sub) % bps == 0, BAND, 0)
            for h in hs:
                k_prev = kp_ref[:, sls[h]] if sub == 0 else kc_ref[lo, sls[h]]
                v_prev = vp_ref[:, sls[h]] if sub == 0 else vc_ref[lo, sls[h]]
                chains.append((q_ref[rows, sls[h]], kc_ref[rows, sls[h]], k_prev, vc_ref[rows, sls[h]], v_prev, h, valid_p))
        cs = range(len(chains))
        sc = [lax.dot_general(chains[c][0], chains[c][1], _NT, preferred_element_type=F32) for c in cs]
        sp = [lax.dot_general(chains[c][0], chains[c][2], _NT, preferred_element_type=F32) for c in cs]
        sc = [jnp.where(valid_c, sc[c] * scale - (slopes[chains[c][5]] * dil) * dist_c, NEG) for c in cs]
        sp = [jnp.where(chains[c][6], sp[c] * scale - (slopes[chains[c][5]] * dil) * dist_p, NEG) for c in cs]
        m = [jnp.maximum(jnp.max(sc[c], axis=1, keepdims=True), jnp.max(sp[c], axis=1, keepdims=True)) for c in cs]
        pc = [jnp.exp(sc[c] - m[c]) for c in cs]
        pp = [jnp.exp(sp[c] - m[c]) for c in cs]
        den = [jnp.sum(pc[c], axis=1, keepdims=True) + jnp.sum(pp[c], axis=1, keepdims=True) for c in cs]
        acc = [jnp.dot(pc[c].astype(BF16), chains[c][3], preferred_element_type=F32) for c in cs]
        acc = [acc[c] + jnp.dot(pp[c].astype(BF16), chains[c][4], preferred_element_type=F32) for c in cs]
        out = [jnp.concatenate([acc[c] / den[c] for c in cs[sub * n_heads : (sub + 1) * n_heads]], axis=1) for sub in range(2)]
        o_ref[...] = jnp.concatenate(out, axis=0).astype(o_ref.dtype)
        lse = [_to_lanes([m[c] + jnp.log(den[c]) for c in cs[sub * n_heads : (sub + 1) * n_heads]]) for sub in range(2)]
        l_ref[...] = jnp.concatenate(lse, axis=0)

    cur = pl.BlockSpec((2 * BAND, aw), lambda i: (i, 0))
    prev = pl.BlockSpec((BAND, aw), lambda i: (jnp.maximum(2 * i - 1, 0), 0))
    return pl.pallas_call(
        body, name=name, grid=(nb // 2,), in_specs=[cur, prev, cur, prev, cur],
        out_specs=[cur, pl.BlockSpec((2 * BAND, HEAD_DIM), lambda i: (i, 0))],
        out_shape=[jax.ShapeDtypeStruct((s, aw), BF16), jax.ShapeDtypeStruct((s, HEAD_DIM), F32)],
        compiler_params=_cparams(("arbitrary",)),
    )(q, k, k, v, v)


def _attn_bwd(name, q, k, v, do, lse, delta, bps, dil):
    s, aw = q.shape
    n_heads, nb = aw // HEAD_DIM, s // BAND
    scale = HEAD_DIM**-0.5
    slopes = _slopes(n_heads)

    assert nb % 2 == 0

    def body(qc_ref, qn_ref, doc_ref, don_ref, lc_ref, ln_ref, dc_ref, dn_ref, k_ref, v_ref, dq_ref, dk_ref, dv_ref, carry):
        i = pl.program_id(0)

        @pl.when(i == 0)
        def _():
            carry[...] = jnp.zeros(carry.shape, F32)

        ri, ci = _band_iotas()
        dist_c = (ri - ci).astype(F32)
        dist_p = (ri - ci + BAND).astype(F32)
        valid_c = ci <= ri
        valid_ab = ci >= ri + jnp.where((2 * i + 1) % bps != 0, 0, BAND)
        valid_bn = ci >= ri + jnp.where(jnp.logical_and((2 * i + 2) % bps != 0, 2 * i + 2 < nb), 0, BAND)
        carried = carry[...]
        hs = range(n_heads)
        sls = [slice(h * HEAD_DIM, (h + 1) * HEAD_DIM) for h in hs]
        lo, hi, al = slice(0, BAND), slice(BAND, 2 * BAND), slice(None)
        sides = [((qc_ref, doc_ref, lc_ref, dc_ref), lo, lo, dist_c, valid_c), ((qc_ref, doc_ref, lc_ref, dc_ref), hi, lo, dist_p, valid_ab),
                 ((qc_ref, doc_ref, lc_ref, dc_ref), hi, hi, dist_c, valid_c), ((qn_ref, don_ref, ln_ref, dn_ref), al, hi, dist_p, valid_bn)]
        res = []
        for (q_ref, do_ref, l_ref, d_ref), qr, kr, dist, valid in sides:
            qs, dos = [q_ref[qr, sl] for sl in sls], [do_ref[qr, sl] for sl in sls]
            ks, vs = [k_ref[kr, sl] for sl in sls], [v_ref[kr, sl] for sl in sls]
            sc = [lax.dot_general(qs[h], ks[h], _NT, preferred_element_type=F32) for h in hs]
            dp = [lax.dot_general(dos[h], vs[h], _NT, preferred_element_type=F32) for h in hs]
            lse_v, delta_v = l_ref[qr, :], d_ref[qr, :]
            p = [jnp.where(valid, jnp.exp(sc[h] * scale - (slopes[h] * dil) * dist - _from_lanes(lse_v, h, BAND)), 0.0) for h in hs]
            ds = [(p[h] * (dp[h] - _from_lanes(delta_v, h, BAND))).astype(BF16) for h in hs]
            pb = [p[h].astype(BF16) for h in hs]
            dv = [lax.dot_general(pb[h], dos[h], _TN, preferred_element_type=F32) for h in hs]
            dk = [lax.dot_general(ds[h], qs[h], _TN, preferred_element_type=F32) for h in hs]
            dq = [jnp.dot(ds[h], ks[h], preferred_element_type=F32) for h in hs]
            res.append((dq, dk, dv))
        (dq_aa, dk_aa, dv_aa), (dq_ba, dk_ba, dv_ba), (dq_bb, dk_bb, dv_bb), (dq_nb, dk_nb, dv_nb) = res
        cat = lambda parts: jnp.concatenate(parts, axis=1)
        dq_a = carried + cat(dq_aa) * scale
        dq_b = cat([dq_ba[h] + dq_bb[h] for h in hs]) * scale
        dq_ref[...] = jnp.concatenate([dq_a, dq_b], axis=0).astype(dq_ref.dtype)
        carry[...] = cat(dq_nb) * scale
        dk_a, dk_b = cat([dk_aa[h] + dk_ba[h] for h in hs]), cat([dk_bb[h] + dk_nb[h] for h in hs])
        dk_ref[...] = (jnp.concatenate([dk_a, dk_b], axis=0) * scale).astype(dk_ref.dtype)
        dv_a, dv_b = cat([dv_aa[h] + dv_ba[h] for h in hs]), cat([dv_bb[h] + dv_nb[h] for h in hs])
        dv_ref[...] = jnp.concatenate([dv_a, dv_b], axis=0).astype(dv_ref.dtype)

    cur = pl.BlockSpec((2 * BAND, aw), lambda i: (i, 0))
    nxt = pl.BlockSpec((BAND, aw), lambda i: (jnp.minimum(2 * i + 2, nb - 1), 0))
    cur1 = pl.BlockSpec((2 * BAND, HEAD_DIM), lambda i: (i, 0))
    nxt1 = pl.BlockSpec((BAND, HEAD_DIM), lambda i: (jnp.minimum(2 * i + 2, nb - 1), 0))
    return pl.pallas_call(
        body, name=name, grid=(nb // 2,), in_specs=[cur, nxt, cur, nxt, cur1, nxt1, cur1, nxt1, cur, cur], out_specs=[cur] * 3,
        out_shape=[jax.ShapeDtypeStruct((s, aw), BF16)] * 3, scratch_shapes=[pltpu.VMEM((BAND, aw), F32)],
        compiler_params=_cparams(("arbitrary",)),
    )(q, q, do, do, lse, lse, delta, delta, k, v)


STREAM_ROWS = BAND * max(dil for _, dil in DILATION_PATTERNS)


def _stream_orders(name, fn, srcs, vecs, n_vals, n_heads):
    s = srcs[0][0].shape[0]
    dils = [dil for _, dil in DILATION_PATTERNS]
    n_in = len(srcs) + len(vecs)

    def body(*refs):
        outs, scr = refs[n_in:-1], refs[-1]
        for vi, val in enumerate(fn(*[r[...] for r in refs[:n_in]])):
            scr[...] = val
            for di, dil in enumerate(dils):
                o = outs[vi * len(dils) + di]
                if dil == 1:
                    o[...] = val.astype(o.dtype)
                    continue
                for part in range(STREAM_ROWS // (dil * BAND)):
                    for r in range(dil):
                        rows = scr[pl.ds(part * dil * BAND + r, BAND, stride=dil), :]
                        o[r, part * BAND : (part + 1) * BAND, :] = rows.astype(o.dtype)

    in_specs = [pl.BlockSpec((STREAM_ROWS, HEAD_DIM), lambda b, h, cb=cb: (b, cb + h)) for (_, cb) in srcs]
    in_specs += [pl.BlockSpec(v.shape, lambda b, h, nd=v.ndim: (0,) * nd) for v in vecs]
    out_specs, out_shape = [], []
    for _ in range(n_vals):
        for dil in dils:
            if dil == 1:
                out_specs.append(pl.BlockSpec((STREAM_ROWS, HEAD_DIM), lambda b, h: (b, h)))
                out_shape.append(jax.ShapeDtypeStruct((s, n_heads * HEAD_DIM), BF16))
            else:
                out_specs.append(pl.BlockSpec((dil, STREAM_ROWS // dil, HEAD_DIM), lambda b, h: (0, b, h)))
                out_shape.append(jax.ShapeDtypeStruct((dil, s // dil, n_heads * HEAD_DIM), BF16))
    res = pl.pallas_call(
        body, name=name, grid=(s // STREAM_ROWS, n_heads), in_specs=in_specs, out_specs=out_specs, out_shape=out_shape,
        scratch_shapes=[pltpu.VMEM((STREAM_ROWS, HEAD_DIM), F32)], compiler_params=_cparams(("arbitrary", "arbitrary")),
    )(*[a for (a, _) in srcs], *vecs)
    res = [r.reshape(s, n_heads * HEAD_DIM) for r in res]
    return [res[vi * len(dils) : (vi + 1) * len(dils)] for vi in range(n_vals)]


def _natural_order(name, fn, vals, extras, out_dtypes, n_heads):
    s = vals[0][0].shape[0]
    dils = [dil for _, dil in DILATION_PATTERNS]
    n_in = len(vals) * len(dils) + len(extras)

    def body(*refs):
        outs, scr = refs[n_in:-1], refs[-1]
        blocks = []
        for vi in range(len(vals)):
            per_pattern = []
            for di, dil in enumerate(dils):
                x = refs[vi * len(dils) + di]
                if dil == 1:
                    per_pattern.append(x[...].astype(F32))
                    continue
                for part in range(STREAM_ROWS // (dil * BAND)):
                    for r in range(dil):
                        scr[pl.ds(part * dil * BAND + r, BAND, stride=dil), :] = x[r, part * BAND : (part + 1) * BAND, :].astype(F32)
                per_pattern.append(scr[...])
            blocks.append(per_pattern)
        res = fn(pl.program_id(1), blocks, *[r[...] for r in refs[len(vals) * len(dils) : n_in]])
        for o, v in zip(outs, res):
            o[...] = v.astype(o.dtype)

    in_specs, ins = [], []
    for per_pattern in vals:
        for arr, dil in zip(per_pattern, dils):
            if dil == 1:
                in_specs.append(pl.BlockSpec((STREAM_ROWS, HEAD_DIM), lambda b, h: (b, h)))
                ins.append(arr)
            else:
                in_specs.append(pl.BlockSpec((dil, STREAM_ROWS // dil, HEAD_DIM), lambda b, h: (0, b, h)))
                ins.append(arr.reshape(dil, s // dil, n_heads * HEAD_DIM))
    in_specs += [pl.BlockSpec((STREAM_ROWS, HEAD_DIM), lambda b, h: (b, 0)) for _ in extras]
    return pl.pallas_call(
        body, name=name, grid=(s // STREAM_ROWS, n_heads), in_specs=in_specs,
        out_specs=[pl.BlockSpec((STREAM_ROWS, HEAD_DIM), lambda b, h: (b, h)) for _ in out_dtypes],
        out_shape=[jax.ShapeDtypeStruct((s, n_heads * HEAD_DIM), dt) for dt in out_dtypes],
        scratch_shapes=[pltpu.VMEM((STREAM_ROWS, HEAD_DIM), F32)], compiler_params=_cparams(("arbitrary", "arbitrary")),
    )(*ins, *extras)


def _to_streams(v, dil):
    if dil == 1:
        return v
    s, w = v.shape
    return v.reshape(s // dil, dil, w).transpose(1, 0, 2).reshape(s, w)


def _from_streams(v, dil):
    if dil == 1:
        return v
    s, w = v.shape
    return v.reshape(dil, s // dil, w).transpose(1, 0, 2).reshape(s, w)


SCAN_LANES = 1024
SCAN_LANES_BWD = 512
SSM_ROWS = 128
SSM_FWD_ROWS = 256
FF_TM = 1024
FF_TN = 2048


def _cmul(ar, ai, br, bi):
    return ar * br - ai * bi, ar * bi + ai * br


def _cmul_conj(ar, ai, br, bi):
    return ar * br + ai * bi, ar * bi - ai * br


def _ssm_fwd(u_src, bre, bim, abre, abim, cre, cim, consts, d_skip):
    u_arr, sw, ucb = u_src
    s = u_arr.shape[0]
    nblk = bre.shape[0]
    nst = nblk * 512
    t = _blk(s, SSM_FWD_ROWS)
    w = _blk(nst, SCAN_LANES)

    def body(u_ref, bre_ref, bim_ref, abre_ref, abim_ref, cre_ref, cim_ref, k_ref, ds_ref, y_ref, hr_ref, hi_ref, xr, xi, car_r, car_i):
        @pl.when(pl.program_id(0) == 0)
        def _():
            car_r[...] = jnp.zeros(car_r.shape, F32)
            car_i[...] = jnp.zeros(car_i.shape, F32)

        u = u_ref[...].astype(F32)
        ub = u_ref[...].astype(BF16)
        row = lax.broadcasted_iota(jnp.int32, u.shape, 0)
        u1b = jnp.where(row % SUBLANES == 0, 0.0, pltpu.roll(u, 1, 0)).astype(BF16)
        for b in range(nblk):
            ch, st = slice(b * 128, (b + 1) * 128), slice(b * 512, (b + 1) * 512)
            xr[:, st] = jnp.dot(ub[:, ch], bre_ref[b], preferred_element_type=F32) + jnp.dot(u1b[:, ch], abre_ref[b], preferred_element_type=F32)
            xi[:, st] = jnp.dot(ub[:, ch], bim_ref[b], preferred_element_type=F32) + jnp.dot(u1b[:, ch], abim_ref[b], preferred_element_type=F32)
        for c in range(nst // w):
            ls = slice(c * w, (c + 1) * w)
            mult = [k_ref[i, :, ls] for i in range(8)]

            def step(i, carry, ls=ls, mult=mult):
                cr, ci = carry
                r0 = pl.multiple_of(i * SUBLANES, SUBLANES)
                hr, hi = xr[pl.ds(r0, SUBLANES), ls], xi[pl.ds(r0, SUBLANES), ls]
                for n, sh in ((1, 2), (2, 4)):
                    pr, pi = _cmul(mult[2 * n], mult[2 * n + 1], pltpu.roll(hr, sh, 0), pltpu.roll(hi, sh, 0))
                    hr, hi = hr + pr, hi + pi
                pr, pi = _cmul(mult[6], mult[7], cr, ci)
                hr, hi = hr + pr, hi + pi
                xr[pl.ds(r0, SUBLANES), ls] = hr
                xi[pl.ds(r0, SUBLANES), ls] = hi
                return jnp.broadcast_to(hr[7:8, :], hr.shape), jnp.broadcast_to(hi[7:8, :], hi.shape)

            cr, ci = lax.fori_loop(0, t // SUBLANES, step, (car_r[:, ls], car_i[:, ls]))
            car_r[:, ls] = cr
            car_i[:, ls] = ci
        hrb, hib = xr[...].astype(BF16), xi[...].astype(BF16)
        hr_ref[...] = hrb
        hi_ref[...] = hib
        for b in range(nblk):
            ch, st = slice(b * 128, (b + 1) * 128), slice(b * 512, (b + 1) * 512)
            yb = jnp.dot(hrb[:, st], cre_ref[b], preferred_element_type=F32)
            yb -= jnp.dot(hib[:, st], cim_ref[b], preferred_element_type=F32)
            y_ref[:, ch] = yb + ds_ref[:, ch] * u[:, ch]

    whole = lambda a: pl.BlockSpec(a.shape, lambda i, nd=a.ndim: (0,) * nd)
    return pl.pallas_call(
        body, name="ssm_fwd", grid=(s // t,),
        in_specs=[pl.BlockSpec((t, sw), lambda i: (i, ucb))] + [whole(a) for a in (bre, bim, abre, abim, cre, cim, consts, d_skip)],
        out_specs=[pl.BlockSpec((t, sw), lambda i: (i, 0)), pl.BlockSpec((t, nst), lambda i: (i, 0)), pl.BlockSpec((t, nst), lambda i: (i, 0))],
        out_shape=[jax.ShapeDtypeStruct((s, sw), F32), jax.ShapeDtypeStruct((s, nst), BF16), jax.ShapeDtypeStruct((s, nst), BF16)],
        scratch_shapes=[pltpu.VMEM((t, nst), F32), pltpu.VMEM((t, nst), F32), pltpu.VMEM((SUBLANES, nst), F32), pltpu.VMEM((SUBLANES, nst), F32)],
        compiler_params=_cparams(("arbitrary",)),
    )(u_arr, bre, bim, abre, abim, cre, cim, consts, d_skip)


def _ssm_bwd(dy, u_src, hr, hi, ctre, ctim, acre, acim, btre, btim, consts, d_skip):
    u_arr, sw, ucb = u_src
    s = dy.shape[0]
    nblk = ctre.shape[0]
    nst = nblk * 512
    t = _blk(s, SSM_ROWS)
    w = _blk(nst, SCAN_LANES_BWD)
    nt = s // t

    def body(dy_ref, u_ref, hr_ref, hi_ref, ctre_ref, ctim_ref, acre_ref, acim_ref, btre_ref, btim_ref, k_ref, ds_ref,
             du_ref, db_ref, dc_ref, da_ref, dd_ref, lr, li, hfr, hfi, car_r, car_i):
        @pl.when(pl.program_id(0) == 0)
        def _():
            car_r[...] = jnp.zeros(car_r.shape, F32)
            car_i[...] = jnp.zeros(car_i.shape, F32)
            db_ref[...] = jnp.zeros(db_ref.shape, F32)
            dc_ref[...] = jnp.zeros(dc_ref.shape, F32)
            da_ref[...] = jnp.zeros(da_ref.shape, F32)
            dd_ref[...] = jnp.zeros(dd_ref.shape, F32)

        dyv, u = dy_ref[...].astype(F32), u_ref[...].astype(F32)
        dyb, ub = dyv.astype(BF16), u.astype(BF16)
        hrb, hib = hr_ref[...], hi_ref[...]
        hfr[...] = hrb.astype(F32)
        hfi[...] = hib.astype(F32)
        row = lax.broadcasted_iota(jnp.int32, dyv.shape, 0)
        dy1b = jnp.where(row % SUBLANES == SUBLANES - 1, 0.0, pltpu.roll(dyv, t - 1, 0)).astype(BF16)
        for b in range(nblk):
            ch, st = slice(b * 128, (b + 1) * 128), slice(b * 512, (b + 1) * 512)
            lr[:, st] = jnp.dot(dyb[:, ch], ctre_ref[b], preferred_element_type=F32) + jnp.dot(dy1b[:, ch], acre_ref[b], preferred_element_type=F32)
            li[:, st] = -(jnp.dot(dyb[:, ch], ctim_ref[b], preferred_element_type=F32) + jnp.dot(dy1b[:, ch], acim_ref[b], preferred_element_type=F32))
        last_row = lax.broadcasted_iota(jnp.int32, (SUBLANES, w), 0) == SUBLANES - 1
        for c in range(nst // w):
            ls = slice(c * w, (c + 1) * w)
            mult = [k_ref[i, :, ls] for i in range(8)]

            def step(i, carry, ls=ls, mult=mult):
                cr, ci, ar, ai = carry
                r0 = pl.multiple_of((t // SUBLANES - 1 - i) * SUBLANES, SUBLANES)
                gr, gi = lr[pl.ds(r0, SUBLANES), ls], li[pl.ds(r0, SUBLANES), ls]
                for n, sh in ((1, 2), (2, 4)):
                    pr, pi = _cmul_conj(mult[2 * n], mult[2 * n + 1], pltpu.roll(gr, SUBLANES - sh, 0), pltpu.roll(gi, SUBLANES - sh, 0))
                    gr, gi = gr + pr, gi + pi
                pr, pi = _cmul_conj(mult[6], mult[7], cr, ci)
                gr, gi = gr + pr, gi + pi
                lr[pl.ds(r0, SUBLANES), ls] = gr
                li[pl.ds(r0, SUBLANES), ls] = gi
                nr = jnp.where(last_row, cr, pltpu.roll(gr, SUBLANES - 1, 0))
                ni = jnp.where(last_row, ci, pltpu.roll(gi, SUBLANES - 1, 0))
                fr, fi = hfr[pl.ds(r0, SUBLANES), ls], hfi[pl.ds(r0, SUBLANES), ls]
                pr, pi = _cmul_conj(fr, fi, nr, ni)
                return (jnp.broadcast_to(gr[0:1, :], gr.shape), jnp.broadcast_to(gi[0:1, :], gi.shape), ar + pr, ai + pi)

            zero = jnp.zeros((SUBLANES, w), F32)
            cr, ci, ar, ai = lax.fori_loop(0, t // SUBLANES, step, (car_r[:, ls], car_i[:, ls], zero, zero))
            car_r[:, ls] = cr
            car_i[:, ls] = ci
            da_ref[0, :, ls] += ar
            da_ref[1, :, ls] += ai
        lrb, lib = lr[...].astype(BF16), li[...].astype(BF16)
        for b in range(nblk):
            ch, st = slice(b * 128, (b + 1) * 128), slice(b * 512, (b + 1) * 512)
            dub = jnp.dot(lrb[:, st], btre_ref[b], preferred_element_type=F32)
            dub += jnp.dot(lib[:, st], btim_ref[b], preferred_element_type=F32)
            du_ref[:, ch] = (dub + ds_ref[:, ch] * dyv[:, ch]).astype(du_ref.dtype)
            db_ref[0, b] += lax.dot_general(ub[:, ch], lrb[:, st], _TN, preferred_element_type=F32)
            db_ref[1, b] += lax.dot_general(ub[:, ch], lib[:, st], _TN, preferred_element_type=F32)
            dc_ref[0, b] += lax.dot_general(dyb[:, ch], hrb[:, st], _TN, preferred_element_type=F32)
            dc_ref[1, b] -= lax.dot_general(dyb[:, ch], hib[:, st], _TN, preferred_element_type=F32)
        dd_ref[...] += _colsum8(dyv * u)

    whole = lambda a: pl.BlockSpec(a.shape, lambda i, nd=a.ndim: (0,) * nd)
    rev = lambda wd, cb=0: pl.BlockSpec((t, wd), lambda i, cb=cb: (nt - 1 - i, cb))
    acc = lambda sh: pl.BlockSpec(sh, lambda i, nd=len(sh): (0,) * nd)
    db_shape, da_shape, dd_shape = (2, nblk, 128, 512), (2, SUBLANES, nst), (SUBLANES, sw)
    return pl.pallas_call(
        body, name="ssm_bwd", grid=(nt,),
        in_specs=[rev(sw), rev(sw, ucb), rev(nst), rev(nst)] + [whole(a) for a in (ctre, ctim, acre, acim, btre, btim, consts, d_skip)],
        out_specs=[rev(sw), acc(db_shape), acc(db_shape), acc(da_shape), acc(dd_shape)],
        out_shape=[jax.ShapeDtypeStruct((s, sw), BF16), jax.ShapeDtypeStruct(db_shape, F32), jax.ShapeDtypeStruct(db_shape, F32),
                   jax.ShapeDtypeStruct(da_shape, F32), jax.ShapeDtypeStruct(dd_shape, F32)],
        scratch_shapes=[pltpu.VMEM((t, nst), F32)] * 4 + [pltpu.VMEM((SUBLANES, nst), F32)] * 2,
        compiler_params=_cparams(("arbitrary",)),
    )(dy, u_arr, hr, hi, ctre, ctim, acre, acim, btre, btim, consts, d_skip)


def _ssm_discretise(lam_re, lam_im, log_step):
    step = jnp.exp(log_step)
    e = jnp.exp(lam_re * step)
    ar, ai = e * jnp.cos(lam_im * step), e * jnp.sin(lam_im * step)
    den = lam_re * lam_re + lam_im * lam_im
    inv_r, inv_i = lam_re / den, -lam_im / den
    fr, fi = _cmul(ar - 1.0, ai, inv_r, inv_i)
    return step, ar, ai, inv_r, inv_i, fr, fi


def _ssm_prep_fn(lam_re, lam_im, log_step):
    _, ar, ai, _, _, fr, fi = _ssm_discretise(lam_re, lam_im, log_step)
    pw = [(ar, ai)]
    for n in range(1, 8):
        pw.append(_cmul(*pw[n - 1], ar, ai))
    return [p[0] for p in pw] + [p[1] for p in pw] + [fr, fi]


def _ssm_param_grad_fn(lam_re, lam_im, log_step, da_r, da_i, df_r, df_i):
    step, ar, ai, inv_r, inv_i, fr, fi = _ssm_discretise(lam_re, lam_im, log_step)
    pr, pi = _cmul_conj(inv_r, inv_i, df_r, df_i)
    dat_r, dat_i = da_r + pr, da_i + pi
    wr, wi = _cmul(fr, fi, inv_r, inv_i)
    dl_r, dl_i = _cmul_conj(-wr, -wi, df_r, df_i)
    dz_r, dz_i = _cmul_conj(ar, ai, dat_r, dat_i)
    dl_r, dl_i = dl_r + step * dz_r, dl_i + step * dz_i
    dstep = jnp.sum(dz_r * lam_re + dz_i * lam_im, axis=-1, keepdims=True)
    return [dl_r, dl_i, step * dstep]


def _block_diag(v, rows_first):
    g, a, b = v.shape
    nb = g // GROUPS_PER_BLOCK
    eye = jnp.eye(GROUPS_PER_BLOCK, dtype=v.dtype)
    v = v.reshape(nb, GROUPS_PER_BLOCK, a, 1, b) * eye[None, :, None, :, None]
    return v.reshape(nb, GROUPS_PER_BLOCK * a, GROUPS_PER_BLOCK * b)


def _block_diag_take(v, a, b):
    nb = v.shape[0]
    v = v.reshape(nb, GROUPS_PER_BLOCK, a, GROUPS_PER_BLOCK, b)
    return jnp.stack([v[:, g, :, g, :] for g in range(GROUPS_PER_BLOCK)], axis=1).reshape(nb * GROUPS_PER_BLOCK, a, b)


def _adamw(w, g, m, v):
    m = ADAM_B1 * m + (1.0 - ADAM_B1) * g
    v = ADAM_B2 * v + (1.0 - ADAM_B2) * (g * g)
    m_hat = m / (1.0 - ADAM_B1**ADAM_STEP)
    v_hat = v / (1.0 - ADAM_B2**ADAM_STEP)
    delta = -ADAM_LR * (m_hat / (jnp.sqrt(v_hat) + ADAM_EPS) + ADAM_WD * w)
    return delta, m, v


def _adamw_rows(name, w, g, m, v):
    wd = w.shape[1]
    return _rowwise(name, lambda a, b, c, d: (list(_adamw(a, b, c, d)), []), [w, g, m, v], [], [(wd, F32)] * 3, [], tm=128)


SMALL = ["b_ada", "norm1_g", "q_norm_g", "k_norm_g", "lam_re", "lam_im", "log_step", "b_re", "b_im", "c_re", "c_im",
         "d_skip", "b_glu", "attn_out_g", "ssm_out_g", "norm2_g"]
LARGE = ["w_in", "w_glu", "w_out", "w_ff1", "w_ff2"]
ORDER = ["w_ada", "b_ada", "norm1_g", "w_in", "q_norm_g", "k_norm_g", "lam_re", "lam_im", "log_step", "b_re", "b_im", "c_re",
         "c_im", "d_skip", "w_glu", "b_glu", "attn_out_g", "ssm_out_g", "w_out", "norm2_g", "w_ff1", "w_ff2"]


def _pack(arrs):
    flat = jnp.concatenate([a.reshape(-1) for a in arrs])
    rows = -(-flat.shape[0] // PACK_LANES)
    rows = -(-rows // SUBLANES) * SUBLANES
    return jnp.pad(flat, (0, rows * PACK_LANES - flat.shape[0])).reshape(rows, PACK_LANES)


def _unpack(packed, shapes):
    flat, out, o = packed.reshape(-1), [], 0
    for sh in shapes:
        n = int(np.prod(sh))
        out.append(flat[o : o + n].reshape(sh))
        o += n
    return out


def kernel(x, c, w_ada, b_ada, norm1_g, w_in, q_norm_g, k_norm_g, lam_re, lam_im, log_step, b_re, b_im, c_re, c_im, d_skip, w_glu, b_glu, attn_out_g, ssm_out_g, w_out, norm2_g, w_ff1, w_ff2, loss_target, m_w_ada, m_b_ada, m_norm1_g, m_w_in, m_q_norm_g, m_k_norm_g, m_lam_re, m_lam_im, m_log_step, m_b_re, m_b_im, m_c_re, m_c_im, m_d_skip, m_w_glu, m_b_glu, m_attn_out_g, m_ssm_out_g, m_w_out, m_norm2_g, m_w_ff1, m_w_ff2, v_w_ada, v_b_ada, v_norm1_g, v_w_in, v_q_norm_g, v_k_norm_g, v_lam_re, v_lam_im, v_log_step, v_b_re, v_b_im, v_c_re, v_c_im, v_d_skip, v_w_glu, v_b_glu, v_attn_out_g, v_ssm_out_g, v_w_out, v_norm2_g, v_w_ff1, v_w_ff2):
    args = dict(locals())
    wts = {n: args[n] for n in ORDER}
    mom = {n: args["m_" + n] for n in ORDER}
    var = {n: args["v_" + n] for n in ORDER}

    x2, tgt = x[0], loss_target[0]
    s, d = x2.shape
    aw = d // 2
    sw = d - aw
    n_groups = sw // SSM_GROUP
    nst = n_groups * STATE_DIM
    chip = 2 * lax.axis_index("x") + lax.axis_index("y")

    chip_idx = jnp.reshape(chip, (1,)).astype(jnp.int32)
    core_idx = jnp.reshape(lax.axis_index("c"), (1,)).astype(jnp.int32)

    def cast_into_stack(n):
        w2 = wts[n][0]
        rows, cols = w2.shape
        tr = _blk(rows, 256)
        return _indexed("cast_" + n, lambda a: a, chip_idx, [(w2, (tr, cols), lambda r, ix: (r, 0))],
                        (((4, rows, cols), BF16), (None, tr, cols), lambda r, ix: (ix[0], r, 0)), (rows // tr,))

    stacks = {n: cast_into_stack(n) for n in LARGE}
    dff = 4 * wts["w_ff2"].shape[1]

    c_all = _allgather8("gather_c", jnp.pad(c, ((0, SUBLANES - 1), (0, 0))))[::SUBLANES]
    n_mod = w_ada.shape[2]
    b_ada_mine = lax.dynamic_slice(b_ada, (0, chip * n_mod), (1, n_mod))
    sil_c = _single("silu_c", lambda v: [v * _sigmoid(v)], [c_all], [(c_all.shape, F32)])[0]
    mod_part = _mm("ada_fwd", sil_c, w_ada[0], "nn", F32, tn=512, epi=lambda acc, b: acc + b[0:1, :],
                   extras=[jnp.broadcast_to(b_ada_mine, (SUBLANES, n_mod))])
    mod_all = _allgather8("gather_mod", mod_part)
    me = 2 * chip + lax.axis_index("c")
    mod = jnp.concatenate([lax.dynamic_slice(mod_all, (16 * k + me, 0), (1, n_mod)) for k in range(4)], axis=1)
    sh1, sc1, g1, sh2, sc2, g2 = [mod[:, i * d : (i + 1) * d] for i in range(6)]

    (w_in_s,) = _gather_weights([stacks["w_in"]], mod_all)

    def gather_plan(refs):
        px0, py0, pc, chips = _place()
        slot = 2 * px0 + py0
        return [(r.at[slot], r.at[slot], (px, py, pc), r.at[2 * px + py]) for r in refs for (px, py) in chips]

    mix_sems_s, mix_sems_r, mix_stacks, mix_token = _split_start(
        "gather_mix_start", [stacks["w_glu"], stacks["w_out"]], gather_plan, 6, extra=[w_in_s])
    w_in_s = mix_stacks.pop()
    ff_sems_s, ff_sems_r, ff_stacks, ff_token = _split_start(
        "gather_ff_start", [stacks["w_ff1"], stacks["w_ff2"]], gather_plan, 6, extra=[w_in_s])
    w_in_s = ff_stacks.pop()
    sc1 = sc1 + (mix_token[0:1, 0:1] + ff_token[0:1, 0:1])

    def norm_mod(v, g, sc, sh):
        return [v * _rstd(v) * g * (1.0 + sc) + sh], []

    (h1b,) = _rowwise("norm1", norm_mod, [x2], [norm1_g, sc1, sh1], [(d, BF16)], [])
    proj = _mm("proj_in", h1b, w_in_s, "nn", BF16, b_cs=True, tm=FF_TM)
    assert aw == sw
    q_src, k_src, v_src, u_src = [(proj, aw, i) for i in range(3)] + [(proj, sw, 3)]

    n_heads = aw // HEAD_DIM
    up = lambda v: v.astype(F32)
    qk_orders = _stream_orders("qk_prep", lambda q, k, v, gq, gk: [up(q) * _rstd(up(q)) * gq, up(k) * _rstd(up(k)) * gk, up(v)],
                               [(proj, 0), (proj, n_heads), (proj, 2 * n_heads)], [q_norm_g, k_norm_g], 3, n_heads)

    pats = []
    for pi, (window, dil) in enumerate(DILATION_PATTERNS):
        assert window // dil == BAND and s % STREAM_ROWS == 0
        bps = s // dil // BAND
        qp, kp, vp = [order[pi] for order in qk_orders]
        o_p, l_p = _attn_fwd(f"attn_fwd_d{dil}", qp, kp, vp, bps, dil)
        pats.append((dil, bps, qp, kp, vp, o_p, _from_streams(l_p, dil)))

    def lse_total(l1, l2, l3):
        m = jnp.maximum(jnp.maximum(l1, l2), l3)
        return m + jnp.log(jnp.exp(l1 - m) + jnp.exp(l2 - m) + jnp.exp(l3 - m))

    (lse,) = _rowwise("attn_lse", lambda l1, l2, l3: ([lse_total(l1, l2, l3)], []), [p[6] for p in pats], [], [(HEAD_DIM, F32)], [])

    def combine(h, blocks, l1, l2, l3):
        tot = lse_total(l1, l2, l3)
        lane = lax.broadcasted_iota(jnp.int32, l1.shape, 1)
        mix = 0.0
        for o_blk, l in zip(blocks[0], (l1, l2, l3)):
            w = jnp.sum(jnp.where(lane == h, jnp.exp(l - tot), 0.0), axis=1, keepdims=True)
            mix = mix + w * o_blk
        return [mix]

    (attn,) = _natural_order("attn_mix", combine, [[p[5] for p in pats]], [p[6] for p in pats], [F32], n_heads)

    lam_re2, lam_im2, log_step2 = lam_re[0], lam_im[0], log_step[0].reshape(n_groups, 1)
    prep = _single("ssm_prep", _ssm_prep_fn, [lam_re2, lam_im2, log_step2], [((n_groups, STATE_DIM), F32)] * 18)
    p_r = jnp.stack([p.reshape(nst) for p in prep[:8]])
    p_i = jnp.stack([p.reshape(nst) for p in prep[8:16]])
    f_ri = jnp.stack(prep[16:])
    row = jnp.arange(SUBLANES)[:, None]
    zero = jnp.zeros((SUBLANES, nst), F32)
    sel = lambda cond, v: jnp.where(cond, jnp.broadcast_to(v[None, :], (SUBLANES, nst)), zero)
    fwd_consts = jnp.stack([sel(row >= 1, p_r[0]), sel(row >= 1, p_i[0]), sel(row >= 2, p_r[1]), sel(row >= 2, p_i[1]),
                            sel(row >= 4, p_r[3]), sel(row >= 4, p_i[3]), p_r, p_i])
    bwd_consts = jnp.stack([sel(row <= 6, p_r[0]), sel(row <= 6, p_i[0]), sel(row <= 5, p_r[1]), sel(row <= 5, p_i[1]),
                            sel(row <= 3, p_r[3]), sel(row <= 3, p_i[3]), p_r[::-1], p_i[::-1]])
    nblk = n_groups // GROUPS_PER_BLOCK
    f_rows = f_ri.reshape(2, nblk, 1, 512)
    braw_r = _block_diag(b_re[0].transpose(0, 2, 1), True)
    braw_i = _block_diag(b_im[0].transpose(0, 2, 1), True)

    craw_r, craw_i = _block_diag(c_re[0], True), _block_diag(c_im[0], True)
    a_rows = jnp.stack([p_r[0], p_i[0]]).reshape(2, nblk, 1, 512)

    def bbar_fn(br, bi, f, cr, ci, a):
        r, i = _cmul(f[0], f[1], br, bi)
        ar, ai = _cmul(a[0], a[1], r, i)
        acr, aci = _cmul(a[0], a[1], cr, ci)
        return [r, i, ar, ai, acr, aci]

    bbar_r, bbar_i, abbar_r, abbar_i, ac_r, ac_i = _single(
        "ssm_bbar", bbar_fn, [braw_r, braw_i, f_rows, craw_r, craw_i, a_rows], [(braw_r.shape, BF16)] * 6)
    ct_r, ct_i = craw_r.astype(BF16), craw_i.astype(BF16)
    ds2 = d_skip
    y_ssm, h_r, h_i = _ssm_fwd(u_src, bbar_r, bbar_i, abbar_r, abbar_i, ct_r.transpose(0, 2, 1), ct_i.transpose(0, 2, 1), fwd_consts, ds2)

    gelu_b = lambda v: _gelu(v).astype(BF16)
    w_glu_s, w_out_s = _split_wait("gather_mix_wait", mix_stacks, mix_sems_s, mix_sems_r, y_ssm, gather_plan)
    w_glu_f = w_glu_s.reshape(sw, sw)
    w_out_f = w_out_s.reshape(d, d)
    z = _mm("glu_fwd", y_ssm, w_glu_f, "nn", BF16, a_pro=gelu_b)

    def glu(ys, zz, b):
        return [_gelu(ys) * _sigmoid(zz.astype(F32) + b)], []

    (ssm,) = _rowwise("glu", glu, [y_ssm, z], [b_glu], [(sw, F32)], [])

    def cat_norm(a, sm, ga, gs):
        return [jnp.concatenate([a * _rstd(a) * ga, sm * _rstd(sm) * gs], axis=1)], []

    (catb,) = _rowwise("cat_norm", cat_norm, [attn, ssm], [attn_out_g, ssm_out_g], [(d, BF16)], [])
    mixed = _mm("mix_out", catb, w_out_f, "nn", BF16, tm=FF_TM, tn=FF_TN)

    def resid_norm(xv, mx, g1v, g, sc, sh):
        x1v = xv + g1v * mx.astype(F32)
        return [x1v, x1v * _rstd(x1v) * g * (1.0 + sc) + sh], []

    x1, h2b = _rowwise("resid_norm2", resid_norm, [x2, mixed], [g1, norm2_g, sc2, sh2], [(d, F32), (d, BF16)], [])
    w_ff1_s, w_ff2_s = _split_wait("gather_ff_wait", ff_stacks, ff_sems_s, ff_sems_r, x1, gather_plan)
    w_ff2_f = w_ff2_s.reshape(dff, d)
    a_ff = _mm("ff1", h2b, w_ff1_s, "nn", BF16, b_cs=True, tm=FF_TM, tn=FF_TN)
    sq_relu = lambda v: jnp.square(jnp.maximum(v.astype(F32), 0.0)).astype(BF16)
    ff = _mm("ff2", a_ff, w_ff2_f, "nn", BF16, a_pro=sq_relu, tm=FF_TM)

    def loss_fn(x1v, ffv, tg, g2v):
        ffv = ffv.astype(F32)
        diff = x1v + g2v * ffv - tg
        dout = diff * (1.0 / d)
        return [dout, dout * g2v], [_colsum8(0.5 * diff * dout), _colsum8(dout * ffv)]

    dout, dffb, loss_acc, dg2_acc = _rowwise("loss", loss_fn, [x1, ff, tgt], [g2], [(d, F32), (d, BF16)], [(SUBLANES, d)] * 2)

    da_ff = _mm("ff2_bwd", dffb, w_ff2_f, "nt", BF16, epi=lambda acc, av: acc * (2.0 * jnp.maximum(av.astype(F32), 0.0)), extras=[a_ff], tm=FF_TM, tn=FF_TN)
    gw = {}
    gw["w_ff2"] = _mm("ff2_wgrad", a_ff, dffb, "tn", BF16, a_pro=sq_relu, tm=FF_TM).reshape(4, dff // 4, d)
    gw["w_ff1"] = _mm("ff1_wgrad", h2b, da_ff, "tn", BF16, o_cs=True, tm=FF_TM)

    def add2(n, g, b):
        _, half, cols = b.shape
        tr = _blk(half, 256)
        nrb = half // tr
        blk = (None, tr, cols)
        return _indexed("grad_add2_" + n, lambda p, q: p.astype(F32) + q.astype(F32), core_idx,
                        [(g, blk, lambda k, r, ix: (k, ix[0] * nrb + r, 0)), (b, blk, lambda k, r, ix: (k, r, 0))],
                        ((b.shape, BF16), blk, lambda k, r, ix: (k, r, 0)), (4, nrb))

    def scatter_plan(refs):
        px0, py0, pc, chips = _place()
        nw = len(refs) // 2
        return [(refs[w].at[2 * px + py], refs[nw + w].at[j], (px, py, pc), refs[nw + w].at[j])
                for w in range(nw) for j, (px, py) in enumerate(chips)]

    def swap_plan(refs):
        px0, py0, pc, _ = _place()
        nw = len(refs) // 2
        plan = []
        for w in range(nw):
            half = refs[w].shape[1] // 2
            plan.append((refs[w].at[:, pl.ds((1 - pc) * half, half), :], refs[nw + w], (px0, py0, 1 - pc), refs[nw + w]))
        return plan

    early = ["w_ff2", "w_ff1"]
    sw_lands = [lax.empty((4, gw[n].shape[1] // 2, gw[n].shape[2]), BF16) for n in early]
    sw_sems_s, sw_sems_r, sw_bufs, sw_token = _split_start("swap_ff_start", [gw[n] for n in early] + sw_lands, swap_plan, len(early))
    sc2_late = sc2 + sw_token[0:1, 0:1]

    dh2 = _mm("ff1_bwd", da_ff, w_ff1_s, "nt", BF16, b_cs=True, tm=FF_TM)

    def norm2_bwd(dh, x1v, do, mx, g, sc, g1v):
        dh, mx = dh.astype(F32), mx.astype(F32)
        r = _rstd(x1v)
        xh = x1v * r
        dn = dh * (1.0 + sc)
        dx = do + _rms_bwd(dn * g, xh, r)
        return [dx, dx * g1v], [_colsum8(dh * xh * g), _colsum8(dh), _colsum8(dn * xh), _colsum8(dx * mx)]

    dx1, dmixb, dsc2_acc, dsh2_acc, dn2g_acc, dg1_acc = _rowwise(
        "norm2_bwd", norm2_bwd, [dh2, x1, dout, mixed], [norm2_g, sc2_late, g1], [(d, F32), (d, BF16)], [(SUBLANES, d)] * 4)

    sw_bufs = _split_wait("swap_ff_wait", sw_bufs, sw_sems_s, sw_sems_r, dmixb, swap_plan)
    parts = {n: add2(n, g, b) for n, g, b in zip(early, sw_bufs[: len(early)], sw_bufs[len(early) :])}
    lands = [lax.empty((3,) + parts[n].shape[1:], BF16) for n in early]
    sc_sems_s, sc_sems_r, sc_bufs, sc_token = _split_start("scatter_ff_start", [parts[n] for n in early] + lands, scatter_plan, 6)
    attn_out_g_late = attn_out_g + sc_token[0:1, 0:1]

    dcat = _mm("mix_out_bwd", dmixb, w_out_f, "nt", BF16, tm=FF_TM, tn=FF_TN)
    gw["w_out"] = _mm("mix_out_wgrad", catb, dmixb, "tn", BF16, tm=FF_TM).reshape(4, d // 4, d)

    def cat_bwd(da, dsm, a, sm, ga, gs):
        da, dsm = da.astype(F32), dsm.astype(F32)
        ra, rs = _rstd(a), _rstd(sm)
        ah, sh = a * ra, sm * rs
        return [_rms_bwd(da * ga, ah, ra), _rms_bwd(dsm * gs, sh, rs)], [_colsum8(da * ah), _colsum8(dsm * sh)]

    dattn, dssm, dga_acc, dgs_acc = _rowwise(
        "cat_norm_bwd", cat_bwd, [(dcat, aw, 0), (dcat, sw, 1), attn, ssm], [attn_out_g_late, ssm_out_g], [(aw, F32), (sw, F32)],
        [(SUBLANES, aw), (SUBLANES, sw)])

    def glu_bwd(dsm, ys, zz, b):
        sg = _sigmoid(zz.astype(F32) + b)
        dz = dsm * _gelu(ys) * sg * (1.0 - sg)
        return [dz, dsm * sg], [_colsum8(dz)]

    dzb, dy_direct, dbglu_acc = _rowwise("glu_bwd", glu_bwd, [dssm, y_ssm, z], [b_glu], [(sw, BF16), (sw, BF16)], [(SUBLANES, sw)])
    gw["w_glu"] = _mm("glu_wgrad", y_ssm, dzb, "tn", BF16, a_pro=gelu_b).reshape(4, sw // 4, sw)
    dys = _mm("glu_bwd_mm", dzb, w_glu_f, "nt", BF16, epi=lambda acc, direct, ys: (acc + direct.astype(F32)) * _gelu_grad(ys), extras=[dy_direct, y_ssm])
    du, db_bar, dc_bd, da_acc, dds_acc = _ssm_bwd(dys, u_src, h_r, h_i, ct_r, ct_i, ac_r, ac_i, bbar_r.transpose(0, 2, 1),
                                                  bbar_i.transpose(0, 2, 1), bwd_consts, ds2)

    def delta_fn(da, a):
        return [_to_lanes([jnp.sum(p * q, axis=1, keepdims=True) for p, q in zip(_heads(da), _heads(a))])], []

    (delta,) = _rowwise("attn_delta", delta_fn, [dattn, attn], [], [(HEAD_DIM, F32)], [])
    (do_orders,) = _stream_orders("attn_do", lambda da: [da], [(dattn, 0)], [], 1, n_heads)
    dqs, dks, dvs = [], [], []
    for pi, (dil, bps, qp, kp, vp, _, _) in enumerate(pats):
        lsp, dlp = [_to_streams(t, dil) for t in (lse, delta)]
        dq_p, dk_p, dv_p = _attn_bwd(f"attn_bwd_d{dil}", qp, kp, vp, do_orders[pi], lsp, dlp, bps, dil)
        dqs.append(dq_p)
        dks.append(dk_p)
        dvs.append(dv_p)
    dq_t, dk_t, dv_t = _natural_order("attn_grad_sum", lambda h, blocks: [(b[0] + b[1]) + b[2] for b in blocks],
                                      [dqs, dks, dvs], [], [BF16] * 3, n_heads)

    def qk_bwd(dq_sum, dk_sum, dv_sum, duv, q, k, gq, gk):
        def one(dn, raw, g):
            outs, acc = [], 0.0
            for dh, h in zip(_heads(dn), _heads(raw)):
                r = _rstd(h)
                hh = h * r
                outs.append(_rms_bwd(dh * g, hh, r))
                acc = acc + _colsum8(dh * hh)
            return jnp.concatenate(outs, axis=1), acc

        dq, gq_acc = one(dq_sum.astype(F32), q.astype(F32), gq)
        dk, gk_acc = one(dk_sum.astype(F32), k.astype(F32), gk)
        return [jnp.concatenate([dq, dk, dv_sum.astype(F32), duv.astype(F32)], axis=1)], [gq_acc, gk_acc]

    dprojb, dgq_acc, dgk_acc = _rowwise("qk_norm_bwd", qk_bwd, [dq_t, dk_t, dv_t, du, q_src, k_src], [q_norm_g, k_norm_g],
                                        [(3 * aw + sw, BF16)], [(SUBLANES, HEAD_DIM)] * 2)
    gw["w_in"] = _mm("proj_in_wgrad", h1b, dprojb, "tn", BF16, o_cs=True, tm=FF_TM)
    late = [n for n in LARGE if n not in early]
    parts.update({n: add2(n, gw[n], b) for n, b in zip(late, _swap_halves([gw[n] for n in late], "swap_halves_rest"))})
    lands_late = [lax.empty((3,) + parts[n].shape[1:], BF16) for n in late]
    rest_sems_s, rest_sems_r, rest_bufs, rest_token = _split_start(
        "scatter_rest_start", [parts[n] for n in late] + lands_late, scatter_plan, 3 * len(late))
    sc1_late = sc1 + rest_token[0:1, 0:1]
    dh1 = _mm("proj_in_bwd", dprojb, w_in_s, "nt", BF16, b_cs=True, tm=FF_TM, tn=FF_TN)

    def norm1_bwd(dh, xv, dxr, g, sc):
        dh = dh.astype(F32)
        r = _rstd(xv)
        xh = xv * r
        dn = dh * (1.0 + sc)
        return [dxr + _rms_bwd(dn * g, xh, r)], [_colsum8(dh * xh * g), _colsum8(dh), _colsum8(dn * xh)]

    grad_x, dsc1_acc, dsh1_acc, dn1g_acc = _rowwise("norm1_bwd", norm1_bwd, [dh1, x2, dx1], [norm1_g, sc1_late], [(d, F32)], [(SUBLANES, d)] * 3)

    def b_grad_fn(dbb, br, bi, f):
        fr, fi = f[0], f[1]
        dbr, dbi = _cmul_conj(fr, fi, dbb[0], dbb[1])
        dfr = jnp.sum(dbb[0] * br + dbb[1] * bi, axis=1, keepdims=True)
        dfi = jnp.sum(dbb[1] * br - dbb[0] * bi, axis=1, keepdims=True)
        return [dbr, dbi, dfr, dfi]

    dbr_bd, dbi_bd, df_r, df_i = _single("ssm_b_grad", b_grad_fn, [db_bar, braw_r, braw_i, f_rows],
                                         [(braw_r.shape, F32)] * 2 + [((nblk, 1, 512), F32)] * 2)

    def sums_fn(*accs):
        return [jnp.sum(a, axis=-2, keepdims=True) for a in accs]

    acc_list = [dsh1_acc, dsc1_acc, dg1_acc, dsh2_acc, dsc2_acc, dg2_acc, dn1g_acc, dgq_acc, dgk_acc, dds_acc, dbglu_acc, dga_acc,
                dgs_acc, dn2g_acc, loss_acc, da_acc]
    sums = _single("small_sums", sums_fn, acc_list, [(a.shape[:-2] + (1, a.shape[-1]), F32) for a in acc_list])
    (dsh1, dsc1, dg1, dsh2, dsc2, dg2, dn1g, dgq, dgk, dds, dbglu, dga, dgs, dn2g, loss_row, da_sum) = sums
    loss_dev = _single("loss_sum", lambda v: [jnp.sum(v, axis=1, keepdims=True)], [loss_row], [((1, 1), F32)])[0]
    gp = (n_groups, STATE_DIM)
    dlam_re, dlam_im, dlog_step = _single(
        "ssm_param_grad", _ssm_param_grad_fn,
        [lam_re2, lam_im2, log_step2, da_sum[0].reshape(gp), da_sum[1].reshape(gp), df_r.reshape(gp), df_i.reshape(gp)],
        [(gp, F32), (gp, F32), ((n_groups, 1), F32)])
    small_dev = {
        "b_ada": jnp.concatenate([dsh1, dsc1, dg1, dsh2, dsc2, dg2], axis=1),
        "norm1_g": dn1g, "q_norm_g": dgq, "k_norm_g": dgk, "lam_re": dlam_re, "lam_im": dlam_im, "log_step": dlog_step,
        "b_re": _block_diag_take(dbr_bd, SSM_GROUP, STATE_DIM).transpose(0, 2, 1),
        "b_im": _block_diag_take(dbi_bd, SSM_GROUP, STATE_DIM).transpose(0, 2, 1),
        "c_re": _block_diag_take(dc_bd[0], SSM_GROUP, STATE_DIM), "c_im": _block_diag_take(dc_bd[1], SSM_GROUP, STATE_DIM),
        "d_skip": dds, "b_glu": dbglu, "attn_out_g": dga, "ssm_out_g": dgs, "norm2_g": dn2g,
    }

    packed = _pack([small_dev[n] for n in SMALL])
    rows = packed.shape[0]
    tr = _blk(rows, 256)
    me_idx = jnp.reshape(me, (1,)).astype(jnp.int32)
    gathered = _indexed("pack_small", lambda a: a, me_idx, [(packed, (tr, PACK_LANES), lambda r, ix: (r, 0))],
                        (((8, rows, PACK_LANES), F32), (None, tr, PACK_LANES), lambda r, ix: (ix[0], r, 0)), (rows // tr,))

    def small_plan(refs):
        (g,) = refs
        px0, py0, pc0, _ = _place()
        flip = lambda v, bit: 1 - v if bit else v
        plan = []
        for mask in range(1, 8):
            px, py, pc = flip(px0, mask & 4), flip(py0, mask & 2), flip(pc0, mask & 1)
            mine = g.at[4 * px0 + 2 * py0 + pc0]
            plan.append((mine, mine, (px, py, pc), g.at[4 * px + 2 * py + pc]))
        return plan

    small_sems_s, small_sems_r, (gathered,), small_token = _split_start("gather_small_start", [gathered], small_plan, 7)

    sc_bufs = _split_wait("scatter_ff_wait", sc_bufs, sc_sems_s, sc_sems_r, small_token, scatter_plan)
    parts.update(zip(early, sc_bufs[: len(early)]))
    got2 = dict(zip(early, sc_bufs[len(early) :]))
    rest_bufs = _split_wait("scatter_rest_wait", rest_bufs, rest_sems_s, rest_sems_r, small_token, scatter_plan)
    parts.update(zip(late, rest_bufs[: len(late)]))
    got2.update(zip(late, rest_bufs[len(late) :]))
    place_idx = jnp.stack([chip, lax.axis_index("c")]).astype(jnp.int32)

    def add4(n, p, b):
        _, half, cols = p.shape
        tr = _blk(half, 256)
        blk = (None, tr, cols)
        fn = lambda own, q0, q1, q2: ((own.astype(F32) + q0.astype(F32)) + q1.astype(F32)) + q2.astype(F32)
        srcs = [(p, blk, lambda r, ix: (ix[0], r, 0))] + [(b, blk, lambda r, ix, j=j: (j, r, 0)) for j in range(3)]
        return _indexed("grad_add4_" + n, fn, place_idx, srcs, (((2, half, cols), F32), blk, lambda r, ix: (ix[1], r, 0)), (half // tr,))

    def join_plan(refs):
        px0, py0, pc, _ = _place()
        return [(r.at[pc], r.at[pc], (px0, py0, 1 - pc), r.at[1 - pc]) for r in refs]

    out = {}

    def update_large(names, fulls):
        last = None
        for n, full in zip(names, fulls):
            g = full.reshape(wts[n].shape[1:])
            last = _adamw_rows("adamw_" + n, wts[n][0], g, mom[n][0], var[n][0])
            for kind, val in zip(("grad", "delta", "new_m", "new_v"), (g,) + tuple(last)):
                out[kind, n] = val[None]
        return last[0]

    rest = [n for n in LARGE if n not in early]
    ja_s, ja_r, fulls_a, _ = _split_start("join_ff_start", [add4(n, parts[n], got2[n]) for n in early], join_plan, len(early))
    jb_s, jb_r, fulls_b, jb_token = _split_start("join_rest_start", [add4(n, parts[n], got2[n]) for n in rest], join_plan, len(rest))
    done_a = update_large(early, _split_wait("join_ff_wait", fulls_a, ja_s, ja_r, jb_token, join_plan))
    done_b = update_large(rest, _split_wait("join_rest_wait", fulls_b, jb_s, jb_r, done_a, join_plan))
    (gathered,) = _split_wait("gather_small_wait", [gathered], small_sems_s, small_sems_r, done_b, small_plan)

    def sum8_fn(g):
        acc = g[0]
        for i in range(1, 8):
            acc = acc + g[i]
        return [acc]

    g_small = _single("sum_small", sum8_fn, [gathered], [((rows, PACK_LANES), F32)])[0]

    w_small, m_small, v_small = [_pack([src[n] for n in SMALL]) for src in (wts, mom, var)]
    upd_small = _adamw_rows("adamw_small", w_small, g_small, m_small, v_small)
    shapes = [wts[n].shape for n in SMALL]
    for kind, arr in zip(("grad", "delta", "new_m", "new_v"), (g_small,) + tuple(upd_small)):
        for n, val in zip(SMALL, _unpack(arr, shapes)):
            out[kind, n] = val

    dmod_all = gathered[:, 0 : -(-6 * d // PACK_LANES), :].reshape(8, -1)[:, : 6 * d]
    dmod_mine = lax.dynamic_slice(dmod_all, (0, chip * n_mod), (8, n_mod))
    g_ada = _mm("ada_wgrad", sil_c, dmod_mine, "tn", F32, tm=512, tn=512)
    upd = _adamw_rows("adamw_w_ada", w_ada[0], g_ada, m_w_ada[0], v_w_ada[0])
    for kind, val in zip(("grad", "delta", "new_m", "new_v"), (g_ada,) + tuple(upd)):
        out[kind, "w_ada"] = val[None]

    loss = lax.psum(loss_dev[0, 0], ("x", "y", "c"))
    return (loss, grad_x[None], *[out[kind, n] for kind in ("grad", "delta", "new_m", "new_v") for n in ORDER])
```
